```python
import jax, jax.numpy as jnp
from jax import lax
import numpy as np

D_MODEL = 1024
BATCH = 8
SEQ = 8192
DEPTH = 2

CHUNK = 64
PLE_DIM = 256
EPS = 1e-6

D_SGU = 1024
SGU_BLOCK = 128
SGU_HEADS = 8
SGU_HEAD_DIM = D_SGU // SGU_HEADS

D_CONV = 1024
CONV_WIDTH = 31

D_POOL = 1024
POOL_WINDOWS = (2, 4, 8, 16)
POOL_GROUPS = len(POOL_WINDOWS)
POOL_GROUP_DIM = D_POOL // POOL_GROUPS

N_BRANCH = 3
D_IN = 2 * D_SGU + 2 * D_CONV + D_POOL + N_BRANCH * D_MODEL
SPLITS = (2 * D_SGU, 2 * D_SGU + 2 * D_CONV, 2 * D_SGU + 2 * D_CONV + D_POOL)

D_FF = -(-8 * D_MODEL // (3 * 256)) * 256

kernel_name = "hybrid_sgu_conformer_pool_block"


def rms_norm(x, g):
    xf = x.astype(jnp.float32)
    y = xf * lax.rsqrt(jnp.mean(xf * xf, axis=-1, keepdims=True) + EPS)
    return (y * g.astype(jnp.float32)).astype(x.dtype)


def layer_norm(x, g, b):
    xf = x.astype(jnp.float32)
    mu = jnp.mean(xf, axis=-1, keepdims=True)
    xc = xf - mu
    var = jnp.mean(xc * xc, axis=-1, keepdims=True)
    y = xc * lax.rsqrt(var + EPS) * g.astype(jnp.float32) + b.astype(jnp.float32)
    return y.astype(x.dtype)


def sgu_mask():
    c = jnp.arange(SGU_BLOCK) // CHUNK
    return c[None, :] <= c[:, None]


def spatial_gating(z, w_s, b_s, g_v, b_v):
    u, v = jnp.split(z, 2, axis=-1)
    v = layer_norm(v, g_v, b_v)
    bsz, s, _ = v.shape
    nb = s // SGU_BLOCK
    v = v.reshape(bsz, nb, SGU_BLOCK, SGU_HEADS, SGU_HEAD_DIM)
    w = jnp.where(sgu_mask()[None], w_s, jnp.zeros_like(w_s))
    mixed = jnp.einsum('hij,bnjhc->bnihc', w, v) + b_s.T[None, None, :, :, None]
    return u * mixed.reshape(bsz, s, D_SGU)


def conformer_conv(z, w_dw, b_dw, g_ln, b_ln):
    a, gate = jnp.split(z, 2, axis=-1)
    h = a * jax.nn.sigmoid(gate)
    h = lax.conv_general_dilated(
        h, w_dw, window_strides=(1,), padding=((CONV_WIDTH - 1, 0),),
        dimension_numbers=('NWC', 'WIO', 'NWC'),
        feature_group_count=D_CONV) + b_dw
    h = layer_norm(h, g_ln, b_ln)
    return jax.nn.silu(h)


def multiscale_pool(z, w_pool, s_pool):
    bsz, s, _ = z.shape
    zf = z.astype(jnp.float32)
    cs = jnp.cumsum(zf, axis=1)
    t = jnp.arange(1, s + 1, dtype=jnp.float32)
    outs = []
    for gi, w in enumerate(POOL_WINDOWS):
        sl = slice(gi * POOL_GROUP_DIM, (gi + 1) * POOL_GROUP_DIM)
        c = cs[..., sl]
        prev = jnp.pad(c[:, :s - w], ((0, 0), (w, 0), (0, 0)))
        cnt = jnp.minimum(t, float(w))[None, :, None]
        outs.append((c - prev) / cnt - zf[..., sl])
    pooled = jnp.stack(outs, axis=2).astype(z.dtype)
    mixed = jnp.einsum('bsgc,gcd->bsgd', pooled, w_pool)
    return mixed.reshape(bsz, s, D_POOL) * s_pool


def _fwd_setup_inputs(seed: int = 0) -> dict:
    key = jax.random.key(seed)
    ks = jax.random.split(key, 32)
    f32 = jnp.float32

    def nrm(k, shape, scale):
        return jax.random.normal(k, shape, f32) * scale

    def gain(k, shape):
        return 1.0 + 0.05 * jax.random.normal(k, shape, f32)

    L = DEPTH
    return {
        "x": nrm(ks[0], (BATCH, SEQ, D_MODEL), 1.0),
        "p": nrm(ks[1], (DEPTH, BATCH, SEQ, PLE_DIM), 1.0),
        "g_mix_pre": gain(ks[2], (L, D_MODEL)),
        "w_in": nrm(ks[3], (L, D_MODEL, D_IN), D_MODEL ** -0.5),
        "w_sgu_s": nrm(ks[4], (L, SGU_HEADS, SGU_BLOCK, SGU_BLOCK), SGU_BLOCK ** -0.5),
        "b_sgu_s": 1.0 + 0.1 * jax.random.normal(ks[5], (L, SGU_HEADS, SGU_BLOCK), f32),
        "g_sgu_v": gain(ks[6], (L, D_SGU)),
        "b_sgu_v": nrm(ks[7], (L, D_SGU), 0.02),
        "w_sgu_out": nrm(ks[8], (L, D_SGU, D_MODEL), D_SGU ** -0.5),
        "w_dw": nrm(ks[9], (L, CONV_WIDTH, 1, D_CONV), CONV_WIDTH ** -0.5),
        "b_dw": nrm(ks[10], (L, D_CONV), 0.02),
        "g_conv_ln": gain(ks[11], (L, D_CONV)),
        "b_conv_ln": nrm(ks[12], (L, D_CONV), 0.02),
        "w_conv_out": nrm(ks[13], (L, D_CONV, D_MODEL), D_CONV ** -0.5),
        "w_pool": nrm(ks[14], (L, POOL_GROUPS, POOL_GROUP_DIM, POOL_GROUP_DIM), POOL_GROUP_DIM ** -0.5),
        "s_pool": 1.0 + 0.1 * jax.random.normal(ks[15], (L, D_POOL), f32),
        "w_pool_out": nrm(ks[16], (L, D_POOL, D_MODEL), D_POOL ** -0.5),
        "w_out": nrm(ks[17], (L, D_MODEL, D_MODEL), D_MODEL ** -0.5),
        "g_mix_post": gain(ks[18], (L, D_MODEL)),
        "g_ffn_pre": gain(ks[19], (L, D_MODEL)),
        "w_ffn_in": nrm(ks[20], (L, D_MODEL, 2 * D_FF), D_MODEL ** -0.5),
        "w_ffn_out": nrm(ks[21], (L, D_FF, D_MODEL), D_FF ** -0.5),
        "g_ffn_post": gain(ks[22], (L, D_MODEL)),
        "w_ple": nrm(ks[23], (L, PLE_DIM, D_MODEL), PLE_DIM ** -0.5),
        "w_ple_gate": nrm(ks[24], (L, D_MODEL, D_MODEL), D_MODEL ** -0.5),
    }


def _fwd_reference(x, p, g_mix_pre, w_in, w_sgu_s, b_sgu_s, g_sgu_v, b_sgu_v, w_sgu_out,
              w_dw, b_dw, g_conv_ln, b_conv_ln, w_conv_out, w_pool, s_pool, w_pool_out,
              w_out, g_mix_post, g_ffn_pre, w_ffn_in, w_ffn_out, g_ffn_post,
              w_ple, w_ple_gate):
    h = x
    bsz, s, _ = x.shape
    for i in range(DEPTH):
        hn = rms_norm(h, g_mix_pre[i])
        proj = hn @ w_in[i]
        z_sgu, z_conv, z_pool, z_gate = jnp.split(proj, SPLITS, axis=-1)

        br_a = spatial_gating(jax.nn.gelu(z_sgu), w_sgu_s[i], b_sgu_s[i],
                              g_sgu_v[i], b_sgu_v[i]) @ w_sgu_out[i]
        br_b = conformer_conv(z_conv, w_dw[i], b_dw[i],
                              g_conv_ln[i], b_conv_ln[i]) @ w_conv_out[i]
        br_c = multiscale_pool(z_pool, w_pool[i], s_pool[i]) @ w_pool_out[i]

        gates = jax.nn.sigmoid(z_gate).reshape(bsz, s, N_BRANCH, D_MODEL)
        merged = gates[:, :, 0] * br_a + gates[:, :, 1] * br_b + gates[:, :, 2] * br_c
        h = h + rms_norm(merged @ w_out[i], g_mix_post[i])

        hn = rms_norm(h, g_ffn_pre[i])
        f_gate, f_up = jnp.split(hn @ w_ffn_in[i], 2, axis=-1)
        f = (jax.nn.silu(f_gate) * f_up) @ w_ffn_out[i]
        h = h + rms_norm(f, g_ffn_post[i])

        h = h + jax.nn.sigmoid(h @ w_ple_gate[i]) * (p[i] @ w_ple[i])
    return h


import jax as _jax
import jax.numpy as _jnp

TWIN_FORMAT = 'train_step'
FWD_PARAMS = ['x', 'p', 'g_mix_pre', 'w_in', 'w_sgu_s', 'b_sgu_s', 'g_sgu_v', 'b_sgu_v', 'w_sgu_out', 'w_dw', 'b_dw', 'g_conv_ln', 'b_conv_ln', 'w_conv_out', 'w_pool', 's_pool', 'w_pool_out', 'w_out', 'g_mix_post', 'g_ffn_pre', 'w_ffn_in', 'w_ffn_out', 'g_ffn_post', 'w_ple', 'w_ple_gate']
TWIN_WEIGHTS = ['g_mix_pre', 'w_in', 'w_sgu_s', 'b_sgu_s', 'g_sgu_v', 'b_sgu_v', 'w_sgu_out', 'w_dw', 'b_dw', 'g_conv_ln', 'b_conv_ln', 'w_conv_out', 'w_pool', 's_pool', 'w_pool_out', 'w_out', 'g_mix_post', 'g_ffn_pre', 'w_ffn_in', 'w_ffn_out', 'g_ffn_post', 'w_ple', 'w_ple_gate']
TWIN_DIFF_INPUT = 'x'
TWIN_INPUTS = ['x', 'p', 'g_mix_pre', 'w_in', 'w_sgu_s', 'b_sgu_s', 'g_sgu_v', 'b_sgu_v', 'w_sgu_out', 'w_dw', 'b_dw', 'g_conv_ln', 'b_conv_ln', 'w_conv_out', 'w_pool', 's_pool', 'w_pool_out', 'w_out', 'g_mix_post', 'g_ffn_pre', 'w_ffn_in', 'w_ffn_out', 'g_ffn_post', 'w_ple', 'w_ple_gate', 'loss_target', 'm_g_mix_pre', 'm_w_in', 'm_w_sgu_s', 'm_b_sgu_s', 'm_g_sgu_v', 'm_b_sgu_v', 'm_w_sgu_out', 'm_w_dw', 'm_b_dw', 'm_g_conv_ln', 'm_b_conv_ln', 'm_w_conv_out', 'm_w_pool', 'm_s_pool', 'm_w_pool_out', 'm_w_out', 'm_g_mix_post', 'm_g_ffn_pre', 'm_w_ffn_in', 'm_w_ffn_out', 'm_g_ffn_post', 'm_w_ple', 'm_w_ple_gate', 'v_g_mix_pre', 'v_w_in', 'v_w_sgu_s', 'v_b_sgu_s', 'v_g_sgu_v', 'v_b_sgu_v', 'v_w_sgu_out', 'v_w_dw', 'v_b_dw', 'v_g_conv_ln', 'v_b_conv_ln', 'v_w_conv_out', 'v_w_pool', 'v_s_pool', 'v_w_pool_out', 'v_w_out', 'v_g_mix_post', 'v_g_ffn_pre', 'v_w_ffn_in', 'v_w_ffn_out', 'v_g_ffn_post', 'v_w_ple', 'v_w_ple_gate']
TWIN_OUTPUTS = ['loss', 'grad_x', 'grad_g_mix_pre', 'grad_w_in', 'grad_w_sgu_s', 'grad_b_sgu_s', 'grad_g_sgu_v', 'grad_b_sgu_v', 'grad_w_sgu_out', 'grad_w_dw', 'grad_b_dw', 'grad_g_conv_ln', 'grad_b_conv_ln', 'grad_w_conv_out', 'grad_w_pool', 'grad_s_pool', 'grad_w_pool_out', 'grad_w_out', 'grad_g_mix_post', 'grad_g_ffn_pre', 'grad_w_ffn_in', 'grad_w_ffn_out', 'grad_g_ffn_post', 'grad_w_ple', 'grad_w_ple_gate', 'delta_g_mix_pre', 'delta_w_in', 'delta_w_sgu_s', 'delta_b_sgu_s', 'delta_g_sgu_v', 'delta_b_sgu_v', 'delta_w_sgu_out', 'delta_w_dw', 'delta_b_dw', 'delta_g_conv_ln', 'delta_b_conv_ln', 'delta_w_conv_out', 'delta_w_pool', 'delta_s_pool', 'delta_w_pool_out', 'delta_w_out', 'delta_g_mix_post', 'delta_g_ffn_pre', 'delta_w_ffn_in', 'delta_w_ffn_out', 'delta_g_ffn_post', 'delta_w_ple', 'delta_w_ple_gate', 'new_m_g_mix_pre', 'new_m_w_in', 'new_m_w_sgu_s', 'new_m_b_sgu_s', 'new_m_g_sgu_v', 'new_m_b_sgu_v', 'new_m_w_sgu_out', 'new_m_w_dw', 'new_m_b_dw', 'new_m_g_conv_ln', 'new_m_b_conv_ln', 'new_m_w_conv_out', 'new_m_w_pool', 'new_m_s_pool', 'new_m_w_pool_out', 'new_m_w_out', 'new_m_g_mix_post', 'new_m_g_ffn_pre', 'new_m_w_ffn_in', 'new_m_w_ffn_out', 'new_m_g_ffn_post', 'new_m_w_ple', 'new_m_w_ple_gate', 'new_v_g_mix_pre', 'new_v_w_in', 'new_v_w_sgu_s', 'new_v_b_sgu_s', 'new_v_g_sgu_v', 'new_v_b_sgu_v', 'new_v_w_sgu_out', 'new_v_w_dw', 'new_v_b_dw', 'new_v_g_conv_ln', 'new_v_b_conv_ln', 'new_v_w_conv_out', 'new_v_w_pool', 'new_v_s_pool', 'new_v_w_pool_out', 'new_v_w_out', 'new_v_g_mix_post', 'new_v_g_ffn_pre', 'new_v_w_ffn_in', 'new_v_w_ffn_out', 'new_v_g_ffn_post', 'new_v_w_ple', 'new_v_w_ple_gate']
TWIN_LEAF_KINDS = {'loss': 'loss', 'grad_x': 'grad_x', 'grad_g_mix_pre': 'grad_w', 'grad_w_in': 'grad_w', 'grad_w_sgu_s': 'grad_w', 'grad_b_sgu_s': 'grad_w', 'grad_g_sgu_v': 'grad_w', 'grad_b_sgu_v': 'grad_w', 'grad_w_sgu_out': 'grad_w', 'grad_w_dw': 'grad_w', 'grad_b_dw': 'grad_w', 'grad_g_conv_ln': 'grad_w', 'grad_b_conv_ln': 'grad_w', 'grad_w_conv_out': 'grad_w', 'grad_w_pool': 'grad_w', 'grad_s_pool': 'grad_w', 'grad_w_pool_out': 'grad_w', 'grad_w_out': 'grad_w', 'grad_g_mix_post': 'grad_w', 'grad_g_ffn_pre': 'grad_w', 'grad_w_ffn_in': 'grad_w', 'grad_w_ffn_out': 'grad_w', 'grad_g_ffn_post': 'grad_w', 'grad_w_ple': 'grad_w', 'grad_w_ple_gate': 'grad_w', 'delta_g_mix_pre': 'delta_w', 'delta_w_in': 'delta_w', 'delta_w_sgu_s': 'delta_w', 'delta_b_sgu_s': 'delta_w', 'delta_g_sgu_v': 'delta_w', 'delta_b_sgu_v': 'delta_w', 'delta_w_sgu_out': 'delta_w', 'delta_w_dw': 'delta_w', 'delta_b_dw': 'delta_w', 'delta_g_conv_ln': 'delta_w', 'delta_b_conv_ln': 'delta_w', 'delta_w_conv_out': 'delta_w', 'delta_w_pool': 'delta_w', 'delta_s_pool': 'delta_w', 'delta_w_pool_out': 'delta_w', 'delta_w_out': 'delta_w', 'delta_g_mix_post': 'delta_w', 'delta_g_ffn_pre': 'delta_w', 'delta_w_ffn_in': 'delta_w', 'delta_w_ffn_out': 'delta_w', 'delta_g_ffn_post': 'delta_w', 'delta_w_ple': 'delta_w', 'delta_w_ple_gate': 'delta_w', 'new_m_g_mix_pre': 'new_m', 'new_m_w_in': 'new_m', 'new_m_w_sgu_s': 'new_m', 'new_m_b_sgu_s': 'new_m', 'new_m_g_sgu_v': 'new_m', 'new_m_b_sgu_v': 'new_m', 'new_m_w_sgu_out': 'new_m', 'new_m_w_dw': 'new_m', 'new_m_b_dw': 'new_m', 'new_m_g_conv_ln': 'new_m', 'new_m_b_conv_ln': 'new_m', 'new_m_w_conv_out': 'new_m', 'new_m_w_pool': 'new_m', 'new_m_s_pool': 'new_m', 'new_m_w_pool_out': 'new_m', 'new_m_w_out': 'new_m', 'new_m_g_mix_post': 'new_m', 'new_m_g_ffn_pre': 'new_m', 'new_m_w_ffn_in': 'new_m', 'new_m_w_ffn_out': 'new_m', 'new_m_g_ffn_post': 'new_m', 'new_m_w_ple': 'new_m', 'new_m_w_ple_gate': 'new_m', 'new_v_g_mix_pre': 'new_v', 'new_v_w_in': 'new_v', 'new_v_w_sgu_s': 'new_v', 'new_v_b_sgu_s': 'new_v', 'new_v_g_sgu_v': 'new_v', 'new_v_b_sgu_v': 'new_v', 'new_v_w_sgu_out': 'new_v', 'new_v_w_dw': 'new_v', 'new_v_b_dw': 'new_v', 'new_v_g_conv_ln': 'new_v', 'new_v_b_conv_ln': 'new_v', 'new_v_w_conv_out': 'new_v', 'new_v_w_pool': 'new_v', 'new_v_s_pool': 'new_v', 'new_v_w_pool_out': 'new_v', 'new_v_w_out': 'new_v', 'new_v_g_mix_post': 'new_v', 'new_v_g_ffn_pre': 'new_v', 'new_v_w_ffn_in': 'new_v', 'new_v_w_ffn_out': 'new_v', 'new_v_g_ffn_post': 'new_v', 'new_v_w_ple': 'new_v', 'new_v_w_ple_gate': 'new_v'}


def _forward(args):
    return _fwd_reference(*[args[k] for k in FWD_PARAMS])


def _output_shape():
    def fwd():
        inp = _fwd_setup_inputs(0)
        return _fwd_reference(*[inp[k] for k in FWD_PARAMS])
    out = _jax.eval_shape(fwd)
    return out.shape, out.dtype

N_MICROBATCH = 1
ADAM_LR = 0.001
ADAM_B1 = 0.9
ADAM_B2 = 0.999
ADAM_EPS = 1e-08
ADAM_WD = 0.01
ADAM_STEP = 10
PER_EXAMPLE_BATCH_AXIS = {'x': 0, 'p': 1, 'loss_target': 0}
SHARED_INPUTS = []
_WEIGHT_DTYPES = {'g_mix_pre': _jnp.float32, 'w_in': _jnp.float32, 'w_sgu_s': _jnp.float32, 'b_sgu_s': _jnp.float32, 'g_sgu_v': _jnp.float32, 'b_sgu_v': _jnp.float32, 'w_sgu_out': _jnp.float32, 'w_dw': _jnp.float32, 'b_dw': _jnp.float32, 'g_conv_ln': _jnp.float32, 'b_conv_ln': _jnp.float32, 'w_conv_out': _jnp.float32, 'w_pool': _jnp.float32, 's_pool': _jnp.float32, 'w_pool_out': _jnp.float32, 'w_out': _jnp.float32, 'g_mix_post': _jnp.float32, 'g_ffn_pre': _jnp.float32, 'w_ffn_in': _jnp.float32, 'w_ffn_out': _jnp.float32, 'g_ffn_post': _jnp.float32, 'w_ple': _jnp.float32, 'w_ple_gate': _jnp.float32}
MOMENT_SCALE = {'g_mix_pre': 2.051804e+00, 'w_in': 6.864881e-01, 'w_sgu_s': 4.345772e-01, 'b_sgu_s': 5.763712e-01, 'g_sgu_v': 4.763255e-01, 'b_sgu_v': 4.432501e-01, 'w_sgu_out': 3.170456e+00, 'w_dw': 7.335910e-01, 'b_dw': 1.029238e+01, 'g_conv_ln': 3.923950e+00, 'b_conv_ln': 6.152151e+00, 'w_conv_out': 2.305171e+00, 'w_pool': 1.468418e+00, 's_pool': 1.509477e+00, 'w_pool_out': 1.499828e+00, 'w_out': 4.138171e+00, 'g_mix_post': 6.563359e+01, 'g_ffn_pre': 2.020740e+00, 'w_ffn_in': 8.552022e-01, 'w_ffn_out': 1.796326e+00, 'g_ffn_post': 6.511418e+01, 'w_ple': 1.047127e+00, 'w_ple_gate': 1.073669e+00}


def _to_microbatches(a, axis):
    t = _jnp.moveaxis(a, axis, 0)
    t = t.reshape((N_MICROBATCH, t.shape[0] // N_MICROBATCH) + t.shape[1:])
    return _jnp.moveaxis(t, 1, axis + 1)


def setup_inputs(seed: int = 0) -> dict:
    inp = _fwd_setup_inputs(seed)
    key = _jax.random.fold_in(_jax.random.key(seed), 7919)
    shape, _ = _output_shape()
    out = dict(inp)
    out["loss_target"] = _jax.random.normal(_jax.random.fold_in(key, 0), shape, _jnp.float32)
    for i, name in enumerate(TWIN_WEIGHTS):
        w = inp[name].astype(_jnp.float32)
        if MOMENT_SCALE is None:
            s = _jnp.sqrt(_jnp.mean(_jnp.square(w)) + 1e-30)
        else:
            s = MOMENT_SCALE[name]
        km, kv = _jax.random.split(_jax.random.fold_in(key, i + 1))
        out[name] = w
        out["m_" + name] = s * _jax.random.normal(km, w.shape, _jnp.float32)
        out["v_" + name] = (s * s) * _jax.random.uniform(kv, w.shape, _jnp.float32, 0.5, 1.5)
    if N_MICROBATCH > 1:
        for name, axis in PER_EXAMPLE_BATCH_AXIS.items():
            out[name] = _to_microbatches(out[name], axis)
    return {'x': out['x'], 'p': out['p'], 'g_mix_pre': out['g_mix_pre'], 'w_in': out['w_in'], 'w_sgu_s': out['w_sgu_s'], 'b_sgu_s': out['b_sgu_s'], 'g_sgu_v': out['g_sgu_v'], 'b_sgu_v': out['b_sgu_v'], 'w_sgu_out': out['w_sgu_out'], 'w_dw': out['w_dw'], 'b_dw': out['b_dw'], 'g_conv_ln': out['g_conv_ln'], 'b_conv_ln': out['b_conv_ln'], 'w_conv_out': out['w_conv_out'], 'w_pool': out['w_pool'], 's_pool': out['s_pool'], 'w_pool_out': out['w_pool_out'], 'w_out': out['w_out'], 'g_mix_post': out['g_mix_post'], 'g_ffn_pre': out['g_ffn_pre'], 'w_ffn_in': out['w_ffn_in'], 'w_ffn_out': out['w_ffn_out'], 'g_ffn_post': out['g_ffn_post'], 'w_ple': out['w_ple'], 'w_ple_gate': out['w_ple_gate'], 'loss_target': out['loss_target'], 'm_g_mix_pre': out['m_g_mix_pre'], 'm_w_in': out['m_w_in'], 'm_w_sgu_s': out['m_w_sgu_s'], 'm_b_sgu_s': out['m_b_sgu_s'], 'm_g_sgu_v': out['m_g_sgu_v'], 'm_b_sgu_v': out['m_b_sgu_v'], 'm_w_sgu_out': out['m_w_sgu_out'], 'm_w_dw': out['m_w_dw'], 'm_b_dw': out['m_b_dw'], 'm_g_conv_ln': out['m_g_conv_ln'], 'm_b_conv_ln': out['m_b_conv_ln'], 'm_w_conv_out': out['m_w_conv_out'], 'm_w_pool': out['m_w_pool'], 'm_s_pool': out['m_s_pool'], 'm_w_pool_out': out['m_w_pool_out'], 'm_w_out': out['m_w_out'], 'm_g_mix_post': out['m_g_mix_post'], 'm_g_ffn_pre': out['m_g_ffn_pre'], 'm_w_ffn_in': out['m_w_ffn_in'], 'm_w_ffn_out': out['m_w_ffn_out'], 'm_g_ffn_post': out['m_g_ffn_post'], 'm_w_ple': out['m_w_ple'], 'm_w_ple_gate': out['m_w_ple_gate'], 'v_g_mix_pre': out['v_g_mix_pre'], 'v_w_in': out['v_w_in'], 'v_w_sgu_s': out['v_w_sgu_s'], 'v_b_sgu_s': out['v_b_sgu_s'], 'v_g_sgu_v': out['v_g_sgu_v'], 'v_b_sgu_v': out['v_b_sgu_v'], 'v_w_sgu_out': out['v_w_sgu_out'], 'v_w_dw': out['v_w_dw'], 'v_b_dw': out['v_b_dw'], 'v_g_conv_ln': out['v_g_conv_ln'], 'v_b_conv_ln': out['v_b_conv_ln'], 'v_w_conv_out': out['v_w_conv_out'], 'v_w_pool': out['v_w_pool'], 'v_s_pool': out['v_s_pool'], 'v_w_pool_out': out['v_w_pool_out'], 'v_w_out': out['v_w_out'], 'v_g_mix_post': out['v_g_mix_post'], 'v_g_ffn_pre': out['v_g_ffn_pre'], 'v_w_ffn_in': out['v_w_ffn_in'], 'v_w_ffn_out': out['v_w_ffn_out'], 'v_g_ffn_post': out['v_g_ffn_post'], 'v_w_ple': out['v_w_ple'], 'v_w_ple_gate': out['v_w_ple_gate']}


def _loss(weights, diff, rest, loss_target):
    with _jax.named_scope("forward"):
        args = {**rest, TWIN_DIFF_INPUT: diff, **{k: w.astype(_WEIGHT_DTYPES[k]) for k, w in weights.items()}}
        y = _forward(args)
    with _jax.named_scope("loss_head"):
        err = _jnp.square(y.astype(_jnp.float32) - loss_target)
        return 0.5 * _jnp.sum(_jnp.mean(err, axis=-1)) if err.ndim else 0.5 * err


def _adamw(w, g, m, v):
    m = ADAM_B1 * m + (1.0 - ADAM_B1) * g
    v = ADAM_B2 * v + (1.0 - ADAM_B2) * _jnp.square(g)
    m_hat = m / (1.0 - ADAM_B1 ** ADAM_STEP)
    v_hat = v / (1.0 - ADAM_B2 ** ADAM_STEP)
    delta = -ADAM_LR * (m_hat / (_jnp.sqrt(v_hat) + ADAM_EPS) + ADAM_WD * w)
    return delta, m, v


def reference(x, p, g_mix_pre, w_in, w_sgu_s, b_sgu_s, g_sgu_v, b_sgu_v, w_sgu_out, w_dw, b_dw, g_conv_ln, b_conv_ln, w_conv_out, w_pool, s_pool, w_pool_out, w_out, g_mix_post, g_ffn_pre, w_ffn_in, w_ffn_out, g_ffn_post, w_ple, w_ple_gate, loss_target, m_g_mix_pre, m_w_in, m_w_sgu_s, m_b_sgu_s, m_g_sgu_v, m_b_sgu_v, m_w_sgu_out, m_w_dw, m_b_dw, m_g_conv_ln, m_b_conv_ln, m_w_conv_out, m_w_pool, m_s_pool, m_w_pool_out, m_w_out, m_g_mix_post, m_g_ffn_pre, m_w_ffn_in, m_w_ffn_out, m_g_ffn_post, m_w_ple, m_w_ple_gate, v_g_mix_pre, v_w_in, v_w_sgu_s, v_b_sgu_s, v_g_sgu_v, v_b_sgu_v, v_w_sgu_out, v_w_dw, v_b_dw, v_g_conv_ln, v_b_conv_ln, v_w_conv_out, v_w_pool, v_s_pool, v_w_pool_out, v_w_out, v_g_mix_post, v_g_ffn_pre, v_w_ffn_in, v_w_ffn_out, v_g_ffn_post, v_w_ple, v_w_ple_gate):
    given = dict(x=x, p=p, g_mix_pre=g_mix_pre, w_in=w_in, w_sgu_s=w_sgu_s, b_sgu_s=b_sgu_s, g_sgu_v=g_sgu_v, b_sgu_v=b_sgu_v, w_sgu_out=w_sgu_out, w_dw=w_dw, b_dw=b_dw, g_conv_ln=g_conv_ln, b_conv_ln=b_conv_ln, w_conv_out=w_conv_out, w_pool=w_pool, s_pool=s_pool, w_pool_out=w_pool_out, w_out=w_out, g_mix_post=g_mix_post, g_ffn_pre=g_ffn_pre, w_ffn_in=w_ffn_in, w_ffn_out=w_ffn_out, g_ffn_post=g_ffn_post, w_ple=w_ple, w_ple_gate=w_ple_gate, loss_target=loss_target, m_g_mix_pre=m_g_mix_pre, m_w_in=m_w_in, m_w_sgu_s=m_w_sgu_s, m_b_sgu_s=m_b_sgu_s, m_g_sgu_v=m_g_sgu_v, m_b_sgu_v=m_b_sgu_v, m_w_sgu_out=m_w_sgu_out, m_w_dw=m_w_dw, m_b_dw=m_b_dw, m_g_conv_ln=m_g_conv_ln, m_b_conv_ln=m_b_conv_ln, m_w_conv_out=m_w_conv_out, m_w_pool=m_w_pool, m_s_pool=m_s_pool, m_w_pool_out=m_w_pool_out, m_w_out=m_w_out, m_g_mix_post=m_g_mix_post, m_g_ffn_pre=m_g_ffn_pre, m_w_ffn_in=m_w_ffn_in, m_w_ffn_out=m_w_ffn_out, m_g_ffn_post=m_g_ffn_post, m_w_ple=m_w_ple, m_w_ple_gate=m_w_ple_gate, v_g_mix_pre=v_g_mix_pre, v_w_in=v_w_in, v_w_sgu_s=v_w_sgu_s, v_b_sgu_s=v_b_sgu_s, v_g_sgu_v=v_g_sgu_v, v_b_sgu_v=v_b_sgu_v, v_w_sgu_out=v_w_sgu_out, v_w_dw=v_w_dw, v_b_dw=v_b_dw, v_g_conv_ln=v_g_conv_ln, v_b_conv_ln=v_b_conv_ln, v_w_conv_out=v_w_conv_out, v_w_pool=v_w_pool, v_s_pool=v_s_pool, v_w_pool_out=v_w_pool_out, v_w_out=v_w_out, v_g_mix_post=v_g_mix_post, v_g_ffn_pre=v_g_ffn_pre, v_w_ffn_in=v_w_ffn_in, v_w_ffn_out=v_w_ffn_out, v_g_ffn_post=v_g_ffn_post, v_w_ple=v_w_ple, v_w_ple_gate=v_w_ple_gate)
    weights = {n: given[n] for n in TWIN_WEIGHTS}
    shared = {n: given[n] for n in SHARED_INPUTS}
    per_example = {n: given[n] for n in ['x', 'p']}
    grad_fn = _jax.value_and_grad(_loss, argnums=(0, 1))

    def one_microbatch(ex, loss_target):
        ex = dict(ex)
        diff = ex.pop(TWIN_DIFF_INPUT)
        return grad_fn(weights, diff, {**shared, **ex}, loss_target)

    if N_MICROBATCH == 1:
        loss, (grad_w, grad_x) = one_microbatch(per_example, given["loss_target"])
    else:
        def body(carry, xs):
            loss_sum, grad_sum = carry
            l_k, (gw_k, gx_k) = one_microbatch(xs[0], xs[1])
            with _jax.named_scope("update"):
                return (loss_sum + l_k, _jax.tree.map(_jnp.add, grad_sum, gw_k)), gx_k

        init = (_jnp.zeros((), _jnp.float32), _jax.tree.map(_jnp.zeros_like, weights))
        (loss, grad_w), grad_x = _jax.lax.scan(body, init, (per_example, given["loss_target"]))
    with _jax.named_scope("update"):
        delta_w, new_m, new_v = {}, {}, {}
        for n in TWIN_WEIGHTS:
            delta_w[n], new_m[n], new_v[n] = _adamw(weights[n], grad_w[n], given["m_" + n], given["v_" + n])
    return (loss, grad_x, *[grad_w[n] for n in TWIN_WEIGHTS], *[delta_w[n] for n in TWIN_WEIGHTS],
            *[new_m[n] for n in TWIN_WEIGHTS], *[new_v[n] for n in TWIN_WEIGHTS])
```

```python
import functools
import math

import jax
import jax.numpy as jnp
from jax import lax
from jax.experimental import pallas as pl
from jax.experimental.pallas import tpu as pltpu

F32 = jnp.float32
BF = jnp.bfloat16

D = 1024
D_FF = 2816
PLE = 256
N_DEV = 8
HEADS = 8
BLK = 128
CHUNK = 64
CONV_W = 31
POOL_WINDOWS = (2, 4, 8, 16)
POOL_GD = 256
EPS = 1e-6

V7X_VMEM_BYTES = 64 * 2**20
VMEM_LIMIT = V7X_VMEM_BYTES * 7 // 8
HALO = 32
RC = 64
LC = 128
TM = 512
TMB = 256
TS = 512

ADAM_LR, ADAM_B1, ADAM_B2, ADAM_EPS, ADAM_WD, ADAM_STEP = 0.001, 0.9, 0.999, 1e-08, 0.01, 10

MESH = pl.DeviceIdType.MESH
ANY = pl.BlockSpec(memory_space=pl.ANY)

_GELU_K0 = math.sqrt(2.0 / math.pi)
_GELU_K1 = 0.044715


def _dot(a, b):
    return jnp.dot(a, b, preferred_element_type=F32)


def _dot_nt(a, b):
    return lax.dot_general(a, b, (((1,), (1,)), ((), ())), preferred_element_type=F32)


def _dot_tn(a, b):
    return lax.dot_general(a, b, (((0,), (0,)), ((), ())), preferred_element_type=F32)


def _sig(x):
    return 1.0 / (1.0 + jnp.exp(-x))


def _gelu(x):
    t = jnp.tanh(_GELU_K0 * (x + _GELU_K1 * x * x * x))
    return 0.5 * x * (1.0 + t), t


def _gelu_grad(x, t):
    return 0.5 * (1.0 + t) + 0.5 * x * (1.0 - t * t) * _GELU_K0 * (1.0 + 3.0 * _GELU_K1 * x * x)


def _rstd(x):
    return lax.rsqrt(jnp.mean(x * x, axis=-1, keepdims=True) + EPS)


def _rms_bwd(x, gd, r):
    return r * gd - x * (r * r * r) * jnp.mean(gd * x, axis=-1, keepdims=True)


def _ln_fwd(x):
    mu = jnp.mean(x, axis=-1, keepdims=True)
    xc = x - mu
    rs = lax.rsqrt(jnp.mean(xc * xc, axis=-1, keepdims=True) + EPS)
    return xc * rs, rs


def _ln_bwd(dhat, hat, rs):
    return rs * (dhat - jnp.mean(dhat, axis=-1, keepdims=True) - hat * jnp.mean(dhat * hat, axis=-1, keepdims=True))


def _colsum(x):
    return jnp.sum(x, axis=0, keepdims=True)


def _accum(ref, first, val):
    @pl.when(first)
    def _():
        ref[...] = val

    @pl.when(jnp.logical_not(first))
    def _():
        ref[...] += val


def _sgu_mask(transposed):
    r = lax.broadcasted_iota(jnp.int32, (BLK, BLK), 0) // CHUNK
    c = lax.broadcasted_iota(jnp.int32, (BLK, BLK), 1) // CHUNK
    return (r <= c) if transposed else (c <= r)


def _inv_count(i, tm, w):
    t = lax.broadcasted_iota(jnp.int32, (tm, 1), 0) + i * tm
    return 1.0 / jnp.minimum(t + 1, w).astype(F32)


def _count(i, tm, w):
    t = lax.broadcasted_iota(jnp.int32, (tm, 1), 0) + i * tm
    return jnp.minimum(t + 1, w).astype(F32)


def _params(n_grid):
    return pltpu.CompilerParams(dimension_semantics=("arbitrary",) * n_grid, vmem_limit_bytes=VMEM_LIMIT)


def _sds(shape, dtype):
    return jax.ShapeDtypeStruct(shape, dtype)


def _cols(tm, width, cb):
    return pl.BlockSpec((tm, width), lambda i: (i, cb))


def _whole(shape):
    nd = len(shape)
    return pl.BlockSpec(shape, lambda i: (0,) * nd)


def _prev_halo(tm, width, cb):
    return pl.BlockSpec((HALO, width), lambda i: (jnp.maximum(i * (tm // HALO) - 1, 0), cb))


def _next_halo(tm, width, cb, s_len):
    last = s_len // HALO - 1
    return pl.BlockSpec((HALO, width), lambda i: (jnp.minimum((i + 1) * (tm // HALO), last), cb))


def _rowsharded(layer, rows):
    return pl.BlockSpec((N_DEV, None, rows, D), lambda i: (0, layer, 0, 0))


def _win_block(layer, j):
    return pl.BlockSpec((None, None, D, D), lambda i: (j, layer, 0, 0))


def _row_tile(rows, cap):
    t = min(rows, cap)
    while rows % t or t % 16:
        t -= 16
    return t


def all_gather(arrs, name):
    n = len(arrs)

    def body(*refs):
        ins, outs = refs[:n], refs[n:2 * n]
        send, recv, local = refs[2 * n:]
        x, y, c = lax.axis_index("x"), lax.axis_index("y"), lax.axis_index("c")
        me, sibling = (x, y, c), (x, y, 1 - c)
        chips = [(1 - x, y), (x, 1 - y), (1 - x, 1 - y)]

        def slot(px, py, pc):
            return 4 * px + 2 * py + pc

        def copy(a, k, block, to, src=None):
            dst = outs[a].at[slot(*block)]
            return pltpu.make_async_remote_copy(
                src_ref=dst if src is None else src, dst_ref=dst, send_sem=send.at[a, k], recv_sem=recv.at[a, k],
                device_id=to, device_id_type=MESH)

        mine = [pltpu.make_async_copy(ins[a], outs[a].at[slot(*me)], local.at[a]) for a in range(n)]
        for cp in mine:
            cp.start()
        first = []
        for a in range(n):
            first.append(copy(a, 0, me, sibling, src=ins[a]))
            first += [copy(a, 1 + j, me, (*chip, c), src=ins[a]) for j, chip in enumerate(chips)]
        for cp in first:
            cp.start()
        passed = []
        for a in range(n):
            for j, chip in enumerate(chips):
                copy(a, 1 + j, (*chip, c), me).wait_recv()
                fwd = copy(a, 4 + j, (*chip, c), sibling)
                fwd.start()
                passed.append(fwd)
        for a in range(n):
            copy(a, 0, sibling, me).wait_recv()
            for j, chip in enumerate(chips):
                copy(a, 4 + j, (*chip, 1 - c), me).wait_recv()
        for cp in first + passed:
            cp.wait_send()
        for cp in mine:
            cp.wait()

    return pl.pallas_call(
        body, name=name,
        out_shape=[_sds((N_DEV,) + a.shape, a.dtype) for a in arrs],
        in_specs=[ANY] * n, out_specs=[ANY] * n,
        scratch_shapes=[pltpu.SemaphoreType.DMA((n, 7)), pltpu.SemaphoreType.DMA((n, 7)), pltpu.SemaphoreType.DMA((n,))],
    )(*arrs)


def rs_to_sibling(parts, name):
    n = len(parts)

    def body(*refs):
        ins, outs = refs[:n], refs[n:2 * n]
        send, recv = refs[2 * n:]
        x, y, c = lax.axis_index("x"), lax.axis_index("y"), lax.axis_index("c")
        sibling = (x, y, 1 - c)
        for a in range(n):
            for q in range(4):
                pltpu.make_async_remote_copy(
                    src_ref=ins[a].at[2 * q + 1 - c], dst_ref=outs[a].at[q], send_sem=send.at[a], recv_sem=recv.at[a],
                    device_id=sibling, device_id_type=MESH).start()
        for a in range(n):
            pltpu.make_async_remote_copy(
                src_ref=outs[a], dst_ref=outs[a], send_sem=send.at[a], recv_sem=recv.at[a],
                device_id=sibling, device_id_type=MESH).wait()

    return pl.pallas_call(
        body, name=name,
        out_shape=[_sds((4,) + p.shape[1:], p.dtype) for p in parts],
        in_specs=[ANY] * n, out_specs=[ANY] * n,
        scratch_shapes=[pltpu.SemaphoreType.DMA((n,)), pltpu.SemaphoreType.DMA((n,))],
    )(*parts)


def rs_to_chips(cps, name):
    n = len(cps)

    def body(*refs):
        ins, outs = refs[:n], refs[n:2 * n]
        send, recv = refs[2 * n:]
        x, y, c = lax.axis_index("x"), lax.axis_index("y"), lax.axis_index("c")
        chips = [(1 - x, y), (x, 1 - y), (1 - x, 1 - y)]
        for a in range(n):
            for r, (px, py) in enumerate(chips):
                pltpu.make_async_remote_copy(
                    src_ref=ins[a].at[2 * px + py], dst_ref=outs[a].at[r], send_sem=send.at[a], recv_sem=recv.at[a],
                    device_id=(px, py, c), device_id_type=MESH).start()
        for a in range(n):
            pltpu.make_async_remote_copy(
                src_ref=outs[a], dst_ref=outs[a], send_sem=send.at[a], recv_sem=recv.at[a],
                device_id=(x, y, c), device_id_type=MESH).wait()

    return pl.pallas_call(
        body, name=name,
        out_shape=[_sds((3,) + p.shape[1:], p.dtype) for p in cps],
        in_specs=[ANY] * n, out_specs=[ANY] * n,
        scratch_shapes=[pltpu.SemaphoreType.DMA((n,)), pltpu.SemaphoreType.DMA((n,))],
    )(*cps)


def add_bf16(a, b, name):
    rows, cols = a.shape
    tr = _row_tile(rows, 512)

    def body(a_ref, b_ref, o_ref):
        o_ref[...] = (a_ref[...].astype(F32) + b_ref[...].astype(F32)).astype(o_ref.dtype)

    spec = pl.BlockSpec((tr, cols), lambda i: (i, 0))
    return pl.pallas_call(body, name=name, grid=(rows // tr,), in_specs=[spec, spec], out_specs=spec,
                          out_shape=_sds(a.shape, BF), compiler_params=_params(1))(a, b)


def adamw(w, m, v, pieces, name):
    rows, cols = w.shape
    tr = _row_tile(rows, 256) if rows % 16 == 0 else rows
    np_ = len(pieces)
    c1 = 1.0 / (1.0 - ADAM_B1 ** ADAM_STEP)
    c2 = 1.0 / (1.0 - ADAM_B2 ** ADAM_STEP)

    def body(*refs):
        w_ref, m_ref, v_ref = refs[:3]
        p_refs = refs[3:3 + np_]
        g_ref, d_ref, nm_ref, nv_ref = refs[3 + np_:]
        g = p_refs[0][...].astype(F32)
        for pr in p_refs[1:]:
            g = g + pr[...].astype(F32)
        nm = ADAM_B1 * m_ref[...] + (1.0 - ADAM_B1) * g
        nv = ADAM_B2 * v_ref[...] + (1.0 - ADAM_B2) * (g * g)
        g_ref[...] = g
        nm_ref[...] = nm
        nv_ref[...] = nv
        d_ref[...] = -ADAM_LR * ((nm * c1) / (jnp.sqrt(nv * c2) + ADAM_EPS) + ADAM_WD * w_ref[...])

    spec = pl.BlockSpec((tr, cols), lambda i: (i, 0))
    p_specs = []
    for arr, k in pieces:
        if k is None:
            p_specs.append(spec)
        else:
            p_specs.append(pl.BlockSpec((None, tr, cols), functools.partial(lambda i, kk: (kk, i, 0), kk=k)))
    out = _sds(w.shape, F32)
    return pl.pallas_call(body, name=name, grid=(rows // tr,), in_specs=[spec] * 3 + p_specs, out_specs=[spec] * 4,
                          out_shape=[out] * 4, compiler_params=_params(1))(w, m, v, *[a for a, _ in pieces])


def sum_slabs(g, name):
    n, rows, cols = g.shape

    def body(g_ref, o_ref):
        s = g_ref[0]
        for k in range(1, n):
            s = s + g_ref[k]
        o_ref[...] = s

    return pl.pallas_call(body, name=name, out_shape=_sds((rows, cols), F32))(g)


def norm_proj(h, g, w, wspec_fn, nb, tn, name):
    s_len = h.shape[0]
    tm = min(TM, s_len)

    def body(h_ref, g_ref, w_ref, o_ref, hn_ref, hn_s):
        @pl.when(pl.program_id(1) == 0)
        def _():
            x = h_ref[...]
            hn = (x * _rstd(x) * g_ref[...]).astype(BF)
            hn_s[...] = hn
            hn_ref[...] = hn
        o_ref[...] = _dot(hn_s[...], w_ref[...]).astype(BF)

    return pl.pallas_call(
        body, name=name, grid=(s_len // tm, nb),
        in_specs=[pl.BlockSpec((tm, D), lambda i, j: (i, 0)), pl.BlockSpec((1, D), lambda i, j: (0, 0)), wspec_fn()],
        out_specs=[pl.BlockSpec((tm, tn), lambda i, j: (i, j)), pl.BlockSpec((tm, D), lambda i, j: (i, 0))],
        out_shape=[_sds((s_len, nb * tn), BF), _sds((s_len, D), BF)],
        scratch_shapes=[pltpu.VMEM((tm, D), BF)], compiler_params=_params(2))(h, g, w)


def _sgu_mix(ws_ref, vln_s, mix_s, bs_ref, tm, transposed):
    mask = _sgu_mask(transposed)
    for hd in range(HEADS):
        wm = jnp.where(mask, ws_ref[hd], 0.0).astype(BF)
        cs = slice(hd * BLK, (hd + 1) * BLK)
        for n in range(tm // BLK):
            rs = slice(n * BLK, (n + 1) * BLK)
            r = _dot(wm, vln_s[rs, cs])
            mix_s[rs, cs] = r if bs_ref is None else r + bs_ref[:, cs]


def sgu_fwd(proj, ws, bsfull, gv, bv, wo, layer, name):
    s_len = proj.shape[0]
    tm = min(TM, s_len)

    def body(zu_ref, zv_ref, ws_ref, bs_ref, gv_ref, bv_ref, wo_ref, sgu_ref, br_ref, vln_s, mix_s):
        u, _ = _gelu(zu_ref[...].astype(F32))
        v, _ = _gelu(zv_ref[...].astype(F32))
        vhat, _ = _ln_fwd(v)
        vln_s[...] = (vhat * gv_ref[...] + bv_ref[...]).astype(BF)
        _sgu_mix(ws_ref, vln_s, mix_s, bs_ref, tm, False)
        sgu = (u * mix_s[...]).astype(BF)
        sgu_ref[...] = sgu
        br_ref[...] = _dot(sgu, wo_ref[...].reshape(D, D)).astype(BF)

    return pl.pallas_call(
        body, name=name, grid=(s_len // tm,),
        in_specs=[_cols(tm, D, 0), _cols(tm, D, 1), _whole((HEADS, BLK, BLK)), _whole((BLK, D)), _whole((1, D)), _whole((1, D)),
                  _rowsharded(layer, BLK)],
        out_specs=[_cols(tm, D, 0)] * 2, out_shape=[_sds((s_len, D), BF)] * 2,
        scratch_shapes=[pltpu.VMEM((tm, D), BF), pltpu.VMEM((tm, D), F32)], compiler_params=_params(1),
    )(proj, proj, ws, bsfull, gv, bv, wo)


def _causal_conv(ext_s, out_s, wdw_ref, bias_ref, tm):
    def chunk(ci, carry):
        r0 = pl.multiple_of((ci // (D // LC)) * RC, RC)
        l0 = pl.multiple_of((ci % (D // LC)) * LC, LC)
        win = ext_s[pl.ds(r0, RC + HALO), pl.ds(l0, LC)]
        acc = jnp.broadcast_to(bias_ref[:, pl.ds(l0, LC)], (RC, LC))
        for r in range(8):
            wr = win if r == 0 else pltpu.roll(win, r, 0)
            for m in range(4):
                d = 8 * m + r
                if d < CONV_W:
                    k = CONV_W - 1 - d
                    acc = acc + wdw_ref[k:k + 1, pl.ds(l0, LC)] * wr[HALO - 8 * m:HALO - 8 * m + RC]
        out_s[pl.ds(r0, RC), pl.ds(l0, LC)] = acc
        return carry
    lax.fori_loop(0, (tm // RC) * (D // LC), chunk, 0)


def _glu_ext(a_ref, g_ref, ah_ref, gh_ref, ext_s, first):
    hh = ah_ref[...].astype(F32) * _sig(gh_ref[...].astype(F32))
    ext_s[0:HALO, :] = jnp.where(first, 0.0, hh)
    ext_s[HALO:, :] = a_ref[...].astype(F32) * _sig(g_ref[...].astype(F32))


def conv_fwd(proj, wdw, bdw, gln, bln, wo, layer, name):
    s_len = proj.shape[0]
    tm = min(TM, s_len)

    def body(a_ref, g_ref, ah_ref, gh_ref, wdw_ref, bdw_ref, gln_ref, bln_ref, wo_ref, cb_ref, br_ref, ext_s, conv_s):
        _glu_ext(a_ref, g_ref, ah_ref, gh_ref, ext_s, pl.program_id(0) == 0)
        _causal_conv(ext_s, conv_s, wdw_ref, bdw_ref, tm)
        chat, _ = _ln_fwd(conv_s[...])
        yl = chat * gln_ref[...] + bln_ref[...]
        cb = (yl * _sig(yl)).astype(BF)
        cb_ref[...] = cb
        br_ref[...] = _dot(cb, wo_ref[...].reshape(D, D)).astype(BF)

    return pl.pallas_call(
        body, name=name, grid=(s_len // tm,),
        in_specs=[_cols(tm, D, 2), _cols(tm, D, 3), _prev_halo(tm, D, 2), _prev_halo(tm, D, 3), _whole((HALO, D)),
                  _whole((1, D)), _whole((1, D)), _whole((1, D)), _rowsharded(layer, BLK)],
        out_specs=[_cols(tm, D, 0)] * 2, out_shape=[_sds((s_len, D), BF)] * 2,
        scratch_shapes=[pltpu.VMEM((tm + HALO, D), F32), pltpu.VMEM((tm, D), F32)], compiler_params=_params(1),
    )(proj, proj, proj, proj, wdw, bdw, gln, bln, wo)


def pool_fwd(proj, wpool, spool, wo, layer, name):
    s_len = proj.shape[0]
    tm = min(TM, s_len)

    def body(z_ref, zh_ref, wp_ref, sp_ref, wo_ref, pooled_ref, pm_ref, br_ref, ext_s, mr_s):
        i = pl.program_id(0)
        ext_s[0:HALO, :] = jnp.where(i == 0, 0.0, zh_ref[...].astype(F32))
        ext_s[HALO:, :] = z_ref[...].astype(F32)
        for gi, w in enumerate(POOL_WINDOWS):
            cs = slice(gi * POOL_GD, (gi + 1) * POOL_GD)
            e = ext_s[:, cs]
            s = e
            sh = 1
            while sh < w:
                s = s + pltpu.roll(s, sh, 0)
                sh *= 2
            pooled = (s[HALO:] * _inv_count(i, tm, w) - e[HALO:]).astype(BF)
            pooled_ref[:, cs] = pooled
            mr_s[:, cs] = _dot(pooled, wp_ref[gi])
        pm = (mr_s[...] * sp_ref[...]).astype(BF)
        pm_ref[...] = pm
        br_ref[...] = _dot(pm, wo_ref[...].reshape(D, D)).astype(BF)

    return pl.pallas_call(
        body, name=name, grid=(s_len // tm,),
        in_specs=[_cols(tm, D, 4), _prev_halo(tm, D, 4), _whole((4, POOL_GD, POOL_GD)), _whole((1, D)), _rowsharded(layer, BLK)],
        out_specs=[_cols(tm, D, 0)] * 3, out_shape=[_sds((s_len, D), BF)] * 3,
        scratch_shapes=[pltpu.VMEM((tm + HALO, D), F32), pltpu.VMEM((tm, D), F32)], compiler_params=_params(1),
    )(proj, proj, wpool, spool, wo)


def merge_out(proj, bra, brb, brc, h, wout, gpost, layer, name):
    s_len = h.shape[0]
    tm = min(TM, s_len)

    def body(z0, z1, z2, a_ref, b_ref, c_ref, h_ref, wo_ref, g_ref, mg_ref, mo_ref, h1_ref):
        merged = (_sig(z0[...].astype(F32)) * a_ref[...].astype(F32) + _sig(z1[...].astype(F32)) * b_ref[...].astype(F32)
                  + _sig(z2[...].astype(F32)) * c_ref[...].astype(F32)).astype(BF)
        mg_ref[...] = merged
        mo = _dot(merged, wo_ref[...].reshape(D, D))
        mo_ref[...] = mo.astype(BF)
        h1_ref[...] = h_ref[...] + mo * _rstd(mo) * g_ref[...]

    row = _cols(tm, D, 0)
    return pl.pallas_call(
        body, name=name, grid=(s_len // tm,),
        in_specs=[_cols(tm, D, 5), _cols(tm, D, 6), _cols(tm, D, 7), row, row, row, row, _rowsharded(layer, BLK), _whole((1, D))],
        out_specs=[row] * 3, out_shape=[_sds((s_len, D), BF), _sds((s_len, D), BF), _sds((s_len, D), F32)],
        compiler_params=_params(1))(proj, proj, proj, bra, brb, brc, h, wout, gpost)


def _p_spec(tm, layer):
    return pl.BlockSpec((None, None, tm, PLE), lambda i: (layer, 0, i, 0))


def ffn_out(ff, h1, p, wfo, gpost, wpg, wple, layer, name):
    s_len = h1.shape[0]
    tm = min(TMB, s_len)

    def body(fg_ref, fu_ref, h1_ref, p_ref, wfo_ref, g_ref, wpg_ref, wple_ref, act_ref, f_ref, h2_ref, pg_ref, h3_ref):
        gt = fg_ref[...].astype(F32)
        act = (gt * _sig(gt) * fu_ref[...].astype(F32)).astype(BF)
        act_ref[...] = act
        f = _dot(act, wfo_ref[...].reshape(D_FF, D))
        f_ref[...] = f.astype(BF)
        h2 = h1_ref[...] + f * _rstd(f) * g_ref[...]
        h2_ref[...] = h2
        pg = _dot(h2.astype(BF), wpg_ref[...].reshape(D, D)).astype(BF)
        pg_ref[...] = pg
        pe = _dot(p_ref[...].astype(BF), wple_ref[...])
        h3_ref[...] = h2 + _sig(pg.astype(F32)) * pe

    row = _cols(tm, D, 0)
    return pl.pallas_call(
        body, name=name, grid=(s_len // tm,),
        in_specs=[_cols(tm, D_FF, 0), _cols(tm, D_FF, 1), row, _p_spec(tm, layer), _rowsharded(layer, D_FF // N_DEV),
                  _whole((1, D)), _rowsharded(layer, BLK), _whole((PLE, D))],
        out_specs=[_cols(tm, D_FF, 0), row, row, row, row],
        out_shape=[_sds((s_len, D_FF), BF), _sds((s_len, D), BF), _sds((s_len, D), F32), _sds((s_len, D), BF), _sds((s_len, D), F32)],
        compiler_params=_params(1))(ff, ff, h1, p, wfo, gpost, wpg, wple)


def loss_grad(y, target, name):
    s_len = y.shape[0]
    tm = min(TM, s_len)
    nt = s_len // tm

    def body(y_ref, t_ref, dy_ref, loss_ref, acc):
        i = pl.program_id(0)
        e = y_ref[...] - t_ref[...]
        dy_ref[...] = e * (1.0 / D)
        _accum(acc, i == 0, _colsum(e * e))

        @pl.when(i == nt - 1)
        def _():
            loss_ref[...] = jnp.broadcast_to(jnp.sum(acc[...], axis=1, keepdims=True) * (0.5 / D), (1, LC))

    row = _cols(tm, D, 0)
    return pl.pallas_call(
        body, name=name, grid=(nt,), in_specs=[row, row], out_specs=[row, _whole((1, LC))],
        out_shape=[_sds((s_len, D), F32), _sds((1, LC), F32)], scratch_shapes=[pltpu.VMEM((1, D), F32)],
        compiler_params=_params(1))(y, target)


def ple_ffn_bwd(dh3, pg, p, f, ff, wple, wpg, wfo, gpost, layer, name):
    s_len = dh3.shape[0]
    tm = min(TMB, s_len)

    def body(dh3_ref, pg_ref, p_ref, f_ref, fg_ref, fu_ref, wple_ref, wpg_ref, wfo_ref, g_ref,
             dh2_ref, dpe_ref, dpg_ref, df_ref, dff_ref, dg_ref):
        i = pl.program_id(0)
        dh3v = dh3_ref[...]
        s = _sig(pg_ref[...].astype(F32))
        pe = _dot(p_ref[...].astype(BF), wple_ref[...])
        dpe_ref[...] = (dh3v * s).astype(BF)
        dpg = (dh3v * pe * s * (1.0 - s)).astype(BF)
        dpg_ref[...] = dpg
        dh2 = dh3v + _dot_nt(dpg, wpg_ref[...].reshape(D, D))
        dh2_ref[...] = dh2
        fv = f_ref[...].astype(F32)
        r = _rstd(fv)
        _accum(dg_ref, i == 0, _colsum(dh2 * fv * r))
        df = _rms_bwd(fv, dh2 * g_ref[...], r).astype(BF)
        df_ref[...] = df
        dact = _dot_nt(df, wfo_ref[...].reshape(D_FF, D))
        gt = fg_ref[...].astype(F32)
        sg = _sig(gt)
        up = fu_ref[...].astype(F32)
        dff_ref[:, 0:D_FF] = (dact * up * sg * (1.0 + gt * (1.0 - sg))).astype(BF)
        dff_ref[:, D_FF:2 * D_FF] = (dact * gt * sg).astype(BF)

    row = _cols(tm, D, 0)
    return pl.pallas_call(
        body, name=name, grid=(s_len // tm,),
        in_specs=[row, row, _p_spec(tm, layer), row, _cols(tm, D_FF, 0), _cols(tm, D_FF, 1), _whole((PLE, D)),
                  _rowsharded(layer, BLK), _rowsharded(layer, D_FF // N_DEV), _whole((1, D))],
        out_specs=[row, row, row, row, _cols(tm, 2 * D_FF, 0), _whole((1, D))],
        out_shape=[_sds((s_len, D), F32), _sds((s_len, D), BF), _sds((s_len, D), BF), _sds((s_len, D), BF),
                   _sds((s_len, 2 * D_FF), BF), _sds((1, D), F32)],
        compiler_params=_params(1))(dh3, pg, p, f, ff, ff, wple, wpg, wfo, gpost)


def ffn_in_bwd(dff, w3, h1, dh2, gpre, name):
    s_len = h1.shape[0]
    tm = min(TMB, s_len)
    nb, _, tn = w3.shape

    def body(dff_ref, w_ref, h1_ref, dh2_ref, g_ref, dh1_ref, dg_ref, acc):
        i, j = pl.program_id(0), pl.program_id(1)
        _accum(acc, j == 0, _dot_nt(dff_ref[...], w_ref[0]))

        @pl.when(j == nb - 1)
        def _():
            x = h1_ref[...]
            r = _rstd(x)
            dhn = acc[...]
            _accum(dg_ref, i == 0, _colsum(dhn * x * r))
            dh1_ref[...] = dh2_ref[...] + _rms_bwd(x, dhn * g_ref[...], r)

    row = pl.BlockSpec((tm, D), lambda i, j: (i, 0))
    return pl.pallas_call(
        body, name=name, grid=(s_len // tm, nb),
        in_specs=[pl.BlockSpec((tm, tn), lambda i, j: (i, j)), pl.BlockSpec((1, D, tn), lambda i, j: (j, 0, 0)), row, row,
                  pl.BlockSpec((1, D), lambda i, j: (0, 0))],
        out_specs=[row, pl.BlockSpec((1, D), lambda i, j: (0, 0))],
        out_shape=[_sds((s_len, D), F32), _sds((1, D), F32)],
        scratch_shapes=[pltpu.VMEM((tm, D), F32)], compiler_params=_params(2))(dff, w3, h1, dh2, gpre)


def mix_post_bwd(dh1, mo, gpost, wout, proj, bra, brb, brc, win, layer, name):
    s_len = dh1.shape[0]
    tm = min(TMB, s_len)

    def body(dh1_ref, mo_ref, g_ref, wo_ref, z0, z1, z2, a_ref, b_ref, c_ref, w5, w6, w7,
             dmo_ref, da_ref, db_ref, dc_ref, dz_ref, dhn_ref, dg_ref):
        i = pl.program_id(0)
        dh1v = dh1_ref[...]
        mo_v = mo_ref[...].astype(F32)
        r = _rstd(mo_v)
        _accum(dg_ref, i == 0, _colsum(dh1v * mo_v * r))
        dmo = _rms_bwd(mo_v, dh1v * g_ref[...], r).astype(BF)
        dmo_ref[...] = dmo
        dmerged = _dot_nt(dmo, wo_ref[...].reshape(D, D))
        dhn = jnp.zeros((tm, D), F32)
        for k, (z, br, dbr, w) in enumerate(((z0, a_ref, da_ref, w5), (z1, b_ref, db_ref, w6), (z2, c_ref, dc_ref, w7))):
            s = _sig(z[...].astype(F32))
            dbr[...] = (dmerged * s).astype(BF)
            dz = (dmerged * br[...].astype(F32) * s * (1.0 - s)).astype(BF)
            dz_ref[:, k * D:(k + 1) * D] = dz
            dhn = dhn + _dot_nt(dz, w[...])
        dhn_ref[...] = dhn

    row = _cols(tm, D, 0)
    return pl.pallas_call(
        body, name=name, grid=(s_len // tm,),
        in_specs=[row, row, _whole((1, D)), _rowsharded(layer, BLK), _cols(tm, D, 5), _cols(tm, D, 6), _cols(tm, D, 7), row, row, row,
                  _win_block(layer, 5), _win_block(layer, 6), _win_block(layer, 7)],
        out_specs=[row, row, row, row, _cols(tm, 3 * D, 0), row, _whole((1, D))],
        out_shape=[_sds((s_len, D), BF)] * 4 + [_sds((s_len, 3 * D), BF), _sds((s_len, D), F32), _sds((1, D), F32)],
        compiler_params=_params(1))(dh1, mo, gpost, wout, proj, proj, proj, bra, brb, brc, win, win, win)


def sgu_bwd(dbr, wo, proj, ws, wst, bsfull, gv, bv, win, layer, name):
    s_len = dbr.shape[0]
    tm = min(TMB, s_len)
    nt = s_len // tm

    def body(dbr_ref, wo_ref, zu_ref, zv_ref, ws_ref, wst_ref, bs_ref, gv_ref, bv_ref, w0, w1,
             dz_ref, dhn_ref, dws_ref, dbs_ref, dgv_ref, dbv_ref, vln_s, mix_s, dmix_s, dvln_s, bs_acc):
        i = pl.program_id(0)
        first = i == 0
        dsgu = _dot_nt(dbr_ref[...], wo_ref[...].reshape(D, D))
        zu = zu_ref[...].astype(F32)
        zv = zv_ref[...].astype(F32)
        u, tu = _gelu(zu)
        v, tv = _gelu(zv)
        vhat, rs = _ln_fwd(v)
        vln_s[...] = (vhat * gv_ref[...] + bv_ref[...]).astype(BF)
        _sgu_mix(ws_ref, vln_s, mix_s, bs_ref, tm, False)
        du = dsgu * mix_s[...]
        dmix = dsgu * u
        dmix_s[...] = dmix.astype(BF)
        blocks = dmix[0:BLK]
        for n in range(1, tm // BLK):
            blocks = blocks + dmix[n * BLK:(n + 1) * BLK]
        _accum(bs_acc, first, blocks)
        for hd in range(HEADS):
            cs = slice(hd * BLK, (hd + 1) * BLK)
            g = _dot_nt(dmix_s[0:BLK, cs], vln_s[0:BLK, cs])
            for n in range(1, tm // BLK):
                g = g + _dot_nt(dmix_s[n * BLK:(n + 1) * BLK, cs], vln_s[n * BLK:(n + 1) * BLK, cs])

            @pl.when(first)
            def _():
                dws_ref[hd] = g

            @pl.when(jnp.logical_not(first))
            def _():
                dws_ref[hd] += g
        _sgu_mix(wst_ref, dmix_s, dvln_s, None, tm, True)
        dvln = dvln_s[...]
        _accum(dgv_ref, first, _colsum(dvln * vhat))
        _accum(dbv_ref, first, _colsum(dvln))
        dv = _ln_bwd(dvln * gv_ref[...], vhat, rs)
        dzu = (du * _gelu_grad(zu, tu)).astype(BF)
        dzv = (dv * _gelu_grad(zv, tv)).astype(BF)
        dz_ref[:, 0:D] = dzu
        dz_ref[:, D:2 * D] = dzv
        dhn_ref[...] = _dot_nt(dzu, w0[...]) + _dot_nt(dzv, w1[...])

        @pl.when(i == nt - 1)
        def _():
            mask = _sgu_mask(False)
            for hd in range(HEADS):
                dws_ref[hd] = jnp.where(mask, dws_ref[hd], 0.0)
                dbs_ref[:, hd:hd + 1] = jnp.sum(bs_acc[:, hd * BLK:(hd + 1) * BLK], axis=1, keepdims=True)

    row = _cols(tm, D, 0)
    vec = _whole((1, D))
    return pl.pallas_call(
        body, name=name, grid=(nt,),
        in_specs=[row, _rowsharded(layer, BLK), _cols(tm, D, 0), _cols(tm, D, 1), _whole((HEADS, BLK, BLK)), _whole((HEADS, BLK, BLK)),
                  _whole((BLK, D)), vec, vec, _win_block(layer, 0), _win_block(layer, 1)],
        out_specs=[_cols(tm, 2 * D, 0), row, _whole((HEADS, BLK, BLK)), _whole((BLK, HEADS)), vec, vec],
        out_shape=[_sds((s_len, 2 * D), BF), _sds((s_len, D), F32), _sds((HEADS, BLK, BLK), F32), _sds((BLK, HEADS), F32),
                   _sds((1, D), F32), _sds((1, D), F32)],
        scratch_shapes=[pltpu.VMEM((tm, D), BF), pltpu.VMEM((tm, D), F32), pltpu.VMEM((tm, D), BF), pltpu.VMEM((tm, D), F32),
                        pltpu.VMEM((BLK, D), F32)],
        compiler_params=_params(1))(dbr, wo, proj, proj, ws, wst, bsfull, gv, bv, win, win)


def conv_bwd(dbr, wo, proj, wdw, bdw, gln, bln, layer, name):
    s_len = dbr.shape[0]
    tm = min(TMB, s_len)
    nt = s_len // tm

    def body(dbr_ref, wo_ref, a_ref, g_ref, ah_ref, gh_ref, wdw_ref, bdw_ref, gln_ref, bln_ref,
             dc_ref, dw_ref, dbdw_ref, dgln_ref, dbln_ref, ext_s, conv_s, dc_s, dw_acc):
        i = pl.program_id(0)
        first = i == 0
        dcb = _dot_nt(dbr_ref[...], wo_ref[...].reshape(D, D))
        _glu_ext(a_ref, g_ref, ah_ref, gh_ref, ext_s, first)
        _causal_conv(ext_s, conv_s, wdw_ref, bdw_ref, tm)
        chat, rs = _ln_fwd(conv_s[...])
        yl = chat * gln_ref[...] + bln_ref[...]
        sy = _sig(yl)
        dyl = dcb * sy * (1.0 + yl * (1.0 - sy))
        _accum(dgln_ref, first, _colsum(dyl * chat))
        _accum(dbln_ref, first, _colsum(dyl))
        dc = _ln_bwd(dyl * gln_ref[...], chat, rs)
        _accum(dbdw_ref, first, _colsum(dc))
        dc_ref[...] = dc.astype(BF)
        dc_s[...] = dc

        @pl.when(first)
        def _():
            dw_acc[...] = jnp.zeros_like(dw_acc)

        def chunk(ci, carry):
            r0 = pl.multiple_of((ci // (D // LC)) * RC, RC)
            l0 = pl.multiple_of((ci % (D // LC)) * LC, LC)
            win = ext_s[pl.ds(r0, RC + HALO), pl.ds(l0, LC)]
            dcw = dc_s[pl.ds(r0, RC), pl.ds(l0, LC)]
            for r in range(8):
                wr = win if r == 0 else pltpu.roll(win, r, 0)
                for m in range(4):
                    d = 8 * m + r
                    if d < CONV_W:
                        k = CONV_W - 1 - d
                        prod = dcw * wr[HALO - 8 * m:HALO - 8 * m + RC]
                        dw_acc[k * 8:(k + 1) * 8, pl.ds(l0, LC)] += prod.reshape(RC // 8, 8, LC).sum(axis=0)
            return carry
        lax.fori_loop(0, (tm // RC) * (D // LC), chunk, 0)

        @pl.when(i == nt - 1)
        def _():
            dw_ref[...] = dw_acc[...].reshape(HALO, 8, D).sum(axis=1)

    row = _cols(tm, D, 0)
    vec = _whole((1, D))
    return pl.pallas_call(
        body, name=name, grid=(nt,),
        in_specs=[row, _rowsharded(layer, BLK), _cols(tm, D, 2), _cols(tm, D, 3), _prev_halo(tm, D, 2), _prev_halo(tm, D, 3),
                  _whole((HALO, D)), vec, vec, vec],
        out_specs=[row, _whole((HALO, D)), vec, vec, vec],
        out_shape=[_sds((s_len, D), BF), _sds((HALO, D), F32), _sds((1, D), F32), _sds((1, D), F32), _sds((1, D), F32)],
        scratch_shapes=[pltpu.VMEM((tm + HALO, D), F32), pltpu.VMEM((tm, D), F32), pltpu.VMEM((tm, D), F32),
                        pltpu.VMEM((HALO * 8, D), F32)],
        compiler_params=_params(1))(dbr, wo, proj, proj, proj, proj, wdw, bdw, gln, bln)


def pool_bwd(dbr, wo, pooled, wpool, spool, layer, name):
    s_len = dbr.shape[0]
    tm = min(TMB, s_len)

    def body(dbr_ref, wo_ref, pl_ref, wp_ref, sp_ref, dmr_ref, q_ref, dsp_ref, mr_s):
        i = pl.program_id(0)
        dpm = _dot_nt(dbr_ref[...], wo_ref[...].reshape(D, D))
        for gi in range(4):
            cs = slice(gi * POOL_GD, (gi + 1) * POOL_GD)
            mr_s[:, cs] = _dot(pl_ref[:, cs], wp_ref[gi])
        _accum(dsp_ref, i == 0, _colsum(dpm * mr_s[...]))
        dmr = (dpm * sp_ref[...]).astype(BF)
        dmr_ref[...] = dmr
        for gi, w in enumerate(POOL_WINDOWS):
            cs = slice(gi * POOL_GD, (gi + 1) * POOL_GD)
            q_ref[:, cs] = (_dot_nt(dmr[:, cs], wp_ref[gi]) * _inv_count(i, tm, w)).astype(BF)

    row = _cols(tm, D, 0)
    return pl.pallas_call(
        body, name=name, grid=(s_len // tm,),
        in_specs=[row, _rowsharded(layer, BLK), row, _whole((4, POOL_GD, POOL_GD)), _whole((1, D))],
        out_specs=[row, row, _whole((1, D))],
        out_shape=[_sds((s_len, D), BF), _sds((s_len, D), BF), _sds((1, D), F32)],
        scratch_shapes=[pltpu.VMEM((tm, D), F32)], compiler_params=_params(1))(dbr, wo, pooled, wpool, spool)


def seq_bwd(dc, q, proj, wdw, win, layer, name):
    s_len = dc.shape[0]
    tm = min(TMB, s_len)
    nt = s_len // tm

    def body(dc_ref, dch_ref, q_ref, qh_ref, a_ref, g_ref, wdw_ref, w2, w3, w4, dz_ref, dhn_ref, ext_s, dhc_s, qext_s):
        i = pl.program_id(0)
        last = i == nt - 1
        ext_s[0:tm, :] = dc_ref[...].astype(F32)
        ext_s[tm:, :] = jnp.where(last, 0.0, dch_ref[...].astype(F32))

        def chunk(ci, carry):
            r0 = pl.multiple_of((ci // (D // LC)) * RC, RC)
            l0 = pl.multiple_of((ci % (D // LC)) * LC, LC)
            win_ = ext_s[pl.ds(r0, RC + HALO), pl.ds(l0, LC)]
            acc = jnp.zeros((RC, LC), F32)
            for r in range(8):
                wr = win_ if r == 0 else pltpu.roll(win_, RC + HALO - r, 0)
                for m in range(4):
                    d = 8 * m + r
                    if d < CONV_W:
                        k = CONV_W - 1 - d
                        acc = acc + wdw_ref[k:k + 1, pl.ds(l0, LC)] * wr[8 * m:8 * m + RC]
            dhc_s[pl.ds(r0, RC), pl.ds(l0, LC)] = acc
            return carry
        lax.fori_loop(0, (tm // RC) * (D // LC), chunk, 0)

        dhc = dhc_s[...]
        av = a_ref[...].astype(F32)
        sg = _sig(g_ref[...].astype(F32))
        da = (dhc * sg).astype(BF)
        dg = (dhc * av * sg * (1.0 - sg)).astype(BF)
        dz_ref[:, 0:D] = da
        dz_ref[:, D:2 * D] = dg

        qext_s[0:tm, :] = q_ref[...].astype(F32)
        qext_s[tm:, :] = jnp.where(last, 0.0, qh_ref[...].astype(F32))
        for gi, w in enumerate(POOL_WINDOWS):
            cs = slice(gi * POOL_GD, (gi + 1) * POOL_GD)
            e = qext_s[:, cs]
            s = e
            sh = 1
            while sh < w:
                s = s + pltpu.roll(s, tm + HALO - sh, 0)
                sh *= 2
            dz_ref[:, 2 * D + gi * POOL_GD:2 * D + (gi + 1) * POOL_GD] = (s[0:tm] - e[0:tm] * _count(i, tm, w)).astype(BF)
        dhn_ref[...] = _dot_nt(da, w2[...]) + _dot_nt(dg, w3[...]) + _dot_nt(dz_ref[:, 2 * D:3 * D], w4[...])

    row = _cols(tm, D, 0)
    return pl.pallas_call(
        body, name=name, grid=(nt,),
        in_specs=[row, _next_halo(tm, D, 0, s_len), row, _next_halo(tm, D, 0, s_len), _cols(tm, D, 2), _cols(tm, D, 3),
                  _whole((HALO, D)), _win_block(layer, 2), _win_block(layer, 3), _win_block(layer, 4)],
        out_specs=[_cols(tm, 3 * D, 0), row],
        out_shape=[_sds((s_len, 3 * D), BF), _sds((s_len, D), F32)],
        scratch_shapes=[pltpu.VMEM((tm + HALO, D), F32), pltpu.VMEM((tm, D), F32), pltpu.VMEM((tm + HALO, D), F32)],
        compiler_params=_params(1))(dc, dc, q, q, proj, proj, wdw, win, win, win)


def mix_pre_bwd(h, dh1, dhn_a, dhn_c, dhn_g, gpre, name):
    s_len = h.shape[0]
    tm = min(TM, s_len)

    def body(h_ref, dh1_ref, a_ref, c_ref, g3_ref, g_ref, dh_ref, dg_ref):
        x = h_ref[...]
        r = _rstd(x)
        dhn = a_ref[...] + c_ref[...] + g3_ref[...]
        _accum(dg_ref, pl.program_id(0) == 0, _colsum(dhn * x * r))
        dh_ref[...] = dh1_ref[...] + _rms_bwd(x, dhn * g_ref[...], r)

    row = _cols(tm, D, 0)
    return pl.pallas_call(
        body, name=name, grid=(s_len // tm,), in_specs=[row] * 5 + [_whole((1, D))], out_specs=[row, _whole((1, D))],
        out_shape=[_sds((s_len, D), F32), _sds((1, D), F32)], compiler_params=_params(1))(h, dh1, dhn_a, dhn_c, dhn_g, gpre)


def wgrad(a, b, tk, tn, name, stacked=False, diag=False, a_spec=None):
    s_len = b.shape[0]
    k_dim = a.shape[-1]
    n_dim = b.shape[1]
    ts = min(TS, s_len)
    nk = 1 if diag else k_dim // tk
    nn, ns = n_dim // tn, s_len // ts

    def body(a_ref, b_ref, o_ref, acc):
        s = pl.program_id(2)
        _accum(acc, s == 0, _dot_tn(a_ref[...].astype(BF), b_ref[...].astype(BF)))

        @pl.when(s == ns - 1)
        def _():
            o_ref[...] = acc[...].astype(BF).reshape(o_ref.shape)

    if a_spec is None:
        a_spec = pl.BlockSpec((ts, tk), (lambda k, n, s: (s, n)) if diag else (lambda k, n, s: (s, k)))
    if stacked or diag:
        out_shape = _sds((nn, tk if diag else k_dim, tn), BF)
        o_spec = pl.BlockSpec((1, tk, tn), lambda k, n, s: (n, k, 0))
    else:
        out_shape = _sds((k_dim, n_dim), BF)
        o_spec = pl.BlockSpec((tk, tn), lambda k, n, s: (k, n))
    return pl.pallas_call(
        body, name=name, grid=(nk, nn, ns),
        in_specs=[a_spec, pl.BlockSpec((ts, tn), lambda k, n, s: (s, n))], out_specs=o_spec, out_shape=out_shape,
        scratch_shapes=[pltpu.VMEM((tk, tn), F32)], compiler_params=_params(3))(a, b)


_WEIGHTS = ['g_mix_pre', 'w_in', 'w_sgu_s', 'b_sgu_s', 'g_sgu_v', 'b_sgu_v', 'w_sgu_out', 'w_dw', 'b_dw', 'g_conv_ln', 'b_conv_ln',
            'w_conv_out', 'w_pool', 's_pool', 'w_pool_out', 'w_out', 'g_mix_post', 'g_ffn_pre', 'w_ffn_in', 'w_ffn_out', 'g_ffn_post',
            'w_ple', 'w_ple_gate']
_SHARDED = ['w_in', 'w_sgu_out', 'w_conv_out', 'w_pool', 'w_pool_out', 'w_out', 'w_ffn_in', 'w_ffn_out', 'w_ple', 'w_ple_gate']
_VECTORS = ['g_mix_pre', 'g_sgu_v', 'b_sgu_v', 'b_dw', 'g_conv_ln', 'b_conv_ln', 's_pool', 'g_mix_post', 'g_ffn_pre', 'g_ffn_post']
_REP_ROWS = 2 * len(_VECTORS) + 2 + 2 * HEADS * BLK * BLK // D
_REP_PAD = -(-_REP_ROWS // 8) * 8
_DW_ROWS = 2 * HALO


def _pack_replicated(t):
    rows = [t[k].reshape(2, D) for k in _VECTORS] + [t['b_sgu_s'].reshape(2, D), t['w_sgu_s'].reshape(-1, D)]
    rows.append(jnp.zeros((_REP_PAD - _REP_ROWS, D), F32))
    return jnp.concatenate(rows, axis=0)


def _unpack_replicated(packed):
    out = {}
    for i, k in enumerate(_VECTORS):
        out[k] = packed[2 * i:2 * i + 2]
    o = 2 * len(_VECTORS)
    out['b_sgu_s'] = packed[o:o + 2].reshape(2, HEADS, BLK)
    out['w_sgu_s'] = packed[o + 2:_REP_ROWS].reshape(2, HEADS, BLK, BLK)
    return out


def _pad_taps(w):
    return jnp.concatenate([w, jnp.zeros((HALO - CONV_W, D), F32)], axis=0)


def kernel(x, p, g_mix_pre, w_in, w_sgu_s, b_sgu_s, g_sgu_v, b_sgu_v, w_sgu_out, w_dw, b_dw, g_conv_ln, b_conv_ln, w_conv_out, w_pool, s_pool, w_pool_out, w_out, g_mix_post, g_ffn_pre, w_ffn_in, w_ffn_out, g_ffn_post, w_ple, w_ple_gate, loss_target, m_g_mix_pre, m_w_in, m_w_sgu_s, m_b_sgu_s, m_g_sgu_v, m_b_sgu_v, m_w_sgu_out, m_w_dw, m_b_dw, m_g_conv_ln, m_b_conv_ln, m_w_conv_out, m_w_pool, m_s_pool, m_w_pool_out, m_w_out, m_g_mix_post, m_g_ffn_pre, m_w_ffn_in, m_w_ffn_out, m_g_ffn_post, m_w_ple, m_w_ple_gate, v_g_mix_pre, v_w_in, v_w_sgu_s, v_b_sgu_s, v_g_sgu_v, v_b_sgu_v, v_w_sgu_out, v_w_dw, v_b_dw, v_g_conv_ln, v_b_conv_ln, v_w_conv_out, v_w_pool, v_s_pool, v_w_pool_out, v_w_out, v_g_mix_post, v_g_ffn_pre, v_w_ffn_in, v_w_ffn_out, v_g_ffn_post, v_w_ple, v_w_ple_gate):
    W = dict(g_mix_pre=g_mix_pre, w_in=w_in, w_sgu_s=w_sgu_s, b_sgu_s=b_sgu_s, g_sgu_v=g_sgu_v, b_sgu_v=b_sgu_v, w_sgu_out=w_sgu_out,
             w_dw=w_dw, b_dw=b_dw, g_conv_ln=g_conv_ln, b_conv_ln=b_conv_ln, w_conv_out=w_conv_out, w_pool=w_pool, s_pool=s_pool,
             w_pool_out=w_pool_out, w_out=w_out, g_mix_post=g_mix_post, g_ffn_pre=g_ffn_pre, w_ffn_in=w_ffn_in, w_ffn_out=w_ffn_out,
             g_ffn_post=g_ffn_post, w_ple=w_ple, w_ple_gate=w_ple_gate)
    M = dict(g_mix_pre=m_g_mix_pre, w_in=m_w_in, w_sgu_s=m_w_sgu_s, b_sgu_s=m_b_sgu_s, g_sgu_v=m_g_sgu_v, b_sgu_v=m_b_sgu_v,
             w_sgu_out=m_w_sgu_out, w_dw=m_w_dw, b_dw=m_b_dw, g_conv_ln=m_g_conv_ln, b_conv_ln=m_b_conv_ln, w_conv_out=m_w_conv_out,
             w_pool=m_w_pool, s_pool=m_s_pool, w_pool_out=m_w_pool_out, w_out=m_w_out, g_mix_post=m_g_mix_post, g_ffn_pre=m_g_ffn_pre,
             w_ffn_in=m_w_ffn_in, w_ffn_out=m_w_ffn_out, g_ffn_post=m_g_ffn_post, w_ple=m_w_ple, w_ple_gate=m_w_ple_gate)
    V = dict(g_mix_pre=v_g_mix_pre, w_in=v_w_in, w_sgu_s=v_w_sgu_s, b_sgu_s=v_b_sgu_s, g_sgu_v=v_g_sgu_v, b_sgu_v=v_b_sgu_v,
             w_sgu_out=v_w_sgu_out, w_dw=v_w_dw, b_dw=v_b_dw, g_conv_ln=v_g_conv_ln, b_conv_ln=v_b_conv_ln, w_conv_out=v_w_conv_out,
             w_pool=v_w_pool, s_pool=v_s_pool, w_pool_out=v_w_pool_out, w_out=v_w_out, g_mix_post=v_g_mix_post, g_ffn_pre=v_g_ffn_pre,
             w_ffn_in=v_w_ffn_in, w_ffn_out=v_w_ffn_out, g_ffn_post=v_g_ffn_post, w_ple=v_w_ple, w_ple_gate=v_w_ple_gate)

    my_c = lax.axis_index("c")
    my_chip = 2 * lax.axis_index("x") + lax.axis_index("y")
    my_dev = 2 * my_chip + my_c
    s_len = x.shape[1]
    h0 = x.reshape(s_len, D)
    target = loss_target.reshape(s_len, D)

    gathered = all_gather([W[k].astype(BF) for k in _SHARDED] + [w_dw], "all_gather_weights")
    G = dict(zip(_SHARDED + ['w_dw'], gathered))
    wfi = G['w_ffn_in'].reshape(4, 2, 2, D, D_FF // 4)
    wfi = jnp.transpose(wfi, (2, 0, 3, 1, 4)).reshape(2, 4, D, D_FF // 2)
    wpool_g = jnp.transpose(G['w_pool'], (1, 2, 0, 3, 4)).reshape(2, 4, POOL_GD, POOL_GD)
    wple_g = jnp.transpose(G['w_ple'], (1, 2, 0, 3)).reshape(2, PLE, D)
    wdw_g = jnp.transpose(G['w_dw'].reshape(N_DEV, 2, CONV_W, BLK), (1, 2, 0, 3)).reshape(2, CONV_W, D)

    def vec(name, layer):
        return W[name][layer].reshape(1, D)

    saved = []
    h = h0
    for l in range(2):
        sv = {'h': h}
        wdw_l = _pad_taps(wdw_g[l])
        bsfull = jnp.repeat(b_sgu_s[l].T, BLK, axis=1)
        wst = jnp.swapaxes(w_sgu_s[l], 1, 2)
        sv.update(wdw=wdw_l, bsfull=bsfull, wst=wst)
        proj, hn = norm_proj(h, vec('g_mix_pre', l), G['w_in'],
                             functools.partial(lambda ll: pl.BlockSpec((None, None, D, D), lambda i, j: (j, ll, 0, 0)), l),
                             N_DEV, D, f"norm_proj_in_{l}")
        sgu, bra = sgu_fwd(proj, w_sgu_s[l], bsfull, vec('g_sgu_v', l), vec('b_sgu_v', l), G['w_sgu_out'], l, f"sgu_fwd_{l}")
        cb, brb = conv_fwd(proj, wdw_l, vec('b_dw', l), vec('g_conv_ln', l), vec('b_conv_ln', l), G['w_conv_out'], l, f"conv_fwd_{l}")
        pooled, pm, brc = pool_fwd(proj, wpool_g[l], vec('s_pool', l), G['w_pool_out'], l, f"pool_fwd_{l}")
        merged, mo, h1 = merge_out(proj, bra, brb, brc, h, G['w_out'], vec('g_mix_post', l), l, f"merge_out_{l}")
        ff, hn2 = norm_proj(h1, vec('g_ffn_pre', l), wfi[l],
                            lambda: pl.BlockSpec((None, D, D_FF // 2), lambda i, j: (j, 0, 0)), 4, D_FF // 2, f"norm_proj_ffn_{l}")
        act, f, h2, pg, h3 = ffn_out(ff, h1, p, G['w_ffn_out'], vec('g_ffn_post', l), G['w_ple_gate'], wple_g[l], l, f"ffn_out_{l}")
        sv.update(proj=proj, hn=hn, sgu=sgu, bra=bra, cb=cb, brb=brb, pooled=pooled, pm=pm, brc=brc, merged=merged, mo=mo, h1=h1,
                  ff=ff, hn2=hn2, act=act, f=f, h2=h2, pg=pg)
        saved.append(sv)
        h = h3

    dh, loss_row = loss_grad(h, target, "loss_grad")
    loss = lax.psum(loss_row[0, 0], ("x", "y", "c"))

    parts = {k: [None, None] for k in _SHARDED}
    small = {k: [None, None] for k in _VECTORS + ['b_sgu_s', 'w_sgu_s', 'w_dw']}
    for l in (1, 0):
        sv = saved[l]
        dh2, dpe, dpg, df, dff, small['g_ffn_post'][l] = ple_ffn_bwd(
            dh, sv['pg'], p, sv['f'], sv['ff'], wple_g[l], G['w_ple_gate'], G['w_ffn_out'], vec('g_ffn_post', l), l, f"ple_ffn_bwd_{l}")
        dh1, small['g_ffn_pre'][l] = ffn_in_bwd(dff, wfi[l], sv['h1'], dh2, vec('g_ffn_pre', l), f"ffn_in_bwd_{l}")
        dmo, dbra, dbrb, dbrc, dzg, dhn_g, small['g_mix_post'][l] = mix_post_bwd(
            dh1, sv['mo'], vec('g_mix_post', l), G['w_out'], sv['proj'], sv['bra'], sv['brb'], sv['brc'], G['w_in'], l, f"mix_post_bwd_{l}")
        dzs, dhn_a, dws, dbs, small['g_sgu_v'][l], small['b_sgu_v'][l] = sgu_bwd(
            dbra, G['w_sgu_out'], sv['proj'], w_sgu_s[l], sv['wst'], sv['bsfull'], vec('g_sgu_v', l), vec('b_sgu_v', l), G['w_in'], l,
            f"sgu_bwd_{l}")
        small['w_sgu_s'][l] = dws
        small['b_sgu_s'][l] = dbs.T
        dc, dwdw, small['b_dw'][l], small['g_conv_ln'][l], small['b_conv_ln'][l] = conv_bwd(
            dbrb, G['w_conv_out'], sv['proj'], sv['wdw'], vec('b_dw', l), vec('g_conv_ln', l), vec('b_conv_ln', l), l, f"conv_bwd_{l}")
        small['w_dw'][l] = dwdw
        dmr, q, small['s_pool'][l] = pool_bwd(dbrc, G['w_pool_out'], sv['pooled'], wpool_g[l], vec('s_pool', l), l, f"pool_bwd_{l}")
        dzc, dhn_c = seq_bwd(dc, q, sv['proj'], sv['wdw'], G['w_in'], l, f"seq_bwd_{l}")
        dh, small['g_mix_pre'][l] = mix_pre_bwd(sv['h'], dh1, dhn_a, dhn_c, dhn_g, vec('g_mix_pre', l), f"mix_pre_bwd_{l}")

        p_spec = pl.BlockSpec((None, None, min(TS, s_len), PLE), functools.partial(lambda k, n, s, ll: (ll, 0, s, 0), ll=l))
        g_ple = wgrad(p, dpe, PLE, D, f"wgrad_ple_{l}", a_spec=p_spec)
        parts['w_ple'][l] = jnp.transpose(g_ple.reshape(PLE, N_DEV, BLK), (1, 0, 2))
        parts['w_ple_gate'][l] = wgrad(sv['h2'], dpg, D, D, f"wgrad_ple_gate_{l}").reshape(N_DEV, BLK, D)
        parts['w_ffn_out'][l] = wgrad(sv['act'], df, D_FF // 2, D, f"wgrad_ffn_out_{l}").reshape(N_DEV, D_FF // N_DEV, D)
        g_fi = wgrad(sv['hn2'], dff, D, D_FF // 2, f"wgrad_ffn_in_{l}", stacked=True)
        parts['w_ffn_in'][l] = jnp.transpose(g_fi.reshape(4, D, 2, D_FF // 4), (0, 2, 1, 3)).reshape(N_DEV, D, D_FF // 4)
        parts['w_out'][l] = wgrad(sv['merged'], dmo, D, D, f"wgrad_out_{l}").reshape(N_DEV, BLK, D)
        parts['w_sgu_out'][l] = wgrad(sv['sgu'], dbra, D, D, f"wgrad_sgu_out_{l}").reshape(N_DEV, BLK, D)
        parts['w_conv_out'][l] = wgrad(sv['cb'], dbrb, D, D, f"wgrad_conv_out_{l}").reshape(N_DEV, BLK, D)
        parts['w_pool_out'][l] = wgrad(sv['pm'], dbrc, D, D, f"wgrad_pool_out_{l}").reshape(N_DEV, BLK, D)
        g_pool = wgrad(sv['pooled'], dmr, POOL_GD, POOL_GD, f"wgrad_pool_{l}", diag=True)
        parts['w_pool'][l] = jnp.transpose(g_pool.reshape(4, N_DEV, POOL_GD // N_DEV, POOL_GD), (1, 0, 2, 3))
        parts['w_in'][l] = jnp.concatenate([
            wgrad(sv['hn'], dzs, D, D, f"wgrad_in_sgu_{l}", stacked=True),
            wgrad(sv['hn'], dzc, D, D, f"wgrad_in_seq_{l}", stacked=True),
            wgrad(sv['hn'], dzg, D, D, f"wgrad_in_gate_{l}", stacked=True)], axis=0)
    grad_x = dh.reshape(1, s_len, D)

    stacked = [jnp.stack(parts[k], axis=1) for k in _SHARDED]
    from_sibling = rs_to_sibling(stacked, "rs_to_sibling")
    chip_parts = []
    for k, st, rv in zip(_SHARDED, stacked, from_sibling):
        mine = lax.dynamic_index_in_dim(st.reshape((4, 2) + st.shape[1:]), my_c, axis=1, keepdims=False)
        cols = st.shape[-1]
        chip_parts.append(add_bf16(mine.reshape(-1, cols), rv.reshape(-1, cols), f"rs_add_{k}").reshape(rv.shape))
    from_chips = rs_to_chips(chip_parts, "rs_to_chips")

    outs = {}
    for k, cp, rv in zip(_SHARDED, chip_parts, from_chips):
        cols = cp.shape[-1]
        own = lax.dynamic_index_in_dim(cp, my_chip, axis=0, keepdims=False).reshape(-1, cols)
        rows = own.shape[0]
        rv3 = rv.reshape(3, rows, cols)
        res = adamw(W[k].reshape(rows, cols), M[k].reshape(rows, cols), V[k].reshape(rows, cols),
                    [(own, None), (rv3, 0), (rv3, 1), (rv3, 2)], f"adamw_{k}")
        outs[k] = [r.reshape(W[k].shape) for r in res]

    small_full = {k: jnp.stack(small[k], axis=0) for k in small}
    rep_grads = _pack_replicated(small_full)
    dw_grads = small_full['w_dw'].reshape(_DW_ROWS, D)
    gathered_small = all_gather([jnp.concatenate([rep_grads, dw_grads], axis=0)], "all_gather_small_grads")[0]
    rep_part = gathered_small[:, :_REP_PAD, :]
    rep_res = adamw(_pack_replicated(W), _pack_replicated(M), _pack_replicated(V), [(rep_part, d) for d in range(N_DEV)],
                    "adamw_replicated")
    for idx, packed in enumerate(rep_res):
        for name, val in _unpack_replicated(packed).items():
            outs.setdefault(name, [None] * 4)[idx] = val
    dw_sum = sum_slabs(gathered_small[:, _REP_PAD:, :], "sum_w_dw")
    dw_mine = lax.dynamic_slice_in_dim(dw_sum.reshape(2, HALO, D)[:, :CONV_W], my_dev * BLK, BLK, axis=2)
    res = adamw(w_dw.reshape(2 * CONV_W, BLK), m_w_dw.reshape(2 * CONV_W, BLK), v_w_dw.reshape(2 * CONV_W, BLK),
                [(dw_mine.reshape(2 * CONV_W, BLK), None)], "adamw_w_dw")
    outs['w_dw'] = [r.reshape(w_dw.shape) for r in res]

    result = [loss, grad_x]
    for idx in range(4):
        result += [outs[k][idx] for k in _WEIGHTS]
    return tuple(result)
```

```python
import functools
import math

import jax
import jax.numpy as jnp
from jax import lax
from jax.experimental import pallas as pl
from jax.experimental.pallas import tpu as pltpu

F32 = jnp.float32
BF = jnp.bfloat16

D = 1024
D_FF = 2816
PLE = 256
N_DEV = 8
HEADS = 8
BLK = 128
CHUNK = 64
CONV_W = 31
POOL_WINDOWS = (2, 4, 8, 16)
POOL_GD = 256
EPS = 1e-6

V7X_VMEM_BYTES = 64 * 2**20
VMEM_LIMIT = V7X_VMEM_BYTES * 7 // 8
HALO = 32
RC = 64
LC = 128
TM = 512
TMB = 256
TMP = 1024
TS = 2048

ADAM_LR, ADAM_B1, ADAM_B2, ADAM_EPS, ADAM_WD, ADAM_STEP = 0.001, 0.9, 0.999, 1e-08, 0.01, 10

MESH = pl.DeviceIdType.MESH
ANY = pl.BlockSpec(memory_space=pl.ANY)

_GELU_K0 = math.sqrt(2.0 / math.pi)
_GELU_K1 = 0.044715


def _dot(a, b):
    return jnp.dot(a, b, preferred_element_type=F32)


def _dot_nt(a, b):
    return lax.dot_general(a, b, (((1,), (1,)), ((), ())), preferred_element_type=F32)


def _dot_tn(a, b):
    return lax.dot_general(a, b, (((0,), (0,)), ((), ())), preferred_element_type=F32)


def _sig(x):
    return 1.0 / (1.0 + jnp.exp(-x))


def _gelu(x):
    t = jnp.tanh(_GELU_K0 * (x + _GELU_K1 * x * x * x))
    return 0.5 * x * (1.0 + t), t


def _gelu_grad(x, t):
    return 0.5 * (1.0 + t) + 0.5 * x * (1.0 - t * t) * _GELU_K0 * (1.0 + 3.0 * _GELU_K1 * x * x)


def _rstd(x):
    return lax.rsqrt(jnp.mean(x * x, axis=-1, keepdims=True) + EPS)


def _rms_bwd(x, gd, r):
    return r * gd - x * (r * r * r) * jnp.mean(gd * x, axis=-1, keepdims=True)


def _ln_fwd(x):
    mu = jnp.mean(x, axis=-1, keepdims=True)
    xc = x - mu
    rs = lax.rsqrt(jnp.mean(xc * xc, axis=-1, keepdims=True) + EPS)
    return xc * rs, rs


def _ln_bwd(dhat, hat, rs):
    return rs * (dhat - jnp.mean(dhat, axis=-1, keepdims=True) - hat * jnp.mean(dhat * hat, axis=-1, keepdims=True))


def _colsum(x):
    return jnp.sum(x, axis=0, keepdims=True)


def _accum(ref, first, val):
    @pl.when(first)
    def _():
        ref[...] = val

    @pl.when(jnp.logical_not(first))
    def _():
        ref[...] += val


def _sgu_mask(transposed):
    r = lax.broadcasted_iota(jnp.int32, (BLK, BLK), 0) // CHUNK
    c = lax.broadcasted_iota(jnp.int32, (BLK, BLK), 1) // CHUNK
    return (r <= c) if transposed else (c <= r)


def _inv_count(i, tm, w):
    t = lax.broadcasted_iota(jnp.int32, (tm, 1), 0) + i * tm
    return 1.0 / jnp.minimum(t + 1, w).astype(F32)


def _count(i, tm, w):
    t = lax.broadcasted_iota(jnp.int32, (tm, 1), 0) + i * tm
    return jnp.minimum(t + 1, w).astype(F32)


def _params(n_grid):
    return pltpu.CompilerParams(dimension_semantics=("arbitrary",) * n_grid, vmem_limit_bytes=VMEM_LIMIT)


def _sds(shape, dtype):
    return jax.ShapeDtypeStruct(shape, dtype)


def _cols(tm, width, cb):
    return pl.BlockSpec((tm, width), lambda i: (i, cb))


def _whole(shape):
    nd = len(shape)
    return pl.BlockSpec(shape, lambda i: (0,) * nd)


def _prev_halo(tm, width, cb):
    return pl.BlockSpec((HALO, width), lambda i: (jnp.maximum(i * (tm // HALO) - 1, 0), cb))


def _next_halo(tm, width, cb, s_len):
    last = s_len // HALO - 1
    return pl.BlockSpec((HALO, width), lambda i: (jnp.minimum((i + 1) * (tm // HALO), last), cb))


def _rowsharded(layer, rows):
    return pl.BlockSpec((N_DEV, None, rows, D), lambda i: (0, layer, 0, 0))


def _win_block(layer, j):
    return pl.BlockSpec((None, None, D, D), lambda i: (j, layer, 0, 0))


def _row_tile(rows, cap):
    t = min(rows, cap)
    while rows % t or t % 16:
        t -= 16
    return t


def all_gather(arrs, name):
    n = len(arrs)

    def body(*refs):
        ins, outs = refs[:n], refs[n:2 * n]
        send, recv, local = refs[2 * n:]
        x, y, c = lax.axis_index("x"), lax.axis_index("y"), lax.axis_index("c")
        me, sibling = (x, y, c), (x, y, 1 - c)
        chips = [(1 - x, y), (x, 1 - y), (1 - x, 1 - y)]

        def slot(px, py, pc):
            return 4 * px + 2 * py + pc

        def copy(a, k, block, to, src=None):
            dst = outs[a].at[slot(*block)]
            return pltpu.make_async_remote_copy(
                src_ref=dst if src is None else src, dst_ref=dst, send_sem=send.at[a, k], recv_sem=recv.at[a, k],
                device_id=to, device_id_type=MESH)

        mine = [pltpu.make_async_copy(ins[a], outs[a].at[slot(*me)], local.at[a]) for a in range(n)]
        for cp in mine:
            cp.start()
        first = []
        for a in range(n):
            first.append(copy(a, 0, me, sibling, src=ins[a]))
            first += [copy(a, 1 + j, me, (*chip, c), src=ins[a]) for j, chip in enumerate(chips)]
        for cp in first:
            cp.start()
        passed = []
        for a in range(n):
            for j, chip in enumerate(chips):
                copy(a, 1 + j, (*chip, c), me).wait_recv()
                fwd = copy(a, 4 + j, (*chip, c), sibling)
                fwd.start()
                passed.append(fwd)
        for a in range(n):
            copy(a, 0, sibling, me).wait_recv()
            for j, chip in enumerate(chips):
                copy(a, 4 + j, (*chip, 1 - c), me).wait_recv()
        for cp in first + passed:
            cp.wait_send()
        for cp in mine:
            cp.wait()

    return pl.pallas_call(
        body, name=name,
        out_shape=[_sds((N_DEV,) + a.shape, a.dtype) for a in arrs],
        in_specs=[ANY] * n, out_specs=[ANY] * n,
        scratch_shapes=[pltpu.SemaphoreType.DMA((n, 7)), pltpu.SemaphoreType.DMA((n, 7)), pltpu.SemaphoreType.DMA((n,))],
    )(*arrs)


def rs_to_sibling(parts, name):
    n = len(parts)

    def body(*refs):
        ins, outs = refs[:n], refs[n:2 * n]
        send, recv = refs[2 * n:]
        x, y, c = lax.axis_index("x"), lax.axis_index("y"), lax.axis_index("c")
        sibling = (x, y, 1 - c)
        for a in range(n):
            for q in range(4):
                pltpu.make_async_remote_copy(
                    src_ref=ins[a].at[2 * q + 1 - c], dst_ref=outs[a].at[q], send_sem=send.at[a], recv_sem=recv.at[a],
                    device_id=sibling, device_id_type=MESH).start()
        for a in range(n):
            pltpu.make_async_remote_copy(
                src_ref=outs[a], dst_ref=outs[a], send_sem=send.at[a], recv_sem=recv.at[a],
                device_id=sibling, device_id_type=MESH).wait()

    return pl.pallas_call(
        body, name=name,
        out_shape=[_sds((4,) + p.shape[1:], p.dtype) for p in parts],
        in_specs=[ANY] * n, out_specs=[ANY] * n,
        scratch_shapes=[pltpu.SemaphoreType.DMA((n,)), pltpu.SemaphoreType.DMA((n,))],
    )(*parts)


def rs_to_chips(cps, name):
    n = len(cps)

    def body(*refs):
        ins, outs = refs[:n], refs[n:2 * n]
        send, recv = refs[2 * n:]
        x, y, c = lax.axis_index("x"), lax.axis_index("y"), lax.axis_index("c")
        chips = [(1 - x, y), (x, 1 - y), (1 - x, 1 - y)]
        for a in range(n):
            for r, (px, py) in enumerate(chips):
                pltpu.make_async_remote_copy(
                    src_ref=ins[a].at[2 * px + py], dst_ref=outs[a].at[r], send_sem=send.at[a], recv_sem=recv.at[a],
                    device_id=(px, py, c), device_id_type=MESH).start()
        for a in range(n):
            pltpu.make_async_remote_copy(
                src_ref=outs[a], dst_ref=outs[a], send_sem=send.at[a], recv_sem=recv.at[a],
                device_id=(x, y, c), device_id_type=MESH).wait()

    return pl.pallas_call(
        body, name=name,
        out_shape=[_sds((3,) + p.shape[1:], p.dtype) for p in cps],
        in_specs=[ANY] * n, out_specs=[ANY] * n,
        scratch_shapes=[pltpu.SemaphoreType.DMA((n,)), pltpu.SemaphoreType.DMA((n,))],
    )(*cps)


def add_bf16(a, b, name):
    rows, cols = a.shape
    tr = _row_tile(rows, 512)

    def body(a_ref, b_ref, o_ref):
        o_ref[...] = (a_ref[...].astype(F32) + b_ref[...].astype(F32)).astype(o_ref.dtype)

    spec = pl.BlockSpec((tr, cols), lambda i: (i, 0))
    return pl.pallas_call(body, name=name, grid=(rows // tr,), in_specs=[spec, spec], out_specs=spec,
                          out_shape=_sds(a.shape, BF), compiler_params=_params(1))(a, b)


def adamw(w, m, v, pieces, name):
    rows, cols = w.shape
    tr = _row_tile(rows, 256) if rows % 16 == 0 else rows
    np_ = len(pieces)
    c1 = 1.0 / (1.0 - ADAM_B1 ** ADAM_STEP)
    c2 = 1.0 / (1.0 - ADAM_B2 ** ADAM_STEP)

    def body(*refs):
        w_ref, m_ref, v_ref = refs[:3]
        p_refs = refs[3:3 + np_]
        g_ref, d_ref, nm_ref, nv_ref = refs[3 + np_:]
        g = p_refs[0][...].astype(F32)
        for pr in p_refs[1:]:
            g = g + pr[...].astype(F32)
        nm = ADAM_B1 * m_ref[...] + (1.0 - ADAM_B1) * g
        nv = ADAM_B2 * v_ref[...] + (1.0 - ADAM_B2) * (g * g)
        g_ref[...] = g
        nm_ref[...] = nm
        nv_ref[...] = nv
        d_ref[...] = -ADAM_LR * ((nm * c1) / (jnp.sqrt(nv * c2) + ADAM_EPS) + ADAM_WD * w_ref[...])

    spec = pl.BlockSpec((tr, cols), lambda i: (i, 0))
    p_specs = []
    for arr, k in pieces:
        if k is None:
            p_specs.append(spec)
        else:
            p_specs.append(pl.BlockSpec((None, tr, cols), functools.partial(lambda i, kk: (kk, i, 0), kk=k)))
    out = _sds(w.shape, F32)
    return pl.pallas_call(body, name=name, grid=(rows // tr,), in_specs=[spec] * 3 + p_specs, out_specs=[spec] * 4,
                          out_shape=[out] * 4, compiler_params=_params(1))(w, m, v, *[a for a, _ in pieces])


def sum_slabs(g, name):
    n, rows, cols = g.shape

    def body(g_ref, o_ref):
        s = g_ref[0]
        for k in range(1, n):
            s = s + g_ref[k]
        o_ref[...] = s

    return pl.pallas_call(body, name=name, out_shape=_sds((rows, cols), F32))(g)


def norm_proj(h, g, w, wspec_fn, nb, tn, name):
    s_len = h.shape[0]
    tm = min(TMP, s_len)
    nt = s_len // tm

    def body(h_ref, g_ref, w_ref, o_ref, hn_ref, hn_s):
        rows = pl.ds(pl.multiple_of(pl.program_id(1) * tm, tm), tm)

        @pl.when(pl.program_id(0) == 0)
        def _():
            x = h_ref[...]
            hn = (x * _rstd(x) * g_ref[...]).astype(BF)
            hn_s[rows, :] = hn
            hn_ref[...] = hn
        o_ref[...] = _dot(hn_s[rows, :], w_ref[...]).astype(BF)

    def first_pass_rows(j, i):
        return (jnp.where(j == 0, i, nt - 1), 0)

    return pl.pallas_call(
        body, name=name, grid=(nb, nt),
        in_specs=[pl.BlockSpec((tm, D), first_pass_rows), pl.BlockSpec((1, D), lambda j, i: (0, 0)), wspec_fn()],
        out_specs=[pl.BlockSpec((tm, tn), lambda j, i: (i, j)), pl.BlockSpec((tm, D), first_pass_rows)],
        out_shape=[_sds((s_len, nb * tn), BF), _sds((s_len, D), BF)],
        scratch_shapes=[pltpu.VMEM((s_len, D), BF)], compiler_params=_params(2))(h, g, w)


def _sgu_mix(ws_ref, vln_s, mix_s, bs_ref, tm, transposed):
    mask = _sgu_mask(transposed)
    for hd in range(HEADS):
        wm = jnp.where(mask, ws_ref[hd], 0.0).astype(BF)
        cs = slice(hd * BLK, (hd + 1) * BLK)
        for n in range(tm // BLK):
            rs = slice(n * BLK, (n + 1) * BLK)
            r = _dot(wm, vln_s[rs, cs])
            mix_s[rs, cs] = r if bs_ref is None else r + bs_ref[:, cs]


def sgu_fwd(proj, ws, bsfull, gv, bv, wo, layer, name):
    s_len = proj.shape[0]
    tm = min(TM, s_len)

    def body(zu_ref, zv_ref, ws_ref, bs_ref, gv_ref, bv_ref, wo_ref, sgu_ref, br_ref, vln_s, mix_s):
        u, _ = _gelu(zu_ref[...].astype(F32))
        v, _ = _gelu(zv_ref[...].astype(F32))
        vhat, _ = _ln_fwd(v)
        vln_s[...] = (vhat * gv_ref[...] + bv_ref[...]).astype(BF)
        _sgu_mix(ws_ref, vln_s, mix_s, bs_ref, tm, False)
        sgu = (u * mix_s[...]).astype(BF)
        sgu_ref[...] = sgu
        br_ref[...] = _dot(sgu, wo_ref[...].reshape(D, D)).astype(BF)

    return pl.pallas_call(
        body, name=name, grid=(s_len // tm,),
        in_specs=[_cols(tm, D, 0), _cols(tm, D, 1), _whole((HEADS, BLK, BLK)), _whole((BLK, D)), _whole((1, D)), _whole((1, D)),
                  _rowsharded(layer, BLK)],
        out_specs=[_cols(tm, D, 0)] * 2, out_shape=[_sds((s_len, D), BF)] * 2,
        scratch_shapes=[pltpu.VMEM((tm, D), BF), pltpu.VMEM((tm, D), F32)], compiler_params=_params(1),
    )(proj, proj, ws, bsfull, gv, bv, wo)


def _causal_conv(ext_s, out_s, wdw_ref, bias_ref, tm):
    def chunk(ci, carry):
        r0 = pl.multiple_of((ci // (D // LC)) * RC, RC)
        l0 = pl.multiple_of((ci % (D // LC)) * LC, LC)
        win = ext_s[pl.ds(r0, RC + HALO), pl.ds(l0, LC)]
        acc = jnp.broadcast_to(bias_ref[:, pl.ds(l0, LC)], (RC, LC))
        for r in range(8):
            wr = win if r == 0 else pltpu.roll(win, r, 0)
            for m in range(4):
                d = 8 * m + r
                if d < CONV_W:
                    k = CONV_W - 1 - d
                    acc = acc + wdw_ref[k:k + 1, pl.ds(l0, LC)] * wr[HALO - 8 * m:HALO - 8 * m + RC]
        out_s[pl.ds(r0, RC), pl.ds(l0, LC)] = acc
        return carry
    lax.fori_loop(0, (tm // RC) * (D // LC), chunk, 0)


def _glu_ext(a_ref, g_ref, ah_ref, gh_ref, ext_s, first):
    hh = ah_ref[...].astype(F32) * _sig(gh_ref[...].astype(F32))
    ext_s[0:HALO, :] = jnp.where(first, 0.0, hh)
    ext_s[HALO:, :] = a_ref[...].astype(F32) * _sig(g_ref[...].astype(F32))


def conv_fwd(proj, wdw, bdw, gln, bln, wo, layer, name):
    s_len = proj.shape[0]
    tm = min(TM, s_len)

    def body(a_ref, g_ref, ah_ref, gh_ref, wdw_ref, bdw_ref, gln_ref, bln_ref, wo_ref, cv_ref, cb_ref, br_ref, ext_s, conv_s):
        _glu_ext(a_ref, g_ref, ah_ref, gh_ref, ext_s, pl.program_id(0) == 0)
        _causal_conv(ext_s, conv_s, wdw_ref, bdw_ref, tm)
        cv = conv_s[...].astype(BF)
        cv_ref[...] = cv
        chat, _ = _ln_fwd(cv.astype(F32))
        yl = chat * gln_ref[...] + bln_ref[...]
        cb = (yl * _sig(yl)).astype(BF)
        cb_ref[...] = cb
        br_ref[...] = _dot(cb, wo_ref[...].reshape(D, D)).astype(BF)

    return pl.pallas_call(
        body, name=name, grid=(s_len // tm,),
        in_specs=[_cols(tm, D, 2), _cols(tm, D, 3), _prev_halo(tm, D, 2), _prev_halo(tm, D, 3), _whole((HALO, D)),
                  _whole((1, D)), _whole((1, D)), _whole((1, D)), _rowsharded(layer, BLK)],
        out_specs=[_cols(tm, D, 0)] * 3, out_shape=[_sds((s_len, D), BF)] * 3,
        scratch_shapes=[pltpu.VMEM((tm + HALO, D), F32), pltpu.VMEM((tm, D), F32)], compiler_params=_params(1),
    )(proj, proj, proj, proj, wdw, bdw, gln, bln, wo)


def pool_fwd(proj, wpool, spool, wo, layer, name):
    s_len = proj.shape[0]
    tm = min(TM, s_len)

    def body(z_ref, zh_ref, wp_ref, sp_ref, wo_ref, pooled_ref, pm_ref, br_ref, ext_s, mr_s):
        i = pl.program_id(0)
        ext_s[0:HALO, :] = jnp.where(i == 0, 0.0, zh_ref[...].astype(F32))
        ext_s[HALO:, :] = z_ref[...].astype(F32)
        for gi, w in enumerate(POOL_WINDOWS):
            cs = slice(gi * POOL_GD, (gi + 1) * POOL_GD)
            e = ext_s[:, cs]
            s = e
            sh = 1
            while sh < w:
                s = s + pltpu.roll(s, sh, 0)
                sh *= 2
            pooled = (s[HALO:] * _inv_count(i, tm, w) - e[HALO:]).astype(BF)
            pooled_ref[:, cs] = pooled
            mr_s[:, cs] = _dot(pooled, wp_ref[gi])
        pm = (mr_s[...] * sp_ref[...]).astype(BF)
        pm_ref[...] = pm
        br_ref[...] = _dot(pm, wo_ref[...].reshape(D, D)).astype(BF)

    return pl.pallas_call(
        body, name=name, grid=(s_len // tm,),
        in_specs=[_cols(tm, D, 4), _prev_halo(tm, D, 4), _whole((4, POOL_GD, POOL_GD)), _whole((1, D)), _rowsharded(layer, BLK)],
        out_specs=[_cols(tm, D, 0)] * 3, out_shape=[_sds((s_len, D), BF)] * 3,
        scratch_shapes=[pltpu.VMEM((tm + HALO, D), F32), pltpu.VMEM((tm, D), F32)], compiler_params=_params(1),
    )(proj, proj, wpool, spool, wo)


def merge_out(proj, bra, brb, brc, h, wout, gpost, layer, name):
    s_len = h.shape[0]
    tm = min(TM, s_len)

    def body(z0, z1, z2, a_ref, b_ref, c_ref, h_ref, wo_ref, g_ref, mg_ref, mo_ref, h1_ref):
        merged = (_sig(z0[...].astype(F32)) * a_ref[...].astype(F32) + _sig(z1[...].astype(F32)) * b_ref[...].astype(F32)
                  + _sig(z2[...].astype(F32)) * c_ref[...].astype(F32)).astype(BF)
        mg_ref[...] = merged
        mo = _dot(merged, wo_ref[...].reshape(D, D))
        mo_ref[...] = mo.astype(BF)
        h1_ref[...] = h_ref[...] + mo * _rstd(mo) * g_ref[...]

    row = _cols(tm, D, 0)
    return pl.pallas_call(
        body, name=name, grid=(s_len // tm,),
        in_specs=[_cols(tm, D, 5), _cols(tm, D, 6), _cols(tm, D, 7), row, row, row, row, _rowsharded(layer, BLK), _whole((1, D))],
        out_specs=[row] * 3, out_shape=[_sds((s_len, D), BF), _sds((s_len, D), BF), _sds((s_len, D), F32)],
        compiler_params=_params(1))(proj, proj, proj, bra, brb, brc, h, wout, gpost)


def _p_spec(tm, layer):
    return pl.BlockSpec((None, None, tm, PLE), lambda i: (layer, 0, i, 0))


def ffn_out(ff, h1, p, wfo, gpost, wpg, wple, layer, name):
    s_len = h1.shape[0]
    tm = min(TMB, s_len)

    def body(fg_ref, fu_ref, h1_ref, p_ref, wfo_ref, g_ref, wpg_ref, wple_ref, act_ref, f_ref, h2_ref, pg_ref, h3_ref):
        gt = fg_ref[...].astype(F32)
        act = (gt * _sig(gt) * fu_ref[...].astype(F32)).astype(BF)
        act_ref[...] = act
        f = _dot(act, wfo_ref[...].reshape(D_FF, D))
        f_ref[...] = f.astype(BF)
        h2 = h1_ref[...] + f * _rstd(f) * g_ref[...]
        h2_ref[...] = h2
        pg = _dot(h2.astype(BF), wpg_ref[...].reshape(D, D)).astype(BF)
        pg_ref[...] = pg
        pe = _dot(p_ref[...].astype(BF), wple_ref[...])
        h3_ref[...] = h2 + _sig(pg.astype(F32)) * pe

    row = _cols(tm, D, 0)
    return pl.pallas_call(
        body, name=name, grid=(s_len // tm,),
        in_specs=[_cols(tm, D_FF, 0), _cols(tm, D_FF, 1), row, _p_spec(tm, layer), _rowsharded(layer, D_FF // N_DEV),
                  _whole((1, D)), _rowsharded(layer, BLK), _whole((PLE, D))],
        out_specs=[_cols(tm, D_FF, 0), row, row, row, row],
        out_shape=[_sds((s_len, D_FF), BF), _sds((s_len, D), BF), _sds((s_len, D), F32), _sds((s_len, D), BF), _sds((s_len, D), F32)],
        compiler_params=_params(1))(ff, ff, h1, p, wfo, gpost, wpg, wple)


def loss_grad(y, target, name):
    s_len = y.shape[0]
    tm = min(TM, s_len)
    nt = s_len // tm

    def body(y_ref, t_ref, dy_ref, loss_ref, acc):
        i = pl.program_id(0)
        e = y_ref[...] - t_ref[...]
        dy_ref[...] = e * (1.0 / D)
        _accum(acc, i == 0, _colsum(e * e))

        @pl.when(i == nt - 1)
        def _():
            loss_ref[...] = jnp.broadcast_to(jnp.sum(acc[...], axis=1, keepdims=True) * (0.5 / D), (1, LC))

    row = _cols(tm, D, 0)
    return pl.pallas_call(
        body, name=name, grid=(nt,), in_specs=[row, row], out_specs=[row, _whole((1, LC))],
        out_shape=[_sds((s_len, D), F32), _sds((1, LC), F32)], scratch_shapes=[pltpu.VMEM((1, D), F32)],
        compiler_params=_params(1))(y, target)


def ple_ffn_bwd(dh3, pg, p, f, ff, wple, wpg, wfo, gpost, layer, name):
    s_len = dh3.shape[0]
    tm = min(TMB, s_len)

    def body(dh3_ref, pg_ref, p_ref, f_ref, fg_ref, fu_ref, wple_ref, wpg_ref, wfo_ref, g_ref,
             dh2_ref, dpe_ref, dpg_ref, df_ref, dff_ref, dg_ref):
        i = pl.program_id(0)
        dh3v = dh3_ref[...]
        s = _sig(pg_ref[...].astype(F32))
        pe = _dot(p_ref[...].astype(BF), wple_ref[...])
        dpe_ref[...] = (dh3v * s).astype(BF)
        dpg = (dh3v * pe * s * (1.0 - s)).astype(BF)
        dpg_ref[...] = dpg
        dh2 = dh3v + _dot_nt(dpg, wpg_ref[...].reshape(D, D))
        dh2_ref[...] = dh2
        fv = f_ref[...].astype(F32)
        r = _rstd(fv)
        _accum(dg_ref, i == 0, _colsum(dh2 * fv * r))
        df = _rms_bwd(fv, dh2 * g_ref[...], r).astype(BF)
        df_ref[...] = df
        dact = _dot_nt(df, wfo_ref[...].reshape(D_FF, D))
        gt = fg_ref[...].astype(F32)
        sg = _sig(gt)
        up = fu_ref[...].astype(F32)
        dff_ref[:, 0:D_FF] = (dact * up * sg * (1.0 + gt * (1.0 - sg))).astype(BF)
        dff_ref[:, D_FF:2 * D_FF] = (dact * gt * sg).astype(BF)

    row = _cols(tm, D, 0)
    return pl.pallas_call(
        body, name=name, grid=(s_len // tm,),
        in_specs=[row, row, _p_spec(tm, layer), row, _cols(tm, D_FF, 0), _cols(tm, D_FF, 1), _whole((PLE, D)),
                  _rowsharded(layer, BLK), _rowsharded(layer, D_FF // N_DEV), _whole((1, D))],
        out_specs=[row, row, row, row, _cols(tm, 2 * D_FF, 0), _whole((1, D))],
        out_shape=[_sds((s_len, D), F32), _sds((s_len, D), BF), _sds((s_len, D), BF), _sds((s_len, D), BF),
                   _sds((s_len, 2 * D_FF), BF), _sds((1, D), F32)],
        compiler_params=_params(1))(dh3, pg, p, f, ff, ff, wple, wpg, wfo, gpost)


def ffn_in_bwd(dff, w3, h1, dh2, gpre, name):
    s_len = h1.shape[0]
    tm = min(TMB, s_len)
    nb, _, tn = w3.shape

    def body(dff_ref, w_ref, h1_ref, dh2_ref, g_ref, dh1_ref, dg_ref):
        dhn = _dot_nt(dff_ref[:, 0:tn], w_ref[0])
        for j in range(1, nb):
            dhn = dhn + _dot_nt(dff_ref[:, j * tn:(j + 1) * tn], w_ref[j])
        x = h1_ref[...]
        r = _rstd(x)
        _accum(dg_ref, pl.program_id(0) == 0, _colsum(dhn * x * r))
        dh1_ref[...] = dh2_ref[...] + _rms_bwd(x, dhn * g_ref[...], r)

    row = _cols(tm, D, 0)
    return pl.pallas_call(
        body, name=name, grid=(s_len // tm,),
        in_specs=[_cols(tm, nb * tn, 0), pl.BlockSpec((nb, D, tn), lambda i: (0, 0, 0), pipeline_mode=pl.Buffered(1)), row, row,
                  _whole((1, D))],
        out_specs=[row, _whole((1, D))],
        out_shape=[_sds((s_len, D), F32), _sds((1, D), F32)], compiler_params=_params(1))(dff, w3, h1, dh2, gpre)


def mix_post_bwd(dh1, mo, gpost, wout, proj, bra, brb, brc, win, layer, name):
    s_len = dh1.shape[0]
    tm = min(TMB, s_len)

    def body(dh1_ref, mo_ref, g_ref, wo_ref, z0, z1, z2, a_ref, b_ref, c_ref, w5, w6, w7,
             dmo_ref, da_ref, db_ref, dc_ref, dz_ref, dhn_ref, dg_ref):
        i = pl.program_id(0)
        dh1v = dh1_ref[...]
        mo_v = mo_ref[...].astype(F32)
        r = _rstd(mo_v)
        _accum(dg_ref, i == 0, _colsum(dh1v * mo_v * r))
        dmo = _rms_bwd(mo_v, dh1v * g_ref[...], r).astype(BF)
        dmo_ref[...] = dmo
        dmerged = _dot_nt(dmo, wo_ref[...].reshape(D, D))
        dhn = jnp.zeros((tm, D), F32)
        for k, (z, br, dbr, w) in enumerate(((z0, a_ref, da_ref, w5), (z1, b_ref, db_ref, w6), (z2, c_ref, dc_ref, w7))):
            s = _sig(z[...].astype(F32))
            dbr[...] = (dmerged * s).astype(BF)
            dz = (dmerged * br[...].astype(F32) * s * (1.0 - s)).astype(BF)
            dz_ref[:, k * D:(k + 1) * D] = dz
            dhn = dhn + _dot_nt(dz, w[...])
        dhn_ref[...] = dhn

    row = _cols(tm, D, 0)
    return pl.pallas_call(
        body, name=name, grid=(s_len // tm,),
        in_specs=[row, row, _whole((1, D)), _rowsharded(layer, BLK), _cols(tm, D, 5), _cols(tm, D, 6), _cols(tm, D, 7), row, row, row,
                  _win_block(layer, 5), _win_block(layer, 6), _win_block(layer, 7)],
        out_specs=[row, row, row, row, _cols(tm, 3 * D, 0), row, _whole((1, D))],
        out_shape=[_sds((s_len, D), BF)] * 4 + [_sds((s_len, 3 * D), BF), _sds((s_len, D), F32), _sds((1, D), F32)],
        compiler_params=_params(1))(dh1, mo, gpost, wout, proj, proj, proj, bra, brb, brc, win, win, win)


def sgu_bwd(dbr, wo, proj, ws, wst, bsfull, gv, bv, win, layer, name):
    s_len = dbr.shape[0]
    tm = min(TMB, s_len)
    nt = s_len // tm

    def body(dbr_ref, wo_ref, zu_ref, zv_ref, ws_ref, wst_ref, bs_ref, gv_ref, bv_ref, w0, w1,
             dz_ref, dhn_ref, dws_ref, dbs_ref, dgv_ref, dbv_ref, vln_s, mix_s, dmix_s, dvln_s, bs_acc):
        i = pl.program_id(0)
        first = i == 0
        dsgu = _dot_nt(dbr_ref[...], wo_ref[...].reshape(D, D))
        zu = zu_ref[...].astype(F32)
        zv = zv_ref[...].astype(F32)
        u, tu = _gelu(zu)
        v, tv = _gelu(zv)
        vhat, rs = _ln_fwd(v)
        vln_s[...] = (vhat * gv_ref[...] + bv_ref[...]).astype(BF)
        _sgu_mix(ws_ref, vln_s, mix_s, bs_ref, tm, False)
        du = dsgu * mix_s[...]
        dmix = dsgu * u
        dmix_s[...] = dmix.astype(BF)
        blocks = dmix[0:BLK]
        for n in range(1, tm // BLK):
            blocks = blocks + dmix[n * BLK:(n + 1) * BLK]
        _accum(bs_acc, first, blocks)
        for hd in range(HEADS):
            cs = slice(hd * BLK, (hd + 1) * BLK)
            g = _dot_nt(dmix_s[0:BLK, cs], vln_s[0:BLK, cs])
            for n in range(1, tm // BLK):
                g = g + _dot_nt(dmix_s[n * BLK:(n + 1) * BLK, cs], vln_s[n * BLK:(n + 1) * BLK, cs])

            @pl.when(first)
            def _():
                dws_ref[hd] = g

            @pl.when(jnp.logical_not(first))
            def _():
                dws_ref[hd] += g
        _sgu_mix(wst_ref, dmix_s, dvln_s, None, tm, True)
        dvln = dvln_s[...]
        _accum(dgv_ref, first, _colsum(dvln * vhat))
        _accum(dbv_ref, first, _colsum(dvln))
        dv = _ln_bwd(dvln * gv_ref[...], vhat, rs)
        dzu = (du * _gelu_grad(zu, tu)).astype(BF)
        dzv = (dv * _gelu_grad(zv, tv)).astype(BF)
        dz_ref[:, 0:D] = dzu
        dz_ref[:, D:2 * D] = dzv
        dhn_ref[...] = _dot_nt(dzu, w0[...]) + _dot_nt(dzv, w1[...])

        @pl.when(i == nt - 1)
        def _():
            mask = _sgu_mask(False)
            for hd in range(HEADS):
                dws_ref[hd] = jnp.where(mask, dws_ref[hd], 0.0)
                dbs_ref[:, hd:hd + 1] = jnp.sum(bs_acc[:, hd * BLK:(hd + 1) * BLK], axis=1, keepdims=True)

    row = _cols(tm, D, 0)
    vec = _whole((1, D))
    return pl.pallas_call(
        body, name=name, grid=(nt,),
        in_specs=[row, _rowsharded(layer, BLK), _cols(tm, D, 0), _cols(tm, D, 1), _whole((HEADS, BLK, BLK)), _whole((HEADS, BLK, BLK)),
                  _whole((BLK, D)), vec, vec, _win_block(layer, 0), _win_block(layer, 1)],
        out_specs=[_cols(tm, 2 * D, 0), row, _whole((HEADS, BLK, BLK)), _whole((BLK, HEADS)), vec, vec],
        out_shape=[_sds((s_len, 2 * D), BF), _sds((s_len, D), F32), _sds((HEADS, BLK, BLK), F32), _sds((BLK, HEADS), F32),
                   _sds((1, D), F32), _sds((1, D), F32)],
        scratch_shapes=[pltpu.VMEM((tm, D), BF), pltpu.VMEM((tm, D), F32), pltpu.VMEM((tm, D), BF), pltpu.VMEM((tm, D), F32),
                        pltpu.VMEM((BLK, D), F32)],
        compiler_params=_params(1))(dbr, wo, proj, proj, ws, wst, bsfull, gv, bv, win, win)


def conv_bwd(dbr, wo, proj, conv, gln, bln, layer, name):
    s_len = dbr.shape[0]
    tm = min(TMB, s_len)
    nt = s_len // tm

    def body(dbr_ref, wo_ref, a_ref, g_ref, ah_ref, gh_ref, cv_ref, gln_ref, bln_ref,
             dc_ref, dw_ref, dbdw_ref, dgln_ref, dbln_ref, ext_s, dc_s, dw_acc):
        i = pl.program_id(0)
        first = i == 0
        dcb = _dot_nt(dbr_ref[...], wo_ref[...].reshape(D, D))
        _glu_ext(a_ref, g_ref, ah_ref, gh_ref, ext_s, first)
        chat, rs = _ln_fwd(cv_ref[...].astype(F32))
        yl = chat * gln_ref[...] + bln_ref[...]
        sy = _sig(yl)
        dyl = dcb * sy * (1.0 + yl * (1.0 - sy))
        _accum(dgln_ref, first, _colsum(dyl * chat))
        _accum(dbln_ref, first, _colsum(dyl))
        dc = _ln_bwd(dyl * gln_ref[...], chat, rs)
        _accum(dbdw_ref, first, _colsum(dc))
        dc_ref[...] = dc.astype(BF)
        dc_s[...] = dc

        @pl.when(first)
        def _():
            dw_acc[...] = jnp.zeros_like(dw_acc)

        def chunk(ci, carry):
            r0 = pl.multiple_of((ci // (D // LC)) * RC, RC)
            l0 = pl.multiple_of((ci % (D // LC)) * LC, LC)
            win = ext_s[pl.ds(r0, RC + HALO), pl.ds(l0, LC)]
            dcw = dc_s[pl.ds(r0, RC), pl.ds(l0, LC)]
            for r in range(8):
                wr = win if r == 0 else pltpu.roll(win, r, 0)
                for m in range(4):
                    d = 8 * m + r
                    if d < CONV_W:
                        k = CONV_W - 1 - d
                        prod = dcw * wr[HALO - 8 * m:HALO - 8 * m + RC]
                        dw_acc[k * 8:(k + 1) * 8, pl.ds(l0, LC)] += prod.reshape(RC // 8, 8, LC).sum(axis=0)
            return carry
        lax.fori_loop(0, (tm // RC) * (D // LC), chunk, 0)

        @pl.when(i == nt - 1)
        def _():
            dw_ref[...] = dw_acc[...].reshape(HALO, 8, D).sum(axis=1)

    row = _cols(tm, D, 0)
    vec = _whole((1, D))
    return pl.pallas_call(
        body, name=name, grid=(nt,),
        in_specs=[row, _rowsharded(layer, BLK), _cols(tm, D, 2), _cols(tm, D, 3), _prev_halo(tm, D, 2), _prev_halo(tm, D, 3),
                  row, vec, vec],
        out_specs=[row, _whole((HALO, D)), vec, vec, vec],
        out_shape=[_sds((s_len, D), BF), _sds((HALO, D), F32), _sds((1, D), F32), _sds((1, D), F32), _sds((1, D), F32)],
        scratch_shapes=[pltpu.VMEM((tm + HALO, D), F32), pltpu.VMEM((tm, D), F32), pltpu.VMEM((HALO * 8, D), F32)],
        compiler_params=_params(1))(dbr, wo, proj, proj, proj, proj, conv, gln, bln)


def pool_bwd(dbr, wo, pooled, wpool, spool, layer, name):
    s_len = dbr.shape[0]
    tm = min(TMB, s_len)

    def body(dbr_ref, wo_ref, pl_ref, wp_ref, sp_ref, dmr_ref, q_ref, dsp_ref, mr_s):
        i = pl.program_id(0)
        dpm = _dot_nt(dbr_ref[...], wo_ref[...].reshape(D, D))
        for gi in range(4):
            cs = slice(gi * POOL_GD, (gi + 1) * POOL_GD)
            mr_s[:, cs] = _dot(pl_ref[:, cs], wp_ref[gi])
        _accum(dsp_ref, i == 0, _colsum(dpm * mr_s[...]))
        dmr = (dpm * sp_ref[...]).astype(BF)
        dmr_ref[...] = dmr
        for gi, w in enumerate(POOL_WINDOWS):
            cs = slice(gi * POOL_GD, (gi + 1) * POOL_GD)
            q_ref[:, cs] = (_dot_nt(dmr[:, cs], wp_ref[gi]) * _inv_count(i, tm, w)).astype(BF)

    row = _cols(tm, D, 0)
    return pl.pallas_call(
        body, name=name, grid=(s_len // tm,),
        in_specs=[row, _rowsharded(layer, BLK), row, _whole((4, POOL_GD, POOL_GD)), _whole((1, D))],
        out_specs=[row, row, _whole((1, D))],
        out_shape=[_sds((s_len, D), BF), _sds((s_len, D), BF), _sds((1, D), F32)],
        scratch_shapes=[pltpu.VMEM((tm, D), F32)], compiler_params=_params(1))(dbr, wo, pooled, wpool, spool)


def seq_bwd(dc, q, proj, wdw, win, layer, name):
    s_len = dc.shape[0]
    tm = min(TMB, s_len)
    nt = s_len // tm

    def body(dc_ref, dch_ref, q_ref, qh_ref, a_ref, g_ref, wdw_ref, w2, w3, w4, dz_ref, dhn_ref, ext_s, dhc_s, qext_s):
        i = pl.program_id(0)
        last = i == nt - 1
        ext_s[0:tm, :] = dc_ref[...].astype(F32)
        ext_s[tm:, :] = jnp.where(last, 0.0, dch_ref[...].astype(F32))

        def chunk(ci, carry):
            r0 = pl.multiple_of((ci // (D // LC)) * RC, RC)
            l0 = pl.multiple_of((ci % (D // LC)) * LC, LC)
            win_ = ext_s[pl.ds(r0, RC + HALO), pl.ds(l0, LC)]
            acc = jnp.zeros((RC, LC), F32)
            for r in range(8):
                wr = win_ if r == 0 else pltpu.roll(win_, RC + HALO - r, 0)
                for m in range(4):
                    d = 8 * m + r
                    if d < CONV_W:
                        k = CONV_W - 1 - d
                        acc = acc + wdw_ref[k:k + 1, pl.ds(l0, LC)] * wr[8 * m:8 * m + RC]
            dhc_s[pl.ds(r0, RC), pl.ds(l0, LC)] = acc
            return carry
        lax.fori_loop(0, (tm // RC) * (D // LC), chunk, 0)

        dhc = dhc_s[...]
        av = a_ref[...].astype(F32)
        sg = _sig(g_ref[...].astype(F32))
        da = (dhc * sg).astype(BF)
        dg = (dhc * av * sg * (1.0 - sg)).astype(BF)
        dz_ref[:, 0:D] = da
        dz_ref[:, D:2 * D] = dg

        qext_s[0:tm, :] = q_ref[...].astype(F32)
        qext_s[tm:, :] = jnp.where(last, 0.0, qh_ref[...].astype(F32))
        for gi, w in enumerate(POOL_WINDOWS):
            cs = slice(gi * POOL_GD, (gi + 1) * POOL_GD)
            e = qext_s[:, cs]
            s = e
            sh = 1
            while sh < w:
                s = s + pltpu.roll(s, tm + HALO - sh, 0)
                sh *= 2
            dz_ref[:, 2 * D + gi * POOL_GD:2 * D + (gi + 1) * POOL_GD] = (s[0:tm] - e[0:tm] * _count(i, tm, w)).astype(BF)
        dhn_ref[...] = _dot_nt(da, w2[...]) + _dot_nt(dg, w3[...]) + _dot_nt(dz_ref[:, 2 * D:3 * D], w4[...])

    row = _cols(tm, D, 0)
    return pl.pallas_call(
        body, name=name, grid=(nt,),
        in_specs=[row, _next_halo(tm, D, 0, s_len), row, _next_halo(tm, D, 0, s_len), _cols(tm, D, 2), _cols(tm, D, 3),
                  _whole((HALO, D)), _win_block(layer, 2), _win_block(layer, 3), _win_block(layer, 4)],
        out_specs=[_cols(tm, 3 * D, 0), row],
        out_shape=[_sds((s_len, 3 * D), BF), _sds((s_len, D), F32)],
        scratch_shapes=[pltpu.VMEM((tm + HALO, D), F32), pltpu.VMEM((tm, D), F32), pltpu.VMEM((tm + HALO, D), F32)],
        compiler_params=_params(1))(dc, dc, q, q, proj, proj, wdw, win, win, win)


def mix_pre_bwd(h, dh1, dhn_a, dhn_c, dhn_g, gpre, name):
    s_len = h.shape[0]
    tm = min(TM, s_len)

    def body(h_ref, dh1_ref, a_ref, c_ref, g3_ref, g_ref, dh_ref, dg_ref):
        x = h_ref[...]
        r = _rstd(x)
        dhn = a_ref[...] + c_ref[...] + g3_ref[...]
        _accum(dg_ref, pl.program_id(0) == 0, _colsum(dhn * x * r))
        dh_ref[...] = dh1_ref[...] + _rms_bwd(x, dhn * g_ref[...], r)

    row = _cols(tm, D, 0)
    return pl.pallas_call(
        body, name=name, grid=(s_len // tm,), in_specs=[row] * 5 + [_whole((1, D))], out_specs=[row, _whole((1, D))],
        out_shape=[_sds((s_len, D), F32), _sds((1, D), F32)], compiler_params=_params(1))(h, dh1, dhn_a, dhn_c, dhn_g, gpre)


def wgrad(a, b, tk, tn, name, stacked=False, diag=False, a_spec=None):
    s_len = b.shape[0]
    k_dim = a.shape[-1]
    n_dim = b.shape[1]
    ts = min(TS, s_len)
    nk = 1 if diag else k_dim // tk
    nn, ns = n_dim // tn, s_len // ts

    def body(a_ref, b_ref, o_ref, acc):
        s = pl.program_id(2)
        _accum(acc, s == 0, _dot_tn(a_ref[...].astype(BF), b_ref[...].astype(BF)))

        @pl.when(s == ns - 1)
        def _():
            o_ref[...] = acc[...].astype(BF).reshape(o_ref.shape)

    if a_spec is None:
        a_spec = pl.BlockSpec((ts, tk), (lambda k, n, s: (s, n)) if diag else (lambda k, n, s: (s, k)))
    if stacked or diag:
        out_shape = _sds((nn, tk if diag else k_dim, tn), BF)
        o_spec = pl.BlockSpec((1, tk, tn), lambda k, n, s: (n, k, 0))
    else:
        out_shape = _sds((k_dim, n_dim), BF)
        o_spec = pl.BlockSpec((tk, tn), lambda k, n, s: (k, n))
    return pl.pallas_call(
        body, name=name, grid=(nk, nn, ns),
        in_specs=[a_spec, pl.BlockSpec((ts, tn), lambda k, n, s: (s, n))], out_specs=o_spec, out_shape=out_shape,
        scratch_shapes=[pltpu.VMEM((tk, tn), F32)], compiler_params=_params(3))(a, b)


_WEIGHTS = ['g_mix_pre', 'w_in', 'w_sgu_s', 'b_sgu_s', 'g_sgu_v', 'b_sgu_v', 'w_sgu_out', 'w_dw', 'b_dw', 'g_conv_ln', 'b_conv_ln',
            'w_conv_out', 'w_pool', 's_pool', 'w_pool_out', 'w_out', 'g_mix_post', 'g_ffn_pre', 'w_ffn_in', 'w_ffn_out', 'g_ffn_post',
            'w_ple', 'w_ple_gate']
_SHARDED = ['w_in', 'w_sgu_out', 'w_conv_out', 'w_pool', 'w_pool_out', 'w_out', 'w_ffn_in', 'w_ffn_out', 'w_ple', 'w_ple_gate']
_VECTORS = ['g_mix_pre', 'g_sgu_v', 'b_sgu_v', 'b_dw', 'g_conv_ln', 'b_conv_ln', 's_pool', 'g_mix_post', 'g_ffn_pre', 'g_ffn_post']
_SUBLANES = 8
_REP_PAD = _SUBLANES * (len(_VECTORS) + 1) + 2 * HEADS * BLK * BLK // D
_DW_ROWS = 2 * HALO


def _pack_replicated(t):
    rows = [jnp.pad(t[k].reshape(2, D), ((0, _SUBLANES - 2), (0, 0))) for k in _VECTORS + ['b_sgu_s']]
    return jnp.concatenate(rows + [t['w_sgu_s'].reshape(-1, D)], axis=0)


def _unpack_replicated(packed):
    out = {}
    for i, k in enumerate(_VECTORS):
        out[k] = packed[_SUBLANES * i:_SUBLANES * i + 2]
    o = _SUBLANES * len(_VECTORS)
    out['b_sgu_s'] = packed[o:o + 2].reshape(2, HEADS, BLK)
    out['w_sgu_s'] = packed[o + _SUBLANES:].reshape(2, HEADS, BLK, BLK)
    return out


def _pad_taps(w):
    return jnp.pad(w, ((0, HALO - CONV_W), (0, 0)))


def kernel(x, p, g_mix_pre, w_in, w_sgu_s, b_sgu_s, g_sgu_v, b_sgu_v, w_sgu_out, w_dw, b_dw, g_conv_ln, b_conv_ln, w_conv_out, w_pool, s_pool, w_pool_out, w_out, g_mix_post, g_ffn_pre, w_ffn_in, w_ffn_out, g_ffn_post, w_ple, w_ple_gate, loss_target, m_g_mix_pre, m_w_in, m_w_sgu_s, m_b_sgu_s, m_g_sgu_v, m_b_sgu_v, m_w_sgu_out, m_w_dw, m_b_dw, m_g_conv_ln, m_b_conv_ln, m_w_conv_out, m_w_pool, m_s_pool, m_w_pool_out, m_w_out, m_g_mix_post, m_g_ffn_pre, m_w_ffn_in, m_w_ffn_out, m_g_ffn_post, m_w_ple, m_w_ple_gate, v_g_mix_pre, v_w_in, v_w_sgu_s, v_b_sgu_s, v_g_sgu_v, v_b_sgu_v, v_w_sgu_out, v_w_dw, v_b_dw, v_g_conv_ln, v_b_conv_ln, v_w_conv_out, v_w_pool, v_s_pool, v_w_pool_out, v_w_out, v_g_mix_post, v_g_ffn_pre, v_w_ffn_in, v_w_ffn_out, v_g_ffn_post, v_w_ple, v_w_ple_gate):
    W = dict(g_mix_pre=g_mix_pre, w_in=w_in, w_sgu_s=w_sgu_s, b_sgu_s=b_sgu_s, g_sgu_v=g_sgu_v, b_sgu_v=b_sgu_v, w_sgu_out=w_sgu_out,
             w_dw=w_dw, b_dw=b_dw, g_conv_ln=g_conv_ln, b_conv_ln=b_conv_ln, w_conv_out=w_conv_out, w_pool=w_pool, s_pool=s_pool,
             w_pool_out=w_pool_out, w_out=w_out, g_mix_post=g_mix_post, g_ffn_pre=g_ffn_pre, w_ffn_in=w_ffn_in, w_ffn_out=w_ffn_out,
             g_ffn_post=g_ffn_post, w_ple=w_ple, w_ple_gate=w_ple_gate)
    M = dict(g_mix_pre=m_g_mix_pre, w_in=m_w_in, w_sgu_s=m_w_sgu_s, b_sgu_s=m_b_sgu_s, g_sgu_v=m_g_sgu_v, b_sgu_v=m_b_sgu_v,
             w_sgu_out=m_w_sgu_out, w_dw=m_w_dw, b_dw=m_b_dw, g_conv_ln=m_g_conv_ln, b_conv_ln=m_b_conv_ln, w_conv_out=m_w_conv_out,
             w_pool=m_w_pool, s_pool=m_s_pool, w_pool_out=m_w_pool_out, w_out=m_w_out, g_mix_post=m_g_mix_post, g_ffn_pre=m_g_ffn_pre,
             w_ffn_in=m_w_ffn_in, w_ffn_out=m_w_ffn_out, g_ffn_post=m_g_ffn_post, w_ple=m_w_ple, w_ple_gate=m_w_ple_gate)
    V = dict(g_mix_pre=v_g_mix_pre, w_in=v_w_in, w_sgu_s=v_w_sgu_s, b_sgu_s=v_b_sgu_s, g_sgu_v=v_g_sgu_v, b_sgu_v=v_b_sgu_v,
             w_sgu_out=v_w_sgu_out, w_dw=v_w_dw, b_dw=v_b_dw, g_conv_ln=v_g_conv_ln, b_conv_ln=v_b_conv_ln, w_conv_out=v_w_conv_out,
             w_pool=v_w_pool, s_pool=v_s_pool, w_pool_out=v_w_pool_out, w_out=v_w_out, g_mix_post=v_g_mix_post, g_ffn_pre=v_g_ffn_pre,
             w_ffn_in=v_w_ffn_in, w_ffn_out=v_w_ffn_out, g_ffn_post=v_g_ffn_post, w_ple=v_w_ple, w_ple_gate=v_w_ple_gate)

    my_c = lax.axis_index("c")
    my_chip = 2 * lax.axis_index("x") + lax.axis_index("y")
    my_dev = 2 * my_chip + my_c
    s_len = x.shape[1]
    h0 = x.reshape(s_len, D)
    target = loss_target.reshape(s_len, D)

    gathered = all_gather([W[k].astype(BF) for k in _SHARDED] + [w_dw], "all_gather_weights")
    G = dict(zip(_SHARDED + ['w_dw'], gathered))
    wfi = G['w_ffn_in'].reshape(4, 2, 2, D, D_FF // 4)
    wfi = jnp.transpose(wfi, (2, 0, 3, 1, 4)).reshape(2, 4, D, D_FF // 2)
    wpool_g = jnp.transpose(G['w_pool'], (1, 2, 0, 3, 4)).reshape(2, 4, POOL_GD, POOL_GD)
    wple_g = jnp.transpose(G['w_ple'], (1, 2, 0, 3)).reshape(2, PLE, D)
    wdw_g = jnp.transpose(G['w_dw'].reshape(N_DEV, 2, CONV_W, BLK), (1, 2, 0, 3)).reshape(2, CONV_W, D)

    def vec(name, layer):
        return W[name][layer].reshape(1, D)

    saved = []
    h = h0
    for l in range(2):
        sv = {'h': h}
        wdw_l = _pad_taps(wdw_g[l])
        bsfull = jnp.repeat(b_sgu_s[l].T, BLK, axis=1)
        wst = jnp.swapaxes(w_sgu_s[l], 1, 2)
        sv.update(wdw=wdw_l, bsfull=bsfull, wst=wst)
        proj, hn = norm_proj(h, vec('g_mix_pre', l), G['w_in'],
                             functools.partial(lambda ll: pl.BlockSpec((None, None, D, D), lambda j, i: (j, ll, 0, 0)), l),
                             N_DEV, D, f"norm_proj_in_{l}")
        sgu, bra = sgu_fwd(proj, w_sgu_s[l], bsfull, vec('g_sgu_v', l), vec('b_sgu_v', l), G['w_sgu_out'], l, f"sgu_fwd_{l}")
        conv, cb, brb = conv_fwd(proj, wdw_l, vec('b_dw', l), vec('g_conv_ln', l), vec('b_conv_ln', l), G['w_conv_out'], l, f"conv_fwd_{l}")
        pooled, pm, brc = pool_fwd(proj, wpool_g[l], vec('s_pool', l), G['w_pool_out'], l, f"pool_fwd_{l}")
        merged, mo, h1 = merge_out(proj, bra, brb, brc, h, G['w_out'], vec('g_mix_post', l), l, f"merge_out_{l}")
        ff, hn2 = norm_proj(h1, vec('g_ffn_pre', l), wfi[l],
                            lambda: pl.BlockSpec((None, D, D_FF // 2), lambda j, i: (j, 0, 0)), 4, D_FF // 2, f"norm_proj_ffn_{l}")
        act, f, h2, pg, h3 = ffn_out(ff, h1, p, G['w_ffn_out'], vec('g_ffn_post', l), G['w_ple_gate'], wple_g[l], l, f"ffn_out_{l}")
        sv.update(proj=proj, hn=hn, sgu=sgu, bra=bra, conv=conv, cb=cb, brb=brb, pooled=pooled, pm=pm, brc=brc, merged=merged, mo=mo, h1=h1,
                  ff=ff, hn2=hn2, act=act, f=f, h2=h2, pg=pg)
        saved.append(sv)
        h = h3

    dh, loss_row = loss_grad(h, target, "loss_grad")
    loss = lax.psum(loss_row[0, 0], ("x", "y", "c"))

    parts = {k: [None, None] for k in _SHARDED}
    small = {k: [None, None] for k in _VECTORS + ['b_sgu_s', 'w_sgu_s', 'w_dw']}
    for l in (1, 0):
        sv = saved[l]
        dh2, dpe, dpg, df, dff, small['g_ffn_post'][l] = ple_ffn_bwd(
            dh, sv['pg'], p, sv['f'], sv['ff'], wple_g[l], G['w_ple_gate'], G['w_ffn_out'], vec('g_ffn_post', l), l, f"ple_ffn_bwd_{l}")
        dh1, small['g_ffn_pre'][l] = ffn_in_bwd(dff, wfi[l], sv['h1'], dh2, vec('g_ffn_pre', l), f"ffn_in_bwd_{l}")
        dmo, dbra, dbrb, dbrc, dzg, dhn_g, small['g_mix_post'][l] = mix_post_bwd(
            dh1, sv['mo'], vec('g_mix_post', l), G['w_out'], sv['proj'], sv['bra'], sv['brb'], sv['brc'], G['w_in'], l, f"mix_post_bwd_{l}")
        dzs, dhn_a, dws, dbs, small['g_sgu_v'][l], small['b_sgu_v'][l] = sgu_bwd(
            dbra, G['w_sgu_out'], sv['proj'], w_sgu_s[l], sv['wst'], sv['bsfull'], vec('g_sgu_v', l), vec('b_sgu_v', l), G['w_in'], l,
            f"sgu_bwd_{l}")
        small['w_sgu_s'][l] = dws
        small['b_sgu_s'][l] = dbs.T
        dc, dwdw, small['b_dw'][l], small['g_conv_ln'][l], small['b_conv_ln'][l] = conv_bwd(
            dbrb, G['w_conv_out'], sv['proj'], sv['conv'], vec('g_conv_ln', l), vec('b_conv_ln', l), l, f"conv_bwd_{l}")
        small['w_dw'][l] = dwdw
        dmr, q, small['s_pool'][l] = pool_bwd(dbrc, G['w_pool_out'], sv['pooled'], wpool_g[l], vec('s_pool', l), l, f"pool_bwd_{l}")
        dzc, dhn_c = seq_bwd(dc, q, sv['proj'], sv['wdw'], G['w_in'], l, f"seq_bwd_{l}")
        dh, small['g_mix_pre'][l] = mix_pre_bwd(sv['h'], dh1, dhn_a, dhn_c, dhn_g, vec('g_mix_pre', l), f"mix_pre_bwd_{l}")

        p_spec = pl.BlockSpec((None, None, min(TS, s_len), PLE), functools.partial(lambda k, n, s, ll: (ll, 0, s, 0), ll=l))
        g_ple = wgrad(p, dpe, PLE, D, f"wgrad_ple_{l}", a_spec=p_spec)
        parts['w_ple'][l] = jnp.transpose(g_ple.reshape(PLE, N_DEV, BLK), (1, 0, 2))
        parts['w_ple_gate'][l] = wgrad(sv['h2'], dpg, D, D, f"wgrad_ple_gate_{l}").reshape(N_DEV, BLK, D)
        parts['w_ffn_out'][l] = wgrad(sv['act'], df, D_FF // 2, D, f"wgrad_ffn_out_{l}").reshape(N_DEV, D_FF // N_DEV, D)
        g_fi = wgrad(sv['hn2'], dff, D, D_FF // 2, f"wgrad_ffn_in_{l}", stacked=True)
        parts['w_ffn_in'][l] = jnp.transpose(g_fi.reshape(4, D, 2, D_FF // 4), (0, 2, 1, 3)).reshape(N_DEV, D, D_FF // 4)
        parts['w_out'][l] = wgrad(sv['merged'], dmo, D, D, f"wgrad_out_{l}").reshape(N_DEV, BLK, D)
        parts['w_sgu_out'][l] = wgrad(sv['sgu'], dbra, D, D, f"wgrad_sgu_out_{l}").reshape(N_DEV, BLK, D)
        parts['w_conv_out'][l] = wgrad(sv['cb'], dbrb, D, D, f"wgrad_conv_out_{l}").reshape(N_DEV, BLK, D)
        parts['w_pool_out'][l] = wgrad(sv['pm'], dbrc, D, D, f"wgrad_pool_out_{l}").reshape(N_DEV, BLK, D)
        g_pool = wgrad(sv['pooled'], dmr, POOL_GD, POOL_GD, f"wgrad_pool_{l}", diag=True)
        parts['w_pool'][l] = jnp.transpose(g_pool.reshape(4, N_DEV, POOL_GD // N_DEV, POOL_GD), (1, 0, 2, 3))
        parts['w_in'][l] = jnp.concatenate([
            wgrad(sv['hn'], dzs, D, D, f"wgrad_in_sgu_{l}", stacked=True),
            wgrad(sv['hn'], dzc, D, D, f"wgrad_in_seq_{l}", stacked=True),
            wgrad(sv['hn'], dzg, D, D, f"wgrad_in_gate_{l}", stacked=True)], axis=0)
    grad_x = dh.reshape(1, s_len, D)

    stacked = [jnp.stack(parts[k], axis=1) for k in _SHARDED]
    from_sibling = rs_to_sibling(stacked, "rs_to_sibling")
    chip_parts = []
    for k, st, rv in zip(_SHARDED, stacked, from_sibling):
        mine = lax.dynamic_index_in_dim(st.reshape((4, 2) + st.shape[1:]), my_c, axis=1, keepdims=False)
        cols = st.shape[-1]
        chip_parts.append(add_bf16(mine.reshape(-1, cols), rv.reshape(-1, cols), f"rs_add_{k}").reshape(rv.shape))
    from_chips = rs_to_chips(chip_parts, "rs_to_chips")

    outs = {}
    for k, cp, rv in zip(_SHARDED, chip_parts, from_chips):
        cols = cp.shape[-1]
        own = lax.dynamic_index_in_dim(cp, my_chip, axis=0, keepdims=False).reshape(-1, cols)
        rows = own.shape[0]
        rv3 = rv.reshape(3, rows, cols)
        res = adamw(W[k].reshape(rows, cols), M[k].reshape(rows, cols), V[k].reshape(rows, cols),
                    [(own, None), (rv3, 0), (rv3, 1), (rv3, 2)], f"adamw_{k}")
        outs[k] = [r.reshape(W[k].shape) for r in res]

    small_full = {k: jnp.stack(small[k], axis=0) for k in small}
    rep_grads = _pack_replicated(small_full)
    dw_grads = small_full['w_dw'].reshape(_DW_ROWS, D)
    gathered_small = all_gather([jnp.concatenate([rep_grads, dw_grads], axis=0)], "all_gather_small_grads")[0]
    rep_part = gathered_small[:, :_REP_PAD, :]
    rep_res = adamw(_pack_replicated(W), _pack_replicated(M), _pack_replicated(V), [(rep_part, d) for d in range(N_DEV)],
                    "adamw_replicated")
    for idx, packed in enumerate(rep_res):
        for name, val in _unpack_replicated(packed).items():
            outs.setdefault(name, [None] * 4)[idx] = val
    dw_sum = sum_slabs(gathered_small[:, _REP_PAD:, :], "sum_w_dw")
    dw_mine = lax.dynamic_slice_in_dim(dw_sum.reshape(2, HALO, D)[:, :CONV_W], my_dev * BLK, BLK, axis=2)
    res = adamw(w_dw.reshape(2 * CONV_W, BLK), m_w_dw.reshape(2 * CONV_W, BLK), v_w_dw.reshape(2 * CONV_W, BLK),
                [(dw_mine.reshape(2 * CONV_W, BLK), None)], "adamw_w_dw")
    outs['w_dw'] = [r.reshape(w_dw.shape) for r in res]

    result = [loss, grad_x]
    for idx in range(4):
        result += [outs[k][idx] for k in _WEIGHTS]
    return tuple(result)
```

```python
import functools
import math

import jax
import jax.numpy as jnp
from jax import lax
from jax.experimental import pallas as pl
from jax.experimental.pallas import tpu as pltpu

F32 = jnp.float32
BF = jnp.bfloat16

D = 1024
D_FF = 2816
PLE = 256
N_DEV = 8
HEADS = 8
BLK = 128
CHUNK = 64
CONV_W = 31
POOL_WINDOWS = (2, 4, 8, 16)
POOL_GD = 256
EPS = 1e-6

V7X_VMEM_BYTES = 64 * 2**20
VMEM_LIMIT = V7X_VMEM_BYTES * 7 // 8
HALO = 32
RC = 64
LC = 128
TM = 512
TMB = 256
TMP = 1024
TS = 2048

ADAM_LR, ADAM_B1, ADAM_B2, ADAM_EPS, ADAM_WD, ADAM_STEP = 0.001, 0.9, 0.999, 1e-08, 0.01, 10

MESH = pl.DeviceIdType.MESH
ANY = pl.BlockSpec(memory_space=pl.ANY)

_GELU_K0 = math.sqrt(2.0 / math.pi)
_GELU_K1 = 0.044715


def _dot(a, b):
    return jnp.dot(a, b, preferred_element_type=F32)


def _dot_nt(a, b):
    return lax.dot_general(a, b, (((1,), (1,)), ((), ())), preferred_element_type=F32)


def _dot_tn(a, b):
    return lax.dot_general(a, b, (((0,), (0,)), ((), ())), preferred_element_type=F32)


def _sig(x):
    return 1.0 / (1.0 + jnp.exp(-x))


def _gelu(x):
    t = jnp.tanh(_GELU_K0 * (x + _GELU_K1 * x * x * x))
    return 0.5 * x * (1.0 + t), t


def _gelu_grad(x, t):
    return 0.5 * (1.0 + t) + 0.5 * x * (1.0 - t * t) * _GELU_K0 * (1.0 + 3.0 * _GELU_K1 * x * x)


def _rstd(x):
    return lax.rsqrt(jnp.mean(x * x, axis=-1, keepdims=True) + EPS)


def _rms_bwd(x, gd, r):
    return r * gd - x * (r * r * r) * jnp.mean(gd * x, axis=-1, keepdims=True)


def _ln_fwd(x):
    mu = jnp.mean(x, axis=-1, keepdims=True)
    xc = x - mu
    rs = lax.rsqrt(jnp.mean(xc * xc, axis=-1, keepdims=True) + EPS)
    return xc * rs, rs


def _ln_bwd(dhat, hat, rs):
    return rs * (dhat - jnp.mean(dhat, axis=-1, keepdims=True) - hat * jnp.mean(dhat * hat, axis=-1, keepdims=True))


def _colsum(x):
    return jnp.sum(x, axis=0, keepdims=True)


def _accum(ref, first, val):
    @pl.when(first)
    def _():
        ref[...] = val

    @pl.when(jnp.logical_not(first))
    def _():
        ref[...] += val


def _sgu_mask(transposed):
    r = lax.broadcasted_iota(jnp.int32, (BLK, BLK), 0) // CHUNK
    c = lax.broadcasted_iota(jnp.int32, (BLK, BLK), 1) // CHUNK
    return (r <= c) if transposed else (c <= r)


def _inv_count(i, tm, w):
    t = lax.broadcasted_iota(jnp.int32, (tm, 1), 0) + i * tm
    return 1.0 / jnp.minimum(t + 1, w).astype(F32)


def _count(i, tm, w):
    t = lax.broadcasted_iota(jnp.int32, (tm, 1), 0) + i * tm
    return jnp.minimum(t + 1, w).astype(F32)


def _params(n_grid):
    return pltpu.CompilerParams(dimension_semantics=("arbitrary",) * n_grid, vmem_limit_bytes=VMEM_LIMIT)


def _sds(shape, dtype):
    return jax.ShapeDtypeStruct(shape, dtype)


def _cols(tm, width, cb):
    return pl.BlockSpec((tm, width), lambda i: (i, cb))


def _whole(shape):
    nd = len(shape)
    return pl.BlockSpec(shape, lambda i: (0,) * nd)


def _prev_halo(tm, width, cb):
    return pl.BlockSpec((HALO, width), lambda i: (jnp.maximum(i * (tm // HALO) - 1, 0), cb))


def _next_halo(tm, width, cb, s_len):
    last = s_len // HALO - 1
    return pl.BlockSpec((HALO, width), lambda i: (jnp.minimum((i + 1) * (tm // HALO), last), cb))


def _rowsharded(rows):
    return pl.BlockSpec((N_DEV, rows, D), lambda i: (0, 0, 0))


def _win_block(j):
    return pl.BlockSpec((None, D, D), lambda i: (j, 0, 0))


def _row_tile(rows, cap):
    t = min(rows, cap)
    while rows % t or t % 16:
        t -= 16
    return t


class Gather:
    def __init__(self, arrs):
        self.arrs = list(arrs)
        n = self.n = len(self.arrs)
        self.out_shape = [_sds((N_DEV,) + a.shape, a.dtype) for a in self.arrs]
        self.scratch = [pltpu.SemaphoreType.DMA((n, 7)), pltpu.SemaphoreType.DMA((n, 7)), pltpu.SemaphoreType.DMA((n,))]

    def _plan(self, ins, outs, sems):
        send, recv, local = sems
        x, y, c = lax.axis_index("x"), lax.axis_index("y"), lax.axis_index("c")
        me, sibling = (x, y, c), (x, y, 1 - c)
        chips = [(1 - x, y), (x, 1 - y), (1 - x, 1 - y)]

        def copy(a, k, block, to, src=None):
            dst = outs[a].at[4 * block[0] + 2 * block[1] + block[2]]
            return pltpu.make_async_remote_copy(
                src_ref=dst if src is None else src, dst_ref=dst, send_sem=send.at[a, k], recv_sem=recv.at[a, k],
                device_id=to, device_id_type=MESH)

        mine = [pltpu.make_async_copy(ins[a], outs[a].at[4 * x + 2 * y + c], local.at[a]) for a in range(self.n)]
        first = []
        for a in range(self.n):
            first.append(copy(a, 0, me, sibling, src=ins[a]))
            first += [copy(a, 1 + j, me, (*chip, c), src=ins[a]) for j, chip in enumerate(chips)]
        return me, sibling, chips, c, copy, mine, first

    def start(self, ins, outs, sems):
        *_, mine, first = self._plan(ins, outs, sems)
        for cp in mine + first:
            cp.start()

    def finish(self, ins, outs, sems):
        me, sibling, chips, c, copy, mine, first = self._plan(ins, outs, sems)
        passed = []
        for a in range(self.n):
            for j, chip in enumerate(chips):
                copy(a, 1 + j, (*chip, c), me).wait_recv()
                fwd = copy(a, 4 + j, (*chip, c), sibling)
                fwd.start()
                passed.append(fwd)
        for a in range(self.n):
            copy(a, 0, sibling, me).wait_recv()
            for j, chip in enumerate(chips):
                copy(a, 4 + j, (*chip, 1 - c), me).wait_recv()
        for cp in first + passed:
            cp.wait_send()
        for cp in mine:
            cp.wait()


class ToSibling:
    def __init__(self, parts):
        self.arrs = list(parts)
        n = self.n = len(self.arrs)
        self.out_shape = [_sds((4,) + p.shape[1:], p.dtype) for p in self.arrs]
        self.scratch = [pltpu.SemaphoreType.DMA((n,)), pltpu.SemaphoreType.DMA((n,))]

    def start(self, ins, outs, sems):
        send, recv = sems
        x, y, c = lax.axis_index("x"), lax.axis_index("y"), lax.axis_index("c")
        for a in range(self.n):
            for q in range(4):
                pltpu.make_async_remote_copy(
                    src_ref=ins[a].at[2 * q + 1 - c], dst_ref=outs[a].at[q], send_sem=send.at[a], recv_sem=recv.at[a],
                    device_id=(x, y, 1 - c), device_id_type=MESH).start()

    def finish(self, ins, outs, sems):
        send, recv = sems
        x, y, c = lax.axis_index("x"), lax.axis_index("y"), lax.axis_index("c")
        for a in range(self.n):
            pltpu.make_async_remote_copy(
                src_ref=outs[a], dst_ref=outs[a], send_sem=send.at[a], recv_sem=recv.at[a],
                device_id=(x, y, 1 - c), device_id_type=MESH).wait()


class ToChips:
    def __init__(self, cps):
        self.arrs = list(cps)
        n = self.n = len(self.arrs)
        self.out_shape = [_sds((3,) + p.shape[1:], p.dtype) for p in self.arrs]
        self.scratch = [pltpu.SemaphoreType.DMA((n,)), pltpu.SemaphoreType.DMA((n,))]

    def start(self, ins, outs, sems):
        send, recv = sems
        x, y, c = lax.axis_index("x"), lax.axis_index("y"), lax.axis_index("c")
        for a in range(self.n):
            for r, (px, py) in enumerate([(1 - x, y), (x, 1 - y), (1 - x, 1 - y)]):
                pltpu.make_async_remote_copy(
                    src_ref=ins[a].at[2 * px + py], dst_ref=outs[a].at[r], send_sem=send.at[a], recv_sem=recv.at[a],
                    device_id=(px, py, c), device_id_type=MESH).start()

    def finish(self, ins, outs, sems):
        send, recv = sems
        x, y, c = lax.axis_index("x"), lax.axis_index("y"), lax.axis_index("c")
        for a in range(self.n):
            pltpu.make_async_remote_copy(
                src_ref=outs[a], dst_ref=outs[a], send_sem=send.at[a], recv_sem=recv.at[a],
                device_id=(x, y, c), device_id_type=MESH).wait()


def run_comm(comm, name):
    n = comm.n

    def body(*refs):
        ins, outs, sems = refs[:n], refs[n:2 * n], refs[2 * n:]
        comm.start(ins, outs, sems)
        comm.finish(ins, outs, sems)

    return pl.pallas_call(body, name=name, out_shape=comm.out_shape, in_specs=[ANY] * n, out_specs=[ANY] * n,
                          scratch_shapes=comm.scratch)(*comm.arrs)


def _pcall(body, args, *, name, grid, in_specs, out_specs, out_shape, scratch_shapes=(), comm=None):
    params = _params(len(grid))
    scratch_shapes = list(scratch_shapes)
    if comm is None:
        outs = pl.pallas_call(body, name=name, grid=grid, in_specs=in_specs, out_specs=out_specs, out_shape=out_shape,
                              scratch_shapes=scratch_shapes, compiler_params=params)(*args)
        return outs, None
    n_in, n_out, n_scr, nc = len(in_specs), len(out_specs), len(scratch_shapes), comm.n

    def hosted(*refs):
        ins, cins = refs[:n_in], refs[n_in:n_in + nc]
        o0 = n_in + nc
        outs, couts = refs[o0:o0 + n_out], refs[o0 + n_out:o0 + n_out + nc]
        s0 = o0 + n_out + nc
        scr, csems = refs[s0:s0 + n_scr], refs[s0 + n_scr:]
        first = pl.program_id(0) == 0
        last = pl.program_id(0) == grid[0] - 1
        for ax in range(1, len(grid)):
            first = jnp.logical_and(first, pl.program_id(ax) == 0)
            last = jnp.logical_and(last, pl.program_id(ax) == grid[ax] - 1)

        @pl.when(first)
        def _():
            comm.start(cins, couts, csems)
        body(*ins, *outs, *scr)

        @pl.when(last)
        def _():
            comm.finish(cins, couts, csems)

    res = pl.pallas_call(
        hosted, name=name, grid=grid, in_specs=list(in_specs) + [ANY] * nc, out_specs=list(out_specs) + [ANY] * nc,
        out_shape=list(out_shape) + comm.out_shape, scratch_shapes=scratch_shapes + comm.scratch,
        compiler_params=params)(*args, *comm.arrs)
    return res[:n_out], res[n_out:]


def add_bf16(a, b, name):
    rows, cols = a.shape
    tr = _row_tile(rows, 512)

    def body(a_ref, b_ref, o_ref):
        o_ref[...] = (a_ref[...].astype(F32) + b_ref[...].astype(F32)).astype(o_ref.dtype)

    spec = pl.BlockSpec((tr, cols), lambda i: (i, 0))
    return pl.pallas_call(body, name=name, grid=(rows // tr,), in_specs=[spec, spec], out_specs=spec,
                          out_shape=_sds(a.shape, BF), compiler_params=_params(1))(a, b)


def adamw(w, m, v, pieces, name):
    rows, cols = w.shape
    tr = _row_tile(rows, 256) if rows % 16 == 0 else rows
    np_ = len(pieces)
    c1 = 1.0 / (1.0 - ADAM_B1 ** ADAM_STEP)
    c2 = 1.0 / (1.0 - ADAM_B2 ** ADAM_STEP)

    def body(*refs):
        w_ref, m_ref, v_ref = refs[:3]
        p_refs = refs[3:3 + np_]
        g_ref, d_ref, nm_ref, nv_ref = refs[3 + np_:]
        g = p_refs[0][...].astype(F32)
        for pr in p_refs[1:]:
            g = g + pr[...].astype(F32)
        nm = ADAM_B1 * m_ref[...] + (1.0 - ADAM_B1) * g
        nv = ADAM_B2 * v_ref[...] + (1.0 - ADAM_B2) * (g * g)
        g_ref[...] = g
        nm_ref[...] = nm
        nv_ref[...] = nv
        d_ref[...] = -ADAM_LR * ((nm * c1) / (jnp.sqrt(nv * c2) + ADAM_EPS) + ADAM_WD * w_ref[...])

    spec = pl.BlockSpec((tr, cols), lambda i: (i, 0))
    p_specs = []
    for arr, k in pieces:
        if k is None:
            p_specs.append(spec)
        else:
            p_specs.append(pl.BlockSpec((None, tr, cols), functools.partial(lambda i, kk: (kk, i, 0), kk=k)))
    out = _sds(w.shape, F32)
    return pl.pallas_call(body, name=name, grid=(rows // tr,), in_specs=[spec] * 3 + p_specs, out_specs=[spec] * 4,
                          out_shape=[out] * 4, compiler_params=_params(1))(w, m, v, *[a for a, _ in pieces])


def adamw_layers(w, m, v, pieces, name):
    _, rows, cols = w.shape
    tr = _row_tile(rows, 256)
    nt = rows // tr
    counts = [len(pieces[0]), len(pieces[1])]
    c1 = 1.0 / (1.0 - ADAM_B1 ** ADAM_STEP)
    c2 = 1.0 / (1.0 - ADAM_B2 ** ADAM_STEP)

    def body(*refs):
        w_ref, m_ref, v_ref = refs[:3]
        p_refs = refs[3:3 + sum(counts)]
        g_ref, d_ref, nm_ref, nv_ref = refs[3 + sum(counts):]
        sums = []
        for group in (p_refs[:counts[0]], p_refs[counts[0]:]):
            s = group[0][...].astype(F32)
            for pr in group[1:]:
                s = s + pr[...].astype(F32)
            sums.append(s)
        g = jnp.where(pl.program_id(0) == 0, sums[0], sums[1])
        nm = ADAM_B1 * m_ref[...] + (1.0 - ADAM_B1) * g
        nv = ADAM_B2 * v_ref[...] + (1.0 - ADAM_B2) * (g * g)
        g_ref[...] = g
        nm_ref[...] = nm
        nv_ref[...] = nv
        d_ref[...] = -ADAM_LR * ((nm * c1) / (jnp.sqrt(nv * c2) + ADAM_EPS) + ADAM_WD * w_ref[...])

    def rows_of(layer):
        parked = nt - 1 if layer == 0 else 0
        return lambda l, i: jnp.where(l == layer, i, parked)

    spec = pl.BlockSpec((None, tr, cols), lambda l, i: (l, i, 0))
    p_specs, p_args = [], []
    for layer in (0, 1):
        row_of = rows_of(layer)
        for arr, k in pieces[layer]:
            p_args.append(arr)
            if k is None:
                p_specs.append(pl.BlockSpec((tr, cols), functools.partial(lambda l, i, f: (f(l, i), 0), f=row_of)))
            else:
                p_specs.append(pl.BlockSpec((None, tr, cols), functools.partial(lambda l, i, f, kk: (kk, f(l, i), 0), f=row_of, kk=k)))
    out = _sds(w.shape, F32)
    return pl.pallas_call(body, name=name, grid=(2, nt), in_specs=[spec] * 3 + p_specs, out_specs=[spec] * 4,
                          out_shape=[out] * 4, compiler_params=_params(2))(w, m, v, *p_args)


def sum_slabs(g, name):
    n, rows, cols = g.shape

    def body(g_ref, o_ref):
        s = g_ref[0]
        for k in range(1, n):
            s = s + g_ref[k]
        o_ref[...] = s

    return pl.pallas_call(body, name=name, out_shape=_sds((rows, cols), F32))(g)


def norm_proj(h, g, w, name, comm=None):
    s_len = h.shape[0]
    nb, _, tn = w.shape
    tm = min(TMP, s_len)
    nt = s_len // tm

    def body(h_ref, g_ref, w_ref, o_ref, hn_ref, hn_s):
        rows = pl.ds(pl.multiple_of(pl.program_id(1) * tm, tm), tm)

        @pl.when(pl.program_id(0) == 0)
        def _():
            x = h_ref[...]
            hn = (x * _rstd(x) * g_ref[...]).astype(BF)
            hn_s[rows, :] = hn
            hn_ref[...] = hn
        o_ref[...] = _dot(hn_s[rows, :], w_ref[...]).astype(BF)

    def first_pass_rows(j, i):
        return (jnp.where(j == 0, i, nt - 1), 0)

    return _pcall(
        body, (h, g, w), name=name, grid=(nb, nt),
        in_specs=[pl.BlockSpec((tm, D), first_pass_rows), pl.BlockSpec((1, D), lambda j, i: (0, 0)),
                  pl.BlockSpec((None, D, tn), lambda j, i: (j, 0, 0))],
        out_specs=[pl.BlockSpec((tm, tn), lambda j, i: (i, j)), pl.BlockSpec((tm, D), first_pass_rows)],
        out_shape=[_sds((s_len, nb * tn), BF), _sds((s_len, D), BF)],
        scratch_shapes=[pltpu.VMEM((s_len, D), BF)], comm=comm)


def _sgu_mix(ws_ref, vln_s, mix_s, bs_ref, tm, transposed):
    mask = _sgu_mask(transposed)
    for hd in range(HEADS):
        wm = jnp.where(mask, ws_ref[hd], 0.0).astype(BF)
        cs = slice(hd * BLK, (hd + 1) * BLK)
        for n in range(tm // BLK):
            rs = slice(n * BLK, (n + 1) * BLK)
            r = _dot(wm, vln_s[rs, cs])
            mix_s[rs, cs] = r if bs_ref is None else r + bs_ref[:, cs]


def sgu_fwd(proj, ws, bsfull, gv, bv, wo, name):
    s_len = proj.shape[0]
    tm = min(TM, s_len)

    def body(zu_ref, zv_ref, ws_ref, bs_ref, gv_ref, bv_ref, wo_ref, sgu_ref, br_ref, vln_s, mix_s):
        u, _ = _gelu(zu_ref[...].astype(F32))
        v, _ = _gelu(zv_ref[...].astype(F32))
        vhat, _ = _ln_fwd(v)
        vln_s[...] = (vhat * gv_ref[...] + bv_ref[...]).astype(BF)
        _sgu_mix(ws_ref, vln_s, mix_s, bs_ref, tm, False)
        sgu = (u * mix_s[...]).astype(BF)
        sgu_ref[...] = sgu
        br_ref[...] = _dot(sgu, wo_ref[...].reshape(D, D)).astype(BF)

    return pl.pallas_call(
        body, name=name, grid=(s_len // tm,),
        in_specs=[_cols(tm, D, 0), _cols(tm, D, 1), _whole((HEADS, BLK, BLK)), _whole((BLK, D)), _whole((1, D)), _whole((1, D)),
                  _rowsharded(BLK)],
        out_specs=[_cols(tm, D, 0)] * 2, out_shape=[_sds((s_len, D), BF)] * 2,
        scratch_shapes=[pltpu.VMEM((tm, D), BF), pltpu.VMEM((tm, D), F32)], compiler_params=_params(1),
    )(proj, proj, ws, bsfull, gv, bv, wo)


def _causal_conv(ext_s, out_s, wdw_ref, bias_ref, tm):
    def chunk(ci, carry):
        r0 = pl.multiple_of((ci // (D // LC)) * RC, RC)
        l0 = pl.multiple_of((ci % (D // LC)) * LC, LC)
        win = ext_s[pl.ds(r0, RC + HALO), pl.ds(l0, LC)]
        acc = jnp.broadcast_to(bias_ref[:, pl.ds(l0, LC)], (RC, LC))
        for r in range(8):
            wr = win if r == 0 else pltpu.roll(win, r, 0)
            for m in range(4):
                d = 8 * m + r
                if d < CONV_W:
                    k = CONV_W - 1 - d
                    acc = acc + wdw_ref[k:k + 1, pl.ds(l0, LC)] * wr[HALO - 8 * m:HALO - 8 * m + RC]
        out_s[pl.ds(r0, RC), pl.ds(l0, LC)] = acc
        return carry
    lax.fori_loop(0, (tm // RC) * (D // LC), chunk, 0)


def _glu_ext(a_ref, g_ref, ah_ref, gh_ref, ext_s, first):
    hh = ah_ref[...].astype(F32) * _sig(gh_ref[...].astype(F32))
    ext_s[0:HALO, :] = jnp.where(first, 0.0, hh)
    ext_s[HALO:, :] = a_ref[...].astype(F32) * _sig(g_ref[...].astype(F32))


def conv_fwd(proj, wdw, bdw, gln, bln, wo, name, comm=None):
    s_len = proj.shape[0]
    tm = min(TM, s_len)

    def body(a_ref, g_ref, ah_ref, gh_ref, wdw_ref, bdw_ref, gln_ref, bln_ref, wo_ref, cv_ref, cb_ref, br_ref, ext_s, conv_s):
        _glu_ext(a_ref, g_ref, ah_ref, gh_ref, ext_s, pl.program_id(0) == 0)
        _causal_conv(ext_s, conv_s, wdw_ref, bdw_ref, tm)
        cv = conv_s[...].astype(BF)
        cv_ref[...] = cv
        chat, _ = _ln_fwd(cv.astype(F32))
        yl = chat * gln_ref[...] + bln_ref[...]
        cb = (yl * _sig(yl)).astype(BF)
        cb_ref[...] = cb
        br_ref[...] = _dot(cb, wo_ref[...].reshape(D, D)).astype(BF)

    return _pcall(
        body, (proj, proj, proj, proj, wdw, bdw, gln, bln, wo), name=name, grid=(s_len // tm,),
        in_specs=[_cols(tm, D, 2), _cols(tm, D, 3), _prev_halo(tm, D, 2), _prev_halo(tm, D, 3), _whole((HALO, D)),
                  _whole((1, D)), _whole((1, D)), _whole((1, D)), _rowsharded(BLK)],
        out_specs=[_cols(tm, D, 0)] * 3, out_shape=[_sds((s_len, D), BF)] * 3,
        scratch_shapes=[pltpu.VMEM((tm + HALO, D), F32), pltpu.VMEM((tm, D), F32)], comm=comm)


def pool_fwd(proj, wpool, spool, wo, name):
    s_len = proj.shape[0]
    tm = min(TM, s_len)

    def body(z_ref, zh_ref, wp_ref, sp_ref, wo_ref, pooled_ref, pm_ref, br_ref, ext_s, mr_s):
        i = pl.program_id(0)
        ext_s[0:HALO, :] = jnp.where(i == 0, 0.0, zh_ref[...].astype(F32))
        ext_s[HALO:, :] = z_ref[...].astype(F32)
        for gi, w in enumerate(POOL_WINDOWS):
            cs = slice(gi * POOL_GD, (gi + 1) * POOL_GD)
            e = ext_s[:, cs]
            s = e
            sh = 1
            while sh < w:
                s = s + pltpu.roll(s, sh, 0)
                sh *= 2
            pooled = (s[HALO:] * _inv_count(i, tm, w) - e[HALO:]).astype(BF)
            pooled_ref[:, cs] = pooled
            mr_s[:, cs] = _dot(pooled, wp_ref[gi])
        pm = (mr_s[...] * sp_ref[...]).astype(BF)
        pm_ref[...] = pm
        br_ref[...] = _dot(pm, wo_ref[...].reshape(D, D)).astype(BF)

    return pl.pallas_call(
        body, name=name, grid=(s_len // tm,),
        in_specs=[_cols(tm, D, 4), _prev_halo(tm, D, 4), _whole((4, POOL_GD, POOL_GD)), _whole((1, D)), _rowsharded(BLK)],
        out_specs=[_cols(tm, D, 0)] * 3, out_shape=[_sds((s_len, D), BF)] * 3,
        scratch_shapes=[pltpu.VMEM((tm + HALO, D), F32), pltpu.VMEM((tm, D), F32)], compiler_params=_params(1),
    )(proj, proj, wpool, spool, wo)


def merge_out(proj, bra, brb, brc, h, wout, gpost, name):
    s_len = h.shape[0]
    tm = min(TM, s_len)

    def body(z0, z1, z2, a_ref, b_ref, c_ref, h_ref, wo_ref, g_ref, mg_ref, mo_ref, h1_ref):
        merged = (_sig(z0[...].astype(F32)) * a_ref[...].astype(F32) + _sig(z1[...].astype(F32)) * b_ref[...].astype(F32)
                  + _sig(z2[...].astype(F32)) * c_ref[...].astype(F32)).astype(BF)
        mg_ref[...] = merged
        mo = _dot(merged, wo_ref[...].reshape(D, D))
        mo_ref[...] = mo.astype(BF)
        h1_ref[...] = h_ref[...] + mo * _rstd(mo) * g_ref[...]

    row = _cols(tm, D, 0)
    return pl.pallas_call(
        body, name=name, grid=(s_len // tm,),
        in_specs=[_cols(tm, D, 5), _cols(tm, D, 6), _cols(tm, D, 7), row, row, row, row, _rowsharded(BLK), _whole((1, D))],
        out_specs=[row] * 3, out_shape=[_sds((s_len, D), BF), _sds((s_len, D), BF), _sds((s_len, D), F32)],
        compiler_params=_params(1))(proj, proj, proj, bra, brb, brc, h, wout, gpost)


def _p_spec(tm, layer):
    return pl.BlockSpec((None, None, tm, PLE), lambda i: (layer, 0, i, 0))


def ffn_out(ff, h1, p, wfo, gpost, wpg, wple, layer, name):
    s_len = h1.shape[0]
    tm = min(TMB, s_len)

    def body(fg_ref, fu_ref, h1_ref, p_ref, wfo_ref, g_ref, wpg_ref, wple_ref, act_ref, f_ref, h2_ref, pg_ref, h3_ref):
        gt = fg_ref[...].astype(F32)
        act = (gt * _sig(gt) * fu_ref[...].astype(F32)).astype(BF)
        act_ref[...] = act
        f = _dot(act, wfo_ref[...].reshape(D_FF, D))
        f_ref[...] = f.astype(BF)
        h2 = h1_ref[...] + f * _rstd(f) * g_ref[...]
        h2_ref[...] = h2
        pg = _dot(h2.astype(BF), wpg_ref[...].reshape(D, D)).astype(BF)
        pg_ref[...] = pg
        pe = _dot(p_ref[...].astype(BF), wple_ref[...])
        h3_ref[...] = h2 + _sig(pg.astype(F32)) * pe

    row = _cols(tm, D, 0)
    return pl.pallas_call(
        body, name=name, grid=(s_len // tm,),
        in_specs=[_cols(tm, D_FF, 0), _cols(tm, D_FF, 1), row, _p_spec(tm, layer), _rowsharded(D_FF // N_DEV),
                  _whole((1, D)), _rowsharded(BLK), _whole((PLE, D))],
        out_specs=[_cols(tm, D_FF, 0), row, row, row, row],
        out_shape=[_sds((s_len, D_FF), BF), _sds((s_len, D), BF), _sds((s_len, D), F32), _sds((s_len, D), BF), _sds((s_len, D), F32)],
        compiler_params=_params(1))(ff, ff, h1, p, wfo, gpost, wpg, wple)


def loss_grad(y, target, name):
    s_len = y.shape[0]
    tm = min(TM, s_len)
    nt = s_len // tm

    def body(y_ref, t_ref, dy_ref, loss_ref, acc):
        i = pl.program_id(0)
        e = y_ref[...] - t_ref[...]
        dy_ref[...] = e * (1.0 / D)
        _accum(acc, i == 0, _colsum(e * e))

        @pl.when(i == nt - 1)
        def _():
            loss_ref[...] = jnp.broadcast_to(jnp.sum(acc[...], axis=1, keepdims=True) * (0.5 / D), (1, LC))

    row = _cols(tm, D, 0)
    return pl.pallas_call(
        body, name=name, grid=(nt,), in_specs=[row, row], out_specs=[row, _whole((1, LC))],
        out_shape=[_sds((s_len, D), F32), _sds((1, LC), F32)], scratch_shapes=[pltpu.VMEM((1, D), F32)],
        compiler_params=_params(1))(y, target)


def ple_ffn_bwd(dh3, pg, p, f, ff, wple, wpg, wfo, gpost, layer, name, comm=None):
    s_len = dh3.shape[0]
    tm = min(TMB, s_len)

    def body(dh3_ref, pg_ref, p_ref, f_ref, fg_ref, fu_ref, wple_ref, wpg_ref, wfo_ref, g_ref,
             dh2_ref, dpe_ref, dpg_ref, df_ref, dff_ref, dg_ref):
        i = pl.program_id(0)
        dh3v = dh3_ref[...]
        s = _sig(pg_ref[...].astype(F32))
        pe = _dot(p_ref[...].astype(BF), wple_ref[...])
        dpe_ref[...] = (dh3v * s).astype(BF)
        dpg = (dh3v * pe * s * (1.0 - s)).astype(BF)
        dpg_ref[...] = dpg
        dh2 = dh3v + _dot_nt(dpg, wpg_ref[...].reshape(D, D))
        dh2_ref[...] = dh2
        fv = f_ref[...].astype(F32)
        r = _rstd(fv)
        _accum(dg_ref, i == 0, _colsum(dh2 * fv * r))
        df = _rms_bwd(fv, dh2 * g_ref[...], r).astype(BF)
        df_ref[...] = df
        dact = _dot_nt(df, wfo_ref[...].reshape(D_FF, D))
        gt = fg_ref[...].astype(F32)
        sg = _sig(gt)
        up = fu_ref[...].astype(F32)
        dff_ref[:, 0:D_FF] = (dact * up * sg * (1.0 + gt * (1.0 - sg))).astype(BF)
        dff_ref[:, D_FF:2 * D_FF] = (dact * gt * sg).astype(BF)

    row = _cols(tm, D, 0)
    return _pcall(
        body, (dh3, pg, p, f, ff, ff, wple, wpg, wfo, gpost), name=name, grid=(s_len // tm,),
        in_specs=[row, row, _p_spec(tm, layer), row, _cols(tm, D_FF, 0), _cols(tm, D_FF, 1), _whole((PLE, D)),
                  _rowsharded(BLK), _rowsharded(D_FF // N_DEV), _whole((1, D))],
        out_specs=[row, row, row, row, _cols(tm, 2 * D_FF, 0), _whole((1, D))],
        out_shape=[_sds((s_len, D), F32), _sds((s_len, D), BF), _sds((s_len, D), BF), _sds((s_len, D), BF),
                   _sds((s_len, 2 * D_FF), BF), _sds((1, D), F32)], comm=comm)


def ffn_in_bwd(dff, w3, h1, dh2, gpre, name):
    s_len = h1.shape[0]
    tm = min(TMB, s_len)
    nb, _, tn = w3.shape

    def body(dff_ref, w_ref, h1_ref, dh2_ref, g_ref, dh1_ref, dg_ref):
        dhn = _dot_nt(dff_ref[:, 0:tn], w_ref[0])
        for j in range(1, nb):
            dhn = dhn + _dot_nt(dff_ref[:, j * tn:(j + 1) * tn], w_ref[j])
        x = h1_ref[...]
        r = _rstd(x)
        _accum(dg_ref, pl.program_id(0) == 0, _colsum(dhn * x * r))
        dh1_ref[...] = dh2_ref[...] + _rms_bwd(x, dhn * g_ref[...], r)

    row = _cols(tm, D, 0)
    return pl.pallas_call(
        body, name=name, grid=(s_len // tm,),
        in_specs=[_cols(tm, nb * tn, 0), pl.BlockSpec((nb, D, tn), lambda i: (0, 0, 0), pipeline_mode=pl.Buffered(1)), row, row,
                  _whole((1, D))],
        out_specs=[row, _whole((1, D))],
        out_shape=[_sds((s_len, D), F32), _sds((1, D), F32)], compiler_params=_params(1))(dff, w3, h1, dh2, gpre)


def mix_post_bwd(dh1, mo, gpost, wout, proj, bra, brb, brc, win, name, comm=None):
    s_len = dh1.shape[0]
    tm = min(TMB, s_len)

    def body(dh1_ref, mo_ref, g_ref, wo_ref, z0, z1, z2, a_ref, b_ref, c_ref, w5, w6, w7,
             dmo_ref, da_ref, db_ref, dc_ref, dz_ref, dhn_ref, dg_ref):
        i = pl.program_id(0)
        dh1v = dh1_ref[...]
        mo_v = mo_ref[...].astype(F32)
        r = _rstd(mo_v)
        _accum(dg_ref, i == 0, _colsum(dh1v * mo_v * r))
        dmo = _rms_bwd(mo_v, dh1v * g_ref[...], r).astype(BF)
        dmo_ref[...] = dmo
        dmerged = _dot_nt(dmo, wo_ref[...].reshape(D, D))
        dhn = jnp.zeros((tm, D), F32)
        for k, (z, br, dbr, w) in enumerate(((z0, a_ref, da_ref, w5), (z1, b_ref, db_ref, w6), (z2, c_ref, dc_ref, w7))):
            s = _sig(z[...].astype(F32))
            dbr[...] = (dmerged * s).astype(BF)
            dz = (dmerged * br[...].astype(F32) * s * (1.0 - s)).astype(BF)
            dz_ref[:, k * D:(k + 1) * D] = dz
            dhn = dhn + _dot_nt(dz, w[...])
        dhn_ref[...] = dhn

    row = _cols(tm, D, 0)
    return _pcall(
        body, (dh1, mo, gpost, wout, proj, proj, proj, bra, brb, brc, win, win, win), name=name, grid=(s_len // tm,),
        in_specs=[row, row, _whole((1, D)), _rowsharded(BLK), _cols(tm, D, 5), _cols(tm, D, 6), _cols(tm, D, 7), row, row, row,
                  _win_block(5), _win_block(6), _win_block(7)],
        out_specs=[row, row, row, row, _cols(tm, 3 * D, 0), row, _whole((1, D))],
        out_shape=[_sds((s_len, D), BF)] * 4 + [_sds((s_len, 3 * D), BF), _sds((s_len, D), F32), _sds((1, D), F32)], comm=comm)


def sgu_bwd(dbr, wo, proj, ws, wst, bsfull, gv, bv, win, name, comm=None):
    s_len = dbr.shape[0]
    tm = min(TMB, s_len)
    nt = s_len // tm

    def body(dbr_ref, wo_ref, zu_ref, zv_ref, ws_ref, wst_ref, bs_ref, gv_ref, bv_ref, w0, w1,
             dz_ref, dhn_ref, dws_ref, dbs_ref, dgv_ref, dbv_ref, vln_s, mix_s, dmix_s, dvln_s, bs_acc):
        i = pl.program_id(0)
        first = i == 0
        dsgu = _dot_nt(dbr_ref[...], wo_ref[...].reshape(D, D))
        zu = zu_ref[...].astype(F32)
        zv = zv_ref[...].astype(F32)
        u, tu = _gelu(zu)
        v, tv = _gelu(zv)
        vhat, rs = _ln_fwd(v)
        vln_s[...] = (vhat * gv_ref[...] + bv_ref[...]).astype(BF)
        _sgu_mix(ws_ref, vln_s, mix_s, bs_ref, tm, False)
        du = dsgu * mix_s[...]
        dmix = dsgu * u
        dmix_s[...] = dmix.astype(BF)
        blocks = dmix[0:BLK]
        for n in range(1, tm // BLK):
            blocks = blocks + dmix[n * BLK:(n + 1) * BLK]
        _accum(bs_acc, first, blocks)
        for hd in range(HEADS):
            cs = slice(hd * BLK, (hd + 1) * BLK)
            g = _dot_nt(dmix_s[0:BLK, cs], vln_s[0:BLK, cs])
            for n in range(1, tm // BLK):
                g = g + _dot_nt(dmix_s[n * BLK:(n + 1) * BLK, cs], vln_s[n * BLK:(n + 1) * BLK, cs])

            @pl.when(first)
            def _():
                dws_ref[hd] = g

            @pl.when(jnp.logical_not(first))
            def _():
                dws_ref[hd] += g
        _sgu_mix(wst_ref, dmix_s, dvln_s, None, tm, True)
        dvln = dvln_s[...]
        _accum(dgv_ref, first, _colsum(dvln * vhat))
        _accum(dbv_ref, first, _colsum(dvln))
        dv = _ln_bwd(dvln * gv_ref[...], vhat, rs)
        dzu = (du * _gelu_grad(zu, tu)).astype(BF)
        dzv = (dv * _gelu_grad(zv, tv)).astype(BF)
        dz_ref[:, 0:D] = dzu
        dz_ref[:, D:2 * D] = dzv
        dhn_ref[...] = _dot_nt(dzu, w0[...]) + _dot_nt(dzv, w1[...])

        @pl.when(i == nt - 1)
        def _():
            mask = _sgu_mask(False)
            for hd in range(HEADS):
                dws_ref[hd] = jnp.where(mask, dws_ref[hd], 0.0)
                dbs_ref[:, hd:hd + 1] = jnp.sum(bs_acc[:, hd * BLK:(hd + 1) * BLK], axis=1, keepdims=True)

    row = _cols(tm, D, 0)
    vec = _whole((1, D))
    return _pcall(
        body, (dbr, wo, proj, proj, ws, wst, bsfull, gv, bv, win, win), name=name, grid=(nt,),
        in_specs=[row, _rowsharded(BLK), _cols(tm, D, 0), _cols(tm, D, 1), _whole((HEADS, BLK, BLK)), _whole((HEADS, BLK, BLK)),
                  _whole((BLK, D)), vec, vec, _win_block(0), _win_block(1)],
        out_specs=[_cols(tm, 2 * D, 0), row, _whole((HEADS, BLK, BLK)), _whole((BLK, HEADS)), vec, vec],
        out_shape=[_sds((s_len, 2 * D), BF), _sds((s_len, D), F32), _sds((HEADS, BLK, BLK), F32), _sds((BLK, HEADS), F32),
                   _sds((1, D), F32), _sds((1, D), F32)],
        scratch_shapes=[pltpu.VMEM((tm, D), BF), pltpu.VMEM((tm, D), F32), pltpu.VMEM((tm, D), BF), pltpu.VMEM((tm, D), F32),
                        pltpu.VMEM((BLK, D), F32)], comm=comm)


def conv_bwd(dbr, wo, proj, conv, gln, bln, name, comm=None):
    s_len = dbr.shape[0]
    tm = min(TMB, s_len)
    nt = s_len // tm

    def body(dbr_ref, wo_ref, a_ref, g_ref, ah_ref, gh_ref, cv_ref, gln_ref, bln_ref,
             dc_ref, dw_ref, dbdw_ref, dgln_ref, dbln_ref, ext_s, dc_s, dw_acc):
        i = pl.program_id(0)
        first = i == 0
        dcb = _dot_nt(dbr_ref[...], wo_ref[...].reshape(D, D))
        _glu_ext(a_ref, g_ref, ah_ref, gh_ref, ext_s, first)
        chat, rs = _ln_fwd(cv_ref[...].astype(F32))
        yl = chat * gln_ref[...] + bln_ref[...]
        sy = _sig(yl)
        dyl = dcb * sy * (1.0 + yl * (1.0 - sy))
        _accum(dgln_ref, first, _colsum(dyl * chat))
        _accum(dbln_ref, first, _colsum(dyl))
        dc = _ln_bwd(dyl * gln_ref[...], chat, rs)
        _accum(dbdw_ref, first, _colsum(dc))
        dc_ref[...] = dc.astype(BF)
        dc_s[...] = dc

        @pl.when(first)
        def _():
            dw_acc[...] = jnp.zeros_like(dw_acc)

        def chunk(ci, carry):
            r0 = pl.multiple_of((ci // (D // LC)) * RC, RC)
            l0 = pl.multiple_of((ci % (D // LC)) * LC, LC)
            win = ext_s[pl.ds(r0, RC + HALO), pl.ds(l0, LC)]
            dcw = dc_s[pl.ds(r0, RC), pl.ds(l0, LC)]
            for r in range(8):
                wr = win if r == 0 else pltpu.roll(win, r, 0)
                for m in range(4):
                    d = 8 * m + r
                    if d < CONV_W:
                        k = CONV_W - 1 - d
                        prod = dcw * wr[HALO - 8 * m:HALO - 8 * m + RC]
                        dw_acc[k * 8:(k + 1) * 8, pl.ds(l0, LC)] += prod.reshape(RC // 8, 8, LC).sum(axis=0)
            return carry
        lax.fori_loop(0, (tm // RC) * (D // LC), chunk, 0)

        @pl.when(i == nt - 1)
        def _():
            dw_ref[...] = dw_acc[...].reshape(HALO, 8, D).sum(axis=1)

    row = _cols(tm, D, 0)
    vec = _whole((1, D))
    return _pcall(
        body, (dbr, wo, proj, proj, proj, proj, conv, gln, bln), name=name, grid=(nt,),
        in_specs=[row, _rowsharded(BLK), _cols(tm, D, 2), _cols(tm, D, 3), _prev_halo(tm, D, 2), _prev_halo(tm, D, 3),
                  row, vec, vec],
        out_specs=[row, _whole((HALO, D)), vec, vec, vec],
        out_shape=[_sds((s_len, D), BF), _sds((HALO, D), F32), _sds((1, D), F32), _sds((1, D), F32), _sds((1, D), F32)],
        scratch_shapes=[pltpu.VMEM((tm + HALO, D), F32), pltpu.VMEM((tm, D), F32), pltpu.VMEM((HALO * 8, D), F32)], comm=comm)


def pool_bwd(dbr, wo, pooled, wpool, spool, name):
    s_len = dbr.shape[0]
    tm = min(TMB, s_len)

    def body(dbr_ref, wo_ref, pl_ref, wp_ref, sp_ref, dmr_ref, q_ref, dsp_ref, mr_s):
        i = pl.program_id(0)
        dpm = _dot_nt(dbr_ref[...], wo_ref[...].reshape(D, D))
        for gi in range(4):
            cs = slice(gi * POOL_GD, (gi + 1) * POOL_GD)
            mr_s[:, cs] = _dot(pl_ref[:, cs], wp_ref[gi])
        _accum(dsp_ref, i == 0, _colsum(dpm * mr_s[...]))
        dmr = (dpm * sp_ref[...]).astype(BF)
        dmr_ref[...] = dmr
        for gi, w in enumerate(POOL_WINDOWS):
            cs = slice(gi * POOL_GD, (gi + 1) * POOL_GD)
            q_ref[:, cs] = (_dot_nt(dmr[:, cs], wp_ref[gi]) * _inv_count(i, tm, w)).astype(BF)

    row = _cols(tm, D, 0)
    return pl.pallas_call(
        body, name=name, grid=(s_len // tm,),
        in_specs=[row, _rowsharded(BLK), row, _whole((4, POOL_GD, POOL_GD)), _whole((1, D))],
        out_specs=[row, row, _whole((1, D))],
        out_shape=[_sds((s_len, D), BF), _sds((s_len, D), BF), _sds((1, D), F32)],
        scratch_shapes=[pltpu.VMEM((tm, D), F32)], compiler_params=_params(1))(dbr, wo, pooled, wpool, spool)


def seq_bwd(dc, q, proj, wdw, win, name, comm=None):
    s_len = dc.shape[0]
    tm = min(TMB, s_len)
    nt = s_len // tm

    def body(dc_ref, dch_ref, q_ref, qh_ref, a_ref, g_ref, wdw_ref, w2, w3, w4, dz_ref, dhn_ref, ext_s, dhc_s, qext_s):
        i = pl.program_id(0)
        last = i == nt - 1
        ext_s[0:tm, :] = dc_ref[...].astype(F32)
        ext_s[tm:, :] = jnp.where(last, 0.0, dch_ref[...].astype(F32))

        def chunk(ci, carry):
            r0 = pl.multiple_of((ci // (D // LC)) * RC, RC)
            l0 = pl.multiple_of((ci % (D // LC)) * LC, LC)
            win_ = ext_s[pl.ds(r0, RC + HALO), pl.ds(l0, LC)]
            acc = jnp.zeros((RC, LC), F32)
            for r in range(8):
                wr = win_ if r == 0 else pltpu.roll(win_, RC + HALO - r, 0)
                for m in range(4):
                    d = 8 * m + r
                    if d < CONV_W:
                        k = CONV_W - 1 - d
                        acc = acc + wdw_ref[k:k + 1, pl.ds(l0, LC)] * wr[8 * m:8 * m + RC]
            dhc_s[pl.ds(r0, RC), pl.ds(l0, LC)] = acc
            return carry
        lax.fori_loop(0, (tm // RC) * (D // LC), chunk, 0)

        dhc = dhc_s[...]
        av = a_ref[...].astype(F32)
        sg = _sig(g_ref[...].astype(F32))
        da = (dhc * sg).astype(BF)
        dg = (dhc * av * sg * (1.0 - sg)).astype(BF)
        dz_ref[:, 0:D] = da
        dz_ref[:, D:2 * D] = dg

        qext_s[0:tm, :] = q_ref[...].astype(F32)
        qext_s[tm:, :] = jnp.where(last, 0.0, qh_ref[...].astype(F32))
        for gi, w in enumerate(POOL_WINDOWS):
            cs = slice(gi * POOL_GD, (gi + 1) * POOL_GD)
            e = qext_s[:, cs]
            s = e
            sh = 1
            while sh < w:
                s = s + pltpu.roll(s, tm + HALO - sh, 0)
                sh *= 2
            dz_ref[:, 2 * D + gi * POOL_GD:2 * D + (gi + 1) * POOL_GD] = (s[0:tm] - e[0:tm] * _count(i, tm, w)).astype(BF)
        dhn_ref[...] = _dot_nt(da, w2[...]) + _dot_nt(dg, w3[...]) + _dot_nt(dz_ref[:, 2 * D:3 * D], w4[...])

    row = _cols(tm, D, 0)
    return _pcall(
        body, (dc, dc, q, q, proj, proj, wdw, win, win, win), name=name, grid=(nt,),
        in_specs=[row, _next_halo(tm, D, 0, s_len), row, _next_halo(tm, D, 0, s_len), _cols(tm, D, 2), _cols(tm, D, 3),
                  _whole((HALO, D)), _win_block(2), _win_block(3), _win_block(4)],
        out_specs=[_cols(tm, 3 * D, 0), row],
        out_shape=[_sds((s_len, 3 * D), BF), _sds((s_len, D), F32)],
        scratch_shapes=[pltpu.VMEM((tm + HALO, D), F32), pltpu.VMEM((tm, D), F32), pltpu.VMEM((tm + HALO, D), F32)], comm=comm)


def mix_pre_bwd(h, dh1, dhn_a, dhn_c, dhn_g, gpre, name):
    s_len = h.shape[0]
    tm = min(TM, s_len)

    def body(h_ref, dh1_ref, a_ref, c_ref, g3_ref, g_ref, dh_ref, dg_ref):
        x = h_ref[...]
        r = _rstd(x)
        dhn = a_ref[...] + c_ref[...] + g3_ref[...]
        _accum(dg_ref, pl.program_id(0) == 0, _colsum(dhn * x * r))
        dh_ref[...] = dh1_ref[...] + _rms_bwd(x, dhn * g_ref[...], r)

    row = _cols(tm, D, 0)
    return pl.pallas_call(
        body, name=name, grid=(s_len // tm,), in_specs=[row] * 5 + [_whole((1, D))], out_specs=[row, _whole((1, D))],
        out_shape=[_sds((s_len, D), F32), _sds((1, D), F32)], compiler_params=_params(1))(h, dh1, dhn_a, dhn_c, dhn_g, gpre)


def wgrad(a, b, tk, tn, name, stacked=False, diag=False, a_spec=None):
    s_len = b.shape[0]
    k_dim = a.shape[-1]
    n_dim = b.shape[1]
    ts = min(TS, s_len)
    nk = 1 if diag else k_dim // tk
    nn, ns = n_dim // tn, s_len // ts

    def body(a_ref, b_ref, o_ref, acc):
        s = pl.program_id(2)
        _accum(acc, s == 0, _dot_tn(a_ref[...].astype(BF), b_ref[...].astype(BF)))

        @pl.when(s == ns - 1)
        def _():
            o_ref[...] = acc[...].astype(BF).reshape(o_ref.shape)

    if a_spec is None:
        a_spec = pl.BlockSpec((ts, tk), (lambda k, n, s: (s, n)) if diag else (lambda k, n, s: (s, k)))
    if stacked or diag:
        out_shape = _sds((nn, tk if diag else k_dim, tn), BF)
        o_spec = pl.BlockSpec((1, tk, tn), lambda k, n, s: (n, k, 0))
    else:
        out_shape = _sds((k_dim, n_dim), BF)
        o_spec = pl.BlockSpec((tk, tn), lambda k, n, s: (k, n))
    return pl.pallas_call(
        body, name=name, grid=(nk, nn, ns),
        in_specs=[a_spec, pl.BlockSpec((ts, tn), lambda k, n, s: (s, n))], out_specs=o_spec, out_shape=out_shape,
        scratch_shapes=[pltpu.VMEM((tk, tn), F32)], compiler_params=_params(3))(a, b)


_WEIGHTS = ['g_mix_pre', 'w_in', 'w_sgu_s', 'b_sgu_s', 'g_sgu_v', 'b_sgu_v', 'w_sgu_out', 'w_dw', 'b_dw', 'g_conv_ln', 'b_conv_ln',
            'w_conv_out', 'w_pool', 's_pool', 'w_pool_out', 'w_out', 'g_mix_post', 'g_ffn_pre', 'w_ffn_in', 'w_ffn_out', 'g_ffn_post',
            'w_ple', 'w_ple_gate']
_SHARDED = ['w_in', 'w_sgu_out', 'w_conv_out', 'w_pool', 'w_pool_out', 'w_out', 'w_ffn_in', 'w_ffn_out', 'w_ple', 'w_ple_gate']
_VECTORS = ['g_mix_pre', 'g_sgu_v', 'b_sgu_v', 'b_dw', 'g_conv_ln', 'b_conv_ln', 's_pool', 'g_mix_post', 'g_ffn_pre', 'g_ffn_post']
_SUBLANES = 8
_REP_PAD = _SUBLANES * (len(_VECTORS) + 1) + 2 * HEADS * BLK * BLK // D
_DW_ROWS = 2 * HALO


def _pack_replicated(t):
    rows = [jnp.pad(t[k].reshape(2, D), ((0, _SUBLANES - 2), (0, 0))) for k in _VECTORS + ['b_sgu_s']]
    return jnp.concatenate(rows + [t['w_sgu_s'].reshape(-1, D)], axis=0)


def _unpack_replicated(packed):
    out = {}
    for i, k in enumerate(_VECTORS):
        out[k] = packed[_SUBLANES * i:_SUBLANES * i + 2]
    o = _SUBLANES * len(_VECTORS)
    out['b_sgu_s'] = packed[o:o + 2].reshape(2, HEADS, BLK)
    out['w_sgu_s'] = packed[o + _SUBLANES:].reshape(2, HEADS, BLK, BLK)
    return out


def _pad_taps(w):
    return jnp.pad(w, ((0, HALO - CONV_W), (0, 0)))


def kernel(x, p, g_mix_pre, w_in, w_sgu_s, b_sgu_s, g_sgu_v, b_sgu_v, w_sgu_out, w_dw, b_dw, g_conv_ln, b_conv_ln, w_conv_out, w_pool, s_pool, w_pool_out, w_out, g_mix_post, g_ffn_pre, w_ffn_in, w_ffn_out, g_ffn_post, w_ple, w_ple_gate, loss_target, m_g_mix_pre, m_w_in, m_w_sgu_s, m_b_sgu_s, m_g_sgu_v, m_b_sgu_v, m_w_sgu_out, m_w_dw, m_b_dw, m_g_conv_ln, m_b_conv_ln, m_w_conv_out, m_w_pool, m_s_pool, m_w_pool_out, m_w_out, m_g_mix_post, m_g_ffn_pre, m_w_ffn_in, m_w_ffn_out, m_g_ffn_post, m_w_ple, m_w_ple_gate, v_g_mix_pre, v_w_in, v_w_sgu_s, v_b_sgu_s, v_g_sgu_v, v_b_sgu_v, v_w_sgu_out, v_w_dw, v_b_dw, v_g_conv_ln, v_b_conv_ln, v_w_conv_out, v_w_pool, v_s_pool, v_w_pool_out, v_w_out, v_g_mix_post, v_g_ffn_pre, v_w_ffn_in, v_w_ffn_out, v_g_ffn_post, v_w_ple, v_w_ple_gate):
    W = dict(g_mix_pre=g_mix_pre, w_in=w_in, w_sgu_s=w_sgu_s, b_sgu_s=b_sgu_s, g_sgu_v=g_sgu_v, b_sgu_v=b_sgu_v, w_sgu_out=w_sgu_out,
             w_dw=w_dw, b_dw=b_dw, g_conv_ln=g_conv_ln, b_conv_ln=b_conv_ln, w_conv_out=w_conv_out, w_pool=w_pool, s_pool=s_pool,
             w_pool_out=w_pool_out, w_out=w_out, g_mix_post=g_mix_post, g_ffn_pre=g_ffn_pre, w_ffn_in=w_ffn_in, w_ffn_out=w_ffn_out,
             g_ffn_post=g_ffn_post, w_ple=w_ple, w_ple_gate=w_ple_gate)
    M = dict(g_mix_pre=m_g_mix_pre, w_in=m_w_in, w_sgu_s=m_w_sgu_s, b_sgu_s=m_b_sgu_s, g_sgu_v=m_g_sgu_v, b_sgu_v=m_b_sgu_v,
             w_sgu_out=m_w_sgu_out, w_dw=m_w_dw, b_dw=m_b_dw, g_conv_ln=m_g_conv_ln, b_conv_ln=m_b_conv_ln, w_conv_out=m_w_conv_out,
             w_pool=m_w_pool, s_pool=m_s_pool, w_pool_out=m_w_pool_out, w_out=m_w_out, g_mix_post=m_g_mix_post, g_ffn_pre=m_g_ffn_pre,
             w_ffn_in=m_w_ffn_in, w_ffn_out=m_w_ffn_out, g_ffn_post=m_g_ffn_post, w_ple=m_w_ple, w_ple_gate=m_w_ple_gate)
    V = dict(g_mix_pre=v_g_mix_pre, w_in=v_w_in, w_sgu_s=v_w_sgu_s, b_sgu_s=v_b_sgu_s, g_sgu_v=v_g_sgu_v, b_sgu_v=v_b_sgu_v,
             w_sgu_out=v_w_sgu_out, w_dw=v_w_dw, b_dw=v_b_dw, g_conv_ln=v_g_conv_ln, b_conv_ln=v_b_conv_ln, w_conv_out=v_w_conv_out,
             w_pool=v_w_pool, s_pool=v_s_pool, w_pool_out=v_w_pool_out, w_out=v_w_out, g_mix_post=v_g_mix_post, g_ffn_pre=v_g_ffn_pre,
             w_ffn_in=v_w_ffn_in, w_ffn_out=v_w_ffn_out, g_ffn_post=v_g_ffn_post, w_ple=v_w_ple, w_ple_gate=v_w_ple_gate)

    my_c = lax.axis_index("c")
    my_chip = 2 * lax.axis_index("x") + lax.axis_index("y")
    my_dev = 2 * my_chip + my_c
    s_len = x.shape[1]
    h0 = x.reshape(s_len, D)
    target = loss_target.reshape(s_len, D)

    shard = [{k: W[k][l].astype(BF) for k in _SHARDED} for l in range(2)]
    for l in range(2):
        shard[l]['w_dw'] = w_dw[l]
    rest = [k for k in _SHARDED if k != 'w_in'] + ['w_dw']
    hosted_gather = {
        'norm_proj_in': (0, rest),
        'conv_fwd': (1, ['w_in', 'w_ffn_in']),
        'norm_proj_ffn': (1, [k for k in rest if k != 'w_ffn_in']),
    }
    G = [{'w_in': run_comm(Gather([shard[0]['w_in']]), "gather_w_in_0")[0]}, {}]

    def gather_in(layer, call):
        if layer != 0:
            return None, (lambda got: None)
        to_layer, keys = hosted_gather[call]
        return Gather([shard[to_layer][k] for k in keys]), (lambda got: G[to_layer].update(zip(keys, got)))

    def natural(g):
        wfi = jnp.transpose(g['w_ffn_in'].reshape(4, 2, D, D_FF // 4), (0, 2, 1, 3)).reshape(4, D, D_FF // 2)
        wpool = jnp.transpose(g['w_pool'], (1, 0, 2, 3)).reshape(4, POOL_GD, POOL_GD)
        wple = jnp.transpose(g['w_ple'], (1, 0, 2)).reshape(PLE, D)
        wdw = jnp.transpose(g['w_dw'].reshape(N_DEV, CONV_W, BLK), (1, 0, 2)).reshape(CONV_W, D)
        return dict(wfi=wfi, wpool=wpool, wple=wple, wdw=_pad_taps(wdw))

    def vec(name, layer):
        return W[name][layer].reshape(1, D)

    saved = []
    h = h0
    for l in range(2):
        g = G[l]
        comm, land = gather_in(l, 'norm_proj_in')
        (proj, hn), got = norm_proj(h, vec('g_mix_pre', l), g['w_in'], f"norm_proj_in_{l}", comm=comm)
        land(got)
        nat = natural(g)
        bsfull = jnp.repeat(b_sgu_s[l].T, BLK, axis=1)
        wst = jnp.swapaxes(w_sgu_s[l], 1, 2)
        sgu, bra = sgu_fwd(proj, w_sgu_s[l], bsfull, vec('g_sgu_v', l), vec('b_sgu_v', l), g['w_sgu_out'], f"sgu_fwd_{l}")
        comm, land = gather_in(l, 'conv_fwd')
        (conv, cb, brb), got = conv_fwd(proj, nat['wdw'], vec('b_dw', l), vec('g_conv_ln', l), vec('b_conv_ln', l), g['w_conv_out'],
                                        f"conv_fwd_{l}", comm=comm)
        land(got)
        pooled, pm, brc = pool_fwd(proj, nat['wpool'], vec('s_pool', l), g['w_pool_out'], f"pool_fwd_{l}")
        merged, mo, h1 = merge_out(proj, bra, brb, brc, h, g['w_out'], vec('g_mix_post', l), f"merge_out_{l}")
        comm, land = gather_in(l, 'norm_proj_ffn')
        (ff, hn2), got = norm_proj(h1, vec('g_ffn_pre', l), nat['wfi'], f"norm_proj_ffn_{l}", comm=comm)
        land(got)
        act, f, h2, pg, h3 = ffn_out(ff, h1, p, g['w_ffn_out'], vec('g_ffn_post', l), g['w_ple_gate'], nat['wple'], l, f"ffn_out_{l}")
        saved.append(dict(h=h, nat=nat, bsfull=bsfull, wst=wst, proj=proj, hn=hn, sgu=sgu, bra=bra, conv=conv, cb=cb, brb=brb,
                          pooled=pooled, pm=pm, brc=brc, merged=merged, mo=mo, h1=h1, ff=ff, hn2=hn2, act=act, f=f, h2=h2, pg=pg))
        h = h3

    dh, loss_row = loss_grad(h, target, "loss_grad")
    loss = lax.psum(loss_row[0, 0], ("x", "y", "c"))

    parts = {k: [None, None] for k in _SHARDED}
    small = {k: [None, None] for k in _VECTORS + ['b_sgu_s', 'w_sgu_s', 'w_dw']}
    chip_parts = [None, None]
    from_chips = [{}, {}]
    big = ['w_in', 'w_ffn_in']

    def add_siblings(layer, from_sibling):
        out = {}
        for k, rv in zip(_SHARDED, from_sibling):
            st = parts[k][layer]
            mine = lax.dynamic_index_in_dim(st.reshape((4, 2) + st.shape[1:]), my_c, axis=1, keepdims=False)
            cols = st.shape[-1]
            out[k] = add_bf16(mine.reshape(-1, cols), rv.reshape(-1, cols), f"rs_add_{k}_{layer}").reshape(rv.shape)
        return out

    for l in (1, 0):
        sv, g, nat = saved[l], G[l], saved[l]['nat']
        comm = ToSibling([parts[k][1] for k in _SHARDED]) if l == 0 else None
        (dh2, dpe, dpg, df, dff, small['g_ffn_post'][l]), got = ple_ffn_bwd(
            dh, sv['pg'], p, sv['f'], sv['ff'], nat['wple'], g['w_ple_gate'], g['w_ffn_out'], vec('g_ffn_post', l), l,
            f"ple_ffn_bwd_{l}", comm=comm)
        if l == 0:
            chip_parts[1] = add_siblings(1, got)
        dh1, small['g_ffn_pre'][l] = ffn_in_bwd(dff, nat['wfi'], sv['h1'], dh2, vec('g_ffn_pre', l), f"ffn_in_bwd_{l}")
        (dmo, dbra, dbrb, dbrc, dzg, dhn_g, small['g_mix_post'][l]), _ = mix_post_bwd(
            dh1, sv['mo'], vec('g_mix_post', l), g['w_out'], sv['proj'], sv['bra'], sv['brb'], sv['brc'], g['w_in'], f"mix_post_bwd_{l}")
        comm = ToChips([chip_parts[1][k] for k in big]) if l == 0 else None
        (dzs, dhn_a, dws, dbs, small['g_sgu_v'][l], small['b_sgu_v'][l]), got = sgu_bwd(
            dbra, g['w_sgu_out'], sv['proj'], w_sgu_s[l], sv['wst'], sv['bsfull'], vec('g_sgu_v', l), vec('b_sgu_v', l), g['w_in'],
            f"sgu_bwd_{l}", comm=comm)
        if l == 0:
            from_chips[1].update(zip(big, got))
        small['w_sgu_s'][l] = dws
        small['b_sgu_s'][l] = dbs.T
        others = [k for k in _SHARDED if k not in big]
        comm = ToChips([chip_parts[1][k] for k in others]) if l == 0 else None
        (dc, dwdw, small['b_dw'][l], small['g_conv_ln'][l], small['b_conv_ln'][l]), got = conv_bwd(
            dbrb, g['w_conv_out'], sv['proj'], sv['conv'], vec('g_conv_ln', l), vec('b_conv_ln', l), f"conv_bwd_{l}", comm=comm)
        if l == 0:
            from_chips[1].update(zip(others, got))
        small['w_dw'][l] = dwdw
        dmr, q, small['s_pool'][l] = pool_bwd(dbrc, g['w_pool_out'], sv['pooled'], nat['wpool'], vec('s_pool', l), f"pool_bwd_{l}")
        (dzc, dhn_c), _ = seq_bwd(dc, q, sv['proj'], nat['wdw'], g['w_in'], f"seq_bwd_{l}")
        dh, small['g_mix_pre'][l] = mix_pre_bwd(sv['h'], dh1, dhn_a, dhn_c, dhn_g, vec('g_mix_pre', l), f"mix_pre_bwd_{l}")

        p_spec = pl.BlockSpec((None, None, min(TS, s_len), PLE), functools.partial(lambda k, n, s, ll: (ll, 0, s, 0), ll=l))
        g_ple = wgrad(p, dpe, PLE, D, f"wgrad_ple_{l}", a_spec=p_spec)
        parts['w_ple'][l] = jnp.transpose(g_ple.reshape(PLE, N_DEV, BLK), (1, 0, 2))
        parts['w_ple_gate'][l] = wgrad(sv['h2'], dpg, D, D, f"wgrad_ple_gate_{l}").reshape(N_DEV, BLK, D)
        parts['w_ffn_out'][l] = wgrad(sv['act'], df, D_FF // 2, D, f"wgrad_ffn_out_{l}").reshape(N_DEV, D_FF // N_DEV, D)
        g_fi = wgrad(sv['hn2'], dff, D, D_FF // 2, f"wgrad_ffn_in_{l}", stacked=True)
        parts['w_ffn_in'][l] = jnp.transpose(g_fi.reshape(4, D, 2, D_FF // 4), (0, 2, 1, 3)).reshape(N_DEV, D, D_FF // 4)
        parts['w_out'][l] = wgrad(sv['merged'], dmo, D, D, f"wgrad_out_{l}").reshape(N_DEV, BLK, D)
        parts['w_sgu_out'][l] = wgrad(sv['sgu'], dbra, D, D, f"wgrad_sgu_out_{l}").reshape(N_DEV, BLK, D)
        parts['w_conv_out'][l] = wgrad(sv['cb'], dbrb, D, D, f"wgrad_conv_out_{l}").reshape(N_DEV, BLK, D)
        parts['w_pool_out'][l] = wgrad(sv['pm'], dbrc, D, D, f"wgrad_pool_out_{l}").reshape(N_DEV, BLK, D)
        g_pool = wgrad(sv['pooled'], dmr, POOL_GD, POOL_GD, f"wgrad_pool_{l}", diag=True)
        parts['w_pool'][l] = jnp.transpose(g_pool.reshape(4, N_DEV, POOL_GD // N_DEV, POOL_GD), (1, 0, 2, 3))
        parts['w_in'][l] = jnp.concatenate([
            wgrad(sv['hn'], dzs, D, D, f"wgrad_in_sgu_{l}", stacked=True),
            wgrad(sv['hn'], dzc, D, D, f"wgrad_in_seq_{l}", stacked=True),
            wgrad(sv['hn'], dzg, D, D, f"wgrad_in_gate_{l}", stacked=True)], axis=0)
    grad_x = dh.reshape(1, s_len, D)

    chip_parts[0] = add_siblings(0, run_comm(ToSibling([parts[k][0] for k in _SHARDED]), "rs_to_sibling_0"))
    from_chips[0].update(zip(_SHARDED, run_comm(ToChips([chip_parts[0][k] for k in _SHARDED]), "rs_to_chips_0")))

    outs = {}
    for k in _SHARDED:
        cols = W[k].shape[-1]
        pieces = []
        for layer in range(2):
            own = lax.dynamic_index_in_dim(chip_parts[layer][k], my_chip, axis=0, keepdims=False).reshape(-1, cols)
            rv3 = from_chips[layer][k].reshape(3, -1, cols)
            pieces.append([(own, None), (rv3, 0), (rv3, 1), (rv3, 2)])
        res = adamw_layers(W[k].reshape(2, -1, cols), M[k].reshape(2, -1, cols), V[k].reshape(2, -1, cols), pieces, f"adamw_{k}")
        outs[k] = [r.reshape(W[k].shape) for r in res]

    small_full = {k: jnp.stack(small[k], axis=0) for k in small}
    rep_grads = _pack_replicated(small_full)
    dw_grads = small_full['w_dw'].reshape(_DW_ROWS, D)
    gathered_small = run_comm(Gather([jnp.concatenate([rep_grads, dw_grads], axis=0)]), "all_gather_small_grads")[0]
    rep_part = gathered_small[:, :_REP_PAD, :]
    rep_res = adamw(_pack_replicated(W), _pack_replicated(M), _pack_replicated(V), [(rep_part, d) for d in range(N_DEV)],
                    "adamw_replicated")
    for idx, packed in enumerate(rep_res):
        for name, val in _unpack_replicated(packed).items():
            outs.setdefault(name, [None] * 4)[idx] = val
    dw_sum = sum_slabs(gathered_small[:, _REP_PAD:, :], "sum_w_dw")
    dw_mine = lax.dynamic_slice_in_dim(dw_sum.reshape(2, HALO, D)[:, :CONV_W], my_dev * BLK, BLK, axis=2)
    res = adamw(w_dw.reshape(2 * CONV_W, BLK), m_w_dw.reshape(2 * CONV_W, BLK), v_w_dw.reshape(2 * CONV_W, BLK),
                [(dw_mine.reshape(2 * CONV_W, BLK), None)], "adamw_w_dw")
    outs['w_dw'] = [r.reshape(w_dw.shape) for r in res]

    result = [loss, grad_x]
    for idx in range(4):
        result += [outs[k][idx] for k in _WEIGHTS]
    return tuple(result)
```

```python
import functools
import math

import jax
import jax.numpy as jnp
from jax import lax
from jax.experimental import pallas as pl
from jax.experimental.pallas import tpu as pltpu

F32 = jnp.float32
BF = jnp.bfloat16

D = 1024
D_FF = 2816
PLE = 256
N_DEV = 8
HEADS = 8
BLK = 128
CHUNK = 64
CONV_W = 31
POOL_WINDOWS = (2, 4, 8, 16)
POOL_GD = 256
EPS = 1e-6

V7X_VMEM_BYTES = 64 * 2**20
VMEM_LIMIT = V7X_VMEM_BYTES * 7 // 8
HALO = 32
RC = 64
LC = 128
TM = 512
TMB = 256
TMP = 1024
TS = 2048

ADAM_LR, ADAM_B1, ADAM_B2, ADAM_EPS, ADAM_WD, ADAM_STEP = 0.001, 0.9, 0.999, 1e-08, 0.01, 10

MESH = pl.DeviceIdType.MESH
ANY = pl.BlockSpec(memory_space=pl.ANY)

_GELU_K0 = math.sqrt(2.0 / math.pi)
_GELU_K1 = 0.044715


def _dot(a, b):
    return jnp.dot(a, b, preferred_element_type=F32)


def _dot_nt(a, b):
    return lax.dot_general(a, b, (((1,), (1,)), ((), ())), preferred_element_type=F32)


def _dot_tn(a, b):
    return lax.dot_general(a, b, (((0,), (0,)), ((), ())), preferred_element_type=F32)


def _sig(x):
    return 1.0 / (1.0 + jnp.exp(-x))


def _gelu(x):
    t = jnp.tanh(_GELU_K0 * (x + _GELU_K1 * x * x * x))
    return 0.5 * x * (1.0 + t), t


def _gelu_grad(x, t):
    return 0.5 * (1.0 + t) + 0.5 * x * (1.0 - t * t) * _GELU_K0 * (1.0 + 3.0 * _GELU_K1 * x * x)


def _rstd(x):
    return lax.rsqrt(jnp.mean(x * x, axis=-1, keepdims=True) + EPS)


def _rms_bwd(x, gd, r):
    return r * gd - x * (r * r * r) * jnp.mean(gd * x, axis=-1, keepdims=True)


def _ln_fwd(x):
    mu = jnp.mean(x, axis=-1, keepdims=True)
    xc = x - mu
    rs = lax.rsqrt(jnp.mean(xc * xc, axis=-1, keepdims=True) + EPS)
    return xc * rs, rs


def _ln_bwd(dhat, hat, rs):
    return rs * (dhat - jnp.mean(dhat, axis=-1, keepdims=True) - hat * jnp.mean(dhat * hat, axis=-1, keepdims=True))


def _colsum(x):
    return jnp.sum(x, axis=0, keepdims=True)


def _accum(ref, first, val):
    @pl.when(first)
    def _():
        ref[...] = val

    @pl.when(jnp.logical_not(first))
    def _():
        ref[...] += val


def _sgu_mask(transposed):
    r = lax.broadcasted_iota(jnp.int32, (BLK, BLK), 0) // CHUNK
    c = lax.broadcasted_iota(jnp.int32, (BLK, BLK), 1) // CHUNK
    return (r <= c) if transposed else (c <= r)


def _inv_count(i, tm, w):
    t = lax.broadcasted_iota(jnp.int32, (tm, 1), 0) + i * tm
    return 1.0 / jnp.minimum(t + 1, w).astype(F32)


def _count(i, tm, w):
    t = lax.broadcasted_iota(jnp.int32, (tm, 1), 0) + i * tm
    return jnp.minimum(t + 1, w).astype(F32)


def _params(n_grid):
    return pltpu.CompilerParams(dimension_semantics=("arbitrary",) * n_grid, vmem_limit_bytes=VMEM_LIMIT)


def _sds(shape, dtype):
    return jax.ShapeDtypeStruct(shape, dtype)


def _cols(tm, width, cb):
    return pl.BlockSpec((tm, width), lambda i: (i, cb))


def _whole(shape):
    nd = len(shape)
    return pl.BlockSpec(shape, lambda i: (0,) * nd)


def _prev_halo(tm, width, cb):
    return pl.BlockSpec((HALO, width), lambda i: (jnp.maximum(i * (tm // HALO) - 1, 0), cb))


def _next_halo(tm, width, cb, s_len):
    last = s_len // HALO - 1
    return pl.BlockSpec((HALO, width), lambda i: (jnp.minimum((i + 1) * (tm // HALO), last), cb))


def _rowsharded(rows):
    return pl.BlockSpec((N_DEV, rows, D), lambda i: (0, 0, 0))


def _win_block(j):
    return pl.BlockSpec((None, D, D), lambda i: (j, 0, 0))


def _row_tile(rows, cap):
    t = min(rows, cap)
    while rows % t or t % 16:
        t -= 16
    return t


class Gather:
    def __init__(self, arrs):
        self.arrs = list(arrs)
        n = self.n = len(self.arrs)
        self.out_shape = [_sds((N_DEV,) + a.shape, a.dtype) for a in self.arrs]
        self.scratch = [pltpu.SemaphoreType.DMA((n, 7)), pltpu.SemaphoreType.DMA((n, 7)), pltpu.SemaphoreType.DMA((n,))]

    def _plan(self, ins, outs, sems):
        send, recv, local = sems
        x, y, c = lax.axis_index("x"), lax.axis_index("y"), lax.axis_index("c")
        me, sibling = (x, y, c), (x, y, 1 - c)
        chips = [(1 - x, y), (x, 1 - y), (1 - x, 1 - y)]

        def copy(a, k, block, to, src=None):
            dst = outs[a].at[4 * block[0] + 2 * block[1] + block[2]]
            return pltpu.make_async_remote_copy(
                src_ref=dst if src is None else src, dst_ref=dst, send_sem=send.at[a, k], recv_sem=recv.at[a, k],
                device_id=to, device_id_type=MESH)

        mine = [pltpu.make_async_copy(ins[a], outs[a].at[4 * x + 2 * y + c], local.at[a]) for a in range(self.n)]
        first = []
        for a in range(self.n):
            first.append(copy(a, 0, me, sibling, src=ins[a]))
            first += [copy(a, 1 + j, me, (*chip, c), src=ins[a]) for j, chip in enumerate(chips)]
        return me, sibling, chips, c, copy, mine, first

    def start(self, ins, outs, sems):
        *_, mine, first = self._plan(ins, outs, sems)
        for cp in mine + first:
            cp.start()

    def finish(self, ins, outs, sems):
        me, sibling, chips, c, copy, mine, first = self._plan(ins, outs, sems)
        passed = []
        for a in range(self.n):
            for j, chip in enumerate(chips):
                copy(a, 1 + j, (*chip, c), me).wait_recv()
                fwd = copy(a, 4 + j, (*chip, c), sibling)
                fwd.start()
                passed.append(fwd)
        for a in range(self.n):
            copy(a, 0, sibling, me).wait_recv()
            for j, chip in enumerate(chips):
                copy(a, 4 + j, (*chip, 1 - c), me).wait_recv()
        for cp in first + passed:
            cp.wait_send()
        for cp in mine:
            cp.wait()


class ToSibling:
    def __init__(self, parts):
        self.arrs = list(parts)
        n = self.n = len(self.arrs)
        self.out_shape = [_sds((4,) + p.shape[1:], p.dtype) for p in self.arrs]
        self.scratch = [pltpu.SemaphoreType.DMA((n,)), pltpu.SemaphoreType.DMA((n,))]

    def start(self, ins, outs, sems):
        send, recv = sems
        x, y, c = lax.axis_index("x"), lax.axis_index("y"), lax.axis_index("c")
        for a in range(self.n):
            for q in range(4):
                pltpu.make_async_remote_copy(
                    src_ref=ins[a].at[2 * q + 1 - c], dst_ref=outs[a].at[q], send_sem=send.at[a], recv_sem=recv.at[a],
                    device_id=(x, y, 1 - c), device_id_type=MESH).start()

    def finish(self, ins, outs, sems):
        send, recv = sems
        x, y, c = lax.axis_index("x"), lax.axis_index("y"), lax.axis_index("c")
        for a in range(self.n):
            pltpu.make_async_remote_copy(
                src_ref=outs[a], dst_ref=outs[a], send_sem=send.at[a], recv_sem=recv.at[a],
                device_id=(x, y, 1 - c), device_id_type=MESH).wait()


class ToChips:
    def __init__(self, cps):
        self.arrs = list(cps)
        n = self.n = len(self.arrs)
        self.out_shape = [_sds((3,) + p.shape[1:], p.dtype) for p in self.arrs]
        self.scratch = [pltpu.SemaphoreType.DMA((n,)), pltpu.SemaphoreType.DMA((n,))]

    def start(self, ins, outs, sems):
        send, recv = sems
        x, y, c = lax.axis_index("x"), lax.axis_index("y"), lax.axis_index("c")
        for a in range(self.n):
            for r, (px, py) in enumerate([(1 - x, y), (x, 1 - y), (1 - x, 1 - y)]):
                pltpu.make_async_remote_copy(
                    src_ref=ins[a].at[2 * px + py], dst_ref=outs[a].at[r], send_sem=send.at[a], recv_sem=recv.at[a],
                    device_id=(px, py, c), device_id_type=MESH).start()

    def finish(self, ins, outs, sems):
        send, recv = sems
        x, y, c = lax.axis_index("x"), lax.axis_index("y"), lax.axis_index("c")
        for a in range(self.n):
            pltpu.make_async_remote_copy(
                src_ref=outs[a], dst_ref=outs[a], send_sem=send.at[a], recv_sem=recv.at[a],
                device_id=(x, y, c), device_id_type=MESH).wait()


def run_comm(comm, name):
    n = comm.n

    def body(*refs):
        ins, outs, sems = refs[:n], refs[n:2 * n], refs[2 * n:]
        comm.start(ins, outs, sems)
        comm.finish(ins, outs, sems)

    return pl.pallas_call(body, name=name, out_shape=comm.out_shape, in_specs=[ANY] * n, out_specs=[ANY] * n,
                          scratch_shapes=comm.scratch)(*comm.arrs)


def _pcall(body, args, *, name, grid, in_specs, out_specs, out_shape, scratch_shapes=(), comm=None):
    params = _params(len(grid))
    scratch_shapes = list(scratch_shapes)
    if comm is None:
        outs = pl.pallas_call(body, name=name, grid=grid, in_specs=in_specs, out_specs=out_specs, out_shape=out_shape,
                              scratch_shapes=scratch_shapes, compiler_params=params)(*args)
        return outs, None
    n_in, n_out, n_scr, nc = len(in_specs), len(out_specs), len(scratch_shapes), comm.n

    def hosted(*refs):
        ins, cins = refs[:n_in], refs[n_in:n_in + nc]
        o0 = n_in + nc
        outs, couts = refs[o0:o0 + n_out], refs[o0 + n_out:o0 + n_out + nc]
        s0 = o0 + n_out + nc
        scr, csems = refs[s0:s0 + n_scr], refs[s0 + n_scr:]
        first = pl.program_id(0) == 0
        last = pl.program_id(0) == grid[0] - 1
        for ax in range(1, len(grid)):
            first = jnp.logical_and(first, pl.program_id(ax) == 0)
            last = jnp.logical_and(last, pl.program_id(ax) == grid[ax] - 1)

        @pl.when(first)
        def _():
            comm.start(cins, couts, csems)
        body(*ins, *outs, *scr)

        @pl.when(last)
        def _():
            comm.finish(cins, couts, csems)

    res = pl.pallas_call(
        hosted, name=name, grid=grid, in_specs=list(in_specs) + [ANY] * nc, out_specs=list(out_specs) + [ANY] * nc,
        out_shape=list(out_shape) + comm.out_shape, scratch_shapes=scratch_shapes + comm.scratch,
        compiler_params=params)(*args, *comm.arrs)
    return res[:n_out], res[n_out:]


def add_bf16(a, b, name):
    rows, cols = a.shape
    tr = _row_tile(rows, 512)

    def body(a_ref, b_ref, o_ref):
        o_ref[...] = (a_ref[...].astype(F32) + b_ref[...].astype(F32)).astype(o_ref.dtype)

    spec = pl.BlockSpec((tr, cols), lambda i: (i, 0))
    return pl.pallas_call(body, name=name, grid=(rows // tr,), in_specs=[spec, spec], out_specs=spec,
                          out_shape=_sds(a.shape, BF), compiler_params=_params(1))(a, b)


def adamw(w, m, v, pieces, name):
    rows, cols = w.shape
    tr = _row_tile(rows, 256) if rows % 16 == 0 else rows
    np_ = len(pieces)
    c1 = 1.0 / (1.0 - ADAM_B1 ** ADAM_STEP)
    c2 = 1.0 / (1.0 - ADAM_B2 ** ADAM_STEP)

    def body(*refs):
        w_ref, m_ref, v_ref = refs[:3]
        p_refs = refs[3:3 + np_]
        g_ref, d_ref, nm_ref, nv_ref = refs[3 + np_:]
        g = p_refs[0][...].astype(F32)
        for pr in p_refs[1:]:
            g = g + pr[...].astype(F32)
        nm = ADAM_B1 * m_ref[...] + (1.0 - ADAM_B1) * g
        nv = ADAM_B2 * v_ref[...] + (1.0 - ADAM_B2) * (g * g)
        g_ref[...] = g
        nm_ref[...] = nm
        nv_ref[...] = nv
        d_ref[...] = -ADAM_LR * ((nm * c1) / (jnp.sqrt(nv * c2) + ADAM_EPS) + ADAM_WD * w_ref[...])

    spec = pl.BlockSpec((tr, cols), lambda i: (i, 0))
    p_specs = []
    for arr, k in pieces:
        if k is None:
            p_specs.append(spec)
        else:
            p_specs.append(pl.BlockSpec((None, tr, cols), functools.partial(lambda i, kk: (kk, i, 0), kk=k)))
    out = _sds(w.shape, F32)
    return pl.pallas_call(body, name=name, grid=(rows // tr,), in_specs=[spec] * 3 + p_specs, out_specs=[spec] * 4,
                          out_shape=[out] * 4, compiler_params=_params(1))(w, m, v, *[a for a, _ in pieces])


def adamw_layers(w, m, v, pieces, name):
    _, rows, cols = w.shape
    tr = _row_tile(rows, 256)
    nt = rows // tr
    counts = [len(pieces[0]), len(pieces[1])]
    c1 = 1.0 / (1.0 - ADAM_B1 ** ADAM_STEP)
    c2 = 1.0 / (1.0 - ADAM_B2 ** ADAM_STEP)

    def body(*refs):
        w_ref, m_ref, v_ref = refs[:3]
        p_refs = refs[3:3 + sum(counts)]
        g_ref, d_ref, nm_ref, nv_ref = refs[3 + sum(counts):]
        sums = []
        for group in (p_refs[:counts[0]], p_refs[counts[0]:]):
            s = group[0][...].astype(F32)
            for pr in group[1:]:
                s = s + pr[...].astype(F32)
            sums.append(s)
        g = jnp.where(pl.program_id(0) == 0, sums[0], sums[1])
        nm = ADAM_B1 * m_ref[...] + (1.0 - ADAM_B1) * g
        nv = ADAM_B2 * v_ref[...] + (1.0 - ADAM_B2) * (g * g)
        g_ref[...] = g
        nm_ref[...] = nm
        nv_ref[...] = nv
        d_ref[...] = -ADAM_LR * ((nm * c1) / (jnp.sqrt(nv * c2) + ADAM_EPS) + ADAM_WD * w_ref[...])

    def rows_of(layer):
        parked = nt - 1 if layer == 0 else 0
        return lambda l, i: jnp.where(l == layer, i, parked)

    spec = pl.BlockSpec((None, tr, cols), lambda l, i: (l, i, 0))
    p_specs, p_args = [], []
    for layer in (0, 1):
        row_of = rows_of(layer)
        for arr, k in pieces[layer]:
            p_args.append(arr)
            if k is None:
                p_specs.append(pl.BlockSpec((tr, cols), functools.partial(lambda l, i, f: (f(l, i), 0), f=row_of)))
            else:
                p_specs.append(pl.BlockSpec((None, tr, cols), functools.partial(lambda l, i, f, kk: (kk, f(l, i), 0), f=row_of, kk=k)))
    out = _sds(w.shape, F32)
    return pl.pallas_call(body, name=name, grid=(2, nt), in_specs=[spec] * 3 + p_specs, out_specs=[spec] * 4,
                          out_shape=[out] * 4, compiler_params=_params(2))(w, m, v, *p_args)


def sum_slabs(g, name):
    n, rows, cols = g.shape

    def body(g_ref, o_ref):
        s = g_ref[0]
        for k in range(1, n):
            s = s + g_ref[k]
        o_ref[...] = s

    return pl.pallas_call(body, name=name, out_shape=_sds((rows, cols), F32))(g)


def norm_proj(h, g, w, name, comm=None):
    s_len = h.shape[0]
    nb, _, tn = w.shape
    tm = min(TMP, s_len)
    nt = s_len // tm

    def body(h_ref, g_ref, w_ref, o_ref, hn_ref, hn_s):
        rows = pl.ds(pl.multiple_of(pl.program_id(1) * tm, tm), tm)

        @pl.when(pl.program_id(0) == 0)
        def _():
            x = h_ref[...]
            hn = (x * _rstd(x) * g_ref[...]).astype(BF)
            hn_s[rows, :] = hn
            hn_ref[...] = hn
        o_ref[...] = _dot(hn_s[rows, :], w_ref[...]).astype(BF)

    def first_pass_rows(j, i):
        return (jnp.where(j == 0, i, nt - 1), 0)

    return _pcall(
        body, (h, g, w), name=name, grid=(nb, nt),
        in_specs=[pl.BlockSpec((tm, D), first_pass_rows), pl.BlockSpec((1, D), lambda j, i: (0, 0)),
                  pl.BlockSpec((None, D, tn), lambda j, i: (j, 0, 0))],
        out_specs=[pl.BlockSpec((tm, tn), lambda j, i: (i, j)), pl.BlockSpec((tm, D), first_pass_rows)],
        out_shape=[_sds((s_len, nb * tn), BF), _sds((s_len, D), BF)],
        scratch_shapes=[pltpu.VMEM((s_len, D), BF)], comm=comm)


def _sgu_mix(ws_ref, vln_s, mix_s, bs_ref, tm, transposed):
    mask = _sgu_mask(transposed)
    for hd in range(HEADS):
        wm = jnp.where(mask, ws_ref[hd], 0.0).astype(BF)
        cs = slice(hd * BLK, (hd + 1) * BLK)
        for n in range(tm // BLK):
            rs = slice(n * BLK, (n + 1) * BLK)
            r = _dot(wm, vln_s[rs, cs])
            mix_s[rs, cs] = r if bs_ref is None else r + bs_ref[:, cs]


def sgu_fwd(proj, ws, bsfull, gv, bv, wo, name):
    s_len = proj.shape[0]
    tm = min(TM, s_len)

    def body(zu_ref, zv_ref, ws_ref, bs_ref, gv_ref, bv_ref, wo_ref, sgu_ref, br_ref, vln_s, mix_s):
        u, _ = _gelu(zu_ref[...].astype(F32))
        v, _ = _gelu(zv_ref[...].astype(F32))
        vhat, _ = _ln_fwd(v)
        vln_s[...] = (vhat * gv_ref[...] + bv_ref[...]).astype(BF)
        _sgu_mix(ws_ref, vln_s, mix_s, bs_ref, tm, False)
        sgu = (u * mix_s[...]).astype(BF)
        sgu_ref[...] = sgu
        br_ref[...] = _dot(sgu, wo_ref[...].reshape(D, D)).astype(BF)

    return pl.pallas_call(
        body, name=name, grid=(s_len // tm,),
        in_specs=[_cols(tm, D, 0), _cols(tm, D, 1), _whole((HEADS, BLK, BLK)), _whole((BLK, D)), _whole((1, D)), _whole((1, D)),
                  _rowsharded(BLK)],
        out_specs=[_cols(tm, D, 0)] * 2, out_shape=[_sds((s_len, D), BF)] * 2,
        scratch_shapes=[pltpu.VMEM((tm, D), BF), pltpu.VMEM((tm, D), F32)], compiler_params=_params(1),
    )(proj, proj, ws, bsfull, gv, bv, wo)


def _causal_conv(ext_s, out_s, wdw_ref, bias_ref, tm):
    def chunk(ci, carry):
        r0 = pl.multiple_of((ci // (D // LC)) * RC, RC)
        l0 = pl.multiple_of((ci % (D // LC)) * LC, LC)
        win = ext_s[pl.ds(r0, RC + HALO), pl.ds(l0, LC)]
        acc = jnp.broadcast_to(bias_ref[:, pl.ds(l0, LC)], (RC, LC))
        for r in range(8):
            wr = win if r == 0 else pltpu.roll(win, r, 0)
            for m in range(4):
                d = 8 * m + r
                if d < CONV_W:
                    k = CONV_W - 1 - d
                    acc = acc + wdw_ref[k:k + 1, pl.ds(l0, LC)] * wr[HALO - 8 * m:HALO - 8 * m + RC]
        out_s[pl.ds(r0, RC), pl.ds(l0, LC)] = acc
        return carry
    lax.fori_loop(0, (tm // RC) * (D // LC), chunk, 0)


def _glu_ext(a_ref, g_ref, ah_ref, gh_ref, ext_s, first):
    hh = ah_ref[...].astype(F32) * _sig(gh_ref[...].astype(F32))
    ext_s[0:HALO, :] = jnp.where(first, 0.0, hh)
    ext_s[HALO:, :] = a_ref[...].astype(F32) * _sig(g_ref[...].astype(F32))


def conv_fwd(proj, wdw, bdw, gln, bln, wo, name, comm=None):
    s_len = proj.shape[0]
    tm = min(TM, s_len)

    def body(a_ref, g_ref, ah_ref, gh_ref, wdw_ref, bdw_ref, gln_ref, bln_ref, wo_ref, cv_ref, cb_ref, br_ref, ext_s, conv_s):
        _glu_ext(a_ref, g_ref, ah_ref, gh_ref, ext_s, pl.program_id(0) == 0)
        _causal_conv(ext_s, conv_s, wdw_ref, bdw_ref, tm)
        cv = conv_s[...].astype(BF)
        cv_ref[...] = cv
        chat, _ = _ln_fwd(cv.astype(F32))
        yl = chat * gln_ref[...] + bln_ref[...]
        cb = (yl * _sig(yl)).astype(BF)
        cb_ref[...] = cb
        br_ref[...] = _dot(cb, wo_ref[...].reshape(D, D)).astype(BF)

    return _pcall(
        body, (proj, proj, proj, proj, wdw, bdw, gln, bln, wo), name=name, grid=(s_len // tm,),
        in_specs=[_cols(tm, D, 2), _cols(tm, D, 3), _prev_halo(tm, D, 2), _prev_halo(tm, D, 3), _whole((HALO, D)),
                  _whole((1, D)), _whole((1, D)), _whole((1, D)), _rowsharded(BLK)],
        out_specs=[_cols(tm, D, 0)] * 3, out_shape=[_sds((s_len, D), BF)] * 3,
        scratch_shapes=[pltpu.VMEM((tm + HALO, D), F32), pltpu.VMEM((tm, D), F32)], comm=comm)


def pool_fwd(proj, wpool, spool, wo, name):
    s_len = proj.shape[0]
    tm = min(TM, s_len)

    def body(z_ref, zh_ref, wp_ref, sp_ref, wo_ref, pooled_ref, pm_ref, br_ref, ext_s, mr_s):
        i = pl.program_id(0)
        ext_s[0:HALO, :] = jnp.where(i == 0, 0.0, zh_ref[...].astype(F32))
        ext_s[HALO:, :] = z_ref[...].astype(F32)
        for gi, w in enumerate(POOL_WINDOWS):
            cs = slice(gi * POOL_GD, (gi + 1) * POOL_GD)
            e = ext_s[:, cs]
            s = e
            sh = 1
            while sh < w:
                s = s + pltpu.roll(s, sh, 0)
                sh *= 2
            pooled = (s[HALO:] * _inv_count(i, tm, w) - e[HALO:]).astype(BF)
            pooled_ref[:, cs] = pooled
            mr_s[:, cs] = _dot(pooled, wp_ref[gi])
        pm = (mr_s[...] * sp_ref[...]).astype(BF)
        pm_ref[...] = pm
        br_ref[...] = _dot(pm, wo_ref[...].reshape(D, D)).astype(BF)

    return pl.pallas_call(
        body, name=name, grid=(s_len // tm,),
        in_specs=[_cols(tm, D, 4), _prev_halo(tm, D, 4), _whole((4, POOL_GD, POOL_GD)), _whole((1, D)), _rowsharded(BLK)],
        out_specs=[_cols(tm, D, 0)] * 3, out_shape=[_sds((s_len, D), BF)] * 3,
        scratch_shapes=[pltpu.VMEM((tm + HALO, D), F32), pltpu.VMEM((tm, D), F32)], compiler_params=_params(1),
    )(proj, proj, wpool, spool, wo)


def merge_out(proj, bra, brb, brc, h, wout, gpost, name):
    s_len = h.shape[0]
    tm = min(TM, s_len)

    def body(z0, z1, z2, a_ref, b_ref, c_ref, h_ref, wo_ref, g_ref, mg_ref, mo_ref, h1_ref):
        merged = (_sig(z0[...].astype(F32)) * a_ref[...].astype(F32) + _sig(z1[...].astype(F32)) * b_ref[...].astype(F32)
                  + _sig(z2[...].astype(F32)) * c_ref[...].astype(F32)).astype(BF)
        mg_ref[...] = merged
        mo = _dot(merged, wo_ref[...].reshape(D, D))
        mo_ref[...] = mo.astype(BF)
        h1_ref[...] = h_ref[...] + mo * _rstd(mo) * g_ref[...]

    row = _cols(tm, D, 0)
    return pl.pallas_call(
        body, name=name, grid=(s_len // tm,),
        in_specs=[_cols(tm, D, 5), _cols(tm, D, 6), _cols(tm, D, 7), row, row, row, row, _rowsharded(BLK), _whole((1, D))],
        out_specs=[row] * 3, out_shape=[_sds((s_len, D), BF), _sds((s_len, D), BF), _sds((s_len, D), F32)],
        compiler_params=_params(1))(proj, proj, proj, bra, brb, brc, h, wout, gpost)


def _p_spec(tm, layer):
    return pl.BlockSpec((None, None, tm, PLE), lambda i: (layer, 0, i, 0))


def ffn_out(ff, h1, p, wfo, gpost, wpg, wple, layer, name):
    s_len = h1.shape[0]
    tm = min(TMB, s_len)

    def body(fg_ref, fu_ref, h1_ref, p_ref, wfo_ref, g_ref, wpg_ref, wple_ref, act_ref, f_ref, h2_ref, pg_ref, h3_ref):
        gt = fg_ref[...].astype(F32)
        act = (gt * _sig(gt) * fu_ref[...].astype(F32)).astype(BF)
        act_ref[...] = act
        f = _dot(act, wfo_ref[...].reshape(D_FF, D))
        f_ref[...] = f.astype(BF)
        h2 = h1_ref[...] + f * _rstd(f) * g_ref[...]
        h2_ref[...] = h2
        pg = _dot(h2.astype(BF), wpg_ref[...].reshape(D, D)).astype(BF)
        pg_ref[...] = pg
        pe = _dot(p_ref[...].astype(BF), wple_ref[...])
        h3_ref[...] = h2 + _sig(pg.astype(F32)) * pe

    row = _cols(tm, D, 0)
    return pl.pallas_call(
        body, name=name, grid=(s_len // tm,),
        in_specs=[_cols(tm, D_FF, 0), _cols(tm, D_FF, 1), row, _p_spec(tm, layer), _rowsharded(D_FF // N_DEV),
                  _whole((1, D)), _rowsharded(BLK), _whole((PLE, D))],
        out_specs=[_cols(tm, D_FF, 0), row, row, row, row],
        out_shape=[_sds((s_len, D_FF), BF), _sds((s_len, D), BF), _sds((s_len, D), F32), _sds((s_len, D), BF), _sds((s_len, D), F32)],
        compiler_params=_params(1))(ff, ff, h1, p, wfo, gpost, wpg, wple)


def loss_grad(y, target, name):
    s_len = y.shape[0]
    tm = min(TM, s_len)
    nt = s_len // tm

    def body(y_ref, t_ref, dy_ref, loss_ref, acc):
        i = pl.program_id(0)
        e = y_ref[...] - t_ref[...]
        dy_ref[...] = e * (1.0 / D)
        _accum(acc, i == 0, _colsum(e * e))

        @pl.when(i == nt - 1)
        def _():
            loss_ref[...] = jnp.broadcast_to(jnp.sum(acc[...], axis=1, keepdims=True) * (0.5 / D), (1, LC))

    row = _cols(tm, D, 0)
    return pl.pallas_call(
        body, name=name, grid=(nt,), in_specs=[row, row], out_specs=[row, _whole((1, LC))],
        out_shape=[_sds((s_len, D), F32), _sds((1, LC), F32)], scratch_shapes=[pltpu.VMEM((1, D), F32)],
        compiler_params=_params(1))(y, target)


def ple_ffn_bwd(dh3, pg, p, f, ff, wple, wpg, wfo, gpost, layer, name, comm=None):
    s_len = dh3.shape[0]
    tm = min(TMB, s_len)

    def body(dh3_ref, pg_ref, p_ref, f_ref, fg_ref, fu_ref, wple_ref, wpg_ref, wfo_ref, g_ref,
             dh2_ref, dpe_ref, dpg_ref, df_ref, dff_ref, dg_ref):
        i = pl.program_id(0)
        dh3v = dh3_ref[...]
        s = _sig(pg_ref[...].astype(F32))
        pe = _dot(p_ref[...].astype(BF), wple_ref[...])
        dpe_ref[...] = (dh3v * s).astype(BF)
        dpg = (dh3v * pe * s * (1.0 - s)).astype(BF)
        dpg_ref[...] = dpg
        dh2 = dh3v + _dot_nt(dpg, wpg_ref[...].reshape(D, D))
        dh2_ref[...] = dh2
        fv = f_ref[...].astype(F32)
        r = _rstd(fv)
        _accum(dg_ref, i == 0, _colsum(dh2 * fv * r))
        df = _rms_bwd(fv, dh2 * g_ref[...], r).astype(BF)
        df_ref[...] = df
        dact = _dot_nt(df, wfo_ref[...].reshape(D_FF, D))
        gt = fg_ref[...].astype(F32)
        sg = _sig(gt)
        up = fu_ref[...].astype(F32)
        dff_ref[:, 0:D_FF] = (dact * up * sg * (1.0 + gt * (1.0 - sg))).astype(BF)
        dff_ref[:, D_FF:2 * D_FF] = (dact * gt * sg).astype(BF)

    row = _cols(tm, D, 0)
    return _pcall(
        body, (dh3, pg, p, f, ff, ff, wple, wpg, wfo, gpost), name=name, grid=(s_len // tm,),
        in_specs=[row, row, _p_spec(tm, layer), row, _cols(tm, D_FF, 0), _cols(tm, D_FF, 1), _whole((PLE, D)),
                  _rowsharded(BLK), _rowsharded(D_FF // N_DEV), _whole((1, D))],
        out_specs=[row, row, row, row, _cols(tm, 2 * D_FF, 0), _whole((1, D))],
        out_shape=[_sds((s_len, D), F32), _sds((s_len, D), BF), _sds((s_len, D), BF), _sds((s_len, D), BF),
                   _sds((s_len, 2 * D_FF), BF), _sds((1, D), F32)], comm=comm)


def ffn_in_bwd(dff, w3, h1, dh2, gpre, name):
    s_len = h1.shape[0]
    tm = min(TMB, s_len)
    nb, _, tn = w3.shape

    def body(dff_ref, w_ref, h1_ref, dh2_ref, g_ref, dh1_ref, dg_ref):
        dhn = _dot_nt(dff_ref[:, 0:tn], w_ref[0])
        for j in range(1, nb):
            dhn = dhn + _dot_nt(dff_ref[:, j * tn:(j + 1) * tn], w_ref[j])
        x = h1_ref[...]
        r = _rstd(x)
        _accum(dg_ref, pl.program_id(0) == 0, _colsum(dhn * x * r))
        dh1_ref[...] = dh2_ref[...] + _rms_bwd(x, dhn * g_ref[...], r)

    row = _cols(tm, D, 0)
    return pl.pallas_call(
        body, name=name, grid=(s_len // tm,),
        in_specs=[_cols(tm, nb * tn, 0), pl.BlockSpec((nb, D, tn), lambda i: (0, 0, 0), pipeline_mode=pl.Buffered(1)), row, row,
                  _whole((1, D))],
        out_specs=[row, _whole((1, D))],
        out_shape=[_sds((s_len, D), F32), _sds((1, D), F32)], compiler_params=_params(1))(dff, w3, h1, dh2, gpre)


def mix_post_bwd(dh1, mo, gpost, wout, proj, bra, brb, brc, win, name, comm=None):
    s_len = dh1.shape[0]
    tm = min(TMB, s_len)

    def body(dh1_ref, mo_ref, g_ref, wo_ref, z0, z1, z2, a_ref, b_ref, c_ref, w5, w6, w7,
             dmo_ref, da_ref, db_ref, dc_ref, dz_ref, dhn_ref, dg_ref):
        i = pl.program_id(0)
        dh1v = dh1_ref[...]
        mo_v = mo_ref[...].astype(F32)
        r = _rstd(mo_v)
        _accum(dg_ref, i == 0, _colsum(dh1v * mo_v * r))
        dmo = _rms_bwd(mo_v, dh1v * g_ref[...], r).astype(BF)
        dmo_ref[...] = dmo
        dmerged = _dot_nt(dmo, wo_ref[...].reshape(D, D))
        dhn = jnp.zeros((tm, D), F32)
        for k, (z, br, dbr, w) in enumerate(((z0, a_ref, da_ref, w5), (z1, b_ref, db_ref, w6), (z2, c_ref, dc_ref, w7))):
            s = _sig(z[...].astype(F32))
            dbr[...] = (dmerged * s).astype(BF)
            dz = (dmerged * br[...].astype(F32) * s * (1.0 - s)).astype(BF)
            dz_ref[:, k * D:(k + 1) * D] = dz
            dhn = dhn + _dot_nt(dz, w[...])
        dhn_ref[...] = dhn

    row = _cols(tm, D, 0)
    return _pcall(
        body, (dh1, mo, gpost, wout, proj, proj, proj, bra, brb, brc, win, win, win), name=name, grid=(s_len // tm,),
        in_specs=[row, row, _whole((1, D)), _rowsharded(BLK), _cols(tm, D, 5), _cols(tm, D, 6), _cols(tm, D, 7), row, row, row,
                  _win_block(5), _win_block(6), _win_block(7)],
        out_specs=[row, row, row, row, _cols(tm, 3 * D, 0), row, _whole((1, D))],
        out_shape=[_sds((s_len, D), BF)] * 4 + [_sds((s_len, 3 * D), BF), _sds((s_len, D), F32), _sds((1, D), F32)], comm=comm)


def sgu_bwd(dbr, wo, proj, ws, wst, bsfull, gv, bv, win, name, comm=None):
    s_len = dbr.shape[0]
    tm = min(TMB, s_len)
    nt = s_len // tm

    def body(dbr_ref, wo_ref, zu_ref, zv_ref, ws_ref, wst_ref, bs_ref, gv_ref, bv_ref, w0, w1,
             dz_ref, dhn_ref, dws_ref, dbs_ref, dgv_ref, dbv_ref, vln_s, mix_s, dmix_s, dvln_s, bs_acc):
        i = pl.program_id(0)
        first = i == 0
        dsgu = _dot_nt(dbr_ref[...], wo_ref[...].reshape(D, D))
        zu = zu_ref[...].astype(F32)
        zv = zv_ref[...].astype(F32)
        u, tu = _gelu(zu)
        v, tv = _gelu(zv)
        vhat, rs = _ln_fwd(v)
        vln_s[...] = (vhat * gv_ref[...] + bv_ref[...]).astype(BF)
        _sgu_mix(ws_ref, vln_s, mix_s, bs_ref, tm, False)
        du = dsgu * mix_s[...]
        dmix = dsgu * u
        dmix_s[...] = dmix.astype(BF)
        blocks = dmix[0:BLK]
        for n in range(1, tm // BLK):
            blocks = blocks + dmix[n * BLK:(n + 1) * BLK]
        _accum(bs_acc, first, blocks)
        for hd in range(HEADS):
            cs = slice(hd * BLK, (hd + 1) * BLK)
            g = _dot_nt(dmix_s[0:BLK, cs], vln_s[0:BLK, cs])
            for n in range(1, tm // BLK):
                g = g + _dot_nt(dmix_s[n * BLK:(n + 1) * BLK, cs], vln_s[n * BLK:(n + 1) * BLK, cs])

            @pl.when(first)
            def _():
                dws_ref[hd] = g

            @pl.when(jnp.logical_not(first))
            def _():
                dws_ref[hd] += g
        _sgu_mix(wst_ref, dmix_s, dvln_s, None, tm, True)
        dvln = dvln_s[...]
        _accum(dgv_ref, first, _colsum(dvln * vhat))
        _accum(dbv_ref, first, _colsum(dvln))
        dv = _ln_bwd(dvln * gv_ref[...], vhat, rs)
        dzu = (du * _gelu_grad(zu, tu)).astype(BF)
        dzv = (dv * _gelu_grad(zv, tv)).astype(BF)
        dz_ref[:, 0:D] = dzu
        dz_ref[:, D:2 * D] = dzv
        dhn_ref[...] = _dot_nt(dzu, w0[...]) + _dot_nt(dzv, w1[...])

        @pl.when(i == nt - 1)
        def _():
            mask = _sgu_mask(False)
            for hd in range(HEADS):
                dws_ref[hd] = jnp.where(mask, dws_ref[hd], 0.0)
                dbs_ref[:, hd:hd + 1] = jnp.sum(bs_acc[:, hd * BLK:(hd + 1) * BLK], axis=1, keepdims=True)

    row = _cols(tm, D, 0)
    vec = _whole((1, D))
    return _pcall(
        body, (dbr, wo, proj, proj, ws, wst, bsfull, gv, bv, win, win), name=name, grid=(nt,),
        in_specs=[row, _rowsharded(BLK), _cols(tm, D, 0), _cols(tm, D, 1), _whole((HEADS, BLK, BLK)), _whole((HEADS, BLK, BLK)),
                  _whole((BLK, D)), vec, vec, _win_block(0), _win_block(1)],
        out_specs=[_cols(tm, 2 * D, 0), row, _whole((HEADS, BLK, BLK)), _whole((BLK, HEADS)), vec, vec],
        out_shape=[_sds((s_len, 2 * D), BF), _sds((s_len, D), F32), _sds((HEADS, BLK, BLK), F32), _sds((BLK, HEADS), F32),
                   _sds((1, D), F32), _sds((1, D), F32)],
        scratch_shapes=[pltpu.VMEM((tm, D), BF), pltpu.VMEM((tm, D), F32), pltpu.VMEM((tm, D), BF), pltpu.VMEM((tm, D), F32),
                        pltpu.VMEM((BLK, D), F32)], comm=comm)


def conv_bwd(dbr, wo, proj, conv, gln, bln, name, comm=None):
    s_len = dbr.shape[0]
    tm = min(TMB, s_len)
    nt = s_len // tm

    def body(dbr_ref, wo_ref, a_ref, g_ref, ah_ref, gh_ref, cv_ref, gln_ref, bln_ref,
             dc_ref, dw_ref, dbdw_ref, dgln_ref, dbln_ref, ext_s, dc_s, dw_acc):
        i = pl.program_id(0)
        first = i == 0
        dcb = _dot_nt(dbr_ref[...], wo_ref[...].reshape(D, D))
        _glu_ext(a_ref, g_ref, ah_ref, gh_ref, ext_s, first)
        chat, rs = _ln_fwd(cv_ref[...].astype(F32))
        yl = chat * gln_ref[...] + bln_ref[...]
        sy = _sig(yl)
        dyl = dcb * sy * (1.0 + yl * (1.0 - sy))
        _accum(dgln_ref, first, _colsum(dyl * chat))
        _accum(dbln_ref, first, _colsum(dyl))
        dc = _ln_bwd(dyl * gln_ref[...], chat, rs)
        _accum(dbdw_ref, first, _colsum(dc))
        dc_ref[...] = dc.astype(BF)
        dc_s[...] = dc

        @pl.when(first)
        def _():
            dw_acc[...] = jnp.zeros_like(dw_acc)

        def chunk(ci, carry):
            r0 = pl.multiple_of((ci // (D // LC)) * RC, RC)
            l0 = pl.multiple_of((ci % (D // LC)) * LC, LC)
            win = ext_s[pl.ds(r0, RC + HALO), pl.ds(l0, LC)]
            dcw = dc_s[pl.ds(r0, RC), pl.ds(l0, LC)]
            for r in range(8):
                wr = win if r == 0 else pltpu.roll(win, r, 0)
                for m in range(4):
                    d = 8 * m + r
                    if d < CONV_W:
                        k = CONV_W - 1 - d
                        prod = dcw * wr[HALO - 8 * m:HALO - 8 * m + RC]
                        dw_acc[k * 8:(k + 1) * 8, pl.ds(l0, LC)] += prod.reshape(RC // 8, 8, LC).sum(axis=0)
            return carry
        lax.fori_loop(0, (tm // RC) * (D // LC), chunk, 0)

        @pl.when(i == nt - 1)
        def _():
            dw_ref[...] = dw_acc[...].reshape(HALO, 8, D).sum(axis=1)

    row = _cols(tm, D, 0)
    vec = _whole((1, D))
    return _pcall(
        body, (dbr, wo, proj, proj, proj, proj, conv, gln, bln), name=name, grid=(nt,),
        in_specs=[row, _rowsharded(BLK), _cols(tm, D, 2), _cols(tm, D, 3), _prev_halo(tm, D, 2), _prev_halo(tm, D, 3),
                  row, vec, vec],
        out_specs=[row, _whole((HALO, D)), vec, vec, vec],
        out_shape=[_sds((s_len, D), BF), _sds((HALO, D), F32), _sds((1, D), F32), _sds((1, D), F32), _sds((1, D), F32)],
        scratch_shapes=[pltpu.VMEM((tm + HALO, D), F32), pltpu.VMEM((tm, D), F32), pltpu.VMEM((HALO * 8, D), F32)], comm=comm)


def pool_bwd(dbr, wo, pooled, wpool, spool, name):
    s_len = dbr.shape[0]
    tm = min(TMB, s_len)

    def body(dbr_ref, wo_ref, pl_ref, wp_ref, sp_ref, dmr_ref, q_ref, dsp_ref, mr_s):
        i = pl.program_id(0)
        dpm = _dot_nt(dbr_ref[...], wo_ref[...].reshape(D, D))
        for gi in range(4):
            cs = slice(gi * POOL_GD, (gi + 1) * POOL_GD)
            mr_s[:, cs] = _dot(pl_ref[:, cs], wp_ref[gi])
        _accum(dsp_ref, i == 0, _colsum(dpm * mr_s[...]))
        dmr = (dpm * sp_ref[...]).astype(BF)
        dmr_ref[...] = dmr
        for gi, w in enumerate(POOL_WINDOWS):
            cs = slice(gi * POOL_GD, (gi + 1) * POOL_GD)
            q_ref[:, cs] = (_dot_nt(dmr[:, cs], wp_ref[gi]) * _inv_count(i, tm, w)).astype(BF)

    row = _cols(tm, D, 0)
    return pl.pallas_call(
        body, name=name, grid=(s_len // tm,),
        in_specs=[row, _rowsharded(BLK), row, _whole((4, POOL_GD, POOL_GD)), _whole((1, D))],
        out_specs=[row, row, _whole((1, D))],
        out_shape=[_sds((s_len, D), BF), _sds((s_len, D), BF), _sds((1, D), F32)],
        scratch_shapes=[pltpu.VMEM((tm, D), F32)], compiler_params=_params(1))(dbr, wo, pooled, wpool, spool)


def seq_bwd(dc, q, proj, wdw, win, name, comm=None):
    s_len = dc.shape[0]
    tm = min(TMB, s_len)
    nt = s_len // tm

    def body(dc_ref, dch_ref, q_ref, qh_ref, a_ref, g_ref, wdw_ref, w2, w3, w4, dz_ref, dhn_ref, ext_s, dhc_s, qext_s):
        i = pl.program_id(0)
        last = i == nt - 1
        ext_s[0:tm, :] = dc_ref[...].astype(F32)
        ext_s[tm:, :] = jnp.where(last, 0.0, dch_ref[...].astype(F32))

        def chunk(ci, carry):
            r0 = pl.multiple_of((ci // (D // LC)) * RC, RC)
            l0 = pl.multiple_of((ci % (D // LC)) * LC, LC)
            win_ = ext_s[pl.ds(r0, RC + HALO), pl.ds(l0, LC)]
            acc = jnp.zeros((RC, LC), F32)
            for r in range(8):
                wr = win_ if r == 0 else pltpu.roll(win_, RC + HALO - r, 0)
                for m in range(4):
                    d = 8 * m + r
                    if d < CONV_W:
                        k = CONV_W - 1 - d
                        acc = acc + wdw_ref[k:k + 1, pl.ds(l0, LC)] * wr[8 * m:8 * m + RC]
            dhc_s[pl.ds(r0, RC), pl.ds(l0, LC)] = acc
            return carry
        lax.fori_loop(0, (tm // RC) * (D // LC), chunk, 0)

        dhc = dhc_s[...]
        av = a_ref[...].astype(F32)
        sg = _sig(g_ref[...].astype(F32))
        da = (dhc * sg).astype(BF)
        dg = (dhc * av * sg * (1.0 - sg)).astype(BF)
        dz_ref[:, 0:D] = da
        dz_ref[:, D:2 * D] = dg

        qext_s[0:tm, :] = q_ref[...].astype(F32)
        qext_s[tm:, :] = jnp.where(last, 0.0, qh_ref[...].astype(F32))
        for gi, w in enumerate(POOL_WINDOWS):
            cs = slice(gi * POOL_GD, (gi + 1) * POOL_GD)
            e = qext_s[:, cs]
            s = e
            sh = 1
            while sh < w:
                s = s + pltpu.roll(s, tm + HALO - sh, 0)
                sh *= 2
            dz_ref[:, 2 * D + gi * POOL_GD:2 * D + (gi + 1) * POOL_GD] = (s[0:tm] - e[0:tm] * _count(i, tm, w)).astype(BF)
        dhn_ref[...] = _dot_nt(da, w2[...]) + _dot_nt(dg, w3[...]) + _dot_nt(dz_ref[:, 2 * D:3 * D], w4[...])

    row = _cols(tm, D, 0)
    return _pcall(
        body, (dc, dc, q, q, proj, proj, wdw, win, win, win), name=name, grid=(nt,),
        in_specs=[row, _next_halo(tm, D, 0, s_len), row, _next_halo(tm, D, 0, s_len), _cols(tm, D, 2), _cols(tm, D, 3),
                  _whole((HALO, D)), _win_block(2), _win_block(3), _win_block(4)],
        out_specs=[_cols(tm, 3 * D, 0), row],
        out_shape=[_sds((s_len, 3 * D), BF), _sds((s_len, D), F32)],
        scratch_shapes=[pltpu.VMEM((tm + HALO, D), F32), pltpu.VMEM((tm, D), F32), pltpu.VMEM((tm + HALO, D), F32)], comm=comm)


def mix_pre_bwd(h, dh1, dhn_a, dhn_c, dhn_g, gpre, name):
    s_len = h.shape[0]
    tm = min(TM, s_len)

    def body(h_ref, dh1_ref, a_ref, c_ref, g3_ref, g_ref, dh_ref, dg_ref):
        x = h_ref[...]
        r = _rstd(x)
        dhn = a_ref[...] + c_ref[...] + g3_ref[...]
        _accum(dg_ref, pl.program_id(0) == 0, _colsum(dhn * x * r))
        dh_ref[...] = dh1_ref[...] + _rms_bwd(x, dhn * g_ref[...], r)

    row = _cols(tm, D, 0)
    return pl.pallas_call(
        body, name=name, grid=(s_len // tm,), in_specs=[row] * 5 + [_whole((1, D))], out_specs=[row, _whole((1, D))],
        out_shape=[_sds((s_len, D), F32), _sds((1, D), F32)], compiler_params=_params(1))(h, dh1, dhn_a, dhn_c, dhn_g, gpre)


def wgrad(a, b, tk, tn, name, stacked=False, diag=False, a_spec=None, comm=None):
    s_len = b.shape[0]
    k_dim = a.shape[-1]
    n_dim = b.shape[1]
    ts = min(TS, s_len)
    nk = 1 if diag else k_dim // tk
    nn, ns = n_dim // tn, s_len // ts

    def body(a_ref, b_ref, o_ref, acc):
        s = pl.program_id(2)
        _accum(acc, s == 0, _dot_tn(a_ref[...].astype(BF), b_ref[...].astype(BF)))

        @pl.when(s == ns - 1)
        def _():
            o_ref[...] = acc[...].astype(BF).reshape(o_ref.shape)

    if a_spec is None:
        a_spec = pl.BlockSpec((ts, tk), (lambda k, n, s: (s, n)) if diag else (lambda k, n, s: (s, k)))
    if stacked or diag:
        out_shape = _sds((nn, tk if diag else k_dim, tn), BF)
        o_spec = pl.BlockSpec((1, tk, tn), lambda k, n, s: (n, k, 0))
    else:
        out_shape = _sds((k_dim, n_dim), BF)
        o_spec = pl.BlockSpec((tk, tn), lambda k, n, s: (k, n))
    (out,), got = _pcall(
        body, (a, b), name=name, grid=(nk, nn, ns),
        in_specs=[a_spec, pl.BlockSpec((ts, tn), lambda k, n, s: (s, n))], out_specs=[o_spec], out_shape=[out_shape],
        scratch_shapes=[pltpu.VMEM((tk, tn), F32)], comm=comm)
    return out if comm is None else (out, got)


_WEIGHTS = ['g_mix_pre', 'w_in', 'w_sgu_s', 'b_sgu_s', 'g_sgu_v', 'b_sgu_v', 'w_sgu_out', 'w_dw', 'b_dw', 'g_conv_ln', 'b_conv_ln',
            'w_conv_out', 'w_pool', 's_pool', 'w_pool_out', 'w_out', 'g_mix_post', 'g_ffn_pre', 'w_ffn_in', 'w_ffn_out', 'g_ffn_post',
            'w_ple', 'w_ple_gate']
_SHARDED = ['w_in', 'w_sgu_out', 'w_conv_out', 'w_pool', 'w_pool_out', 'w_out', 'w_ffn_in', 'w_ffn_out', 'w_ple', 'w_ple_gate']
_VECTORS = ['g_mix_pre', 'g_sgu_v', 'b_sgu_v', 'b_dw', 'g_conv_ln', 'b_conv_ln', 's_pool', 'g_mix_post', 'g_ffn_pre', 'g_ffn_post']
_SUBLANES = 8
_SGU_ROWS = HEADS * BLK * BLK // D
_REP_ROWS = _SUBLANES * (len(_VECTORS) + 2) + _SGU_ROWS


def _pack_replicated(t, layer):
    rows = [jnp.pad(t[k][layer].reshape(1, D), ((0, _SUBLANES - 1), (0, 0))) for k in _VECTORS + ['b_sgu_s']]
    return jnp.concatenate(rows + [t['w_sgu_s'][layer].reshape(_SGU_ROWS, D), jnp.zeros((_SUBLANES, D), F32)], axis=0)


def _unpack_replicated(packed):
    out = {}
    for i, k in enumerate(_VECTORS):
        out[k] = packed[:, _SUBLANES * i, :]
    o = _SUBLANES * len(_VECTORS)
    out['b_sgu_s'] = packed[:, o, :].reshape(2, HEADS, BLK)
    out['w_sgu_s'] = packed[:, o + _SUBLANES:o + _SUBLANES + _SGU_ROWS, :].reshape(2, HEADS, BLK, BLK)
    return out


def _pad_taps(w):
    return jnp.pad(w, ((0, HALO - CONV_W), (0, 0)))


def kernel(x, p, g_mix_pre, w_in, w_sgu_s, b_sgu_s, g_sgu_v, b_sgu_v, w_sgu_out, w_dw, b_dw, g_conv_ln, b_conv_ln, w_conv_out, w_pool, s_pool, w_pool_out, w_out, g_mix_post, g_ffn_pre, w_ffn_in, w_ffn_out, g_ffn_post, w_ple, w_ple_gate, loss_target, m_g_mix_pre, m_w_in, m_w_sgu_s, m_b_sgu_s, m_g_sgu_v, m_b_sgu_v, m_w_sgu_out, m_w_dw, m_b_dw, m_g_conv_ln, m_b_conv_ln, m_w_conv_out, m_w_pool, m_s_pool, m_w_pool_out, m_w_out, m_g_mix_post, m_g_ffn_pre, m_w_ffn_in, m_w_ffn_out, m_g_ffn_post, m_w_ple, m_w_ple_gate, v_g_mix_pre, v_w_in, v_w_sgu_s, v_b_sgu_s, v_g_sgu_v, v_b_sgu_v, v_w_sgu_out, v_w_dw, v_b_dw, v_g_conv_ln, v_b_conv_ln, v_w_conv_out, v_w_pool, v_s_pool, v_w_pool_out, v_w_out, v_g_mix_post, v_g_ffn_pre, v_w_ffn_in, v_w_ffn_out, v_g_ffn_post, v_w_ple, v_w_ple_gate):
    W = dict(g_mix_pre=g_mix_pre, w_in=w_in, w_sgu_s=w_sgu_s, b_sgu_s=b_sgu_s, g_sgu_v=g_sgu_v, b_sgu_v=b_sgu_v, w_sgu_out=w_sgu_out,
             w_dw=w_dw, b_dw=b_dw, g_conv_ln=g_conv_ln, b_conv_ln=b_conv_ln, w_conv_out=w_conv_out, w_pool=w_pool, s_pool=s_pool,
             w_pool_out=w_pool_out, w_out=w_out, g_mix_post=g_mix_post, g_ffn_pre=g_ffn_pre, w_ffn_in=w_ffn_in, w_ffn_out=w_ffn_out,
             g_ffn_post=g_ffn_post, w_ple=w_ple, w_ple_gate=w_ple_gate)
    M = dict(g_mix_pre=m_g_mix_pre, w_in=m_w_in, w_sgu_s=m_w_sgu_s, b_sgu_s=m_b_sgu_s, g_sgu_v=m_g_sgu_v, b_sgu_v=m_b_sgu_v,
             w_sgu_out=m_w_sgu_out, w_dw=m_w_dw, b_dw=m_b_dw, g_conv_ln=m_g_conv_ln, b_conv_ln=m_b_conv_ln, w_conv_out=m_w_conv_out,
             w_pool=m_w_pool, s_pool=m_s_pool, w_pool_out=m_w_pool_out, w_out=m_w_out, g_mix_post=m_g_mix_post, g_ffn_pre=m_g_ffn_pre,
             w_ffn_in=m_w_ffn_in, w_ffn_out=m_w_ffn_out, g_ffn_post=m_g_ffn_post, w_ple=m_w_ple, w_ple_gate=m_w_ple_gate)
    V = dict(g_mix_pre=v_g_mix_pre, w_in=v_w_in, w_sgu_s=v_w_sgu_s, b_sgu_s=v_b_sgu_s, g_sgu_v=v_g_sgu_v, b_sgu_v=v_b_sgu_v,
             w_sgu_out=v_w_sgu_out, w_dw=v_w_dw, b_dw=v_b_dw, g_conv_ln=v_g_conv_ln, b_conv_ln=v_b_conv_ln, w_conv_out=v_w_conv_out,
             w_pool=v_w_pool, s_pool=v_s_pool, w_pool_out=v_w_pool_out, w_out=v_w_out, g_mix_post=v_g_mix_post, g_ffn_pre=v_g_ffn_pre,
             w_ffn_in=v_w_ffn_in, w_ffn_out=v_w_ffn_out, g_ffn_post=v_g_ffn_post, w_ple=v_w_ple, w_ple_gate=v_w_ple_gate)

    my_c = lax.axis_index("c")
    my_chip = 2 * lax.axis_index("x") + lax.axis_index("y")
    my_dev = 2 * my_chip + my_c
    s_len = x.shape[1]
    h0 = x.reshape(s_len, D)
    target = loss_target.reshape(s_len, D)

    shard = [{k: W[k][l].astype(BF) for k in _SHARDED} for l in range(2)]
    for l in range(2):
        shard[l]['w_dw'] = w_dw[l]
    rest = [k for k in _SHARDED if k != 'w_in'] + ['w_dw']
    hosted_gather = {
        'norm_proj_in': (0, rest),
        'conv_fwd': (1, ['w_in', 'w_ffn_in']),
        'norm_proj_ffn': (1, [k for k in rest if k != 'w_ffn_in']),
    }
    G = [{'w_in': run_comm(Gather([shard[0]['w_in']]), "gather_w_in_0")[0]}, {}]

    def gather_in(layer, call):
        if layer != 0:
            return None, (lambda got: None)
        to_layer, keys = hosted_gather[call]
        return Gather([shard[to_layer][k] for k in keys]), (lambda got: G[to_layer].update(zip(keys, got)))

    def natural(g):
        wfi = jnp.transpose(g['w_ffn_in'].reshape(4, 2, D, D_FF // 4), (0, 2, 1, 3)).reshape(4, D, D_FF // 2)
        wpool = jnp.transpose(g['w_pool'], (1, 0, 2, 3)).reshape(4, POOL_GD, POOL_GD)
        wple = jnp.transpose(g['w_ple'], (1, 0, 2)).reshape(PLE, D)
        wdw = jnp.transpose(g['w_dw'].reshape(N_DEV, CONV_W, BLK), (1, 0, 2)).reshape(CONV_W, D)
        return dict(wfi=wfi, wpool=wpool, wple=wple, wdw=_pad_taps(wdw))

    def vec(name, layer):
        return W[name][layer].reshape(1, D)

    saved = []
    h = h0
    for l in range(2):
        g = G[l]
        comm, land = gather_in(l, 'norm_proj_in')
        (proj, hn), got = norm_proj(h, vec('g_mix_pre', l), g['w_in'], f"norm_proj_in_{l}", comm=comm)
        land(got)
        nat = natural(g)
        bsfull = jnp.repeat(b_sgu_s[l].T, BLK, axis=1)
        wst = jnp.swapaxes(w_sgu_s[l], 1, 2)
        sgu, bra = sgu_fwd(proj, w_sgu_s[l], bsfull, vec('g_sgu_v', l), vec('b_sgu_v', l), g['w_sgu_out'], f"sgu_fwd_{l}")
        comm, land = gather_in(l, 'conv_fwd')
        (conv, cb, brb), got = conv_fwd(proj, nat['wdw'], vec('b_dw', l), vec('g_conv_ln', l), vec('b_conv_ln', l), g['w_conv_out'],
                                        f"conv_fwd_{l}", comm=comm)
        land(got)
        pooled, pm, brc = pool_fwd(proj, nat['wpool'], vec('s_pool', l), g['w_pool_out'], f"pool_fwd_{l}")
        merged, mo, h1 = merge_out(proj, bra, brb, brc, h, g['w_out'], vec('g_mix_post', l), f"merge_out_{l}")
        comm, land = gather_in(l, 'norm_proj_ffn')
        (ff, hn2), got = norm_proj(h1, vec('g_ffn_pre', l), nat['wfi'], f"norm_proj_ffn_{l}", comm=comm)
        land(got)
        act, f, h2, pg, h3 = ffn_out(ff, h1, p, g['w_ffn_out'], vec('g_ffn_post', l), g['w_ple_gate'], nat['wple'], l, f"ffn_out_{l}")
        saved.append(dict(h=h, nat=nat, bsfull=bsfull, wst=wst, proj=proj, hn=hn, sgu=sgu, bra=bra, conv=conv, cb=cb, brb=brb,
                          pooled=pooled, pm=pm, brc=brc, merged=merged, mo=mo, h1=h1, ff=ff, hn2=hn2, act=act, f=f, h2=h2, pg=pg))
        h = h3

    dh, loss_row = loss_grad(h, target, "loss_grad")
    loss = lax.psum(loss_row[0, 0], ("x", "y", "c"))

    parts = {k: [None, None] for k in _SHARDED}
    small = {k: [None, None] for k in _VECTORS + ['b_sgu_s', 'w_sgu_s', 'w_dw']}
    chip_parts = [{}, {}]
    from_chips = [{}, {}]
    gathered_small = [None, None]

    def to_sibling(layer, keys):
        return ToSibling([parts[k][layer] for k in keys])

    def add_siblings(layer, keys, from_sibling):
        for k, rv in zip(keys, from_sibling):
            st = parts[k][layer]
            mine = lax.dynamic_index_in_dim(st.reshape((4, 2) + st.shape[1:]), my_c, axis=1, keepdims=False)
            cols = st.shape[-1]
            chip_parts[layer][k] = add_bf16(mine.reshape(-1, cols), rv.reshape(-1, cols), f"rs_add_{k}_{layer}").reshape(rv.shape)

    def to_chips(layer, keys):
        return ToChips([chip_parts[layer][k] for k in keys])

    def small_pack(layer):
        return jnp.concatenate([_pack_replicated(small, layer), small['w_dw'][layer]], axis=0)

    ffn_group = ['w_ffn_in', 'w_ffn_out', 'w_ple_gate', 'w_ple']
    mix_group = ['w_out', 'w_sgu_out', 'w_conv_out', 'w_pool_out', 'w_pool']
    big = ['w_in', 'w_ffn_in']
    others = [k for k in _SHARDED if k not in big]
    hosted_rs = {
        'ple_ffn_bwd': (lambda: to_sibling(1, _SHARDED), lambda got: add_siblings(1, _SHARDED, got)),
        'mix_post_bwd': (lambda: to_sibling(0, ffn_group), lambda got: add_siblings(0, ffn_group, got)),
        'sgu_bwd': (lambda: to_chips(1, big), lambda got: from_chips[1].update(zip(big, got))),
        'conv_bwd': (lambda: to_chips(1, others), lambda got: from_chips[1].update(zip(others, got))),
        'seq_bwd': (lambda: to_chips(0, ffn_group), lambda got: from_chips[0].update(zip(ffn_group, got))),
        'wgrad_in_sgu': (lambda: to_sibling(0, mix_group), lambda got: add_siblings(0, mix_group, got)),
        'wgrad_in_seq': (lambda: to_chips(0, mix_group), lambda got: from_chips[0].update(zip(mix_group, got))),
        'wgrad_in_gate': (lambda: Gather([small_pack(1)]), lambda got: gathered_small.__setitem__(1, got[0])),
    }

    def exchange_in(layer, call):
        if layer != 0 or call not in hosted_rs:
            return None, (lambda got: None)
        make, land = hosted_rs[call]
        return make(), land

    for l in (1, 0):
        sv, g, nat = saved[l], G[l], saved[l]['nat']

        def wg(call, a, b, tk, tn, **kw):
            comm, land = exchange_in(l, call)
            if comm is None:
                return wgrad(a, b, tk, tn, f"{call}_{l}", **kw)
            out, got = wgrad(a, b, tk, tn, f"{call}_{l}", comm=comm, **kw)
            land(got)
            return out

        comm, land = exchange_in(l, 'ple_ffn_bwd')
        (dh2, dpe, dpg, df, dff, small['g_ffn_post'][l]), got = ple_ffn_bwd(
            dh, sv['pg'], p, sv['f'], sv['ff'], nat['wple'], g['w_ple_gate'], g['w_ffn_out'], vec('g_ffn_post', l), l,
            f"ple_ffn_bwd_{l}", comm=comm)
        land(got)
        dh1, small['g_ffn_pre'][l] = ffn_in_bwd(dff, nat['wfi'], sv['h1'], dh2, vec('g_ffn_pre', l), f"ffn_in_bwd_{l}")
        p_spec = pl.BlockSpec((None, None, min(TS, s_len), PLE), functools.partial(lambda k, n, s, ll: (ll, 0, s, 0), ll=l))
        parts['w_ple'][l] = jnp.transpose(wg('wgrad_ple', p, dpe, PLE, D, a_spec=p_spec).reshape(PLE, N_DEV, BLK), (1, 0, 2))
        parts['w_ple_gate'][l] = wg('wgrad_ple_gate', sv['h2'], dpg, D, D).reshape(N_DEV, BLK, D)
        parts['w_ffn_out'][l] = wg('wgrad_ffn_out', sv['act'], df, D_FF // 2, D).reshape(N_DEV, D_FF // N_DEV, D)
        g_fi = wg('wgrad_ffn_in', sv['hn2'], dff, D, D_FF // 2, stacked=True)
        parts['w_ffn_in'][l] = jnp.transpose(g_fi.reshape(4, D, 2, D_FF // 4), (0, 2, 1, 3)).reshape(N_DEV, D, D_FF // 4)

        comm, land = exchange_in(l, 'mix_post_bwd')
        (dmo, dbra, dbrb, dbrc, dzg, dhn_g, small['g_mix_post'][l]), got = mix_post_bwd(
            dh1, sv['mo'], vec('g_mix_post', l), g['w_out'], sv['proj'], sv['bra'], sv['brb'], sv['brc'], g['w_in'],
            f"mix_post_bwd_{l}", comm=comm)
        land(got)
        comm, land = exchange_in(l, 'sgu_bwd')
        (dzs, dhn_a, dws, dbs, small['g_sgu_v'][l], small['b_sgu_v'][l]), got = sgu_bwd(
            dbra, g['w_sgu_out'], sv['proj'], w_sgu_s[l], sv['wst'], sv['bsfull'], vec('g_sgu_v', l), vec('b_sgu_v', l), g['w_in'],
            f"sgu_bwd_{l}", comm=comm)
        land(got)
        small['w_sgu_s'][l] = dws
        small['b_sgu_s'][l] = dbs.T
        comm, land = exchange_in(l, 'conv_bwd')
        (dc, dwdw, small['b_dw'][l], small['g_conv_ln'][l], small['b_conv_ln'][l]), got = conv_bwd(
            dbrb, g['w_conv_out'], sv['proj'], sv['conv'], vec('g_conv_ln', l), vec('b_conv_ln', l), f"conv_bwd_{l}", comm=comm)
        land(got)
        small['w_dw'][l] = dwdw
        dmr, q, small['s_pool'][l] = pool_bwd(dbrc, g['w_pool_out'], sv['pooled'], nat['wpool'], vec('s_pool', l), f"pool_bwd_{l}")
        comm, land = exchange_in(l, 'seq_bwd')
        (dzc, dhn_c), got = seq_bwd(dc, q, sv['proj'], nat['wdw'], g['w_in'], f"seq_bwd_{l}", comm=comm)
        land(got)
        dh, small['g_mix_pre'][l] = mix_pre_bwd(sv['h'], dh1, dhn_a, dhn_c, dhn_g, vec('g_mix_pre', l), f"mix_pre_bwd_{l}")

        parts['w_out'][l] = wg('wgrad_out', sv['merged'], dmo, D, D).reshape(N_DEV, BLK, D)
        parts['w_sgu_out'][l] = wg('wgrad_sgu_out', sv['sgu'], dbra, D, D).reshape(N_DEV, BLK, D)
        parts['w_conv_out'][l] = wg('wgrad_conv_out', sv['cb'], dbrb, D, D).reshape(N_DEV, BLK, D)
        parts['w_pool_out'][l] = wg('wgrad_pool_out', sv['pm'], dbrc, D, D).reshape(N_DEV, BLK, D)
        g_pool = wg('wgrad_pool', sv['pooled'], dmr, POOL_GD, POOL_GD, diag=True)
        parts['w_pool'][l] = jnp.transpose(g_pool.reshape(4, N_DEV, POOL_GD // N_DEV, POOL_GD), (1, 0, 2, 3))
        parts['w_in'][l] = jnp.concatenate([
            wg('wgrad_in_sgu', sv['hn'], dzs, D, D, stacked=True),
            wg('wgrad_in_seq', sv['hn'], dzc, D, D, stacked=True),
            wg('wgrad_in_gate', sv['hn'], dzg, D, D, stacked=True)], axis=0)
    grad_x = dh.reshape(1, s_len, D)

    add_siblings(0, ['w_in'], run_comm(to_sibling(0, ['w_in']), "rs_to_sibling_w_in_0"))
    from_chips[0]['w_in'] = run_comm(to_chips(0, ['w_in']), "rs_to_chips_w_in_0")[0]
    gathered_small[0] = run_comm(Gather([small_pack(0)]), "all_gather_small_grads_0")[0]

    outs = {}
    for k in _SHARDED:
        cols = W[k].shape[-1]
        pieces = []
        for layer in range(2):
            own = lax.dynamic_index_in_dim(chip_parts[layer][k], my_chip, axis=0, keepdims=False).reshape(-1, cols)
            rv3 = from_chips[layer][k].reshape(3, -1, cols)
            pieces.append([(own, None), (rv3, 0), (rv3, 1), (rv3, 2)])
        res = adamw_layers(W[k].reshape(2, -1, cols), M[k].reshape(2, -1, cols), V[k].reshape(2, -1, cols), pieces, f"adamw_{k}")
        outs[k] = [r.reshape(W[k].shape) for r in res]

    packed = [jnp.stack([_pack_replicated(t, 0), _pack_replicated(t, 1)], axis=0) for t in (W, M, V)]
    rep_res = adamw_layers(*packed, [[(gathered_small[layer], d) for d in range(N_DEV)] for layer in range(2)], "adamw_replicated")
    for idx, res in enumerate(rep_res):
        for name, val in _unpack_replicated(res).items():
            outs.setdefault(name, [None] * 4)[idx] = val
    dw_sum = jnp.stack([sum_slabs(gathered_small[layer][:, _REP_ROWS:, :], f"sum_w_dw_{layer}") for layer in range(2)], axis=0)
    dw_mine = lax.dynamic_slice_in_dim(dw_sum[:, :CONV_W], my_dev * BLK, BLK, axis=2)
    res = adamw(w_dw.reshape(2 * CONV_W, BLK), m_w_dw.reshape(2 * CONV_W, BLK), v_w_dw.reshape(2 * CONV_W, BLK),
                [(dw_mine.reshape(2 * CONV_W, BLK), None)], "adamw_w_dw")
    outs['w_dw'] = [r.reshape(w_dw.shape) for r in res]

    result = [loss, grad_x]
    for idx in range(4):
        result += [outs[k][idx] for k in _WEIGHTS]
    return tuple(result)
```

```python
import functools
import math

import jax
import jax.numpy as jnp
from jax import lax
from jax.experimental import pallas as pl
from jax.experimental.pallas import tpu as pltpu

F32 = jnp.float32
BF = jnp.bfloat16

D = 1024
D_FF = 2816
PLE = 256
N_DEV = 8
HEADS = 8
BLK = 128
CHUNK = 64
CONV_W = 31
POOL_WINDOWS = (2, 4, 8, 16)
POOL_GD = 256
EPS = 1e-6

V7X_VMEM_BYTES = 64 * 2**20
VMEM_LIMIT = V7X_VMEM_BYTES * 7 // 8
HALO = 32
RC = 64
LC = 128
TM = 512
TMB = 256
TMP = 1024
TS = 2048

ADAM_LR, ADAM_B1, ADAM_B2, ADAM_EPS, ADAM_WD, ADAM_STEP = 0.001, 0.9, 0.999, 1e-08, 0.01, 10

MESH = pl.DeviceIdType.MESH
ANY = pl.BlockSpec(memory_space=pl.ANY)

_GELU_K0 = math.sqrt(2.0 / math.pi)
_GELU_K1 = 0.044715


def _dot(a, b):
    return jnp.dot(a, b, preferred_element_type=F32)


def _dot_nt(a, b):
    return lax.dot_general(a, b, (((1,), (1,)), ((), ())), preferred_element_type=F32)


def _dot_tn(a, b):
    return lax.dot_general(a, b, (((0,), (0,)), ((), ())), preferred_element_type=F32)


def _sig(x):
    return 1.0 / (1.0 + jnp.exp(-x))


def _gelu(x):
    t = jnp.tanh(_GELU_K0 * (x + _GELU_K1 * x * x * x))
    return 0.5 * x * (1.0 + t), t


def _gelu_grad(x, t):
    return 0.5 * (1.0 + t) + 0.5 * x * (1.0 - t * t) * _GELU_K0 * (1.0 + 3.0 * _GELU_K1 * x * x)


def _rstd(x):
    return lax.rsqrt(jnp.mean(x * x, axis=-1, keepdims=True) + EPS)


def _rms_bwd(x, gd, r):
    return r * gd - x * (r * r * r) * jnp.mean(gd * x, axis=-1, keepdims=True)


def _ln_fwd(x):
    mu = jnp.mean(x, axis=-1, keepdims=True)
    xc = x - mu
    rs = lax.rsqrt(jnp.mean(xc * xc, axis=-1, keepdims=True) + EPS)
    return xc * rs, rs


def _ln_bwd(dhat, hat, rs):
    return rs * (dhat - jnp.mean(dhat, axis=-1, keepdims=True) - hat * jnp.mean(dhat * hat, axis=-1, keepdims=True))


def _colsum(x):
    return jnp.sum(x, axis=0, keepdims=True)


def _accum(ref, first, val):
    @pl.when(first)
    def _():
        ref[...] = val

    @pl.when(jnp.logical_not(first))
    def _():
        ref[...] += val


def _sgu_mask(transposed):
    r = lax.broadcasted_iota(jnp.int32, (BLK, BLK), 0) // CHUNK
    c = lax.broadcasted_iota(jnp.int32, (BLK, BLK), 1) // CHUNK
    return (r <= c) if transposed else (c <= r)


def _inv_count(i, tm, w):
    t = lax.broadcasted_iota(jnp.int32, (tm, 1), 0) + i * tm
    return 1.0 / jnp.minimum(t + 1, w).astype(F32)


def _count(i, tm, w):
    t = lax.broadcasted_iota(jnp.int32, (tm, 1), 0) + i * tm
    return jnp.minimum(t + 1, w).astype(F32)


def _params(n_grid):
    return pltpu.CompilerParams(dimension_semantics=("arbitrary",) * n_grid, vmem_limit_bytes=VMEM_LIMIT)


def _sds(shape, dtype):
    return jax.ShapeDtypeStruct(shape, dtype)


def _cols(tm, width, cb):
    return pl.BlockSpec((tm, width), lambda i: (i, cb))


def _whole(shape):
    nd = len(shape)
    return pl.BlockSpec(shape, lambda i: (0,) * nd)


def _prev_halo(tm, width, cb):
    return pl.BlockSpec((HALO, width), lambda i: (jnp.maximum(i * (tm // HALO) - 1, 0), cb))


def _next_halo(tm, width, cb, s_len):
    last = s_len // HALO - 1
    return pl.BlockSpec((HALO, width), lambda i: (jnp.minimum((i + 1) * (tm // HALO), last), cb))


def _rowsharded(rows):
    return pl.BlockSpec((N_DEV, rows, D), lambda i: (0, 0, 0))


def _win_block(j):
    return pl.BlockSpec((None, D, D), lambda i: (j, 0, 0))


def _row_tile(rows, cap):
    t = min(rows, cap)
    while rows % t or t % 16:
        t -= 16
    return t


class Gather:
    def __init__(self, arrs):
        self.arrs = list(arrs)
        n = self.n = len(self.arrs)
        self.out_shape = [_sds((N_DEV,) + a.shape, a.dtype) for a in self.arrs]
        self.scratch = [pltpu.SemaphoreType.DMA((n, 7)), pltpu.SemaphoreType.DMA((n, 7)), pltpu.SemaphoreType.DMA((n,))]

    def _plan(self, ins, outs, sems):
        send, recv, local = sems
        x, y, c = lax.axis_index("x"), lax.axis_index("y"), lax.axis_index("c")
        me, sibling = (x, y, c), (x, y, 1 - c)
        chips = [(1 - x, y), (x, 1 - y), (1 - x, 1 - y)]

        def copy(a, k, block, to, src=None):
            dst = outs[a].at[4 * block[0] + 2 * block[1] + block[2]]
            return pltpu.make_async_remote_copy(
                src_ref=dst if src is None else src, dst_ref=dst, send_sem=send.at[a, k], recv_sem=recv.at[a, k],
                device_id=to, device_id_type=MESH)

        mine = [pltpu.make_async_copy(ins[a], outs[a].at[4 * x + 2 * y + c], local.at[a]) for a in range(self.n)]
        first = []
        for a in range(self.n):
            first.append(copy(a, 0, me, sibling, src=ins[a]))
            first += [copy(a, 1 + j, me, (*chip, c), src=ins[a]) for j, chip in enumerate(chips)]
        return me, sibling, chips, c, copy, mine, first

    def start(self, ins, outs, sems):
        *_, mine, first = self._plan(ins, outs, sems)
        for cp in mine + first:
            cp.start()

    def finish(self, ins, outs, sems):
        me, sibling, chips, c, copy, mine, first = self._plan(ins, outs, sems)
        passed = []
        for a in range(self.n):
            for j, chip in enumerate(chips):
                copy(a, 1 + j, (*chip, c), me).wait_recv()
                fwd = copy(a, 4 + j, (*chip, c), sibling)
                fwd.start()
                passed.append(fwd)
        for a in range(self.n):
            copy(a, 0, sibling, me).wait_recv()
            for j, chip in enumerate(chips):
                copy(a, 4 + j, (*chip, 1 - c), me).wait_recv()
        for cp in first + passed:
            cp.wait_send()
        for cp in mine:
            cp.wait()


class ToSibling:
    def __init__(self, parts):
        self.arrs = list(parts)
        n = self.n = len(self.arrs)
        self.out_shape = [_sds((4,) + p.shape[1:], p.dtype) for p in self.arrs]
        self.scratch = [pltpu.SemaphoreType.DMA((n,)), pltpu.SemaphoreType.DMA((n,))]

    def start(self, ins, outs, sems):
        send, recv = sems
        x, y, c = lax.axis_index("x"), lax.axis_index("y"), lax.axis_index("c")
        for a in range(self.n):
            for q in range(4):
                pltpu.make_async_remote_copy(
                    src_ref=ins[a].at[2 * q + 1 - c], dst_ref=outs[a].at[q], send_sem=send.at[a], recv_sem=recv.at[a],
                    device_id=(x, y, 1 - c), device_id_type=MESH).start()

    def finish(self, ins, outs, sems):
        send, recv = sems
        x, y, c = lax.axis_index("x"), lax.axis_index("y"), lax.axis_index("c")
        for a in range(self.n):
            pltpu.make_async_remote_copy(
                src_ref=outs[a], dst_ref=outs[a], send_sem=send.at[a], recv_sem=recv.at[a],
                device_id=(x, y, 1 - c), device_id_type=MESH).wait()


class ToChips:
    def __init__(self, cps):
        self.arrs = list(cps)
        n = self.n = len(self.arrs)
        self.out_shape = [_sds((3,) + p.shape[1:], p.dtype) for p in self.arrs]
        self.scratch = [pltpu.SemaphoreType.DMA((n,)), pltpu.SemaphoreType.DMA((n,))]

    def start(self, ins, outs, sems):
        send, recv = sems
        x, y, c = lax.axis_index("x"), lax.axis_index("y"), lax.axis_index("c")
        for a in range(self.n):
            for r, (px, py) in enumerate([(1 - x, y), (x, 1 - y), (1 - x, 1 - y)]):
                pltpu.make_async_remote_copy(
                    src_ref=ins[a].at[2 * px + py], dst_ref=outs[a].at[r], send_sem=send.at[a], recv_sem=recv.at[a],
                    device_id=(px, py, c), device_id_type=MESH).start()

    def finish(self, ins, outs, sems):
        send, recv = sems
        x, y, c = lax.axis_index("x"), lax.axis_index("y"), lax.axis_index("c")
        for a in range(self.n):
            pltpu.make_async_remote_copy(
                src_ref=outs[a], dst_ref=outs[a], send_sem=send.at[a], recv_sem=recv.at[a],
                device_id=(x, y, c), device_id_type=MESH).wait()


def run_comm(comm, name):
    n = comm.n

    def body(*refs):
        ins, outs, sems = refs[:n], refs[n:2 * n], refs[2 * n:]
        comm.start(ins, outs, sems)
        comm.finish(ins, outs, sems)

    return pl.pallas_call(body, name=name, out_shape=comm.out_shape, in_specs=[ANY] * n, out_specs=[ANY] * n,
                          scratch_shapes=comm.scratch)(*comm.arrs)


def _pcall(body, args, *, name, grid, in_specs, out_specs, out_shape, scratch_shapes=(), comm=None):
    params = _params(len(grid))
    scratch_shapes = list(scratch_shapes)
    if comm is None:
        outs = pl.pallas_call(body, name=name, grid=grid, in_specs=in_specs, out_specs=out_specs, out_shape=out_shape,
                              scratch_shapes=scratch_shapes, compiler_params=params)(*args)
        return outs, None
    n_in, n_out, n_scr, nc = len(in_specs), len(out_specs), len(scratch_shapes), comm.n

    def hosted(*refs):
        ins, cins = refs[:n_in], refs[n_in:n_in + nc]
        o0 = n_in + nc
        outs, couts = refs[o0:o0 + n_out], refs[o0 + n_out:o0 + n_out + nc]
        s0 = o0 + n_out + nc
        scr, csems = refs[s0:s0 + n_scr], refs[s0 + n_scr:]
        first = pl.program_id(0) == 0
        last = pl.program_id(0) == grid[0] - 1
        for ax in range(1, len(grid)):
            first = jnp.logical_and(first, pl.program_id(ax) == 0)
            last = jnp.logical_and(last, pl.program_id(ax) == grid[ax] - 1)

        @pl.when(first)
        def _():
            comm.start(cins, couts, csems)
        body(*ins, *outs, *scr)

        @pl.when(last)
        def _():
            comm.finish(cins, couts, csems)

    res = pl.pallas_call(
        hosted, name=name, grid=grid, in_specs=list(in_specs) + [ANY] * nc, out_specs=list(out_specs) + [ANY] * nc,
        out_shape=list(out_shape) + comm.out_shape, scratch_shapes=scratch_shapes + comm.scratch,
        compiler_params=params)(*args, *comm.arrs)
    return res[:n_out], res[n_out:]


def add_bf16(a, b, name):
    rows, cols = a.shape
    tr = _row_tile(rows, 512)

    def body(a_ref, b_ref, o_ref):
        o_ref[...] = (a_ref[...].astype(F32) + b_ref[...].astype(F32)).astype(o_ref.dtype)

    spec = pl.BlockSpec((tr, cols), lambda i: (i, 0))
    return pl.pallas_call(body, name=name, grid=(rows // tr,), in_specs=[spec, spec], out_specs=spec,
                          out_shape=_sds(a.shape, BF), compiler_params=_params(1))(a, b)


def adamw(w, m, v, pieces, name):
    rows, cols = w.shape
    tr = _row_tile(rows, 256) if rows % 16 == 0 else rows
    np_ = len(pieces)
    c1 = 1.0 / (1.0 - ADAM_B1 ** ADAM_STEP)
    c2 = 1.0 / (1.0 - ADAM_B2 ** ADAM_STEP)

    def body(*refs):
        w_ref, m_ref, v_ref = refs[:3]
        p_refs = refs[3:3 + np_]
        g_ref, d_ref, nm_ref, nv_ref = refs[3 + np_:]
        g = p_refs[0][...].astype(F32)
        for pr in p_refs[1:]:
            g = g + pr[...].astype(F32)
        nm = ADAM_B1 * m_ref[...] + (1.0 - ADAM_B1) * g
        nv = ADAM_B2 * v_ref[...] + (1.0 - ADAM_B2) * (g * g)
        g_ref[...] = g
        nm_ref[...] = nm
        nv_ref[...] = nv
        d_ref[...] = -ADAM_LR * ((nm * c1) / (jnp.sqrt(nv * c2) + ADAM_EPS) + ADAM_WD * w_ref[...])

    spec = pl.BlockSpec((tr, cols), lambda i: (i, 0))
    p_specs = []
    for arr, k in pieces:
        if k is None:
            p_specs.append(spec)
        else:
            p_specs.append(pl.BlockSpec((None, tr, cols), functools.partial(lambda i, kk: (kk, i, 0), kk=k)))
    out = _sds(w.shape, F32)
    return pl.pallas_call(body, name=name, grid=(rows // tr,), in_specs=[spec] * 3 + p_specs, out_specs=[spec] * 4,
                          out_shape=[out] * 4, compiler_params=_params(1))(w, m, v, *[a for a, _ in pieces])


def adamw_layers(w, m, v, pieces, name):
    _, rows, cols = w.shape
    tr = _row_tile(rows, 256)
    nt = rows // tr
    counts = [len(pieces[0]), len(pieces[1])]
    c1 = 1.0 / (1.0 - ADAM_B1 ** ADAM_STEP)
    c2 = 1.0 / (1.0 - ADAM_B2 ** ADAM_STEP)

    def body(*refs):
        w_ref, m_ref, v_ref = refs[:3]
        p_refs = refs[3:3 + sum(counts)]
        g_ref, d_ref, nm_ref, nv_ref = refs[3 + sum(counts):]
        sums = []
        for group in (p_refs[:counts[0]], p_refs[counts[0]:]):
            s = group[0][...].astype(F32)
            for pr in group[1:]:
                s = s + pr[...].astype(F32)
            sums.append(s)
        g = jnp.where(pl.program_id(0) == 0, sums[0], sums[1])
        nm = ADAM_B1 * m_ref[...] + (1.0 - ADAM_B1) * g
        nv = ADAM_B2 * v_ref[...] + (1.0 - ADAM_B2) * (g * g)
        g_ref[...] = g
        nm_ref[...] = nm
        nv_ref[...] = nv
        d_ref[...] = -ADAM_LR * ((nm * c1) / (jnp.sqrt(nv * c2) + ADAM_EPS) + ADAM_WD * w_ref[...])

    def rows_of(layer):
        parked = nt - 1 if layer == 0 else 0
        return lambda l, i: jnp.where(l == layer, i, parked)

    spec = pl.BlockSpec((None, tr, cols), lambda l, i: (l, i, 0))
    p_specs, p_args = [], []
    for layer in (0, 1):
        row_of = rows_of(layer)
        for arr, k in pieces[layer]:
            p_args.append(arr)
            if k is None:
                p_specs.append(pl.BlockSpec((tr, cols), functools.partial(lambda l, i, f: (f(l, i), 0), f=row_of)))
            else:
                p_specs.append(pl.BlockSpec((None, tr, cols), functools.partial(lambda l, i, f, kk: (kk, f(l, i), 0), f=row_of, kk=k)))
    out = _sds(w.shape, F32)
    return pl.pallas_call(body, name=name, grid=(2, nt), in_specs=[spec] * 3 + p_specs, out_specs=[spec] * 4,
                          out_shape=[out] * 4, compiler_params=_params(2))(w, m, v, *p_args)


def sum_slabs(g, name):
    n, rows, cols = g.shape

    def body(g_ref, o_ref):
        s = g_ref[0]
        for k in range(1, n):
            s = s + g_ref[k]
        o_ref[...] = s

    return pl.pallas_call(body, name=name, out_shape=_sds((rows, cols), F32))(g)


def norm_proj(h, g, w, name, comm=None):
    s_len = h.shape[0]
    nb, _, tn = w.shape
    tm = min(TMP, s_len)
    nt = s_len // tm

    def body(h_ref, g_ref, w_ref, o_ref, hn_ref, hn_s):
        rows = pl.ds(pl.multiple_of(pl.program_id(1) * tm, tm), tm)

        @pl.when(pl.program_id(0) == 0)
        def _():
            x = h_ref[...]
            hn = (x * _rstd(x) * g_ref[...]).astype(BF)
            hn_s[rows, :] = hn
            hn_ref[...] = hn
        o_ref[...] = _dot(hn_s[rows, :], w_ref[...]).astype(BF)

    def first_pass_rows(j, i):
        return (jnp.where(j == 0, i, nt - 1), 0)

    return _pcall(
        body, (h, g, w), name=name, grid=(nb, nt),
        in_specs=[pl.BlockSpec((tm, D), first_pass_rows), pl.BlockSpec((1, D), lambda j, i: (0, 0)),
                  pl.BlockSpec((None, D, tn), lambda j, i: (j, 0, 0))],
        out_specs=[pl.BlockSpec((tm, tn), lambda j, i: (i, j)), pl.BlockSpec((tm, D), first_pass_rows)],
        out_shape=[_sds((s_len, nb * tn), BF), _sds((s_len, D), BF)],
        scratch_shapes=[pltpu.VMEM((s_len, D), BF)], comm=comm)


def _sgu_mix(ws_ref, vln_s, mix_s, bs_ref, tm, transposed):
    mask = _sgu_mask(transposed)
    for hd in range(HEADS):
        wm = jnp.where(mask, ws_ref[hd], 0.0).astype(BF)
        cs = slice(hd * BLK, (hd + 1) * BLK)
        for n in range(tm // BLK):
            rs = slice(n * BLK, (n + 1) * BLK)
            r = _dot(wm, vln_s[rs, cs])
            mix_s[rs, cs] = r if bs_ref is None else r + bs_ref[:, cs]


def sgu_fwd(proj, ws, bsfull, gv, bv, wo, name):
    s_len = proj.shape[0]
    tm = min(TM, s_len)

    def body(zu_ref, zv_ref, ws_ref, bs_ref, gv_ref, bv_ref, wo_ref, sgu_ref, br_ref, vln_s, mix_s):
        u, _ = _gelu(zu_ref[...].astype(F32))
        v, _ = _gelu(zv_ref[...].astype(F32))
        vhat, _ = _ln_fwd(v)
        vln_s[...] = (vhat * gv_ref[...] + bv_ref[...]).astype(BF)
        _sgu_mix(ws_ref, vln_s, mix_s, bs_ref, tm, False)
        sgu = (u * mix_s[...]).astype(BF)
        sgu_ref[...] = sgu
        br_ref[...] = _dot(sgu, wo_ref[...].reshape(D, D)).astype(BF)

    return pl.pallas_call(
        body, name=name, grid=(s_len // tm,),
        in_specs=[_cols(tm, D, 0), _cols(tm, D, 1), _whole((HEADS, BLK, BLK)), _whole((BLK, D)), _whole((1, D)), _whole((1, D)),
                  _rowsharded(BLK)],
        out_specs=[_cols(tm, D, 0)] * 2, out_shape=[_sds((s_len, D), BF)] * 2,
        scratch_shapes=[pltpu.VMEM((tm, D), BF), pltpu.VMEM((tm, D), F32)], compiler_params=_params(1),
    )(proj, proj, ws, bsfull, gv, bv, wo)


def _causal_conv(ext_s, out_s, wdw_ref, bias_ref, tm):
    def chunk(ci, carry):
        r0 = pl.multiple_of((ci // (D // LC)) * RC, RC)
        l0 = pl.multiple_of((ci % (D // LC)) * LC, LC)
        win = ext_s[pl.ds(r0, RC + HALO), pl.ds(l0, LC)]
        acc = jnp.broadcast_to(bias_ref[:, pl.ds(l0, LC)], (RC, LC))
        for r in range(8):
            wr = win if r == 0 else pltpu.roll(win, r, 0)
            for m in range(4):
                d = 8 * m + r
                if d < CONV_W:
                    k = CONV_W - 1 - d
                    acc = acc + wdw_ref[k:k + 1, pl.ds(l0, LC)] * wr[HALO - 8 * m:HALO - 8 * m + RC]
        out_s[pl.ds(r0, RC), pl.ds(l0, LC)] = acc
        return carry
    lax.fori_loop(0, (tm // RC) * (D // LC), chunk, 0)


def _glu_ext(a_ref, g_ref, ah_ref, gh_ref, ext_s, first):
    hh = ah_ref[...].astype(F32) * _sig(gh_ref[...].astype(F32))
    ext_s[0:HALO, :] = jnp.where(first, 0.0, hh)
    ext_s[HALO:, :] = a_ref[...].astype(F32) * _sig(g_ref[...].astype(F32))


def conv_fwd(proj, wdw, bdw, gln, bln, wo, name, comm=None):
    s_len = proj.shape[0]
    tm = min(TM, s_len)

    def body(a_ref, g_ref, ah_ref, gh_ref, wdw_ref, bdw_ref, gln_ref, bln_ref, wo_ref, cv_ref, cb_ref, br_ref, ext_s, conv_s):
        _glu_ext(a_ref, g_ref, ah_ref, gh_ref, ext_s, pl.program_id(0) == 0)
        _causal_conv(ext_s, conv_s, wdw_ref, bdw_ref, tm)
        cv = conv_s[...].astype(BF)
        cv_ref[...] = cv
        chat, _ = _ln_fwd(cv.astype(F32))
        yl = chat * gln_ref[...] + bln_ref[...]
        cb = (yl * _sig(yl)).astype(BF)
        cb_ref[...] = cb
        br_ref[...] = _dot(cb, wo_ref[...].reshape(D, D)).astype(BF)

    return _pcall(
        body, (proj, proj, proj, proj, wdw, bdw, gln, bln, wo), name=name, grid=(s_len // tm,),
        in_specs=[_cols(tm, D, 2), _cols(tm, D, 3), _prev_halo(tm, D, 2), _prev_halo(tm, D, 3), _whole((HALO, D)),
                  _whole((1, D)), _whole((1, D)), _whole((1, D)), _rowsharded(BLK)],
        out_specs=[_cols(tm, D, 0)] * 3, out_shape=[_sds((s_len, D), BF)] * 3,
        scratch_shapes=[pltpu.VMEM((tm + HALO, D), F32), pltpu.VMEM((tm, D), F32)], comm=comm)


def pool_fwd(proj, wpool, spool, wo, name):
    s_len = proj.shape[0]
    tm = min(TM, s_len)

    def body(z_ref, zh_ref, wp_ref, sp_ref, wo_ref, pooled_ref, pm_ref, br_ref, ext_s, mr_s):
        i = pl.program_id(0)
        ext_s[0:HALO, :] = jnp.where(i == 0, 0.0, zh_ref[...].astype(F32))
        ext_s[HALO:, :] = z_ref[...].astype(F32)
        for gi, w in enumerate(POOL_WINDOWS):
            cs = slice(gi * POOL_GD, (gi + 1) * POOL_GD)
            e = ext_s[:, cs]
            s = e
            sh = 1
            while sh < w:
                s = s + pltpu.roll(s, sh, 0)
                sh *= 2
            pooled = (s[HALO:] * _inv_count(i, tm, w) - e[HALO:]).astype(BF)
            pooled_ref[:, cs] = pooled
            mr_s[:, cs] = _dot(pooled, wp_ref[gi])
        pm = (mr_s[...] * sp_ref[...]).astype(BF)
        pm_ref[...] = pm
        br_ref[...] = _dot(pm, wo_ref[...].reshape(D, D)).astype(BF)

    return pl.pallas_call(
        body, name=name, grid=(s_len // tm,),
        in_specs=[_cols(tm, D, 4), _prev_halo(tm, D, 4), _whole((4, POOL_GD, POOL_GD)), _whole((1, D)), _rowsharded(BLK)],
        out_specs=[_cols(tm, D, 0)] * 3, out_shape=[_sds((s_len, D), BF)] * 3,
        scratch_shapes=[pltpu.VMEM((tm + HALO, D), F32), pltpu.VMEM((tm, D), F32)], compiler_params=_params(1),
    )(proj, proj, wpool, spool, wo)


def merge_out(proj, bra, brb, brc, h, wout, gpost, name, comm=None):
    s_len = h.shape[0]
    tm = min(TM, s_len)

    def body(z0, z1, z2, a_ref, b_ref, c_ref, h_ref, wo_ref, g_ref, mg_ref, mo_ref, h1_ref):
        merged = (_sig(z0[...].astype(F32)) * a_ref[...].astype(F32) + _sig(z1[...].astype(F32)) * b_ref[...].astype(F32)
                  + _sig(z2[...].astype(F32)) * c_ref[...].astype(F32)).astype(BF)
        mg_ref[...] = merged
        mo = _dot(merged, wo_ref[...].reshape(D, D))
        mo_ref[...] = mo.astype(BF)
        h1_ref[...] = h_ref[...] + mo * _rstd(mo) * g_ref[...]

    row = _cols(tm, D, 0)
    return _pcall(
        body, (proj, proj, proj, bra, brb, brc, h, wout, gpost), name=name, grid=(s_len // tm,),
        in_specs=[_cols(tm, D, 5), _cols(tm, D, 6), _cols(tm, D, 7), row, row, row, row, _rowsharded(BLK), _whole((1, D))],
        out_specs=[row] * 3, out_shape=[_sds((s_len, D), BF), _sds((s_len, D), BF), _sds((s_len, D), F32)], comm=comm)


def _p_spec(tm, layer):
    return pl.BlockSpec((None, None, tm, PLE), lambda i: (layer, 0, i, 0))


def ffn_out(ff, h1, p, wfo, gpost, wpg, wple, layer, name, comm=None):
    s_len = h1.shape[0]
    tm = min(TMB, s_len)

    def body(fg_ref, fu_ref, h1_ref, p_ref, wfo_ref, g_ref, wpg_ref, wple_ref, act_ref, f_ref, h2_ref, pg_ref, h3_ref):
        gt = fg_ref[...].astype(F32)
        act = (gt * _sig(gt) * fu_ref[...].astype(F32)).astype(BF)
        act_ref[...] = act
        f = _dot(act, wfo_ref[...].reshape(D_FF, D))
        f_ref[...] = f.astype(BF)
        h2 = h1_ref[...] + f * _rstd(f) * g_ref[...]
        h2_ref[...] = h2
        pg = _dot(h2.astype(BF), wpg_ref[...].reshape(D, D)).astype(BF)
        pg_ref[...] = pg
        pe = _dot(p_ref[...].astype(BF), wple_ref[...])
        h3_ref[...] = h2 + _sig(pg.astype(F32)) * pe

    row = _cols(tm, D, 0)
    return _pcall(
        body, (ff, ff, h1, p, wfo, gpost, wpg, wple), name=name, grid=(s_len // tm,),
        in_specs=[_cols(tm, D_FF, 0), _cols(tm, D_FF, 1), row, _p_spec(tm, layer), _rowsharded(D_FF // N_DEV),
                  _whole((1, D)), _rowsharded(BLK), _whole((PLE, D))],
        out_specs=[_cols(tm, D_FF, 0), row, row, row, row],
        out_shape=[_sds((s_len, D_FF), BF), _sds((s_len, D), BF), _sds((s_len, D), F32), _sds((s_len, D), BF), _sds((s_len, D), F32)],
        comm=comm)


def ple_ffn_bwd(dh3, pg, p, f, ff, wple, wpg, wfo, gpost, layer, name, target=None, comm=None):
    s_len = dh3.shape[0]
    tm = min(TMB, s_len)
    nt = s_len // tm
    with_loss = target is not None

    def body(*refs):
        if with_loss:
            t_ref, refs, loss_ref, loss_acc = refs[0], refs[1:-2], refs[-2], refs[-1]
        (dh3_ref, pg_ref, p_ref, f_ref, fg_ref, fu_ref, wple_ref, wpg_ref, wfo_ref, g_ref,
         dh2_ref, dpe_ref, dpg_ref, df_ref, dff_ref, dg_ref) = refs
        i = pl.program_id(0)
        dh3v = dh3_ref[...]
        if with_loss:
            err = dh3v - t_ref[...]
            dh3v = err * (1.0 / D)
            _accum(loss_acc, i == 0, _colsum(err * err))

            @pl.when(i == nt - 1)
            def _():
                loss_ref[...] = jnp.broadcast_to(jnp.sum(loss_acc[...], axis=1, keepdims=True) * (0.5 / D), (1, LC))
        s = _sig(pg_ref[...].astype(F32))
        pe = _dot(p_ref[...].astype(BF), wple_ref[...])
        dpe_ref[...] = (dh3v * s).astype(BF)
        dpg = (dh3v * pe * s * (1.0 - s)).astype(BF)
        dpg_ref[...] = dpg
        dh2 = dh3v + _dot_nt(dpg, wpg_ref[...].reshape(D, D))
        dh2_ref[...] = dh2
        fv = f_ref[...].astype(F32)
        r = _rstd(fv)
        _accum(dg_ref, i == 0, _colsum(dh2 * fv * r))
        df = _rms_bwd(fv, dh2 * g_ref[...], r).astype(BF)
        df_ref[...] = df
        dact = _dot_nt(df, wfo_ref[...].reshape(D_FF, D))
        gt = fg_ref[...].astype(F32)
        sg = _sig(gt)
        up = fu_ref[...].astype(F32)
        dff_ref[:, 0:D_FF] = (dact * up * sg * (1.0 + gt * (1.0 - sg))).astype(BF)
        dff_ref[:, D_FF:2 * D_FF] = (dact * gt * sg).astype(BF)

    row = _cols(tm, D, 0)
    args = (dh3, pg, p, f, ff, ff, wple, wpg, wfo, gpost)
    in_specs = [row, row, _p_spec(tm, layer), row, _cols(tm, D_FF, 0), _cols(tm, D_FF, 1), _whole((PLE, D)),
                _rowsharded(BLK), _rowsharded(D_FF // N_DEV), _whole((1, D))]
    out_specs = [row, row, row, row, _cols(tm, 2 * D_FF, 0), _whole((1, D))]
    out_shape = [_sds((s_len, D), F32), _sds((s_len, D), BF), _sds((s_len, D), BF), _sds((s_len, D), BF),
                 _sds((s_len, 2 * D_FF), BF), _sds((1, D), F32)]
    scratch = []
    if with_loss:
        args, in_specs = (target,) + args, [row] + in_specs
        out_specs, out_shape = out_specs + [_whole((1, LC))], out_shape + [_sds((1, LC), F32)]
        scratch = [pltpu.VMEM((1, D), F32)]
    return _pcall(body, args, name=name, grid=(nt,), in_specs=in_specs, out_specs=out_specs, out_shape=out_shape,
                  scratch_shapes=scratch, comm=comm)


def ffn_in_bwd(dff, w3, h1, dh2, gpre, name):
    s_len = h1.shape[0]
    tm = min(TMB, s_len)
    nb, _, tn = w3.shape

    def body(dff_ref, w_ref, h1_ref, dh2_ref, g_ref, dh1_ref, dg_ref):
        dhn = _dot_nt(dff_ref[:, 0:tn], w_ref[0])
        for j in range(1, nb):
            dhn = dhn + _dot_nt(dff_ref[:, j * tn:(j + 1) * tn], w_ref[j])
        x = h1_ref[...]
        r = _rstd(x)
        _accum(dg_ref, pl.program_id(0) == 0, _colsum(dhn * x * r))
        dh1_ref[...] = dh2_ref[...] + _rms_bwd(x, dhn * g_ref[...], r)

    row = _cols(tm, D, 0)
    return pl.pallas_call(
        body, name=name, grid=(s_len // tm,),
        in_specs=[_cols(tm, nb * tn, 0), pl.BlockSpec((nb, D, tn), lambda i: (0, 0, 0), pipeline_mode=pl.Buffered(1)), row, row,
                  _whole((1, D))],
        out_specs=[row, _whole((1, D))],
        out_shape=[_sds((s_len, D), F32), _sds((1, D), F32)], compiler_params=_params(1))(dff, w3, h1, dh2, gpre)


def mix_post_bwd(dh1, mo, gpost, wout, proj, bra, brb, brc, win, name, comm=None):
    s_len = dh1.shape[0]
    tm = min(TMB, s_len)

    def body(dh1_ref, mo_ref, g_ref, wo_ref, z0, z1, z2, a_ref, b_ref, c_ref, w5, w6, w7,
             dmo_ref, da_ref, db_ref, dc_ref, dz_ref, dhn_ref, dg_ref):
        i = pl.program_id(0)
        dh1v = dh1_ref[...]
        mo_v = mo_ref[...].astype(F32)
        r = _rstd(mo_v)
        _accum(dg_ref, i == 0, _colsum(dh1v * mo_v * r))
        dmo = _rms_bwd(mo_v, dh1v * g_ref[...], r).astype(BF)
        dmo_ref[...] = dmo
        dmerged = _dot_nt(dmo, wo_ref[...].reshape(D, D))
        dhn = jnp.zeros((tm, D), F32)
        for k, (z, br, dbr, w) in enumerate(((z0, a_ref, da_ref, w5), (z1, b_ref, db_ref, w6), (z2, c_ref, dc_ref, w7))):
            s = _sig(z[...].astype(F32))
            dbr[...] = (dmerged * s).astype(BF)
            dz = (dmerged * br[...].astype(F32) * s * (1.0 - s)).astype(BF)
            dz_ref[:, k * D:(k + 1) * D] = dz
            dhn = dhn + _dot_nt(dz, w[...])
        dhn_ref[...] = dhn

    row = _cols(tm, D, 0)
    return _pcall(
        body, (dh1, mo, gpost, wout, proj, proj, proj, bra, brb, brc, win, win, win), name=name, grid=(s_len // tm,),
        in_specs=[row, row, _whole((1, D)), _rowsharded(BLK), _cols(tm, D, 5), _cols(tm, D, 6), _cols(tm, D, 7), row, row, row,
                  _win_block(5), _win_block(6), _win_block(7)],
        out_specs=[row, row, row, row, _cols(tm, 3 * D, 0), row, _whole((1, D))],
        out_shape=[_sds((s_len, D), BF)] * 4 + [_sds((s_len, 3 * D), BF), _sds((s_len, D), F32), _sds((1, D), F32)], comm=comm)


def sgu_bwd(dbr, wo, proj, ws, wst, bsfull, gv, bv, win, dhn_in, name, comm=None):
    s_len = dbr.shape[0]
    tm = min(TMB, s_len)
    nt = s_len // tm

    def body(dbr_ref, wo_ref, zu_ref, zv_ref, ws_ref, wst_ref, bs_ref, gv_ref, bv_ref, w0, w1, dhn_in_ref,
             dz_ref, dhn_ref, dws_ref, dbs_ref, dgv_ref, dbv_ref, vln_s, mix_s, dmix_s, dvln_s, bs_acc):
        i = pl.program_id(0)
        first = i == 0
        dsgu = _dot_nt(dbr_ref[...], wo_ref[...].reshape(D, D))
        zu = zu_ref[...].astype(F32)
        zv = zv_ref[...].astype(F32)
        u, tu = _gelu(zu)
        v, tv = _gelu(zv)
        vhat, rs = _ln_fwd(v)
        vln_s[...] = (vhat * gv_ref[...] + bv_ref[...]).astype(BF)
        _sgu_mix(ws_ref, vln_s, mix_s, bs_ref, tm, False)
        du = dsgu * mix_s[...]
        dmix = dsgu * u
        dmix_s[...] = dmix.astype(BF)
        blocks = dmix[0:BLK]
        for n in range(1, tm // BLK):
            blocks = blocks + dmix[n * BLK:(n + 1) * BLK]
        _accum(bs_acc, first, blocks)
        for hd in range(HEADS):
            cs = slice(hd * BLK, (hd + 1) * BLK)
            g = _dot_nt(dmix_s[0:BLK, cs], vln_s[0:BLK, cs])
            for n in range(1, tm // BLK):
                g = g + _dot_nt(dmix_s[n * BLK:(n + 1) * BLK, cs], vln_s[n * BLK:(n + 1) * BLK, cs])

            @pl.when(first)
            def _():
                dws_ref[hd] = g

            @pl.when(jnp.logical_not(first))
            def _():
                dws_ref[hd] += g
        _sgu_mix(wst_ref, dmix_s, dvln_s, None, tm, True)
        dvln = dvln_s[...]
        _accum(dgv_ref, first, _colsum(dvln * vhat))
        _accum(dbv_ref, first, _colsum(dvln))
        dv = _ln_bwd(dvln * gv_ref[...], vhat, rs)
        dzu = (du * _gelu_grad(zu, tu)).astype(BF)
        dzv = (dv * _gelu_grad(zv, tv)).astype(BF)
        dz_ref[:, 0:D] = dzu
        dz_ref[:, D:2 * D] = dzv
        dhn_ref[...] = dhn_in_ref[...] + _dot_nt(dzu, w0[...]) + _dot_nt(dzv, w1[...])

        @pl.when(i == nt - 1)
        def _():
            mask = _sgu_mask(False)
            for hd in range(HEADS):
                dws_ref[hd] = jnp.where(mask, dws_ref[hd], 0.0)
                dbs_ref[:, hd:hd + 1] = jnp.sum(bs_acc[:, hd * BLK:(hd + 1) * BLK], axis=1, keepdims=True)

    row = _cols(tm, D, 0)
    vec = _whole((1, D))
    return _pcall(
        body, (dbr, wo, proj, proj, ws, wst, bsfull, gv, bv, win, win, dhn_in), name=name, grid=(nt,),
        in_specs=[row, _rowsharded(BLK), _cols(tm, D, 0), _cols(tm, D, 1), _whole((HEADS, BLK, BLK)), _whole((HEADS, BLK, BLK)),
                  _whole((BLK, D)), vec, vec, _win_block(0), _win_block(1), row],
        out_specs=[_cols(tm, 2 * D, 0), row, _whole((HEADS, BLK, BLK)), _whole((BLK, HEADS)), vec, vec],
        out_shape=[_sds((s_len, 2 * D), BF), _sds((s_len, D), F32), _sds((HEADS, BLK, BLK), F32), _sds((BLK, HEADS), F32),
                   _sds((1, D), F32), _sds((1, D), F32)],
        scratch_shapes=[pltpu.VMEM((tm, D), BF), pltpu.VMEM((tm, D), F32), pltpu.VMEM((tm, D), BF), pltpu.VMEM((tm, D), F32),
                        pltpu.VMEM((BLK, D), F32)], comm=comm)


def conv_bwd(dbr, wo, proj, conv, gln, bln, name, comm=None):
    s_len = dbr.shape[0]
    tm = min(TMB, s_len)
    nt = s_len // tm

    def body(dbr_ref, wo_ref, a_ref, g_ref, ah_ref, gh_ref, cv_ref, gln_ref, bln_ref,
             dc_ref, dw_ref, dbdw_ref, dgln_ref, dbln_ref, ext_s, dc_s, dw_acc):
        i = pl.program_id(0)
        first = i == 0
        dcb = _dot_nt(dbr_ref[...], wo_ref[...].reshape(D, D))
        _glu_ext(a_ref, g_ref, ah_ref, gh_ref, ext_s, first)
        chat, rs = _ln_fwd(cv_ref[...].astype(F32))
        yl = chat * gln_ref[...] + bln_ref[...]
        sy = _sig(yl)
        dyl = dcb * sy * (1.0 + yl * (1.0 - sy))
        _accum(dgln_ref, first, _colsum(dyl * chat))
        _accum(dbln_ref, first, _colsum(dyl))
        dc = _ln_bwd(dyl * gln_ref[...], chat, rs)
        _accum(dbdw_ref, first, _colsum(dc))
        dc_ref[...] = dc.astype(BF)
        dc_s[...] = dc

        @pl.when(first)
        def _():
            dw_acc[...] = jnp.zeros_like(dw_acc)

        def chunk(ci, carry):
            r0 = pl.multiple_of((ci // (D // LC)) * RC, RC)
            l0 = pl.multiple_of((ci % (D // LC)) * LC, LC)
            win = ext_s[pl.ds(r0, RC + HALO), pl.ds(l0, LC)]
            dcw = dc_s[pl.ds(r0, RC), pl.ds(l0, LC)]
            for r in range(8):
                wr = win if r == 0 else pltpu.roll(win, r, 0)
                for m in range(4):
                    d = 8 * m + r
                    if d < CONV_W:
                        k = CONV_W - 1 - d
                        prod = dcw * wr[HALO - 8 * m:HALO - 8 * m + RC]
                        dw_acc[k * 8:(k + 1) * 8, pl.ds(l0, LC)] += prod.reshape(RC // 8, 8, LC).sum(axis=0)
            return carry
        lax.fori_loop(0, (tm // RC) * (D // LC), chunk, 0)

        @pl.when(i == nt - 1)
        def _():
            dw_ref[...] = dw_acc[...].reshape(HALO, 8, D).sum(axis=1)

    row = _cols(tm, D, 0)
    vec = _whole((1, D))
    return _pcall(
        body, (dbr, wo, proj, proj, proj, proj, conv, gln, bln), name=name, grid=(nt,),
        in_specs=[row, _rowsharded(BLK), _cols(tm, D, 2), _cols(tm, D, 3), _prev_halo(tm, D, 2), _prev_halo(tm, D, 3),
                  row, vec, vec],
        out_specs=[row, _whole((HALO, D)), vec, vec, vec],
        out_shape=[_sds((s_len, D), BF), _sds((HALO, D), F32), _sds((1, D), F32), _sds((1, D), F32), _sds((1, D), F32)],
        scratch_shapes=[pltpu.VMEM((tm + HALO, D), F32), pltpu.VMEM((tm, D), F32), pltpu.VMEM((HALO * 8, D), F32)], comm=comm)


def pool_bwd(dbr, wo, pooled, wpool, spool, name):
    s_len = dbr.shape[0]
    tm = min(TMB, s_len)

    def body(dbr_ref, wo_ref, pl_ref, wp_ref, sp_ref, dmr_ref, q_ref, dsp_ref, mr_s):
        i = pl.program_id(0)
        dpm = _dot_nt(dbr_ref[...], wo_ref[...].reshape(D, D))
        for gi in range(4):
            cs = slice(gi * POOL_GD, (gi + 1) * POOL_GD)
            mr_s[:, cs] = _dot(pl_ref[:, cs], wp_ref[gi])
        _accum(dsp_ref, i == 0, _colsum(dpm * mr_s[...]))
        dmr = (dpm * sp_ref[...]).astype(BF)
        dmr_ref[...] = dmr
        for gi, w in enumerate(POOL_WINDOWS):
            cs = slice(gi * POOL_GD, (gi + 1) * POOL_GD)
            q_ref[:, cs] = (_dot_nt(dmr[:, cs], wp_ref[gi]) * _inv_count(i, tm, w)).astype(BF)

    row = _cols(tm, D, 0)
    return pl.pallas_call(
        body, name=name, grid=(s_len // tm,),
        in_specs=[row, _rowsharded(BLK), row, _whole((4, POOL_GD, POOL_GD)), _whole((1, D))],
        out_specs=[row, row, _whole((1, D))],
        out_shape=[_sds((s_len, D), BF), _sds((s_len, D), BF), _sds((1, D), F32)],
        scratch_shapes=[pltpu.VMEM((tm, D), F32)], compiler_params=_params(1))(dbr, wo, pooled, wpool, spool)


def seq_bwd(dc, q, proj, wdw, win, dhn_in, h, dh1, gpre, name, comm=None):
    s_len = dc.shape[0]
    tm = min(TMB, s_len)
    nt = s_len // tm

    def body(dc_ref, dch_ref, q_ref, qh_ref, a_ref, g_ref, wdw_ref, w2, w3, w4, dhn_in_ref, h_ref, dh1_ref, gpre_ref,
             dz_ref, dh_ref, dgpre_ref, ext_s, dhc_s, qext_s):
        i = pl.program_id(0)
        last = i == nt - 1
        ext_s[0:tm, :] = dc_ref[...].astype(F32)
        ext_s[tm:, :] = jnp.where(last, 0.0, dch_ref[...].astype(F32))

        def chunk(ci, carry):
            r0 = pl.multiple_of((ci // (D // LC)) * RC, RC)
            l0 = pl.multiple_of((ci % (D // LC)) * LC, LC)
            win_ = ext_s[pl.ds(r0, RC + HALO), pl.ds(l0, LC)]
            acc = jnp.zeros((RC, LC), F32)
            for r in range(8):
                wr = win_ if r == 0 else pltpu.roll(win_, RC + HALO - r, 0)
                for m in range(4):
                    d = 8 * m + r
                    if d < CONV_W:
                        k = CONV_W - 1 - d
                        acc = acc + wdw_ref[k:k + 1, pl.ds(l0, LC)] * wr[8 * m:8 * m + RC]
            dhc_s[pl.ds(r0, RC), pl.ds(l0, LC)] = acc
            return carry
        lax.fori_loop(0, (tm // RC) * (D // LC), chunk, 0)

        dhc = dhc_s[...]
        av = a_ref[...].astype(F32)
        sg = _sig(g_ref[...].astype(F32))
        da = (dhc * sg).astype(BF)
        dg = (dhc * av * sg * (1.0 - sg)).astype(BF)
        dz_ref[:, 0:D] = da
        dz_ref[:, D:2 * D] = dg

        qext_s[0:tm, :] = q_ref[...].astype(F32)
        qext_s[tm:, :] = jnp.where(last, 0.0, qh_ref[...].astype(F32))
        for gi, w in enumerate(POOL_WINDOWS):
            cs = slice(gi * POOL_GD, (gi + 1) * POOL_GD)
            e = qext_s[:, cs]
            s = e
            sh = 1
            while sh < w:
                s = s + pltpu.roll(s, tm + HALO - sh, 0)
                sh *= 2
            dz_ref[:, 2 * D + gi * POOL_GD:2 * D + (gi + 1) * POOL_GD] = (s[0:tm] - e[0:tm] * _count(i, tm, w)).astype(BF)
        dhn = dhn_in_ref[...] + _dot_nt(da, w2[...]) + _dot_nt(dg, w3[...]) + _dot_nt(dz_ref[:, 2 * D:3 * D], w4[...])
        x = h_ref[...]
        r = _rstd(x)
        _accum(dgpre_ref, i == 0, _colsum(dhn * x * r))
        dh_ref[...] = dh1_ref[...] + _rms_bwd(x, dhn * gpre_ref[...], r)

    row = _cols(tm, D, 0)
    return _pcall(
        body, (dc, dc, q, q, proj, proj, wdw, win, win, win, dhn_in, h, dh1, gpre), name=name, grid=(nt,),
        in_specs=[row, _next_halo(tm, D, 0, s_len), row, _next_halo(tm, D, 0, s_len), _cols(tm, D, 2), _cols(tm, D, 3),
                  _whole((HALO, D)), _win_block(2), _win_block(3), _win_block(4), row, row, row, _whole((1, D))],
        out_specs=[_cols(tm, 3 * D, 0), row, _whole((1, D))],
        out_shape=[_sds((s_len, 3 * D), BF), _sds((s_len, D), F32), _sds((1, D), F32)],
        scratch_shapes=[pltpu.VMEM((tm + HALO, D), F32), pltpu.VMEM((tm, D), F32), pltpu.VMEM((tm + HALO, D), F32)], comm=comm)


def wgrad(a, b, tk, tn, name, stacked=False, diag=False, a_spec=None, comm=None):
    s_len = b.shape[0]
    k_dim = a.shape[-1]
    n_dim = b.shape[1]
    ts = min(TS, s_len)
    nk = 1 if diag else k_dim // tk
    nn, ns = n_dim // tn, s_len // ts

    def body(a_ref, b_ref, o_ref, acc):
        s = pl.program_id(2)
        _accum(acc, s == 0, _dot_tn(a_ref[...].astype(BF), b_ref[...].astype(BF)))

        @pl.when(s == ns - 1)
        def _():
            o_ref[...] = acc[...].astype(BF).reshape(o_ref.shape)

    if a_spec is None:
        a_spec = pl.BlockSpec((ts, tk), (lambda k, n, s: (s, n)) if diag else (lambda k, n, s: (s, k)))
    if stacked or diag:
        out_shape = _sds((nn, tk if diag else k_dim, tn), BF)
        o_spec = pl.BlockSpec((1, tk, tn), lambda k, n, s: (n, k, 0))
    else:
        out_shape = _sds((k_dim, n_dim), BF)
        o_spec = pl.BlockSpec((tk, tn), lambda k, n, s: (k, n))
    (out,), got = _pcall(
        body, (a, b), name=name, grid=(nk, nn, ns),
        in_specs=[a_spec, pl.BlockSpec((ts, tn), lambda k, n, s: (s, n))], out_specs=[o_spec], out_shape=[out_shape],
        scratch_shapes=[pltpu.VMEM((tk, tn), F32)], comm=comm)
    return out if comm is None else (out, got)


_WEIGHTS = ['g_mix_pre', 'w_in', 'w_sgu_s', 'b_sgu_s', 'g_sgu_v', 'b_sgu_v', 'w_sgu_out', 'w_dw', 'b_dw', 'g_conv_ln', 'b_conv_ln',
            'w_conv_out', 'w_pool', 's_pool', 'w_pool_out', 'w_out', 'g_mix_post', 'g_ffn_pre', 'w_ffn_in', 'w_ffn_out', 'g_ffn_post',
            'w_ple', 'w_ple_gate']
_SHARDED = ['w_in', 'w_sgu_out', 'w_conv_out', 'w_pool', 'w_pool_out', 'w_out', 'w_ffn_in', 'w_ffn_out', 'w_ple', 'w_ple_gate']
_VECTORS = ['g_mix_pre', 'g_sgu_v', 'b_sgu_v', 'b_dw', 'g_conv_ln', 'b_conv_ln', 's_pool', 'g_mix_post', 'g_ffn_pre', 'g_ffn_post']
_SUBLANES = 8
_SGU_ROWS = HEADS * BLK * BLK // D
_REP_ROWS = _SUBLANES * (len(_VECTORS) + 2) + _SGU_ROWS


def _pack_replicated(t, layer):
    rows = [jnp.pad(t[k][layer].reshape(1, D), ((0, _SUBLANES - 1), (0, 0))) for k in _VECTORS + ['b_sgu_s']]
    return jnp.concatenate(rows + [t['w_sgu_s'][layer].reshape(_SGU_ROWS, D), jnp.zeros((_SUBLANES, D), F32)], axis=0)


def _unpack_replicated(packed):
    out = {}
    for i, k in enumerate(_VECTORS):
        out[k] = packed[:, _SUBLANES * i, :]
    o = _SUBLANES * len(_VECTORS)
    out['b_sgu_s'] = packed[:, o, :].reshape(2, HEADS, BLK)
    out['w_sgu_s'] = packed[:, o + _SUBLANES:o + _SUBLANES + _SGU_ROWS, :].reshape(2, HEADS, BLK, BLK)
    return out


def _pad_taps(w):
    return jnp.pad(w, ((0, HALO - CONV_W), (0, 0)))


def kernel(x, p, g_mix_pre, w_in, w_sgu_s, b_sgu_s, g_sgu_v, b_sgu_v, w_sgu_out, w_dw, b_dw, g_conv_ln, b_conv_ln, w_conv_out, w_pool, s_pool, w_pool_out, w_out, g_mix_post, g_ffn_pre, w_ffn_in, w_ffn_out, g_ffn_post, w_ple, w_ple_gate, loss_target, m_g_mix_pre, m_w_in, m_w_sgu_s, m_b_sgu_s, m_g_sgu_v, m_b_sgu_v, m_w_sgu_out, m_w_dw, m_b_dw, m_g_conv_ln, m_b_conv_ln, m_w_conv_out, m_w_pool, m_s_pool, m_w_pool_out, m_w_out, m_g_mix_post, m_g_ffn_pre, m_w_ffn_in, m_w_ffn_out, m_g_ffn_post, m_w_ple, m_w_ple_gate, v_g_mix_pre, v_w_in, v_w_sgu_s, v_b_sgu_s, v_g_sgu_v, v_b_sgu_v, v_w_sgu_out, v_w_dw, v_b_dw, v_g_conv_ln, v_b_conv_ln, v_w_conv_out, v_w_pool, v_s_pool, v_w_pool_out, v_w_out, v_g_mix_post, v_g_ffn_pre, v_w_ffn_in, v_w_ffn_out, v_g_ffn_post, v_w_ple, v_w_ple_gate):
    W = dict(g_mix_pre=g_mix_pre, w_in=w_in, w_sgu_s=w_sgu_s, b_sgu_s=b_sgu_s, g_sgu_v=g_sgu_v, b_sgu_v=b_sgu_v, w_sgu_out=w_sgu_out,
             w_dw=w_dw, b_dw=b_dw, g_conv_ln=g_conv_ln, b_conv_ln=b_conv_ln, w_conv_out=w_conv_out, w_pool=w_pool, s_pool=s_pool,
             w_pool_out=w_pool_out, w_out=w_out, g_mix_post=g_mix_post, g_ffn_pre=g_ffn_pre, w_ffn_in=w_ffn_in, w_ffn_out=w_ffn_out,
             g_ffn_post=g_ffn_post, w_ple=w_ple, w_ple_gate=w_ple_gate)
    M = dict(g_mix_pre=m_g_mix_pre, w_in=m_w_in, w_sgu_s=m_w_sgu_s, b_sgu_s=m_b_sgu_s, g_sgu_v=m_g_sgu_v, b_sgu_v=m_b_sgu_v,
             w_sgu_out=m_w_sgu_out, w_dw=m_w_dw, b_dw=m_b_dw, g_conv_ln=m_g_conv_ln, b_conv_ln=m_b_conv_ln, w_conv_out=m_w_conv_out,
             w_pool=m_w_pool, s_pool=m_s_pool, w_pool_out=m_w_pool_out, w_out=m_w_out, g_mix_post=m_g_mix_post, g_ffn_pre=m_g_ffn_pre,
             w_ffn_in=m_w_ffn_in, w_ffn_out=m_w_ffn_out, g_ffn_post=m_g_ffn_post, w_ple=m_w_ple, w_ple_gate=m_w_ple_gate)
    V = dict(g_mix_pre=v_g_mix_pre, w_in=v_w_in, w_sgu_s=v_w_sgu_s, b_sgu_s=v_b_sgu_s, g_sgu_v=v_g_sgu_v, b_sgu_v=v_b_sgu_v,
             w_sgu_out=v_w_sgu_out, w_dw=v_w_dw, b_dw=v_b_dw, g_conv_ln=v_g_conv_ln, b_conv_ln=v_b_conv_ln, w_conv_out=v_w_conv_out,
             w_pool=v_w_pool, s_pool=v_s_pool, w_pool_out=v_w_pool_out, w_out=v_w_out, g_mix_post=v_g_mix_post, g_ffn_pre=v_g_ffn_pre,
             w_ffn_in=v_w_ffn_in, w_ffn_out=v_w_ffn_out, g_ffn_post=v_g_ffn_post, w_ple=v_w_ple, w_ple_gate=v_w_ple_gate)

    my_c = lax.axis_index("c")
    my_chip = 2 * lax.axis_index("x") + lax.axis_index("y")
    my_dev = 2 * my_chip + my_c
    s_len = x.shape[1]
    h0 = x.reshape(s_len, D)
    target = loss_target.reshape(s_len, D)

    shard = [{k: W[k][l].astype(BF) for k in _SHARDED} for l in range(2)]
    for l in range(2):
        shard[l]['w_dw'] = w_dw[l]
    mixer_w = ['w_sgu_out', 'w_conv_out', 'w_pool', 'w_pool_out', 'w_out', 'w_dw']
    ffn_w = ['w_ffn_in', 'w_ffn_out', 'w_ple', 'w_ple_gate']
    hosted_gather = {
        'norm_proj_in': (0, mixer_w),
        'conv_fwd': (0, ffn_w),
        'merge_out': (1, mixer_w),
        'norm_proj_ffn': (1, ['w_in']),
        'ffn_out': (1, ffn_w),
    }
    G = [{'w_in': run_comm(Gather([shard[0]['w_in']]), "gather_w_in_0")[0]}, {}]

    def gather_in(layer, call):
        if layer != 0:
            return None, (lambda got: None)
        to_layer, keys = hosted_gather[call]
        return Gather([shard[to_layer][k] for k in keys]), (lambda got: G[to_layer].update(zip(keys, got)))

    def natural_mixer(g):
        wpool = jnp.transpose(g['w_pool'], (1, 0, 2, 3)).reshape(4, POOL_GD, POOL_GD)
        wdw = jnp.transpose(g['w_dw'].reshape(N_DEV, CONV_W, BLK), (1, 0, 2)).reshape(CONV_W, D)
        return dict(wpool=wpool, wdw=_pad_taps(wdw))

    def natural_ffn(g):
        wfi = jnp.transpose(g['w_ffn_in'].reshape(4, 2, D, D_FF // 4), (0, 2, 1, 3)).reshape(4, D, D_FF // 2)
        wple = jnp.transpose(g['w_ple'], (1, 0, 2)).reshape(PLE, D)
        return dict(wfi=wfi, wple=wple)

    def vec(name, layer):
        return W[name][layer].reshape(1, D)

    saved = []
    h = h0
    for l in range(2):
        g = G[l]
        comm, land = gather_in(l, 'norm_proj_in')
        (proj, hn), got = norm_proj(h, vec('g_mix_pre', l), g['w_in'], f"norm_proj_in_{l}", comm=comm)
        land(got)
        nat = natural_mixer(g)
        bsfull = jnp.repeat(b_sgu_s[l].T, BLK, axis=1)
        wst = jnp.swapaxes(w_sgu_s[l], 1, 2)
        sgu, bra = sgu_fwd(proj, w_sgu_s[l], bsfull, vec('g_sgu_v', l), vec('b_sgu_v', l), g['w_sgu_out'], f"sgu_fwd_{l}")
        comm, land = gather_in(l, 'conv_fwd')
        (conv, cb, brb), got = conv_fwd(proj, nat['wdw'], vec('b_dw', l), vec('g_conv_ln', l), vec('b_conv_ln', l), g['w_conv_out'],
                                        f"conv_fwd_{l}", comm=comm)
        land(got)
        nat.update(natural_ffn(g))
        pooled, pm, brc = pool_fwd(proj, nat['wpool'], vec('s_pool', l), g['w_pool_out'], f"pool_fwd_{l}")
        comm, land = gather_in(l, 'merge_out')
        (merged, mo, h1), got = merge_out(proj, bra, brb, brc, h, g['w_out'], vec('g_mix_post', l), f"merge_out_{l}", comm=comm)
        land(got)
        comm, land = gather_in(l, 'norm_proj_ffn')
        (ff, hn2), got = norm_proj(h1, vec('g_ffn_pre', l), nat['wfi'], f"norm_proj_ffn_{l}", comm=comm)
        land(got)
        comm, land = gather_in(l, 'ffn_out')
        (act, f, h2, pg, h3), got = ffn_out(ff, h1, p, g['w_ffn_out'], vec('g_ffn_post', l), g['w_ple_gate'], nat['wple'], l,
                                            f"ffn_out_{l}", comm=comm)
        land(got)
        saved.append(dict(h=h, nat=nat, bsfull=bsfull, wst=wst, proj=proj, hn=hn, sgu=sgu, bra=bra, conv=conv, cb=cb, brb=brb,
                          pooled=pooled, pm=pm, brc=brc, merged=merged, mo=mo, h1=h1, ff=ff, hn2=hn2, act=act, f=f, h2=h2, pg=pg))
        h = h3

    dh = h

    parts = {k: [None, None] for k in _SHARDED}
    small = {k: [None, None] for k in _VECTORS + ['b_sgu_s', 'w_sgu_s', 'w_dw']}
    chip_parts = [{}, {}]
    from_chips = [{}, {}]
    gathered_small = [None, None]

    def to_sibling(layer, keys):
        return ToSibling([parts[k][layer] for k in keys])

    def add_siblings(layer, keys, from_sibling):
        for k, rv in zip(keys, from_sibling):
            st = parts[k][layer]
            mine = lax.dynamic_index_in_dim(st.reshape((4, 2) + st.shape[1:]), my_c, axis=1, keepdims=False)
            cols = st.shape[-1]
            chip_parts[layer][k] = add_bf16(mine.reshape(-1, cols), rv.reshape(-1, cols), f"rs_add_{k}_{layer}").reshape(rv.shape)

    def to_chips(layer, keys):
        return ToChips([chip_parts[layer][k] for k in keys])

    def small_pack(layer):
        return jnp.concatenate([_pack_replicated(small, layer), small['w_dw'][layer]], axis=0)

    ffn_group = ['w_ffn_in', 'w_ffn_out', 'w_ple_gate', 'w_ple']
    mix_group = ['w_out', 'w_sgu_out', 'w_conv_out', 'w_pool_out', 'w_pool']
    big = ['w_in', 'w_ffn_in']
    others = [k for k in _SHARDED if k not in big]
    hosted_rs = {
        'ple_ffn_bwd': (lambda: to_sibling(1, _SHARDED), lambda got: add_siblings(1, _SHARDED, got)),
        'mix_post_bwd': (lambda: to_sibling(0, ffn_group), lambda got: add_siblings(0, ffn_group, got)),
        'sgu_bwd': (lambda: to_chips(1, big), lambda got: from_chips[1].update(zip(big, got))),
        'conv_bwd': (lambda: to_chips(1, others), lambda got: from_chips[1].update(zip(others, got))),
        'seq_bwd': (lambda: to_chips(0, ffn_group), lambda got: from_chips[0].update(zip(ffn_group, got))),
        'wgrad_in_sgu': (lambda: to_sibling(0, mix_group), lambda got: add_siblings(0, mix_group, got)),
        'wgrad_in_seq': (lambda: to_chips(0, mix_group), lambda got: from_chips[0].update(zip(mix_group, got))),
        'wgrad_in_gate': (lambda: Gather([small_pack(1)]), lambda got: gathered_small.__setitem__(1, got[0])),
    }

    def exchange_in(layer, call):
        if layer != 0 or call not in hosted_rs:
            return None, (lambda got: None)
        make, land = hosted_rs[call]
        return make(), land

    for l in (1, 0):
        sv, g, nat = saved[l], G[l], saved[l]['nat']

        def wg(call, a, b, tk, tn, **kw):
            comm, land = exchange_in(l, call)
            if comm is None:
                return wgrad(a, b, tk, tn, f"{call}_{l}", **kw)
            out, got = wgrad(a, b, tk, tn, f"{call}_{l}", comm=comm, **kw)
            land(got)
            return out

        comm, land = exchange_in(l, 'ple_ffn_bwd')
        res, got = ple_ffn_bwd(
            dh, sv['pg'], p, sv['f'], sv['ff'], nat['wple'], g['w_ple_gate'], g['w_ffn_out'], vec('g_ffn_post', l), l,
            f"ple_ffn_bwd_{l}", target=target if l == 1 else None, comm=comm)
        land(got)
        dh2, dpe, dpg, df, dff, small['g_ffn_post'][l] = res[:6]
        if l == 1:
            loss = lax.psum(res[6][0, 0], ("x", "y", "c"))
        dh1, small['g_ffn_pre'][l] = ffn_in_bwd(dff, nat['wfi'], sv['h1'], dh2, vec('g_ffn_pre', l), f"ffn_in_bwd_{l}")
        p_spec = pl.BlockSpec((None, None, min(TS, s_len), PLE), functools.partial(lambda k, n, s, ll: (ll, 0, s, 0), ll=l))
        parts['w_ple'][l] = jnp.transpose(wg('wgrad_ple', p, dpe, PLE, D, a_spec=p_spec).reshape(PLE, N_DEV, BLK), (1, 0, 2))
        parts['w_ple_gate'][l] = wg('wgrad_ple_gate', sv['h2'], dpg, D, D).reshape(N_DEV, BLK, D)
        parts['w_ffn_out'][l] = wg('wgrad_ffn_out', sv['act'], df, D_FF // 2, D).reshape(N_DEV, D_FF // N_DEV, D)
        g_fi = wg('wgrad_ffn_in', sv['hn2'], dff, D, D_FF // 2, stacked=True)
        parts['w_ffn_in'][l] = jnp.transpose(g_fi.reshape(4, D, 2, D_FF // 4), (0, 2, 1, 3)).reshape(N_DEV, D, D_FF // 4)

        comm, land = exchange_in(l, 'mix_post_bwd')
        (dmo, dbra, dbrb, dbrc, dzg, dhn_g, small['g_mix_post'][l]), got = mix_post_bwd(
            dh1, sv['mo'], vec('g_mix_post', l), g['w_out'], sv['proj'], sv['bra'], sv['brb'], sv['brc'], g['w_in'],
            f"mix_post_bwd_{l}", comm=comm)
        land(got)
        comm, land = exchange_in(l, 'sgu_bwd')
        (dzs, dhn_ag, dws, dbs, small['g_sgu_v'][l], small['b_sgu_v'][l]), got = sgu_bwd(
            dbra, g['w_sgu_out'], sv['proj'], w_sgu_s[l], sv['wst'], sv['bsfull'], vec('g_sgu_v', l), vec('b_sgu_v', l), g['w_in'],
            dhn_g, f"sgu_bwd_{l}", comm=comm)
        land(got)
        small['w_sgu_s'][l] = dws
        small['b_sgu_s'][l] = dbs.T
        comm, land = exchange_in(l, 'conv_bwd')
        (dc, dwdw, small['b_dw'][l], small['g_conv_ln'][l], small['b_conv_ln'][l]), got = conv_bwd(
            dbrb, g['w_conv_out'], sv['proj'], sv['conv'], vec('g_conv_ln', l), vec('b_conv_ln', l), f"conv_bwd_{l}", comm=comm)
        land(got)
        small['w_dw'][l] = dwdw
        dmr, q, small['s_pool'][l] = pool_bwd(dbrc, g['w_pool_out'], sv['pooled'], nat['wpool'], vec('s_pool', l), f"pool_bwd_{l}")
        comm, land = exchange_in(l, 'seq_bwd')
        (dzc, dh, small['g_mix_pre'][l]), got = seq_bwd(dc, q, sv['proj'], nat['wdw'], g['w_in'], dhn_ag, sv['h'], dh1,
                                                        vec('g_mix_pre', l), f"seq_bwd_{l}", comm=comm)
        land(got)

        parts['w_out'][l] = wg('wgrad_out', sv['merged'], dmo, D, D).reshape(N_DEV, BLK, D)
        parts['w_sgu_out'][l] = wg('wgrad_sgu_out', sv['sgu'], dbra, D, D).reshape(N_DEV, BLK, D)
        parts['w_conv_out'][l] = wg('wgrad_conv_out', sv['cb'], dbrb, D, D).reshape(N_DEV, BLK, D)
        parts['w_pool_out'][l] = wg('wgrad_pool_out', sv['pm'], dbrc, D, D).reshape(N_DEV, BLK, D)
        g_pool = wg('wgrad_pool', sv['pooled'], dmr, POOL_GD, POOL_GD, diag=True)
        parts['w_pool'][l] = jnp.transpose(g_pool.reshape(4, N_DEV, POOL_GD // N_DEV, POOL_GD), (1, 0, 2, 3))
        parts['w_in'][l] = jnp.concatenate([
            wg('wgrad_in_sgu', sv['hn'], dzs, D, D, stacked=True),
            wg('wgrad_in_seq', sv['hn'], dzc, D, D, stacked=True),
            wg('wgrad_in_gate', sv['hn'], dzg, D, D, stacked=True)], axis=0)
    grad_x = dh.reshape(1, s_len, D)

    add_siblings(0, ['w_in'], run_comm(to_sibling(0, ['w_in']), "rs_to_sibling_w_in_0"))
    from_chips[0]['w_in'] = run_comm(to_chips(0, ['w_in']), "rs_to_chips_w_in_0")[0]
    gathered_small[0] = run_comm(Gather([small_pack(0)]), "all_gather_small_grads_0")[0]

    outs = {}
    for k in _SHARDED:
        cols = W[k].shape[-1]
        pieces = []
        for layer in range(2):
            own = lax.dynamic_index_in_dim(chip_parts[layer][k], my_chip, axis=0, keepdims=False).reshape(-1, cols)
            rv3 = from_chips[layer][k].reshape(3, -1, cols)
            pieces.append([(own, None), (rv3, 0), (rv3, 1), (rv3, 2)])
        res = adamw_layers(W[k].reshape(2, -1, cols), M[k].reshape(2, -1, cols), V[k].reshape(2, -1, cols), pieces, f"adamw_{k}")
        outs[k] = [r.reshape(W[k].shape) for r in res]

    packed = [jnp.stack([_pack_replicated(t, 0), _pack_replicated(t, 1)], axis=0) for t in (W, M, V)]
    rep_res = adamw_layers(*packed, [[(gathered_small[layer], d) for d in range(N_DEV)] for layer in range(2)], "adamw_replicated")
    for idx, res in enumerate(rep_res):
        for name, val in _unpack_replicated(res).items():
            outs.setdefault(name, [None] * 4)[idx] = val
    dw_sum = jnp.stack([sum_slabs(gathered_small[layer][:, _REP_ROWS:, :], f"sum_w_dw_{layer}") for layer in range(2)], axis=0)
    dw_mine = lax.dynamic_slice_in_dim(dw_sum[:, :CONV_W], my_dev * BLK, BLK, axis=2)
    res = adamw(w_dw.reshape(2 * CONV_W, BLK), m_w_dw.reshape(2 * CONV_W, BLK), v_w_dw.reshape(2 * CONV_W, BLK),
                [(dw_mine.reshape(2 * CONV_W, BLK), None)], "adamw_w_dw")
    outs['w_dw'] = [r.reshape(w_dw.shape) for r in res]

    result = [loss, grad_x]
    for idx in range(4):
        result += [outs[k][idx] for k in _WEIGHTS]
    return tuple(result)
```

```python
import functools
import math

import jax
import jax.numpy as jnp
from jax import lax
from jax.experimental import pallas as pl
from jax.experimental.pallas import tpu as pltpu

F32 = jnp.float32
BF = jnp.bfloat16

D = 1024
D_FF = 2816
PLE = 256
N_DEV = 8
HEADS = 8
BLK = 128
CHUNK = 64
CONV_W = 31
POOL_WINDOWS = (2, 4, 8, 16)
POOL_GD = 256
EPS = 1e-6

V7X_VMEM_BYTES = 64 * 2**20
VMEM_LIMIT = V7X_VMEM_BYTES * 7 // 8
HALO = 32
RC = 64
LC = 128
TM = 512
TMB = 256
TMP = 1024
TS = 2048

ADAM_LR, ADAM_B1, ADAM_B2, ADAM_EPS, ADAM_WD, ADAM_STEP = 0.001, 0.9, 0.999, 1e-08, 0.01, 10

MESH = pl.DeviceIdType.MESH
ANY = pl.BlockSpec(memory_space=pl.ANY)

_GELU_K0 = math.sqrt(2.0 / math.pi)
_GELU_K1 = 0.044715


def _dot(a, b):
    return jnp.dot(a, b, preferred_element_type=F32)


def _dot_nt(a, b):
    return lax.dot_general(a, b, (((1,), (1,)), ((), ())), preferred_element_type=F32)


def _dot_tn(a, b):
    return lax.dot_general(a, b, (((0,), (0,)), ((), ())), preferred_element_type=F32)


def _sig(x):
    return 1.0 / (1.0 + jnp.exp(-x))


def _gelu(x):
    t = jnp.tanh(_GELU_K0 * (x + _GELU_K1 * x * x * x))
    return 0.5 * x * (1.0 + t), t


def _gelu_grad(x, t):
    return 0.5 * (1.0 + t) + 0.5 * x * (1.0 - t * t) * _GELU_K0 * (1.0 + 3.0 * _GELU_K1 * x * x)


def _rstd(x):
    return lax.rsqrt(jnp.mean(x * x, axis=-1, keepdims=True) + EPS)


def _rms_bwd(x, gd, r):
    return r * gd - x * (r * r * r) * jnp.mean(gd * x, axis=-1, keepdims=True)


def _ln_fwd(x):
    mu = jnp.mean(x, axis=-1, keepdims=True)
    xc = x - mu
    rs = lax.rsqrt(jnp.mean(xc * xc, axis=-1, keepdims=True) + EPS)
    return xc * rs, rs


def _ln_bwd(dhat, hat, rs):
    return rs * (dhat - jnp.mean(dhat, axis=-1, keepdims=True) - hat * jnp.mean(dhat * hat, axis=-1, keepdims=True))


def _colsum(x):
    return jnp.sum(x, axis=0, keepdims=True)


def _accum(ref, first, val):
    @pl.when(first)
    def _():
        ref[...] = val

    @pl.when(jnp.logical_not(first))
    def _():
        ref[...] += val


def _sgu_mask(transposed):
    r = lax.broadcasted_iota(jnp.int32, (BLK, BLK), 0) // CHUNK
    c = lax.broadcasted_iota(jnp.int32, (BLK, BLK), 1) // CHUNK
    return (r <= c) if transposed else (c <= r)


def _inv_count(i, tm, w):
    t = lax.broadcasted_iota(jnp.int32, (tm, 1), 0) + i * tm
    return 1.0 / jnp.minimum(t + 1, w).astype(F32)


def _count(i, tm, w):
    t = lax.broadcasted_iota(jnp.int32, (tm, 1), 0) + i * tm
    return jnp.minimum(t + 1, w).astype(F32)


def _params(n_grid):
    return pltpu.CompilerParams(dimension_semantics=("arbitrary",) * n_grid, vmem_limit_bytes=VMEM_LIMIT)


def _sds(shape, dtype):
    return jax.ShapeDtypeStruct(shape, dtype)


def _cols(tm, width, cb):
    return pl.BlockSpec((tm, width), lambda i: (i, cb))


def _whole(shape):
    nd = len(shape)
    return pl.BlockSpec(shape, lambda i: (0,) * nd)


def _prev_halo(tm, width, cb):
    return pl.BlockSpec((HALO, width), lambda i: (jnp.maximum(i * (tm // HALO) - 1, 0), cb))


def _next_halo(tm, width, cb, s_len):
    last = s_len // HALO - 1
    return pl.BlockSpec((HALO, width), lambda i: (jnp.minimum((i + 1) * (tm // HALO), last), cb))


def _rowsharded(rows):
    return pl.BlockSpec((N_DEV, rows, D), lambda i: (0, 0, 0))


def _win_block(j):
    return pl.BlockSpec((None, D, D), lambda i: (j, 0, 0))


def _row_tile(rows, cap):
    t = min(rows, cap)
    while rows % t or t % 16:
        t -= 16
    return t


class Gather:
    def __init__(self, arrs):
        self.arrs = list(arrs)
        n = self.n = len(self.arrs)
        self.out_shape = [_sds((N_DEV,) + a.shape, a.dtype) for a in self.arrs]
        self.scratch = [pltpu.SemaphoreType.DMA((n, 7)), pltpu.SemaphoreType.DMA((n, 7)), pltpu.SemaphoreType.DMA((n,))]

    def _plan(self, ins, outs, sems):
        send, recv, local = sems
        x, y, c = lax.axis_index("x"), lax.axis_index("y"), lax.axis_index("c")
        me, sibling = (x, y, c), (x, y, 1 - c)
        chips = [(1 - x, y), (x, 1 - y), (1 - x, 1 - y)]

        def copy(a, k, block, to, src=None):
            dst = outs[a].at[4 * block[0] + 2 * block[1] + block[2]]
            return pltpu.make_async_remote_copy(
                src_ref=dst if src is None else src, dst_ref=dst, send_sem=send.at[a, k], recv_sem=recv.at[a, k],
                device_id=to, device_id_type=MESH)

        mine = [pltpu.make_async_copy(ins[a], outs[a].at[4 * x + 2 * y + c], local.at[a]) for a in range(self.n)]
        first = []
        for a in range(self.n):
            first.append(copy(a, 0, me, sibling, src=ins[a]))
            first += [copy(a, 1 + j, me, (*chip, c), src=ins[a]) for j, chip in enumerate(chips)]
        return me, sibling, chips, c, copy, mine, first

    def start(self, ins, outs, sems):
        *_, mine, first = self._plan(ins, outs, sems)
        for cp in mine + first:
            cp.start()

    def finish(self, ins, outs, sems):
        me, sibling, chips, c, copy, mine, first = self._plan(ins, outs, sems)
        passed = []
        for a in range(self.n):
            for j, chip in enumerate(chips):
                copy(a, 1 + j, (*chip, c), me).wait_recv()
                fwd = copy(a, 4 + j, (*chip, c), sibling)
                fwd.start()
                passed.append(fwd)
        for a in range(self.n):
            copy(a, 0, sibling, me).wait_recv()
            for j, chip in enumerate(chips):
                copy(a, 4 + j, (*chip, 1 - c), me).wait_recv()
        for cp in first + passed:
            cp.wait_send()
        for cp in mine:
            cp.wait()


class ToSibling:
    def __init__(self, parts):
        self.arrs = list(parts)
        n = self.n = len(self.arrs)
        self.out_shape = [_sds((4,) + p.shape[1:], p.dtype) for p in self.arrs]
        self.scratch = [pltpu.SemaphoreType.DMA((n,)), pltpu.SemaphoreType.DMA((n,))]

    def start(self, ins, outs, sems):
        send, recv = sems
        x, y, c = lax.axis_index("x"), lax.axis_index("y"), lax.axis_index("c")
        for a in range(self.n):
            for q in range(4):
                pltpu.make_async_remote_copy(
                    src_ref=ins[a].at[2 * q + 1 - c], dst_ref=outs[a].at[q], send_sem=send.at[a], recv_sem=recv.at[a],
                    device_id=(x, y, 1 - c), device_id_type=MESH).start()

    def finish(self, ins, outs, sems):
        send, recv = sems
        x, y, c = lax.axis_index("x"), lax.axis_index("y"), lax.axis_index("c")
        for a in range(self.n):
            pltpu.make_async_remote_copy(
                src_ref=outs[a], dst_ref=outs[a], send_sem=send.at[a], recv_sem=recv.at[a],
                device_id=(x, y, 1 - c), device_id_type=MESH).wait()


class ToChips:
    def __init__(self, cps):
        self.arrs = list(cps)
        n = self.n = len(self.arrs)
        self.out_shape = [_sds((3,) + p.shape[1:], p.dtype) for p in self.arrs]
        self.scratch = [pltpu.SemaphoreType.DMA((n,)), pltpu.SemaphoreType.DMA((n,))]

    def start(self, ins, outs, sems):
        send, recv = sems
        x, y, c = lax.axis_index("x"), lax.axis_index("y"), lax.axis_index("c")
        for a in range(self.n):
            for r, (px, py) in enumerate([(1 - x, y), (x, 1 - y), (1 - x, 1 - y)]):
                pltpu.make_async_remote_copy(
                    src_ref=ins[a].at[2 * px + py], dst_ref=outs[a].at[r], send_sem=send.at[a], recv_sem=recv.at[a],
                    device_id=(px, py, c), device_id_type=MESH).start()

    def finish(self, ins, outs, sems):
        send, recv = sems
        x, y, c = lax.axis_index("x"), lax.axis_index("y"), lax.axis_index("c")
        for a in range(self.n):
            pltpu.make_async_remote_copy(
                src_ref=outs[a], dst_ref=outs[a], send_sem=send.at[a], recv_sem=recv.at[a],
                device_id=(x, y, c), device_id_type=MESH).wait()


def run_comm(comm, name):
    n = comm.n

    def body(*refs):
        ins, outs, sems = refs[:n], refs[n:2 * n], refs[2 * n:]
        comm.start(ins, outs, sems)
        comm.finish(ins, outs, sems)

    return pl.pallas_call(body, name=name, out_shape=comm.out_shape, in_specs=[ANY] * n, out_specs=[ANY] * n,
                          scratch_shapes=comm.scratch)(*comm.arrs)


def _pcall(body, args, *, name, grid, in_specs, out_specs, out_shape, scratch_shapes=(), comm=None):
    params = _params(len(grid))
    scratch_shapes = list(scratch_shapes)
    if comm is None:
        outs = pl.pallas_call(body, name=name, grid=grid, in_specs=in_specs, out_specs=out_specs, out_shape=out_shape,
                              scratch_shapes=scratch_shapes, compiler_params=params)(*args)
        return outs, None
    n_in, n_out, n_scr, nc = len(in_specs), len(out_specs), len(scratch_shapes), comm.n

    def hosted(*refs):
        ins, cins = refs[:n_in], refs[n_in:n_in + nc]
        o0 = n_in + nc
        outs, couts = refs[o0:o0 + n_out], refs[o0 + n_out:o0 + n_out + nc]
        s0 = o0 + n_out + nc
        scr, csems = refs[s0:s0 + n_scr], refs[s0 + n_scr:]
        first = pl.program_id(0) == 0
        last = pl.program_id(0) == grid[0] - 1
        for ax in range(1, len(grid)):
            first = jnp.logical_and(first, pl.program_id(ax) == 0)
            last = jnp.logical_and(last, pl.program_id(ax) == grid[ax] - 1)

        @pl.when(first)
        def _():
            comm.start(cins, couts, csems)
        body(*ins, *outs, *scr)

        @pl.when(last)
        def _():
            comm.finish(cins, couts, csems)

    res = pl.pallas_call(
        hosted, name=name, grid=grid, in_specs=list(in_specs) + [ANY] * nc, out_specs=list(out_specs) + [ANY] * nc,
        out_shape=list(out_shape) + comm.out_shape, scratch_shapes=scratch_shapes + comm.scratch,
        compiler_params=params)(*args, *comm.arrs)
    return res[:n_out], res[n_out:]


def add_core_side(parts, from_sibling, core, name):
    _, rows, cols = parts.shape
    tr = _row_tile(rows, 512)

    def body(c_ref, a_ref, b_ref, o_ref):
        o_ref[...] = (a_ref[...].astype(F32) + b_ref[...].astype(F32)).astype(o_ref.dtype)

    side = pl.BlockSpec((None, tr, cols), lambda q, i, c: (q, i, 0))
    return pl.pallas_call(
        body, name=name,
        grid_spec=pltpu.PrefetchScalarGridSpec(
            num_scalar_prefetch=1, grid=(4, rows // tr),
            in_specs=[pl.BlockSpec((None, tr, cols), lambda q, i, c: (2 * q + c[0], i, 0)), side], out_specs=side),
        out_shape=_sds(from_sibling.shape, BF), compiler_params=_params(2))(core, parts, from_sibling)


def adamw(w, m, v, pieces, name):
    rows, cols = w.shape
    tr = _row_tile(rows, 256) if rows % 16 == 0 else rows
    np_ = len(pieces)
    c1 = 1.0 / (1.0 - ADAM_B1 ** ADAM_STEP)
    c2 = 1.0 / (1.0 - ADAM_B2 ** ADAM_STEP)

    def body(*refs):
        w_ref, m_ref, v_ref = refs[:3]
        p_refs = refs[3:3 + np_]
        g_ref, d_ref, nm_ref, nv_ref = refs[3 + np_:]
        g = p_refs[0][...].astype(F32)
        for pr in p_refs[1:]:
            g = g + pr[...].astype(F32)
        nm = ADAM_B1 * m_ref[...] + (1.0 - ADAM_B1) * g
        nv = ADAM_B2 * v_ref[...] + (1.0 - ADAM_B2) * (g * g)
        g_ref[...] = g
        nm_ref[...] = nm
        nv_ref[...] = nv
        d_ref[...] = -ADAM_LR * ((nm * c1) / (jnp.sqrt(nv * c2) + ADAM_EPS) + ADAM_WD * w_ref[...])

    spec = pl.BlockSpec((tr, cols), lambda i: (i, 0))
    p_specs = []
    for arr, k in pieces:
        if k is None:
            p_specs.append(spec)
        else:
            p_specs.append(pl.BlockSpec((None, tr, cols), functools.partial(lambda i, kk: (kk, i, 0), kk=k)))
    out = _sds(w.shape, F32)
    return pl.pallas_call(body, name=name, grid=(rows // tr,), in_specs=[spec] * 3 + p_specs, out_specs=[spec] * 4,
                          out_shape=[out] * 4, compiler_params=_params(1))(w, m, v, *[a for a, _ in pieces])


def adamw_layers(w, m, v, pieces, name):
    _, rows, cols = w.shape
    tr = _row_tile(rows, 256)
    nt = rows // tr
    counts = [len(pieces[0]), len(pieces[1])]
    c1 = 1.0 / (1.0 - ADAM_B1 ** ADAM_STEP)
    c2 = 1.0 / (1.0 - ADAM_B2 ** ADAM_STEP)

    def body(*refs):
        w_ref, m_ref, v_ref = refs[:3]
        p_refs = refs[3:3 + sum(counts)]
        g_ref, d_ref, nm_ref, nv_ref = refs[3 + sum(counts):]
        sums = []
        for group in (p_refs[:counts[0]], p_refs[counts[0]:]):
            s = group[0][...].astype(F32)
            for pr in group[1:]:
                s = s + pr[...].astype(F32)
            sums.append(s)
        g = jnp.where(pl.program_id(0) == 0, sums[0], sums[1])
        nm = ADAM_B1 * m_ref[...] + (1.0 - ADAM_B1) * g
        nv = ADAM_B2 * v_ref[...] + (1.0 - ADAM_B2) * (g * g)
        g_ref[...] = g
        nm_ref[...] = nm
        nv_ref[...] = nv
        d_ref[...] = -ADAM_LR * ((nm * c1) / (jnp.sqrt(nv * c2) + ADAM_EPS) + ADAM_WD * w_ref[...])

    def rows_of(layer):
        parked = nt - 1 if layer == 0 else 0
        return lambda l, i: jnp.where(l == layer, i, parked)

    spec = pl.BlockSpec((None, tr, cols), lambda l, i: (l, i, 0))
    p_specs, p_args = [], []
    for layer in (0, 1):
        row_of = rows_of(layer)
        for arr, k in pieces[layer]:
            p_args.append(arr)
            if k is None:
                p_specs.append(pl.BlockSpec((tr, cols), functools.partial(lambda l, i, f: (f(l, i), 0), f=row_of)))
            else:
                p_specs.append(pl.BlockSpec((None, tr, cols), functools.partial(lambda l, i, f, kk: (kk, f(l, i), 0), f=row_of, kk=k)))
    out = _sds(w.shape, F32)
    return pl.pallas_call(body, name=name, grid=(2, nt), in_specs=[spec] * 3 + p_specs, out_specs=[spec] * 4,
                          out_shape=[out] * 4, compiler_params=_params(2))(w, m, v, *p_args)


def sum_slabs(g, name):
    n, rows, cols = g.shape

    def body(g_ref, o_ref):
        s = g_ref[0]
        for k in range(1, n):
            s = s + g_ref[k]
        o_ref[...] = s

    return pl.pallas_call(body, name=name, out_shape=_sds((rows, cols), F32))(g)


def norm_proj(h, g, w, name, transposed=False, comm=None):
    s_len = h.shape[0]
    nb, tn = (w.shape[0], w.shape[1]) if transposed else (w.shape[0], w.shape[2])
    tm = min(TMP, s_len)
    nt = s_len // tm
    matmul = _dot_nt if transposed else _dot

    def body(h_ref, g_ref, w_ref, o_ref, hn_ref, hn_s):
        rows = pl.ds(pl.multiple_of(pl.program_id(1) * tm, tm), tm)

        @pl.when(pl.program_id(0) == 0)
        def _():
            x = h_ref[...]
            hn = (x * _rstd(x) * g_ref[...]).astype(BF)
            hn_s[rows, :] = hn
            hn_ref[...] = hn
        o_ref[...] = matmul(hn_s[rows, :], w_ref[...]).astype(BF)

    def first_pass_rows(j, i):
        return (jnp.where(j == 0, i, nt - 1), 0)

    return _pcall(
        body, (h, g, w), name=name, grid=(nb, nt),
        in_specs=[pl.BlockSpec((tm, D), first_pass_rows), pl.BlockSpec((1, D), lambda j, i: (0, 0)),
                  pl.BlockSpec((None,) + w.shape[1:], lambda j, i: (j, 0, 0))],
        out_specs=[pl.BlockSpec((tm, tn), lambda j, i: (i, j)), pl.BlockSpec((tm, D), first_pass_rows)],
        out_shape=[_sds((s_len, nb * tn), BF), _sds((s_len, D), BF)],
        scratch_shapes=[pltpu.VMEM((s_len, D), BF)], comm=comm)


def _sgu_mix(ws_ref, vln_s, mix_s, bs_ref, tm, transposed):
    mask = _sgu_mask(transposed)
    for hd in range(HEADS):
        wm = jnp.where(mask, ws_ref[hd], 0.0).astype(BF)
        cs = slice(hd * BLK, (hd + 1) * BLK)
        for n in range(tm // BLK):
            rs = slice(n * BLK, (n + 1) * BLK)
            r = _dot(wm, vln_s[rs, cs])
            mix_s[rs, cs] = r if bs_ref is None else r + bs_ref[:, cs]


def sgu_fwd(proj, ws, bsfull, gv, bv, wo, name):
    s_len = proj.shape[0]
    tm = min(TM, s_len)

    def body(zu_ref, zv_ref, ws_ref, bs_ref, gv_ref, bv_ref, wo_ref, sgu_ref, br_ref, vln_s, mix_s):
        u, _ = _gelu(zu_ref[...].astype(F32))
        v, _ = _gelu(zv_ref[...].astype(F32))
        vhat, _ = _ln_fwd(v)
        vln_s[...] = (vhat * gv_ref[...] + bv_ref[...]).astype(BF)
        _sgu_mix(ws_ref, vln_s, mix_s, bs_ref, tm, False)
        sgu = (u * mix_s[...]).astype(BF)
        sgu_ref[...] = sgu
        br_ref[...] = _dot(sgu, wo_ref[...].reshape(D, D)).astype(BF)

    return pl.pallas_call(
        body, name=name, grid=(s_len // tm,),
        in_specs=[_cols(tm, D, 0), _cols(tm, D, 1), _whole((HEADS, BLK, BLK)), _whole((BLK, D)), _whole((1, D)), _whole((1, D)),
                  _rowsharded(BLK)],
        out_specs=[_cols(tm, D, 0)] * 2, out_shape=[_sds((s_len, D), BF)] * 2,
        scratch_shapes=[pltpu.VMEM((tm, D), BF), pltpu.VMEM((tm, D), F32)], compiler_params=_params(1),
    )(proj, proj, ws, bsfull, gv, bv, wo)


def _causal_conv(ext_s, out_s, wdw_ref, bias_ref, tm):
    def chunk(ci, carry):
        r0 = pl.multiple_of((ci // (D // LC)) * RC, RC)
        l0 = pl.multiple_of((ci % (D // LC)) * LC, LC)
        win = ext_s[pl.ds(r0, RC + HALO), pl.ds(l0, LC)]
        acc = jnp.broadcast_to(bias_ref[:, pl.ds(l0, LC)], (RC, LC))
        for r in range(8):
            wr = win if r == 0 else pltpu.roll(win, r, 0)
            for m in range(4):
                d = 8 * m + r
                if d < CONV_W:
                    k = CONV_W - 1 - d
                    acc = acc + wdw_ref[k:k + 1, pl.ds(l0, LC)] * wr[HALO - 8 * m:HALO - 8 * m + RC]
        out_s[pl.ds(r0, RC), pl.ds(l0, LC)] = acc
        return carry
    lax.fori_loop(0, (tm // RC) * (D // LC), chunk, 0)


def _glu_ext(a_ref, g_ref, ah_ref, gh_ref, ext_s, first):
    hh = ah_ref[...].astype(F32) * _sig(gh_ref[...].astype(F32))
    ext_s[0:HALO, :] = jnp.where(first, 0.0, hh)
    ext_s[HALO:, :] = a_ref[...].astype(F32) * _sig(g_ref[...].astype(F32))


def conv_fwd(proj, wdw, bdw, gln, bln, wo, name, comm=None):
    s_len = proj.shape[0]
    tm = min(TM, s_len)

    def body(a_ref, g_ref, ah_ref, gh_ref, wdw_ref, bdw_ref, gln_ref, bln_ref, wo_ref, cv_ref, cb_ref, br_ref, ext_s, conv_s):
        _glu_ext(a_ref, g_ref, ah_ref, gh_ref, ext_s, pl.program_id(0) == 0)
        _causal_conv(ext_s, conv_s, wdw_ref, bdw_ref, tm)
        cv = conv_s[...].astype(BF)
        cv_ref[...] = cv
        chat, _ = _ln_fwd(cv.astype(F32))
        yl = chat * gln_ref[...] + bln_ref[...]
        cb = (yl * _sig(yl)).astype(BF)
        cb_ref[...] = cb
        br_ref[...] = _dot(cb, wo_ref[...].reshape(D, D)).astype(BF)

    return _pcall(
        body, (proj, proj, proj, proj, wdw, bdw, gln, bln, wo), name=name, grid=(s_len // tm,),
        in_specs=[_cols(tm, D, 2), _cols(tm, D, 3), _prev_halo(tm, D, 2), _prev_halo(tm, D, 3), _whole((HALO, D)),
                  _whole((1, D)), _whole((1, D)), _whole((1, D)), _rowsharded(BLK)],
        out_specs=[_cols(tm, D, 0)] * 3, out_shape=[_sds((s_len, D), BF)] * 3,
        scratch_shapes=[pltpu.VMEM((tm + HALO, D), F32), pltpu.VMEM((tm, D), F32)], comm=comm)


def pool_fwd(proj, wpool, spool, wo, name):
    s_len = proj.shape[0]
    tm = min(TM, s_len)

    def body(z_ref, zh_ref, wp_ref, sp_ref, wo_ref, pooled_ref, pm_ref, br_ref, ext_s, mr_s):
        i = pl.program_id(0)
        ext_s[0:HALO, :] = jnp.where(i == 0, 0.0, zh_ref[...].astype(F32))
        ext_s[HALO:, :] = z_ref[...].astype(F32)
        for gi, w in enumerate(POOL_WINDOWS):
            cs = slice(gi * POOL_GD, (gi + 1) * POOL_GD)
            e = ext_s[:, cs]
            s = e
            sh = 1
            while sh < w:
                s = s + pltpu.roll(s, sh, 0)
                sh *= 2
            pooled = (s[HALO:] * _inv_count(i, tm, w) - e[HALO:]).astype(BF)
            pooled_ref[:, cs] = pooled
            mr_s[:, cs] = _dot(pooled, wp_ref[gi])
        pm = (mr_s[...] * sp_ref[...]).astype(BF)
        pm_ref[...] = pm
        br_ref[...] = _dot(pm, wo_ref[...].reshape(D, D)).astype(BF)

    return pl.pallas_call(
        body, name=name, grid=(s_len // tm,),
        in_specs=[_cols(tm, D, 4), _prev_halo(tm, D, 4), _whole((4, POOL_GD, POOL_GD)), _whole((1, D)), _rowsharded(BLK)],
        out_specs=[_cols(tm, D, 0)] * 3, out_shape=[_sds((s_len, D), BF)] * 3,
        scratch_shapes=[pltpu.VMEM((tm + HALO, D), F32), pltpu.VMEM((tm, D), F32)], compiler_params=_params(1),
    )(proj, proj, wpool, spool, wo)


def merge_out(proj, bra, brb, brc, h, wout, gpost, name, comm=None):
    s_len = h.shape[0]
    tm = min(TM, s_len)

    def body(z0, z1, z2, a_ref, b_ref, c_ref, h_ref, wo_ref, g_ref, mg_ref, mo_ref, h1_ref):
        merged = (_sig(z0[...].astype(F32)) * a_ref[...].astype(F32) + _sig(z1[...].astype(F32)) * b_ref[...].astype(F32)
                  + _sig(z2[...].astype(F32)) * c_ref[...].astype(F32)).astype(BF)
        mg_ref[...] = merged
        mo = _dot(merged, wo_ref[...].reshape(D, D))
        mo_ref[...] = mo.astype(BF)
        h1_ref[...] = h_ref[...] + mo * _rstd(mo) * g_ref[...]

    row = _cols(tm, D, 0)
    return _pcall(
        body, (proj, proj, proj, bra, brb, brc, h, wout, gpost), name=name, grid=(s_len // tm,),
        in_specs=[_cols(tm, D, 5), _cols(tm, D, 6), _cols(tm, D, 7), row, row, row, row, _rowsharded(BLK), _whole((1, D))],
        out_specs=[row] * 3, out_shape=[_sds((s_len, D), BF), _sds((s_len, D), BF), _sds((s_len, D), F32)], comm=comm)


def _p_spec(tm, layer):
    return pl.BlockSpec((None, None, tm, PLE), lambda i: (layer, 0, i, 0))


def ffn_out(ff, h1, p, wfo, gpost, wpg, wple, layer, name, comm=None):
    s_len = h1.shape[0]
    tm = min(TMB, s_len)

    def body(fg_ref, fu_ref, h1_ref, p_ref, wfo_ref, g_ref, wpg_ref, wple_ref, act_ref, f_ref, h2_ref, pg_ref, h3_ref):
        gt = fg_ref[...].astype(F32)
        act = (gt * _sig(gt) * fu_ref[...].astype(F32)).astype(BF)
        act_ref[...] = act
        f = _dot(act, wfo_ref[...].reshape(D_FF, D))
        f_ref[...] = f.astype(BF)
        h2 = h1_ref[...] + f * _rstd(f) * g_ref[...]
        h2_ref[...] = h2
        pg = _dot(h2.astype(BF), wpg_ref[...].reshape(D, D)).astype(BF)
        pg_ref[...] = pg
        pe = _dot(p_ref[...].astype(BF), wple_ref[...])
        h3_ref[...] = h2 + _sig(pg.astype(F32)) * pe

    row = _cols(tm, D, 0)
    return _pcall(
        body, (ff, ff, h1, p, wfo, gpost, wpg, wple), name=name, grid=(s_len // tm,),
        in_specs=[_cols(tm, D_FF, 0), _cols(tm, D_FF, 1), row, _p_spec(tm, layer), _rowsharded(D_FF // N_DEV),
                  _whole((1, D)), _rowsharded(BLK), _whole((PLE, D))],
        out_specs=[_cols(tm, D_FF, 0), row, row, row, row],
        out_shape=[_sds((s_len, D_FF), BF), _sds((s_len, D), BF), _sds((s_len, D), F32), _sds((s_len, D), BF), _sds((s_len, D), F32)],
        comm=comm)


def ple_ffn_bwd(dh3, pg, p, f, ff, wple, wpg, wfo, gpost, layer, name, target=None, comm=None):
    s_len = dh3.shape[0]
    tm = min(TMB, s_len)
    nt = s_len // tm
    with_loss = target is not None

    def body(*refs):
        if with_loss:
            t_ref, refs, loss_ref, loss_acc = refs[0], refs[1:-2], refs[-2], refs[-1]
        (dh3_ref, pg_ref, p_ref, f_ref, fg_ref, fu_ref, wple_ref, wpg_ref, wfo_ref, g_ref,
         dh2_ref, dpe_ref, dpg_ref, df_ref, dff_ref, dg_ref) = refs
        i = pl.program_id(0)
        dh3v = dh3_ref[...]
        if with_loss:
            err = dh3v - t_ref[...]
            dh3v = err * (1.0 / D)
            _accum(loss_acc, i == 0, _colsum(err * err))

            @pl.when(i == nt - 1)
            def _():
                loss_ref[...] = jnp.broadcast_to(jnp.sum(loss_acc[...], axis=1, keepdims=True) * (0.5 / D), (1, LC))
        s = _sig(pg_ref[...].astype(F32))
        pe = _dot(p_ref[...].astype(BF), wple_ref[...])
        dpe_ref[...] = (dh3v * s).astype(BF)
        dpg = (dh3v * pe * s * (1.0 - s)).astype(BF)
        dpg_ref[...] = dpg
        dh2 = dh3v + _dot_nt(dpg, wpg_ref[...].reshape(D, D))
        dh2_ref[...] = dh2
        fv = f_ref[...].astype(F32)
        r = _rstd(fv)
        _accum(dg_ref, i == 0, _colsum(dh2 * fv * r))
        df = _rms_bwd(fv, dh2 * g_ref[...], r).astype(BF)
        df_ref[...] = df
        dact = _dot_nt(df, wfo_ref[...].reshape(D_FF, D))
        gt = fg_ref[...].astype(F32)
        sg = _sig(gt)
        up = fu_ref[...].astype(F32)
        dff_ref[:, 0:D_FF] = (dact * up * sg * (1.0 + gt * (1.0 - sg))).astype(BF)
        dff_ref[:, D_FF:2 * D_FF] = (dact * gt * sg).astype(BF)

    row = _cols(tm, D, 0)
    args = (dh3, pg, p, f, ff, ff, wple, wpg, wfo, gpost)
    in_specs = [row, row, _p_spec(tm, layer), row, _cols(tm, D_FF, 0), _cols(tm, D_FF, 1), _whole((PLE, D)),
                _rowsharded(BLK), _rowsharded(D_FF // N_DEV), _whole((1, D))]
    out_specs = [row, row, row, row, _cols(tm, 2 * D_FF, 0), _whole((1, D))]
    out_shape = [_sds((s_len, D), F32), _sds((s_len, D), BF), _sds((s_len, D), BF), _sds((s_len, D), BF),
                 _sds((s_len, 2 * D_FF), BF), _sds((1, D), F32)]
    scratch = []
    if with_loss:
        args, in_specs = (target,) + args, [row] + in_specs
        out_specs, out_shape = out_specs + [_whole((1, LC))], out_shape + [_sds((1, LC), F32)]
        scratch = [pltpu.VMEM((1, D), F32)]
    return _pcall(body, args, name=name, grid=(nt,), in_specs=in_specs, out_specs=out_specs, out_shape=out_shape,
                  scratch_shapes=scratch, comm=comm)


def ffn_in_bwd(dff, wt, h1, dh2, gpre, name, comm=None):
    s_len = h1.shape[0]
    tm = min(TMB, s_len)
    nb, tn, _ = wt.shape

    def body(dff_ref, w_ref, h1_ref, dh2_ref, g_ref, dh1_ref, dg_ref):
        dhn = _dot(dff_ref[:, 0:tn], w_ref[0])
        for j in range(1, nb):
            dhn = dhn + _dot(dff_ref[:, j * tn:(j + 1) * tn], w_ref[j])
        x = h1_ref[...]
        r = _rstd(x)
        _accum(dg_ref, pl.program_id(0) == 0, _colsum(dhn * x * r))
        dh1_ref[...] = dh2_ref[...] + _rms_bwd(x, dhn * g_ref[...], r)

    row = _cols(tm, D, 0)
    return _pcall(
        body, (dff, wt, h1, dh2, gpre), name=name, grid=(s_len // tm,),
        in_specs=[_cols(tm, nb * tn, 0), pl.BlockSpec((nb, tn, D), lambda i: (0, 0, 0), pipeline_mode=pl.Buffered(1)), row, row,
                  _whole((1, D))],
        out_specs=[row, _whole((1, D))],
        out_shape=[_sds((s_len, D), F32), _sds((1, D), F32)], comm=comm)


def mix_post_bwd(dh1, mo, gpost, wout, proj, bra, brb, brc, win, name, comm=None):
    s_len = dh1.shape[0]
    tm = min(TMB, s_len)

    def body(dh1_ref, mo_ref, g_ref, wo_ref, z0, z1, z2, a_ref, b_ref, c_ref, w5, w6, w7,
             dmo_ref, da_ref, db_ref, dc_ref, dz_ref, dhn_ref, dg_ref):
        i = pl.program_id(0)
        dh1v = dh1_ref[...]
        mo_v = mo_ref[...].astype(F32)
        r = _rstd(mo_v)
        _accum(dg_ref, i == 0, _colsum(dh1v * mo_v * r))
        dmo = _rms_bwd(mo_v, dh1v * g_ref[...], r).astype(BF)
        dmo_ref[...] = dmo
        dmerged = _dot_nt(dmo, wo_ref[...].reshape(D, D))
        dhn = jnp.zeros((tm, D), F32)
        for k, (z, br, dbr, w) in enumerate(((z0, a_ref, da_ref, w5), (z1, b_ref, db_ref, w6), (z2, c_ref, dc_ref, w7))):
            s = _sig(z[...].astype(F32))
            dbr[...] = (dmerged * s).astype(BF)
            dz = (dmerged * br[...].astype(F32) * s * (1.0 - s)).astype(BF)
            dz_ref[:, k * D:(k + 1) * D] = dz
            dhn = dhn + _dot_nt(dz, w[...])
        dhn_ref[...] = dhn

    row = _cols(tm, D, 0)
    return _pcall(
        body, (dh1, mo, gpost, wout, proj, proj, proj, bra, brb, brc, win, win, win), name=name, grid=(s_len // tm,),
        in_specs=[row, row, _whole((1, D)), _rowsharded(BLK), _cols(tm, D, 5), _cols(tm, D, 6), _cols(tm, D, 7), row, row, row,
                  _win_block(5), _win_block(6), _win_block(7)],
        out_specs=[row, row, row, row, _cols(tm, 3 * D, 0), row, _whole((1, D))],
        out_shape=[_sds((s_len, D), BF)] * 4 + [_sds((s_len, 3 * D), BF), _sds((s_len, D), F32), _sds((1, D), F32)], comm=comm)


def sgu_bwd(dbr, wo, proj, ws, wst, bsfull, gv, bv, win, dhn_in, name, comm=None):
    s_len = dbr.shape[0]
    tm = min(TMB, s_len)
    nt = s_len // tm

    def body(dbr_ref, wo_ref, zu_ref, zv_ref, ws_ref, wst_ref, bs_ref, gv_ref, bv_ref, w0, w1, dhn_in_ref,
             dz_ref, dhn_ref, dws_ref, dbs_ref, dgv_ref, dbv_ref, vln_s, mix_s, dmix_s, dvln_s, bs_acc):
        i = pl.program_id(0)
        first = i == 0
        dsgu = _dot_nt(dbr_ref[...], wo_ref[...].reshape(D, D))
        zu = zu_ref[...].astype(F32)
        zv = zv_ref[...].astype(F32)
        u, tu = _gelu(zu)
        v, tv = _gelu(zv)
        vhat, rs = _ln_fwd(v)
        vln_s[...] = (vhat * gv_ref[...] + bv_ref[...]).astype(BF)
        _sgu_mix(ws_ref, vln_s, mix_s, bs_ref, tm, False)
        du = dsgu * mix_s[...]
        dmix = dsgu * u
        dmix_s[...] = dmix.astype(BF)
        blocks = dmix[0:BLK]
        for n in range(1, tm // BLK):
            blocks = blocks + dmix[n * BLK:(n + 1) * BLK]
        _accum(bs_acc, first, blocks)
        for hd in range(HEADS):
            cs = slice(hd * BLK, (hd + 1) * BLK)
            g = _dot_nt(dmix_s[0:BLK, cs], vln_s[0:BLK, cs])
            for n in range(1, tm // BLK):
                g = g + _dot_nt(dmix_s[n * BLK:(n + 1) * BLK, cs], vln_s[n * BLK:(n + 1) * BLK, cs])

            @pl.when(first)
            def _():
                dws_ref[hd] = g

            @pl.when(jnp.logical_not(first))
            def _():
                dws_ref[hd] += g
        _sgu_mix(wst_ref, dmix_s, dvln_s, None, tm, True)
        dvln = dvln_s[...]
        _accum(dgv_ref, first, _colsum(dvln * vhat))
        _accum(dbv_ref, first, _colsum(dvln))
        dv = _ln_bwd(dvln * gv_ref[...], vhat, rs)
        dzu = (du * _gelu_grad(zu, tu)).astype(BF)
        dzv = (dv * _gelu_grad(zv, tv)).astype(BF)
        dz_ref[:, 0:D] = dzu
        dz_ref[:, D:2 * D] = dzv
        dhn_ref[...] = dhn_in_ref[...] + _dot_nt(dzu, w0[...]) + _dot_nt(dzv, w1[...])

        @pl.when(i == nt - 1)
        def _():
            mask = _sgu_mask(False)
            for hd in range(HEADS):
                dws_ref[hd] = jnp.where(mask, dws_ref[hd], 0.0)
                dbs_ref[:, hd:hd + 1] = jnp.sum(bs_acc[:, hd * BLK:(hd + 1) * BLK], axis=1, keepdims=True)

    row = _cols(tm, D, 0)
    vec = _whole((1, D))
    return _pcall(
        body, (dbr, wo, proj, proj, ws, wst, bsfull, gv, bv, win, win, dhn_in), name=name, grid=(nt,),
        in_specs=[row, _rowsharded(BLK), _cols(tm, D, 0), _cols(tm, D, 1), _whole((HEADS, BLK, BLK)), _whole((HEADS, BLK, BLK)),
                  _whole((BLK, D)), vec, vec, _win_block(0), _win_block(1), row],
        out_specs=[_cols(tm, 2 * D, 0), row, _whole((HEADS, BLK, BLK)), _whole((BLK, HEADS)), vec, vec],
        out_shape=[_sds((s_len, 2 * D), BF), _sds((s_len, D), F32), _sds((HEADS, BLK, BLK), F32), _sds((BLK, HEADS), F32),
                   _sds((1, D), F32), _sds((1, D), F32)],
        scratch_shapes=[pltpu.VMEM((tm, D), BF), pltpu.VMEM((tm, D), F32), pltpu.VMEM((tm, D), BF), pltpu.VMEM((tm, D), F32),
                        pltpu.VMEM((BLK, D), F32)], comm=comm)


def conv_bwd(dbr, wo, proj, conv, gln, bln, name, comm=None):
    s_len = dbr.shape[0]
    tm = min(TMB, s_len)
    nt = s_len // tm

    def body(dbr_ref, wo_ref, a_ref, g_ref, ah_ref, gh_ref, cv_ref, gln_ref, bln_ref,
             dc_ref, dw_ref, dbdw_ref, dgln_ref, dbln_ref, ext_s, dc_s, dw_acc):
        i = pl.program_id(0)
        first = i == 0
        dcb = _dot_nt(dbr_ref[...], wo_ref[...].reshape(D, D))
        _glu_ext(a_ref, g_ref, ah_ref, gh_ref, ext_s, first)
        chat, rs = _ln_fwd(cv_ref[...].astype(F32))
        yl = chat * gln_ref[...] + bln_ref[...]
        sy = _sig(yl)
        dyl = dcb * sy * (1.0 + yl * (1.0 - sy))
        _accum(dgln_ref, first, _colsum(dyl * chat))
        _accum(dbln_ref, first, _colsum(dyl))
        dc = _ln_bwd(dyl * gln_ref[...], chat, rs)
        _accum(dbdw_ref, first, _colsum(dc))
        dc_ref[...] = dc.astype(BF)
        dc_s[...] = dc

        @pl.when(first)
        def _():
            dw_acc[...] = jnp.zeros_like(dw_acc)

        def chunk(ci, carry):
            r0 = pl.multiple_of((ci // (D // LC)) * RC, RC)
            l0 = pl.multiple_of((ci % (D // LC)) * LC, LC)
            win = ext_s[pl.ds(r0, RC + HALO), pl.ds(l0, LC)]
            dcw = dc_s[pl.ds(r0, RC), pl.ds(l0, LC)]
            for r in range(8):
                wr = win if r == 0 else pltpu.roll(win, r, 0)
                for m in range(4):
                    d = 8 * m + r
                    if d < CONV_W:
                        k = CONV_W - 1 - d
                        prod = dcw * wr[HALO - 8 * m:HALO - 8 * m + RC]
                        dw_acc[k * 8:(k + 1) * 8, pl.ds(l0, LC)] += prod.reshape(RC // 8, 8, LC).sum(axis=0)
            return carry
        lax.fori_loop(0, (tm // RC) * (D // LC), chunk, 0)

        @pl.when(i == nt - 1)
        def _():
            dw_ref[...] = dw_acc[...].reshape(HALO, 8, D).sum(axis=1)

    row = _cols(tm, D, 0)
    vec = _whole((1, D))
    return _pcall(
        body, (dbr, wo, proj, proj, proj, proj, conv, gln, bln), name=name, grid=(nt,),
        in_specs=[row, _rowsharded(BLK), _cols(tm, D, 2), _cols(tm, D, 3), _prev_halo(tm, D, 2), _prev_halo(tm, D, 3),
                  row, vec, vec],
        out_specs=[row, _whole((HALO, D)), vec, vec, vec],
        out_shape=[_sds((s_len, D), BF), _sds((HALO, D), F32), _sds((1, D), F32), _sds((1, D), F32), _sds((1, D), F32)],
        scratch_shapes=[pltpu.VMEM((tm + HALO, D), F32), pltpu.VMEM((tm, D), F32), pltpu.VMEM((HALO * 8, D), F32)], comm=comm)


def pool_bwd(dbr, wo, pooled, wpool, spool, name):
    s_len = dbr.shape[0]
    tm = min(TMB, s_len)

    def body(dbr_ref, wo_ref, pl_ref, wp_ref, sp_ref, dmr_ref, q_ref, dsp_ref, mr_s):
        i = pl.program_id(0)
        dpm = _dot_nt(dbr_ref[...], wo_ref[...].reshape(D, D))
        for gi in range(4):
            cs = slice(gi * POOL_GD, (gi + 1) * POOL_GD)
            mr_s[:, cs] = _dot(pl_ref[:, cs], wp_ref[gi])
        _accum(dsp_ref, i == 0, _colsum(dpm * mr_s[...]))
        dmr = (dpm * sp_ref[...]).astype(BF)
        dmr_ref[...] = dmr
        for gi, w in enumerate(POOL_WINDOWS):
            cs = slice(gi * POOL_GD, (gi + 1) * POOL_GD)
            q_ref[:, cs] = (_dot_nt(dmr[:, cs], wp_ref[gi]) * _inv_count(i, tm, w)).astype(BF)

    row = _cols(tm, D, 0)
    return pl.pallas_call(
        body, name=name, grid=(s_len // tm,),
        in_specs=[row, _rowsharded(BLK), row, _whole((4, POOL_GD, POOL_GD)), _whole((1, D))],
        out_specs=[row, row, _whole((1, D))],
        out_shape=[_sds((s_len, D), BF), _sds((s_len, D), BF), _sds((1, D), F32)],
        scratch_shapes=[pltpu.VMEM((tm, D), F32)], compiler_params=_params(1))(dbr, wo, pooled, wpool, spool)


def seq_bwd(dc, q, proj, wdw, win, dhn_in, h, dh1, gpre, name, comm=None):
    s_len = dc.shape[0]
    tm = min(TMB, s_len)
    nt = s_len // tm

    def body(dc_ref, dch_ref, q_ref, qh_ref, a_ref, g_ref, wdw_ref, w2, w3, w4, dhn_in_ref, h_ref, dh1_ref, gpre_ref,
             dz_ref, dh_ref, dgpre_ref, ext_s, dhc_s, qext_s):
        i = pl.program_id(0)
        last = i == nt - 1
        ext_s[0:tm, :] = dc_ref[...].astype(F32)
        ext_s[tm:, :] = jnp.where(last, 0.0, dch_ref[...].astype(F32))

        def chunk(ci, carry):
            r0 = pl.multiple_of((ci // (D // LC)) * RC, RC)
            l0 = pl.multiple_of((ci % (D // LC)) * LC, LC)
            win_ = ext_s[pl.ds(r0, RC + HALO), pl.ds(l0, LC)]
            acc = jnp.zeros((RC, LC), F32)
            for r in range(8):
                wr = win_ if r == 0 else pltpu.roll(win_, RC + HALO - r, 0)
                for m in range(4):
                    d = 8 * m + r
                    if d < CONV_W:
                        k = CONV_W - 1 - d
                        acc = acc + wdw_ref[k:k + 1, pl.ds(l0, LC)] * wr[8 * m:8 * m + RC]
            dhc_s[pl.ds(r0, RC), pl.ds(l0, LC)] = acc
            return carry
        lax.fori_loop(0, (tm // RC) * (D // LC), chunk, 0)

        dhc = dhc_s[...]
        av = a_ref[...].astype(F32)
        sg = _sig(g_ref[...].astype(F32))
        da = (dhc * sg).astype(BF)
        dg = (dhc * av * sg * (1.0 - sg)).astype(BF)
        dz_ref[:, 0:D] = da
        dz_ref[:, D:2 * D] = dg

        qext_s[0:tm, :] = q_ref[...].astype(F32)
        qext_s[tm:, :] = jnp.where(last, 0.0, qh_ref[...].astype(F32))
        for gi, w in enumerate(POOL_WINDOWS):
            cs = slice(gi * POOL_GD, (gi + 1) * POOL_GD)
            e = qext_s[:, cs]
            s = e
            sh = 1
            while sh < w:
                s = s + pltpu.roll(s, tm + HALO - sh, 0)
                sh *= 2
            dz_ref[:, 2 * D + gi * POOL_GD:2 * D + (gi + 1) * POOL_GD] = (s[0:tm] - e[0:tm] * _count(i, tm, w)).astype(BF)
        dhn = dhn_in_ref[...] + _dot_nt(da, w2[...]) + _dot_nt(dg, w3[...]) + _dot_nt(dz_ref[:, 2 * D:3 * D], w4[...])
        x = h_ref[...]
        r = _rstd(x)
        _accum(dgpre_ref, i == 0, _colsum(dhn * x * r))
        dh_ref[...] = dh1_ref[...] + _rms_bwd(x, dhn * gpre_ref[...], r)

    row = _cols(tm, D, 0)
    return _pcall(
        body, (dc, dc, q, q, proj, proj, wdw, win, win, win, dhn_in, h, dh1, gpre), name=name, grid=(nt,),
        in_specs=[row, _next_halo(tm, D, 0, s_len), row, _next_halo(tm, D, 0, s_len), _cols(tm, D, 2), _cols(tm, D, 3),
                  _whole((HALO, D)), _win_block(2), _win_block(3), _win_block(4), row, row, row, _whole((1, D))],
        out_specs=[_cols(tm, 3 * D, 0), row, _whole((1, D))],
        out_shape=[_sds((s_len, 3 * D), BF), _sds((s_len, D), F32), _sds((1, D), F32)],
        scratch_shapes=[pltpu.VMEM((tm + HALO, D), F32), pltpu.VMEM((tm, D), F32), pltpu.VMEM((tm + HALO, D), F32)], comm=comm)


def wgrad(a, b, tk, tn, name, stacked=False, diag=False, a_spec=None, comm=None):
    s_len = b.shape[0]
    k_dim = a.shape[-1]
    n_dim = b.shape[1]
    ts = min(TS, s_len)
    nk = 1 if diag else k_dim // tk
    nn, ns = n_dim // tn, s_len // ts

    def body(a_ref, b_ref, o_ref, acc):
        s = pl.program_id(2)
        _accum(acc, s == 0, _dot_tn(a_ref[...].astype(BF), b_ref[...].astype(BF)))

        @pl.when(s == ns - 1)
        def _():
            o_ref[...] = acc[...].astype(BF).reshape(o_ref.shape)

    if a_spec is None:
        a_spec = pl.BlockSpec((ts, tk), (lambda k, n, s: (s, n)) if diag else (lambda k, n, s: (s, k)))
    if stacked or diag:
        out_shape = _sds((nn, tk if diag else k_dim, tn), BF)
        o_spec = pl.BlockSpec((1, tk, tn), lambda k, n, s: (n, k, 0))
    else:
        out_shape = _sds((k_dim, n_dim), BF)
        o_spec = pl.BlockSpec((tk, tn), lambda k, n, s: (k, n))
    (out,), got = _pcall(
        body, (a, b), name=name, grid=(nk, nn, ns),
        in_specs=[a_spec, pl.BlockSpec((ts, tn), lambda k, n, s: (s, n))], out_specs=[o_spec], out_shape=[out_shape],
        scratch_shapes=[pltpu.VMEM((tk, tn), F32)], comm=comm)
    return out if comm is None else (out, got)


_WEIGHTS = ['g_mix_pre', 'w_in', 'w_sgu_s', 'b_sgu_s', 'g_sgu_v', 'b_sgu_v', 'w_sgu_out', 'w_dw', 'b_dw', 'g_conv_ln', 'b_conv_ln',
            'w_conv_out', 'w_pool', 's_pool', 'w_pool_out', 'w_out', 'g_mix_post', 'g_ffn_pre', 'w_ffn_in', 'w_ffn_out', 'g_ffn_post',
            'w_ple', 'w_ple_gate']
_SHARDED = ['w_in', 'w_sgu_out', 'w_conv_out', 'w_pool', 'w_pool_out', 'w_out', 'w_ffn_in', 'w_ffn_out', 'w_ple', 'w_ple_gate']
_VECTORS = ['g_mix_pre', 'g_sgu_v', 'b_sgu_v', 'b_dw', 'g_conv_ln', 'b_conv_ln', 's_pool', 'g_mix_post', 'g_ffn_pre', 'g_ffn_post']
_SUBLANES = 8
_SGU_ROWS = HEADS * BLK * BLK // D
_REP_ROWS = _SUBLANES * (len(_VECTORS) + 2) + _SGU_ROWS


def _pack_replicated(t, layer):
    rows = [jnp.pad(t[k][layer].reshape(1, D), ((0, _SUBLANES - 1), (0, 0))) for k in _VECTORS + ['b_sgu_s']]
    return jnp.concatenate(rows + [t['w_sgu_s'][layer].reshape(_SGU_ROWS, D), jnp.zeros((_SUBLANES, D), F32)], axis=0)


def _unpack_replicated(packed):
    out = {}
    for i, k in enumerate(_VECTORS):
        out[k] = packed[:, _SUBLANES * i, :]
    o = _SUBLANES * len(_VECTORS)
    out['b_sgu_s'] = packed[:, o, :].reshape(2, HEADS, BLK)
    out['w_sgu_s'] = packed[:, o + _SUBLANES:o + _SUBLANES + _SGU_ROWS, :].reshape(2, HEADS, BLK, BLK)
    return out


def _pad_taps(w):
    return jnp.pad(w, ((0, HALO - CONV_W), (0, 0)))


def kernel(x, p, g_mix_pre, w_in, w_sgu_s, b_sgu_s, g_sgu_v, b_sgu_v, w_sgu_out, w_dw, b_dw, g_conv_ln, b_conv_ln, w_conv_out, w_pool, s_pool, w_pool_out, w_out, g_mix_post, g_ffn_pre, w_ffn_in, w_ffn_out, g_ffn_post, w_ple, w_ple_gate, loss_target, m_g_mix_pre, m_w_in, m_w_sgu_s, m_b_sgu_s, m_g_sgu_v, m_b_sgu_v, m_w_sgu_out, m_w_dw, m_b_dw, m_g_conv_ln, m_b_conv_ln, m_w_conv_out, m_w_pool, m_s_pool, m_w_pool_out, m_w_out, m_g_mix_post, m_g_ffn_pre, m_w_ffn_in, m_w_ffn_out, m_g_ffn_post, m_w_ple, m_w_ple_gate, v_g_mix_pre, v_w_in, v_w_sgu_s, v_b_sgu_s, v_g_sgu_v, v_b_sgu_v, v_w_sgu_out, v_w_dw, v_b_dw, v_g_conv_ln, v_b_conv_ln, v_w_conv_out, v_w_pool, v_s_pool, v_w_pool_out, v_w_out, v_g_mix_post, v_g_ffn_pre, v_w_ffn_in, v_w_ffn_out, v_g_ffn_post, v_w_ple, v_w_ple_gate):
    W = dict(g_mix_pre=g_mix_pre, w_in=w_in, w_sgu_s=w_sgu_s, b_sgu_s=b_sgu_s, g_sgu_v=g_sgu_v, b_sgu_v=b_sgu_v, w_sgu_out=w_sgu_out,
             w_dw=w_dw, b_dw=b_dw, g_conv_ln=g_conv_ln, b_conv_ln=b_conv_ln, w_conv_out=w_conv_out, w_pool=w_pool, s_pool=s_pool,
             w_pool_out=w_pool_out, w_out=w_out, g_mix_post=g_mix_post, g_ffn_pre=g_ffn_pre, w_ffn_in=w_ffn_in, w_ffn_out=w_ffn_out,
             g_ffn_post=g_ffn_post, w_ple=w_ple, w_ple_gate=w_ple_gate)
    M = dict(g_mix_pre=m_g_mix_pre, w_in=m_w_in, w_sgu_s=m_w_sgu_s, b_sgu_s=m_b_sgu_s, g_sgu_v=m_g_sgu_v, b_sgu_v=m_b_sgu_v,
             w_sgu_out=m_w_sgu_out, w_dw=m_w_dw, b_dw=m_b_dw, g_conv_ln=m_g_conv_ln, b_conv_ln=m_b_conv_ln, w_conv_out=m_w_conv_out,
             w_pool=m_w_pool, s_pool=m_s_pool, w_pool_out=m_w_pool_out, w_out=m_w_out, g_mix_post=m_g_mix_post, g_ffn_pre=m_g_ffn_pre,
             w_ffn_in=m_w_ffn_in, w_ffn_out=m_w_ffn_out, g_ffn_post=m_g_ffn_post, w_ple=m_w_ple, w_ple_gate=m_w_ple_gate)
    V = dict(g_mix_pre=v_g_mix_pre, w_in=v_w_in, w_sgu_s=v_w_sgu_s, b_sgu_s=v_b_sgu_s, g_sgu_v=v_g_sgu_v, b_sgu_v=v_b_sgu_v,
             w_sgu_out=v_w_sgu_out, w_dw=v_w_dw, b_dw=v_b_dw, g_conv_ln=v_g_conv_ln, b_conv_ln=v_b_conv_ln, w_conv_out=v_w_conv_out,
             w_pool=v_w_pool, s_pool=v_s_pool, w_pool_out=v_w_pool_out, w_out=v_w_out, g_mix_post=v_g_mix_post, g_ffn_pre=v_g_ffn_pre,
             w_ffn_in=v_w_ffn_in, w_ffn_out=v_w_ffn_out, g_ffn_post=v_g_ffn_post, w_ple=v_w_ple, w_ple_gate=v_w_ple_gate)

    my_c = lax.axis_index("c")
    core_id = my_c.astype(jnp.int32).reshape(1)
    my_chip = 2 * lax.axis_index("x") + lax.axis_index("y")
    my_dev = 2 * my_chip + my_c
    s_len = x.shape[1]
    h0 = x.reshape(s_len, D)
    target = loss_target.reshape(s_len, D)

    shard = [{k: W[k][l].astype(BF) for k in _SHARDED} for l in range(2)]
    for l in range(2):
        shard[l]['w_dw'] = w_dw[l]
        shard[l]['w_ffn_in'] = jnp.swapaxes(w_ffn_in[l], 0, 1).astype(BF)
    mixer_w = ['w_sgu_out', 'w_conv_out', 'w_pool', 'w_pool_out', 'w_out', 'w_dw']
    ffn_w = ['w_ffn_in', 'w_ffn_out', 'w_ple', 'w_ple_gate']
    hosted_gather = {
        'norm_proj_in': (0, mixer_w),
        'conv_fwd': (0, ffn_w),
        'merge_out': (1, mixer_w),
        'norm_proj_ffn': (1, ['w_in']),
        'ffn_out': (1, ffn_w),
    }
    G = [{'w_in': run_comm(Gather([shard[0]['w_in']]), "gather_w_in_0")[0]}, {}]

    def gather_in(layer, call):
        if layer != 0:
            return None, (lambda got: None)
        to_layer, keys = hosted_gather[call]
        return Gather([shard[to_layer][k] for k in keys]), (lambda got: G[to_layer].update(zip(keys, got)))

    def natural_mixer(g):
        wpool = jnp.transpose(g['w_pool'], (1, 0, 2, 3)).reshape(4, POOL_GD, POOL_GD)
        wdw = jnp.transpose(g['w_dw'].reshape(N_DEV, CONV_W, BLK), (1, 0, 2)).reshape(CONV_W, D)
        return dict(wpool=wpool, wdw=_pad_taps(wdw))

    def natural_ffn(g):
        wfit = g['w_ffn_in'].reshape(4, D_FF // 2, D)
        wple = jnp.transpose(g['w_ple'], (1, 0, 2)).reshape(PLE, D)
        return dict(wfit=wfit, wple=wple)

    def vec(name, layer):
        return W[name][layer].reshape(1, D)

    saved = []
    h = h0
    for l in range(2):
        g = G[l]
        comm, land = gather_in(l, 'norm_proj_in')
        (proj, hn), got = norm_proj(h, vec('g_mix_pre', l), g['w_in'], f"norm_proj_in_{l}", comm=comm)
        land(got)
        nat = natural_mixer(g)
        bsfull = jnp.repeat(b_sgu_s[l].T, BLK, axis=1)
        wst = jnp.swapaxes(w_sgu_s[l], 1, 2)
        sgu, bra = sgu_fwd(proj, w_sgu_s[l], bsfull, vec('g_sgu_v', l), vec('b_sgu_v', l), g['w_sgu_out'], f"sgu_fwd_{l}")
        comm, land = gather_in(l, 'conv_fwd')
        (conv, cb, brb), got = conv_fwd(proj, nat['wdw'], vec('b_dw', l), vec('g_conv_ln', l), vec('b_conv_ln', l), g['w_conv_out'],
                                        f"conv_fwd_{l}", comm=comm)
        land(got)
        nat.update(natural_ffn(g))
        pooled, pm, brc = pool_fwd(proj, nat['wpool'], vec('s_pool', l), g['w_pool_out'], f"pool_fwd_{l}")
        comm, land = gather_in(l, 'merge_out')
        (merged, mo, h1), got = merge_out(proj, bra, brb, brc, h, g['w_out'], vec('g_mix_post', l), f"merge_out_{l}", comm=comm)
        land(got)
        comm, land = gather_in(l, 'norm_proj_ffn')
        (ff, hn2), got = norm_proj(h1, vec('g_ffn_pre', l), nat['wfit'], f"norm_proj_ffn_{l}", transposed=True, comm=comm)
        land(got)
        comm, land = gather_in(l, 'ffn_out')
        (act, f, h2, pg, h3), got = ffn_out(ff, h1, p, g['w_ffn_out'], vec('g_ffn_post', l), g['w_ple_gate'], nat['wple'], l,
                                            f"ffn_out_{l}", comm=comm)
        land(got)
        saved.append(dict(h=h, nat=nat, bsfull=bsfull, wst=wst, proj=proj, hn=hn, sgu=sgu, bra=bra, conv=conv, cb=cb, brb=brb,
                          pooled=pooled, pm=pm, brc=brc, merged=merged, mo=mo, h1=h1, ff=ff, hn2=hn2, act=act, f=f, h2=h2, pg=pg))
        h = h3

    dh = h

    parts = {k: [None, None] for k in _SHARDED}
    small = {k: [None, None] for k in _VECTORS + ['b_sgu_s', 'w_sgu_s', 'w_dw']}
    chip_parts = [{}, {}]
    from_chips = [{}, {}]
    gathered_small = [None, None]

    def to_sibling(layer, keys):
        return ToSibling([parts[k][layer] for k in keys])

    def add_siblings(layer, keys, from_sibling):
        for k, rv in zip(keys, from_sibling):
            st = parts[k][layer]
            cols = st.shape[-1]
            chip_parts[layer][k] = add_core_side(st.reshape(N_DEV, -1, cols), rv.reshape(4, -1, cols), core_id,
                                                 f"rs_add_{k}_{layer}").reshape(rv.shape)

    def to_chips(layer, keys):
        return ToChips([chip_parts[layer][k] for k in keys])

    def small_pack(layer):
        return jnp.concatenate([_pack_replicated(small, layer), small['w_dw'][layer]], axis=0)

    ffn_group = ['w_ffn_in', 'w_ffn_out', 'w_ple_gate', 'w_ple']
    mix_group = ['w_out', 'w_sgu_out', 'w_conv_out', 'w_pool_out', 'w_pool']
    big = ['w_in', 'w_ffn_in']
    others = [k for k in _SHARDED if k not in big]
    hosted_rs = {
        'ple_ffn_bwd': (lambda: to_sibling(1, _SHARDED), lambda got: add_siblings(1, _SHARDED, got)),
        'mix_post_bwd': (lambda: to_sibling(0, ffn_group), lambda got: add_siblings(0, ffn_group, got)),
        'ffn_in_bwd': (lambda: to_chips(1, others), lambda got: from_chips[1].update(zip(others, got))),
        'sgu_bwd': (lambda: to_chips(1, big), lambda got: from_chips[1].update(zip(big, got))),
        'conv_bwd': (lambda: to_chips(0, ffn_group), lambda got: from_chips[0].update(zip(ffn_group, got))),
        'wgrad_in_sgu': (lambda: to_sibling(0, mix_group), lambda got: add_siblings(0, mix_group, got)),
        'wgrad_in_seq': (lambda: to_chips(0, mix_group), lambda got: from_chips[0].update(zip(mix_group, got))),
        'wgrad_in_gate': (lambda: Gather([small_pack(1)]), lambda got: gathered_small.__setitem__(1, got[0])),
    }

    def exchange_in(layer, call):
        if layer != 0 or call not in hosted_rs:
            return None, (lambda got: None)
        make, land = hosted_rs[call]
        return make(), land

    for l in (1, 0):
        sv, g, nat = saved[l], G[l], saved[l]['nat']

        def wg(call, a, b, tk, tn, **kw):
            comm, land = exchange_in(l, call)
            if comm is None:
                return wgrad(a, b, tk, tn, f"{call}_{l}", **kw)
            out, got = wgrad(a, b, tk, tn, f"{call}_{l}", comm=comm, **kw)
            land(got)
            return out

        comm, land = exchange_in(l, 'ple_ffn_bwd')
        res, got = ple_ffn_bwd(
            dh, sv['pg'], p, sv['f'], sv['ff'], nat['wple'], g['w_ple_gate'], g['w_ffn_out'], vec('g_ffn_post', l), l,
            f"ple_ffn_bwd_{l}", target=target if l == 1 else None, comm=comm)
        land(got)
        dh2, dpe, dpg, df, dff, small['g_ffn_post'][l] = res[:6]
        if l == 1:
            loss = lax.psum(res[6][0, 0], ("x", "y", "c"))
        comm, land = exchange_in(l, 'ffn_in_bwd')
        (dh1, small['g_ffn_pre'][l]), got = ffn_in_bwd(dff, nat['wfit'], sv['h1'], dh2, vec('g_ffn_pre', l), f"ffn_in_bwd_{l}",
                                                       comm=comm)
        land(got)
        p_spec = pl.BlockSpec((None, None, min(TS, s_len), PLE), functools.partial(lambda k, n, s, ll: (ll, 0, s, 0), ll=l))
        parts['w_ple'][l] = jnp.transpose(wg('wgrad_ple', p, dpe, PLE, D, a_spec=p_spec).reshape(PLE, N_DEV, BLK), (1, 0, 2))
        parts['w_ple_gate'][l] = wg('wgrad_ple_gate', sv['h2'], dpg, D, D).reshape(N_DEV, BLK, D)
        parts['w_ffn_out'][l] = wg('wgrad_ffn_out', sv['act'], df, D_FF // 2, D).reshape(N_DEV, D_FF // N_DEV, D)
        parts['w_ffn_in'][l] = wg('wgrad_ffn_in', dff, sv['hn2'], D_FF // 2, D).reshape(N_DEV, D_FF // 4, D)

        comm, land = exchange_in(l, 'mix_post_bwd')
        (dmo, dbra, dbrb, dbrc, dzg, dhn_g, small['g_mix_post'][l]), got = mix_post_bwd(
            dh1, sv['mo'], vec('g_mix_post', l), g['w_out'], sv['proj'], sv['bra'], sv['brb'], sv['brc'], g['w_in'],
            f"mix_post_bwd_{l}", comm=comm)
        land(got)
        comm, land = exchange_in(l, 'sgu_bwd')
        (dzs, dhn_ag, dws, dbs, small['g_sgu_v'][l], small['b_sgu_v'][l]), got = sgu_bwd(
            dbra, g['w_sgu_out'], sv['proj'], w_sgu_s[l], sv['wst'], sv['bsfull'], vec('g_sgu_v', l), vec('b_sgu_v', l), g['w_in'],
            dhn_g, f"sgu_bwd_{l}", comm=comm)
        land(got)
        small['w_sgu_s'][l] = dws
        small['b_sgu_s'][l] = dbs.T
        comm, land = exchange_in(l, 'conv_bwd')
        (dc, dwdw, small['b_dw'][l], small['g_conv_ln'][l], small['b_conv_ln'][l]), got = conv_bwd(
            dbrb, g['w_conv_out'], sv['proj'], sv['conv'], vec('g_conv_ln', l), vec('b_conv_ln', l), f"conv_bwd_{l}", comm=comm)
        land(got)
        small['w_dw'][l] = dwdw
        dmr, q, small['s_pool'][l] = pool_bwd(dbrc, g['w_pool_out'], sv['pooled'], nat['wpool'], vec('s_pool', l), f"pool_bwd_{l}")
        comm, land = exchange_in(l, 'seq_bwd')
        (dzc, dh, small['g_mix_pre'][l]), got = seq_bwd(dc, q, sv['proj'], nat['wdw'], g['w_in'], dhn_ag, sv['h'], dh1,
                                                        vec('g_mix_pre', l), f"seq_bwd_{l}", comm=comm)
        land(got)

        parts['w_out'][l] = wg('wgrad_out', sv['merged'], dmo, D, D).reshape(N_DEV, BLK, D)
        parts['w_sgu_out'][l] = wg('wgrad_sgu_out', sv['sgu'], dbra, D, D).reshape(N_DEV, BLK, D)
        parts['w_conv_out'][l] = wg('wgrad_conv_out', sv['cb'], dbrb, D, D).reshape(N_DEV, BLK, D)
        parts['w_pool_out'][l] = wg('wgrad_pool_out', sv['pm'], dbrc, D, D).reshape(N_DEV, BLK, D)
        g_pool = wg('wgrad_pool', sv['pooled'], dmr, POOL_GD, POOL_GD, diag=True)
        parts['w_pool'][l] = jnp.transpose(g_pool.reshape(4, N_DEV, POOL_GD // N_DEV, POOL_GD), (1, 0, 2, 3))
        parts['w_in'][l] = jnp.concatenate([
            wg('wgrad_in_sgu', sv['hn'], dzs, D, D, stacked=True),
            wg('wgrad_in_seq', sv['hn'], dzc, D, D, stacked=True),
            wg('wgrad_in_gate', sv['hn'], dzg, D, D, stacked=True)], axis=0)
    grad_x = dh.reshape(1, s_len, D)

    add_siblings(0, ['w_in'], run_comm(to_sibling(0, ['w_in']), "rs_to_sibling_w_in_0"))
    from_chips[0]['w_in'] = run_comm(to_chips(0, ['w_in']), "rs_to_chips_w_in_0")[0]
    gathered_small[0] = run_comm(Gather([small_pack(0)]), "all_gather_small_grads_0")[0]

    outs = {}
    for k in _SHARDED:
        wmv = [jnp.swapaxes(t[k], 1, 2) if k == 'w_ffn_in' else t[k] for t in (W, M, V)]
        cols = wmv[0].shape[-1]
        pieces = []
        for layer in range(2):
            own = lax.dynamic_index_in_dim(chip_parts[layer][k], my_chip, axis=0, keepdims=False).reshape(-1, cols)
            rv3 = from_chips[layer][k].reshape(3, -1, cols)
            pieces.append([(own, None), (rv3, 0), (rv3, 1), (rv3, 2)])
        res = adamw_layers(*[t.reshape(2, -1, cols) for t in wmv], pieces, f"adamw_{k}")
        res = [r.reshape(wmv[0].shape) for r in res]
        outs[k] = [jnp.swapaxes(r, 1, 2) for r in res] if k == 'w_ffn_in' else res

    packed = [jnp.stack([_pack_replicated(t, 0), _pack_replicated(t, 1)], axis=0) for t in (W, M, V)]
    rep_res = adamw_layers(*packed, [[(gathered_small[layer], d) for d in range(N_DEV)] for layer in range(2)], "adamw_replicated")
    for idx, res in enumerate(rep_res):
        for name, val in _unpack_replicated(res).items():
            outs.setdefault(name, [None] * 4)[idx] = val
    dw_sum = jnp.stack([sum_slabs(gathered_small[layer][:, _REP_ROWS:, :], f"sum_w_dw_{layer}") for layer in range(2)], axis=0)
    dw_mine = lax.dynamic_slice_in_dim(dw_sum[:, :CONV_W], my_dev * BLK, BLK, axis=2)
    res = adamw(w_dw.reshape(2 * CONV_W, BLK), m_w_dw.reshape(2 * CONV_W, BLK), v_w_dw.reshape(2 * CONV_W, BLK),
                [(dw_mine.reshape(2 * CONV_W, BLK), None)], "adamw_w_dw")
    outs['w_dw'] = [r.reshape(w_dw.shape) for r in res]

    result = [loss, grad_x]
    for idx in range(4):
        result += [outs[k][idx] for k in _WEIGHTS]
    return tuple(result)
```

```python
import functools
import math

import jax
import jax.numpy as jnp
from jax import lax
from jax.experimental import pallas as pl
from jax.experimental.pallas import tpu as pltpu

F32 = jnp.float32
BF = jnp.bfloat16

D = 1024
D_FF = 2816
PLE = 256
N_DEV = 8
HEADS = 8
BLK = 128
CHUNK = 64
CONV_W = 31
POOL_WINDOWS = (2, 4, 8, 16)
POOL_GD = 256
EPS = 1e-6

V7X_VMEM_BYTES = 64 * 2**20
VMEM_LIMIT = V7X_VMEM_BYTES * 7 // 8
HALO = 32
RC = 64
LC = 128
TM = 512
TMB = 256
TMP = 1024
TS = 2048

ADAM_LR, ADAM_B1, ADAM_B2, ADAM_EPS, ADAM_WD, ADAM_STEP = 0.001, 0.9, 0.999, 1e-08, 0.01, 10

MESH = pl.DeviceIdType.MESH
ANY = pl.BlockSpec(memory_space=pl.ANY)

_GELU_K0 = math.sqrt(2.0 / math.pi)
_GELU_K1 = 0.044715
_LOG2E = 1.4426950408889634


def _dot(a, b):
    return jnp.dot(a, b, preferred_element_type=F32)


def _dot_nt(a, b):
    return lax.dot_general(a, b, (((1,), (1,)), ((), ())), preferred_element_type=F32)


def _dot_tn(a, b):
    return lax.dot_general(a, b, (((0,), (0,)), ((), ())), preferred_element_type=F32)


def _sig(x):
    return 1.0 / (1.0 + jnp.exp2(x * (-_LOG2E)))


def _gelu(x):
    s = 1.0 / (1.0 + jnp.exp2(x * ((-2.0 * _GELU_K0 * _LOG2E) + (-2.0 * _GELU_K0 * _GELU_K1 * _LOG2E) * (x * x))))
    return x * s, s


def _gelu_grad(x, s):
    return s + x * s * (1.0 - s) * ((2.0 * _GELU_K0) + (6.0 * _GELU_K0 * _GELU_K1) * (x * x))


def _rstd(x):
    return lax.rsqrt(jnp.mean(x * x, axis=-1, keepdims=True) + EPS)


def _rms_bwd(x, gd, r):
    return r * gd - x * (r * r * r) * jnp.mean(gd * x, axis=-1, keepdims=True)


def _ln_fwd(x):
    mu = jnp.mean(x, axis=-1, keepdims=True)
    xc = x - mu
    rs = lax.rsqrt(jnp.mean(xc * xc, axis=-1, keepdims=True) + EPS)
    return xc * rs, rs


def _ln_bwd(dhat, hat, rs):
    return rs * (dhat - jnp.mean(dhat, axis=-1, keepdims=True) - hat * jnp.mean(dhat * hat, axis=-1, keepdims=True))


def _colsum(x):
    return jnp.sum(x, axis=0, keepdims=True)


def _accum(ref, first, val):
    @pl.when(first)
    def _():
        ref[...] = val

    @pl.when(jnp.logical_not(first))
    def _():
        ref[...] += val


def _sgu_mask(transposed):
    r = lax.broadcasted_iota(jnp.int32, (BLK, BLK), 0) // CHUNK
    c = lax.broadcasted_iota(jnp.int32, (BLK, BLK), 1) // CHUNK
    return (r <= c) if transposed else (c <= r)


def _inv_count(i, tm, w):
    t = lax.broadcasted_iota(jnp.int32, (tm, 1), 0) + i * tm
    return 1.0 / jnp.minimum(t + 1, w).astype(F32)


def _count(i, tm, w):
    t = lax.broadcasted_iota(jnp.int32, (tm, 1), 0) + i * tm
    return jnp.minimum(t + 1, w).astype(F32)


def _params(n_grid):
    return pltpu.CompilerParams(dimension_semantics=("arbitrary",) * n_grid, vmem_limit_bytes=VMEM_LIMIT)


def _sds(shape, dtype):
    return jax.ShapeDtypeStruct(shape, dtype)


def _cols(tm, width, cb):
    return pl.BlockSpec((tm, width), lambda i: (i, cb))


def _whole(shape):
    nd = len(shape)
    return pl.BlockSpec(shape, lambda i: (0,) * nd)


def _prev_halo(tm, width, cb):
    return pl.BlockSpec((HALO, width), lambda i: (jnp.maximum(i * (tm // HALO) - 1, 0), cb))


def _next_halo(tm, width, cb, s_len):
    last = s_len // HALO - 1
    return pl.BlockSpec((HALO, width), lambda i: (jnp.minimum((i + 1) * (tm // HALO), last), cb))


def _rowsharded(rows):
    return pl.BlockSpec((N_DEV, rows, D), lambda i: (0, 0, 0))


def _win_block(j):
    return pl.BlockSpec((None, D, D), lambda i: (j, 0, 0))


def _row_tile(rows, cap):
    t = min(rows, cap)
    while rows % t or t % 16:
        t -= 16
    return t


class Gather:
    def __init__(self, arrs):
        self.arrs = list(arrs)
        n = self.n = len(self.arrs)
        self.out_shape = [_sds((N_DEV,) + a.shape, a.dtype) for a in self.arrs]
        self.scratch = [pltpu.SemaphoreType.DMA((n, 7)), pltpu.SemaphoreType.DMA((n, 7)), pltpu.SemaphoreType.DMA((n,))]

    def _plan(self, ins, outs, sems):
        send, recv, local = sems
        x, y, c = lax.axis_index("x"), lax.axis_index("y"), lax.axis_index("c")
        me, sibling = (x, y, c), (x, y, 1 - c)
        chips = [(1 - x, y), (x, 1 - y), (1 - x, 1 - y)]

        def copy(a, k, block, to, src=None):
            dst = outs[a].at[4 * block[0] + 2 * block[1] + block[2]]
            return pltpu.make_async_remote_copy(
                src_ref=dst if src is None else src, dst_ref=dst, send_sem=send.at[a, k], recv_sem=recv.at[a, k],
                device_id=to, device_id_type=MESH)

        mine = [pltpu.make_async_copy(ins[a], outs[a].at[4 * x + 2 * y + c], local.at[a]) for a in range(self.n)]
        first = []
        for a in range(self.n):
            first.append(copy(a, 0, me, sibling, src=ins[a]))
            first += [copy(a, 1 + j, me, (*chip, c), src=ins[a]) for j, chip in enumerate(chips)]
        return me, sibling, chips, c, copy, mine, first

    def start(self, ins, outs, sems):
        *_, mine, first = self._plan(ins, outs, sems)
        for cp in mine + first:
            cp.start()

    def finish(self, ins, outs, sems):
        me, sibling, chips, c, copy, mine, first = self._plan(ins, outs, sems)
        passed = []
        for a in range(self.n):
            for j, chip in enumerate(chips):
                copy(a, 1 + j, (*chip, c), me).wait_recv()
                fwd = copy(a, 4 + j, (*chip, c), sibling)
                fwd.start()
                passed.append(fwd)
        for a in range(self.n):
            copy(a, 0, sibling, me).wait_recv()
            for j, chip in enumerate(chips):
                copy(a, 4 + j, (*chip, 1 - c), me).wait_recv()
        for cp in first + passed:
            cp.wait_send()
        for cp in mine:
            cp.wait()


class ToSibling:
    def __init__(self, parts):
        self.arrs = list(parts)
        n = self.n = len(self.arrs)
        self.out_shape = [_sds((4,) + p.shape[1:], p.dtype) for p in self.arrs]
        self.scratch = [pltpu.SemaphoreType.DMA((n,)), pltpu.SemaphoreType.DMA((n,))]

    def start(self, ins, outs, sems):
        send, recv = sems
        x, y, c = lax.axis_index("x"), lax.axis_index("y"), lax.axis_index("c")
        for a in range(self.n):
            for q in range(4):
                pltpu.make_async_remote_copy(
                    src_ref=ins[a].at[2 * q + 1 - c], dst_ref=outs[a].at[q], send_sem=send.at[a], recv_sem=recv.at[a],
                    device_id=(x, y, 1 - c), device_id_type=MESH).start()

    def finish(self, ins, outs, sems):
        send, recv = sems
        x, y, c = lax.axis_index("x"), lax.axis_index("y"), lax.axis_index("c")
        for a in range(self.n):
            pltpu.make_async_remote_copy(
                src_ref=outs[a], dst_ref=outs[a], send_sem=send.at[a], recv_sem=recv.at[a],
                device_id=(x, y, 1 - c), device_id_type=MESH).wait()


class ToChips:
    def __init__(self, cps):
        self.arrs = list(cps)
        n = self.n = len(self.arrs)
        self.out_shape = [_sds((3,) + p.shape[1:], p.dtype) for p in self.arrs]
        self.scratch = [pltpu.SemaphoreType.DMA((n,)), pltpu.SemaphoreType.DMA((n,))]

    def start(self, ins, outs, sems):
        send, recv = sems
        x, y, c = lax.axis_index("x"), lax.axis_index("y"), lax.axis_index("c")
        for a in range(self.n):
            for r, (px, py) in enumerate([(1 - x, y), (x, 1 - y), (1 - x, 1 - y)]):
                pltpu.make_async_remote_copy(
                    src_ref=ins[a].at[2 * px + py], dst_ref=outs[a].at[r], send_sem=send.at[a], recv_sem=recv.at[a],
                    device_id=(px, py, c), device_id_type=MESH).start()

    def finish(self, ins, outs, sems):
        send, recv = sems
        x, y, c = lax.axis_index("x"), lax.axis_index("y"), lax.axis_index("c")
        for a in range(self.n):
            pltpu.make_async_remote_copy(
                src_ref=outs[a], dst_ref=outs[a], send_sem=send.at[a], recv_sem=recv.at[a],
                device_id=(x, y, c), device_id_type=MESH).wait()


def run_comm(comm, name):
    n = comm.n

    def body(*refs):
        ins, outs, sems = refs[:n], refs[n:2 * n], refs[2 * n:]
        comm.start(ins, outs, sems)
        comm.finish(ins, outs, sems)

    return pl.pallas_call(body, name=name, out_shape=comm.out_shape, in_specs=[ANY] * n, out_specs=[ANY] * n,
                          scratch_shapes=comm.scratch)(*comm.arrs)


def _pcall(body, args, *, name, grid, in_specs, out_specs, out_shape, scratch_shapes=(), comm=None):
    params = _params(len(grid))
    scratch_shapes = list(scratch_shapes)
    if comm is None:
        outs = pl.pallas_call(body, name=name, grid=grid, in_specs=in_specs, out_specs=out_specs, out_shape=out_shape,
                              scratch_shapes=scratch_shapes, compiler_params=params)(*args)
        return outs, None
    n_in, n_out, n_scr, nc = len(in_specs), len(out_specs), len(scratch_shapes), comm.n

    def hosted(*refs):
        ins, cins = refs[:n_in], refs[n_in:n_in + nc]
        o0 = n_in + nc
        outs, couts = refs[o0:o0 + n_out], refs[o0 + n_out:o0 + n_out + nc]
        s0 = o0 + n_out + nc
        scr, csems = refs[s0:s0 + n_scr], refs[s0 + n_scr:]
        first = pl.program_id(0) == 0
        last = pl.program_id(0) == grid[0] - 1
        for ax in range(1, len(grid)):
            first = jnp.logical_and(first, pl.program_id(ax) == 0)
            last = jnp.logical_and(last, pl.program_id(ax) == grid[ax] - 1)

        @pl.when(first)
        def _():
            comm.start(cins, couts, csems)
        body(*ins, *outs, *scr)

        @pl.when(last)
        def _():
            comm.finish(cins, couts, csems)

    res = pl.pallas_call(
        hosted, name=name, grid=grid, in_specs=list(in_specs) + [ANY] * nc, out_specs=list(out_specs) + [ANY] * nc,
        out_shape=list(out_shape) + comm.out_shape, scratch_shapes=scratch_shapes + comm.scratch,
        compiler_params=params)(*args, *comm.arrs)
    return res[:n_out], res[n_out:]


def add_core_side(parts, from_sibling, core, name):
    _, rows, cols = parts.shape
    tr = _row_tile(rows, 512)

    def body(c_ref, a_ref, b_ref, o_ref):
        o_ref[...] = (a_ref[...].astype(F32) + b_ref[...].astype(F32)).astype(o_ref.dtype)

    side = pl.BlockSpec((None, tr, cols), lambda q, i, c: (q, i, 0))
    return pl.pallas_call(
        body, name=name,
        grid_spec=pltpu.PrefetchScalarGridSpec(
            num_scalar_prefetch=1, grid=(4, rows // tr),
            in_specs=[pl.BlockSpec((None, tr, cols), lambda q, i, c: (2 * q + c[0], i, 0)), side], out_specs=side),
        out_shape=_sds(from_sibling.shape, BF), compiler_params=_params(2))(core, parts, from_sibling)


def adamw(w, m, v, pieces, name):
    rows, cols = w.shape
    tr = _row_tile(rows, 256) if rows % 16 == 0 else rows
    np_ = len(pieces)
    c1 = 1.0 / (1.0 - ADAM_B1 ** ADAM_STEP)
    c2 = 1.0 / (1.0 - ADAM_B2 ** ADAM_STEP)

    def body(*refs):
        w_ref, m_ref, v_ref = refs[:3]
        p_refs = refs[3:3 + np_]
        g_ref, d_ref, nm_ref, nv_ref = refs[3 + np_:]
        g = p_refs[0][...].astype(F32)
        for pr in p_refs[1:]:
            g = g + pr[...].astype(F32)
        nm = ADAM_B1 * m_ref[...] + (1.0 - ADAM_B1) * g
        nv = ADAM_B2 * v_ref[...] + (1.0 - ADAM_B2) * (g * g)
        g_ref[...] = g
        nm_ref[...] = nm
        nv_ref[...] = nv
        d_ref[...] = -ADAM_LR * ((nm * c1) / (jnp.sqrt(nv * c2) + ADAM_EPS) + ADAM_WD * w_ref[...])

    spec = pl.BlockSpec((tr, cols), lambda i: (i, 0))
    p_specs = []
    for arr, k in pieces:
        if k is None:
            p_specs.append(spec)
        else:
            p_specs.append(pl.BlockSpec((None, tr, cols), functools.partial(lambda i, kk: (kk, i, 0), kk=k)))
    out = _sds(w.shape, F32)
    return pl.pallas_call(body, name=name, grid=(rows // tr,), in_specs=[spec] * 3 + p_specs, out_specs=[spec] * 4,
                          out_shape=[out] * 4, compiler_params=_params(1))(w, m, v, *[a for a, _ in pieces])


def adamw_layers(w, m, v, pieces, name):
    _, rows, cols = w.shape
    tr = _row_tile(rows, 256)
    nt = rows // tr
    counts = [len(pieces[0]), len(pieces[1])]
    c1 = 1.0 / (1.0 - ADAM_B1 ** ADAM_STEP)
    c2 = 1.0 / (1.0 - ADAM_B2 ** ADAM_STEP)

    def body(*refs):
        w_ref, m_ref, v_ref = refs[:3]
        p_refs = refs[3:3 + sum(counts)]
        g_ref, d_ref, nm_ref, nv_ref = refs[3 + sum(counts):]
        sums = []
        for group in (p_refs[:counts[0]], p_refs[counts[0]:]):
            s = group[0][...].astype(F32)
            for pr in group[1:]:
                s = s + pr[...].astype(F32)
            sums.append(s)
        g = jnp.where(pl.program_id(0) == 0, sums[0], sums[1])
        nm = ADAM_B1 * m_ref[...] + (1.0 - ADAM_B1) * g
        nv = ADAM_B2 * v_ref[...] + (1.0 - ADAM_B2) * (g * g)
        g_ref[...] = g
        nm_ref[...] = nm
        nv_ref[...] = nv
        d_ref[...] = -ADAM_LR * ((nm * c1) / (jnp.sqrt(nv * c2) + ADAM_EPS) + ADAM_WD * w_ref[...])

    def rows_of(layer):
        parked = nt - 1 if layer == 0 else 0
        return lambda l, i: jnp.where(l == layer, i, parked)

    spec = pl.BlockSpec((None, tr, cols), lambda l, i: (l, i, 0))
    p_specs, p_args = [], []
    for layer in (0, 1):
        row_of = rows_of(layer)
        for arr, k in pieces[layer]:
            p_args.append(arr)
            if k is None:
                p_specs.append(pl.BlockSpec((tr, cols), functools.partial(lambda l, i, f: (f(l, i), 0), f=row_of)))
            else:
                p_specs.append(pl.BlockSpec((None, tr, cols), functools.partial(lambda l, i, f, kk: (kk, f(l, i), 0), f=row_of, kk=k)))
    out = _sds(w.shape, F32)
    return pl.pallas_call(body, name=name, grid=(2, nt), in_specs=[spec] * 3 + p_specs, out_specs=[spec] * 4,
                          out_shape=[out] * 4, compiler_params=_params(2))(w, m, v, *p_args)


def sum_slabs(g, name):
    n, rows, cols = g.shape

    def body(g_ref, o_ref):
        s = g_ref[0]
        for k in range(1, n):
            s = s + g_ref[k]
        o_ref[...] = s

    return pl.pallas_call(body, name=name, out_shape=_sds((rows, cols), F32))(g)


def norm_proj(h, g, w, name, transposed=False, comm=None):
    s_len = h.shape[0]
    nb, tn = (w.shape[0], w.shape[1]) if transposed else (w.shape[0], w.shape[2])
    tm = min(TMP, s_len)
    nt = s_len // tm
    matmul = _dot_nt if transposed else _dot

    def body(h_ref, g_ref, w_ref, o_ref, hn_ref, hn_s):
        rows = pl.ds(pl.multiple_of(pl.program_id(1) * tm, tm), tm)

        @pl.when(pl.program_id(0) == 0)
        def _():
            x = h_ref[...]
            hn = (x * _rstd(x) * g_ref[...]).astype(BF)
            hn_s[rows, :] = hn
            hn_ref[...] = hn
        o_ref[...] = matmul(hn_s[rows, :], w_ref[...]).astype(BF)

    def first_pass_rows(j, i):
        return (jnp.where(j == 0, i, nt - 1), 0)

    return _pcall(
        body, (h, g, w), name=name, grid=(nb, nt),
        in_specs=[pl.BlockSpec((tm, D), first_pass_rows), pl.BlockSpec((1, D), lambda j, i: (0, 0)),
                  pl.BlockSpec((None,) + w.shape[1:], lambda j, i: (j, 0, 0))],
        out_specs=[pl.BlockSpec((tm, tn), lambda j, i: (i, j)), pl.BlockSpec((tm, D), first_pass_rows)],
        out_shape=[_sds((s_len, nb * tn), BF), _sds((s_len, D), BF)],
        scratch_shapes=[pltpu.VMEM((s_len, D), BF)], comm=comm)


def _sgu_mix(ws_ref, vln_s, mix_s, bs_ref, tm, transposed):
    mask = _sgu_mask(transposed)
    for hd in range(HEADS):
        wm = jnp.where(mask, ws_ref[hd], 0.0).astype(BF)
        cs = slice(hd * BLK, (hd + 1) * BLK)
        for n in range(tm // BLK):
            rs = slice(n * BLK, (n + 1) * BLK)
            r = _dot(wm, vln_s[rs, cs])
            mix_s[rs, cs] = r if bs_ref is None else r + bs_ref[:, cs]


def sgu_fwd(proj, ws, bsfull, gv, bv, wo, name):
    s_len = proj.shape[0]
    tm = min(TM, s_len)

    def body(zu_ref, zv_ref, ws_ref, bs_ref, gv_ref, bv_ref, wo_ref, sgu_ref, br_ref, vln_s, mix_s):
        u, _ = _gelu(zu_ref[...].astype(F32))
        v, _ = _gelu(zv_ref[...].astype(F32))
        vhat, _ = _ln_fwd(v)
        vln_s[...] = (vhat * gv_ref[...] + bv_ref[...]).astype(BF)
        _sgu_mix(ws_ref, vln_s, mix_s, bs_ref, tm, False)
        sgu = (u * mix_s[...]).astype(BF)
        sgu_ref[...] = sgu
        br_ref[...] = _dot(sgu, wo_ref[...].reshape(D, D)).astype(BF)

    return pl.pallas_call(
        body, name=name, grid=(s_len // tm,),
        in_specs=[_cols(tm, D, 0), _cols(tm, D, 1), _whole((HEADS, BLK, BLK)), _whole((BLK, D)), _whole((1, D)), _whole((1, D)),
                  _rowsharded(BLK)],
        out_specs=[_cols(tm, D, 0)] * 2, out_shape=[_sds((s_len, D), BF)] * 2,
        scratch_shapes=[pltpu.VMEM((tm, D), BF), pltpu.VMEM((tm, D), F32)], compiler_params=_params(1),
    )(proj, proj, ws, bsfull, gv, bv, wo)


def _causal_conv(ext_s, out_s, wdw_ref, bias_ref, tm):
    def chunk(ci, carry):
        r0 = pl.multiple_of((ci // (D // LC)) * RC, RC)
        l0 = pl.multiple_of((ci % (D // LC)) * LC, LC)
        win = ext_s[pl.ds(r0, RC + HALO), pl.ds(l0, LC)]
        acc = jnp.broadcast_to(bias_ref[:, pl.ds(l0, LC)], (RC, LC))
        for r in range(8):
            wr = win if r == 0 else pltpu.roll(win, r, 0)
            for m in range(4):
                d = 8 * m + r
                if d < CONV_W:
                    k = CONV_W - 1 - d
                    acc = acc + wdw_ref[k:k + 1, pl.ds(l0, LC)] * wr[HALO - 8 * m:HALO - 8 * m + RC]
        out_s[pl.ds(r0, RC), pl.ds(l0, LC)] = acc
        return carry
    lax.fori_loop(0, (tm // RC) * (D // LC), chunk, 0)


def _glu_ext(a_ref, g_ref, ah_ref, gh_ref, ext_s, first):
    hh = ah_ref[...].astype(F32) * _sig(gh_ref[...].astype(F32))
    ext_s[0:HALO, :] = jnp.where(first, 0.0, hh)
    ext_s[HALO:, :] = a_ref[...].astype(F32) * _sig(g_ref[...].astype(F32))


def conv_fwd(proj, wdw, bdw, gln, bln, wo, name, comm=None):
    s_len = proj.shape[0]
    tm = min(TM, s_len)

    def body(a_ref, g_ref, ah_ref, gh_ref, wdw_ref, bdw_ref, gln_ref, bln_ref, wo_ref, cv_ref, cb_ref, br_ref, ext_s, conv_s):
        _glu_ext(a_ref, g_ref, ah_ref, gh_ref, ext_s, pl.program_id(0) == 0)
        _causal_conv(ext_s, conv_s, wdw_ref, bdw_ref, tm)
        cv = conv_s[...].astype(BF)
        cv_ref[...] = cv
        chat, _ = _ln_fwd(cv.astype(F32))
        yl = chat * gln_ref[...] + bln_ref[...]
        cb = (yl * _sig(yl)).astype(BF)
        cb_ref[...] = cb
        br_ref[...] = _dot(cb, wo_ref[...].reshape(D, D)).astype(BF)

    return _pcall(
        body, (proj, proj, proj, proj, wdw, bdw, gln, bln, wo), name=name, grid=(s_len // tm,),
        in_specs=[_cols(tm, D, 2), _cols(tm, D, 3), _prev_halo(tm, D, 2), _prev_halo(tm, D, 3), _whole((HALO, D)),
                  _whole((1, D)), _whole((1, D)), _whole((1, D)), _rowsharded(BLK)],
        out_specs=[_cols(tm, D, 0)] * 3, out_shape=[_sds((s_len, D), BF)] * 3,
        scratch_shapes=[pltpu.VMEM((tm + HALO, D), F32), pltpu.VMEM((tm, D), F32)], comm=comm)


def pool_fwd(proj, wpool, spool, wo, name):
    s_len = proj.shape[0]
    tm = min(TM, s_len)

    def body(z_ref, zh_ref, wp_ref, sp_ref, wo_ref, pooled_ref, pm_ref, br_ref, ext_s, mr_s):
        i = pl.program_id(0)
        ext_s[0:HALO, :] = jnp.where(i == 0, 0.0, zh_ref[...].astype(F32))
        ext_s[HALO:, :] = z_ref[...].astype(F32)
        for gi, w in enumerate(POOL_WINDOWS):
            cs = slice(gi * POOL_GD, (gi + 1) * POOL_GD)
            e = ext_s[:, cs]
            s = e
            sh = 1
            while sh < w:
                s = s + pltpu.roll(s, sh, 0)
                sh *= 2
            pooled = (s[HALO:] * _inv_count(i, tm, w) - e[HALO:]).astype(BF)
            pooled_ref[:, cs] = pooled
            mr_s[:, cs] = _dot(pooled, wp_ref[gi])
        pm = (mr_s[...] * sp_ref[...]).astype(BF)
        pm_ref[...] = pm
        br_ref[...] = _dot(pm, wo_ref[...].reshape(D, D)).astype(BF)

    return pl.pallas_call(
        body, name=name, grid=(s_len // tm,),
        in_specs=[_cols(tm, D, 4), _prev_halo(tm, D, 4), _whole((4, POOL_GD, POOL_GD)), _whole((1, D)), _rowsharded(BLK)],
        out_specs=[_cols(tm, D, 0)] * 3, out_shape=[_sds((s_len, D), BF)] * 3,
        scratch_shapes=[pltpu.VMEM((tm + HALO, D), F32), pltpu.VMEM((tm, D), F32)], compiler_params=_params(1),
    )(proj, proj, wpool, spool, wo)


def merge_out(proj, bra, brb, brc, h, wout, gpost, name, comm=None):
    s_len = h.shape[0]
    tm = min(TM, s_len)

    def body(z0, z1, z2, a_ref, b_ref, c_ref, h_ref, wo_ref, g_ref, mg_ref, mo_ref, h1_ref):
        merged = (_sig(z0[...].astype(F32)) * a_ref[...].astype(F32) + _sig(z1[...].astype(F32)) * b_ref[...].astype(F32)
                  + _sig(z2[...].astype(F32)) * c_ref[...].astype(F32)).astype(BF)
        mg_ref[...] = merged
        mo = _dot(merged, wo_ref[...].reshape(D, D))
        mo_ref[...] = mo.astype(BF)
        h1_ref[...] = h_ref[...] + mo * _rstd(mo) * g_ref[...]

    row = _cols(tm, D, 0)
    return _pcall(
        body, (proj, proj, proj, bra, brb, brc, h, wout, gpost), name=name, grid=(s_len // tm,),
        in_specs=[_cols(tm, D, 5), _cols(tm, D, 6), _cols(tm, D, 7), row, row, row, row, _rowsharded(BLK), _whole((1, D))],
        out_specs=[row] * 3, out_shape=[_sds((s_len, D), BF), _sds((s_len, D), BF), _sds((s_len, D), F32)], comm=comm)


def _p_spec(tm, layer):
    return pl.BlockSpec((None, None, tm, PLE), lambda i: (layer, 0, i, 0))


def ffn_out(ff, h1, p, wfo, gpost, wpg, wple, layer, name, comm=None):
    s_len = h1.shape[0]
    tm = min(TMB, s_len)

    def body(fg_ref, fu_ref, h1_ref, p_ref, wfo_ref, g_ref, wpg_ref, wple_ref, act_ref, f_ref, h2_ref, pg_ref, h3_ref):
        gt = fg_ref[...].astype(F32)
        act = (gt * _sig(gt) * fu_ref[...].astype(F32)).astype(BF)
        act_ref[...] = act
        f = _dot(act, wfo_ref[...].reshape(D_FF, D))
        f_ref[...] = f.astype(BF)
        h2 = h1_ref[...] + f * _rstd(f) * g_ref[...]
        h2_ref[...] = h2
        pg = _dot(h2.astype(BF), wpg_ref[...].reshape(D, D)).astype(BF)
        pg_ref[...] = pg
        pe = _dot(p_ref[...].astype(BF), wple_ref[...])
        h3_ref[...] = h2 + _sig(pg.astype(F32)) * pe

    row = _cols(tm, D, 0)
    return _pcall(
        body, (ff, ff, h1, p, wfo, gpost, wpg, wple), name=name, grid=(s_len // tm,),
        in_specs=[_cols(tm, D_FF, 0), _cols(tm, D_FF, 1), row, _p_spec(tm, layer), _rowsharded(D_FF // N_DEV),
                  _whole((1, D)), _rowsharded(BLK), _whole((PLE, D))],
        out_specs=[_cols(tm, D_FF, 0), row, row, row, row],
        out_shape=[_sds((s_len, D_FF), BF), _sds((s_len, D), BF), _sds((s_len, D), F32), _sds((s_len, D), BF), _sds((s_len, D), F32)],
        comm=comm)


def ple_ffn_bwd(dh3, pg, p, f, ff, wple, wpg, wfo, gpost, layer, name, target=None, comm=None):
    s_len = dh3.shape[0]
    tm = min(TMB, s_len)
    nt = s_len // tm
    with_loss = target is not None

    def body(*refs):
        if with_loss:
            t_ref, refs, loss_ref, loss_acc = refs[0], refs[1:-2], refs[-2], refs[-1]
        (dh3_ref, pg_ref, p_ref, f_ref, fg_ref, fu_ref, wple_ref, wpg_ref, wfo_ref, g_ref,
         dh2_ref, dpe_ref, dpg_ref, df_ref, dff_ref, dg_ref) = refs
        i = pl.program_id(0)
        dh3v = dh3_ref[...]
        if with_loss:
            err = dh3v - t_ref[...]
            dh3v = err * (1.0 / D)
            _accum(loss_acc, i == 0, _colsum(err * err))

            @pl.when(i == nt - 1)
            def _():
                loss_ref[...] = jnp.broadcast_to(jnp.sum(loss_acc[...], axis=1, keepdims=True) * (0.5 / D), (1, LC))
        s = _sig(pg_ref[...].astype(F32))
        pe = _dot(p_ref[...].astype(BF), wple_ref[...])
        dpe_ref[...] = (dh3v * s).astype(BF)
        dpg = (dh3v * pe * s * (1.0 - s)).astype(BF)
        dpg_ref[...] = dpg
        dh2 = dh3v + _dot_nt(dpg, wpg_ref[...].reshape(D, D))
        dh2_ref[...] = dh2
        fv = f_ref[...].astype(F32)
        r = _rstd(fv)
        _accum(dg_ref, i == 0, _colsum(dh2 * fv * r))
        df = _rms_bwd(fv, dh2 * g_ref[...], r).astype(BF)
        df_ref[...] = df
        dact = _dot_nt(df, wfo_ref[...].reshape(D_FF, D))
        gt = fg_ref[...].astype(F32)
        sg = _sig(gt)
        up = fu_ref[...].astype(F32)
        dff_ref[:, 0:D_FF] = (dact * up * sg * (1.0 + gt * (1.0 - sg))).astype(BF)
        dff_ref[:, D_FF:2 * D_FF] = (dact * gt * sg).astype(BF)

    row = _cols(tm, D, 0)
    args = (dh3, pg, p, f, ff, ff, wple, wpg, wfo, gpost)
    in_specs = [row, row, _p_spec(tm, layer), row, _cols(tm, D_FF, 0), _cols(tm, D_FF, 1), _whole((PLE, D)),
                _rowsharded(BLK), _rowsharded(D_FF // N_DEV), _whole((1, D))]
    out_specs = [row, row, row, row, _cols(tm, 2 * D_FF, 0), _whole((1, D))]
    out_shape = [_sds((s_len, D), F32), _sds((s_len, D), BF), _sds((s_len, D), BF), _sds((s_len, D), BF),
                 _sds((s_len, 2 * D_FF), BF), _sds((1, D), F32)]
    scratch = []
    if with_loss:
        args, in_specs = (target,) + args, [row] + in_specs
        out_specs, out_shape = out_specs + [_whole((1, LC))], out_shape + [_sds((1, LC), F32)]
        scratch = [pltpu.VMEM((1, D), F32)]
    return _pcall(body, args, name=name, grid=(nt,), in_specs=in_specs, out_specs=out_specs, out_shape=out_shape,
                  scratch_shapes=scratch, comm=comm)


def ffn_in_bwd(dff, wt, h1, dh2, gpre, name, comm=None):
    s_len = h1.shape[0]
    tm = min(TMB, s_len)
    nb, tn, _ = wt.shape

    def body(dff_ref, w_ref, h1_ref, dh2_ref, g_ref, dh1_ref, dg_ref):
        dhn = _dot(dff_ref[:, 0:tn], w_ref[0])
        for j in range(1, nb):
            dhn = dhn + _dot(dff_ref[:, j * tn:(j + 1) * tn], w_ref[j])
        x = h1_ref[...]
        r = _rstd(x)
        _accum(dg_ref, pl.program_id(0) == 0, _colsum(dhn * x * r))
        dh1_ref[...] = dh2_ref[...] + _rms_bwd(x, dhn * g_ref[...], r)

    row = _cols(tm, D, 0)
    return _pcall(
        body, (dff, wt, h1, dh2, gpre), name=name, grid=(s_len // tm,),
        in_specs=[_cols(tm, nb * tn, 0), pl.BlockSpec((nb, tn, D), lambda i: (0, 0, 0), pipeline_mode=pl.Buffered(1)), row, row,
                  _whole((1, D))],
        out_specs=[row, _whole((1, D))],
        out_shape=[_sds((s_len, D), F32), _sds((1, D), F32)], comm=comm)


def mix_post_bwd(dh1, mo, gpost, wout, proj, bra, brb, brc, win, name, comm=None):
    s_len = dh1.shape[0]
    tm = min(TMB, s_len)

    def body(dh1_ref, mo_ref, g_ref, wo_ref, z0, z1, z2, a_ref, b_ref, c_ref, w5, w6, w7,
             dmo_ref, da_ref, db_ref, dc_ref, dz_ref, dhn_ref, dg_ref):
        i = pl.program_id(0)
        dh1v = dh1_ref[...]
        mo_v = mo_ref[...].astype(F32)
        r = _rstd(mo_v)
        _accum(dg_ref, i == 0, _colsum(dh1v * mo_v * r))
        dmo = _rms_bwd(mo_v, dh1v * g_ref[...], r).astype(BF)
        dmo_ref[...] = dmo
        dmerged = _dot_nt(dmo, wo_ref[...].reshape(D, D))
        dhn = jnp.zeros((tm, D), F32)
        for k, (z, br, dbr, w) in enumerate(((z0, a_ref, da_ref, w5), (z1, b_ref, db_ref, w6), (z2, c_ref, dc_ref, w7))):
            s = _sig(z[...].astype(F32))
            dbr[...] = (dmerged * s).astype(BF)
            dz = (dmerged * br[...].astype(F32) * s * (1.0 - s)).astype(BF)
            dz_ref[:, k * D:(k + 1) * D] = dz
            dhn = dhn + _dot_nt(dz, w[...])
        dhn_ref[...] = dhn

    row = _cols(tm, D, 0)
    return _pcall(
        body, (dh1, mo, gpost, wout, proj, proj, proj, bra, brb, brc, win, win, win), name=name, grid=(s_len // tm,),
        in_specs=[row, row, _whole((1, D)), _rowsharded(BLK), _cols(tm, D, 5), _cols(tm, D, 6), _cols(tm, D, 7), row, row, row,
                  _win_block(5), _win_block(6), _win_block(7)],
        out_specs=[row, row, row, row, _cols(tm, 3 * D, 0), row, _whole((1, D))],
        out_shape=[_sds((s_len, D), BF)] * 4 + [_sds((s_len, 3 * D), BF), _sds((s_len, D), F32), _sds((1, D), F32)], comm=comm)


def sgu_bwd(dbr, wo, proj, ws, wst, bsfull, gv, bv, win, dhn_in, name, comm=None):
    s_len = dbr.shape[0]
    tm = min(TMB, s_len)
    nt = s_len // tm

    def body(dbr_ref, wo_ref, zu_ref, zv_ref, ws_ref, wst_ref, bs_ref, gv_ref, bv_ref, w0, w1, dhn_in_ref,
             dz_ref, dhn_ref, dws_ref, dbs_ref, dgv_ref, dbv_ref, vln_s, mix_s, dmix_s, dvln_s, bs_acc):
        i = pl.program_id(0)
        first = i == 0
        dsgu = _dot_nt(dbr_ref[...], wo_ref[...].reshape(D, D))
        zu = zu_ref[...].astype(F32)
        zv = zv_ref[...].astype(F32)
        u, tu = _gelu(zu)
        v, tv = _gelu(zv)
        vhat, rs = _ln_fwd(v)
        vln_s[...] = (vhat * gv_ref[...] + bv_ref[...]).astype(BF)
        _sgu_mix(ws_ref, vln_s, mix_s, bs_ref, tm, False)
        du = dsgu * mix_s[...]
        dmix = dsgu * u
        dmix_s[...] = dmix.astype(BF)
        blocks = dmix[0:BLK]
        for n in range(1, tm // BLK):
            blocks = blocks + dmix[n * BLK:(n + 1) * BLK]
        _accum(bs_acc, first, blocks)
        for hd in range(HEADS):
            cs = slice(hd * BLK, (hd + 1) * BLK)
            g = _dot_nt(dmix_s[0:BLK, cs], vln_s[0:BLK, cs])
            for n in range(1, tm // BLK):
                g = g + _dot_nt(dmix_s[n * BLK:(n + 1) * BLK, cs], vln_s[n * BLK:(n + 1) * BLK, cs])

            @pl.when(first)
            def _():
                dws_ref[hd] = g

            @pl.when(jnp.logical_not(first))
            def _():
                dws_ref[hd] += g
        _sgu_mix(wst_ref, dmix_s, dvln_s, None, tm, True)
        dvln = dvln_s[...]
        _accum(dgv_ref, first, _colsum(dvln * vhat))
        _accum(dbv_ref, first, _colsum(dvln))
        dv = _ln_bwd(dvln * gv_ref[...], vhat, rs)
        dzu = (du * _gelu_grad(zu, tu)).astype(BF)
        dzv = (dv * _gelu_grad(zv, tv)).astype(BF)
        dz_ref[:, 0:D] = dzu
        dz_ref[:, D:2 * D] = dzv
        dhn_ref[...] = dhn_in_ref[...] + _dot_nt(dzu, w0[...]) + _dot_nt(dzv, w1[...])

        @pl.when(i == nt - 1)
        def _():
            mask = _sgu_mask(False)
            for hd in range(HEADS):
                dws_ref[hd] = jnp.where(mask, dws_ref[hd], 0.0)
                dbs_ref[:, hd:hd + 1] = jnp.sum(bs_acc[:, hd * BLK:(hd + 1) * BLK], axis=1, keepdims=True)

    row = _cols(tm, D, 0)
    vec = _whole((1, D))
    return _pcall(
        body, (dbr, wo, proj, proj, ws, wst, bsfull, gv, bv, win, win, dhn_in), name=name, grid=(nt,),
        in_specs=[row, _rowsharded(BLK), _cols(tm, D, 0), _cols(tm, D, 1), _whole((HEADS, BLK, BLK)), _whole((HEADS, BLK, BLK)),
                  _whole((BLK, D)), vec, vec, _win_block(0), _win_block(1), row],
        out_specs=[_cols(tm, 2 * D, 0), row, _whole((HEADS, BLK, BLK)), _whole((BLK, HEADS)), vec, vec],
        out_shape=[_sds((s_len, 2 * D), BF), _sds((s_len, D), F32), _sds((HEADS, BLK, BLK), F32), _sds((BLK, HEADS), F32),
                   _sds((1, D), F32), _sds((1, D), F32)],
        scratch_shapes=[pltpu.VMEM((tm, D), BF), pltpu.VMEM((tm, D), F32), pltpu.VMEM((tm, D), BF), pltpu.VMEM((tm, D), F32),
                        pltpu.VMEM((BLK, D), F32)], comm=comm)


def conv_bwd(dbr, wo, proj, conv, gln, bln, name, comm=None):
    s_len = dbr.shape[0]
    tm = min(TMB, s_len)
    nt = s_len // tm

    def body(dbr_ref, wo_ref, a_ref, g_ref, ah_ref, gh_ref, cv_ref, gln_ref, bln_ref,
             dc_ref, dw_ref, dbdw_ref, dgln_ref, dbln_ref, ext_s, dc_s, dw_acc):
        i = pl.program_id(0)
        first = i == 0
        dcb = _dot_nt(dbr_ref[...], wo_ref[...].reshape(D, D))
        _glu_ext(a_ref, g_ref, ah_ref, gh_ref, ext_s, first)
        chat, rs = _ln_fwd(cv_ref[...].astype(F32))
        yl = chat * gln_ref[...] + bln_ref[...]
        sy = _sig(yl)
        dyl = dcb * sy * (1.0 + yl * (1.0 - sy))
        _accum(dgln_ref, first, _colsum(dyl * chat))
        _accum(dbln_ref, first, _colsum(dyl))
        dc = _ln_bwd(dyl * gln_ref[...], chat, rs)
        _accum(dbdw_ref, first, _colsum(dc))
        dc_ref[...] = dc.astype(BF)
        dc_s[...] = dc

        @pl.when(first)
        def _():
            dw_acc[...] = jnp.zeros_like(dw_acc)

        def chunk(ci, carry):
            r0 = pl.multiple_of((ci // (D // LC)) * RC, RC)
            l0 = pl.multiple_of((ci % (D // LC)) * LC, LC)
            win = ext_s[pl.ds(r0, RC + HALO), pl.ds(l0, LC)]
            dcw = dc_s[pl.ds(r0, RC), pl.ds(l0, LC)]
            for r in range(8):
                wr = win if r == 0 else pltpu.roll(win, r, 0)
                for m in range(4):
                    d = 8 * m + r
                    if d < CONV_W:
                        k = CONV_W - 1 - d
                        prod = dcw * wr[HALO - 8 * m:HALO - 8 * m + RC]
                        dw_acc[k * 8:(k + 1) * 8, pl.ds(l0, LC)] += prod.reshape(RC // 8, 8, LC).sum(axis=0)
            return carry
        lax.fori_loop(0, (tm // RC) * (D // LC), chunk, 0)

        @pl.when(i == nt - 1)
        def _():
            dw_ref[...] = dw_acc[...].reshape(HALO, 8, D).sum(axis=1)

    row = _cols(tm, D, 0)
    vec = _whole((1, D))
    return _pcall(
        body, (dbr, wo, proj, proj, proj, proj, conv, gln, bln), name=name, grid=(nt,),
        in_specs=[row, _rowsharded(BLK), _cols(tm, D, 2), _cols(tm, D, 3), _prev_halo(tm, D, 2), _prev_halo(tm, D, 3),
                  row, vec, vec],
        out_specs=[row, _whole((HALO, D)), vec, vec, vec],
        out_shape=[_sds((s_len, D), BF), _sds((HALO, D), F32), _sds((1, D), F32), _sds((1, D), F32), _sds((1, D), F32)],
        scratch_shapes=[pltpu.VMEM((tm + HALO, D), F32), pltpu.VMEM((tm, D), F32), pltpu.VMEM((HALO * 8, D), F32)], comm=comm)


def pool_bwd(dbr, wo, pooled, wpool, spool, name):
    s_len = dbr.shape[0]
    tm = min(TMB, s_len)

    def body(dbr_ref, wo_ref, pl_ref, wp_ref, sp_ref, dmr_ref, q_ref, dsp_ref, mr_s):
        i = pl.program_id(0)
        dpm = _dot_nt(dbr_ref[...], wo_ref[...].reshape(D, D))
        for gi in range(4):
            cs = slice(gi * POOL_GD, (gi + 1) * POOL_GD)
            mr_s[:, cs] = _dot(pl_ref[:, cs], wp_ref[gi])
        _accum(dsp_ref, i == 0, _colsum(dpm * mr_s[...]))
        dmr = (dpm * sp_ref[...]).astype(BF)
        dmr_ref[...] = dmr
        for gi, w in enumerate(POOL_WINDOWS):
            cs = slice(gi * POOL_GD, (gi + 1) * POOL_GD)
            q_ref[:, cs] = (_dot_nt(dmr[:, cs], wp_ref[gi]) * _inv_count(i, tm, w)).astype(BF)

    row = _cols(tm, D, 0)
    return pl.pallas_call(
        body, name=name, grid=(s_len // tm,),
        in_specs=[row, _rowsharded(BLK), row, _whole((4, POOL_GD, POOL_GD)), _whole((1, D))],
        out_specs=[row, row, _whole((1, D))],
        out_shape=[_sds((s_len, D), BF), _sds((s_len, D), BF), _sds((1, D), F32)],
        scratch_shapes=[pltpu.VMEM((tm, D), F32)], compiler_params=_params(1))(dbr, wo, pooled, wpool, spool)


def seq_bwd(dc, q, proj, wdw, win, dhn_in, h, dh1, gpre, name, comm=None):
    s_len = dc.shape[0]
    tm = min(TMB, s_len)
    nt = s_len // tm

    def body(dc_ref, dch_ref, q_ref, qh_ref, a_ref, g_ref, wdw_ref, w2, w3, w4, dhn_in_ref, h_ref, dh1_ref, gpre_ref,
             dz_ref, dh_ref, dgpre_ref, ext_s, dhc_s, qext_s):
        i = pl.program_id(0)
        last = i == nt - 1
        ext_s[0:tm, :] = dc_ref[...].astype(F32)
        ext_s[tm:, :] = jnp.where(last, 0.0, dch_ref[...].astype(F32))

        def chunk(ci, carry):
            r0 = pl.multiple_of((ci // (D // LC)) * RC, RC)
            l0 = pl.multiple_of((ci % (D // LC)) * LC, LC)
            win_ = ext_s[pl.ds(r0, RC + HALO), pl.ds(l0, LC)]
            acc = jnp.zeros((RC, LC), F32)
            for r in range(8):
                wr = win_ if r == 0 else pltpu.roll(win_, RC + HALO - r, 0)
                for m in range(4):
                    d = 8 * m + r
                    if d < CONV_W:
                        k = CONV_W - 1 - d
                        acc = acc + wdw_ref[k:k + 1, pl.ds(l0, LC)] * wr[8 * m:8 * m + RC]
            dhc_s[pl.ds(r0, RC), pl.ds(l0, LC)] = acc
            return carry
        lax.fori_loop(0, (tm // RC) * (D // LC), chunk, 0)

        dhc = dhc_s[...]
        av = a_ref[...].astype(F32)
        sg = _sig(g_ref[...].astype(F32))
        da = (dhc * sg).astype(BF)
        dg = (dhc * av * sg * (1.0 - sg)).astype(BF)
        dz_ref[:, 0:D] = da
        dz_ref[:, D:2 * D] = dg

        qext_s[0:tm, :] = q_ref[...].astype(F32)
        qext_s[tm:, :] = jnp.where(last, 0.0, qh_ref[...].astype(F32))
        for gi, w in enumerate(POOL_WINDOWS):
            cs = slice(gi * POOL_GD, (gi + 1) * POOL_GD)
            e = qext_s[:, cs]
            s = e
            sh = 1
            while sh < w:
                s = s + pltpu.roll(s, tm + HALO - sh, 0)
                sh *= 2
            dz_ref[:, 2 * D + gi * POOL_GD:2 * D + (gi + 1) * POOL_GD] = (s[0:tm] - e[0:tm] * _count(i, tm, w)).astype(BF)
        dhn = dhn_in_ref[...] + _dot_nt(da, w2[...]) + _dot_nt(dg, w3[...]) + _dot_nt(dz_ref[:, 2 * D:3 * D], w4[...])
        x = h_ref[...]
        r = _rstd(x)
        _accum(dgpre_ref, i == 0, _colsum(dhn * x * r))
        dh_ref[...] = dh1_ref[...] + _rms_bwd(x, dhn * gpre_ref[...], r)

    row = _cols(tm, D, 0)
    return _pcall(
        body, (dc, dc, q, q, proj, proj, wdw, win, win, win, dhn_in, h, dh1, gpre), name=name, grid=(nt,),
        in_specs=[row, _next_halo(tm, D, 0, s_len), row, _next_halo(tm, D, 0, s_len), _cols(tm, D, 2), _cols(tm, D, 3),
                  _whole((HALO, D)), _win_block(2), _win_block(3), _win_block(4), row, row, row, _whole((1, D))],
        out_specs=[_cols(tm, 3 * D, 0), row, _whole((1, D))],
        out_shape=[_sds((s_len, 3 * D), BF), _sds((s_len, D), F32), _sds((1, D), F32)],
        scratch_shapes=[pltpu.VMEM((tm + HALO, D), F32), pltpu.VMEM((tm, D), F32), pltpu.VMEM((tm + HALO, D), F32)], comm=comm)


def wgrad(a, b, tk, tn, name, stacked=False, diag=False, a_spec=None, comm=None):
    s_len = b.shape[0]
    k_dim = a.shape[-1]
    n_dim = b.shape[1]
    ts = min(TS, s_len)
    nk = 1 if diag else k_dim // tk
    nn, ns = n_dim // tn, s_len // ts

    def body(a_ref, b_ref, o_ref, acc):
        s = pl.program_id(2)
        _accum(acc, s == 0, _dot_tn(a_ref[...].astype(BF), b_ref[...].astype(BF)))

        @pl.when(s == ns - 1)
        def _():
            o_ref[...] = acc[...].astype(BF).reshape(o_ref.shape)

    if a_spec is None:
        a_spec = pl.BlockSpec((ts, tk), (lambda k, n, s: (s, n)) if diag else (lambda k, n, s: (s, k)))
    if stacked or diag:
        out_shape = _sds((nn, tk if diag else k_dim, tn), BF)
        o_spec = pl.BlockSpec((1, tk, tn), lambda k, n, s: (n, k, 0))
    else:
        out_shape = _sds((k_dim, n_dim), BF)
        o_spec = pl.BlockSpec((tk, tn), lambda k, n, s: (k, n))
    (out,), got = _pcall(
        body, (a, b), name=name, grid=(nk, nn, ns),
        in_specs=[a_spec, pl.BlockSpec((ts, tn), lambda k, n, s: (s, n))], out_specs=[o_spec], out_shape=[out_shape],
        scratch_shapes=[pltpu.VMEM((tk, tn), F32)], comm=comm)
    return out if comm is None else (out, got)


_WEIGHTS = ['g_mix_pre', 'w_in', 'w_sgu_s', 'b_sgu_s', 'g_sgu_v', 'b_sgu_v', 'w_sgu_out', 'w_dw', 'b_dw', 'g_conv_ln', 'b_conv_ln',
            'w_conv_out', 'w_pool', 's_pool', 'w_pool_out', 'w_out', 'g_mix_post', 'g_ffn_pre', 'w_ffn_in', 'w_ffn_out', 'g_ffn_post',
            'w_ple', 'w_ple_gate']
_SHARDED = ['w_in', 'w_sgu_out', 'w_conv_out', 'w_pool', 'w_pool_out', 'w_out', 'w_ffn_in', 'w_ffn_out', 'w_ple', 'w_ple_gate']
_VECTORS = ['g_mix_pre', 'g_sgu_v', 'b_sgu_v', 'b_dw', 'g_conv_ln', 'b_conv_ln', 's_pool', 'g_mix_post', 'g_ffn_pre', 'g_ffn_post']
_SUBLANES = 8
_SGU_ROWS = HEADS * BLK * BLK // D
_REP_ROWS = _SUBLANES * (len(_VECTORS) + 2) + _SGU_ROWS


def _pack_replicated(t, layer):
    rows = [jnp.pad(t[k][layer].reshape(1, D), ((0, _SUBLANES - 1), (0, 0))) for k in _VECTORS + ['b_sgu_s']]
    return jnp.concatenate(rows + [t['w_sgu_s'][layer].reshape(_SGU_ROWS, D), jnp.zeros((_SUBLANES, D), F32)], axis=0)


def _unpack_replicated(packed):
    out = {}
    for i, k in enumerate(_VECTORS):
        out[k] = packed[:, _SUBLANES * i, :]
    o = _SUBLANES * len(_VECTORS)
    out['b_sgu_s'] = packed[:, o, :].reshape(2, HEADS, BLK)
    out['w_sgu_s'] = packed[:, o + _SUBLANES:o + _SUBLANES + _SGU_ROWS, :].reshape(2, HEADS, BLK, BLK)
    return out


def _pad_taps(w):
    return jnp.pad(w, ((0, HALO - CONV_W), (0, 0)))


def kernel(x, p, g_mix_pre, w_in, w_sgu_s, b_sgu_s, g_sgu_v, b_sgu_v, w_sgu_out, w_dw, b_dw, g_conv_ln, b_conv_ln, w_conv_out, w_pool, s_pool, w_pool_out, w_out, g_mix_post, g_ffn_pre, w_ffn_in, w_ffn_out, g_ffn_post, w_ple, w_ple_gate, loss_target, m_g_mix_pre, m_w_in, m_w_sgu_s, m_b_sgu_s, m_g_sgu_v, m_b_sgu_v, m_w_sgu_out, m_w_dw, m_b_dw, m_g_conv_ln, m_b_conv_ln, m_w_conv_out, m_w_pool, m_s_pool, m_w_pool_out, m_w_out, m_g_mix_post, m_g_ffn_pre, m_w_ffn_in, m_w_ffn_out, m_g_ffn_post, m_w_ple, m_w_ple_gate, v_g_mix_pre, v_w_in, v_w_sgu_s, v_b_sgu_s, v_g_sgu_v, v_b_sgu_v, v_w_sgu_out, v_w_dw, v_b_dw, v_g_conv_ln, v_b_conv_ln, v_w_conv_out, v_w_pool, v_s_pool, v_w_pool_out, v_w_out, v_g_mix_post, v_g_ffn_pre, v_w_ffn_in, v_w_ffn_out, v_g_ffn_post, v_w_ple, v_w_ple_gate):
    W = dict(g_mix_pre=g_mix_pre, w_in=w_in, w_sgu_s=w_sgu_s, b_sgu_s=b_sgu_s, g_sgu_v=g_sgu_v, b_sgu_v=b_sgu_v, w_sgu_out=w_sgu_out,
             w_dw=w_dw, b_dw=b_dw, g_conv_ln=g_conv_ln, b_conv_ln=b_conv_ln, w_conv_out=w_conv_out, w_pool=w_pool, s_pool=s_pool,
             w_pool_out=w_pool_out, w_out=w_out, g_mix_post=g_mix_post, g_ffn_pre=g_ffn_pre, w_ffn_in=w_ffn_in, w_ffn_out=w_ffn_out,
             g_ffn_post=g_ffn_post, w_ple=w_ple, w_ple_gate=w_ple_gate)
    M = dict(g_mix_pre=m_g_mix_pre, w_in=m_w_in, w_sgu_s=m_w_sgu_s, b_sgu_s=m_b_sgu_s, g_sgu_v=m_g_sgu_v, b_sgu_v=m_b_sgu_v,
             w_sgu_out=m_w_sgu_out, w_dw=m_w_dw, b_dw=m_b_dw, g_conv_ln=m_g_conv_ln, b_conv_ln=m_b_conv_ln, w_conv_out=m_w_conv_out,
             w_pool=m_w_pool, s_pool=m_s_pool, w_pool_out=m_w_pool_out, w_out=m_w_out, g_mix_post=m_g_mix_post, g_ffn_pre=m_g_ffn_pre,
             w_ffn_in=m_w_ffn_in, w_ffn_out=m_w_ffn_out, g_ffn_post=m_g_ffn_post, w_ple=m_w_ple, w_ple_gate=m_w_ple_gate)
    V = dict(g_mix_pre=v_g_mix_pre, w_in=v_w_in, w_sgu_s=v_w_sgu_s, b_sgu_s=v_b_sgu_s, g_sgu_v=v_g_sgu_v, b_sgu_v=v_b_sgu_v,
             w_sgu_out=v_w_sgu_out, w_dw=v_w_dw, b_dw=v_b_dw, g_conv_ln=v_g_conv_ln, b_conv_ln=v_b_conv_ln, w_conv_out=v_w_conv_out,
             w_pool=v_w_pool, s_pool=v_s_pool, w_pool_out=v_w_pool_out, w_out=v_w_out, g_mix_post=v_g_mix_post, g_ffn_pre=v_g_ffn_pre,
             w_ffn_in=v_w_ffn_in, w_ffn_out=v_w_ffn_out, g_ffn_post=v_g_ffn_post, w_ple=v_w_ple, w_ple_gate=v_w_ple_gate)

    my_c = lax.axis_index("c")
    core_id = my_c.astype(jnp.int32).reshape(1)
    my_chip = 2 * lax.axis_index("x") + lax.axis_index("y")
    my_dev = 2 * my_chip + my_c
    s_len = x.shape[1]
    h0 = x.reshape(s_len, D)
    target = loss_target.reshape(s_len, D)

    shard = [{k: W[k][l].astype(BF) for k in _SHARDED} for l in range(2)]
    for l in range(2):
        shard[l]['w_dw'] = w_dw[l]
        shard[l]['w_ffn_in'] = jnp.swapaxes(w_ffn_in[l], 0, 1).astype(BF)
    mixer_w = ['w_sgu_out', 'w_conv_out', 'w_pool', 'w_pool_out', 'w_out', 'w_dw']
    ffn_w = ['w_ffn_in', 'w_ffn_out', 'w_ple', 'w_ple_gate']
    hosted_gather = {
        'norm_proj_in': (0, mixer_w),
        'conv_fwd': (0, ffn_w),
        'merge_out': (1, mixer_w),
        'norm_proj_ffn': (1, ['w_in']),
        'ffn_out': (1, ffn_w),
    }
    G = [{'w_in': run_comm(Gather([shard[0]['w_in']]), "gather_w_in_0")[0]}, {}]

    def gather_in(layer, call):
        if layer != 0:
            return None, (lambda got: None)
        to_layer, keys = hosted_gather[call]
        return Gather([shard[to_layer][k] for k in keys]), (lambda got: G[to_layer].update(zip(keys, got)))

    def natural_mixer(g):
        wpool = jnp.transpose(g['w_pool'], (1, 0, 2, 3)).reshape(4, POOL_GD, POOL_GD)
        wdw = jnp.transpose(g['w_dw'].reshape(N_DEV, CONV_W, BLK), (1, 0, 2)).reshape(CONV_W, D)
        return dict(wpool=wpool, wdw=_pad_taps(wdw))

    def natural_ffn(g):
        wfit = g['w_ffn_in'].reshape(4, D_FF // 2, D)
        wple = jnp.transpose(g['w_ple'], (1, 0, 2)).reshape(PLE, D)
        return dict(wfit=wfit, wple=wple)

    def vec(name, layer):
        return W[name][layer].reshape(1, D)

    saved = []
    h = h0
    for l in range(2):
        g = G[l]
        comm, land = gather_in(l, 'norm_proj_in')
        (proj, hn), got = norm_proj(h, vec('g_mix_pre', l), g['w_in'], f"norm_proj_in_{l}", comm=comm)
        land(got)
        nat = natural_mixer(g)
        bsfull = jnp.repeat(b_sgu_s[l].T, BLK, axis=1)
        wst = jnp.swapaxes(w_sgu_s[l], 1, 2)
        sgu, bra = sgu_fwd(proj, w_sgu_s[l], bsfull, vec('g_sgu_v', l), vec('b_sgu_v', l), g['w_sgu_out'], f"sgu_fwd_{l}")
        comm, land = gather_in(l, 'conv_fwd')
        (conv, cb, brb), got = conv_fwd(proj, nat['wdw'], vec('b_dw', l), vec('g_conv_ln', l), vec('b_conv_ln', l), g['w_conv_out'],
                                        f"conv_fwd_{l}", comm=comm)
        land(got)
        nat.update(natural_ffn(g))
        pooled, pm, brc = pool_fwd(proj, nat['wpool'], vec('s_pool', l), g['w_pool_out'], f"pool_fwd_{l}")
        comm, land = gather_in(l, 'merge_out')
        (merged, mo, h1), got = merge_out(proj, bra, brb, brc, h, g['w_out'], vec('g_mix_post', l), f"merge_out_{l}", comm=comm)
        land(got)
        comm, land = gather_in(l, 'norm_proj_ffn')
        (ff, hn2), got = norm_proj(h1, vec('g_ffn_pre', l), nat['wfit'], f"norm_proj_ffn_{l}", transposed=True, comm=comm)
        land(got)
        comm, land = gather_in(l, 'ffn_out')
        (act, f, h2, pg, h3), got = ffn_out(ff, h1, p, g['w_ffn_out'], vec('g_ffn_post', l), g['w_ple_gate'], nat['wple'], l,
                                            f"ffn_out_{l}", comm=comm)
        land(got)
        saved.append(dict(h=h, nat=nat, bsfull=bsfull, wst=wst, proj=proj, hn=hn, sgu=sgu, bra=bra, conv=conv, cb=cb, brb=brb,
                          pooled=pooled, pm=pm, brc=brc, merged=merged, mo=mo, h1=h1, ff=ff, hn2=hn2, act=act, f=f, h2=h2, pg=pg))
        h = h3

    dh = h

    parts = {k: [None, None] for k in _SHARDED}
    small = {k: [None, None] for k in _VECTORS + ['b_sgu_s', 'w_sgu_s', 'w_dw']}
    chip_parts = [{}, {}]
    from_chips = [{}, {}]
    gathered_small = [None, None]

    def to_sibling(layer, keys):
        return ToSibling([parts[k][layer] for k in keys])

    def add_siblings(layer, keys, from_sibling):
        for k, rv in zip(keys, from_sibling):
            st = parts[k][layer]
            cols = st.shape[-1]
            chip_parts[layer][k] = add_core_side(st.reshape(N_DEV, -1, cols), rv.reshape(4, -1, cols), core_id,
                                                 f"rs_add_{k}_{layer}").reshape(rv.shape)

    def to_chips(layer, keys):
        return ToChips([chip_parts[layer][k] for k in keys])

    def small_pack(layer):
        return jnp.concatenate([_pack_replicated(small, layer), small['w_dw'][layer]], axis=0)

    ffn_group = ['w_ffn_in', 'w_ffn_out', 'w_ple_gate', 'w_ple']
    mix_group = ['w_out', 'w_sgu_out', 'w_conv_out', 'w_pool_out', 'w_pool']
    big = ['w_in', 'w_ffn_in']
    others = [k for k in _SHARDED if k not in big]
    hosted_rs = {
        'ple_ffn_bwd': (lambda: to_sibling(1, _SHARDED), lambda got: add_siblings(1, _SHARDED, got)),
        'mix_post_bwd': (lambda: to_sibling(0, ffn_group), lambda got: add_siblings(0, ffn_group, got)),
        'ffn_in_bwd': (lambda: to_chips(1, others), lambda got: from_chips[1].update(zip(others, got))),
        'sgu_bwd': (lambda: to_chips(1, big), lambda got: from_chips[1].update(zip(big, got))),
        'conv_bwd': (lambda: to_chips(0, ffn_group), lambda got: from_chips[0].update(zip(ffn_group, got))),
        'wgrad_in_sgu': (lambda: to_sibling(0, mix_group), lambda got: add_siblings(0, mix_group, got)),
        'wgrad_in_seq': (lambda: to_chips(0, mix_group), lambda got: from_chips[0].update(zip(mix_group, got))),
        'wgrad_in_gate': (lambda: Gather([small_pack(0), small_pack(1)]), lambda got: gathered_small.__setitem__(slice(0, 2), got)),
    }

    def exchange_in(layer, call):
        if layer != 0 or call not in hosted_rs:
            return None, (lambda got: None)
        make, land = hosted_rs[call]
        return make(), land

    for l in (1, 0):
        sv, g, nat = saved[l], G[l], saved[l]['nat']

        def wg(call, a, b, tk, tn, **kw):
            comm, land = exchange_in(l, call)
            if comm is None:
                return wgrad(a, b, tk, tn, f"{call}_{l}", **kw)
            out, got = wgrad(a, b, tk, tn, f"{call}_{l}", comm=comm, **kw)
            land(got)
            return out

        comm, land = exchange_in(l, 'ple_ffn_bwd')
        res, got = ple_ffn_bwd(
            dh, sv['pg'], p, sv['f'], sv['ff'], nat['wple'], g['w_ple_gate'], g['w_ffn_out'], vec('g_ffn_post', l), l,
            f"ple_ffn_bwd_{l}", target=target if l == 1 else None, comm=comm)
        land(got)
        dh2, dpe, dpg, df, dff, small['g_ffn_post'][l] = res[:6]
        if l == 1:
            loss = lax.psum(res[6][0, 0], ("x", "y", "c"))
        comm, land = exchange_in(l, 'ffn_in_bwd')
        (dh1, small['g_ffn_pre'][l]), got = ffn_in_bwd(dff, nat['wfit'], sv['h1'], dh2, vec('g_ffn_pre', l), f"ffn_in_bwd_{l}",
                                                       comm=comm)
        land(got)
        p_spec = pl.BlockSpec((None, None, min(TS, s_len), PLE), functools.partial(lambda k, n, s, ll: (ll, 0, s, 0), ll=l))
        parts['w_ple'][l] = jnp.transpose(wg('wgrad_ple', p, dpe, PLE, D, a_spec=p_spec).reshape(PLE, N_DEV, BLK), (1, 0, 2))
        parts['w_ple_gate'][l] = wg('wgrad_ple_gate', sv['h2'], dpg, D, D).reshape(N_DEV, BLK, D)
        parts['w_ffn_out'][l] = wg('wgrad_ffn_out', sv['act'], df, D_FF // 2, D).reshape(N_DEV, D_FF // N_DEV, D)
        parts['w_ffn_in'][l] = wg('wgrad_ffn_in', dff, sv['hn2'], D_FF // 2, D).reshape(N_DEV, D_FF // 4, D)

        comm, land = exchange_in(l, 'mix_post_bwd')
        (dmo, dbra, dbrb, dbrc, dzg, dhn_g, small['g_mix_post'][l]), got = mix_post_bwd(
            dh1, sv['mo'], vec('g_mix_post', l), g['w_out'], sv['proj'], sv['bra'], sv['brb'], sv['brc'], g['w_in'],
            f"mix_post_bwd_{l}", comm=comm)
        land(got)
        comm, land = exchange_in(l, 'sgu_bwd')
        (dzs, dhn_ag, dws, dbs, small['g_sgu_v'][l], small['b_sgu_v'][l]), got = sgu_bwd(
            dbra, g['w_sgu_out'], sv['proj'], w_sgu_s[l], sv['wst'], sv['bsfull'], vec('g_sgu_v', l), vec('b_sgu_v', l), g['w_in'],
            dhn_g, f"sgu_bwd_{l}", comm=comm)
        land(got)
        small['w_sgu_s'][l] = dws
        small['b_sgu_s'][l] = dbs.T
        comm, land = exchange_in(l, 'conv_bwd')
        (dc, dwdw, small['b_dw'][l], small['g_conv_ln'][l], small['b_conv_ln'][l]), got = conv_bwd(
            dbrb, g['w_conv_out'], sv['proj'], sv['conv'], vec('g_conv_ln', l), vec('b_conv_ln', l), f"conv_bwd_{l}", comm=comm)
        land(got)
        small['w_dw'][l] = dwdw
        dmr, q, small['s_pool'][l] = pool_bwd(dbrc, g['w_pool_out'], sv['pooled'], nat['wpool'], vec('s_pool', l), f"pool_bwd_{l}")
        comm, land = exchange_in(l, 'seq_bwd')
        (dzc, dh, small['g_mix_pre'][l]), got = seq_bwd(dc, q, sv['proj'], nat['wdw'], g['w_in'], dhn_ag, sv['h'], dh1,
                                                        vec('g_mix_pre', l), f"seq_bwd_{l}", comm=comm)
        land(got)

        parts['w_out'][l] = wg('wgrad_out', sv['merged'], dmo, D, D).reshape(N_DEV, BLK, D)
        parts['w_sgu_out'][l] = wg('wgrad_sgu_out', sv['sgu'], dbra, D, D).reshape(N_DEV, BLK, D)
        parts['w_conv_out'][l] = wg('wgrad_conv_out', sv['cb'], dbrb, D, D).reshape(N_DEV, BLK, D)
        parts['w_pool_out'][l] = wg('wgrad_pool_out', sv['pm'], dbrc, D, D).reshape(N_DEV, BLK, D)
        g_pool = wg('wgrad_pool', sv['pooled'], dmr, POOL_GD, POOL_GD, diag=True)
        parts['w_pool'][l] = jnp.transpose(g_pool.reshape(4, N_DEV, POOL_GD // N_DEV, POOL_GD), (1, 0, 2, 3))
        parts['w_in'][l] = jnp.concatenate([
            wg('wgrad_in_sgu', sv['hn'], dzs, D, D, stacked=True),
            wg('wgrad_in_seq', sv['hn'], dzc, D, D, stacked=True),
            wg('wgrad_in_gate', sv['hn'], dzg, D, D, stacked=True)], axis=0)
    grad_x = dh.reshape(1, s_len, D)

    add_siblings(0, ['w_in'], run_comm(to_sibling(0, ['w_in']), "rs_to_sibling_w_in_0"))
    from_chips[0]['w_in'] = run_comm(to_chips(0, ['w_in']), "rs_to_chips_w_in_0")[0]

    outs = {}
    for k in _SHARDED:
        wmv = [jnp.swapaxes(t[k], 1, 2) if k == 'w_ffn_in' else t[k] for t in (W, M, V)]
        cols = wmv[0].shape[-1]
        pieces = []
        for layer in range(2):
            own = lax.dynamic_index_in_dim(chip_parts[layer][k], my_chip, axis=0, keepdims=False).reshape(-1, cols)
            rv3 = from_chips[layer][k].reshape(3, -1, cols)
            pieces.append([(own, None), (rv3, 0), (rv3, 1), (rv3, 2)])
        res = adamw_layers(*[t.reshape(2, -1, cols) for t in wmv], pieces, f"adamw_{k}")
        res = [r.reshape(wmv[0].shape) for r in res]
        outs[k] = [jnp.swapaxes(r, 1, 2) for r in res] if k == 'w_ffn_in' else res

    packed = [jnp.stack([_pack_replicated(t, 0), _pack_replicated(t, 1)], axis=0) for t in (W, M, V)]
    rep_res = adamw_layers(*packed, [[(gathered_small[layer], d) for d in range(N_DEV)] for layer in range(2)], "adamw_replicated")
    for idx, res in enumerate(rep_res):
        for name, val in _unpack_replicated(res).items():
            outs.setdefault(name, [None] * 4)[idx] = val
    dw_sum = jnp.stack([sum_slabs(gathered_small[layer][:, _REP_ROWS:, :], f"sum_w_dw_{layer}") for layer in range(2)], axis=0)
    dw_mine = lax.dynamic_slice_in_dim(dw_sum[:, :CONV_W], my_dev * BLK, BLK, axis=2)
    res = adamw(w_dw.reshape(2 * CONV_W, BLK), m_w_dw.reshape(2 * CONV_W, BLK), v_w_dw.reshape(2 * CONV_W, BLK),
                [(dw_mine.reshape(2 * CONV_W, BLK), None)], "adamw_w_dw")
    outs['w_dw'] = [r.reshape(w_dw.shape) for r in res]

    result = [loss, grad_x]
    for idx in range(4):
        result += [outs[k][idx] for k in _WEIGHTS]
    return tuple(result)
```

```python
import functools
import math

import jax
import jax.numpy as jnp
from jax import lax
from jax.experimental import pallas as pl
from jax.experimental.pallas import tpu as pltpu

F32 = jnp.float32
BF = jnp.bfloat16

D = 1024
D_FF = 2816
PLE = 256
N_DEV = 8
HEADS = 8
BLK = 128
CHUNK = 64
CONV_W = 31
POOL_WINDOWS = (2, 4, 8, 16)
POOL_GD = 256
EPS = 1e-6

V7X_VMEM_BYTES = 64 * 2**20
VMEM_LIMIT = V7X_VMEM_BYTES * 7 // 8
HALO = 32
RC = 64
LC = 128
TM = 512
TMB = 256
TMP = 1024
TS = 2048

ADAM_LR, ADAM_B1, ADAM_B2, ADAM_EPS, ADAM_WD, ADAM_STEP = 0.001, 0.9, 0.999, 1e-08, 0.01, 10

MESH = pl.DeviceIdType.MESH
ANY = pl.BlockSpec(memory_space=pl.ANY)

_GELU_K0 = math.sqrt(2.0 / math.pi)
_GELU_K1 = 0.044715
_LOG2E = 1.4426950408889634


def _dot(a, b):
    return jnp.dot(a, b, preferred_element_type=F32)


def _dot_nt(a, b):
    return lax.dot_general(a, b, (((1,), (1,)), ((), ())), preferred_element_type=F32)


def _dot_tn(a, b):
    return lax.dot_general(a, b, (((0,), (0,)), ((), ())), preferred_element_type=F32)


def _sig(x):
    return 1.0 / (1.0 + jnp.exp2(x * (-_LOG2E)))


def _gelu(x):
    s = 1.0 / (1.0 + jnp.exp2(x * ((-2.0 * _GELU_K0 * _LOG2E) + (-2.0 * _GELU_K0 * _GELU_K1 * _LOG2E) * (x * x))))
    return x * s, s


def _gelu_grad(x, s):
    return s + x * s * (1.0 - s) * ((2.0 * _GELU_K0) + (6.0 * _GELU_K0 * _GELU_K1) * (x * x))


def _rstd(x):
    return lax.rsqrt(jnp.mean(x * x, axis=-1, keepdims=True) + EPS)


def _rms_bwd(x, gd, r):
    return r * gd - x * (r * r * r) * jnp.mean(gd * x, axis=-1, keepdims=True)


def _ln_fwd(x):
    mu = jnp.mean(x, axis=-1, keepdims=True)
    xc = x - mu
    rs = lax.rsqrt(jnp.mean(xc * xc, axis=-1, keepdims=True) + EPS)
    return xc * rs, rs


def _ln_bwd(dhat, hat, rs):
    return rs * (dhat - jnp.mean(dhat, axis=-1, keepdims=True) - hat * jnp.mean(dhat * hat, axis=-1, keepdims=True))


def _colsum(x):
    return jnp.sum(x, axis=0, keepdims=True)


def _accum(ref, first, val):
    @pl.when(first)
    def _():
        ref[...] = val

    @pl.when(jnp.logical_not(first))
    def _():
        ref[...] += val


def _sgu_mask(transposed):
    r = lax.broadcasted_iota(jnp.int32, (BLK, BLK), 0) // CHUNK
    c = lax.broadcasted_iota(jnp.int32, (BLK, BLK), 1) // CHUNK
    return (r <= c) if transposed else (c <= r)


def _inv_count(i, tm, w):
    t = lax.broadcasted_iota(jnp.int32, (tm, 1), 0) + i * tm
    return 1.0 / jnp.minimum(t + 1, w).astype(F32)


def _count(i, tm, w):
    t = lax.broadcasted_iota(jnp.int32, (tm, 1), 0) + i * tm
    return jnp.minimum(t + 1, w).astype(F32)


def _params(n_grid):
    return pltpu.CompilerParams(dimension_semantics=("arbitrary",) * n_grid, vmem_limit_bytes=VMEM_LIMIT)


def _sds(shape, dtype):
    return jax.ShapeDtypeStruct(shape, dtype)


def _cols(tm, width, cb):
    return pl.BlockSpec((tm, width), lambda i: (i, cb))


def _whole(shape):
    nd = len(shape)
    return pl.BlockSpec(shape, lambda i: (0,) * nd)


def _prev_halo(tm, width, cb):
    return pl.BlockSpec((HALO, width), lambda i: (jnp.maximum(i * (tm // HALO) - 1, 0), cb))


def _next_halo(tm, width, cb, s_len):
    last = s_len // HALO - 1
    return pl.BlockSpec((HALO, width), lambda i: (jnp.minimum((i + 1) * (tm // HALO), last), cb))


def _rowsharded(rows):
    return pl.BlockSpec((N_DEV, rows, D), lambda i: (0, 0, 0))


def _win_block(j):
    return pl.BlockSpec((None, D, D), lambda i: (j, 0, 0))


def _row_tile(rows, cap):
    t = min(rows, cap)
    while rows % t or t % 16:
        t -= 16
    return t


class Gather:
    def __init__(self, arrs):
        self.arrs = list(arrs)
        n = self.n = len(self.arrs)
        self.out_shape = [_sds((N_DEV,) + a.shape, a.dtype) for a in self.arrs]
        self.scratch = [pltpu.SemaphoreType.DMA((n, 7)), pltpu.SemaphoreType.DMA((n, 7)), pltpu.SemaphoreType.DMA((n,))]

    def _plan(self, ins, outs, sems):
        send, recv, local = sems
        x, y, c = lax.axis_index("x"), lax.axis_index("y"), lax.axis_index("c")
        me, sibling = (x, y, c), (x, y, 1 - c)
        chips = [(1 - x, y), (x, 1 - y), (1 - x, 1 - y)]

        def copy(a, k, block, to, src=None):
            dst = outs[a].at[4 * block[0] + 2 * block[1] + block[2]]
            return pltpu.make_async_remote_copy(
                src_ref=dst if src is None else src, dst_ref=dst, send_sem=send.at[a, k], recv_sem=recv.at[a, k],
                device_id=to, device_id_type=MESH)

        mine = [pltpu.make_async_copy(ins[a], outs[a].at[4 * x + 2 * y + c], local.at[a]) for a in range(self.n)]
        first = []
        for a in range(self.n):
            first.append(copy(a, 0, me, sibling, src=ins[a]))
            first += [copy(a, 1 + j, me, (*chip, c), src=ins[a]) for j, chip in enumerate(chips)]
        return me, sibling, chips, c, copy, mine, first

    def start(self, ins, outs, sems):
        *_, mine, first = self._plan(ins, outs, sems)
        for cp in mine + first:
            cp.start()

    def finish(self, ins, outs, sems):
        me, sibling, chips, c, copy, mine, first = self._plan(ins, outs, sems)
        passed = []
        for a in range(self.n):
            for j, chip in enumerate(chips):
                copy(a, 1 + j, (*chip, c), me).wait_recv()
                fwd = copy(a, 4 + j, (*chip, c), sibling)
                fwd.start()
                passed.append(fwd)
        for a in range(self.n):
            copy(a, 0, sibling, me).wait_recv()
            for j, chip in enumerate(chips):
                copy(a, 4 + j, (*chip, 1 - c), me).wait_recv()
        for cp in first + passed:
            cp.wait_send()
        for cp in mine:
            cp.wait()


class ToSibling:
    def __init__(self, parts):
        self.arrs = list(parts)
        n = self.n = len(self.arrs)
        self.out_shape = [_sds((4,) + p.shape[1:], p.dtype) for p in self.arrs]
        self.scratch = [pltpu.SemaphoreType.DMA((n,)), pltpu.SemaphoreType.DMA((n,))]

    def start(self, ins, outs, sems):
        send, recv = sems
        x, y, c = lax.axis_index("x"), lax.axis_index("y"), lax.axis_index("c")
        for a in range(self.n):
            for q in range(4):
                pltpu.make_async_remote_copy(
                    src_ref=ins[a].at[2 * q + 1 - c], dst_ref=outs[a].at[q], send_sem=send.at[a], recv_sem=recv.at[a],
                    device_id=(x, y, 1 - c), device_id_type=MESH).start()

    def finish(self, ins, outs, sems):
        send, recv = sems
        x, y, c = lax.axis_index("x"), lax.axis_index("y"), lax.axis_index("c")
        for a in range(self.n):
            pltpu.make_async_remote_copy(
                src_ref=outs[a], dst_ref=outs[a], send_sem=send.at[a], recv_sem=recv.at[a],
                device_id=(x, y, 1 - c), device_id_type=MESH).wait()


class ToChips:
    def __init__(self, cps):
        self.arrs = list(cps)
        n = self.n = len(self.arrs)
        self.out_shape = [_sds((3,) + p.shape[1:], p.dtype) for p in self.arrs]
        self.scratch = [pltpu.SemaphoreType.DMA((n,)), pltpu.SemaphoreType.DMA((n,))]

    def start(self, ins, outs, sems):
        send, recv = sems
        x, y, c = lax.axis_index("x"), lax.axis_index("y"), lax.axis_index("c")
        for a in range(self.n):
            for r, (px, py) in enumerate([(1 - x, y), (x, 1 - y), (1 - x, 1 - y)]):
                pltpu.make_async_remote_copy(
                    src_ref=ins[a].at[2 * px + py], dst_ref=outs[a].at[r], send_sem=send.at[a], recv_sem=recv.at[a],
                    device_id=(px, py, c), device_id_type=MESH).start()

    def finish(self, ins, outs, sems):
        send, recv = sems
        x, y, c = lax.axis_index("x"), lax.axis_index("y"), lax.axis_index("c")
        for a in range(self.n):
            pltpu.make_async_remote_copy(
                src_ref=outs[a], dst_ref=outs[a], send_sem=send.at[a], recv_sem=recv.at[a],
                device_id=(x, y, c), device_id_type=MESH).wait()


class Both:
    def __init__(self, a, b):
        self.a, self.b = a, b
        self.arrs, self.n = a.arrs + b.arrs, a.n + b.n
        self.out_shape, self.scratch = a.out_shape + b.out_shape, a.scratch + b.scratch

    def _each(self, ins, outs, sems):
        na, ns = self.a.n, len(self.a.scratch)
        return (self.a, (ins[:na], outs[:na], sems[:ns])), (self.b, (ins[na:], outs[na:], sems[ns:]))

    def start(self, ins, outs, sems):
        for comm, refs in self._each(ins, outs, sems):
            comm.start(*refs)

    def finish(self, ins, outs, sems):
        for comm, refs in self._each(ins, outs, sems):
            comm.finish(*refs)


def run_comm(comm, name):
    n = comm.n

    def body(*refs):
        ins, outs, sems = refs[:n], refs[n:2 * n], refs[2 * n:]
        comm.start(ins, outs, sems)
        comm.finish(ins, outs, sems)

    return pl.pallas_call(body, name=name, out_shape=comm.out_shape, in_specs=[ANY] * n, out_specs=[ANY] * n,
                          scratch_shapes=comm.scratch)(*comm.arrs)


def _pcall(body, args, *, name, grid, in_specs, out_specs, out_shape, scratch_shapes=(), comm=None):
    params = _params(len(grid))
    scratch_shapes = list(scratch_shapes)
    if comm is None:
        outs = pl.pallas_call(body, name=name, grid=grid, in_specs=in_specs, out_specs=out_specs, out_shape=out_shape,
                              scratch_shapes=scratch_shapes, compiler_params=params)(*args)
        return outs, None
    n_in, n_out, n_scr, nc = len(in_specs), len(out_specs), len(scratch_shapes), comm.n

    def hosted(*refs):
        ins, cins = refs[:n_in], refs[n_in:n_in + nc]
        o0 = n_in + nc
        outs, couts = refs[o0:o0 + n_out], refs[o0 + n_out:o0 + n_out + nc]
        s0 = o0 + n_out + nc
        scr, csems = refs[s0:s0 + n_scr], refs[s0 + n_scr:]
        first = pl.program_id(0) == 0
        last = pl.program_id(0) == grid[0] - 1
        for ax in range(1, len(grid)):
            first = jnp.logical_and(first, pl.program_id(ax) == 0)
            last = jnp.logical_and(last, pl.program_id(ax) == grid[ax] - 1)

        @pl.when(first)
        def _():
            comm.start(cins, couts, csems)
        body(*ins, *outs, *scr)

        @pl.when(last)
        def _():
            comm.finish(cins, couts, csems)

    res = pl.pallas_call(
        hosted, name=name, grid=grid, in_specs=list(in_specs) + [ANY] * nc, out_specs=list(out_specs) + [ANY] * nc,
        out_shape=list(out_shape) + comm.out_shape, scratch_shapes=scratch_shapes + comm.scratch,
        compiler_params=params)(*args, *comm.arrs)
    return res[:n_out], res[n_out:]


def add_core_side(parts, from_sibling, core, name):
    _, rows, cols = parts.shape
    tr = _row_tile(rows, 512)

    def body(c_ref, a_ref, b_ref, o_ref):
        o_ref[...] = (a_ref[...].astype(F32) + b_ref[...].astype(F32)).astype(o_ref.dtype)

    side = pl.BlockSpec((None, tr, cols), lambda q, i, c: (q, i, 0))
    return pl.pallas_call(
        body, name=name,
        grid_spec=pltpu.PrefetchScalarGridSpec(
            num_scalar_prefetch=1, grid=(4, rows // tr),
            in_specs=[pl.BlockSpec((None, tr, cols), lambda q, i, c: (2 * q + c[0], i, 0)), side], out_specs=side),
        out_shape=_sds(from_sibling.shape, BF), compiler_params=_params(2))(core, parts, from_sibling)


def adamw(w, m, v, pieces, name):
    rows, cols = w.shape
    tr = _row_tile(rows, 256) if rows % 16 == 0 else rows
    np_ = len(pieces)
    c1 = 1.0 / (1.0 - ADAM_B1 ** ADAM_STEP)
    c2 = 1.0 / (1.0 - ADAM_B2 ** ADAM_STEP)

    def body(*refs):
        w_ref, m_ref, v_ref = refs[:3]
        p_refs = refs[3:3 + np_]
        g_ref, d_ref, nm_ref, nv_ref = refs[3 + np_:]
        g = p_refs[0][...].astype(F32)
        for pr in p_refs[1:]:
            g = g + pr[...].astype(F32)
        nm = ADAM_B1 * m_ref[...] + (1.0 - ADAM_B1) * g
        nv = ADAM_B2 * v_ref[...] + (1.0 - ADAM_B2) * (g * g)
        g_ref[...] = g
        nm_ref[...] = nm
        nv_ref[...] = nv
        d_ref[...] = -ADAM_LR * ((nm * c1) / (jnp.sqrt(nv * c2) + ADAM_EPS) + ADAM_WD * w_ref[...])

    spec = pl.BlockSpec((tr, cols), lambda i: (i, 0))
    p_specs = []
    for arr, k in pieces:
        if k is None:
            p_specs.append(spec)
        else:
            p_specs.append(pl.BlockSpec((None, tr, cols), functools.partial(lambda i, kk: (kk, i, 0), kk=k)))
    out = _sds(w.shape, F32)
    return pl.pallas_call(body, name=name, grid=(rows // tr,), in_specs=[spec] * 3 + p_specs, out_specs=[spec] * 4,
                          out_shape=[out] * 4, compiler_params=_params(1))(w, m, v, *[a for a, _ in pieces])


def adamw_layers(w, m, v, pieces, name):
    _, rows, cols = w.shape
    tr = _row_tile(rows, 256)
    nt = rows // tr
    counts = [len(pieces[0]), len(pieces[1])]
    c1 = 1.0 / (1.0 - ADAM_B1 ** ADAM_STEP)
    c2 = 1.0 / (1.0 - ADAM_B2 ** ADAM_STEP)

    def body(*refs):
        w_ref, m_ref, v_ref = refs[:3]
        p_refs = refs[3:3 + sum(counts)]
        g_ref, d_ref, nm_ref, nv_ref = refs[3 + sum(counts):]
        sums = []
        for group in (p_refs[:counts[0]], p_refs[counts[0]:]):
            s = group[0][...].astype(F32)
            for pr in group[1:]:
                s = s + pr[...].astype(F32)
            sums.append(s)
        g = jnp.where(pl.program_id(0) == 0, sums[0], sums[1])
        nm = ADAM_B1 * m_ref[...] + (1.0 - ADAM_B1) * g
        nv = ADAM_B2 * v_ref[...] + (1.0 - ADAM_B2) * (g * g)
        g_ref[...] = g
        nm_ref[...] = nm
        nv_ref[...] = nv
        d_ref[...] = -ADAM_LR * ((nm * c1) / (jnp.sqrt(nv * c2) + ADAM_EPS) + ADAM_WD * w_ref[...])

    def rows_of(layer):
        parked = nt - 1 if layer == 0 else 0
        return lambda l, i: jnp.where(l == layer, i, parked)

    spec = pl.BlockSpec((None, tr, cols), lambda l, i: (l, i, 0))
    p_specs, p_args = [], []
    for layer in (0, 1):
        row_of = rows_of(layer)
        for arr, k in pieces[layer]:
            p_args.append(arr)
            if k is None:
                p_specs.append(pl.BlockSpec((tr, cols), functools.partial(lambda l, i, f: (f(l, i), 0), f=row_of)))
            else:
                p_specs.append(pl.BlockSpec((None, tr, cols), functools.partial(lambda l, i, f, kk: (kk, f(l, i), 0), f=row_of, kk=k)))
    out = _sds(w.shape, F32)
    return pl.pallas_call(body, name=name, grid=(2, nt), in_specs=[spec] * 3 + p_specs, out_specs=[spec] * 4,
                          out_shape=[out] * 4, compiler_params=_params(2))(w, m, v, *p_args)


def sum_slabs(g, name):
    n, rows, cols = g.shape

    def body(g_ref, o_ref):
        s = g_ref[0]
        for k in range(1, n):
            s = s + g_ref[k]
        o_ref[...] = s

    return pl.pallas_call(body, name=name, out_shape=_sds((rows, cols), F32))(g)


def norm_proj(h, g, w, name, transposed=False, comm=None):
    s_len = h.shape[0]
    nb, tn = (w.shape[0], w.shape[1]) if transposed else (w.shape[0], w.shape[2])
    tm = min(TMP, s_len)
    nt = s_len // tm
    matmul = _dot_nt if transposed else _dot

    def body(h_ref, g_ref, w_ref, o_ref, hn_ref, hn_s):
        rows = pl.ds(pl.multiple_of(pl.program_id(1) * tm, tm), tm)

        @pl.when(pl.program_id(0) == 0)
        def _():
            x = h_ref[...]
            hn = (x * _rstd(x) * g_ref[...]).astype(BF)
            hn_s[rows, :] = hn
            hn_ref[...] = hn
        o_ref[...] = matmul(hn_s[rows, :], w_ref[...]).astype(BF)

    def first_pass_rows(j, i):
        return (jnp.where(j == 0, i, nt - 1), 0)

    return _pcall(
        body, (h, g, w), name=name, grid=(nb, nt),
        in_specs=[pl.BlockSpec((tm, D), first_pass_rows), pl.BlockSpec((1, D), lambda j, i: (0, 0)),
                  pl.BlockSpec((None,) + w.shape[1:], lambda j, i: (j, 0, 0))],
        out_specs=[pl.BlockSpec((tm, tn), lambda j, i: (i, j)), pl.BlockSpec((tm, D), first_pass_rows)],
        out_shape=[_sds((s_len, nb * tn), BF), _sds((s_len, D), BF)],
        scratch_shapes=[pltpu.VMEM((s_len, D), BF)], comm=comm)


def _sgu_mix(ws_ref, vln_s, mix_s, bs_ref, tm, transposed):
    mask = _sgu_mask(transposed)
    for hd in range(HEADS):
        wm = jnp.where(mask, ws_ref[hd], 0.0).astype(BF)
        cs = slice(hd * BLK, (hd + 1) * BLK)
        for n in range(tm // BLK):
            rs = slice(n * BLK, (n + 1) * BLK)
            r = _dot(wm, vln_s[rs, cs])
            mix_s[rs, cs] = r if bs_ref is None else r + bs_ref[:, cs]


def sgu_fwd(proj, ws, bsfull, gv, bv, wo, name):
    s_len = proj.shape[0]
    tm = min(TM, s_len)

    def body(zu_ref, zv_ref, ws_ref, bs_ref, gv_ref, bv_ref, wo_ref, mix_ref, sgu_ref, br_ref, vln_s, mix_s):
        u, _ = _gelu(zu_ref[...].astype(F32))
        v, _ = _gelu(zv_ref[...].astype(F32))
        vhat, _ = _ln_fwd(v)
        vln_s[...] = (vhat * gv_ref[...] + bv_ref[...]).astype(BF)
        _sgu_mix(ws_ref, vln_s, mix_s, bs_ref, tm, False)
        mixed = mix_s[...].astype(BF)
        mix_ref[...] = mixed
        sgu = (u * mixed.astype(F32)).astype(BF)
        sgu_ref[...] = sgu
        br_ref[...] = _dot(sgu, wo_ref[...].reshape(D, D)).astype(BF)

    return pl.pallas_call(
        body, name=name, grid=(s_len // tm,),
        in_specs=[_cols(tm, D, 0), _cols(tm, D, 1), _whole((HEADS, BLK, BLK)), _whole((BLK, D)), _whole((1, D)), _whole((1, D)),
                  _rowsharded(BLK)],
        out_specs=[_cols(tm, D, 0)] * 3, out_shape=[_sds((s_len, D), BF)] * 3,
        scratch_shapes=[pltpu.VMEM((tm, D), BF), pltpu.VMEM((tm, D), F32)], compiler_params=_params(1),
    )(proj, proj, ws, bsfull, gv, bv, wo)


def _causal_conv(ext_s, out_s, wdw_ref, bias_ref, tm):
    def chunk(ci, carry):
        r0 = pl.multiple_of((ci // (D // LC)) * RC, RC)
        l0 = pl.multiple_of((ci % (D // LC)) * LC, LC)
        win = ext_s[pl.ds(r0, RC + HALO), pl.ds(l0, LC)]
        acc = jnp.broadcast_to(bias_ref[:, pl.ds(l0, LC)], (RC, LC))
        for r in range(8):
            wr = win if r == 0 else pltpu.roll(win, r, 0)
            for m in range(4):
                d = 8 * m + r
                if d < CONV_W:
                    k = CONV_W - 1 - d
                    acc = acc + wdw_ref[k:k + 1, pl.ds(l0, LC)] * wr[HALO - 8 * m:HALO - 8 * m + RC]
        out_s[pl.ds(r0, RC), pl.ds(l0, LC)] = acc
        return carry
    lax.fori_loop(0, (tm // RC) * (D // LC), chunk, 0)


def _glu_ext(a_ref, g_ref, ah_ref, gh_ref, ext_s, first):
    hh = ah_ref[...].astype(F32) * _sig(gh_ref[...].astype(F32))
    ext_s[0:HALO, :] = jnp.where(first, 0.0, hh)
    ext_s[HALO:, :] = a_ref[...].astype(F32) * _sig(g_ref[...].astype(F32))


def conv_fwd(proj, wdw, bdw, gln, bln, wo, name, comm=None):
    s_len = proj.shape[0]
    tm = min(TM, s_len)

    def body(a_ref, g_ref, ah_ref, gh_ref, wdw_ref, bdw_ref, gln_ref, bln_ref, wo_ref, cv_ref, cb_ref, br_ref, ext_s, conv_s):
        _glu_ext(a_ref, g_ref, ah_ref, gh_ref, ext_s, pl.program_id(0) == 0)
        _causal_conv(ext_s, conv_s, wdw_ref, bdw_ref, tm)
        cv = conv_s[...].astype(BF)
        cv_ref[...] = cv
        chat, _ = _ln_fwd(cv.astype(F32))
        yl = chat * gln_ref[...] + bln_ref[...]
        cb = (yl * _sig(yl)).astype(BF)
        cb_ref[...] = cb
        br_ref[...] = _dot(cb, wo_ref[...].reshape(D, D)).astype(BF)

    return _pcall(
        body, (proj, proj, proj, proj, wdw, bdw, gln, bln, wo), name=name, grid=(s_len // tm,),
        in_specs=[_cols(tm, D, 2), _cols(tm, D, 3), _prev_halo(tm, D, 2), _prev_halo(tm, D, 3), _whole((HALO, D)),
                  _whole((1, D)), _whole((1, D)), _whole((1, D)), _rowsharded(BLK)],
        out_specs=[_cols(tm, D, 0)] * 3, out_shape=[_sds((s_len, D), BF)] * 3,
        scratch_shapes=[pltpu.VMEM((tm + HALO, D), F32), pltpu.VMEM((tm, D), F32)], comm=comm)


def pool_fwd(proj, wpool, spool, wo, name):
    s_len = proj.shape[0]
    tm = min(TM, s_len)

    def body(z_ref, zh_ref, wp_ref, sp_ref, wo_ref, pooled_ref, pm_ref, br_ref, ext_s, mr_s):
        i = pl.program_id(0)
        ext_s[0:HALO, :] = jnp.where(i == 0, 0.0, zh_ref[...].astype(F32))
        ext_s[HALO:, :] = z_ref[...].astype(F32)
        for gi, w in enumerate(POOL_WINDOWS):
            cs = slice(gi * POOL_GD, (gi + 1) * POOL_GD)
            e = ext_s[:, cs]
            s = e
            sh = 1
            while sh < w:
                s = s + pltpu.roll(s, sh, 0)
                sh *= 2
            pooled = (s[HALO:] * _inv_count(i, tm, w) - e[HALO:]).astype(BF)
            pooled_ref[:, cs] = pooled
            mr_s[:, cs] = _dot(pooled, wp_ref[gi])
        pm = (mr_s[...] * sp_ref[...]).astype(BF)
        pm_ref[...] = pm
        br_ref[...] = _dot(pm, wo_ref[...].reshape(D, D)).astype(BF)

    return pl.pallas_call(
        body, name=name, grid=(s_len // tm,),
        in_specs=[_cols(tm, D, 4), _prev_halo(tm, D, 4), _whole((4, POOL_GD, POOL_GD)), _whole((1, D)), _rowsharded(BLK)],
        out_specs=[_cols(tm, D, 0)] * 3, out_shape=[_sds((s_len, D), BF)] * 3,
        scratch_shapes=[pltpu.VMEM((tm + HALO, D), F32), pltpu.VMEM((tm, D), F32)], compiler_params=_params(1),
    )(proj, proj, wpool, spool, wo)


def merge_out(proj, bra, brb, brc, h, wout, gpost, name, comm=None):
    s_len = h.shape[0]
    tm = min(TM, s_len)

    def body(z0, z1, z2, a_ref, b_ref, c_ref, h_ref, wo_ref, g_ref, mg_ref, mo_ref, h1_ref):
        merged = (_sig(z0[...].astype(F32)) * a_ref[...].astype(F32) + _sig(z1[...].astype(F32)) * b_ref[...].astype(F32)
                  + _sig(z2[...].astype(F32)) * c_ref[...].astype(F32)).astype(BF)
        mg_ref[...] = merged
        mo = _dot(merged, wo_ref[...].reshape(D, D))
        mo_ref[...] = mo.astype(BF)
        h1_ref[...] = h_ref[...] + mo * _rstd(mo) * g_ref[...]

    row = _cols(tm, D, 0)
    return _pcall(
        body, (proj, proj, proj, bra, brb, brc, h, wout, gpost), name=name, grid=(s_len // tm,),
        in_specs=[_cols(tm, D, 5), _cols(tm, D, 6), _cols(tm, D, 7), row, row, row, row, _rowsharded(BLK), _whole((1, D))],
        out_specs=[row] * 3, out_shape=[_sds((s_len, D), BF), _sds((s_len, D), BF), _sds((s_len, D), F32)], comm=comm)


def _p_spec(tm, layer):
    return pl.BlockSpec((None, None, tm, PLE), lambda i: (layer, 0, i, 0))


def ffn_out(ff, h1, p, wfo, gpost, wpg, wple, layer, name, comm=None):
    s_len = h1.shape[0]
    tm = min(TMB, s_len)

    def body(fg_ref, fu_ref, h1_ref, p_ref, wfo_ref, g_ref, wpg_ref, wple_ref, act_ref, f_ref, h2_ref, pg_ref, h3_ref):
        gt = fg_ref[...].astype(F32)
        act = (gt * _sig(gt) * fu_ref[...].astype(F32)).astype(BF)
        act_ref[...] = act
        f = _dot(act, wfo_ref[...].reshape(D_FF, D))
        f_ref[...] = f.astype(BF)
        h2 = h1_ref[...] + f * _rstd(f) * g_ref[...]
        h2_ref[...] = h2
        pg = _dot(h2.astype(BF), wpg_ref[...].reshape(D, D)).astype(BF)
        pg_ref[...] = pg
        pe = _dot(p_ref[...].astype(BF), wple_ref[...])
        h3_ref[...] = h2 + _sig(pg.astype(F32)) * pe

    row = _cols(tm, D, 0)
    return _pcall(
        body, (ff, ff, h1, p, wfo, gpost, wpg, wple), name=name, grid=(s_len // tm,),
        in_specs=[_cols(tm, D_FF, 0), _cols(tm, D_FF, 1), row, _p_spec(tm, layer), _rowsharded(D_FF // N_DEV),
                  _whole((1, D)), _rowsharded(BLK), _whole((PLE, D))],
        out_specs=[_cols(tm, D_FF, 0), row, row, row, row],
        out_shape=[_sds((s_len, D_FF), BF), _sds((s_len, D), BF), _sds((s_len, D), F32), _sds((s_len, D), BF), _sds((s_len, D), F32)],
        comm=comm)


def ple_ffn_bwd(dh3, pg, p, f, ff, wple, wpg, wfo, gpost, layer, name, target=None, comm=None):
    s_len = dh3.shape[0]
    tm = min(TMB, s_len)
    nt = s_len // tm
    with_loss = target is not None

    def body(*refs):
        if with_loss:
            t_ref, refs, loss_ref, loss_acc = refs[0], refs[1:-2], refs[-2], refs[-1]
        (dh3_ref, pg_ref, p_ref, f_ref, fg_ref, fu_ref, wple_ref, wpg_ref, wfo_ref, g_ref,
         dh2_ref, dpe_ref, dpg_ref, df_ref, dff_ref, dg_ref) = refs
        i = pl.program_id(0)
        dh3v = dh3_ref[...]
        if with_loss:
            err = dh3v - t_ref[...]
            dh3v = err * (1.0 / D)
            _accum(loss_acc, i == 0, _colsum(err * err))

            @pl.when(i == nt - 1)
            def _():
                loss_ref[...] = jnp.broadcast_to(jnp.sum(loss_acc[...], axis=1, keepdims=True) * (0.5 / D), (1, LC))
        s = _sig(pg_ref[...].astype(F32))
        pe = _dot(p_ref[...].astype(BF), wple_ref[...])
        dpe_ref[...] = (dh3v * s).astype(BF)
        dpg = (dh3v * pe * s * (1.0 - s)).astype(BF)
        dpg_ref[...] = dpg
        dh2 = dh3v + _dot_nt(dpg, wpg_ref[...].reshape(D, D))
        dh2_ref[...] = dh2
        fv = f_ref[...].astype(F32)
        r = _rstd(fv)
        _accum(dg_ref, i == 0, _colsum(dh2 * fv * r))
        df = _rms_bwd(fv, dh2 * g_ref[...], r).astype(BF)
        df_ref[...] = df
        dact = _dot_nt(df, wfo_ref[...].reshape(D_FF, D))
        gt = fg_ref[...].astype(F32)
        sg = _sig(gt)
        up = fu_ref[...].astype(F32)
        dff_ref[:, 0:D_FF] = (dact * up * sg * (1.0 + gt * (1.0 - sg))).astype(BF)
        dff_ref[:, D_FF:2 * D_FF] = (dact * gt * sg).astype(BF)

    row = _cols(tm, D, 0)
    args = (dh3, pg, p, f, ff, ff, wple, wpg, wfo, gpost)
    in_specs = [row, row, _p_spec(tm, layer), row, _cols(tm, D_FF, 0), _cols(tm, D_FF, 1), _whole((PLE, D)),
                _rowsharded(BLK), _rowsharded(D_FF // N_DEV), _whole((1, D))]
    out_specs = [row, row, row, row, _cols(tm, 2 * D_FF, 0), _whole((1, D))]
    out_shape = [_sds((s_len, D), F32), _sds((s_len, D), BF), _sds((s_len, D), BF), _sds((s_len, D), BF),
                 _sds((s_len, 2 * D_FF), BF), _sds((1, D), F32)]
    scratch = []
    if with_loss:
        args, in_specs = (target,) + args, [row] + in_specs
        out_specs, out_shape = out_specs + [_whole((1, LC))], out_shape + [_sds((1, LC), F32)]
        scratch = [pltpu.VMEM((1, D), F32)]
    return _pcall(body, args, name=name, grid=(nt,), in_specs=in_specs, out_specs=out_specs, out_shape=out_shape,
                  scratch_shapes=scratch, comm=comm)


def ffn_in_bwd(dff, wt, h1, dh2, gpre, name, comm=None):
    s_len = h1.shape[0]
    tm = min(TMB, s_len)
    nb, tn, _ = wt.shape

    def body(dff_ref, w_ref, h1_ref, dh2_ref, g_ref, dh1_ref, dg_ref):
        dhn = _dot(dff_ref[:, 0:tn], w_ref[0])
        for j in range(1, nb):
            dhn = dhn + _dot(dff_ref[:, j * tn:(j + 1) * tn], w_ref[j])
        x = h1_ref[...]
        r = _rstd(x)
        _accum(dg_ref, pl.program_id(0) == 0, _colsum(dhn * x * r))
        dh1_ref[...] = dh2_ref[...] + _rms_bwd(x, dhn * g_ref[...], r)

    row = _cols(tm, D, 0)
    return _pcall(
        body, (dff, wt, h1, dh2, gpre), name=name, grid=(s_len // tm,),
        in_specs=[_cols(tm, nb * tn, 0), pl.BlockSpec((nb, tn, D), lambda i: (0, 0, 0), pipeline_mode=pl.Buffered(1)), row, row,
                  _whole((1, D))],
        out_specs=[row, _whole((1, D))],
        out_shape=[_sds((s_len, D), F32), _sds((1, D), F32)], comm=comm)


def mix_post_bwd(dh1, mo, gpost, wout, proj, bra, brb, brc, win, name, comm=None):
    s_len = dh1.shape[0]
    tm = min(TMB, s_len)

    def body(dh1_ref, mo_ref, g_ref, wo_ref, z0, z1, z2, a_ref, b_ref, c_ref, w5, w6, w7,
             dmo_ref, da_ref, db_ref, dc_ref, dz_ref, dhn_ref, dg_ref):
        i = pl.program_id(0)
        dh1v = dh1_ref[...]
        mo_v = mo_ref[...].astype(F32)
        r = _rstd(mo_v)
        _accum(dg_ref, i == 0, _colsum(dh1v * mo_v * r))
        dmo = _rms_bwd(mo_v, dh1v * g_ref[...], r).astype(BF)
        dmo_ref[...] = dmo
        dmerged = _dot_nt(dmo, wo_ref[...].reshape(D, D))
        dhn = jnp.zeros((tm, D), F32)
        for k, (z, br, dbr, w) in enumerate(((z0, a_ref, da_ref, w5), (z1, b_ref, db_ref, w6), (z2, c_ref, dc_ref, w7))):
            s = _sig(z[...].astype(F32))
            dbr[...] = (dmerged * s).astype(BF)
            dz = (dmerged * br[...].astype(F32) * s * (1.0 - s)).astype(BF)
            dz_ref[:, k * D:(k + 1) * D] = dz
            dhn = dhn + _dot_nt(dz, w[...])
        dhn_ref[...] = dhn

    row = _cols(tm, D, 0)
    return _pcall(
        body, (dh1, mo, gpost, wout, proj, proj, proj, bra, brb, brc, win, win, win), name=name, grid=(s_len // tm,),
        in_specs=[row, row, _whole((1, D)), _rowsharded(BLK), _cols(tm, D, 5), _cols(tm, D, 6), _cols(tm, D, 7), row, row, row,
                  _win_block(5), _win_block(6), _win_block(7)],
        out_specs=[row, row, row, row, _cols(tm, 3 * D, 0), row, _whole((1, D))],
        out_shape=[_sds((s_len, D), BF)] * 4 + [_sds((s_len, 3 * D), BF), _sds((s_len, D), F32), _sds((1, D), F32)], comm=comm)


def sgu_bwd(dbr, wo, proj, mixed, wst, gv, bv, win, dhn_in, name, comm=None):
    s_len = dbr.shape[0]
    tm = min(TMB, s_len)
    nt = s_len // tm

    def body(dbr_ref, wo_ref, zu_ref, zv_ref, mix_ref, wst_ref, gv_ref, bv_ref, w0, w1, dhn_in_ref,
             dz_ref, dhn_ref, dws_ref, dbs_ref, dgv_ref, dbv_ref, vln_s, dmix_s, dvln_s, bs_acc):
        i = pl.program_id(0)
        first = i == 0
        dsgu = _dot_nt(dbr_ref[...], wo_ref[...].reshape(D, D))
        zu = zu_ref[...].astype(F32)
        zv = zv_ref[...].astype(F32)
        u, tu = _gelu(zu)
        v, tv = _gelu(zv)
        vhat, rs = _ln_fwd(v)
        vln_s[...] = (vhat * gv_ref[...] + bv_ref[...]).astype(BF)
        du = dsgu * mix_ref[...].astype(F32)
        dmix = dsgu * u
        dmix_s[...] = dmix.astype(BF)
        blocks = dmix[0:BLK]
        for n in range(1, tm // BLK):
            blocks = blocks + dmix[n * BLK:(n + 1) * BLK]
        _accum(bs_acc, first, blocks)
        for hd in range(HEADS):
            cs = slice(hd * BLK, (hd + 1) * BLK)
            g = _dot_nt(dmix_s[0:BLK, cs], vln_s[0:BLK, cs])
            for n in range(1, tm // BLK):
                g = g + _dot_nt(dmix_s[n * BLK:(n + 1) * BLK, cs], vln_s[n * BLK:(n + 1) * BLK, cs])

            @pl.when(first)
            def _():
                dws_ref[hd] = g

            @pl.when(jnp.logical_not(first))
            def _():
                dws_ref[hd] += g
        _sgu_mix(wst_ref, dmix_s, dvln_s, None, tm, True)
        dvln = dvln_s[...]
        _accum(dgv_ref, first, _colsum(dvln * vhat))
        _accum(dbv_ref, first, _colsum(dvln))
        dv = _ln_bwd(dvln * gv_ref[...], vhat, rs)
        dzu = (du * _gelu_grad(zu, tu)).astype(BF)
        dzv = (dv * _gelu_grad(zv, tv)).astype(BF)
        dz_ref[:, 0:D] = dzu
        dz_ref[:, D:2 * D] = dzv
        dhn_ref[...] = dhn_in_ref[...] + _dot_nt(dzu, w0[...]) + _dot_nt(dzv, w1[...])

        @pl.when(i == nt - 1)
        def _():
            mask = _sgu_mask(False)
            for hd in range(HEADS):
                dws_ref[hd] = jnp.where(mask, dws_ref[hd], 0.0)
                dbs_ref[:, hd:hd + 1] = jnp.sum(bs_acc[:, hd * BLK:(hd + 1) * BLK], axis=1, keepdims=True)

    row = _cols(tm, D, 0)
    vec = _whole((1, D))
    return _pcall(
        body, (dbr, wo, proj, proj, mixed, wst, gv, bv, win, win, dhn_in), name=name, grid=(nt,),
        in_specs=[row, _rowsharded(BLK), _cols(tm, D, 0), _cols(tm, D, 1), row, _whole((HEADS, BLK, BLK)),
                  vec, vec, _win_block(0), _win_block(1), row],
        out_specs=[_cols(tm, 2 * D, 0), row, _whole((HEADS, BLK, BLK)), _whole((BLK, HEADS)), vec, vec],
        out_shape=[_sds((s_len, 2 * D), BF), _sds((s_len, D), F32), _sds((HEADS, BLK, BLK), F32), _sds((BLK, HEADS), F32),
                   _sds((1, D), F32), _sds((1, D), F32)],
        scratch_shapes=[pltpu.VMEM((tm, D), BF), pltpu.VMEM((tm, D), BF), pltpu.VMEM((tm, D), F32), pltpu.VMEM((BLK, D), F32)],
        comm=comm)


def conv_bwd(dbr, wo, proj, conv, gln, bln, name, comm=None):
    s_len = dbr.shape[0]
    tm = min(TMB, s_len)
    nt = s_len // tm

    def body(dbr_ref, wo_ref, a_ref, g_ref, ah_ref, gh_ref, cv_ref, gln_ref, bln_ref,
             dc_ref, dw_ref, dbdw_ref, dgln_ref, dbln_ref, ext_s, dc_s, dw_acc):
        i = pl.program_id(0)
        first = i == 0
        dcb = _dot_nt(dbr_ref[...], wo_ref[...].reshape(D, D))
        _glu_ext(a_ref, g_ref, ah_ref, gh_ref, ext_s, first)
        chat, rs = _ln_fwd(cv_ref[...].astype(F32))
        yl = chat * gln_ref[...] + bln_ref[...]
        sy = _sig(yl)
        dyl = dcb * sy * (1.0 + yl * (1.0 - sy))
        _accum(dgln_ref, first, _colsum(dyl * chat))
        _accum(dbln_ref, first, _colsum(dyl))
        dc = _ln_bwd(dyl * gln_ref[...], chat, rs)
        _accum(dbdw_ref, first, _colsum(dc))
        dc_ref[...] = dc.astype(BF)
        dc_s[...] = dc

        @pl.when(first)
        def _():
            dw_acc[...] = jnp.zeros_like(dw_acc)

        def chunk(ci, carry):
            r0 = pl.multiple_of((ci // (D // LC)) * RC, RC)
            l0 = pl.multiple_of((ci % (D // LC)) * LC, LC)
            win = ext_s[pl.ds(r0, RC + HALO), pl.ds(l0, LC)]
            dcw = dc_s[pl.ds(r0, RC), pl.ds(l0, LC)]
            for r in range(8):
                wr = win if r == 0 else pltpu.roll(win, r, 0)
                for m in range(4):
                    d = 8 * m + r
                    if d < CONV_W:
                        k = CONV_W - 1 - d
                        prod = dcw * wr[HALO - 8 * m:HALO - 8 * m + RC]
                        dw_acc[k * 8:(k + 1) * 8, pl.ds(l0, LC)] += prod.reshape(RC // 8, 8, LC).sum(axis=0)
            return carry
        lax.fori_loop(0, (tm // RC) * (D // LC), chunk, 0)

        @pl.when(i == nt - 1)
        def _():
            dw_ref[...] = dw_acc[...].reshape(HALO, 8, D).sum(axis=1)

    row = _cols(tm, D, 0)
    vec = _whole((1, D))
    return _pcall(
        body, (dbr, wo, proj, proj, proj, proj, conv, gln, bln), name=name, grid=(nt,),
        in_specs=[row, _rowsharded(BLK), _cols(tm, D, 2), _cols(tm, D, 3), _prev_halo(tm, D, 2), _prev_halo(tm, D, 3),
                  row, vec, vec],
        out_specs=[row, _whole((HALO, D)), vec, vec, vec],
        out_shape=[_sds((s_len, D), BF), _sds((HALO, D), F32), _sds((1, D), F32), _sds((1, D), F32), _sds((1, D), F32)],
        scratch_shapes=[pltpu.VMEM((tm + HALO, D), F32), pltpu.VMEM((tm, D), F32), pltpu.VMEM((HALO * 8, D), F32)], comm=comm)


def pool_bwd(dbr, wo, pooled, wpool, spool, name):
    s_len = dbr.shape[0]
    tm = min(TMB, s_len)

    def body(dbr_ref, wo_ref, pl_ref, wp_ref, sp_ref, dmr_ref, q_ref, dsp_ref, mr_s):
        i = pl.program_id(0)
        dpm = _dot_nt(dbr_ref[...], wo_ref[...].reshape(D, D))
        for gi in range(4):
            cs = slice(gi * POOL_GD, (gi + 1) * POOL_GD)
            mr_s[:, cs] = _dot(pl_ref[:, cs], wp_ref[gi])
        _accum(dsp_ref, i == 0, _colsum(dpm * mr_s[...]))
        dmr = (dpm * sp_ref[...]).astype(BF)
        dmr_ref[...] = dmr
        for gi, w in enumerate(POOL_WINDOWS):
            cs = slice(gi * POOL_GD, (gi + 1) * POOL_GD)
            q_ref[:, cs] = (_dot_nt(dmr[:, cs], wp_ref[gi]) * _inv_count(i, tm, w)).astype(BF)

    row = _cols(tm, D, 0)
    return pl.pallas_call(
        body, name=name, grid=(s_len // tm,),
        in_specs=[row, _rowsharded(BLK), row, _whole((4, POOL_GD, POOL_GD)), _whole((1, D))],
        out_specs=[row, row, _whole((1, D))],
        out_shape=[_sds((s_len, D), BF), _sds((s_len, D), BF), _sds((1, D), F32)],
        scratch_shapes=[pltpu.VMEM((tm, D), F32)], compiler_params=_params(1))(dbr, wo, pooled, wpool, spool)


def seq_bwd(dc, q, proj, wdw, win, dhn_in, h, dh1, gpre, name, comm=None):
    s_len = dc.shape[0]
    tm = min(TMB, s_len)
    nt = s_len // tm

    def body(dc_ref, dch_ref, q_ref, qh_ref, a_ref, g_ref, wdw_ref, w2, w3, w4, dhn_in_ref, h_ref, dh1_ref, gpre_ref,
             dz_ref, dh_ref, dgpre_ref, ext_s, dhc_s, qext_s):
        i = pl.program_id(0)
        last = i == nt - 1
        ext_s[0:tm, :] = dc_ref[...].astype(F32)
        ext_s[tm:, :] = jnp.where(last, 0.0, dch_ref[...].astype(F32))

        def chunk(ci, carry):
            r0 = pl.multiple_of((ci // (D // LC)) * RC, RC)
            l0 = pl.multiple_of((ci % (D // LC)) * LC, LC)
            win_ = ext_s[pl.ds(r0, RC + HALO), pl.ds(l0, LC)]
            acc = jnp.zeros((RC, LC), F32)
            for r in range(8):
                wr = win_ if r == 0 else pltpu.roll(win_, RC + HALO - r, 0)
                for m in range(4):
                    d = 8 * m + r
                    if d < CONV_W:
                        k = CONV_W - 1 - d
                        acc = acc + wdw_ref[k:k + 1, pl.ds(l0, LC)] * wr[8 * m:8 * m + RC]
            dhc_s[pl.ds(r0, RC), pl.ds(l0, LC)] = acc
            return carry
        lax.fori_loop(0, (tm // RC) * (D // LC), chunk, 0)

        dhc = dhc_s[...]
        av = a_ref[...].astype(F32)
        sg = _sig(g_ref[...].astype(F32))
        da = (dhc * sg).astype(BF)
        dg = (dhc * av * sg * (1.0 - sg)).astype(BF)
        dz_ref[:, 0:D] = da
        dz_ref[:, D:2 * D] = dg

        qext_s[0:tm, :] = q_ref[...].astype(F32)
        qext_s[tm:, :] = jnp.where(last, 0.0, qh_ref[...].astype(F32))
        for gi, w in enumerate(POOL_WINDOWS):
            cs = slice(gi * POOL_GD, (gi + 1) * POOL_GD)
            e = qext_s[:, cs]
            s = e
            sh = 1
            while sh < w:
                s = s + pltpu.roll(s, tm + HALO - sh, 0)
                sh *= 2
            dz_ref[:, 2 * D + gi * POOL_GD:2 * D + (gi + 1) * POOL_GD] = (s[0:tm] - e[0:tm] * _count(i, tm, w)).astype(BF)
        dhn = dhn_in_ref[...] + _dot_nt(da, w2[...]) + _dot_nt(dg, w3[...]) + _dot_nt(dz_ref[:, 2 * D:3 * D], w4[...])
        x = h_ref[...]
        r = _rstd(x)
        _accum(dgpre_ref, i == 0, _colsum(dhn * x * r))
        dh_ref[...] = dh1_ref[...] + _rms_bwd(x, dhn * gpre_ref[...], r)

    row = _cols(tm, D, 0)
    return _pcall(
        body, (dc, dc, q, q, proj, proj, wdw, win, win, win, dhn_in, h, dh1, gpre), name=name, grid=(nt,),
        in_specs=[row, _next_halo(tm, D, 0, s_len), row, _next_halo(tm, D, 0, s_len), _cols(tm, D, 2), _cols(tm, D, 3),
                  _whole((HALO, D)), _win_block(2), _win_block(3), _win_block(4), row, row, row, _whole((1, D))],
        out_specs=[_cols(tm, 3 * D, 0), row, _whole((1, D))],
        out_shape=[_sds((s_len, 3 * D), BF), _sds((s_len, D), F32), _sds((1, D), F32)],
        scratch_shapes=[pltpu.VMEM((tm + HALO, D), F32), pltpu.VMEM((tm, D), F32), pltpu.VMEM((tm + HALO, D), F32)], comm=comm)


def wgrad(a, b, tk, tn, name, stacked=False, diag=False, a_spec=None, comm=None):
    s_len = b.shape[0]
    k_dim = a.shape[-1]
    n_dim = b.shape[1]
    ts = min(TS, s_len)
    nk = 1 if diag else k_dim // tk
    nn, ns = n_dim // tn, s_len // ts

    def body(a_ref, b_ref, o_ref, acc):
        s = pl.program_id(2)
        _accum(acc, s == 0, _dot_tn(a_ref[...].astype(BF), b_ref[...].astype(BF)))

        @pl.when(s == ns - 1)
        def _():
            o_ref[...] = acc[...].astype(BF).reshape(o_ref.shape)

    if a_spec is None:
        a_spec = pl.BlockSpec((ts, tk), (lambda k, n, s: (s, n)) if diag else (lambda k, n, s: (s, k)))
    if stacked or diag:
        out_shape = _sds((nn, tk if diag else k_dim, tn), BF)
        o_spec = pl.BlockSpec((1, tk, tn), lambda k, n, s: (n, k, 0))
    else:
        out_shape = _sds((k_dim, n_dim), BF)
        o_spec = pl.BlockSpec((tk, tn), lambda k, n, s: (k, n))
    (out,), got = _pcall(
        body, (a, b), name=name, grid=(nk, nn, ns),
        in_specs=[a_spec, pl.BlockSpec((ts, tn), lambda k, n, s: (s, n))], out_specs=[o_spec], out_shape=[out_shape],
        scratch_shapes=[pltpu.VMEM((tk, tn), F32)], comm=comm)
    return out if comm is None else (out, got)


_WEIGHTS = ['g_mix_pre', 'w_in', 'w_sgu_s', 'b_sgu_s', 'g_sgu_v', 'b_sgu_v', 'w_sgu_out', 'w_dw', 'b_dw', 'g_conv_ln', 'b_conv_ln',
            'w_conv_out', 'w_pool', 's_pool', 'w_pool_out', 'w_out', 'g_mix_post', 'g_ffn_pre', 'w_ffn_in', 'w_ffn_out', 'g_ffn_post',
            'w_ple', 'w_ple_gate']
_SHARDED = ['w_in', 'w_sgu_out', 'w_conv_out', 'w_pool', 'w_pool_out', 'w_out', 'w_ffn_in', 'w_ffn_out', 'w_ple', 'w_ple_gate']
_VECTORS = ['g_mix_pre', 'g_sgu_v', 'b_sgu_v', 'b_dw', 'g_conv_ln', 'b_conv_ln', 's_pool', 'g_mix_post', 'g_ffn_pre', 'g_ffn_post']
_SUBLANES = 8
_SGU_ROWS = HEADS * BLK * BLK // D
_REP_ROWS = _SUBLANES * (len(_VECTORS) + 2) + _SGU_ROWS


def _pack_replicated(t, layer):
    rows = [jnp.pad(t[k][layer].reshape(1, D), ((0, _SUBLANES - 1), (0, 0))) for k in _VECTORS + ['b_sgu_s']]
    return jnp.concatenate(rows + [t['w_sgu_s'][layer].reshape(_SGU_ROWS, D), jnp.zeros((_SUBLANES, D), F32)], axis=0)


def _unpack_replicated(packed):
    out = {}
    for i, k in enumerate(_VECTORS):
        out[k] = packed[:, _SUBLANES * i, :]
    o = _SUBLANES * len(_VECTORS)
    out['b_sgu_s'] = packed[:, o, :].reshape(2, HEADS, BLK)
    out['w_sgu_s'] = packed[:, o + _SUBLANES:o + _SUBLANES + _SGU_ROWS, :].reshape(2, HEADS, BLK, BLK)
    return out


def _pad_taps(w):
    return jnp.pad(w, ((0, HALO - CONV_W), (0, 0)))


def kernel(x, p, g_mix_pre, w_in, w_sgu_s, b_sgu_s, g_sgu_v, b_sgu_v, w_sgu_out, w_dw, b_dw, g_conv_ln, b_conv_ln, w_conv_out, w_pool, s_pool, w_pool_out, w_out, g_mix_post, g_ffn_pre, w_ffn_in, w_ffn_out, g_ffn_post, w_ple, w_ple_gate, loss_target, m_g_mix_pre, m_w_in, m_w_sgu_s, m_b_sgu_s, m_g_sgu_v, m_b_sgu_v, m_w_sgu_out, m_w_dw, m_b_dw, m_g_conv_ln, m_b_conv_ln, m_w_conv_out, m_w_pool, m_s_pool, m_w_pool_out, m_w_out, m_g_mix_post, m_g_ffn_pre, m_w_ffn_in, m_w_ffn_out, m_g_ffn_post, m_w_ple, m_w_ple_gate, v_g_mix_pre, v_w_in, v_w_sgu_s, v_b_sgu_s, v_g_sgu_v, v_b_sgu_v, v_w_sgu_out, v_w_dw, v_b_dw, v_g_conv_ln, v_b_conv_ln, v_w_conv_out, v_w_pool, v_s_pool, v_w_pool_out, v_w_out, v_g_mix_post, v_g_ffn_pre, v_w_ffn_in, v_w_ffn_out, v_g_ffn_post, v_w_ple, v_w_ple_gate):
    W = dict(g_mix_pre=g_mix_pre, w_in=w_in, w_sgu_s=w_sgu_s, b_sgu_s=b_sgu_s, g_sgu_v=g_sgu_v, b_sgu_v=b_sgu_v, w_sgu_out=w_sgu_out,
             w_dw=w_dw, b_dw=b_dw, g_conv_ln=g_conv_ln, b_conv_ln=b_conv_ln, w_conv_out=w_conv_out, w_pool=w_pool, s_pool=s_pool,
             w_pool_out=w_pool_out, w_out=w_out, g_mix_post=g_mix_post, g_ffn_pre=g_ffn_pre, w_ffn_in=w_ffn_in, w_ffn_out=w_ffn_out,
             g_ffn_post=g_ffn_post, w_ple=w_ple, w_ple_gate=w_ple_gate)
    M = dict(g_mix_pre=m_g_mix_pre, w_in=m_w_in, w_sgu_s=m_w_sgu_s, b_sgu_s=m_b_sgu_s, g_sgu_v=m_g_sgu_v, b_sgu_v=m_b_sgu_v,
             w_sgu_out=m_w_sgu_out, w_dw=m_w_dw, b_dw=m_b_dw, g_conv_ln=m_g_conv_ln, b_conv_ln=m_b_conv_ln, w_conv_out=m_w_conv_out,
             w_pool=m_w_pool, s_pool=m_s_pool, w_pool_out=m_w_pool_out, w_out=m_w_out, g_mix_post=m_g_mix_post, g_ffn_pre=m_g_ffn_pre,
             w_ffn_in=m_w_ffn_in, w_ffn_out=m_w_ffn_out, g_ffn_post=m_g_ffn_post, w_ple=m_w_ple, w_ple_gate=m_w_ple_gate)
    V = dict(g_mix_pre=v_g_mix_pre, w_in=v_w_in, w_sgu_s=v_w_sgu_s, b_sgu_s=v_b_sgu_s, g_sgu_v=v_g_sgu_v, b_sgu_v=v_b_sgu_v,
             w_sgu_out=v_w_sgu_out, w_dw=v_w_dw, b_dw=v_b_dw, g_conv_ln=v_g_conv_ln, b_conv_ln=v_b_conv_ln, w_conv_out=v_w_conv_out,
             w_pool=v_w_pool, s_pool=v_s_pool, w_pool_out=v_w_pool_out, w_out=v_w_out, g_mix_post=v_g_mix_post, g_ffn_pre=v_g_ffn_pre,
             w_ffn_in=v_w_ffn_in, w_ffn_out=v_w_ffn_out, g_ffn_post=v_g_ffn_post, w_ple=v_w_ple, w_ple_gate=v_w_ple_gate)

    my_c = lax.axis_index("c")
    core_id = my_c.astype(jnp.int32).reshape(1)
    my_chip = 2 * lax.axis_index("x") + lax.axis_index("y")
    my_dev = 2 * my_chip + my_c
    s_len = x.shape[1]
    h0 = x.reshape(s_len, D)
    target = loss_target.reshape(s_len, D)

    shard = [{k: W[k][l].astype(BF) for k in _SHARDED} for l in range(2)]
    for l in range(2):
        shard[l]['w_dw'] = w_dw[l]
        shard[l]['w_ffn_in'] = jnp.swapaxes(w_ffn_in[l], 0, 1).astype(BF)
    mixer_w = ['w_sgu_out', 'w_conv_out', 'w_pool', 'w_pool_out', 'w_out', 'w_dw']
    ffn_w = ['w_ffn_in', 'w_ffn_out', 'w_ple', 'w_ple_gate']
    hosted_gather = {
        'norm_proj_in': (0, mixer_w),
        'conv_fwd': (0, ffn_w),
        'merge_out': (1, mixer_w),
        'norm_proj_ffn': (1, ['w_in']),
        'ffn_out': (1, ffn_w),
    }
    G = [{'w_in': run_comm(Gather([shard[0]['w_in']]), "gather_w_in_0")[0]}, {}]

    def gather_in(layer, call):
        if layer != 0:
            return None, (lambda got: None)
        to_layer, keys = hosted_gather[call]
        return Gather([shard[to_layer][k] for k in keys]), (lambda got: G[to_layer].update(zip(keys, got)))

    def natural_mixer(g):
        wpool = jnp.transpose(g['w_pool'], (1, 0, 2, 3)).reshape(4, POOL_GD, POOL_GD)
        wdw = jnp.transpose(g['w_dw'].reshape(N_DEV, CONV_W, BLK), (1, 0, 2)).reshape(CONV_W, D)
        return dict(wpool=wpool, wdw=_pad_taps(wdw))

    def natural_ffn(g):
        wfit = g['w_ffn_in'].reshape(4, D_FF // 2, D)
        wple = jnp.transpose(g['w_ple'], (1, 0, 2)).reshape(PLE, D)
        return dict(wfit=wfit, wple=wple)

    def vec(name, layer):
        return W[name][layer].reshape(1, D)

    saved = []
    h = h0
    for l in range(2):
        g = G[l]
        comm, land = gather_in(l, 'norm_proj_in')
        (proj, hn), got = norm_proj(h, vec('g_mix_pre', l), g['w_in'], f"norm_proj_in_{l}", comm=comm)
        land(got)
        nat = natural_mixer(g)
        bsfull = jnp.repeat(b_sgu_s[l].T, BLK, axis=1)
        wst = jnp.swapaxes(w_sgu_s[l], 1, 2)
        mixed, sgu, bra = sgu_fwd(proj, w_sgu_s[l], bsfull, vec('g_sgu_v', l), vec('b_sgu_v', l), g['w_sgu_out'], f"sgu_fwd_{l}")
        comm, land = gather_in(l, 'conv_fwd')
        (conv, cb, brb), got = conv_fwd(proj, nat['wdw'], vec('b_dw', l), vec('g_conv_ln', l), vec('b_conv_ln', l), g['w_conv_out'],
                                        f"conv_fwd_{l}", comm=comm)
        land(got)
        nat.update(natural_ffn(g))
        pooled, pm, brc = pool_fwd(proj, nat['wpool'], vec('s_pool', l), g['w_pool_out'], f"pool_fwd_{l}")
        comm, land = gather_in(l, 'merge_out')
        (merged, mo, h1), got = merge_out(proj, bra, brb, brc, h, g['w_out'], vec('g_mix_post', l), f"merge_out_{l}", comm=comm)
        land(got)
        comm, land = gather_in(l, 'norm_proj_ffn')
        (ff, hn2), got = norm_proj(h1, vec('g_ffn_pre', l), nat['wfit'], f"norm_proj_ffn_{l}", transposed=True, comm=comm)
        land(got)
        comm, land = gather_in(l, 'ffn_out')
        (act, f, h2, pg, h3), got = ffn_out(ff, h1, p, g['w_ffn_out'], vec('g_ffn_post', l), g['w_ple_gate'], nat['wple'], l,
                                            f"ffn_out_{l}", comm=comm)
        land(got)
        saved.append(dict(h=h, nat=nat, wst=wst, proj=proj, hn=hn, mixed=mixed, sgu=sgu, bra=bra, conv=conv, cb=cb, brb=brb,
                          pooled=pooled, pm=pm, brc=brc, merged=merged, mo=mo, h1=h1, ff=ff, hn2=hn2, act=act, f=f, h2=h2, pg=pg))
        h = h3

    dh = h

    parts = {k: [None, None] for k in _SHARDED}
    small = {k: [None, None] for k in _VECTORS + ['b_sgu_s', 'w_sgu_s', 'w_dw']}
    chip_parts = [{}, {}]
    from_chips = [{}, {}]
    gathered_small = [None, None]

    def to_sibling(layer, keys):
        return ToSibling([parts[k][layer] for k in keys])

    def add_siblings(layer, keys, from_sibling):
        for k, rv in zip(keys, from_sibling):
            st = parts[k][layer]
            cols = st.shape[-1]
            chip_parts[layer][k] = add_core_side(st.reshape(N_DEV, -1, cols), rv.reshape(4, -1, cols), core_id,
                                                 f"rs_add_{k}_{layer}").reshape(rv.shape)

    def to_chips(layer, keys):
        return ToChips([chip_parts[layer][k] for k in keys])

    def small_pack(layer):
        return jnp.concatenate([_pack_replicated(small, layer), small['w_dw'][layer]], axis=0)

    ffn_group = ['w_ffn_in', 'w_ffn_out', 'w_ple_gate', 'w_ple']
    mix_group = ['w_out', 'w_sgu_out', 'w_conv_out', 'w_pool_out', 'w_pool']
    big = ['w_in', 'w_ffn_in']
    others = [k for k in _SHARDED if k not in big]
    hosted_rs = {
        'ple_ffn_bwd': (lambda: to_sibling(1, _SHARDED), lambda got: add_siblings(1, _SHARDED, got)),
        'mix_post_bwd': (lambda: Both(to_sibling(0, ffn_group), Gather([small_pack(1)])),
                         lambda got: (add_siblings(0, ffn_group, got[:-1]), gathered_small.__setitem__(1, got[-1]))),
        'ffn_in_bwd': (lambda: to_chips(1, others), lambda got: from_chips[1].update(zip(others, got))),
        'sgu_bwd': (lambda: to_chips(1, big), lambda got: from_chips[1].update(zip(big, got))),
        'conv_bwd': (lambda: to_chips(0, ffn_group), lambda got: from_chips[0].update(zip(ffn_group, got))),
        'wgrad_in_sgu': (lambda: to_sibling(0, mix_group), lambda got: add_siblings(0, mix_group, got)),
        'wgrad_in_seq': (lambda: to_chips(0, mix_group), lambda got: from_chips[0].update(zip(mix_group, got))),
        'wgrad_in_gate': (lambda: Gather([small_pack(0)]), lambda got: gathered_small.__setitem__(0, got[0])),
    }

    def exchange_in(layer, call):
        if layer != 0 or call not in hosted_rs:
            return None, (lambda got: None)
        make, land = hosted_rs[call]
        return make(), land

    for l in (1, 0):
        sv, g, nat = saved[l], G[l], saved[l]['nat']

        def wg(call, a, b, tk, tn, **kw):
            comm, land = exchange_in(l, call)
            if comm is None:
                return wgrad(a, b, tk, tn, f"{call}_{l}", **kw)
            out, got = wgrad(a, b, tk, tn, f"{call}_{l}", comm=comm, **kw)
            land(got)
            return out

        comm, land = exchange_in(l, 'ple_ffn_bwd')
        res, got = ple_ffn_bwd(
            dh, sv['pg'], p, sv['f'], sv['ff'], nat['wple'], g['w_ple_gate'], g['w_ffn_out'], vec('g_ffn_post', l), l,
            f"ple_ffn_bwd_{l}", target=target if l == 1 else None, comm=comm)
        land(got)
        dh2, dpe, dpg, df, dff, small['g_ffn_post'][l] = res[:6]
        if l == 1:
            loss = lax.psum(res[6][0, 0], ("x", "y", "c"))
        comm, land = exchange_in(l, 'ffn_in_bwd')
        (dh1, small['g_ffn_pre'][l]), got = ffn_in_bwd(dff, nat['wfit'], sv['h1'], dh2, vec('g_ffn_pre', l), f"ffn_in_bwd_{l}",
                                                       comm=comm)
        land(got)
        p_spec = pl.BlockSpec((None, None, min(TS, s_len), PLE), functools.partial(lambda k, n, s, ll: (ll, 0, s, 0), ll=l))
        parts['w_ple'][l] = jnp.transpose(wg('wgrad_ple', p, dpe, PLE, D, a_spec=p_spec).reshape(PLE, N_DEV, BLK), (1, 0, 2))
        parts['w_ple_gate'][l] = wg('wgrad_ple_gate', sv['h2'], dpg, D, D).reshape(N_DEV, BLK, D)
        parts['w_ffn_out'][l] = wg('wgrad_ffn_out', sv['act'], df, D_FF // 2, D).reshape(N_DEV, D_FF // N_DEV, D)
        parts['w_ffn_in'][l] = wg('wgrad_ffn_in', dff, sv['hn2'], D_FF // 2, D).reshape(N_DEV, D_FF // 4, D)

        comm, land = exchange_in(l, 'mix_post_bwd')
        (dmo, dbra, dbrb, dbrc, dzg, dhn_g, small['g_mix_post'][l]), got = mix_post_bwd(
            dh1, sv['mo'], vec('g_mix_post', l), g['w_out'], sv['proj'], sv['bra'], sv['brb'], sv['brc'], g['w_in'],
            f"mix_post_bwd_{l}", comm=comm)
        land(got)
        comm, land = exchange_in(l, 'sgu_bwd')
        (dzs, dhn_ag, dws, dbs, small['g_sgu_v'][l], small['b_sgu_v'][l]), got = sgu_bwd(
            dbra, g['w_sgu_out'], sv['proj'], sv['mixed'], sv['wst'], vec('g_sgu_v', l), vec('b_sgu_v', l), g['w_in'],
            dhn_g, f"sgu_bwd_{l}", comm=comm)
        land(got)
        small['w_sgu_s'][l] = dws
        small['b_sgu_s'][l] = dbs.T
        comm, land = exchange_in(l, 'conv_bwd')
        (dc, dwdw, small['b_dw'][l], small['g_conv_ln'][l], small['b_conv_ln'][l]), got = conv_bwd(
            dbrb, g['w_conv_out'], sv['proj'], sv['conv'], vec('g_conv_ln', l), vec('b_conv_ln', l), f"conv_bwd_{l}", comm=comm)
        land(got)
        small['w_dw'][l] = dwdw
        dmr, q, small['s_pool'][l] = pool_bwd(dbrc, g['w_pool_out'], sv['pooled'], nat['wpool'], vec('s_pool', l), f"pool_bwd_{l}")
        comm, land = exchange_in(l, 'seq_bwd')
        (dzc, dh, small['g_mix_pre'][l]), got = seq_bwd(dc, q, sv['proj'], nat['wdw'], g['w_in'], dhn_ag, sv['h'], dh1,
                                                        vec('g_mix_pre', l), f"seq_bwd_{l}", comm=comm)
        land(got)

        parts['w_out'][l] = wg('wgrad_out', sv['merged'], dmo, D, D).reshape(N_DEV, BLK, D)
        parts['w_sgu_out'][l] = wg('wgrad_sgu_out', sv['sgu'], dbra, D, D).reshape(N_DEV, BLK, D)
        parts['w_conv_out'][l] = wg('wgrad_conv_out', sv['cb'], dbrb, D, D).reshape(N_DEV, BLK, D)
        parts['w_pool_out'][l] = wg('wgrad_pool_out', sv['pm'], dbrc, D, D).reshape(N_DEV, BLK, D)
        g_pool = wg('wgrad_pool', sv['pooled'], dmr, POOL_GD, POOL_GD, diag=True)
        parts['w_pool'][l] = jnp.transpose(g_pool.reshape(4, N_DEV, POOL_GD // N_DEV, POOL_GD), (1, 0, 2, 3))
        parts['w_in'][l] = jnp.concatenate([
            wg('wgrad_in_sgu', sv['hn'], dzs, D, D, stacked=True),
            wg('wgrad_in_seq', sv['hn'], dzc, D, D, stacked=True),
            wg('wgrad_in_gate', sv['hn'], dzg, D, D, stacked=True)], axis=0)
    grad_x = dh.reshape(1, s_len, D)

    add_siblings(0, ['w_in'], run_comm(to_sibling(0, ['w_in']), "rs_to_sibling_w_in_0"))
    from_chips[0]['w_in'] = run_comm(to_chips(0, ['w_in']), "rs_to_chips_w_in_0")[0]

    outs = {}
    for k in _SHARDED:
        wmv = [jnp.swapaxes(t[k], 1, 2) if k == 'w_ffn_in' else t[k] for t in (W, M, V)]
        cols = wmv[0].shape[-1]
        pieces = []
        for layer in range(2):
            own = lax.dynamic_index_in_dim(chip_parts[layer][k], my_chip, axis=0, keepdims=False).reshape(-1, cols)
            rv3 = from_chips[layer][k].reshape(3, -1, cols)
            pieces.append([(own, None), (rv3, 0), (rv3, 1), (rv3, 2)])
        res = adamw_layers(*[t.reshape(2, -1, cols) for t in wmv], pieces, f"adamw_{k}")
        res = [r.reshape(wmv[0].shape) for r in res]
        outs[k] = [jnp.swapaxes(r, 1, 2) for r in res] if k == 'w_ffn_in' else res

    packed = [jnp.stack([_pack_replicated(t, 0), _pack_replicated(t, 1)], axis=0) for t in (W, M, V)]
    rep_res = adamw_layers(*packed, [[(gathered_small[layer], d) for d in range(N_DEV)] for layer in range(2)], "adamw_replicated")
    for idx, res in enumerate(rep_res):
        for name, val in _unpack_replicated(res).items():
            outs.setdefault(name, [None] * 4)[idx] = val
    dw_sum = jnp.stack([sum_slabs(gathered_small[layer][:, _REP_ROWS:, :], f"sum_w_dw_{layer}") for layer in range(2)], axis=0)
    dw_mine = lax.dynamic_slice_in_dim(dw_sum[:, :CONV_W], my_dev * BLK, BLK, axis=2)
    res = adamw(w_dw.reshape(2 * CONV_W, BLK), m_w_dw.reshape(2 * CONV_W, BLK), v_w_dw.reshape(2 * CONV_W, BLK),
                [(dw_mine.reshape(2 * CONV_W, BLK), None)], "adamw_w_dw")
    outs['w_dw'] = [r.reshape(w_dw.shape) for r in res]

    result = [loss, grad_x]
    for idx in range(4):
        result += [outs[k][idx] for k in _WEIGHTS]
    return tuple(result)
```

```python
import functools
import math

import jax
import jax.numpy as jnp
from jax import lax
from jax.experimental import pallas as pl
from jax.experimental.pallas import tpu as pltpu

F32 = jnp.float32
BF = jnp.bfloat16

D = 1024
D_FF = 2816
PLE = 256
N_DEV = 8
HEADS = 8
BLK = 128
CHUNK = 64
CONV_W = 31
POOL_WINDOWS = (2, 4, 8, 16)
POOL_GD = 256
EPS = 1e-6

V7X_VMEM_BYTES = 64 * 2**20
VMEM_LIMIT = V7X_VMEM_BYTES * 7 // 8
HALO = 32
RC = 64
LC = 128
TM = 512
TMB = 512
TMW = 256
TMP = 1024
TS = 2048

ADAM_LR, ADAM_B1, ADAM_B2, ADAM_EPS, ADAM_WD, ADAM_STEP = 0.001, 0.9, 0.999, 1e-08, 0.01, 10

MESH = pl.DeviceIdType.MESH
ANY = pl.BlockSpec(memory_space=pl.ANY)

_GELU_K0 = math.sqrt(2.0 / math.pi)
_GELU_K1 = 0.044715
_LOG2E = 1.4426950408889634


def _dot(a, b):
    return jnp.dot(a, b, preferred_element_type=F32)


def _dot_nt(a, b):
    return lax.dot_general(a, b, (((1,), (1,)), ((), ())), preferred_element_type=F32)


def _dot_tn(a, b):
    return lax.dot_general(a, b, (((0,), (0,)), ((), ())), preferred_element_type=F32)


def _sig(x):
    return 1.0 / (1.0 + jnp.exp2(x * (-_LOG2E)))


def _gelu(x):
    s = 1.0 / (1.0 + jnp.exp2(x * ((-2.0 * _GELU_K0 * _LOG2E) + (-2.0 * _GELU_K0 * _GELU_K1 * _LOG2E) * (x * x))))
    return x * s, s


def _gelu_grad(x, s):
    return s + x * s * (1.0 - s) * ((2.0 * _GELU_K0) + (6.0 * _GELU_K0 * _GELU_K1) * (x * x))


def _rstd(x):
    return lax.rsqrt(jnp.mean(x * x, axis=-1, keepdims=True) + EPS)


def _rms_bwd(x, gd, r):
    return r * gd - x * (r * r * r) * jnp.mean(gd * x, axis=-1, keepdims=True)


def _ln_fwd(x):
    mu = jnp.mean(x, axis=-1, keepdims=True)
    xc = x - mu
    rs = lax.rsqrt(jnp.mean(xc * xc, axis=-1, keepdims=True) + EPS)
    return xc * rs, rs


def _ln_bwd(dhat, hat, rs):
    return rs * (dhat - jnp.mean(dhat, axis=-1, keepdims=True) - hat * jnp.mean(dhat * hat, axis=-1, keepdims=True))


def _colsum(x):
    return jnp.sum(x, axis=0, keepdims=True)


def _accum(ref, first, val):
    @pl.when(first)
    def _():
        ref[...] = val

    @pl.when(jnp.logical_not(first))
    def _():
        ref[...] += val


def _sgu_mask(transposed):
    r = lax.broadcasted_iota(jnp.int32, (BLK, BLK), 0) // CHUNK
    c = lax.broadcasted_iota(jnp.int32, (BLK, BLK), 1) // CHUNK
    return (r <= c) if transposed else (c <= r)


def _inv_count(i, tm, w):
    t = lax.broadcasted_iota(jnp.int32, (tm, 1), 0) + i * tm
    return 1.0 / jnp.minimum(t + 1, w).astype(F32)


def _count(i, tm, w):
    t = lax.broadcasted_iota(jnp.int32, (tm, 1), 0) + i * tm
    return jnp.minimum(t + 1, w).astype(F32)


def _params(n_grid):
    return pltpu.CompilerParams(dimension_semantics=("arbitrary",) * n_grid, vmem_limit_bytes=VMEM_LIMIT)


def _sds(shape, dtype):
    return jax.ShapeDtypeStruct(shape, dtype)


def _cols(tm, width, cb):
    return pl.BlockSpec((tm, width), lambda i: (i, cb))


def _whole(shape):
    nd = len(shape)
    return pl.BlockSpec(shape, lambda i: (0,) * nd)


def _prev_halo(tm, width, cb):
    return pl.BlockSpec((HALO, width), lambda i: (jnp.maximum(i * (tm // HALO) - 1, 0), cb))


def _next_halo(tm, width, cb, s_len):
    last = s_len // HALO - 1
    return pl.BlockSpec((HALO, width), lambda i: (jnp.minimum((i + 1) * (tm // HALO), last), cb))


def _rowsharded(rows):
    return pl.BlockSpec((N_DEV, rows, D), lambda i: (0, 0, 0))


def _win_block(j):
    return pl.BlockSpec((None, D, D), lambda i: (j, 0, 0))


def _row_tile(rows, cap):
    t = min(rows, cap)
    while rows % t or t % 16:
        t -= 16
    return t


class Gather:
    def __init__(self, arrs):
        self.arrs = list(arrs)
        n = self.n = len(self.arrs)
        self.out_shape = [_sds((N_DEV,) + a.shape, a.dtype) for a in self.arrs]
        self.scratch = [pltpu.SemaphoreType.DMA((n, 7)), pltpu.SemaphoreType.DMA((n, 7)), pltpu.SemaphoreType.DMA((n,))]

    def _plan(self, ins, outs, sems):
        send, recv, local = sems
        x, y, c = lax.axis_index("x"), lax.axis_index("y"), lax.axis_index("c")
        me, sibling = (x, y, c), (x, y, 1 - c)
        chips = [(1 - x, y), (x, 1 - y), (1 - x, 1 - y)]

        def copy(a, k, block, to, src=None):
            dst = outs[a].at[4 * block[0] + 2 * block[1] + block[2]]
            return pltpu.make_async_remote_copy(
                src_ref=dst if src is None else src, dst_ref=dst, send_sem=send.at[a, k], recv_sem=recv.at[a, k],
                device_id=to, device_id_type=MESH)

        mine = [pltpu.make_async_copy(ins[a], outs[a].at[4 * x + 2 * y + c], local.at[a]) for a in range(self.n)]
        first = []
        for a in range(self.n):
            first.append(copy(a, 0, me, sibling, src=ins[a]))
            first += [copy(a, 1 + j, me, (*chip, c), src=ins[a]) for j, chip in enumerate(chips)]
        return me, sibling, chips, c, copy, mine, first

    def start(self, ins, outs, sems):
        *_, mine, first = self._plan(ins, outs, sems)
        for cp in mine + first:
            cp.start()

    def finish(self, ins, outs, sems):
        me, sibling, chips, c, copy, mine, first = self._plan(ins, outs, sems)
        passed = []
        for a in range(self.n):
            for j, chip in enumerate(chips):
                copy(a, 1 + j, (*chip, c), me).wait_recv()
                fwd = copy(a, 4 + j, (*chip, c), sibling)
                fwd.start()
                passed.append(fwd)
        for a in range(self.n):
            copy(a, 0, sibling, me).wait_recv()
            for j, chip in enumerate(chips):
                copy(a, 4 + j, (*chip, 1 - c), me).wait_recv()
        for cp in first + passed:
            cp.wait_send()
        for cp in mine:
            cp.wait()


class ToSibling:
    def __init__(self, parts):
        self.arrs = list(parts)
        n = self.n = len(self.arrs)
        self.out_shape = [_sds((4,) + p.shape[1:], p.dtype) for p in self.arrs]
        self.scratch = [pltpu.SemaphoreType.DMA((n,)), pltpu.SemaphoreType.DMA((n,))]

    def start(self, ins, outs, sems):
        send, recv = sems
        x, y, c = lax.axis_index("x"), lax.axis_index("y"), lax.axis_index("c")
        for a in range(self.n):
            for q in range(4):
                pltpu.make_async_remote_copy(
                    src_ref=ins[a].at[2 * q + 1 - c], dst_ref=outs[a].at[q], send_sem=send.at[a], recv_sem=recv.at[a],
                    device_id=(x, y, 1 - c), device_id_type=MESH).start()

    def finish(self, ins, outs, sems):
        send, recv = sems
        x, y, c = lax.axis_index("x"), lax.axis_index("y"), lax.axis_index("c")
        for a in range(self.n):
            pltpu.make_async_remote_copy(
                src_ref=outs[a], dst_ref=outs[a], send_sem=send.at[a], recv_sem=recv.at[a],
                device_id=(x, y, 1 - c), device_id_type=MESH).wait()


class ToChips:
    def __init__(self, cps):
        self.arrs = list(cps)
        n = self.n = len(self.arrs)
        self.out_shape = [_sds((3,) + p.shape[1:], p.dtype) for p in self.arrs]
        self.scratch = [pltpu.SemaphoreType.DMA((n,)), pltpu.SemaphoreType.DMA((n,))]

    def start(self, ins, outs, sems):
        send, recv = sems
        x, y, c = lax.axis_index("x"), lax.axis_index("y"), lax.axis_index("c")
        for a in range(self.n):
            for r, (px, py) in enumerate([(1 - x, y), (x, 1 - y), (1 - x, 1 - y)]):
                pltpu.make_async_remote_copy(
                    src_ref=ins[a].at[2 * px + py], dst_ref=outs[a].at[r], send_sem=send.at[a], recv_sem=recv.at[a],
                    device_id=(px, py, c), device_id_type=MESH).start()

    def finish(self, ins, outs, sems):
        send, recv = sems
        x, y, c = lax.axis_index("x"), lax.axis_index("y"), lax.axis_index("c")
        for a in range(self.n):
            pltpu.make_async_remote_copy(
                src_ref=outs[a], dst_ref=outs[a], send_sem=send.at[a], recv_sem=recv.at[a],
                device_id=(x, y, c), device_id_type=MESH).wait()


class Both:
    def __init__(self, a, b):
        self.a, self.b = a, b
        self.arrs, self.n = a.arrs + b.arrs, a.n + b.n
        self.out_shape, self.scratch = a.out_shape + b.out_shape, a.scratch + b.scratch

    def _each(self, ins, outs, sems):
        na, ns = self.a.n, len(self.a.scratch)
        return (self.a, (ins[:na], outs[:na], sems[:ns])), (self.b, (ins[na:], outs[na:], sems[ns:]))

    def start(self, ins, outs, sems):
        for comm, refs in self._each(ins, outs, sems):
            comm.start(*refs)

    def finish(self, ins, outs, sems):
        for comm, refs in self._each(ins, outs, sems):
            comm.finish(*refs)


def run_comm(comm, name):
    n = comm.n

    def body(*refs):
        ins, outs, sems = refs[:n], refs[n:2 * n], refs[2 * n:]
        comm.start(ins, outs, sems)
        comm.finish(ins, outs, sems)

    return pl.pallas_call(body, name=name, out_shape=comm.out_shape, in_specs=[ANY] * n, out_specs=[ANY] * n,
                          scratch_shapes=comm.scratch)(*comm.arrs)


def _pcall(body, args, *, name, grid, in_specs, out_specs, out_shape, scratch_shapes=(), comm=None):
    params = _params(len(grid))
    scratch_shapes = list(scratch_shapes)
    if comm is None:
        outs = pl.pallas_call(body, name=name, grid=grid, in_specs=in_specs, out_specs=out_specs, out_shape=out_shape,
                              scratch_shapes=scratch_shapes, compiler_params=params)(*args)
        return outs, None
    n_in, n_out, n_scr, nc = len(in_specs), len(out_specs), len(scratch_shapes), comm.n

    def hosted(*refs):
        ins, cins = refs[:n_in], refs[n_in:n_in + nc]
        o0 = n_in + nc
        outs, couts = refs[o0:o0 + n_out], refs[o0 + n_out:o0 + n_out + nc]
        s0 = o0 + n_out + nc
        scr, csems = refs[s0:s0 + n_scr], refs[s0 + n_scr:]
        first = pl.program_id(0) == 0
        last = pl.program_id(0) == grid[0] - 1
        for ax in range(1, len(grid)):
            first = jnp.logical_and(first, pl.program_id(ax) == 0)
            last = jnp.logical_and(last, pl.program_id(ax) == grid[ax] - 1)

        @pl.when(first)
        def _():
            comm.start(cins, couts, csems)
        body(*ins, *outs, *scr)

        @pl.when(last)
        def _():
            comm.finish(cins, couts, csems)

    res = pl.pallas_call(
        hosted, name=name, grid=grid, in_specs=list(in_specs) + [ANY] * nc, out_specs=list(out_specs) + [ANY] * nc,
        out_shape=list(out_shape) + comm.out_shape, scratch_shapes=scratch_shapes + comm.scratch,
        compiler_params=params)(*args, *comm.arrs)
    return res[:n_out], res[n_out:]


def add_core_side(parts, from_sibling, core, name):
    _, rows, cols = parts.shape
    tr = _row_tile(rows, 512)

    def body(c_ref, a_ref, b_ref, o_ref):
        o_ref[...] = (a_ref[...].astype(F32) + b_ref[...].astype(F32)).astype(o_ref.dtype)

    side = pl.BlockSpec((None, tr, cols), lambda q, i, c: (q, i, 0))
    return pl.pallas_call(
        body, name=name,
        grid_spec=pltpu.PrefetchScalarGridSpec(
            num_scalar_prefetch=1, grid=(4, rows // tr),
            in_specs=[pl.BlockSpec((None, tr, cols), lambda q, i, c: (2 * q + c[0], i, 0)), side], out_specs=side),
        out_shape=_sds(from_sibling.shape, BF), compiler_params=_params(2))(core, parts, from_sibling)


def adamw(w, m, v, pieces, name):
    rows, cols = w.shape
    tr = _row_tile(rows, 256) if rows % 16 == 0 else rows
    np_ = len(pieces)
    c1 = 1.0 / (1.0 - ADAM_B1 ** ADAM_STEP)
    c2 = 1.0 / (1.0 - ADAM_B2 ** ADAM_STEP)

    def body(*refs):
        w_ref, m_ref, v_ref = refs[:3]
        p_refs = refs[3:3 + np_]
        g_ref, d_ref, nm_ref, nv_ref = refs[3 + np_:]
        g = p_refs[0][...].astype(F32)
        for pr in p_refs[1:]:
            g = g + pr[...].astype(F32)
        nm = ADAM_B1 * m_ref[...] + (1.0 - ADAM_B1) * g
        nv = ADAM_B2 * v_ref[...] + (1.0 - ADAM_B2) * (g * g)
        g_ref[...] = g
        nm_ref[...] = nm
        nv_ref[...] = nv
        d_ref[...] = -ADAM_LR * ((nm * c1) / (jnp.sqrt(nv * c2) + ADAM_EPS) + ADAM_WD * w_ref[...])

    spec = pl.BlockSpec((tr, cols), lambda i: (i, 0))
    p_specs = []
    for arr, k in pieces:
        if k is None:
            p_specs.append(spec)
        else:
            p_specs.append(pl.BlockSpec((None, tr, cols), functools.partial(lambda i, kk: (kk, i, 0), kk=k)))
    out = _sds(w.shape, F32)
    return pl.pallas_call(body, name=name, grid=(rows // tr,), in_specs=[spec] * 3 + p_specs, out_specs=[spec] * 4,
                          out_shape=[out] * 4, compiler_params=_params(1))(w, m, v, *[a for a, _ in pieces])


def adamw_layers(w, m, v, pieces, name):
    _, rows, cols = w.shape
    tr = _row_tile(rows, 256)
    nt = rows // tr
    counts = [len(pieces[0]), len(pieces[1])]
    c1 = 1.0 / (1.0 - ADAM_B1 ** ADAM_STEP)
    c2 = 1.0 / (1.0 - ADAM_B2 ** ADAM_STEP)

    def body(*refs):
        w_ref, m_ref, v_ref = refs[:3]
        p_refs = refs[3:3 + sum(counts)]
        g_ref, d_ref, nm_ref, nv_ref = refs[3 + sum(counts):]
        sums = []
        for group in (p_refs[:counts[0]], p_refs[counts[0]:]):
            s = group[0][...].astype(F32)
            for pr in group[1:]:
                s = s + pr[...].astype(F32)
            sums.append(s)
        g = jnp.where(pl.program_id(0) == 0, sums[0], sums[1])
        nm = ADAM_B1 * m_ref[...] + (1.0 - ADAM_B1) * g
        nv = ADAM_B2 * v_ref[...] + (1.0 - ADAM_B2) * (g * g)
        g_ref[...] = g
        nm_ref[...] = nm
        nv_ref[...] = nv
        d_ref[...] = -ADAM_LR * ((nm * c1) / (jnp.sqrt(nv * c2) + ADAM_EPS) + ADAM_WD * w_ref[...])

    def rows_of(layer):
        parked = nt - 1 if layer == 0 else 0
        return lambda l, i: jnp.where(l == layer, i, parked)

    spec = pl.BlockSpec((None, tr, cols), lambda l, i: (l, i, 0))
    p_specs, p_args = [], []
    for layer in (0, 1):
        row_of = rows_of(layer)
        for arr, k in pieces[layer]:
            p_args.append(arr)
            if k is None:
                p_specs.append(pl.BlockSpec((tr, cols), functools.partial(lambda l, i, f: (f(l, i), 0), f=row_of)))
            else:
                p_specs.append(pl.BlockSpec((None, tr, cols), functools.partial(lambda l, i, f, kk: (kk, f(l, i), 0), f=row_of, kk=k)))
    out = _sds(w.shape, F32)
    return pl.pallas_call(body, name=name, grid=(2, nt), in_specs=[spec] * 3 + p_specs, out_specs=[spec] * 4,
                          out_shape=[out] * 4, compiler_params=_params(2))(w, m, v, *p_args)


def sum_slabs(g, name):
    n, rows, cols = g.shape

    def body(g_ref, o_ref):
        s = g_ref[0]
        for k in range(1, n):
            s = s + g_ref[k]
        o_ref[...] = s

    return pl.pallas_call(body, name=name, out_shape=_sds((rows, cols), F32))(g)


def norm_proj(h, g, w, name, transposed=False, comm=None):
    s_len = h.shape[0]
    nb, tn = (w.shape[0], w.shape[1]) if transposed else (w.shape[0], w.shape[2])
    tm = min(TMP, s_len)
    nt = s_len // tm
    matmul = _dot_nt if transposed else _dot

    def body(h_ref, g_ref, w_ref, o_ref, hn_ref, hn_s):
        rows = pl.ds(pl.multiple_of(pl.program_id(1) * tm, tm), tm)

        @pl.when(pl.program_id(0) == 0)
        def _():
            x = h_ref[...]
            hn = (x * _rstd(x) * g_ref[...]).astype(BF)
            hn_s[rows, :] = hn
            hn_ref[...] = hn
        o_ref[...] = matmul(hn_s[rows, :], w_ref[...]).astype(BF)

    def first_pass_rows(j, i):
        return (jnp.where(j == 0, i, nt - 1), 0)

    return _pcall(
        body, (h, g, w), name=name, grid=(nb, nt),
        in_specs=[pl.BlockSpec((tm, D), first_pass_rows), pl.BlockSpec((1, D), lambda j, i: (0, 0)),
                  pl.BlockSpec((None,) + w.shape[1:], lambda j, i: (j, 0, 0))],
        out_specs=[pl.BlockSpec((tm, tn), lambda j, i: (i, j)), pl.BlockSpec((tm, D), first_pass_rows)],
        out_shape=[_sds((s_len, nb * tn), BF), _sds((s_len, D), BF)],
        scratch_shapes=[pltpu.VMEM((s_len, D), BF)], comm=comm)


def _sgu_mix(ws_ref, vln_s, mix_s, bs_ref, tm, transposed):
    mask = _sgu_mask(transposed)
    for hd in range(HEADS):
        wm = jnp.where(mask, ws_ref[hd], 0.0).astype(BF)
        cs = slice(hd * BLK, (hd + 1) * BLK)
        for n in range(tm // BLK):
            rs = slice(n * BLK, (n + 1) * BLK)
            r = _dot(wm, vln_s[rs, cs])
            mix_s[rs, cs] = r if bs_ref is None else r + bs_ref[:, cs]


def sgu_fwd(proj, ws, bsfull, gv, bv, wo, name):
    s_len = proj.shape[0]
    tm = min(TM, s_len)

    def body(zu_ref, zv_ref, ws_ref, bs_ref, gv_ref, bv_ref, wo_ref, mix_ref, sgu_ref, br_ref, vln_s, mix_s):
        u, _ = _gelu(zu_ref[...].astype(F32))
        v, _ = _gelu(zv_ref[...].astype(F32))
        vhat, _ = _ln_fwd(v)
        vln_s[...] = (vhat * gv_ref[...] + bv_ref[...]).astype(BF)
        _sgu_mix(ws_ref, vln_s, mix_s, bs_ref, tm, False)
        mixed = mix_s[...].astype(BF)
        mix_ref[...] = mixed
        sgu = (u * mixed.astype(F32)).astype(BF)
        sgu_ref[...] = sgu
        br_ref[...] = _dot(sgu, wo_ref[...].reshape(D, D)).astype(BF)

    return pl.pallas_call(
        body, name=name, grid=(s_len // tm,),
        in_specs=[_cols(tm, D, 0), _cols(tm, D, 1), _whole((HEADS, BLK, BLK)), _whole((BLK, D)), _whole((1, D)), _whole((1, D)),
                  _rowsharded(BLK)],
        out_specs=[_cols(tm, D, 0)] * 3, out_shape=[_sds((s_len, D), BF)] * 3,
        scratch_shapes=[pltpu.VMEM((tm, D), BF), pltpu.VMEM((tm, D), F32)], compiler_params=_params(1),
    )(proj, proj, ws, bsfull, gv, bv, wo)


def _causal_conv(ext_s, out_s, wdw_ref, bias_ref, tm):
    def chunk(ci, carry):
        r0 = pl.multiple_of((ci // (D // LC)) * RC, RC)
        l0 = pl.multiple_of((ci % (D // LC)) * LC, LC)
        win = ext_s[pl.ds(r0, RC + HALO), pl.ds(l0, LC)]
        acc = jnp.broadcast_to(bias_ref[:, pl.ds(l0, LC)], (RC, LC))
        for r in range(8):
            wr = win if r == 0 else pltpu.roll(win, r, 0)
            for m in range(4):
                d = 8 * m + r
                if d < CONV_W:
                    k = CONV_W - 1 - d
                    acc = acc + wdw_ref[k:k + 1, pl.ds(l0, LC)] * wr[HALO - 8 * m:HALO - 8 * m + RC]
        out_s[pl.ds(r0, RC), pl.ds(l0, LC)] = acc
        return carry
    lax.fori_loop(0, (tm // RC) * (D // LC), chunk, 0)


def _glu_ext(a_ref, g_ref, ah_ref, gh_ref, ext_s, first):
    hh = ah_ref[...].astype(F32) * _sig(gh_ref[...].astype(F32))
    ext_s[0:HALO, :] = jnp.where(first, 0.0, hh)
    ext_s[HALO:, :] = a_ref[...].astype(F32) * _sig(g_ref[...].astype(F32))


def conv_fwd(proj, wdw, bdw, gln, bln, wo, name, comm=None):
    s_len = proj.shape[0]
    tm = min(TM, s_len)

    def body(a_ref, g_ref, ah_ref, gh_ref, wdw_ref, bdw_ref, gln_ref, bln_ref, wo_ref, cv_ref, cb_ref, br_ref, ext_s, conv_s):
        _glu_ext(a_ref, g_ref, ah_ref, gh_ref, ext_s, pl.program_id(0) == 0)
        _causal_conv(ext_s, conv_s, wdw_ref, bdw_ref, tm)
        cv = conv_s[...].astype(BF)
        cv_ref[...] = cv
        chat, _ = _ln_fwd(cv.astype(F32))
        yl = chat * gln_ref[...] + bln_ref[...]
        cb = (yl * _sig(yl)).astype(BF)
        cb_ref[...] = cb
        br_ref[...] = _dot(cb, wo_ref[...].reshape(D, D)).astype(BF)

    return _pcall(
        body, (proj, proj, proj, proj, wdw, bdw, gln, bln, wo), name=name, grid=(s_len // tm,),
        in_specs=[_cols(tm, D, 2), _cols(tm, D, 3), _prev_halo(tm, D, 2), _prev_halo(tm, D, 3), _whole((HALO, D)),
                  _whole((1, D)), _whole((1, D)), _whole((1, D)), _rowsharded(BLK)],
        out_specs=[_cols(tm, D, 0)] * 3, out_shape=[_sds((s_len, D), BF)] * 3,
        scratch_shapes=[pltpu.VMEM((tm + HALO, D), F32), pltpu.VMEM((tm, D), F32)], comm=comm)


def pool_fwd(proj, wpool, spool, wo, name):
    s_len = proj.shape[0]
    tm = min(TM, s_len)

    def body(z_ref, zh_ref, wp_ref, sp_ref, wo_ref, pooled_ref, pm_ref, br_ref, ext_s, mr_s):
        i = pl.program_id(0)
        ext_s[0:HALO, :] = jnp.where(i == 0, 0.0, zh_ref[...].astype(F32))
        ext_s[HALO:, :] = z_ref[...].astype(F32)
        for gi, w in enumerate(POOL_WINDOWS):
            cs = slice(gi * POOL_GD, (gi + 1) * POOL_GD)
            e = ext_s[:, cs]
            s = e
            sh = 1
            while sh < w:
                s = s + pltpu.roll(s, sh, 0)
                sh *= 2
            pooled = (s[HALO:] * _inv_count(i, tm, w) - e[HALO:]).astype(BF)
            pooled_ref[:, cs] = pooled
            mr_s[:, cs] = _dot(pooled, wp_ref[gi])
        pm = (mr_s[...] * sp_ref[...]).astype(BF)
        pm_ref[...] = pm
        br_ref[...] = _dot(pm, wo_ref[...].reshape(D, D)).astype(BF)

    return pl.pallas_call(
        body, name=name, grid=(s_len // tm,),
        in_specs=[_cols(tm, D, 4), _prev_halo(tm, D, 4), _whole((4, POOL_GD, POOL_GD)), _whole((1, D)), _rowsharded(BLK)],
        out_specs=[_cols(tm, D, 0)] * 3, out_shape=[_sds((s_len, D), BF)] * 3,
        scratch_shapes=[pltpu.VMEM((tm + HALO, D), F32), pltpu.VMEM((tm, D), F32)], compiler_params=_params(1),
    )(proj, proj, wpool, spool, wo)


def merge_out(proj, bra, brb, brc, h, wout, gpost, name, comm=None):
    s_len = h.shape[0]
    tm = min(TM, s_len)

    def body(z0, z1, z2, a_ref, b_ref, c_ref, h_ref, wo_ref, g_ref, mg_ref, mo_ref, h1_ref):
        merged = (_sig(z0[...].astype(F32)) * a_ref[...].astype(F32) + _sig(z1[...].astype(F32)) * b_ref[...].astype(F32)
                  + _sig(z2[...].astype(F32)) * c_ref[...].astype(F32)).astype(BF)
        mg_ref[...] = merged
        mo = _dot(merged, wo_ref[...].reshape(D, D))
        mo_ref[...] = mo.astype(BF)
        h1_ref[...] = h_ref[...] + mo * _rstd(mo) * g_ref[...]

    row = _cols(tm, D, 0)
    return _pcall(
        body, (proj, proj, proj, bra, brb, brc, h, wout, gpost), name=name, grid=(s_len // tm,),
        in_specs=[_cols(tm, D, 5), _cols(tm, D, 6), _cols(tm, D, 7), row, row, row, row, _rowsharded(BLK), _whole((1, D))],
        out_specs=[row] * 3, out_shape=[_sds((s_len, D), BF), _sds((s_len, D), BF), _sds((s_len, D), F32)], comm=comm)


def _p_spec(tm, layer):
    return pl.BlockSpec((None, None, tm, PLE), lambda i: (layer, 0, i, 0))


def ffn_out(ff, h1, p, wfo, gpost, wpg, wple, layer, name, comm=None):
    s_len = h1.shape[0]
    tm = min(TMB, s_len)

    def body(fg_ref, fu_ref, h1_ref, p_ref, wfo_ref, g_ref, wpg_ref, wple_ref, act_ref, f_ref, h2_ref, pg_ref, h3_ref):
        gt = fg_ref[...].astype(F32)
        act = (gt * _sig(gt) * fu_ref[...].astype(F32)).astype(BF)
        act_ref[...] = act
        f = _dot(act, wfo_ref[...].reshape(D_FF, D))
        f_ref[...] = f.astype(BF)
        h2 = h1_ref[...] + f * _rstd(f) * g_ref[...]
        h2_ref[...] = h2
        pg = _dot(h2.astype(BF), wpg_ref[...].reshape(D, D)).astype(BF)
        pg_ref[...] = pg
        pe = _dot(p_ref[...].astype(BF), wple_ref[...])
        h3_ref[...] = h2 + _sig(pg.astype(F32)) * pe

    row = _cols(tm, D, 0)
    return _pcall(
        body, (ff, ff, h1, p, wfo, gpost, wpg, wple), name=name, grid=(s_len // tm,),
        in_specs=[_cols(tm, D_FF, 0), _cols(tm, D_FF, 1), row, _p_spec(tm, layer), _rowsharded(D_FF // N_DEV),
                  _whole((1, D)), _rowsharded(BLK), _whole((PLE, D))],
        out_specs=[_cols(tm, D_FF, 0), row, row, row, row],
        out_shape=[_sds((s_len, D_FF), BF), _sds((s_len, D), BF), _sds((s_len, D), F32), _sds((s_len, D), BF), _sds((s_len, D), F32)],
        comm=comm)


def ple_ffn_bwd(dh3, pg, p, f, ff, wple, wpg, wfo, gpost, layer, name, target=None, comm=None):
    s_len = dh3.shape[0]
    tm = min(TMW, s_len)
    nt = s_len // tm
    with_loss = target is not None

    def body(*refs):
        if with_loss:
            t_ref, refs, loss_ref, loss_acc = refs[0], refs[1:-2], refs[-2], refs[-1]
        (dh3_ref, pg_ref, p_ref, f_ref, fg_ref, fu_ref, wple_ref, wpg_ref, wfo_ref, g_ref,
         dh2_ref, dpe_ref, dpg_ref, df_ref, dff_ref, dg_ref) = refs
        i = pl.program_id(0)
        dh3v = dh3_ref[...]
        if with_loss:
            err = dh3v - t_ref[...]
            dh3v = err * (1.0 / D)
            _accum(loss_acc, i == 0, _colsum(err * err))

            @pl.when(i == nt - 1)
            def _():
                loss_ref[...] = jnp.broadcast_to(jnp.sum(loss_acc[...], axis=1, keepdims=True) * (0.5 / D), (1, LC))
        s = _sig(pg_ref[...].astype(F32))
        pe = _dot(p_ref[...].astype(BF), wple_ref[...])
        dpe_ref[...] = (dh3v * s).astype(BF)
        dpg = (dh3v * pe * s * (1.0 - s)).astype(BF)
        dpg_ref[...] = dpg
        dh2 = dh3v + _dot_nt(dpg, wpg_ref[...].reshape(D, D))
        dh2_ref[...] = dh2
        fv = f_ref[...].astype(F32)
        r = _rstd(fv)
        _accum(dg_ref, i == 0, _colsum(dh2 * fv * r))
        df = _rms_bwd(fv, dh2 * g_ref[...], r).astype(BF)
        df_ref[...] = df
        dact = _dot_nt(df, wfo_ref[...].reshape(D_FF, D))
        gt = fg_ref[...].astype(F32)
        sg = _sig(gt)
        up = fu_ref[...].astype(F32)
        dff_ref[:, 0:D_FF] = (dact * up * sg * (1.0 + gt * (1.0 - sg))).astype(BF)
        dff_ref[:, D_FF:2 * D_FF] = (dact * gt * sg).astype(BF)

    row = _cols(tm, D, 0)
    args = (dh3, pg, p, f, ff, ff, wple, wpg, wfo, gpost)
    in_specs = [row, row, _p_spec(tm, layer), row, _cols(tm, D_FF, 0), _cols(tm, D_FF, 1), _whole((PLE, D)),
                _rowsharded(BLK), _rowsharded(D_FF // N_DEV), _whole((1, D))]
    out_specs = [row, row, row, row, _cols(tm, 2 * D_FF, 0), _whole((1, D))]
    out_shape = [_sds((s_len, D), F32), _sds((s_len, D), BF), _sds((s_len, D), BF), _sds((s_len, D), BF),
                 _sds((s_len, 2 * D_FF), BF), _sds((1, D), F32)]
    scratch = []
    if with_loss:
        args, in_specs = (target,) + args, [row] + in_specs
        out_specs, out_shape = out_specs + [_whole((1, LC))], out_shape + [_sds((1, LC), F32)]
        scratch = [pltpu.VMEM((1, D), F32)]
    return _pcall(body, args, name=name, grid=(nt,), in_specs=in_specs, out_specs=out_specs, out_shape=out_shape,
                  scratch_shapes=scratch, comm=comm)


def ffn_in_bwd(dff, wt, h1, dh2, gpre, name, comm=None):
    s_len = h1.shape[0]
    tm = min(TMW, s_len)
    nb, tn, _ = wt.shape

    def body(dff_ref, w_ref, h1_ref, dh2_ref, g_ref, dh1_ref, dg_ref):
        dhn = _dot(dff_ref[:, 0:tn], w_ref[0])
        for j in range(1, nb):
            dhn = dhn + _dot(dff_ref[:, j * tn:(j + 1) * tn], w_ref[j])
        x = h1_ref[...]
        r = _rstd(x)
        _accum(dg_ref, pl.program_id(0) == 0, _colsum(dhn * x * r))
        dh1_ref[...] = dh2_ref[...] + _rms_bwd(x, dhn * g_ref[...], r)

    row = _cols(tm, D, 0)
    return _pcall(
        body, (dff, wt, h1, dh2, gpre), name=name, grid=(s_len // tm,),
        in_specs=[_cols(tm, nb * tn, 0), pl.BlockSpec((nb, tn, D), lambda i: (0, 0, 0), pipeline_mode=pl.Buffered(1)), row, row,
                  _whole((1, D))],
        out_specs=[row, _whole((1, D))],
        out_shape=[_sds((s_len, D), F32), _sds((1, D), F32)], comm=comm)


def mix_post_bwd(dh1, mo, gpost, wout, proj, bra, brb, brc, win, name, comm=None):
    s_len = dh1.shape[0]
    tm = min(TMB, s_len)

    def body(dh1_ref, mo_ref, g_ref, wo_ref, z0, z1, z2, a_ref, b_ref, c_ref, w5, w6, w7,
             dmo_ref, da_ref, db_ref, dc_ref, dz_ref, dhn_ref, dg_ref):
        i = pl.program_id(0)
        dh1v = dh1_ref[...]
        mo_v = mo_ref[...].astype(F32)
        r = _rstd(mo_v)
        _accum(dg_ref, i == 0, _colsum(dh1v * mo_v * r))
        dmo = _rms_bwd(mo_v, dh1v * g_ref[...], r).astype(BF)
        dmo_ref[...] = dmo
        dmerged = _dot_nt(dmo, wo_ref[...].reshape(D, D))
        dhn = jnp.zeros((tm, D), F32)
        for k, (z, br, dbr, w) in enumerate(((z0, a_ref, da_ref, w5), (z1, b_ref, db_ref, w6), (z2, c_ref, dc_ref, w7))):
            s = _sig(z[...].astype(F32))
            dbr[...] = (dmerged * s).astype(BF)
            dz = (dmerged * br[...].astype(F32) * s * (1.0 - s)).astype(BF)
            dz_ref[:, k * D:(k + 1) * D] = dz
            dhn = dhn + _dot_nt(dz, w[...])
        dhn_ref[...] = dhn

    row = _cols(tm, D, 0)
    return _pcall(
        body, (dh1, mo, gpost, wout, proj, proj, proj, bra, brb, brc, win, win, win), name=name, grid=(s_len // tm,),
        in_specs=[row, row, _whole((1, D)), _rowsharded(BLK), _cols(tm, D, 5), _cols(tm, D, 6), _cols(tm, D, 7), row, row, row,
                  _win_block(5), _win_block(6), _win_block(7)],
        out_specs=[row, row, row, row, _cols(tm, 3 * D, 0), row, _whole((1, D))],
        out_shape=[_sds((s_len, D), BF)] * 4 + [_sds((s_len, 3 * D), BF), _sds((s_len, D), F32), _sds((1, D), F32)], comm=comm)


def sgu_bwd(dbr, wo, proj, mixed, wst, gv, bv, win, dhn_in, name, comm=None):
    s_len = dbr.shape[0]
    tm = min(TMB, s_len)
    nt = s_len // tm

    def body(dbr_ref, wo_ref, zu_ref, zv_ref, mix_ref, wst_ref, gv_ref, bv_ref, w0, w1, dhn_in_ref,
             dz_ref, dhn_ref, dws_ref, dbs_ref, dgv_ref, dbv_ref, vln_s, dmix_s, dvln_s, bs_acc):
        i = pl.program_id(0)
        first = i == 0
        dsgu = _dot_nt(dbr_ref[...], wo_ref[...].reshape(D, D))
        zu = zu_ref[...].astype(F32)
        zv = zv_ref[...].astype(F32)
        u, tu = _gelu(zu)
        v, tv = _gelu(zv)
        vhat, rs = _ln_fwd(v)
        vln_s[...] = (vhat * gv_ref[...] + bv_ref[...]).astype(BF)
        du = dsgu * mix_ref[...].astype(F32)
        dmix = dsgu * u
        dmix_s[...] = dmix.astype(BF)
        blocks = dmix[0:BLK]
        for n in range(1, tm // BLK):
            blocks = blocks + dmix[n * BLK:(n + 1) * BLK]
        _accum(bs_acc, first, blocks)
        for hd in range(HEADS):
            cs = slice(hd * BLK, (hd + 1) * BLK)
            g = _dot_nt(dmix_s[0:BLK, cs], vln_s[0:BLK, cs])
            for n in range(1, tm // BLK):
                g = g + _dot_nt(dmix_s[n * BLK:(n + 1) * BLK, cs], vln_s[n * BLK:(n + 1) * BLK, cs])

            @pl.when(first)
            def _():
                dws_ref[hd] = g

            @pl.when(jnp.logical_not(first))
            def _():
                dws_ref[hd] += g
        _sgu_mix(wst_ref, dmix_s, dvln_s, None, tm, True)
        dvln = dvln_s[...]
        _accum(dgv_ref, first, _colsum(dvln * vhat))
        _accum(dbv_ref, first, _colsum(dvln))
        dv = _ln_bwd(dvln * gv_ref[...], vhat, rs)
        dzu = (du * _gelu_grad(zu, tu)).astype(BF)
        dzv = (dv * _gelu_grad(zv, tv)).astype(BF)
        dz_ref[:, 0:D] = dzu
        dz_ref[:, D:2 * D] = dzv
        dhn_ref[...] = dhn_in_ref[...] + _dot_nt(dzu, w0[...]) + _dot_nt(dzv, w1[...])

        @pl.when(i == nt - 1)
        def _():
            mask = _sgu_mask(False)
            for hd in range(HEADS):
                dws_ref[hd] = jnp.where(mask, dws_ref[hd], 0.0)
                dbs_ref[:, hd:hd + 1] = jnp.sum(bs_acc[:, hd * BLK:(hd + 1) * BLK], axis=1, keepdims=True)

    row = _cols(tm, D, 0)
    vec = _whole((1, D))
    return _pcall(
        body, (dbr, wo, proj, proj, mixed, wst, gv, bv, win, win, dhn_in), name=name, grid=(nt,),
        in_specs=[row, _rowsharded(BLK), _cols(tm, D, 0), _cols(tm, D, 1), row, _whole((HEADS, BLK, BLK)),
                  vec, vec, _win_block(0), _win_block(1), row],
        out_specs=[_cols(tm, 2 * D, 0), row, _whole((HEADS, BLK, BLK)), _whole((BLK, HEADS)), vec, vec],
        out_shape=[_sds((s_len, 2 * D), BF), _sds((s_len, D), F32), _sds((HEADS, BLK, BLK), F32), _sds((BLK, HEADS), F32),
                   _sds((1, D), F32), _sds((1, D), F32)],
        scratch_shapes=[pltpu.VMEM((tm, D), BF), pltpu.VMEM((tm, D), BF), pltpu.VMEM((tm, D), F32), pltpu.VMEM((BLK, D), F32)],
        comm=comm)


def conv_bwd(dbr, wo, proj, conv, gln, bln, name, comm=None):
    s_len = dbr.shape[0]
    tm = min(TMB, s_len)
    nt = s_len // tm

    def body(dbr_ref, wo_ref, a_ref, g_ref, ah_ref, gh_ref, cv_ref, gln_ref, bln_ref,
             dc_ref, dw_ref, dbdw_ref, dgln_ref, dbln_ref, ext_s, dc_s, dw_acc):
        i = pl.program_id(0)
        first = i == 0
        dcb = _dot_nt(dbr_ref[...], wo_ref[...].reshape(D, D))
        _glu_ext(a_ref, g_ref, ah_ref, gh_ref, ext_s, first)
        chat, rs = _ln_fwd(cv_ref[...].astype(F32))
        yl = chat * gln_ref[...] + bln_ref[...]
        sy = _sig(yl)
        dyl = dcb * sy * (1.0 + yl * (1.0 - sy))
        _accum(dgln_ref, first, _colsum(dyl * chat))
        _accum(dbln_ref, first, _colsum(dyl))
        dc = _ln_bwd(dyl * gln_ref[...], chat, rs)
        _accum(dbdw_ref, first, _colsum(dc))
        dc_ref[...] = dc.astype(BF)
        dc_s[...] = dc

        @pl.when(first)
        def _():
            dw_acc[...] = jnp.zeros_like(dw_acc)

        def chunk(ci, carry):
            r0 = pl.multiple_of((ci // (D // LC)) * RC, RC)
            l0 = pl.multiple_of((ci % (D // LC)) * LC, LC)
            win = ext_s[pl.ds(r0, RC + HALO), pl.ds(l0, LC)]
            dcw = dc_s[pl.ds(r0, RC), pl.ds(l0, LC)]
            for r in range(8):
                wr = win if r == 0 else pltpu.roll(win, r, 0)
                for m in range(4):
                    d = 8 * m + r
                    if d < CONV_W:
                        k = CONV_W - 1 - d
                        prod = dcw * wr[HALO - 8 * m:HALO - 8 * m + RC]
                        dw_acc[k * 8:(k + 1) * 8, pl.ds(l0, LC)] += prod.reshape(RC // 8, 8, LC).sum(axis=0)
            return carry
        lax.fori_loop(0, (tm // RC) * (D // LC), chunk, 0)

        @pl.when(i == nt - 1)
        def _():
            dw_ref[...] = dw_acc[...].reshape(HALO, 8, D).sum(axis=1)

    row = _cols(tm, D, 0)
    vec = _whole((1, D))
    return _pcall(
        body, (dbr, wo, proj, proj, proj, proj, conv, gln, bln), name=name, grid=(nt,),
        in_specs=[row, _rowsharded(BLK), _cols(tm, D, 2), _cols(tm, D, 3), _prev_halo(tm, D, 2), _prev_halo(tm, D, 3),
                  row, vec, vec],
        out_specs=[row, _whole((HALO, D)), vec, vec, vec],
        out_shape=[_sds((s_len, D), BF), _sds((HALO, D), F32), _sds((1, D), F32), _sds((1, D), F32), _sds((1, D), F32)],
        scratch_shapes=[pltpu.VMEM((tm + HALO, D), F32), pltpu.VMEM((tm, D), F32), pltpu.VMEM((HALO * 8, D), F32)], comm=comm)


def pool_bwd(dbr, wo, pooled, wpool, spool, name):
    s_len = dbr.shape[0]
    tm = min(TMB, s_len)

    def body(dbr_ref, wo_ref, pl_ref, wp_ref, sp_ref, dmr_ref, q_ref, dsp_ref, mr_s):
        i = pl.program_id(0)
        dpm = _dot_nt(dbr_ref[...], wo_ref[...].reshape(D, D))
        for gi in range(4):
            cs = slice(gi * POOL_GD, (gi + 1) * POOL_GD)
            mr_s[:, cs] = _dot(pl_ref[:, cs], wp_ref[gi])
        _accum(dsp_ref, i == 0, _colsum(dpm * mr_s[...]))
        dmr = (dpm * sp_ref[...]).astype(BF)
        dmr_ref[...] = dmr
        for gi, w in enumerate(POOL_WINDOWS):
            cs = slice(gi * POOL_GD, (gi + 1) * POOL_GD)
            q_ref[:, cs] = (_dot_nt(dmr[:, cs], wp_ref[gi]) * _inv_count(i, tm, w)).astype(BF)

    row = _cols(tm, D, 0)
    return pl.pallas_call(
        body, name=name, grid=(s_len // tm,),
        in_specs=[row, _rowsharded(BLK), row, _whole((4, POOL_GD, POOL_GD)), _whole((1, D))],
        out_specs=[row, row, _whole((1, D))],
        out_shape=[_sds((s_len, D), BF), _sds((s_len, D), BF), _sds((1, D), F32)],
        scratch_shapes=[pltpu.VMEM((tm, D), F32)], compiler_params=_params(1))(dbr, wo, pooled, wpool, spool)


def seq_bwd(dc, q, proj, wdw, win, dhn_in, h, dh1, gpre, name, comm=None):
    s_len = dc.shape[0]
    tm = min(TMB, s_len)
    nt = s_len // tm

    def body(dc_ref, dch_ref, q_ref, qh_ref, a_ref, g_ref, wdw_ref, w2, w3, w4, dhn_in_ref, h_ref, dh1_ref, gpre_ref,
             dz_ref, dh_ref, dgpre_ref, ext_s, dhc_s, qext_s):
        i = pl.program_id(0)
        last = i == nt - 1
        ext_s[0:tm, :] = dc_ref[...].astype(F32)
        ext_s[tm:, :] = jnp.where(last, 0.0, dch_ref[...].astype(F32))

        def chunk(ci, carry):
            r0 = pl.multiple_of((ci // (D // LC)) * RC, RC)
            l0 = pl.multiple_of((ci % (D // LC)) * LC, LC)
            win_ = ext_s[pl.ds(r0, RC + HALO), pl.ds(l0, LC)]
            acc = jnp.zeros((RC, LC), F32)
            for r in range(8):
                wr = win_ if r == 0 else pltpu.roll(win_, RC + HALO - r, 0)
                for m in range(4):
                    d = 8 * m + r
                    if d < CONV_W:
                        k = CONV_W - 1 - d
                        acc = acc + wdw_ref[k:k + 1, pl.ds(l0, LC)] * wr[8 * m:8 * m + RC]
            dhc_s[pl.ds(r0, RC), pl.ds(l0, LC)] = acc
            return carry
        lax.fori_loop(0, (tm // RC) * (D // LC), chunk, 0)

        dhc = dhc_s[...]
        av = a_ref[...].astype(F32)
        sg = _sig(g_ref[...].astype(F32))
        da = (dhc * sg).astype(BF)
        dg = (dhc * av * sg * (1.0 - sg)).astype(BF)
        dz_ref[:, 0:D] = da
        dz_ref[:, D:2 * D] = dg

        qext_s[0:tm, :] = q_ref[...].astype(F32)
        qext_s[tm:, :] = jnp.where(last, 0.0, qh_ref[...].astype(F32))
        for gi, w in enumerate(POOL_WINDOWS):
            cs = slice(gi * POOL_GD, (gi + 1) * POOL_GD)
            e = qext_s[:, cs]
            s = e
            sh = 1
            while sh < w:
                s = s + pltpu.roll(s, tm + HALO - sh, 0)
                sh *= 2
            dz_ref[:, 2 * D + gi * POOL_GD:2 * D + (gi + 1) * POOL_GD] = (s[0:tm] - e[0:tm] * _count(i, tm, w)).astype(BF)
        dhn = dhn_in_ref[...] + _dot_nt(da, w2[...]) + _dot_nt(dg, w3[...]) + _dot_nt(dz_ref[:, 2 * D:3 * D], w4[...])
        x = h_ref[...]
        r = _rstd(x)
        _accum(dgpre_ref, i == 0, _colsum(dhn * x * r))
        dh_ref[...] = dh1_ref[...] + _rms_bwd(x, dhn * gpre_ref[...], r)

    row = _cols(tm, D, 0)
    return _pcall(
        body, (dc, dc, q, q, proj, proj, wdw, win, win, win, dhn_in, h, dh1, gpre), name=name, grid=(nt,),
        in_specs=[row, _next_halo(tm, D, 0, s_len), row, _next_halo(tm, D, 0, s_len), _cols(tm, D, 2), _cols(tm, D, 3),
                  _whole((HALO, D)), _win_block(2), _win_block(3), _win_block(4), row, row, row, _whole((1, D))],
        out_specs=[_cols(tm, 3 * D, 0), row, _whole((1, D))],
        out_shape=[_sds((s_len, 3 * D), BF), _sds((s_len, D), F32), _sds((1, D), F32)],
        scratch_shapes=[pltpu.VMEM((tm + HALO, D), F32), pltpu.VMEM((tm, D), F32), pltpu.VMEM((tm + HALO, D), F32)], comm=comm)


def wgrad(a, b, tk, tn, name, stacked=False, diag=False, a_spec=None, comm=None):
    s_len = b.shape[0]
    k_dim = a.shape[-1]
    n_dim = b.shape[1]
    ts = min(TS, s_len)
    nk = 1 if diag else k_dim // tk
    nn, ns = n_dim // tn, s_len // ts

    def body(a_ref, b_ref, o_ref, acc):
        s = pl.program_id(2)
        _accum(acc, s == 0, _dot_tn(a_ref[...].astype(BF), b_ref[...].astype(BF)))

        @pl.when(s == ns - 1)
        def _():
            o_ref[...] = acc[...].astype(BF).reshape(o_ref.shape)

    if a_spec is None:
        a_spec = pl.BlockSpec((ts, tk), (lambda k, n, s: (s, n)) if diag else (lambda k, n, s: (s, k)))
    if stacked or diag:
        out_shape = _sds((nn, tk if diag else k_dim, tn), BF)
        o_spec = pl.BlockSpec((1, tk, tn), lambda k, n, s: (n, k, 0))
    else:
        out_shape = _sds((k_dim, n_dim), BF)
        o_spec = pl.BlockSpec((tk, tn), lambda k, n, s: (k, n))
    (out,), got = _pcall(
        body, (a, b), name=name, grid=(nk, nn, ns),
        in_specs=[a_spec, pl.BlockSpec((ts, tn), lambda k, n, s: (s, n))], out_specs=[o_spec], out_shape=[out_shape],
        scratch_shapes=[pltpu.VMEM((tk, tn), F32)], comm=comm)
    return out if comm is None else (out, got)


_WEIGHTS = ['g_mix_pre', 'w_in', 'w_sgu_s', 'b_sgu_s', 'g_sgu_v', 'b_sgu_v', 'w_sgu_out', 'w_dw', 'b_dw', 'g_conv_ln', 'b_conv_ln',
            'w_conv_out', 'w_pool', 's_pool', 'w_pool_out', 'w_out', 'g_mix_post', 'g_ffn_pre', 'w_ffn_in', 'w_ffn_out', 'g_ffn_post',
            'w_ple', 'w_ple_gate']
_SHARDED = ['w_in', 'w_sgu_out', 'w_conv_out', 'w_pool', 'w_pool_out', 'w_out', 'w_ffn_in', 'w_ffn_out', 'w_ple', 'w_ple_gate']
_VECTORS = ['g_mix_pre', 'g_sgu_v', 'b_sgu_v', 'b_dw', 'g_conv_ln', 'b_conv_ln', 's_pool', 'g_mix_post', 'g_ffn_pre', 'g_ffn_post']
_SUBLANES = 8
_SGU_ROWS = HEADS * BLK * BLK // D
_REP_ROWS = _SUBLANES * (len(_VECTORS) + 2) + _SGU_ROWS


def _pack_replicated(t, layer):
    rows = [jnp.pad(t[k][layer].reshape(1, D), ((0, _SUBLANES - 1), (0, 0))) for k in _VECTORS + ['b_sgu_s']]
    return jnp.concatenate(rows + [t['w_sgu_s'][layer].reshape(_SGU_ROWS, D), jnp.zeros((_SUBLANES, D), F32)], axis=0)


def _unpack_replicated(packed):
    out = {}
    for i, k in enumerate(_VECTORS):
        out[k] = packed[:, _SUBLANES * i, :]
    o = _SUBLANES * len(_VECTORS)
    out['b_sgu_s'] = packed[:, o, :].reshape(2, HEADS, BLK)
    out['w_sgu_s'] = packed[:, o + _SUBLANES:o + _SUBLANES + _SGU_ROWS, :].reshape(2, HEADS, BLK, BLK)
    return out


def _pad_taps(w):
    return jnp.pad(w, ((0, HALO - CONV_W), (0, 0)))


def kernel(x, p, g_mix_pre, w_in, w_sgu_s, b_sgu_s, g_sgu_v, b_sgu_v, w_sgu_out, w_dw, b_dw, g_conv_ln, b_conv_ln, w_conv_out, w_pool, s_pool, w_pool_out, w_out, g_mix_post, g_ffn_pre, w_ffn_in, w_ffn_out, g_ffn_post, w_ple, w_ple_gate, loss_target, m_g_mix_pre, m_w_in, m_w_sgu_s, m_b_sgu_s, m_g_sgu_v, m_b_sgu_v, m_w_sgu_out, m_w_dw, m_b_dw, m_g_conv_ln, m_b_conv_ln, m_w_conv_out, m_w_pool, m_s_pool, m_w_pool_out, m_w_out, m_g_mix_post, m_g_ffn_pre, m_w_ffn_in, m_w_ffn_out, m_g_ffn_post, m_w_ple, m_w_ple_gate, v_g_mix_pre, v_w_in, v_w_sgu_s, v_b_sgu_s, v_g_sgu_v, v_b_sgu_v, v_w_sgu_out, v_w_dw, v_b_dw, v_g_conv_ln, v_b_conv_ln, v_w_conv_out, v_w_pool, v_s_pool, v_w_pool_out, v_w_out, v_g_mix_post, v_g_ffn_pre, v_w_ffn_in, v_w_ffn_out, v_g_ffn_post, v_w_ple, v_w_ple_gate):
    W = dict(g_mix_pre=g_mix_pre, w_in=w_in, w_sgu_s=w_sgu_s, b_sgu_s=b_sgu_s, g_sgu_v=g_sgu_v, b_sgu_v=b_sgu_v, w_sgu_out=w_sgu_out,
             w_dw=w_dw, b_dw=b_dw, g_conv_ln=g_conv_ln, b_conv_ln=b_conv_ln, w_conv_out=w_conv_out, w_pool=w_pool, s_pool=s_pool,
             w_pool_out=w_pool_out, w_out=w_out, g_mix_post=g_mix_post, g_ffn_pre=g_ffn_pre, w_ffn_in=w_ffn_in, w_ffn_out=w_ffn_out,
             g_ffn_post=g_ffn_post, w_ple=w_ple, w_ple_gate=w_ple_gate)
    M = dict(g_mix_pre=m_g_mix_pre, w_in=m_w_in, w_sgu_s=m_w_sgu_s, b_sgu_s=m_b_sgu_s, g_sgu_v=m_g_sgu_v, b_sgu_v=m_b_sgu_v,
             w_sgu_out=m_w_sgu_out, w_dw=m_w_dw, b_dw=m_b_dw, g_conv_ln=m_g_conv_ln, b_conv_ln=m_b_conv_ln, w_conv_out=m_w_conv_out,
             w_pool=m_w_pool, s_pool=m_s_pool, w_pool_out=m_w_pool_out, w_out=m_w_out, g_mix_post=m_g_mix_post, g_ffn_pre=m_g_ffn_pre,
             w_ffn_in=m_w_ffn_in, w_ffn_out=m_w_ffn_out, g_ffn_post=m_g_ffn_post, w_ple=m_w_ple, w_ple_gate=m_w_ple_gate)
    V = dict(g_mix_pre=v_g_mix_pre, w_in=v_w_in, w_sgu_s=v_w_sgu_s, b_sgu_s=v_b_sgu_s, g_sgu_v=v_g_sgu_v, b_sgu_v=v_b_sgu_v,
             w_sgu_out=v_w_sgu_out, w_dw=v_w_dw, b_dw=v_b_dw, g_conv_ln=v_g_conv_ln, b_conv_ln=v_b_conv_ln, w_conv_out=v_w_conv_out,
             w_pool=v_w_pool, s_pool=v_s_pool, w_pool_out=v_w_pool_out, w_out=v_w_out, g_mix_post=v_g_mix_post, g_ffn_pre=v_g_ffn_pre,
             w_ffn_in=v_w_ffn_in, w_ffn_out=v_w_ffn_out, g_ffn_post=v_g_ffn_post, w_ple=v_w_ple, w_ple_gate=v_w_ple_gate)

    my_c = lax.axis_index("c")
    core_id = my_c.astype(jnp.int32).reshape(1)
    my_chip = 2 * lax.axis_index("x") + lax.axis_index("y")
    my_dev = 2 * my_chip + my_c
    s_len = x.shape[1]
    h0 = x.reshape(s_len, D)
    target = loss_target.reshape(s_len, D)

    shard = [{k: W[k][l].astype(BF) for k in _SHARDED} for l in range(2)]
    for l in range(2):
        shard[l]['w_dw'] = w_dw[l]
        shard[l]['w_ffn_in'] = jnp.swapaxes(w_ffn_in[l], 0, 1).astype(BF)
    mixer_w = ['w_sgu_out', 'w_conv_out', 'w_pool', 'w_pool_out', 'w_out', 'w_dw']
    ffn_w = ['w_ffn_in', 'w_ffn_out', 'w_ple', 'w_ple_gate']
    hosted_gather = {
        'norm_proj_in': (0, mixer_w),
        'conv_fwd': (0, ffn_w),
        'merge_out': (1, mixer_w),
        'norm_proj_ffn': (1, ['w_in']),
        'ffn_out': (1, ffn_w),
    }
    G = [{'w_in': run_comm(Gather([shard[0]['w_in']]), "gather_w_in_0")[0]}, {}]

    def gather_in(layer, call):
        if layer != 0:
            return None, (lambda got: None)
        to_layer, keys = hosted_gather[call]
        return Gather([shard[to_layer][k] for k in keys]), (lambda got: G[to_layer].update(zip(keys, got)))

    def natural_mixer(g):
        wpool = jnp.transpose(g['w_pool'], (1, 0, 2, 3)).reshape(4, POOL_GD, POOL_GD)
        wdw = jnp.transpose(g['w_dw'].reshape(N_DEV, CONV_W, BLK), (1, 0, 2)).reshape(CONV_W, D)
        return dict(wpool=wpool, wdw=_pad_taps(wdw))

    def natural_ffn(g):
        wfit = g['w_ffn_in'].reshape(4, D_FF // 2, D)
        wple = jnp.transpose(g['w_ple'], (1, 0, 2)).reshape(PLE, D)
        return dict(wfit=wfit, wple=wple)

    def vec(name, layer):
        return W[name][layer].reshape(1, D)

    saved = []
    h = h0
    for l in range(2):
        g = G[l]
        comm, land = gather_in(l, 'norm_proj_in')
        (proj, hn), got = norm_proj(h, vec('g_mix_pre', l), g['w_in'], f"norm_proj_in_{l}", comm=comm)
        land(got)
        nat = natural_mixer(g)
        bsfull = jnp.repeat(b_sgu_s[l].T, BLK, axis=1)
        wst = jnp.swapaxes(w_sgu_s[l], 1, 2)
        mixed, sgu, bra = sgu_fwd(proj, w_sgu_s[l], bsfull, vec('g_sgu_v', l), vec('b_sgu_v', l), g['w_sgu_out'], f"sgu_fwd_{l}")
        comm, land = gather_in(l, 'conv_fwd')
        (conv, cb, brb), got = conv_fwd(proj, nat['wdw'], vec('b_dw', l), vec('g_conv_ln', l), vec('b_conv_ln', l), g['w_conv_out'],
                                        f"conv_fwd_{l}", comm=comm)
        land(got)
        nat.update(natural_ffn(g))
        pooled, pm, brc = pool_fwd(proj, nat['wpool'], vec('s_pool', l), g['w_pool_out'], f"pool_fwd_{l}")
        comm, land = gather_in(l, 'merge_out')
        (merged, mo, h1), got = merge_out(proj, bra, brb, brc, h, g['w_out'], vec('g_mix_post', l), f"merge_out_{l}", comm=comm)
        land(got)
        comm, land = gather_in(l, 'norm_proj_ffn')
        (ff, hn2), got = norm_proj(h1, vec('g_ffn_pre', l), nat['wfit'], f"norm_proj_ffn_{l}", transposed=True, comm=comm)
        land(got)
        comm, land = gather_in(l, 'ffn_out')
        (act, f, h2, pg, h3), got = ffn_out(ff, h1, p, g['w_ffn_out'], vec('g_ffn_post', l), g['w_ple_gate'], nat['wple'], l,
                                            f"ffn_out_{l}", comm=comm)
        land(got)
        saved.append(dict(h=h, nat=nat, wst=wst, proj=proj, hn=hn, mixed=mixed, sgu=sgu, bra=bra, conv=conv, cb=cb, brb=brb,
                          pooled=pooled, pm=pm, brc=brc, merged=merged, mo=mo, h1=h1, ff=ff, hn2=hn2, act=act, f=f, h2=h2, pg=pg))
        h = h3

    dh = h

    parts = {k: [None, None] for k in _SHARDED}
    small = {k: [None, None] for k in _VECTORS + ['b_sgu_s', 'w_sgu_s', 'w_dw']}
    chip_parts = [{}, {}]
    from_chips = [{}, {}]
    gathered_small = [None, None]

    def to_sibling(layer, keys):
        return ToSibling([parts[k][layer] for k in keys])

    def add_siblings(layer, keys, from_sibling):
        for k, rv in zip(keys, from_sibling):
            st = parts[k][layer]
            cols = st.shape[-1]
            chip_parts[layer][k] = add_core_side(st.reshape(N_DEV, -1, cols), rv.reshape(4, -1, cols), core_id,
                                                 f"rs_add_{k}_{layer}").reshape(rv.shape)

    def to_chips(layer, keys):
        return ToChips([chip_parts[layer][k] for k in keys])

    def small_pack(layer):
        return jnp.concatenate([_pack_replicated(small, layer), small['w_dw'][layer]], axis=0)

    ffn_group = ['w_ffn_in', 'w_ffn_out', 'w_ple_gate', 'w_ple']
    mix_group = ['w_out', 'w_sgu_out', 'w_conv_out', 'w_pool_out', 'w_pool']
    big = ['w_in', 'w_ffn_in']
    others = [k for k in _SHARDED if k not in big]
    hosted_rs = {
        'ple_ffn_bwd': (lambda: to_sibling(1, _SHARDED), lambda got: add_siblings(1, _SHARDED, got)),
        'mix_post_bwd': (lambda: Both(to_sibling(0, ffn_group), Gather([small_pack(1)])),
                         lambda got: (add_siblings(0, ffn_group, got[:-1]), gathered_small.__setitem__(1, got[-1]))),
        'ffn_in_bwd': (lambda: to_chips(1, others), lambda got: from_chips[1].update(zip(others, got))),
        'sgu_bwd': (lambda: to_chips(1, big), lambda got: from_chips[1].update(zip(big, got))),
        'conv_bwd': (lambda: to_chips(0, ffn_group), lambda got: from_chips[0].update(zip(ffn_group, got))),
        'wgrad_in_sgu': (lambda: to_sibling(0, mix_group), lambda got: add_siblings(0, mix_group, got)),
        'wgrad_in_seq': (lambda: to_chips(0, mix_group), lambda got: from_chips[0].update(zip(mix_group, got))),
        'wgrad_in_gate': (lambda: Gather([small_pack(0)]), lambda got: gathered_small.__setitem__(0, got[0])),
    }

    def exchange_in(layer, call):
        if layer != 0 or call not in hosted_rs:
            return None, (lambda got: None)
        make, land = hosted_rs[call]
        return make(), land

    for l in (1, 0):
        sv, g, nat = saved[l], G[l], saved[l]['nat']

        def wg(call, a, b, tk, tn, **kw):
            comm, land = exchange_in(l, call)
            if comm is None:
                return wgrad(a, b, tk, tn, f"{call}_{l}", **kw)
            out, got = wgrad(a, b, tk, tn, f"{call}_{l}", comm=comm, **kw)
            land(got)
            return out

        comm, land = exchange_in(l, 'ple_ffn_bwd')
        res, got = ple_ffn_bwd(
            dh, sv['pg'], p, sv['f'], sv['ff'], nat['wple'], g['w_ple_gate'], g['w_ffn_out'], vec('g_ffn_post', l), l,
            f"ple_ffn_bwd_{l}", target=target if l == 1 else None, comm=comm)
        land(got)
        dh2, dpe, dpg, df, dff, small['g_ffn_post'][l] = res[:6]
        if l == 1:
            loss = lax.psum(res[6][0, 0], ("x", "y", "c"))
        comm, land = exchange_in(l, 'ffn_in_bwd')
        (dh1, small['g_ffn_pre'][l]), got = ffn_in_bwd(dff, nat['wfit'], sv['h1'], dh2, vec('g_ffn_pre', l), f"ffn_in_bwd_{l}",
                                                       comm=comm)
        land(got)
        p_spec = pl.BlockSpec((None, None, min(TS, s_len), PLE), functools.partial(lambda k, n, s, ll: (ll, 0, s, 0), ll=l))
        parts['w_ple'][l] = jnp.transpose(wg('wgrad_ple', p, dpe, PLE, D, a_spec=p_spec).reshape(PLE, N_DEV, BLK), (1, 0, 2))
        parts['w_ple_gate'][l] = wg('wgrad_ple_gate', sv['h2'], dpg, D, D).reshape(N_DEV, BLK, D)
        parts['w_ffn_out'][l] = wg('wgrad_ffn_out', sv['act'], df, D_FF // 2, D).reshape(N_DEV, D_FF // N_DEV, D)
        parts['w_ffn_in'][l] = wg('wgrad_ffn_in', dff, sv['hn2'], D_FF // 2, D).reshape(N_DEV, D_FF // 4, D)

        comm, land = exchange_in(l, 'mix_post_bwd')
        (dmo, dbra, dbrb, dbrc, dzg, dhn_g, small['g_mix_post'][l]), got = mix_post_bwd(
            dh1, sv['mo'], vec('g_mix_post', l), g['w_out'], sv['proj'], sv['bra'], sv['brb'], sv['brc'], g['w_in'],
            f"mix_post_bwd_{l}", comm=comm)
        land(got)
        comm, land = exchange_in(l, 'sgu_bwd')
        (dzs, dhn_ag, dws, dbs, small['g_sgu_v'][l], small['b_sgu_v'][l]), got = sgu_bwd(
            dbra, g['w_sgu_out'], sv['proj'], sv['mixed'], sv['wst'], vec('g_sgu_v', l), vec('b_sgu_v', l), g['w_in'],
            dhn_g, f"sgu_bwd_{l}", comm=comm)
        land(got)
        small['w_sgu_s'][l] = dws
        small['b_sgu_s'][l] = dbs.T
        comm, land = exchange_in(l, 'conv_bwd')
        (dc, dwdw, small['b_dw'][l], small['g_conv_ln'][l], small['b_conv_ln'][l]), got = conv_bwd(
            dbrb, g['w_conv_out'], sv['proj'], sv['conv'], vec('g_conv_ln', l), vec('b_conv_ln', l), f"conv_bwd_{l}", comm=comm)
        land(got)
        small['w_dw'][l] = dwdw
        dmr, q, small['s_pool'][l] = pool_bwd(dbrc, g['w_pool_out'], sv['pooled'], nat['wpool'], vec('s_pool', l), f"pool_bwd_{l}")
        comm, land = exchange_in(l, 'seq_bwd')
        (dzc, dh, small['g_mix_pre'][l]), got = seq_bwd(dc, q, sv['proj'], nat['wdw'], g['w_in'], dhn_ag, sv['h'], dh1,
                                                        vec('g_mix_pre', l), f"seq_bwd_{l}", comm=comm)
        land(got)

        parts['w_out'][l] = wg('wgrad_out', sv['merged'], dmo, D, D).reshape(N_DEV, BLK, D)
        parts['w_sgu_out'][l] = wg('wgrad_sgu_out', sv['sgu'], dbra, D, D).reshape(N_DEV, BLK, D)
        parts['w_conv_out'][l] = wg('wgrad_conv_out', sv['cb'], dbrb, D, D).reshape(N_DEV, BLK, D)
        parts['w_pool_out'][l] = wg('wgrad_pool_out', sv['pm'], dbrc, D, D).reshape(N_DEV, BLK, D)
        g_pool = wg('wgrad_pool', sv['pooled'], dmr, POOL_GD, POOL_GD, diag=True)
        parts['w_pool'][l] = jnp.transpose(g_pool.reshape(4, N_DEV, POOL_GD // N_DEV, POOL_GD), (1, 0, 2, 3))
        parts['w_in'][l] = jnp.concatenate([
            wg('wgrad_in_sgu', sv['hn'], dzs, D, D, stacked=True),
            wg('wgrad_in_seq', sv['hn'], dzc, D, D, stacked=True),
            wg('wgrad_in_gate', sv['hn'], dzg, D, D, stacked=True)], axis=0)
    grad_x = dh.reshape(1, s_len, D)

    add_siblings(0, ['w_in'], run_comm(to_sibling(0, ['w_in']), "rs_to_sibling_w_in_0"))
    from_chips[0]['w_in'] = run_comm(to_chips(0, ['w_in']), "rs_to_chips_w_in_0")[0]

    outs = {}
    for k in _SHARDED:
        wmv = [jnp.swapaxes(t[k], 1, 2) if k == 'w_ffn_in' else t[k] for t in (W, M, V)]
        cols = wmv[0].shape[-1]
        pieces = []
        for layer in range(2):
            own = lax.dynamic_index_in_dim(chip_parts[layer][k], my_chip, axis=0, keepdims=False).reshape(-1, cols)
            rv3 = from_chips[layer][k].reshape(3, -1, cols)
            pieces.append([(own, None), (rv3, 0), (rv3, 1), (rv3, 2)])
        res = adamw_layers(*[t.reshape(2, -1, cols) for t in wmv], pieces, f"adamw_{k}")
        res = [r.reshape(wmv[0].shape) for r in res]
        outs[k] = [jnp.swapaxes(r, 1, 2) for r in res] if k == 'w_ffn_in' else res

    packed = [jnp.stack([_pack_replicated(t, 0), _pack_replicated(t, 1)], axis=0) for t in (W, M, V)]
    rep_res = adamw_layers(*packed, [[(gathered_small[layer], d) for d in range(N_DEV)] for layer in range(2)], "adamw_replicated")
    for idx, res in enumerate(rep_res):
        for name, val in _unpack_replicated(res).items():
            outs.setdefault(name, [None] * 4)[idx] = val
    dw_sum = jnp.stack([sum_slabs(gathered_small[layer][:, _REP_ROWS:, :], f"sum_w_dw_{layer}") for layer in range(2)], axis=0)
    dw_mine = lax.dynamic_slice_in_dim(dw_sum[:, :CONV_W], my_dev * BLK, BLK, axis=2)
    res = adamw(w_dw.reshape(2 * CONV_W, BLK), m_w_dw.reshape(2 * CONV_W, BLK), v_w_dw.reshape(2 * CONV_W, BLK),
                [(dw_mine.reshape(2 * CONV_W, BLK), None)], "adamw_w_dw")
    outs['w_dw'] = [r.reshape(w_dw.shape) for r in res]

    result = [loss, grad_x]
    for idx in range(4):
        result += [outs[k][idx] for k in _WEIGHTS]
    return tuple(result)
```

```python
import functools
import math

import jax
import jax.numpy as jnp
from jax import lax
from jax.experimental import pallas as pl
from jax.experimental.pallas import tpu as pltpu

F32 = jnp.float32
BF = jnp.bfloat16

D = 1024
D_FF = 2816
PLE = 256
N_DEV = 8
HEADS = 8
BLK = 128
CHUNK = 64
CONV_W = 31
POOL_WINDOWS = (2, 4, 8, 16)
POOL_GD = 256
EPS = 1e-6

V7X_VMEM_BYTES = 64 * 2**20
VMEM_LIMIT = V7X_VMEM_BYTES * 7 // 8
HALO = 32
RC = 64
LC = 128
TM = 512
TMB = 512
TMW = 256
TMP = 1024
TS_MAX = 4096
WGRAD_VMEM_BUDGET = 44 * 2**20

ADAM_LR, ADAM_B1, ADAM_B2, ADAM_EPS, ADAM_WD, ADAM_STEP = 0.001, 0.9, 0.999, 1e-08, 0.01, 10

MESH = pl.DeviceIdType.MESH
ANY = pl.BlockSpec(memory_space=pl.ANY)

_GELU_K0 = math.sqrt(2.0 / math.pi)
_GELU_K1 = 0.044715
_LOG2E = 1.4426950408889634


def _dot(a, b):
    return jnp.dot(a, b, preferred_element_type=F32)


def _dot_nt(a, b):
    return lax.dot_general(a, b, (((1,), (1,)), ((), ())), preferred_element_type=F32)


def _dot_tn(a, b):
    return lax.dot_general(a, b, (((0,), (0,)), ((), ())), preferred_element_type=F32)


def _sig(x):
    return 1.0 / (1.0 + jnp.exp2(x * (-_LOG2E)))


def _gelu(x):
    s = 1.0 / (1.0 + jnp.exp2(x * ((-2.0 * _GELU_K0 * _LOG2E) + (-2.0 * _GELU_K0 * _GELU_K1 * _LOG2E) * (x * x))))
    return x * s, s


def _gelu_grad(x, s):
    return s + x * s * (1.0 - s) * ((2.0 * _GELU_K0) + (6.0 * _GELU_K0 * _GELU_K1) * (x * x))


def _rstd(x):
    return lax.rsqrt(jnp.mean(x * x, axis=-1, keepdims=True) + EPS)


def _rms_bwd(x, gd, r):
    return r * gd - x * (r * r * r) * jnp.mean(gd * x, axis=-1, keepdims=True)


def _ln_fwd(x):
    mu = jnp.mean(x, axis=-1, keepdims=True)
    xc = x - mu
    rs = lax.rsqrt(jnp.mean(xc * xc, axis=-1, keepdims=True) + EPS)
    return xc * rs, rs


def _ln_bwd(dhat, hat, rs):
    return rs * (dhat - jnp.mean(dhat, axis=-1, keepdims=True) - hat * jnp.mean(dhat * hat, axis=-1, keepdims=True))


def _colsum(x):
    return jnp.sum(x, axis=0, keepdims=True)


def _accum(ref, first, val):
    @pl.when(first)
    def _():
        ref[...] = val

    @pl.when(jnp.logical_not(first))
    def _():
        ref[...] += val


def _sgu_mask(transposed):
    r = lax.broadcasted_iota(jnp.int32, (BLK, BLK), 0) // CHUNK
    c = lax.broadcasted_iota(jnp.int32, (BLK, BLK), 1) // CHUNK
    return (r <= c) if transposed else (c <= r)


def _inv_count(i, tm, w):
    t = lax.broadcasted_iota(jnp.int32, (tm, 1), 0) + i * tm
    return 1.0 / jnp.minimum(t + 1, w).astype(F32)


def _count(i, tm, w):
    t = lax.broadcasted_iota(jnp.int32, (tm, 1), 0) + i * tm
    return jnp.minimum(t + 1, w).astype(F32)


def _params(n_grid):
    return pltpu.CompilerParams(dimension_semantics=("arbitrary",) * n_grid, vmem_limit_bytes=VMEM_LIMIT)


def _sds(shape, dtype):
    return jax.ShapeDtypeStruct(shape, dtype)


def _cols(tm, width, cb):
    return pl.BlockSpec((tm, width), lambda i: (i, cb))


def _whole(shape):
    nd = len(shape)
    return pl.BlockSpec(shape, lambda i: (0,) * nd)


def _prev_halo(tm, width, cb):
    return pl.BlockSpec((HALO, width), lambda i: (jnp.maximum(i * (tm // HALO) - 1, 0), cb))


def _next_halo(tm, width, cb, s_len):
    last = s_len // HALO - 1
    return pl.BlockSpec((HALO, width), lambda i: (jnp.minimum((i + 1) * (tm // HALO), last), cb))


def _rowsharded(rows):
    return pl.BlockSpec((N_DEV, rows, D), lambda i: (0, 0, 0))


def _win_block(j):
    return pl.BlockSpec((None, D, D), lambda i: (j, 0, 0))


def _row_tile(rows, cap):
    t = min(rows, cap)
    while rows % t or t % 16:
        t -= 16
    return t


class Gather:
    def __init__(self, arrs):
        self.arrs = list(arrs)
        n = self.n = len(self.arrs)
        self.out_shape = [_sds((N_DEV,) + a.shape, a.dtype) for a in self.arrs]
        self.scratch = [pltpu.SemaphoreType.DMA((n, 7)), pltpu.SemaphoreType.DMA((n, 7)), pltpu.SemaphoreType.DMA((n,))]

    def _plan(self, ins, outs, sems):
        send, recv, local = sems
        x, y, c = lax.axis_index("x"), lax.axis_index("y"), lax.axis_index("c")
        me, sibling = (x, y, c), (x, y, 1 - c)
        chips = [(1 - x, y), (x, 1 - y), (1 - x, 1 - y)]

        def copy(a, k, block, to, src=None):
            dst = outs[a].at[4 * block[0] + 2 * block[1] + block[2]]
            return pltpu.make_async_remote_copy(
                src_ref=dst if src is None else src, dst_ref=dst, send_sem=send.at[a, k], recv_sem=recv.at[a, k],
                device_id=to, device_id_type=MESH)

        mine = [pltpu.make_async_copy(ins[a], outs[a].at[4 * x + 2 * y + c], local.at[a]) for a in range(self.n)]
        first = []
        for a in range(self.n):
            first.append(copy(a, 0, me, sibling, src=ins[a]))
            first += [copy(a, 1 + j, me, (*chip, c), src=ins[a]) for j, chip in enumerate(chips)]
        return me, sibling, chips, c, copy, mine, first

    def start(self, ins, outs, sems):
        *_, mine, first = self._plan(ins, outs, sems)
        for cp in mine + first:
            cp.start()

    def finish(self, ins, outs, sems):
        me, sibling, chips, c, copy, mine, first = self._plan(ins, outs, sems)
        passed = []
        for a in range(self.n):
            for j, chip in enumerate(chips):
                copy(a, 1 + j, (*chip, c), me).wait_recv()
                fwd = copy(a, 4 + j, (*chip, c), sibling)
                fwd.start()
                passed.append(fwd)
        for a in range(self.n):
            copy(a, 0, sibling, me).wait_recv()
            for j, chip in enumerate(chips):
                copy(a, 4 + j, (*chip, 1 - c), me).wait_recv()
        for cp in first + passed:
            cp.wait_send()
        for cp in mine:
            cp.wait()


class ToSibling:
    def __init__(self, parts):
        self.arrs = list(parts)
        n = self.n = len(self.arrs)
        self.out_shape = [_sds((4,) + p.shape[1:], p.dtype) for p in self.arrs]
        self.scratch = [pltpu.SemaphoreType.DMA((n,)), pltpu.SemaphoreType.DMA((n,))]

    def start(self, ins, outs, sems):
        send, recv = sems
        x, y, c = lax.axis_index("x"), lax.axis_index("y"), lax.axis_index("c")
        for a in range(self.n):
            for q in range(4):
                pltpu.make_async_remote_copy(
                    src_ref=ins[a].at[2 * q + 1 - c], dst_ref=outs[a].at[q], send_sem=send.at[a], recv_sem=recv.at[a],
                    device_id=(x, y, 1 - c), device_id_type=MESH).start()

    def finish(self, ins, outs, sems):
        send, recv = sems
        x, y, c = lax.axis_index("x"), lax.axis_index("y"), lax.axis_index("c")
        for a in range(self.n):
            pltpu.make_async_remote_copy(
                src_ref=outs[a], dst_ref=outs[a], send_sem=send.at[a], recv_sem=recv.at[a],
                device_id=(x, y, 1 - c), device_id_type=MESH).wait()


class ToChips:
    def __init__(self, cps):
        self.arrs = list(cps)
        n = self.n = len(self.arrs)
        self.out_shape = [_sds((3,) + p.shape[1:], p.dtype) for p in self.arrs]
        self.scratch = [pltpu.SemaphoreType.DMA((n,)), pltpu.SemaphoreType.DMA((n,))]

    def start(self, ins, outs, sems):
        send, recv = sems
        x, y, c = lax.axis_index("x"), lax.axis_index("y"), lax.axis_index("c")
        for a in range(self.n):
            for r, (px, py) in enumerate([(1 - x, y), (x, 1 - y), (1 - x, 1 - y)]):
                pltpu.make_async_remote_copy(
                    src_ref=ins[a].at[2 * px + py], dst_ref=outs[a].at[r], send_sem=send.at[a], recv_sem=recv.at[a],
                    device_id=(px, py, c), device_id_type=MESH).start()

    def finish(self, ins, outs, sems):
        send, recv = sems
        x, y, c = lax.axis_index("x"), lax.axis_index("y"), lax.axis_index("c")
        for a in range(self.n):
            pltpu.make_async_remote_copy(
                src_ref=outs[a], dst_ref=outs[a], send_sem=send.at[a], recv_sem=recv.at[a],
                device_id=(x, y, c), device_id_type=MESH).wait()


class Both:
    def __init__(self, a, b):
        self.a, self.b = a, b
        self.arrs, self.n = a.arrs + b.arrs, a.n + b.n
        self.out_shape, self.scratch = a.out_shape + b.out_shape, a.scratch + b.scratch

    def _each(self, ins, outs, sems):
        na, ns = self.a.n, len(self.a.scratch)
        return (self.a, (ins[:na], outs[:na], sems[:ns])), (self.b, (ins[na:], outs[na:], sems[ns:]))

    def start(self, ins, outs, sems):
        for comm, refs in self._each(ins, outs, sems):
            comm.start(*refs)

    def finish(self, ins, outs, sems):
        for comm, refs in self._each(ins, outs, sems):
            comm.finish(*refs)


def run_comm(comm, name):
    n = comm.n

    def body(*refs):
        ins, outs, sems = refs[:n], refs[n:2 * n], refs[2 * n:]
        comm.start(ins, outs, sems)
        comm.finish(ins, outs, sems)

    return pl.pallas_call(body, name=name, out_shape=comm.out_shape, in_specs=[ANY] * n, out_specs=[ANY] * n,
                          scratch_shapes=comm.scratch)(*comm.arrs)


def _pcall(body, args, *, name, grid, in_specs, out_specs, out_shape, scratch_shapes=(), comm=None):
    params = _params(len(grid))
    scratch_shapes = list(scratch_shapes)
    if comm is None:
        outs = pl.pallas_call(body, name=name, grid=grid, in_specs=in_specs, out_specs=out_specs, out_shape=out_shape,
                              scratch_shapes=scratch_shapes, compiler_params=params)(*args)
        return outs, None
    n_in, n_out, n_scr, nc = len(in_specs), len(out_specs), len(scratch_shapes), comm.n

    def hosted(*refs):
        ins, cins = refs[:n_in], refs[n_in:n_in + nc]
        o0 = n_in + nc
        outs, couts = refs[o0:o0 + n_out], refs[o0 + n_out:o0 + n_out + nc]
        s0 = o0 + n_out + nc
        scr, csems = refs[s0:s0 + n_scr], refs[s0 + n_scr:]
        first = pl.program_id(0) == 0
        last = pl.program_id(0) == grid[0] - 1
        for ax in range(1, len(grid)):
            first = jnp.logical_and(first, pl.program_id(ax) == 0)
            last = jnp.logical_and(last, pl.program_id(ax) == grid[ax] - 1)

        @pl.when(first)
        def _():
            comm.start(cins, couts, csems)
        body(*ins, *outs, *scr)

        @pl.when(last)
        def _():
            comm.finish(cins, couts, csems)

    res = pl.pallas_call(
        hosted, name=name, grid=grid, in_specs=list(in_specs) + [ANY] * nc, out_specs=list(out_specs) + [ANY] * nc,
        out_shape=list(out_shape) + comm.out_shape, scratch_shapes=scratch_shapes + comm.scratch,
        compiler_params=params)(*args, *comm.arrs)
    return res[:n_out], res[n_out:]


def add_core_side(parts, from_sibling, core, name):
    _, rows, cols = parts.shape
    tr = _row_tile(rows, 512)

    def body(c_ref, a_ref, b_ref, o_ref):
        o_ref[...] = (a_ref[...].astype(F32) + b_ref[...].astype(F32)).astype(o_ref.dtype)

    side = pl.BlockSpec((None, tr, cols), lambda q, i, c: (q, i, 0))
    return pl.pallas_call(
        body, name=name,
        grid_spec=pltpu.PrefetchScalarGridSpec(
            num_scalar_prefetch=1, grid=(4, rows // tr),
            in_specs=[pl.BlockSpec((None, tr, cols), lambda q, i, c: (2 * q + c[0], i, 0)), side], out_specs=side),
        out_shape=_sds(from_sibling.shape, BF), compiler_params=_params(2))(core, parts, from_sibling)


def adamw(w, m, v, pieces, name):
    rows, cols = w.shape
    tr = _row_tile(rows, 256) if rows % 16 == 0 else rows
    np_ = len(pieces)
    c1 = 1.0 / (1.0 - ADAM_B1 ** ADAM_STEP)
    c2 = 1.0 / (1.0 - ADAM_B2 ** ADAM_STEP)

    def body(*refs):
        w_ref, m_ref, v_ref = refs[:3]
        p_refs = refs[3:3 + np_]
        g_ref, d_ref, nm_ref, nv_ref = refs[3 + np_:]
        g = p_refs[0][...].astype(F32)
        for pr in p_refs[1:]:
            g = g + pr[...].astype(F32)
        nm = ADAM_B1 * m_ref[...] + (1.0 - ADAM_B1) * g
        nv = ADAM_B2 * v_ref[...] + (1.0 - ADAM_B2) * (g * g)
        g_ref[...] = g
        nm_ref[...] = nm
        nv_ref[...] = nv
        d_ref[...] = -ADAM_LR * ((nm * c1) / (jnp.sqrt(nv * c2) + ADAM_EPS) + ADAM_WD * w_ref[...])

    spec = pl.BlockSpec((tr, cols), lambda i: (i, 0))
    p_specs = []
    for arr, k in pieces:
        if k is None:
            p_specs.append(spec)
        else:
            p_specs.append(pl.BlockSpec((None, tr, cols), functools.partial(lambda i, kk: (kk, i, 0), kk=k)))
    out = _sds(w.shape, F32)
    return pl.pallas_call(body, name=name, grid=(rows // tr,), in_specs=[spec] * 3 + p_specs, out_specs=[spec] * 4,
                          out_shape=[out] * 4, compiler_params=_params(1))(w, m, v, *[a for a, _ in pieces])


def adamw_layers(w, m, v, pieces, name):
    _, rows, cols = w.shape
    tr = _row_tile(rows, 256)
    nt = rows // tr
    counts = [len(pieces[0]), len(pieces[1])]
    c1 = 1.0 / (1.0 - ADAM_B1 ** ADAM_STEP)
    c2 = 1.0 / (1.0 - ADAM_B2 ** ADAM_STEP)

    def body(*refs):
        w_ref, m_ref, v_ref = refs[:3]
        p_refs = refs[3:3 + sum(counts)]
        g_ref, d_ref, nm_ref, nv_ref = refs[3 + sum(counts):]
        sums = []
        for group in (p_refs[:counts[0]], p_refs[counts[0]:]):
            s = group[0][...].astype(F32)
            for pr in group[1:]:
                s = s + pr[...].astype(F32)
            sums.append(s)
        g = jnp.where(pl.program_id(0) == 0, sums[0], sums[1])
        nm = ADAM_B1 * m_ref[...] + (1.0 - ADAM_B1) * g
        nv = ADAM_B2 * v_ref[...] + (1.0 - ADAM_B2) * (g * g)
        g_ref[...] = g
        nm_ref[...] = nm
        nv_ref[...] = nv
        d_ref[...] = -ADAM_LR * ((nm * c1) / (jnp.sqrt(nv * c2) + ADAM_EPS) + ADAM_WD * w_ref[...])

    def rows_of(layer):
        parked = nt - 1 if layer == 0 else 0
        return lambda l, i: jnp.where(l == layer, i, parked)

    spec = pl.BlockSpec((None, tr, cols), lambda l, i: (l, i, 0))
    p_specs, p_args = [], []
    for layer in (0, 1):
        row_of = rows_of(layer)
        for arr, k in pieces[layer]:
            p_args.append(arr)
            if k is None:
                p_specs.append(pl.BlockSpec((tr, cols), functools.partial(lambda l, i, f: (f(l, i), 0), f=row_of)))
            else:
                p_specs.append(pl.BlockSpec((None, tr, cols), functools.partial(lambda l, i, f, kk: (kk, f(l, i), 0), f=row_of, kk=k)))
    out = _sds(w.shape, F32)
    return pl.pallas_call(body, name=name, grid=(2, nt), in_specs=[spec] * 3 + p_specs, out_specs=[spec] * 4,
                          out_shape=[out] * 4, compiler_params=_params(2))(w, m, v, *p_args)


def sum_slabs(g, name):
    n, rows, cols = g.shape

    def body(g_ref, o_ref):
        s = g_ref[0]
        for k in range(1, n):
            s = s + g_ref[k]
        o_ref[...] = s

    return pl.pallas_call(body, name=name, out_shape=_sds((rows, cols), F32))(g)


def norm_proj(h, g, w, name, transposed=False, comm=None):
    s_len = h.shape[0]
    nb, tn = (w.shape[0], w.shape[1]) if transposed else (w.shape[0], w.shape[2])
    tm = min(TMP, s_len)
    nt = s_len // tm
    matmul = _dot_nt if transposed else _dot

    def body(h_ref, g_ref, w_ref, o_ref, hn_ref, hn_s):
        rows = pl.ds(pl.multiple_of(pl.program_id(1) * tm, tm), tm)

        @pl.when(pl.program_id(0) == 0)
        def _():
            x = h_ref[...]
            hn = (x * _rstd(x) * g_ref[...]).astype(BF)
            hn_s[rows, :] = hn
            hn_ref[...] = hn
        o_ref[...] = matmul(hn_s[rows, :], w_ref[...]).astype(BF)

    def first_pass_rows(j, i):
        return (jnp.where(j == 0, i, nt - 1), 0)

    return _pcall(
        body, (h, g, w), name=name, grid=(nb, nt),
        in_specs=[pl.BlockSpec((tm, D), first_pass_rows), pl.BlockSpec((1, D), lambda j, i: (0, 0)),
                  pl.BlockSpec((None,) + w.shape[1:], lambda j, i: (j, 0, 0))],
        out_specs=[pl.BlockSpec((tm, tn), lambda j, i: (i, j)), pl.BlockSpec((tm, D), first_pass_rows)],
        out_shape=[_sds((s_len, nb * tn), BF), _sds((s_len, D), BF)],
        scratch_shapes=[pltpu.VMEM((s_len, D), BF)], comm=comm)


def _sgu_mix(ws_ref, vln_s, mix_s, bs_ref, tm, transposed):
    mask = _sgu_mask(transposed)
    for hd in range(HEADS):
        wm = jnp.where(mask, ws_ref[hd], 0.0).astype(BF)
        cs = slice(hd * BLK, (hd + 1) * BLK)
        for n in range(tm // BLK):
            rs = slice(n * BLK, (n + 1) * BLK)
            r = _dot(wm, vln_s[rs, cs])
            mix_s[rs, cs] = r if bs_ref is None else r + bs_ref[:, cs]


def sgu_fwd(proj, ws, bsfull, gv, bv, wo, name):
    s_len = proj.shape[0]
    tm = min(TM, s_len)

    def body(zu_ref, zv_ref, ws_ref, bs_ref, gv_ref, bv_ref, wo_ref, mix_ref, sgu_ref, br_ref, vln_s, mix_s):
        u, _ = _gelu(zu_ref[...].astype(F32))
        v, _ = _gelu(zv_ref[...].astype(F32))
        vhat, _ = _ln_fwd(v)
        vln_s[...] = (vhat * gv_ref[...] + bv_ref[...]).astype(BF)
        _sgu_mix(ws_ref, vln_s, mix_s, bs_ref, tm, False)
        mixed = mix_s[...].astype(BF)
        mix_ref[...] = mixed
        sgu = (u * mixed.astype(F32)).astype(BF)
        sgu_ref[...] = sgu
        br_ref[...] = _dot(sgu, wo_ref[...].reshape(D, D)).astype(BF)

    return pl.pallas_call(
        body, name=name, grid=(s_len // tm,),
        in_specs=[_cols(tm, D, 0), _cols(tm, D, 1), _whole((HEADS, BLK, BLK)), _whole((BLK, D)), _whole((1, D)), _whole((1, D)),
                  _rowsharded(BLK)],
        out_specs=[_cols(tm, D, 0)] * 3, out_shape=[_sds((s_len, D), BF)] * 3,
        scratch_shapes=[pltpu.VMEM((tm, D), BF), pltpu.VMEM((tm, D), F32)], compiler_params=_params(1),
    )(proj, proj, ws, bsfull, gv, bv, wo)


def _causal_conv(ext_s, out_s, wdw_ref, bias_ref, tm):
    def chunk(ci, carry):
        r0 = pl.multiple_of((ci // (D // LC)) * RC, RC)
        l0 = pl.multiple_of((ci % (D // LC)) * LC, LC)
        win = ext_s[pl.ds(r0, RC + HALO), pl.ds(l0, LC)]
        acc = jnp.broadcast_to(bias_ref[:, pl.ds(l0, LC)], (RC, LC))
        for r in range(8):
            wr = win if r == 0 else pltpu.roll(win, r, 0)
            for m in range(4):
                d = 8 * m + r
                if d < CONV_W:
                    k = CONV_W - 1 - d
                    acc = acc + wdw_ref[k:k + 1, pl.ds(l0, LC)] * wr[HALO - 8 * m:HALO - 8 * m + RC]
        out_s[pl.ds(r0, RC), pl.ds(l0, LC)] = acc
        return carry
    lax.fori_loop(0, (tm // RC) * (D // LC), chunk, 0)


def _glu_ext(a_ref, g_ref, ah_ref, gh_ref, ext_s, first):
    hh = ah_ref[...].astype(F32) * _sig(gh_ref[...].astype(F32))
    ext_s[0:HALO, :] = jnp.where(first, 0.0, hh)
    ext_s[HALO:, :] = a_ref[...].astype(F32) * _sig(g_ref[...].astype(F32))


def conv_fwd(proj, wdw, bdw, gln, bln, wo, name, comm=None):
    s_len = proj.shape[0]
    tm = min(TM, s_len)

    def body(a_ref, g_ref, ah_ref, gh_ref, wdw_ref, bdw_ref, gln_ref, bln_ref, wo_ref, cv_ref, cb_ref, br_ref, ext_s, conv_s):
        _glu_ext(a_ref, g_ref, ah_ref, gh_ref, ext_s, pl.program_id(0) == 0)
        _causal_conv(ext_s, conv_s, wdw_ref, bdw_ref, tm)
        cv = conv_s[...].astype(BF)
        cv_ref[...] = cv
        chat, _ = _ln_fwd(cv.astype(F32))
        yl = chat * gln_ref[...] + bln_ref[...]
        cb = (yl * _sig(yl)).astype(BF)
        cb_ref[...] = cb
        br_ref[...] = _dot(cb, wo_ref[...].reshape(D, D)).astype(BF)

    return _pcall(
        body, (proj, proj, proj, proj, wdw, bdw, gln, bln, wo), name=name, grid=(s_len // tm,),
        in_specs=[_cols(tm, D, 2), _cols(tm, D, 3), _prev_halo(tm, D, 2), _prev_halo(tm, D, 3), _whole((HALO, D)),
                  _whole((1, D)), _whole((1, D)), _whole((1, D)), _rowsharded(BLK)],
        out_specs=[_cols(tm, D, 0)] * 3, out_shape=[_sds((s_len, D), BF)] * 3,
        scratch_shapes=[pltpu.VMEM((tm + HALO, D), F32), pltpu.VMEM((tm, D), F32)], comm=comm)


def pool_fwd(proj, wpool, spool, wo, name):
    s_len = proj.shape[0]
    tm = min(TM, s_len)

    def body(z_ref, zh_ref, wp_ref, sp_ref, wo_ref, pooled_ref, pm_ref, br_ref, ext_s, mr_s):
        i = pl.program_id(0)
        ext_s[0:HALO, :] = jnp.where(i == 0, 0.0, zh_ref[...].astype(F32))
        ext_s[HALO:, :] = z_ref[...].astype(F32)
        for gi, w in enumerate(POOL_WINDOWS):
            cs = slice(gi * POOL_GD, (gi + 1) * POOL_GD)
            e = ext_s[:, cs]
            s = e
            sh = 1
            while sh < w:
                s = s + pltpu.roll(s, sh, 0)
                sh *= 2
            pooled = (s[HALO:] * _inv_count(i, tm, w) - e[HALO:]).astype(BF)
            pooled_ref[:, cs] = pooled
            mr_s[:, cs] = _dot(pooled, wp_ref[gi])
        pm = (mr_s[...] * sp_ref[...]).astype(BF)
        pm_ref[...] = pm
        br_ref[...] = _dot(pm, wo_ref[...].reshape(D, D)).astype(BF)

    return pl.pallas_call(
        body, name=name, grid=(s_len // tm,),
        in_specs=[_cols(tm, D, 4), _prev_halo(tm, D, 4), _whole((4, POOL_GD, POOL_GD)), _whole((1, D)), _rowsharded(BLK)],
        out_specs=[_cols(tm, D, 0)] * 3, out_shape=[_sds((s_len, D), BF)] * 3,
        scratch_shapes=[pltpu.VMEM((tm + HALO, D), F32), pltpu.VMEM((tm, D), F32)], compiler_params=_params(1),
    )(proj, proj, wpool, spool, wo)


def merge_out(proj, bra, brb, brc, h, wout, gpost, name, comm=None):
    s_len = h.shape[0]
    tm = min(TM, s_len)

    def body(z0, z1, z2, a_ref, b_ref, c_ref, h_ref, wo_ref, g_ref, mg_ref, mo_ref, h1_ref):
        merged = (_sig(z0[...].astype(F32)) * a_ref[...].astype(F32) + _sig(z1[...].astype(F32)) * b_ref[...].astype(F32)
                  + _sig(z2[...].astype(F32)) * c_ref[...].astype(F32)).astype(BF)
        mg_ref[...] = merged
        mo = _dot(merged, wo_ref[...].reshape(D, D))
        mo_ref[...] = mo.astype(BF)
        h1_ref[...] = h_ref[...] + mo * _rstd(mo) * g_ref[...]

    row = _cols(tm, D, 0)
    return _pcall(
        body, (proj, proj, proj, bra, brb, brc, h, wout, gpost), name=name, grid=(s_len // tm,),
        in_specs=[_cols(tm, D, 5), _cols(tm, D, 6), _cols(tm, D, 7), row, row, row, row, _rowsharded(BLK), _whole((1, D))],
        out_specs=[row] * 3, out_shape=[_sds((s_len, D), BF), _sds((s_len, D), BF), _sds((s_len, D), F32)], comm=comm)


def _p_spec(tm, layer):
    return pl.BlockSpec((None, None, tm, PLE), lambda i: (layer, 0, i, 0))


def ffn_out(ff, h1, p, wfo, gpost, wpg, wple, layer, name, comm=None):
    s_len = h1.shape[0]
    tm = min(TMB, s_len)

    def body(fg_ref, fu_ref, h1_ref, p_ref, wfo_ref, g_ref, wpg_ref, wple_ref, act_ref, f_ref, h2_ref, pg_ref, h3_ref):
        gt = fg_ref[...].astype(F32)
        act = (gt * _sig(gt) * fu_ref[...].astype(F32)).astype(BF)
        act_ref[...] = act
        f = _dot(act, wfo_ref[...].reshape(D_FF, D))
        f_ref[...] = f.astype(BF)
        h2 = h1_ref[...] + f * _rstd(f) * g_ref[...]
        h2_ref[...] = h2
        pg = _dot(h2.astype(BF), wpg_ref[...].reshape(D, D)).astype(BF)
        pg_ref[...] = pg
        pe = _dot(p_ref[...].astype(BF), wple_ref[...])
        h3_ref[...] = h2 + _sig(pg.astype(F32)) * pe

    row = _cols(tm, D, 0)
    return _pcall(
        body, (ff, ff, h1, p, wfo, gpost, wpg, wple), name=name, grid=(s_len // tm,),
        in_specs=[_cols(tm, D_FF, 0), _cols(tm, D_FF, 1), row, _p_spec(tm, layer), _rowsharded(D_FF // N_DEV),
                  _whole((1, D)), _rowsharded(BLK), _whole((PLE, D))],
        out_specs=[_cols(tm, D_FF, 0), row, row, row, row],
        out_shape=[_sds((s_len, D_FF), BF), _sds((s_len, D), BF), _sds((s_len, D), F32), _sds((s_len, D), BF), _sds((s_len, D), F32)],
        comm=comm)


def ple_ffn_bwd(dh3, pg, p, f, ff, wple, wpg, wfo, gpost, layer, name, target=None, comm=None):
    s_len = dh3.shape[0]
    tm = min(TMW, s_len)
    nt = s_len // tm
    with_loss = target is not None

    def body(*refs):
        if with_loss:
            t_ref, refs, loss_ref, loss_acc = refs[0], refs[1:-2], refs[-2], refs[-1]
        (dh3_ref, pg_ref, p_ref, f_ref, fg_ref, fu_ref, wple_ref, wpg_ref, wfo_ref, g_ref,
         dh2_ref, dpe_ref, dpg_ref, df_ref, dff_ref, dg_ref) = refs
        i = pl.program_id(0)
        dh3v = dh3_ref[...]
        if with_loss:
            err = dh3v - t_ref[...]
            dh3v = err * (1.0 / D)
            _accum(loss_acc, i == 0, _colsum(err * err))

            @pl.when(i == nt - 1)
            def _():
                loss_ref[...] = jnp.broadcast_to(jnp.sum(loss_acc[...], axis=1, keepdims=True) * (0.5 / D), (1, LC))
        s = _sig(pg_ref[...].astype(F32))
        pe = _dot(p_ref[...].astype(BF), wple_ref[...])
        dpe_ref[...] = (dh3v * s).astype(BF)
        dpg = (dh3v * pe * s * (1.0 - s)).astype(BF)
        dpg_ref[...] = dpg
        dh2 = dh3v + _dot_nt(dpg, wpg_ref[...].reshape(D, D))
        dh2_ref[...] = dh2
        fv = f_ref[...].astype(F32)
        r = _rstd(fv)
        _accum(dg_ref, i == 0, _colsum(dh2 * fv * r))
        df = _rms_bwd(fv, dh2 * g_ref[...], r).astype(BF)
        df_ref[...] = df
        dact = _dot_nt(df, wfo_ref[...].reshape(D_FF, D))
        gt = fg_ref[...].astype(F32)
        sg = _sig(gt)
        up = fu_ref[...].astype(F32)
        dff_ref[:, 0:D_FF] = (dact * up * sg * (1.0 + gt * (1.0 - sg))).astype(BF)
        dff_ref[:, D_FF:2 * D_FF] = (dact * gt * sg).astype(BF)

    row = _cols(tm, D, 0)
    args = (dh3, pg, p, f, ff, ff, wple, wpg, wfo, gpost)
    in_specs = [row, row, _p_spec(tm, layer), row, _cols(tm, D_FF, 0), _cols(tm, D_FF, 1), _whole((PLE, D)),
                _rowsharded(BLK), _rowsharded(D_FF // N_DEV), _whole((1, D))]
    out_specs = [row, row, row, row, _cols(tm, 2 * D_FF, 0), _whole((1, D))]
    out_shape = [_sds((s_len, D), F32), _sds((s_len, D), BF), _sds((s_len, D), BF), _sds((s_len, D), BF),
                 _sds((s_len, 2 * D_FF), BF), _sds((1, D), F32)]
    scratch = []
    if with_loss:
        args, in_specs = (target,) + args, [row] + in_specs
        out_specs, out_shape = out_specs + [_whole((1, LC))], out_shape + [_sds((1, LC), F32)]
        scratch = [pltpu.VMEM((1, D), F32)]
    return _pcall(body, args, name=name, grid=(nt,), in_specs=in_specs, out_specs=out_specs, out_shape=out_shape,
                  scratch_shapes=scratch, comm=comm)


def ffn_in_bwd(dff, wt, h1, dh2, gpre, name, comm=None):
    s_len = h1.shape[0]
    tm = min(TMW, s_len)
    nb, tn, _ = wt.shape

    def body(dff_ref, w_ref, h1_ref, dh2_ref, g_ref, dh1_ref, dg_ref):
        dhn = _dot(dff_ref[:, 0:tn], w_ref[0])
        for j in range(1, nb):
            dhn = dhn + _dot(dff_ref[:, j * tn:(j + 1) * tn], w_ref[j])
        x = h1_ref[...]
        r = _rstd(x)
        _accum(dg_ref, pl.program_id(0) == 0, _colsum(dhn * x * r))
        dh1_ref[...] = dh2_ref[...] + _rms_bwd(x, dhn * g_ref[...], r)

    row = _cols(tm, D, 0)
    return _pcall(
        body, (dff, wt, h1, dh2, gpre), name=name, grid=(s_len // tm,),
        in_specs=[_cols(tm, nb * tn, 0), pl.BlockSpec((nb, tn, D), lambda i: (0, 0, 0), pipeline_mode=pl.Buffered(1)), row, row,
                  _whole((1, D))],
        out_specs=[row, _whole((1, D))],
        out_shape=[_sds((s_len, D), F32), _sds((1, D), F32)], comm=comm)


def mix_post_bwd(dh1, mo, gpost, wout, proj, bra, brb, brc, win, name, comm=None):
    s_len = dh1.shape[0]
    tm = min(TMB, s_len)

    def body(dh1_ref, mo_ref, g_ref, wo_ref, z0, z1, z2, a_ref, b_ref, c_ref, w5, w6, w7,
             dmo_ref, da_ref, db_ref, dc_ref, dz_ref, dhn_ref, dg_ref):
        i = pl.program_id(0)
        dh1v = dh1_ref[...]
        mo_v = mo_ref[...].astype(F32)
        r = _rstd(mo_v)
        _accum(dg_ref, i == 0, _colsum(dh1v * mo_v * r))
        dmo = _rms_bwd(mo_v, dh1v * g_ref[...], r).astype(BF)
        dmo_ref[...] = dmo
        dmerged = _dot_nt(dmo, wo_ref[...].reshape(D, D))
        dhn = jnp.zeros((tm, D), F32)
        for k, (z, br, dbr, w) in enumerate(((z0, a_ref, da_ref, w5), (z1, b_ref, db_ref, w6), (z2, c_ref, dc_ref, w7))):
            s = _sig(z[...].astype(F32))
            dbr[...] = (dmerged * s).astype(BF)
            dz = (dmerged * br[...].astype(F32) * s * (1.0 - s)).astype(BF)
            dz_ref[:, k * D:(k + 1) * D] = dz
            dhn = dhn + _dot_nt(dz, w[...])
        dhn_ref[...] = dhn

    row = _cols(tm, D, 0)
    return _pcall(
        body, (dh1, mo, gpost, wout, proj, proj, proj, bra, brb, brc, win, win, win), name=name, grid=(s_len // tm,),
        in_specs=[row, row, _whole((1, D)), _rowsharded(BLK), _cols(tm, D, 5), _cols(tm, D, 6), _cols(tm, D, 7), row, row, row,
                  _win_block(5), _win_block(6), _win_block(7)],
        out_specs=[row, row, row, row, _cols(tm, 3 * D, 0), row, _whole((1, D))],
        out_shape=[_sds((s_len, D), BF)] * 4 + [_sds((s_len, 3 * D), BF), _sds((s_len, D), F32), _sds((1, D), F32)], comm=comm)


def sgu_bwd(dbr, wo, proj, mixed, wst, gv, bv, win, dhn_in, name, comm=None):
    s_len = dbr.shape[0]
    tm = min(TMB, s_len)
    nt = s_len // tm

    def body(dbr_ref, wo_ref, zu_ref, zv_ref, mix_ref, wst_ref, gv_ref, bv_ref, w0, w1, dhn_in_ref,
             dz_ref, dhn_ref, dws_ref, dbs_ref, dgv_ref, dbv_ref, vln_s, dmix_s, dvln_s, bs_acc):
        i = pl.program_id(0)
        first = i == 0
        dsgu = _dot_nt(dbr_ref[...], wo_ref[...].reshape(D, D))
        zu = zu_ref[...].astype(F32)
        zv = zv_ref[...].astype(F32)
        u, tu = _gelu(zu)
        v, tv = _gelu(zv)
        vhat, rs = _ln_fwd(v)
        vln_s[...] = (vhat * gv_ref[...] + bv_ref[...]).astype(BF)
        du = dsgu * mix_ref[...].astype(F32)
        dmix = dsgu * u
        dmix_s[...] = dmix.astype(BF)
        blocks = dmix[0:BLK]
        for n in range(1, tm // BLK):
            blocks = blocks + dmix[n * BLK:(n + 1) * BLK]
        _accum(bs_acc, first, blocks)
        for hd in range(HEADS):
            cs = slice(hd * BLK, (hd + 1) * BLK)
            g = _dot_nt(dmix_s[0:BLK, cs], vln_s[0:BLK, cs])
            for n in range(1, tm // BLK):
                g = g + _dot_nt(dmix_s[n * BLK:(n + 1) * BLK, cs], vln_s[n * BLK:(n + 1) * BLK, cs])

            @pl.when(first)
            def _():
                dws_ref[hd] = g

            @pl.when(jnp.logical_not(first))
            def _():
                dws_ref[hd] += g
        _sgu_mix(wst_ref, dmix_s, dvln_s, None, tm, True)
        dvln = dvln_s[...]
        _accum(dgv_ref, first, _colsum(dvln * vhat))
        _accum(dbv_ref, first, _colsum(dvln))
        dv = _ln_bwd(dvln * gv_ref[...], vhat, rs)
        dzu = (du * _gelu_grad(zu, tu)).astype(BF)
        dzv = (dv * _gelu_grad(zv, tv)).astype(BF)
        dz_ref[:, 0:D] = dzu
        dz_ref[:, D:2 * D] = dzv
        dhn_ref[...] = dhn_in_ref[...] + _dot_nt(dzu, w0[...]) + _dot_nt(dzv, w1[...])

        @pl.when(i == nt - 1)
        def _():
            mask = _sgu_mask(False)
            for hd in range(HEADS):
                dws_ref[hd] = jnp.where(mask, dws_ref[hd], 0.0)
                dbs_ref[:, hd:hd + 1] = jnp.sum(bs_acc[:, hd * BLK:(hd + 1) * BLK], axis=1, keepdims=True)

    row = _cols(tm, D, 0)
    vec = _whole((1, D))
    return _pcall(
        body, (dbr, wo, proj, proj, mixed, wst, gv, bv, win, win, dhn_in), name=name, grid=(nt,),
        in_specs=[row, _rowsharded(BLK), _cols(tm, D, 0), _cols(tm, D, 1), row, _whole((HEADS, BLK, BLK)),
                  vec, vec, _win_block(0), _win_block(1), row],
        out_specs=[_cols(tm, 2 * D, 0), row, _whole((HEADS, BLK, BLK)), _whole((BLK, HEADS)), vec, vec],
        out_shape=[_sds((s_len, 2 * D), BF), _sds((s_len, D), F32), _sds((HEADS, BLK, BLK), F32), _sds((BLK, HEADS), F32),
                   _sds((1, D), F32), _sds((1, D), F32)],
        scratch_shapes=[pltpu.VMEM((tm, D), BF), pltpu.VMEM((tm, D), BF), pltpu.VMEM((tm, D), F32), pltpu.VMEM((BLK, D), F32)],
        comm=comm)


def conv_bwd(dbr, wo, proj, conv, gln, bln, name, comm=None):
    s_len = dbr.shape[0]
    tm = min(TMB, s_len)
    nt = s_len // tm

    def body(dbr_ref, wo_ref, a_ref, g_ref, ah_ref, gh_ref, cv_ref, gln_ref, bln_ref,
             dc_ref, dw_ref, dbdw_ref, dgln_ref, dbln_ref, ext_s, dc_s, dw_acc):
        i = pl.program_id(0)
        first = i == 0
        dcb = _dot_nt(dbr_ref[...], wo_ref[...].reshape(D, D))
        _glu_ext(a_ref, g_ref, ah_ref, gh_ref, ext_s, first)
        chat, rs = _ln_fwd(cv_ref[...].astype(F32))
        yl = chat * gln_ref[...] + bln_ref[...]
        sy = _sig(yl)
        dyl = dcb * sy * (1.0 + yl * (1.0 - sy))
        _accum(dgln_ref, first, _colsum(dyl * chat))
        _accum(dbln_ref, first, _colsum(dyl))
        dc = _ln_bwd(dyl * gln_ref[...], chat, rs)
        _accum(dbdw_ref, first, _colsum(dc))
        dc_ref[...] = dc.astype(BF)
        dc_s[...] = dc

        @pl.when(first)
        def _():
            dw_acc[...] = jnp.zeros_like(dw_acc)

        def chunk(ci, carry):
            r0 = pl.multiple_of((ci // (D // LC)) * RC, RC)
            l0 = pl.multiple_of((ci % (D // LC)) * LC, LC)
            win = ext_s[pl.ds(r0, RC + HALO), pl.ds(l0, LC)]
            dcw = dc_s[pl.ds(r0, RC), pl.ds(l0, LC)]
            for r in range(8):
                wr = win if r == 0 else pltpu.roll(win, r, 0)
                for m in range(4):
                    d = 8 * m + r
                    if d < CONV_W:
                        k = CONV_W - 1 - d
                        prod = dcw * wr[HALO - 8 * m:HALO - 8 * m + RC]
                        dw_acc[k * 8:(k + 1) * 8, pl.ds(l0, LC)] += prod.reshape(RC // 8, 8, LC).sum(axis=0)
            return carry
        lax.fori_loop(0, (tm // RC) * (D // LC), chunk, 0)

        @pl.when(i == nt - 1)
        def _():
            dw_ref[...] = dw_acc[...].reshape(HALO, 8, D).sum(axis=1)

    row = _cols(tm, D, 0)
    vec = _whole((1, D))
    return _pcall(
        body, (dbr, wo, proj, proj, proj, proj, conv, gln, bln), name=name, grid=(nt,),
        in_specs=[row, _rowsharded(BLK), _cols(tm, D, 2), _cols(tm, D, 3), _prev_halo(tm, D, 2), _prev_halo(tm, D, 3),
                  row, vec, vec],
        out_specs=[row, _whole((HALO, D)), vec, vec, vec],
        out_shape=[_sds((s_len, D), BF), _sds((HALO, D), F32), _sds((1, D), F32), _sds((1, D), F32), _sds((1, D), F32)],
        scratch_shapes=[pltpu.VMEM((tm + HALO, D), F32), pltpu.VMEM((tm, D), F32), pltpu.VMEM((HALO * 8, D), F32)], comm=comm)


def pool_bwd(dbr, wo, pooled, wpool, spool, name):
    s_len = dbr.shape[0]
    tm = min(TMB, s_len)

    def body(dbr_ref, wo_ref, pl_ref, wp_ref, sp_ref, dmr_ref, q_ref, dsp_ref, mr_s):
        i = pl.program_id(0)
        dpm = _dot_nt(dbr_ref[...], wo_ref[...].reshape(D, D))
        for gi in range(4):
            cs = slice(gi * POOL_GD, (gi + 1) * POOL_GD)
            mr_s[:, cs] = _dot(pl_ref[:, cs], wp_ref[gi])
        _accum(dsp_ref, i == 0, _colsum(dpm * mr_s[...]))
        dmr = (dpm * sp_ref[...]).astype(BF)
        dmr_ref[...] = dmr
        for gi, w in enumerate(POOL_WINDOWS):
            cs = slice(gi * POOL_GD, (gi + 1) * POOL_GD)
            q_ref[:, cs] = (_dot_nt(dmr[:, cs], wp_ref[gi]) * _inv_count(i, tm, w)).astype(BF)

    row = _cols(tm, D, 0)
    return pl.pallas_call(
        body, name=name, grid=(s_len // tm,),
        in_specs=[row, _rowsharded(BLK), row, _whole((4, POOL_GD, POOL_GD)), _whole((1, D))],
        out_specs=[row, row, _whole((1, D))],
        out_shape=[_sds((s_len, D), BF), _sds((s_len, D), BF), _sds((1, D), F32)],
        scratch_shapes=[pltpu.VMEM((tm, D), F32)], compiler_params=_params(1))(dbr, wo, pooled, wpool, spool)


def seq_bwd(dc, q, proj, wdw, win, dhn_in, h, dh1, gpre, name, comm=None):
    s_len = dc.shape[0]
    tm = min(TMB, s_len)
    nt = s_len // tm

    def body(dc_ref, dch_ref, q_ref, qh_ref, a_ref, g_ref, wdw_ref, w2, w3, w4, dhn_in_ref, h_ref, dh1_ref, gpre_ref,
             dz_ref, dh_ref, dgpre_ref, ext_s, dhc_s, qext_s):
        i = pl.program_id(0)
        last = i == nt - 1
        ext_s[0:tm, :] = dc_ref[...].astype(F32)
        ext_s[tm:, :] = jnp.where(last, 0.0, dch_ref[...].astype(F32))

        def chunk(ci, carry):
            r0 = pl.multiple_of((ci // (D // LC)) * RC, RC)
            l0 = pl.multiple_of((ci % (D // LC)) * LC, LC)
            win_ = ext_s[pl.ds(r0, RC + HALO), pl.ds(l0, LC)]
            acc = jnp.zeros((RC, LC), F32)
            for r in range(8):
                wr = win_ if r == 0 else pltpu.roll(win_, RC + HALO - r, 0)
                for m in range(4):
                    d = 8 * m + r
                    if d < CONV_W:
                        k = CONV_W - 1 - d
                        acc = acc + wdw_ref[k:k + 1, pl.ds(l0, LC)] * wr[8 * m:8 * m + RC]
            dhc_s[pl.ds(r0, RC), pl.ds(l0, LC)] = acc
            return carry
        lax.fori_loop(0, (tm // RC) * (D // LC), chunk, 0)

        dhc = dhc_s[...]
        av = a_ref[...].astype(F32)
        sg = _sig(g_ref[...].astype(F32))
        da = (dhc * sg).astype(BF)
        dg = (dhc * av * sg * (1.0 - sg)).astype(BF)
        dz_ref[:, 0:D] = da
        dz_ref[:, D:2 * D] = dg

        qext_s[0:tm, :] = q_ref[...].astype(F32)
        qext_s[tm:, :] = jnp.where(last, 0.0, qh_ref[...].astype(F32))
        for gi, w in enumerate(POOL_WINDOWS):
            cs = slice(gi * POOL_GD, (gi + 1) * POOL_GD)
            e = qext_s[:, cs]
            s = e
            sh = 1
            while sh < w:
                s = s + pltpu.roll(s, tm + HALO - sh, 0)
                sh *= 2
            dz_ref[:, 2 * D + gi * POOL_GD:2 * D + (gi + 1) * POOL_GD] = (s[0:tm] - e[0:tm] * _count(i, tm, w)).astype(BF)
        dhn = dhn_in_ref[...] + _dot_nt(da, w2[...]) + _dot_nt(dg, w3[...]) + _dot_nt(dz_ref[:, 2 * D:3 * D], w4[...])
        x = h_ref[...]
        r = _rstd(x)
        _accum(dgpre_ref, i == 0, _colsum(dhn * x * r))
        dh_ref[...] = dh1_ref[...] + _rms_bwd(x, dhn * gpre_ref[...], r)

    row = _cols(tm, D, 0)
    return _pcall(
        body, (dc, dc, q, q, proj, proj, wdw, win, win, win, dhn_in, h, dh1, gpre), name=name, grid=(nt,),
        in_specs=[row, _next_halo(tm, D, 0, s_len), row, _next_halo(tm, D, 0, s_len), _cols(tm, D, 2), _cols(tm, D, 3),
                  _whole((HALO, D)), _win_block(2), _win_block(3), _win_block(4), row, row, row, _whole((1, D))],
        out_specs=[_cols(tm, 3 * D, 0), row, _whole((1, D))],
        out_shape=[_sds((s_len, 3 * D), BF), _sds((s_len, D), F32), _sds((1, D), F32)],
        scratch_shapes=[pltpu.VMEM((tm + HALO, D), F32), pltpu.VMEM((tm, D), F32), pltpu.VMEM((tm + HALO, D), F32)], comm=comm)


def _contraction_tile(s_len, tk, tn, a_bytes, b_bytes):
    ts = min(TS_MAX, s_len)
    while ts > 512 and 2 * ts * (tk * a_bytes + tn * b_bytes) + tk * tn * (4 + 2 * 2) > WGRAD_VMEM_BUDGET:
        ts //= 2
    return ts


def wgrad(a, b, tk, tn, name, stacked=False, diag=False, a_spec=None, comm=None):
    s_len = b.shape[0]
    k_dim = a.shape[-1]
    n_dim = b.shape[1]
    ts = _contraction_tile(s_len, tk, tn, a.dtype.itemsize, b.dtype.itemsize)
    nk = 1 if diag else k_dim // tk
    nn, ns = n_dim // tn, s_len // ts

    def body(a_ref, b_ref, o_ref, acc):
        s = pl.program_id(2)
        _accum(acc, s == 0, _dot_tn(a_ref[...].astype(BF), b_ref[...].astype(BF)))

        @pl.when(s == ns - 1)
        def _():
            o_ref[...] = acc[...].astype(BF).reshape(o_ref.shape)

    if a_spec is None:
        a_spec = pl.BlockSpec((ts, tk), (lambda k, n, s: (s, n)) if diag else (lambda k, n, s: (s, k)))
    else:
        a_spec = a_spec(ts)
    if stacked or diag:
        out_shape = _sds((nn, tk if diag else k_dim, tn), BF)
        o_spec = pl.BlockSpec((1, tk, tn), lambda k, n, s: (n, k, 0))
    else:
        out_shape = _sds((k_dim, n_dim), BF)
        o_spec = pl.BlockSpec((tk, tn), lambda k, n, s: (k, n))
    (out,), got = _pcall(
        body, (a, b), name=name, grid=(nk, nn, ns),
        in_specs=[a_spec, pl.BlockSpec((ts, tn), lambda k, n, s: (s, n))], out_specs=[o_spec], out_shape=[out_shape],
        scratch_shapes=[pltpu.VMEM((tk, tn), F32)], comm=comm)
    return out if comm is None else (out, got)


_WEIGHTS = ['g_mix_pre', 'w_in', 'w_sgu_s', 'b_sgu_s', 'g_sgu_v', 'b_sgu_v', 'w_sgu_out', 'w_dw', 'b_dw', 'g_conv_ln', 'b_conv_ln',
            'w_conv_out', 'w_pool', 's_pool', 'w_pool_out', 'w_out', 'g_mix_post', 'g_ffn_pre', 'w_ffn_in', 'w_ffn_out', 'g_ffn_post',
            'w_ple', 'w_ple_gate']
_SHARDED = ['w_in', 'w_sgu_out', 'w_conv_out', 'w_pool', 'w_pool_out', 'w_out', 'w_ffn_in', 'w_ffn_out', 'w_ple', 'w_ple_gate']
_VECTORS = ['g_mix_pre', 'g_sgu_v', 'b_sgu_v', 'b_dw', 'g_conv_ln', 'b_conv_ln', 's_pool', 'g_mix_post', 'g_ffn_pre', 'g_ffn_post']
_SUBLANES = 8
_SGU_ROWS = HEADS * BLK * BLK // D
_REP_ROWS = _SUBLANES * (len(_VECTORS) + 2) + _SGU_ROWS


def _pack_replicated(t, layer):
    rows = [jnp.pad(t[k][layer].reshape(1, D), ((0, _SUBLANES - 1), (0, 0))) for k in _VECTORS + ['b_sgu_s']]
    return jnp.concatenate(rows + [t['w_sgu_s'][layer].reshape(_SGU_ROWS, D), jnp.zeros((_SUBLANES, D), F32)], axis=0)


def _unpack_replicated(packed):
    out = {}
    for i, k in enumerate(_VECTORS):
        out[k] = packed[:, _SUBLANES * i, :]
    o = _SUBLANES * len(_VECTORS)
    out['b_sgu_s'] = packed[:, o, :].reshape(2, HEADS, BLK)
    out['w_sgu_s'] = packed[:, o + _SUBLANES:o + _SUBLANES + _SGU_ROWS, :].reshape(2, HEADS, BLK, BLK)
    return out


def _pad_taps(w):
    return jnp.pad(w, ((0, HALO - CONV_W), (0, 0)))


def kernel(x, p, g_mix_pre, w_in, w_sgu_s, b_sgu_s, g_sgu_v, b_sgu_v, w_sgu_out, w_dw, b_dw, g_conv_ln, b_conv_ln, w_conv_out, w_pool, s_pool, w_pool_out, w_out, g_mix_post, g_ffn_pre, w_ffn_in, w_ffn_out, g_ffn_post, w_ple, w_ple_gate, loss_target, m_g_mix_pre, m_w_in, m_w_sgu_s, m_b_sgu_s, m_g_sgu_v, m_b_sgu_v, m_w_sgu_out, m_w_dw, m_b_dw, m_g_conv_ln, m_b_conv_ln, m_w_conv_out, m_w_pool, m_s_pool, m_w_pool_out, m_w_out, m_g_mix_post, m_g_ffn_pre, m_w_ffn_in, m_w_ffn_out, m_g_ffn_post, m_w_ple, m_w_ple_gate, v_g_mix_pre, v_w_in, v_w_sgu_s, v_b_sgu_s, v_g_sgu_v, v_b_sgu_v, v_w_sgu_out, v_w_dw, v_b_dw, v_g_conv_ln, v_b_conv_ln, v_w_conv_out, v_w_pool, v_s_pool, v_w_pool_out, v_w_out, v_g_mix_post, v_g_ffn_pre, v_w_ffn_in, v_w_ffn_out, v_g_ffn_post, v_w_ple, v_w_ple_gate):
    W = dict(g_mix_pre=g_mix_pre, w_in=w_in, w_sgu_s=w_sgu_s, b_sgu_s=b_sgu_s, g_sgu_v=g_sgu_v, b_sgu_v=b_sgu_v, w_sgu_out=w_sgu_out,
             w_dw=w_dw, b_dw=b_dw, g_conv_ln=g_conv_ln, b_conv_ln=b_conv_ln, w_conv_out=w_conv_out, w_pool=w_pool, s_pool=s_pool,
             w_pool_out=w_pool_out, w_out=w_out, g_mix_post=g_mix_post, g_ffn_pre=g_ffn_pre, w_ffn_in=w_ffn_in, w_ffn_out=w_ffn_out,
             g_ffn_post=g_ffn_post, w_ple=w_ple, w_ple_gate=w_ple_gate)
    M = dict(g_mix_pre=m_g_mix_pre, w_in=m_w_in, w_sgu_s=m_w_sgu_s, b_sgu_s=m_b_sgu_s, g_sgu_v=m_g_sgu_v, b_sgu_v=m_b_sgu_v,
             w_sgu_out=m_w_sgu_out, w_dw=m_w_dw, b_dw=m_b_dw, g_conv_ln=m_g_conv_ln, b_conv_ln=m_b_conv_ln, w_conv_out=m_w_conv_out,
             w_pool=m_w_pool, s_pool=m_s_pool, w_pool_out=m_w_pool_out, w_out=m_w_out, g_mix_post=m_g_mix_post, g_ffn_pre=m_g_ffn_pre,
             w_ffn_in=m_w_ffn_in, w_ffn_out=m_w_ffn_out, g_ffn_post=m_g_ffn_post, w_ple=m_w_ple, w_ple_gate=m_w_ple_gate)
    V = dict(g_mix_pre=v_g_mix_pre, w_in=v_w_in, w_sgu_s=v_w_sgu_s, b_sgu_s=v_b_sgu_s, g_sgu_v=v_g_sgu_v, b_sgu_v=v_b_sgu_v,
             w_sgu_out=v_w_sgu_out, w_dw=v_w_dw, b_dw=v_b_dw, g_conv_ln=v_g_conv_ln, b_conv_ln=v_b_conv_ln, w_conv_out=v_w_conv_out,
             w_pool=v_w_pool, s_pool=v_s_pool, w_pool_out=v_w_pool_out, w_out=v_w_out, g_mix_post=v_g_mix_post, g_ffn_pre=v_g_ffn_pre,
             w_ffn_in=v_w_ffn_in, w_ffn_out=v_w_ffn_out, g_ffn_post=v_g_ffn_post, w_ple=v_w_ple, w_ple_gate=v_w_ple_gate)

    my_c = lax.axis_index("c")
    core_id = my_c.astype(jnp.int32).reshape(1)
    my_chip = 2 * lax.axis_index("x") + lax.axis_index("y")
    my_dev = 2 * my_chip + my_c
    s_len = x.shape[1]
    h0 = x.reshape(s_len, D)
    target = loss_target.reshape(s_len, D)

    shard = [{k: W[k][l].astype(BF) for k in _SHARDED} for l in range(2)]
    for l in range(2):
        shard[l]['w_dw'] = w_dw[l]
        shard[l]['w_ffn_in'] = jnp.swapaxes(w_ffn_in[l], 0, 1).astype(BF)
    mixer_w = ['w_sgu_out', 'w_conv_out', 'w_pool', 'w_pool_out', 'w_out', 'w_dw']
    ffn_w = ['w_ffn_in', 'w_ffn_out', 'w_ple', 'w_ple_gate']
    hosted_gather = {
        'norm_proj_in': (0, mixer_w),
        'conv_fwd': (0, ffn_w),
        'merge_out': (1, mixer_w),
        'norm_proj_ffn': (1, ['w_in']),
        'ffn_out': (1, ffn_w),
    }
    G = [{'w_in': run_comm(Gather([shard[0]['w_in']]), "gather_w_in_0")[0]}, {}]

    def gather_in(layer, call):
        if layer != 0:
            return None, (lambda got: None)
        to_layer, keys = hosted_gather[call]
        return Gather([shard[to_layer][k] for k in keys]), (lambda got: G[to_layer].update(zip(keys, got)))

    def natural_mixer(g):
        wpool = jnp.transpose(g['w_pool'], (1, 0, 2, 3)).reshape(4, POOL_GD, POOL_GD)
        wdw = jnp.transpose(g['w_dw'].reshape(N_DEV, CONV_W, BLK), (1, 0, 2)).reshape(CONV_W, D)
        return dict(wpool=wpool, wdw=_pad_taps(wdw))

    def natural_ffn(g):
        wfit = g['w_ffn_in'].reshape(4, D_FF // 2, D)
        wple = jnp.transpose(g['w_ple'], (1, 0, 2)).reshape(PLE, D)
        return dict(wfit=wfit, wple=wple)

    def vec(name, layer):
        return W[name][layer].reshape(1, D)

    saved = []
    h = h0
    for l in range(2):
        g = G[l]
        comm, land = gather_in(l, 'norm_proj_in')
        (proj, hn), got = norm_proj(h, vec('g_mix_pre', l), g['w_in'], f"norm_proj_in_{l}", comm=comm)
        land(got)
        nat = natural_mixer(g)
        bsfull = jnp.repeat(b_sgu_s[l].T, BLK, axis=1)
        wst = jnp.swapaxes(w_sgu_s[l], 1, 2)
        mixed, sgu, bra = sgu_fwd(proj, w_sgu_s[l], bsfull, vec('g_sgu_v', l), vec('b_sgu_v', l), g['w_sgu_out'], f"sgu_fwd_{l}")
        comm, land = gather_in(l, 'conv_fwd')
        (conv, cb, brb), got = conv_fwd(proj, nat['wdw'], vec('b_dw', l), vec('g_conv_ln', l), vec('b_conv_ln', l), g['w_conv_out'],
                                        f"conv_fwd_{l}", comm=comm)
        land(got)
        nat.update(natural_ffn(g))
        pooled, pm, brc = pool_fwd(proj, nat['wpool'], vec('s_pool', l), g['w_pool_out'], f"pool_fwd_{l}")
        comm, land = gather_in(l, 'merge_out')
        (merged, mo, h1), got = merge_out(proj, bra, brb, brc, h, g['w_out'], vec('g_mix_post', l), f"merge_out_{l}", comm=comm)
        land(got)
        comm, land = gather_in(l, 'norm_proj_ffn')
        (ff, hn2), got = norm_proj(h1, vec('g_ffn_pre', l), nat['wfit'], f"norm_proj_ffn_{l}", transposed=True, comm=comm)
        land(got)
        comm, land = gather_in(l, 'ffn_out')
        (act, f, h2, pg, h3), got = ffn_out(ff, h1, p, g['w_ffn_out'], vec('g_ffn_post', l), g['w_ple_gate'], nat['wple'], l,
                                            f"ffn_out_{l}", comm=comm)
        land(got)
        saved.append(dict(h=h, nat=nat, wst=wst, proj=proj, hn=hn, mixed=mixed, sgu=sgu, bra=bra, conv=conv, cb=cb, brb=brb,
                          pooled=pooled, pm=pm, brc=brc, merged=merged, mo=mo, h1=h1, ff=ff, hn2=hn2, act=act, f=f, h2=h2, pg=pg))
        h = h3

    dh = h

    parts = {k: [None, None] for k in _SHARDED}
    small = {k: [None, None] for k in _VECTORS + ['b_sgu_s', 'w_sgu_s', 'w_dw']}
    chip_parts = [{}, {}]
    from_chips = [{}, {}]
    gathered_small = [None, None]

    def to_sibling(layer, keys):
        return ToSibling([parts[k][layer] for k in keys])

    def add_siblings(layer, keys, from_sibling):
        for k, rv in zip(keys, from_sibling):
            st = parts[k][layer]
            cols = st.shape[-1]
            chip_parts[layer][k] = add_core_side(st.reshape(N_DEV, -1, cols), rv.reshape(4, -1, cols), core_id,
                                                 f"rs_add_{k}_{layer}").reshape(rv.shape)

    def to_chips(layer, keys):
        return ToChips([chip_parts[layer][k] for k in keys])

    def small_pack(layer):
        return jnp.concatenate([_pack_replicated(small, layer), small['w_dw'][layer]], axis=0)

    ffn_group = ['w_ffn_in', 'w_ffn_out', 'w_ple_gate', 'w_ple']
    mix_group = ['w_out', 'w_sgu_out', 'w_conv_out', 'w_pool_out', 'w_pool']
    big = ['w_in', 'w_ffn_in']
    others = [k for k in _SHARDED if k not in big]
    hosted_rs = {
        'ple_ffn_bwd': (lambda: to_sibling(1, _SHARDED), lambda got: add_siblings(1, _SHARDED, got)),
        'mix_post_bwd': (lambda: Both(to_sibling(0, ffn_group), Gather([small_pack(1)])),
                         lambda got: (add_siblings(0, ffn_group, got[:-1]), gathered_small.__setitem__(1, got[-1]))),
        'ffn_in_bwd': (lambda: to_chips(1, others), lambda got: from_chips[1].update(zip(others, got))),
        'sgu_bwd': (lambda: to_chips(1, big), lambda got: from_chips[1].update(zip(big, got))),
        'conv_bwd': (lambda: to_chips(0, ffn_group), lambda got: from_chips[0].update(zip(ffn_group, got))),
        'wgrad_in_sgu': (lambda: to_sibling(0, mix_group), lambda got: add_siblings(0, mix_group, got)),
        'wgrad_in_seq': (lambda: to_chips(0, mix_group), lambda got: from_chips[0].update(zip(mix_group, got))),
        'wgrad_in_gate': (lambda: Gather([small_pack(0)]), lambda got: gathered_small.__setitem__(0, got[0])),
    }

    def exchange_in(layer, call):
        if layer != 0 or call not in hosted_rs:
            return None, (lambda got: None)
        make, land = hosted_rs[call]
        return make(), land

    for l in (1, 0):
        sv, g, nat = saved[l], G[l], saved[l]['nat']

        def wg(call, a, b, tk, tn, **kw):
            comm, land = exchange_in(l, call)
            if comm is None:
                return wgrad(a, b, tk, tn, f"{call}_{l}", **kw)
            out, got = wgrad(a, b, tk, tn, f"{call}_{l}", comm=comm, **kw)
            land(got)
            return out

        comm, land = exchange_in(l, 'ple_ffn_bwd')
        res, got = ple_ffn_bwd(
            dh, sv['pg'], p, sv['f'], sv['ff'], nat['wple'], g['w_ple_gate'], g['w_ffn_out'], vec('g_ffn_post', l), l,
            f"ple_ffn_bwd_{l}", target=target if l == 1 else None, comm=comm)
        land(got)
        dh2, dpe, dpg, df, dff, small['g_ffn_post'][l] = res[:6]
        if l == 1:
            loss = lax.psum(res[6][0, 0], ("x", "y", "c"))
        comm, land = exchange_in(l, 'ffn_in_bwd')
        (dh1, small['g_ffn_pre'][l]), got = ffn_in_bwd(dff, nat['wfit'], sv['h1'], dh2, vec('g_ffn_pre', l), f"ffn_in_bwd_{l}",
                                                       comm=comm)
        land(got)
        p_spec = functools.partial(
            lambda ts, ll: pl.BlockSpec((None, None, ts, PLE), functools.partial(lambda k, n, s, lll: (lll, 0, s, 0), lll=ll)), ll=l)
        parts['w_ple'][l] = jnp.transpose(wg('wgrad_ple', p, dpe, PLE, D, a_spec=p_spec).reshape(PLE, N_DEV, BLK), (1, 0, 2))
        parts['w_ple_gate'][l] = wg('wgrad_ple_gate', sv['h2'], dpg, D, D).reshape(N_DEV, BLK, D)
        parts['w_ffn_out'][l] = wg('wgrad_ffn_out', sv['act'], df, D_FF // 2, D).reshape(N_DEV, D_FF // N_DEV, D)
        parts['w_ffn_in'][l] = wg('wgrad_ffn_in', dff, sv['hn2'], D_FF // 2, D).reshape(N_DEV, D_FF // 4, D)

        comm, land = exchange_in(l, 'mix_post_bwd')
        (dmo, dbra, dbrb, dbrc, dzg, dhn_g, small['g_mix_post'][l]), got = mix_post_bwd(
            dh1, sv['mo'], vec('g_mix_post', l), g['w_out'], sv['proj'], sv['bra'], sv['brb'], sv['brc'], g['w_in'],
            f"mix_post_bwd_{l}", comm=comm)
        land(got)
        comm, land = exchange_in(l, 'sgu_bwd')
        (dzs, dhn_ag, dws, dbs, small['g_sgu_v'][l], small['b_sgu_v'][l]), got = sgu_bwd(
            dbra, g['w_sgu_out'], sv['proj'], sv['mixed'], sv['wst'], vec('g_sgu_v', l), vec('b_sgu_v', l), g['w_in'],
            dhn_g, f"sgu_bwd_{l}", comm=comm)
        land(got)
        small['w_sgu_s'][l] = dws
        small['b_sgu_s'][l] = dbs.T
        comm, land = exchange_in(l, 'conv_bwd')
        (dc, dwdw, small['b_dw'][l], small['g_conv_ln'][l], small['b_conv_ln'][l]), got = conv_bwd(
            dbrb, g['w_conv_out'], sv['proj'], sv['conv'], vec('g_conv_ln', l), vec('b_conv_ln', l), f"conv_bwd_{l}", comm=comm)
        land(got)
        small['w_dw'][l] = dwdw
        dmr, q, small['s_pool'][l] = pool_bwd(dbrc, g['w_pool_out'], sv['pooled'], nat['wpool'], vec('s_pool', l), f"pool_bwd_{l}")
        comm, land = exchange_in(l, 'seq_bwd')
        (dzc, dh, small['g_mix_pre'][l]), got = seq_bwd(dc, q, sv['proj'], nat['wdw'], g['w_in'], dhn_ag, sv['h'], dh1,
                                                        vec('g_mix_pre', l), f"seq_bwd_{l}", comm=comm)
        land(got)

        parts['w_out'][l] = wg('wgrad_out', sv['merged'], dmo, D, D).reshape(N_DEV, BLK, D)
        parts['w_sgu_out'][l] = wg('wgrad_sgu_out', sv['sgu'], dbra, D, D).reshape(N_DEV, BLK, D)
        parts['w_conv_out'][l] = wg('wgrad_conv_out', sv['cb'], dbrb, D, D).reshape(N_DEV, BLK, D)
        parts['w_pool_out'][l] = wg('wgrad_pool_out', sv['pm'], dbrc, D, D).reshape(N_DEV, BLK, D)
        g_pool = wg('wgrad_pool', sv['pooled'], dmr, POOL_GD, POOL_GD, diag=True)
        parts['w_pool'][l] = jnp.transpose(g_pool.reshape(4, N_DEV, POOL_GD // N_DEV, POOL_GD), (1, 0, 2, 3))
        parts['w_in'][l] = jnp.concatenate([
            wg('wgrad_in_sgu', sv['hn'], dzs, D, D, stacked=True),
            wg('wgrad_in_seq', sv['hn'], dzc, D, D, stacked=True),
            wg('wgrad_in_gate', sv['hn'], dzg, D, D, stacked=True)], axis=0)
    grad_x = dh.reshape(1, s_len, D)

    add_siblings(0, ['w_in'], run_comm(to_sibling(0, ['w_in']), "rs_to_sibling_w_in_0"))
    from_chips[0]['w_in'] = run_comm(to_chips(0, ['w_in']), "rs_to_chips_w_in_0")[0]

    outs = {}
    for k in _SHARDED:
        wmv = [jnp.swapaxes(t[k], 1, 2) if k == 'w_ffn_in' else t[k] for t in (W, M, V)]
        cols = wmv[0].shape[-1]
        pieces = []
        for layer in range(2):
            own = lax.dynamic_index_in_dim(chip_parts[layer][k], my_chip, axis=0, keepdims=False).reshape(-1, cols)
            rv3 = from_chips[layer][k].reshape(3, -1, cols)
            pieces.append([(own, None), (rv3, 0), (rv3, 1), (rv3, 2)])
        res = adamw_layers(*[t.reshape(2, -1, cols) for t in wmv], pieces, f"adamw_{k}")
        res = [r.reshape(wmv[0].shape) for r in res]
        outs[k] = [jnp.swapaxes(r, 1, 2) for r in res] if k == 'w_ffn_in' else res

    packed = [jnp.stack([_pack_replicated(t, 0), _pack_replicated(t, 1)], axis=0) for t in (W, M, V)]
    rep_res = adamw_layers(*packed, [[(gathered_small[layer], d) for d in range(N_DEV)] for layer in range(2)], "adamw_replicated")
    for idx, res in enumerate(rep_res):
        for name, val in _unpack_replicated(res).items():
            outs.setdefault(name, [None] * 4)[idx] = val
    dw_sum = jnp.stack([sum_slabs(gathered_small[layer][:, _REP_ROWS:, :], f"sum_w_dw_{layer}") for layer in range(2)], axis=0)
    dw_mine = lax.dynamic_slice_in_dim(dw_sum[:, :CONV_W], my_dev * BLK, BLK, axis=2)
    res = adamw(w_dw.reshape(2 * CONV_W, BLK), m_w_dw.reshape(2 * CONV_W, BLK), v_w_dw.reshape(2 * CONV_W, BLK),
                [(dw_mine.reshape(2 * CONV_W, BLK), None)], "adamw_w_dw")
    outs['w_dw'] = [r.reshape(w_dw.shape) for r in res]

    result = [loss, grad_x]
    for idx in range(4):
        result += [outs[k][idx] for k in _WEIGHTS]
    return tuple(result)
```

```python
import functools
import math

import jax
import jax.numpy as jnp
from jax import lax
from jax.experimental import pallas as pl
from jax.experimental.pallas import tpu as pltpu

F32 = jnp.float32
BF = jnp.bfloat16

D = 1024
D_FF = 2816
PLE = 256
N_DEV = 8
HEADS = 8
BLK = 128
CHUNK = 64
CONV_W = 31
POOL_WINDOWS = (2, 4, 8, 16)
POOL_GD = 256
EPS = 1e-6

V7X_VMEM_BYTES = 64 * 2**20
VMEM_LIMIT = V7X_VMEM_BYTES * 7 // 8
HALO = 32
RC = 64
LC = 128
TM = 512
TMB = 512
TMW = 256
TMP = 1024
TS = 2048

ADAM_LR, ADAM_B1, ADAM_B2, ADAM_EPS, ADAM_WD, ADAM_STEP = 0.001, 0.9, 0.999, 1e-08, 0.01, 10

MESH = pl.DeviceIdType.MESH
ANY = pl.BlockSpec(memory_space=pl.ANY)

_GELU_K0 = math.sqrt(2.0 / math.pi)
_GELU_K1 = 0.044715
_LOG2E = 1.4426950408889634


def _dot(a, b):
    return jnp.dot(a, b, preferred_element_type=F32)


def _dot_nt(a, b):
    return lax.dot_general(a, b, (((1,), (1,)), ((), ())), preferred_element_type=F32)


def _dot_tn(a, b):
    return lax.dot_general(a, b, (((0,), (0,)), ((), ())), preferred_element_type=F32)


def _sig(x):
    return 1.0 / (1.0 + jnp.exp2(x * (-_LOG2E)))


def _gelu(x):
    s = 1.0 / (1.0 + jnp.exp2(x * ((-2.0 * _GELU_K0 * _LOG2E) + (-2.0 * _GELU_K0 * _GELU_K1 * _LOG2E) * (x * x))))
    return x * s, s


def _gelu_grad(x, s):
    return s + x * s * (1.0 - s) * ((2.0 * _GELU_K0) + (6.0 * _GELU_K0 * _GELU_K1) * (x * x))


def _rstd(x):
    return lax.rsqrt(jnp.mean(x * x, axis=-1, keepdims=True) + EPS)


def _rms_bwd(x, gd, r):
    return r * gd - x * (r * r * r) * jnp.mean(gd * x, axis=-1, keepdims=True)


def _ln_fwd(x):
    mu = jnp.mean(x, axis=-1, keepdims=True)
    xc = x - mu
    rs = lax.rsqrt(jnp.mean(xc * xc, axis=-1, keepdims=True) + EPS)
    return xc * rs, rs


def _ln_bwd(dhat, hat, rs):
    return rs * (dhat - jnp.mean(dhat, axis=-1, keepdims=True) - hat * jnp.mean(dhat * hat, axis=-1, keepdims=True))


def _colsum(x):
    return jnp.sum(x, axis=0, keepdims=True)


def _accum(ref, first, val):
    @pl.when(first)
    def _():
        ref[...] = val

    @pl.when(jnp.logical_not(first))
    def _():
        ref[...] += val


def _sgu_mask(transposed):
    r = lax.broadcasted_iota(jnp.int32, (BLK, BLK), 0) // CHUNK
    c = lax.broadcasted_iota(jnp.int32, (BLK, BLK), 1) // CHUNK
    return (r <= c) if transposed else (c <= r)


def _inv_count(i, tm, w):
    t = lax.broadcasted_iota(jnp.int32, (tm, 1), 0) + i * tm
    return 1.0 / jnp.minimum(t + 1, w).astype(F32)


def _count(i, tm, w):
    t = lax.broadcasted_iota(jnp.int32, (tm, 1), 0) + i * tm
    return jnp.minimum(t + 1, w).astype(F32)


def _params(n_grid):
    return pltpu.CompilerParams(dimension_semantics=("arbitrary",) * n_grid, vmem_limit_bytes=VMEM_LIMIT)


def _sds(shape, dtype):
    return jax.ShapeDtypeStruct(shape, dtype)


def _cols(tm, width, cb):
    return pl.BlockSpec((tm, width), lambda i: (i, cb))


def _whole(shape):
    nd = len(shape)
    return pl.BlockSpec(shape, lambda i: (0,) * nd)


def _prev_halo(tm, width, cb):
    return pl.BlockSpec((HALO, width), lambda i: (jnp.maximum(i * (tm // HALO) - 1, 0), cb))


def _next_halo(tm, width, cb, s_len):
    last = s_len // HALO - 1
    return pl.BlockSpec((HALO, width), lambda i: (jnp.minimum((i + 1) * (tm // HALO), last), cb))


def _rowsharded(rows):
    return pl.BlockSpec((N_DEV, rows, D), lambda i: (0, 0, 0))


def _win_block(j):
    return pl.BlockSpec((None, D, D), lambda i: (j, 0, 0))


def _row_tile(rows, cap):
    t = min(rows, cap)
    while rows % t or t % 16:
        t -= 16
    return t


class Gather:
    def __init__(self, arrs):
        self.arrs = list(arrs)
        n = self.n = len(self.arrs)
        self.out_shape = [_sds((N_DEV,) + a.shape, a.dtype) for a in self.arrs]
        self.scratch = [pltpu.SemaphoreType.DMA((n, 7)), pltpu.SemaphoreType.DMA((n, 7)), pltpu.SemaphoreType.DMA((n,))]

    def _plan(self, ins, outs, sems):
        send, recv, local = sems
        x, y, c = lax.axis_index("x"), lax.axis_index("y"), lax.axis_index("c")
        me, sibling = (x, y, c), (x, y, 1 - c)
        chips = [(1 - x, y), (x, 1 - y), (1 - x, 1 - y)]

        def copy(a, k, block, to, src=None):
            dst = outs[a].at[4 * block[0] + 2 * block[1] + block[2]]
            return pltpu.make_async_remote_copy(
                src_ref=dst if src is None else src, dst_ref=dst, send_sem=send.at[a, k], recv_sem=recv.at[a, k],
                device_id=to, device_id_type=MESH)

        mine = [pltpu.make_async_copy(ins[a], outs[a].at[4 * x + 2 * y + c], local.at[a]) for a in range(self.n)]
        first = []
        for a in range(self.n):
            first.append(copy(a, 0, me, sibling, src=ins[a]))
            first += [copy(a, 1 + j, me, (*chip, c), src=ins[a]) for j, chip in enumerate(chips)]
        return me, sibling, chips, c, copy, mine, first

    def start(self, ins, outs, sems):
        *_, mine, first = self._plan(ins, outs, sems)
        for cp in mine + first:
            cp.start()

    def finish(self, ins, outs, sems):
        me, sibling, chips, c, copy, mine, first = self._plan(ins, outs, sems)
        passed = []
        for a in range(self.n):
            for j, chip in enumerate(chips):
                copy(a, 1 + j, (*chip, c), me).wait_recv()
                fwd = copy(a, 4 + j, (*chip, c), sibling)
                fwd.start()
                passed.append(fwd)
        for a in range(self.n):
            copy(a, 0, sibling, me).wait_recv()
            for j, chip in enumerate(chips):
                copy(a, 4 + j, (*chip, 1 - c), me).wait_recv()
        for cp in first + passed:
            cp.wait_send()
        for cp in mine:
            cp.wait()


class ToSibling:
    def __init__(self, parts):
        self.arrs = list(parts)
        n = self.n = len(self.arrs)
        self.out_shape = [_sds((4,) + p.shape[1:], p.dtype) for p in self.arrs]
        self.scratch = [pltpu.SemaphoreType.DMA((n,)), pltpu.SemaphoreType.DMA((n,))]

    def start(self, ins, outs, sems):
        send, recv = sems
        x, y, c = lax.axis_index("x"), lax.axis_index("y"), lax.axis_index("c")
        for a in range(self.n):
            for q in range(4):
                pltpu.make_async_remote_copy(
                    src_ref=ins[a].at[2 * q + 1 - c], dst_ref=outs[a].at[q], send_sem=send.at[a], recv_sem=recv.at[a],
                    device_id=(x, y, 1 - c), device_id_type=MESH).start()

    def finish(self, ins, outs, sems):
        send, recv = sems
        x, y, c = lax.axis_index("x"), lax.axis_index("y"), lax.axis_index("c")
        for a in range(self.n):
            pltpu.make_async_remote_copy(
                src_ref=outs[a], dst_ref=outs[a], send_sem=send.at[a], recv_sem=recv.at[a],
                device_id=(x, y, 1 - c), device_id_type=MESH).wait()


class ToChips:
    def __init__(self, cps):
        self.arrs = list(cps)
        n = self.n = len(self.arrs)
        self.out_shape = [_sds((3,) + p.shape[1:], p.dtype) for p in self.arrs]
        self.scratch = [pltpu.SemaphoreType.DMA((n,)), pltpu.SemaphoreType.DMA((n,))]

    def start(self, ins, outs, sems):
        send, recv = sems
        x, y, c = lax.axis_index("x"), lax.axis_index("y"), lax.axis_index("c")
        for a in range(self.n):
            for r, (px, py) in enumerate([(1 - x, y), (x, 1 - y), (1 - x, 1 - y)]):
                pltpu.make_async_remote_copy(
                    src_ref=ins[a].at[2 * px + py], dst_ref=outs[a].at[r], send_sem=send.at[a], recv_sem=recv.at[a],
                    device_id=(px, py, c), device_id_type=MESH).start()

    def finish(self, ins, outs, sems):
        send, recv = sems
        x, y, c = lax.axis_index("x"), lax.axis_index("y"), lax.axis_index("c")
        for a in range(self.n):
            pltpu.make_async_remote_copy(
                src_ref=outs[a], dst_ref=outs[a], send_sem=send.at[a], recv_sem=recv.at[a],
                device_id=(x, y, c), device_id_type=MESH).wait()


class Both:
    def __init__(self, a, b):
        self.a, self.b = a, b
        self.arrs, self.n = a.arrs + b.arrs, a.n + b.n
        self.out_shape, self.scratch = a.out_shape + b.out_shape, a.scratch + b.scratch

    def _each(self, ins, outs, sems):
        na, ns = self.a.n, len(self.a.scratch)
        return (self.a, (ins[:na], outs[:na], sems[:ns])), (self.b, (ins[na:], outs[na:], sems[ns:]))

    def start(self, ins, outs, sems):
        for comm, refs in self._each(ins, outs, sems):
            comm.start(*refs)

    def finish(self, ins, outs, sems):
        for comm, refs in self._each(ins, outs, sems):
            comm.finish(*refs)


def run_comm(comm, name):
    n = comm.n

    def body(*refs):
        ins, outs, sems = refs[:n], refs[n:2 * n], refs[2 * n:]
        comm.start(ins, outs, sems)
        comm.finish(ins, outs, sems)

    return pl.pallas_call(body, name=name, out_shape=comm.out_shape, in_specs=[ANY] * n, out_specs=[ANY] * n,
                          scratch_shapes=comm.scratch)(*comm.arrs)


def _pcall(body, args, *, name, grid, in_specs, out_specs, out_shape, scratch_shapes=(), comm=None):
    params = _params(len(grid))
    scratch_shapes = list(scratch_shapes)
    if comm is None:
        outs = pl.pallas_call(body, name=name, grid=grid, in_specs=in_specs, out_specs=out_specs, out_shape=out_shape,
                              scratch_shapes=scratch_shapes, compiler_params=params)(*args)
        return outs, None
    n_in, n_out, n_scr, nc = len(in_specs), len(out_specs), len(scratch_shapes), comm.n

    def hosted(*refs):
        ins, cins = refs[:n_in], refs[n_in:n_in + nc]
        o0 = n_in + nc
        outs, couts = refs[o0:o0 + n_out], refs[o0 + n_out:o0 + n_out + nc]
        s0 = o0 + n_out + nc
        scr, csems = refs[s0:s0 + n_scr], refs[s0 + n_scr:]
        first = pl.program_id(0) == 0
        last = pl.program_id(0) == grid[0] - 1
        for ax in range(1, len(grid)):
            first = jnp.logical_and(first, pl.program_id(ax) == 0)
            last = jnp.logical_and(last, pl.program_id(ax) == grid[ax] - 1)

        @pl.when(first)
        def _():
            comm.start(cins, couts, csems)
        body(*ins, *outs, *scr)

        @pl.when(last)
        def _():
            comm.finish(cins, couts, csems)

    res = pl.pallas_call(
        hosted, name=name, grid=grid, in_specs=list(in_specs) + [ANY] * nc, out_specs=list(out_specs) + [ANY] * nc,
        out_shape=list(out_shape) + comm.out_shape, scratch_shapes=scratch_shapes + comm.scratch,
        compiler_params=params)(*args, *comm.arrs)
    return res[:n_out], res[n_out:]


def add_core_side(parts, from_sibling, core, name):
    _, rows, cols = parts.shape
    tr = _row_tile(rows, 512)

    def body(c_ref, a_ref, b_ref, o_ref):
        o_ref[...] = (a_ref[...].astype(F32) + b_ref[...].astype(F32)).astype(o_ref.dtype)

    side = pl.BlockSpec((None, tr, cols), lambda q, i, c: (q, i, 0))
    return pl.pallas_call(
        body, name=name,
        grid_spec=pltpu.PrefetchScalarGridSpec(
            num_scalar_prefetch=1, grid=(4, rows // tr),
            in_specs=[pl.BlockSpec((None, tr, cols), lambda q, i, c: (2 * q + c[0], i, 0)), side], out_specs=side),
        out_shape=_sds(from_sibling.shape, BF), compiler_params=_params(2))(core, parts, from_sibling)


def adamw(w, m, v, pieces, name):
    rows, cols = w.shape
    tr = _row_tile(rows, 256) if rows % 16 == 0 else rows
    np_ = len(pieces)
    c1 = 1.0 / (1.0 - ADAM_B1 ** ADAM_STEP)
    c2 = 1.0 / (1.0 - ADAM_B2 ** ADAM_STEP)

    def body(*refs):
        w_ref, m_ref, v_ref = refs[:3]
        p_refs = refs[3:3 + np_]
        g_ref, d_ref, nm_ref, nv_ref = refs[3 + np_:]
        g = p_refs[0][...].astype(F32)
        for pr in p_refs[1:]:
            g = g + pr[...].astype(F32)
        nm = ADAM_B1 * m_ref[...] + (1.0 - ADAM_B1) * g
        nv = ADAM_B2 * v_ref[...] + (1.0 - ADAM_B2) * (g * g)
        g_ref[...] = g
        nm_ref[...] = nm
        nv_ref[...] = nv
        d_ref[...] = -ADAM_LR * ((nm * c1) / (jnp.sqrt(nv * c2) + ADAM_EPS) + ADAM_WD * w_ref[...])

    spec = pl.BlockSpec((tr, cols), lambda i: (i, 0))
    p_specs = []
    for arr, k in pieces:
        if k is None:
            p_specs.append(spec)
        else:
            p_specs.append(pl.BlockSpec((None, tr, cols), functools.partial(lambda i, kk: (kk, i, 0), kk=k)))
    out = _sds(w.shape, F32)
    return pl.pallas_call(body, name=name, grid=(rows // tr,), in_specs=[spec] * 3 + p_specs, out_specs=[spec] * 4,
                          out_shape=[out] * 4, compiler_params=_params(1))(w, m, v, *[a for a, _ in pieces])


OWN_CHIP = "own chip"


def adamw_layers(w, m, v, pieces, chip, name):
    _, rows, cols = w.shape
    tr = _row_tile(rows, 256)
    nt = rows // tr
    counts = [len(pieces[0]), len(pieces[1])]
    c1 = 1.0 / (1.0 - ADAM_B1 ** ADAM_STEP)
    c2 = 1.0 / (1.0 - ADAM_B2 ** ADAM_STEP)

    def body(chip_ref, *refs):
        w_ref, m_ref, v_ref = refs[:3]
        p_refs = refs[3:3 + sum(counts)]
        g_ref, d_ref, nm_ref, nv_ref = refs[3 + sum(counts):]
        sums = []
        for group in (p_refs[:counts[0]], p_refs[counts[0]:]):
            s = group[0][...].astype(F32)
            for pr in group[1:]:
                s = s + pr[...].astype(F32)
            sums.append(s)
        g = jnp.where(pl.program_id(0) == 0, sums[0], sums[1])
        nm = ADAM_B1 * m_ref[...] + (1.0 - ADAM_B1) * g
        nv = ADAM_B2 * v_ref[...] + (1.0 - ADAM_B2) * (g * g)
        g_ref[...] = g
        nm_ref[...] = nm
        nv_ref[...] = nv
        d_ref[...] = -ADAM_LR * ((nm * c1) / (jnp.sqrt(nv * c2) + ADAM_EPS) + ADAM_WD * w_ref[...])

    def rows_of(layer):
        parked = nt - 1 if layer == 0 else 0
        return lambda l, i: jnp.where(l == layer, i, parked)

    spec = pl.BlockSpec((None, tr, cols), lambda l, i, c: (l, i, 0))
    p_specs, p_args = [], []
    for layer in (0, 1):
        row_of = rows_of(layer)
        for arr, k in pieces[layer]:
            p_args.append(arr)
            if k is None:
                p_specs.append(pl.BlockSpec((tr, cols), functools.partial(lambda l, i, c, f: (f(l, i), 0), f=row_of)))
            elif k == OWN_CHIP:
                p_specs.append(pl.BlockSpec((None, tr, cols), functools.partial(lambda l, i, c, f: (c[0], f(l, i), 0), f=row_of)))
            else:
                p_specs.append(pl.BlockSpec((None, tr, cols), functools.partial(lambda l, i, c, f, kk: (kk, f(l, i), 0), f=row_of, kk=k)))
    out = _sds(w.shape, F32)
    return pl.pallas_call(
        body, name=name,
        grid_spec=pltpu.PrefetchScalarGridSpec(num_scalar_prefetch=1, grid=(2, nt), in_specs=[spec] * 3 + p_specs,
                                               out_specs=[spec] * 4),
        out_shape=[out] * 4, compiler_params=_params(2))(chip, w, m, v, *p_args)


def sum_slabs(g, name):
    n, rows, cols = g.shape

    def body(g_ref, o_ref):
        s = g_ref[0]
        for k in range(1, n):
            s = s + g_ref[k]
        o_ref[...] = s

    return pl.pallas_call(body, name=name, out_shape=_sds((rows, cols), F32))(g)


def norm_proj(h, g, w, name, transposed=False, comm=None):
    s_len = h.shape[0]
    nb, tn = (w.shape[0], w.shape[1]) if transposed else (w.shape[0], w.shape[2])
    tm = min(TMP, s_len)
    nt = s_len // tm
    matmul = _dot_nt if transposed else _dot

    def body(h_ref, g_ref, w_ref, o_ref, hn_ref, hn_s):
        rows = pl.ds(pl.multiple_of(pl.program_id(1) * tm, tm), tm)

        @pl.when(pl.program_id(0) == 0)
        def _():
            x = h_ref[...]
            hn = (x * _rstd(x) * g_ref[...]).astype(BF)
            hn_s[rows, :] = hn
            hn_ref[...] = hn
        o_ref[...] = matmul(hn_s[rows, :], w_ref[...]).astype(BF)

    def first_pass_rows(j, i):
        return (jnp.where(j == 0, i, nt - 1), 0)

    return _pcall(
        body, (h, g, w), name=name, grid=(nb, nt),
        in_specs=[pl.BlockSpec((tm, D), first_pass_rows), pl.BlockSpec((1, D), lambda j, i: (0, 0)),
                  pl.BlockSpec((None,) + w.shape[1:], lambda j, i: (j, 0, 0))],
        out_specs=[pl.BlockSpec((tm, tn), lambda j, i: (i, j)), pl.BlockSpec((tm, D), first_pass_rows)],
        out_shape=[_sds((s_len, nb * tn), BF), _sds((s_len, D), BF)],
        scratch_shapes=[pltpu.VMEM((s_len, D), BF)], comm=comm)


def _sgu_mix(ws_ref, vln_s, mix_s, bs_ref, tm, transposed):
    mask = _sgu_mask(transposed)
    for hd in range(HEADS):
        wm = jnp.where(mask, ws_ref[hd], 0.0).astype(BF)
        cs = slice(hd * BLK, (hd + 1) * BLK)
        for n in range(tm // BLK):
            rs = slice(n * BLK, (n + 1) * BLK)
            r = _dot(wm, vln_s[rs, cs])
            mix_s[rs, cs] = r if bs_ref is None else r + bs_ref[:, cs]


def sgu_fwd(proj, ws, bsfull, gv, bv, wo, name):
    s_len = proj.shape[0]
    tm = min(TM, s_len)

    def body(zu_ref, zv_ref, ws_ref, bs_ref, gv_ref, bv_ref, wo_ref, mix_ref, sgu_ref, br_ref, vln_s, mix_s):
        u, _ = _gelu(zu_ref[...].astype(F32))
        v, _ = _gelu(zv_ref[...].astype(F32))
        vhat, _ = _ln_fwd(v)
        vln_s[...] = (vhat * gv_ref[...] + bv_ref[...]).astype(BF)
        _sgu_mix(ws_ref, vln_s, mix_s, bs_ref, tm, False)
        mixed = mix_s[...].astype(BF)
        mix_ref[...] = mixed
        sgu = (u * mixed.astype(F32)).astype(BF)
        sgu_ref[...] = sgu
        br_ref[...] = _dot(sgu, wo_ref[...].reshape(D, D)).astype(BF)

    return pl.pallas_call(
        body, name=name, grid=(s_len // tm,),
        in_specs=[_cols(tm, D, 0), _cols(tm, D, 1), _whole((HEADS, BLK, BLK)), _whole((BLK, D)), _whole((1, D)), _whole((1, D)),
                  _rowsharded(BLK)],
        out_specs=[_cols(tm, D, 0)] * 3, out_shape=[_sds((s_len, D), BF)] * 3,
        scratch_shapes=[pltpu.VMEM((tm, D), BF), pltpu.VMEM((tm, D), F32)], compiler_params=_params(1),
    )(proj, proj, ws, bsfull, gv, bv, wo)


def _causal_conv(ext_s, out_s, wdw_ref, bias_ref, tm):
    def chunk(ci, carry):
        r0 = pl.multiple_of((ci // (D // LC)) * RC, RC)
        l0 = pl.multiple_of((ci % (D // LC)) * LC, LC)
        win = ext_s[pl.ds(r0, RC + HALO), pl.ds(l0, LC)]
        acc = jnp.broadcast_to(bias_ref[:, pl.ds(l0, LC)], (RC, LC))
        for r in range(8):
            wr = win if r == 0 else pltpu.roll(win, r, 0)
            for m in range(4):
                d = 8 * m + r
                if d < CONV_W:
                    k = CONV_W - 1 - d
                    acc = acc + wdw_ref[k:k + 1, pl.ds(l0, LC)] * wr[HALO - 8 * m:HALO - 8 * m + RC]
        out_s[pl.ds(r0, RC), pl.ds(l0, LC)] = acc
        return carry
    lax.fori_loop(0, (tm // RC) * (D // LC), chunk, 0)


def _glu_ext(a_ref, g_ref, ah_ref, gh_ref, ext_s, first):
    hh = ah_ref[...].astype(F32) * _sig(gh_ref[...].astype(F32))
    ext_s[0:HALO, :] = jnp.where(first, 0.0, hh)
    ext_s[HALO:, :] = a_ref[...].astype(F32) * _sig(g_ref[...].astype(F32))


def conv_fwd(proj, wdw, bdw, gln, bln, wo, name, comm=None):
    s_len = proj.shape[0]
    tm = min(TM, s_len)

    def body(a_ref, g_ref, ah_ref, gh_ref, wdw_ref, bdw_ref, gln_ref, bln_ref, wo_ref, cv_ref, cb_ref, br_ref, ext_s, conv_s):
        _glu_ext(a_ref, g_ref, ah_ref, gh_ref, ext_s, pl.program_id(0) == 0)
        _causal_conv(ext_s, conv_s, wdw_ref, bdw_ref, tm)
        cv = conv_s[...].astype(BF)
        cv_ref[...] = cv
        chat, _ = _ln_fwd(cv.astype(F32))
        yl = chat * gln_ref[...] + bln_ref[...]
        cb = (yl * _sig(yl)).astype(BF)
        cb_ref[...] = cb
        br_ref[...] = _dot(cb, wo_ref[...].reshape(D, D)).astype(BF)

    return _pcall(
        body, (proj, proj, proj, proj, wdw, bdw, gln, bln, wo), name=name, grid=(s_len // tm,),
        in_specs=[_cols(tm, D, 2), _cols(tm, D, 3), _prev_halo(tm, D, 2), _prev_halo(tm, D, 3), _whole((HALO, D)),
                  _whole((1, D)), _whole((1, D)), _whole((1, D)), _rowsharded(BLK)],
        out_specs=[_cols(tm, D, 0)] * 3, out_shape=[_sds((s_len, D), BF)] * 3,
        scratch_shapes=[pltpu.VMEM((tm + HALO, D), F32), pltpu.VMEM((tm, D), F32)], comm=comm)


def pool_fwd(proj, wpool, spool, wo, name):
    s_len = proj.shape[0]
    tm = min(TM, s_len)

    def body(z_ref, zh_ref, wp_ref, sp_ref, wo_ref, pooled_ref, pm_ref, br_ref, ext_s, mr_s):
        i = pl.program_id(0)
        ext_s[0:HALO, :] = jnp.where(i == 0, 0.0, zh_ref[...].astype(F32))
        ext_s[HALO:, :] = z_ref[...].astype(F32)
        for gi, w in enumerate(POOL_WINDOWS):
            cs = slice(gi * POOL_GD, (gi + 1) * POOL_GD)
            e = ext_s[:, cs]
            s = e
            sh = 1
            while sh < w:
                s = s + pltpu.roll(s, sh, 0)
                sh *= 2
            pooled = (s[HALO:] * _inv_count(i, tm, w) - e[HALO:]).astype(BF)
            pooled_ref[:, cs] = pooled
            mr_s[:, cs] = _dot(pooled, wp_ref[gi])
        pm = (mr_s[...] * sp_ref[...]).astype(BF)
        pm_ref[...] = pm
        br_ref[...] = _dot(pm, wo_ref[...].reshape(D, D)).astype(BF)

    return pl.pallas_call(
        body, name=name, grid=(s_len // tm,),
        in_specs=[_cols(tm, D, 4), _prev_halo(tm, D, 4), _whole((4, POOL_GD, POOL_GD)), _whole((1, D)), _rowsharded(BLK)],
        out_specs=[_cols(tm, D, 0)] * 3, out_shape=[_sds((s_len, D), BF)] * 3,
        scratch_shapes=[pltpu.VMEM((tm + HALO, D), F32), pltpu.VMEM((tm, D), F32)], compiler_params=_params(1),
    )(proj, proj, wpool, spool, wo)


def merge_out(proj, bra, brb, brc, h, wout, gpost, name, comm=None):
    s_len = h.shape[0]
    tm = min(TM, s_len)

    def body(z0, z1, z2, a_ref, b_ref, c_ref, h_ref, wo_ref, g_ref, mg_ref, mo_ref, h1_ref):
        merged = (_sig(z0[...].astype(F32)) * a_ref[...].astype(F32) + _sig(z1[...].astype(F32)) * b_ref[...].astype(F32)
                  + _sig(z2[...].astype(F32)) * c_ref[...].astype(F32)).astype(BF)
        mg_ref[...] = merged
        mo = _dot(merged, wo_ref[...].reshape(D, D))
        mo_ref[...] = mo.astype(BF)
        h1_ref[...] = h_ref[...] + mo * _rstd(mo) * g_ref[...]

    row = _cols(tm, D, 0)
    return _pcall(
        body, (proj, proj, proj, bra, brb, brc, h, wout, gpost), name=name, grid=(s_len // tm,),
        in_specs=[_cols(tm, D, 5), _cols(tm, D, 6), _cols(tm, D, 7), row, row, row, row, _rowsharded(BLK), _whole((1, D))],
        out_specs=[row] * 3, out_shape=[_sds((s_len, D), BF), _sds((s_len, D), BF), _sds((s_len, D), F32)], comm=comm)


def _p_spec(tm, layer):
    return pl.BlockSpec((None, None, tm, PLE), lambda i: (layer, 0, i, 0))


def ffn_out(ff, h1, p, wfo, gpost, wpg, wple, layer, name, comm=None):
    s_len = h1.shape[0]
    tm = min(TMB, s_len)

    def body(fg_ref, fu_ref, h1_ref, p_ref, wfo_ref, g_ref, wpg_ref, wple_ref, act_ref, f_ref, h2_ref, pg_ref, h3_ref):
        gt = fg_ref[...].astype(F32)
        act = (gt * _sig(gt) * fu_ref[...].astype(F32)).astype(BF)
        act_ref[...] = act
        f = _dot(act, wfo_ref[...].reshape(D_FF, D))
        f_ref[...] = f.astype(BF)
        h2 = h1_ref[...] + f * _rstd(f) * g_ref[...]
        h2_ref[...] = h2
        pg = _dot(h2.astype(BF), wpg_ref[...].reshape(D, D)).astype(BF)
        pg_ref[...] = pg
        pe = _dot(p_ref[...].astype(BF), wple_ref[...])
        h3_ref[...] = h2 + _sig(pg.astype(F32)) * pe

    row = _cols(tm, D, 0)
    return _pcall(
        body, (ff, ff, h1, p, wfo, gpost, wpg, wple), name=name, grid=(s_len // tm,),
        in_specs=[_cols(tm, D_FF, 0), _cols(tm, D_FF, 1), row, _p_spec(tm, layer), _rowsharded(D_FF // N_DEV),
                  _whole((1, D)), _rowsharded(BLK), _whole((PLE, D))],
        out_specs=[_cols(tm, D_FF, 0), row, row, row, row],
        out_shape=[_sds((s_len, D_FF), BF), _sds((s_len, D), BF), _sds((s_len, D), F32), _sds((s_len, D), BF), _sds((s_len, D), F32)],
        comm=comm)


def ple_ffn_bwd(dh3, pg, p, f, ff, wple, wpg, wfo, gpost, layer, name, target=None, comm=None):
    s_len = dh3.shape[0]
    tm = min(TMW, s_len)
    nt = s_len // tm
    with_loss = target is not None

    def body(*refs):
        if with_loss:
            t_ref, refs, loss_ref, loss_acc = refs[0], refs[1:-2], refs[-2], refs[-1]
        (dh3_ref, pg_ref, p_ref, f_ref, fg_ref, fu_ref, wple_ref, wpg_ref, wfo_ref, g_ref,
         dh2_ref, dpe_ref, dpg_ref, df_ref, dff_ref, dg_ref) = refs
        i = pl.program_id(0)
        dh3v = dh3_ref[...]
        if with_loss:
            err = dh3v - t_ref[...]
            dh3v = err * (1.0 / D)
            _accum(loss_acc, i == 0, _colsum(err * err))

            @pl.when(i == nt - 1)
            def _():
                loss_ref[...] = jnp.broadcast_to(jnp.sum(loss_acc[...], axis=1, keepdims=True) * (0.5 / D), (1, LC))
        s = _sig(pg_ref[...].astype(F32))
        pe = _dot(p_ref[...].astype(BF), wple_ref[...])
        dpe_ref[...] = (dh3v * s).astype(BF)
        dpg = (dh3v * pe * s * (1.0 - s)).astype(BF)
        dpg_ref[...] = dpg
        dh2 = dh3v + _dot_nt(dpg, wpg_ref[...].reshape(D, D))
        dh2_ref[...] = dh2
        fv = f_ref[...].astype(F32)
        r = _rstd(fv)
        _accum(dg_ref, i == 0, _colsum(dh2 * fv * r))
        df = _rms_bwd(fv, dh2 * g_ref[...], r).astype(BF)
        df_ref[...] = df
        dact = _dot_nt(df, wfo_ref[...].reshape(D_FF, D))
        gt = fg_ref[...].astype(F32)
        sg = _sig(gt)
        up = fu_ref[...].astype(F32)
        dff_ref[:, 0:D_FF] = (dact * up * sg * (1.0 + gt * (1.0 - sg))).astype(BF)
        dff_ref[:, D_FF:2 * D_FF] = (dact * gt * sg).astype(BF)

    row = _cols(tm, D, 0)
    args = (dh3, pg, p, f, ff, ff, wple, wpg, wfo, gpost)
    in_specs = [row, row, _p_spec(tm, layer), row, _cols(tm, D_FF, 0), _cols(tm, D_FF, 1), _whole((PLE, D)),
                _rowsharded(BLK), _rowsharded(D_FF // N_DEV), _whole((1, D))]
    out_specs = [row, row, row, row, _cols(tm, 2 * D_FF, 0), _whole((1, D))]
    out_shape = [_sds((s_len, D), F32), _sds((s_len, D), BF), _sds((s_len, D), BF), _sds((s_len, D), BF),
                 _sds((s_len, 2 * D_FF), BF), _sds((1, D), F32)]
    scratch = []
    if with_loss:
        args, in_specs = (target,) + args, [row] + in_specs
        out_specs, out_shape = out_specs + [_whole((1, LC))], out_shape + [_sds((1, LC), F32)]
        scratch = [pltpu.VMEM((1, D), F32)]
    return _pcall(body, args, name=name, grid=(nt,), in_specs=in_specs, out_specs=out_specs, out_shape=out_shape,
                  scratch_shapes=scratch, comm=comm)


def ffn_in_bwd(dff, wt, h1, dh2, gpre, name, comm=None):
    s_len = h1.shape[0]
    tm = min(TMW, s_len)
    nb, tn, _ = wt.shape

    def body(dff_ref, w_ref, h1_ref, dh2_ref, g_ref, dh1_ref, dg_ref):
        dhn = _dot(dff_ref[:, 0:tn], w_ref[0])
        for j in range(1, nb):
            dhn = dhn + _dot(dff_ref[:, j * tn:(j + 1) * tn], w_ref[j])
        x = h1_ref[...]
        r = _rstd(x)
        _accum(dg_ref, pl.program_id(0) == 0, _colsum(dhn * x * r))
        dh1_ref[...] = dh2_ref[...] + _rms_bwd(x, dhn * g_ref[...], r)

    row = _cols(tm, D, 0)
    return _pcall(
        body, (dff, wt, h1, dh2, gpre), name=name, grid=(s_len // tm,),
        in_specs=[_cols(tm, nb * tn, 0), pl.BlockSpec((nb, tn, D), lambda i: (0, 0, 0), pipeline_mode=pl.Buffered(1)), row, row,
                  _whole((1, D))],
        out_specs=[row, _whole((1, D))],
        out_shape=[_sds((s_len, D), F32), _sds((1, D), F32)], comm=comm)


def mix_post_bwd(dh1, mo, gpost, wout, proj, bra, brb, brc, win, name, comm=None):
    s_len = dh1.shape[0]
    tm = min(TMB, s_len)

    def body(dh1_ref, mo_ref, g_ref, wo_ref, z0, z1, z2, a_ref, b_ref, c_ref, w5, w6, w7,
             dmo_ref, da_ref, db_ref, dc_ref, dz_ref, dhn_ref, dg_ref):
        i = pl.program_id(0)
        dh1v = dh1_ref[...]
        mo_v = mo_ref[...].astype(F32)
        r = _rstd(mo_v)
        _accum(dg_ref, i == 0, _colsum(dh1v * mo_v * r))
        dmo = _rms_bwd(mo_v, dh1v * g_ref[...], r).astype(BF)
        dmo_ref[...] = dmo
        dmerged = _dot_nt(dmo, wo_ref[...].reshape(D, D))
        dhn = jnp.zeros((tm, D), F32)
        for k, (z, br, dbr, w) in enumerate(((z0, a_ref, da_ref, w5), (z1, b_ref, db_ref, w6), (z2, c_ref, dc_ref, w7))):
            s = _sig(z[...].astype(F32))
            dbr[...] = (dmerged * s).astype(BF)
            dz = (dmerged * br[...].astype(F32) * s * (1.0 - s)).astype(BF)
            dz_ref[:, k * D:(k + 1) * D] = dz
            dhn = dhn + _dot_nt(dz, w[...])
        dhn_ref[...] = dhn

    row = _cols(tm, D, 0)
    return _pcall(
        body, (dh1, mo, gpost, wout, proj, proj, proj, bra, brb, brc, win, win, win), name=name, grid=(s_len // tm,),
        in_specs=[row, row, _whole((1, D)), _rowsharded(BLK), _cols(tm, D, 5), _cols(tm, D, 6), _cols(tm, D, 7), row, row, row,
                  _win_block(5), _win_block(6), _win_block(7)],
        out_specs=[row, row, row, row, _cols(tm, 3 * D, 0), row, _whole((1, D))],
        out_shape=[_sds((s_len, D), BF)] * 4 + [_sds((s_len, 3 * D), BF), _sds((s_len, D), F32), _sds((1, D), F32)], comm=comm)


def sgu_bwd(dbr, wo, proj, mixed, wst, gv, bv, win, dhn_in, name, comm=None):
    s_len = dbr.shape[0]
    tm = min(TMB, s_len)
    nt = s_len // tm

    def body(dbr_ref, wo_ref, zu_ref, zv_ref, mix_ref, wst_ref, gv_ref, bv_ref, w0, w1, dhn_in_ref,
             dz_ref, dhn_ref, dws_ref, dbs_ref, dgv_ref, dbv_ref, vln_s, dmix_s, dvln_s, bs_acc):
        i = pl.program_id(0)
        first = i == 0
        dsgu = _dot_nt(dbr_ref[...], wo_ref[...].reshape(D, D))
        zu = zu_ref[...].astype(F32)
        zv = zv_ref[...].astype(F32)
        u, tu = _gelu(zu)
        v, tv = _gelu(zv)
        vhat, rs = _ln_fwd(v)
        vln_s[...] = (vhat * gv_ref[...] + bv_ref[...]).astype(BF)
        du = dsgu * mix_ref[...].astype(F32)
        dmix = dsgu * u
        dmix_s[...] = dmix.astype(BF)
        blocks = dmix[0:BLK]
        for n in range(1, tm // BLK):
            blocks = blocks + dmix[n * BLK:(n + 1) * BLK]
        _accum(bs_acc, first, blocks)
        for hd in range(HEADS):
            cs = slice(hd * BLK, (hd + 1) * BLK)
            g = _dot_nt(dmix_s[0:BLK, cs], vln_s[0:BLK, cs])
            for n in range(1, tm // BLK):
                g = g + _dot_nt(dmix_s[n * BLK:(n + 1) * BLK, cs], vln_s[n * BLK:(n + 1) * BLK, cs])

            @pl.when(first)
            def _():
                dws_ref[hd] = g

            @pl.when(jnp.logical_not(first))
            def _():
                dws_ref[hd] += g
        _sgu_mix(wst_ref, dmix_s, dvln_s, None, tm, True)
        dvln = dvln_s[...]
        _accum(dgv_ref, first, _colsum(dvln * vhat))
        _accum(dbv_ref, first, _colsum(dvln))
        dv = _ln_bwd(dvln * gv_ref[...], vhat, rs)
        dzu = (du * _gelu_grad(zu, tu)).astype(BF)
        dzv = (dv * _gelu_grad(zv, tv)).astype(BF)
        dz_ref[:, 0:D] = dzu
        dz_ref[:, D:2 * D] = dzv
        dhn_ref[...] = dhn_in_ref[...] + _dot_nt(dzu, w0[...]) + _dot_nt(dzv, w1[...])

        @pl.when(i == nt - 1)
        def _():
            mask = _sgu_mask(False)
            for hd in range(HEADS):
                dws_ref[hd] = jnp.where(mask, dws_ref[hd], 0.0)
                dbs_ref[:, hd:hd + 1] = jnp.sum(bs_acc[:, hd * BLK:(hd + 1) * BLK], axis=1, keepdims=True)

    row = _cols(tm, D, 0)
    vec = _whole((1, D))
    return _pcall(
        body, (dbr, wo, proj, proj, mixed, wst, gv, bv, win, win, dhn_in), name=name, grid=(nt,),
        in_specs=[row, _rowsharded(BLK), _cols(tm, D, 0), _cols(tm, D, 1), row, _whole((HEADS, BLK, BLK)),
                  vec, vec, _win_block(0), _win_block(1), row],
        out_specs=[_cols(tm, 2 * D, 0), row, _whole((HEADS, BLK, BLK)), _whole((BLK, HEADS)), vec, vec],
        out_shape=[_sds((s_len, 2 * D), BF), _sds((s_len, D), F32), _sds((HEADS, BLK, BLK), F32), _sds((BLK, HEADS), F32),
                   _sds((1, D), F32), _sds((1, D), F32)],
        scratch_shapes=[pltpu.VMEM((tm, D), BF), pltpu.VMEM((tm, D), BF), pltpu.VMEM((tm, D), F32), pltpu.VMEM((BLK, D), F32)],
        comm=comm)


def conv_bwd(dbr, wo, proj, conv, gln, bln, name, comm=None):
    s_len = dbr.shape[0]
    tm = min(TMB, s_len)
    nt = s_len // tm

    def body(dbr_ref, wo_ref, a_ref, g_ref, ah_ref, gh_ref, cv_ref, gln_ref, bln_ref,
             dc_ref, dw_ref, dbdw_ref, dgln_ref, dbln_ref, ext_s, dc_s, dw_acc):
        i = pl.program_id(0)
        first = i == 0
        dcb = _dot_nt(dbr_ref[...], wo_ref[...].reshape(D, D))
        _glu_ext(a_ref, g_ref, ah_ref, gh_ref, ext_s, first)
        chat, rs = _ln_fwd(cv_ref[...].astype(F32))
        yl = chat * gln_ref[...] + bln_ref[...]
        sy = _sig(yl)
        dyl = dcb * sy * (1.0 + yl * (1.0 - sy))
        _accum(dgln_ref, first, _colsum(dyl * chat))
        _accum(dbln_ref, first, _colsum(dyl))
        dc = _ln_bwd(dyl * gln_ref[...], chat, rs)
        _accum(dbdw_ref, first, _colsum(dc))
        dc_ref[...] = dc.astype(BF)
        dc_s[...] = dc

        @pl.when(first)
        def _():
            dw_acc[...] = jnp.zeros_like(dw_acc)

        def chunk(ci, carry):
            r0 = pl.multiple_of((ci // (D // LC)) * RC, RC)
            l0 = pl.multiple_of((ci % (D // LC)) * LC, LC)
            win = ext_s[pl.ds(r0, RC + HALO), pl.ds(l0, LC)]
            dcw = dc_s[pl.ds(r0, RC), pl.ds(l0, LC)]
            for r in range(8):
                wr = win if r == 0 else pltpu.roll(win, r, 0)
                for m in range(4):
                    d = 8 * m + r
                    if d < CONV_W:
                        k = CONV_W - 1 - d
                        prod = dcw * wr[HALO - 8 * m:HALO - 8 * m + RC]
                        dw_acc[k * 8:(k + 1) * 8, pl.ds(l0, LC)] += prod.reshape(RC // 8, 8, LC).sum(axis=0)
            return carry
        lax.fori_loop(0, (tm // RC) * (D // LC), chunk, 0)

        @pl.when(i == nt - 1)
        def _():
            dw_ref[...] = dw_acc[...].reshape(HALO, 8, D).sum(axis=1)

    row = _cols(tm, D, 0)
    vec = _whole((1, D))
    return _pcall(
        body, (dbr, wo, proj, proj, proj, proj, conv, gln, bln), name=name, grid=(nt,),
        in_specs=[row, _rowsharded(BLK), _cols(tm, D, 2), _cols(tm, D, 3), _prev_halo(tm, D, 2), _prev_halo(tm, D, 3),
                  row, vec, vec],
        out_specs=[row, _whole((HALO, D)), vec, vec, vec],
        out_shape=[_sds((s_len, D), BF), _sds((HALO, D), F32), _sds((1, D), F32), _sds((1, D), F32), _sds((1, D), F32)],
        scratch_shapes=[pltpu.VMEM((tm + HALO, D), F32), pltpu.VMEM((tm, D), F32), pltpu.VMEM((HALO * 8, D), F32)], comm=comm)


def pool_bwd(dbr, wo, pooled, wpool, spool, name):
    s_len = dbr.shape[0]
    tm = min(TMB, s_len)

    def body(dbr_ref, wo_ref, pl_ref, wp_ref, sp_ref, dmr_ref, q_ref, dsp_ref, mr_s):
        i = pl.program_id(0)
        dpm = _dot_nt(dbr_ref[...], wo_ref[...].reshape(D, D))
        for gi in range(4):
            cs = slice(gi * POOL_GD, (gi + 1) * POOL_GD)
            mr_s[:, cs] = _dot(pl_ref[:, cs], wp_ref[gi])
        _accum(dsp_ref, i == 0, _colsum(dpm * mr_s[...]))
        dmr = (dpm * sp_ref[...]).astype(BF)
        dmr_ref[...] = dmr
        for gi, w in enumerate(POOL_WINDOWS):
            cs = slice(gi * POOL_GD, (gi + 1) * POOL_GD)
            q_ref[:, cs] = (_dot_nt(dmr[:, cs], wp_ref[gi]) * _inv_count(i, tm, w)).astype(BF)

    row = _cols(tm, D, 0)
    return pl.pallas_call(
        body, name=name, grid=(s_len // tm,),
        in_specs=[row, _rowsharded(BLK), row, _whole((4, POOL_GD, POOL_GD)), _whole((1, D))],
        out_specs=[row, row, _whole((1, D))],
        out_shape=[_sds((s_len, D), BF), _sds((s_len, D), BF), _sds((1, D), F32)],
        scratch_shapes=[pltpu.VMEM((tm, D), F32)], compiler_params=_params(1))(dbr, wo, pooled, wpool, spool)


def seq_bwd(dc, q, proj, wdw, win, dhn_in, h, dh1, gpre, name, comm=None):
    s_len = dc.shape[0]
    tm = min(TMB, s_len)
    nt = s_len // tm

    def body(dc_ref, dch_ref, q_ref, qh_ref, a_ref, g_ref, wdw_ref, w2, w3, w4, dhn_in_ref, h_ref, dh1_ref, gpre_ref,
             dz_ref, dh_ref, dgpre_ref, ext_s, dhc_s, qext_s):
        i = pl.program_id(0)
        last = i == nt - 1
        ext_s[0:tm, :] = dc_ref[...].astype(F32)
        ext_s[tm:, :] = jnp.where(last, 0.0, dch_ref[...].astype(F32))

        def chunk(ci, carry):
            r0 = pl.multiple_of((ci // (D // LC)) * RC, RC)
            l0 = pl.multiple_of((ci % (D // LC)) * LC, LC)
            win_ = ext_s[pl.ds(r0, RC + HALO), pl.ds(l0, LC)]
            acc = jnp.zeros((RC, LC), F32)
            for r in range(8):
                wr = win_ if r == 0 else pltpu.roll(win_, RC + HALO - r, 0)
                for m in range(4):
                    d = 8 * m + r
                    if d < CONV_W:
                        k = CONV_W - 1 - d
                        acc = acc + wdw_ref[k:k + 1, pl.ds(l0, LC)] * wr[8 * m:8 * m + RC]
            dhc_s[pl.ds(r0, RC), pl.ds(l0, LC)] = acc
            return carry
        lax.fori_loop(0, (tm // RC) * (D // LC), chunk, 0)

        dhc = dhc_s[...]
        av = a_ref[...].astype(F32)
        sg = _sig(g_ref[...].astype(F32))
        da = (dhc * sg).astype(BF)
        dg = (dhc * av * sg * (1.0 - sg)).astype(BF)
        dz_ref[:, 0:D] = da
        dz_ref[:, D:2 * D] = dg

        qext_s[0:tm, :] = q_ref[...].astype(F32)
        qext_s[tm:, :] = jnp.where(last, 0.0, qh_ref[...].astype(F32))
        for gi, w in enumerate(POOL_WINDOWS):
            cs = slice(gi * POOL_GD, (gi + 1) * POOL_GD)
            e = qext_s[:, cs]
            s = e
            sh = 1
            while sh < w:
                s = s + pltpu.roll(s, tm + HALO - sh, 0)
                sh *= 2
            dz_ref[:, 2 * D + gi * POOL_GD:2 * D + (gi + 1) * POOL_GD] = (s[0:tm] - e[0:tm] * _count(i, tm, w)).astype(BF)
        dhn = dhn_in_ref[...] + _dot_nt(da, w2[...]) + _dot_nt(dg, w3[...]) + _dot_nt(dz_ref[:, 2 * D:3 * D], w4[...])
        x = h_ref[...]
        r = _rstd(x)
        _accum(dgpre_ref, i == 0, _colsum(dhn * x * r))
        dh_ref[...] = dh1_ref[...] + _rms_bwd(x, dhn * gpre_ref[...], r)

    row = _cols(tm, D, 0)
    return _pcall(
        body, (dc, dc, q, q, proj, proj, wdw, win, win, win, dhn_in, h, dh1, gpre), name=name, grid=(nt,),
        in_specs=[row, _next_halo(tm, D, 0, s_len), row, _next_halo(tm, D, 0, s_len), _cols(tm, D, 2), _cols(tm, D, 3),
                  _whole((HALO, D)), _win_block(2), _win_block(3), _win_block(4), row, row, row, _whole((1, D))],
        out_specs=[_cols(tm, 3 * D, 0), row, _whole((1, D))],
        out_shape=[_sds((s_len, 3 * D), BF), _sds((s_len, D), F32), _sds((1, D), F32)],
        scratch_shapes=[pltpu.VMEM((tm + HALO, D), F32), pltpu.VMEM((tm, D), F32), pltpu.VMEM((tm + HALO, D), F32)], comm=comm)


def wgrad(a, b, tk, tn, name, stacked=False, diag=False, a_spec=None, comm=None):
    s_len = b.shape[0]
    k_dim = a.shape[-1]
    n_dim = b.shape[1]
    ts = min(TS, s_len)
    nk = 1 if diag else k_dim // tk
    nn, ns = n_dim // tn, s_len // ts

    def body(a_ref, b_ref, o_ref, acc):
        s = pl.program_id(2)
        _accum(acc, s == 0, _dot_tn(a_ref[...].astype(BF), b_ref[...].astype(BF)))

        @pl.when(s == ns - 1)
        def _():
            o_ref[...] = acc[...].astype(BF).reshape(o_ref.shape)

    if a_spec is None:
        a_spec = pl.BlockSpec((ts, tk), (lambda k, n, s: (s, n)) if diag else (lambda k, n, s: (s, k)))
    if stacked or diag:
        out_shape = _sds((nn, tk if diag else k_dim, tn), BF)
        o_spec = pl.BlockSpec((1, tk, tn), lambda k, n, s: (n, k, 0))
    else:
        out_shape = _sds((k_dim, n_dim), BF)
        o_spec = pl.BlockSpec((tk, tn), lambda k, n, s: (k, n))
    (out,), got = _pcall(
        body, (a, b), name=name, grid=(nk, nn, ns),
        in_specs=[a_spec, pl.BlockSpec((ts, tn), lambda k, n, s: (s, n))], out_specs=[o_spec], out_shape=[out_shape],
        scratch_shapes=[pltpu.VMEM((tk, tn), F32)], comm=comm)
    return out if comm is None else (out, got)


_WEIGHTS = ['g_mix_pre', 'w_in', 'w_sgu_s', 'b_sgu_s', 'g_sgu_v', 'b_sgu_v', 'w_sgu_out', 'w_dw', 'b_dw', 'g_conv_ln', 'b_conv_ln',
            'w_conv_out', 'w_pool', 's_pool', 'w_pool_out', 'w_out', 'g_mix_post', 'g_ffn_pre', 'w_ffn_in', 'w_ffn_out', 'g_ffn_post',
            'w_ple', 'w_ple_gate']
_SHARDED = ['w_in', 'w_sgu_out', 'w_conv_out', 'w_pool', 'w_pool_out', 'w_out', 'w_ffn_in', 'w_ffn_out', 'w_ple', 'w_ple_gate']
_VECTORS = ['g_mix_pre', 'g_sgu_v', 'b_sgu_v', 'b_dw', 'g_conv_ln', 'b_conv_ln', 's_pool', 'g_mix_post', 'g_ffn_pre', 'g_ffn_post']
_SUBLANES = 8
_SGU_ROWS = HEADS * BLK * BLK // D
_REP_ROWS = _SUBLANES * (len(_VECTORS) + 2) + _SGU_ROWS


def _pack_replicated(t, layer):
    rows = [jnp.pad(t[k][layer].reshape(1, D), ((0, _SUBLANES - 1), (0, 0))) for k in _VECTORS + ['b_sgu_s']]
    return jnp.concatenate(rows + [t['w_sgu_s'][layer].reshape(_SGU_ROWS, D), jnp.zeros((_SUBLANES, D), F32)], axis=0)


def _unpack_replicated(packed):
    out = {}
    for i, k in enumerate(_VECTORS):
        out[k] = packed[:, _SUBLANES * i, :]
    o = _SUBLANES * len(_VECTORS)
    out['b_sgu_s'] = packed[:, o, :].reshape(2, HEADS, BLK)
    out['w_sgu_s'] = packed[:, o + _SUBLANES:o + _SUBLANES + _SGU_ROWS, :].reshape(2, HEADS, BLK, BLK)
    return out


def _pad_taps(w):
    return jnp.pad(w, ((0, HALO - CONV_W), (0, 0)))


def kernel(x, p, g_mix_pre, w_in, w_sgu_s, b_sgu_s, g_sgu_v, b_sgu_v, w_sgu_out, w_dw, b_dw, g_conv_ln, b_conv_ln, w_conv_out, w_pool, s_pool, w_pool_out, w_out, g_mix_post, g_ffn_pre, w_ffn_in, w_ffn_out, g_ffn_post, w_ple, w_ple_gate, loss_target, m_g_mix_pre, m_w_in, m_w_sgu_s, m_b_sgu_s, m_g_sgu_v, m_b_sgu_v, m_w_sgu_out, m_w_dw, m_b_dw, m_g_conv_ln, m_b_conv_ln, m_w_conv_out, m_w_pool, m_s_pool, m_w_pool_out, m_w_out, m_g_mix_post, m_g_ffn_pre, m_w_ffn_in, m_w_ffn_out, m_g_ffn_post, m_w_ple, m_w_ple_gate, v_g_mix_pre, v_w_in, v_w_sgu_s, v_b_sgu_s, v_g_sgu_v, v_b_sgu_v, v_w_sgu_out, v_w_dw, v_b_dw, v_g_conv_ln, v_b_conv_ln, v_w_conv_out, v_w_pool, v_s_pool, v_w_pool_out, v_w_out, v_g_mix_post, v_g_ffn_pre, v_w_ffn_in, v_w_ffn_out, v_g_ffn_post, v_w_ple, v_w_ple_gate):
    W = dict(g_mix_pre=g_mix_pre, w_in=w_in, w_sgu_s=w_sgu_s, b_sgu_s=b_sgu_s, g_sgu_v=g_sgu_v, b_sgu_v=b_sgu_v, w_sgu_out=w_sgu_out,
             w_dw=w_dw, b_dw=b_dw, g_conv_ln=g_conv_ln, b_conv_ln=b_conv_ln, w_conv_out=w_conv_out, w_pool=w_pool, s_pool=s_pool,
             w_pool_out=w_pool_out, w_out=w_out, g_mix_post=g_mix_post, g_ffn_pre=g_ffn_pre, w_ffn_in=w_ffn_in, w_ffn_out=w_ffn_out,
             g_ffn_post=g_ffn_post, w_ple=w_ple, w_ple_gate=w_ple_gate)
    M = dict(g_mix_pre=m_g_mix_pre, w_in=m_w_in, w_sgu_s=m_w_sgu_s, b_sgu_s=m_b_sgu_s, g_sgu_v=m_g_sgu_v, b_sgu_v=m_b_sgu_v,
             w_sgu_out=m_w_sgu_out, w_dw=m_w_dw, b_dw=m_b_dw, g_conv_ln=m_g_conv_ln, b_conv_ln=m_b_conv_ln, w_conv_out=m_w_conv_out,
             w_pool=m_w_pool, s_pool=m_s_pool, w_pool_out=m_w_pool_out, w_out=m_w_out, g_mix_post=m_g_mix_post, g_ffn_pre=m_g_ffn_pre,
             w_ffn_in=m_w_ffn_in, w_ffn_out=m_w_ffn_out, g_ffn_post=m_g_ffn_post, w_ple=m_w_ple, w_ple_gate=m_w_ple_gate)
    V = dict(g_mix_pre=v_g_mix_pre, w_in=v_w_in, w_sgu_s=v_w_sgu_s, b_sgu_s=v_b_sgu_s, g_sgu_v=v_g_sgu_v, b_sgu_v=v_b_sgu_v,
             w_sgu_out=v_w_sgu_out, w_dw=v_w_dw, b_dw=v_b_dw, g_conv_ln=v_g_conv_ln, b_conv_ln=v_b_conv_ln, w_conv_out=v_w_conv_out,
             w_pool=v_w_pool, s_pool=v_s_pool, w_pool_out=v_w_pool_out, w_out=v_w_out, g_mix_post=v_g_mix_post, g_ffn_pre=v_g_ffn_pre,
             w_ffn_in=v_w_ffn_in, w_ffn_out=v_w_ffn_out, g_ffn_post=v_g_ffn_post, w_ple=v_w_ple, w_ple_gate=v_w_ple_gate)

    my_c = lax.axis_index("c")
    core_id = my_c.astype(jnp.int32).reshape(1)
    my_chip = 2 * lax.axis_index("x") + lax.axis_index("y")
    chip_id = my_chip.astype(jnp.int32).reshape(1)
    my_dev = 2 * my_chip + my_c
    s_len = x.shape[1]
    h0 = x.reshape(s_len, D)
    target = loss_target.reshape(s_len, D)

    shard = [{k: W[k][l].astype(BF) for k in _SHARDED} for l in range(2)]
    for l in range(2):
        shard[l]['w_dw'] = w_dw[l]
        shard[l]['w_ffn_in'] = jnp.swapaxes(w_ffn_in[l], 0, 1).astype(BF)
    mixer_w = ['w_sgu_out', 'w_conv_out', 'w_pool', 'w_pool_out', 'w_out', 'w_dw']
    ffn_w = ['w_ffn_in', 'w_ffn_out', 'w_ple', 'w_ple_gate']
    hosted_gather = {
        'norm_proj_in': (0, mixer_w),
        'conv_fwd': (0, ffn_w),
        'merge_out': (1, mixer_w),
        'norm_proj_ffn': (1, ['w_in']),
        'ffn_out': (1, ffn_w),
    }
    G = [{'w_in': run_comm(Gather([shard[0]['w_in']]), "gather_w_in_0")[0]}, {}]

    def gather_in(layer, call):
        if layer != 0:
            return None, (lambda got: None)
        to_layer, keys = hosted_gather[call]
        return Gather([shard[to_layer][k] for k in keys]), (lambda got: G[to_layer].update(zip(keys, got)))

    def natural_mixer(g):
        wpool = jnp.transpose(g['w_pool'], (1, 0, 2, 3)).reshape(4, POOL_GD, POOL_GD)
        wdw = jnp.transpose(g['w_dw'].reshape(N_DEV, CONV_W, BLK), (1, 0, 2)).reshape(CONV_W, D)
        return dict(wpool=wpool, wdw=_pad_taps(wdw))

    def natural_ffn(g):
        wfit = g['w_ffn_in'].reshape(4, D_FF // 2, D)
        wple = jnp.transpose(g['w_ple'], (1, 0, 2)).reshape(PLE, D)
        return dict(wfit=wfit, wple=wple)

    def vec(name, layer):
        return W[name][layer].reshape(1, D)

    saved = []
    h = h0
    for l in range(2):
        g = G[l]
        comm, land = gather_in(l, 'norm_proj_in')
        (proj, hn), got = norm_proj(h, vec('g_mix_pre', l), g['w_in'], f"norm_proj_in_{l}", comm=comm)
        land(got)
        nat = natural_mixer(g)
        bsfull = jnp.repeat(b_sgu_s[l].T, BLK, axis=1)
        wst = jnp.swapaxes(w_sgu_s[l], 1, 2)
        mixed, sgu, bra = sgu_fwd(proj, w_sgu_s[l], bsfull, vec('g_sgu_v', l), vec('b_sgu_v', l), g['w_sgu_out'], f"sgu_fwd_{l}")
        comm, land = gather_in(l, 'conv_fwd')
        (conv, cb, brb), got = conv_fwd(proj, nat['wdw'], vec('b_dw', l), vec('g_conv_ln', l), vec('b_conv_ln', l), g['w_conv_out'],
                                        f"conv_fwd_{l}", comm=comm)
        land(got)
        nat.update(natural_ffn(g))
        pooled, pm, brc = pool_fwd(proj, nat['wpool'], vec('s_pool', l), g['w_pool_out'], f"pool_fwd_{l}")
        comm, land = gather_in(l, 'merge_out')
        (merged, mo, h1), got = merge_out(proj, bra, brb, brc, h, g['w_out'], vec('g_mix_post', l), f"merge_out_{l}", comm=comm)
        land(got)
        comm, land = gather_in(l, 'norm_proj_ffn')
        (ff, hn2), got = norm_proj(h1, vec('g_ffn_pre', l), nat['wfit'], f"norm_proj_ffn_{l}", transposed=True, comm=comm)
        land(got)
        comm, land = gather_in(l, 'ffn_out')
        (act, f, h2, pg, h3), got = ffn_out(ff, h1, p, g['w_ffn_out'], vec('g_ffn_post', l), g['w_ple_gate'], nat['wple'], l,
                                            f"ffn_out_{l}", comm=comm)
        land(got)
        saved.append(dict(h=h, nat=nat, wst=wst, proj=proj, hn=hn, mixed=mixed, sgu=sgu, bra=bra, conv=conv, cb=cb, brb=brb,
                          pooled=pooled, pm=pm, brc=brc, merged=merged, mo=mo, h1=h1, ff=ff, hn2=hn2, act=act, f=f, h2=h2, pg=pg))
        h = h3

    dh = h

    parts = {k: [None, None] for k in _SHARDED}
    small = {k: [None, None] for k in _VECTORS + ['b_sgu_s', 'w_sgu_s', 'w_dw']}
    chip_parts = [{}, {}]
    from_chips = [{}, {}]
    gathered_small = [None, None]

    def to_sibling(layer, keys):
        return ToSibling([parts[k][layer] for k in keys])

    def add_siblings(layer, keys, from_sibling):
        for k, rv in zip(keys, from_sibling):
            st = parts[k][layer]
            cols = st.shape[-1]
            chip_parts[layer][k] = add_core_side(st.reshape(N_DEV, -1, cols), rv.reshape(4, -1, cols), core_id,
                                                 f"rs_add_{k}_{layer}").reshape(rv.shape)

    def to_chips(layer, keys):
        return ToChips([chip_parts[layer][k] for k in keys])

    def small_pack(layer):
        return jnp.concatenate([_pack_replicated(small, layer), small['w_dw'][layer]], axis=0)

    ffn_group = ['w_ffn_in', 'w_ffn_out', 'w_ple_gate', 'w_ple']
    mix_group = ['w_out', 'w_sgu_out', 'w_conv_out', 'w_pool_out', 'w_pool']
    big = ['w_in', 'w_ffn_in']
    others = [k for k in _SHARDED if k not in big]
    hosted_rs = {
        'ple_ffn_bwd': (lambda: to_sibling(1, _SHARDED), lambda got: add_siblings(1, _SHARDED, got)),
        'mix_post_bwd': (lambda: Both(to_sibling(0, ffn_group), Gather([small_pack(1)])),
                         lambda got: (add_siblings(0, ffn_group, got[:-1]), gathered_small.__setitem__(1, got[-1]))),
        'ffn_in_bwd': (lambda: to_chips(1, others), lambda got: from_chips[1].update(zip(others, got))),
        'sgu_bwd': (lambda: to_chips(1, big), lambda got: from_chips[1].update(zip(big, got))),
        'conv_bwd': (lambda: to_chips(0, ffn_group), lambda got: from_chips[0].update(zip(ffn_group, got))),
        'wgrad_in_sgu': (lambda: to_sibling(0, mix_group), lambda got: add_siblings(0, mix_group, got)),
        'wgrad_in_seq': (lambda: to_chips(0, mix_group), lambda got: from_chips[0].update(zip(mix_group, got))),
        'wgrad_in_gate': (lambda: Gather([small_pack(0)]), lambda got: gathered_small.__setitem__(0, got[0])),
    }

    def exchange_in(layer, call):
        if layer != 0 or call not in hosted_rs:
            return None, (lambda got: None)
        make, land = hosted_rs[call]
        return make(), land

    for l in (1, 0):
        sv, g, nat = saved[l], G[l], saved[l]['nat']

        def wg(call, a, b, tk, tn, **kw):
            comm, land = exchange_in(l, call)
            if comm is None:
                return wgrad(a, b, tk, tn, f"{call}_{l}", **kw)
            out, got = wgrad(a, b, tk, tn, f"{call}_{l}", comm=comm, **kw)
            land(got)
            return out

        comm, land = exchange_in(l, 'ple_ffn_bwd')
        res, got = ple_ffn_bwd(
            dh, sv['pg'], p, sv['f'], sv['ff'], nat['wple'], g['w_ple_gate'], g['w_ffn_out'], vec('g_ffn_post', l), l,
            f"ple_ffn_bwd_{l}", target=target if l == 1 else None, comm=comm)
        land(got)
        dh2, dpe, dpg, df, dff, small['g_ffn_post'][l] = res[:6]
        if l == 1:
            loss = lax.psum(res[6][0, 0], ("x", "y", "c"))
        comm, land = exchange_in(l, 'ffn_in_bwd')
        (dh1, small['g_ffn_pre'][l]), got = ffn_in_bwd(dff, nat['wfit'], sv['h1'], dh2, vec('g_ffn_pre', l), f"ffn_in_bwd_{l}",
                                                       comm=comm)
        land(got)
        p_spec = pl.BlockSpec((None, None, min(TS, s_len), PLE), functools.partial(lambda k, n, s, ll: (ll, 0, s, 0), ll=l))
        parts['w_ple'][l] = jnp.transpose(wg('wgrad_ple', p, dpe, PLE, D, a_spec=p_spec).reshape(PLE, N_DEV, BLK), (1, 0, 2))
        parts['w_ple_gate'][l] = wg('wgrad_ple_gate', sv['h2'], dpg, D, D).reshape(N_DEV, BLK, D)
        parts['w_ffn_out'][l] = wg('wgrad_ffn_out', sv['act'], df, D_FF // 2, D).reshape(N_DEV, D_FF // N_DEV, D)
        parts['w_ffn_in'][l] = wg('wgrad_ffn_in', dff, sv['hn2'], D_FF // 2, D).reshape(N_DEV, D_FF // 4, D)

        comm, land = exchange_in(l, 'mix_post_bwd')
        (dmo, dbra, dbrb, dbrc, dzg, dhn_g, small['g_mix_post'][l]), got = mix_post_bwd(
            dh1, sv['mo'], vec('g_mix_post', l), g['w_out'], sv['proj'], sv['bra'], sv['brb'], sv['brc'], g['w_in'],
            f"mix_post_bwd_{l}", comm=comm)
        land(got)
        comm, land = exchange_in(l, 'sgu_bwd')
        (dzs, dhn_ag, dws, dbs, small['g_sgu_v'][l], small['b_sgu_v'][l]), got = sgu_bwd(
            dbra, g['w_sgu_out'], sv['proj'], sv['mixed'], sv['wst'], vec('g_sgu_v', l), vec('b_sgu_v', l), g['w_in'],
            dhn_g, f"sgu_bwd_{l}", comm=comm)
        land(got)
        small['w_sgu_s'][l] = dws
        small['b_sgu_s'][l] = dbs.T
        comm, land = exchange_in(l, 'conv_bwd')
        (dc, dwdw, small['b_dw'][l], small['g_conv_ln'][l], small['b_conv_ln'][l]), got = conv_bwd(
            dbrb, g['w_conv_out'], sv['proj'], sv['conv'], vec('g_conv_ln', l), vec('b_conv_ln', l), f"conv_bwd_{l}", comm=comm)
        land(got)
        small['w_dw'][l] = dwdw
        dmr, q, small['s_pool'][l] = pool_bwd(dbrc, g['w_pool_out'], sv['pooled'], nat['wpool'], vec('s_pool', l), f"pool_bwd_{l}")
        comm, land = exchange_in(l, 'seq_bwd')
        (dzc, dh, small['g_mix_pre'][l]), got = seq_bwd(dc, q, sv['proj'], nat['wdw'], g['w_in'], dhn_ag, sv['h'], dh1,
                                                        vec('g_mix_pre', l), f"seq_bwd_{l}", comm=comm)
        land(got)

        parts['w_out'][l] = wg('wgrad_out', sv['merged'], dmo, D, D).reshape(N_DEV, BLK, D)
        parts['w_sgu_out'][l] = wg('wgrad_sgu_out', sv['sgu'], dbra, D, D).reshape(N_DEV, BLK, D)
        parts['w_conv_out'][l] = wg('wgrad_conv_out', sv['cb'], dbrb, D, D).reshape(N_DEV, BLK, D)
        parts['w_pool_out'][l] = wg('wgrad_pool_out', sv['pm'], dbrc, D, D).reshape(N_DEV, BLK, D)
        g_pool = wg('wgrad_pool', sv['pooled'], dmr, POOL_GD, POOL_GD, diag=True)
        parts['w_pool'][l] = jnp.transpose(g_pool.reshape(4, N_DEV, POOL_GD // N_DEV, POOL_GD), (1, 0, 2, 3))
        parts['w_in'][l] = jnp.concatenate([
            wg('wgrad_in_sgu', sv['hn'], dzs, D, D, stacked=True),
            wg('wgrad_in_seq', sv['hn'], dzc, D, D, stacked=True),
            wg('wgrad_in_gate', sv['hn'], dzg, D, D, stacked=True)], axis=0)
    grad_x = dh.reshape(1, s_len, D)

    add_siblings(0, ['w_in'], run_comm(to_sibling(0, ['w_in']), "rs_to_sibling_w_in_0"))
    from_chips[0]['w_in'] = run_comm(to_chips(0, ['w_in']), "rs_to_chips_w_in_0")[0]

    outs = {}
    for k in _SHARDED:
        wmv = [jnp.swapaxes(t[k], 1, 2) if k == 'w_ffn_in' else t[k] for t in (W, M, V)]
        cols = wmv[0].shape[-1]
        pieces = []
        for layer in range(2):
            own4 = chip_parts[layer][k].reshape(4, -1, cols)
            rv3 = from_chips[layer][k].reshape(3, -1, cols)
            pieces.append([(own4, OWN_CHIP), (rv3, 0), (rv3, 1), (rv3, 2)])
        res = adamw_layers(*[t.reshape(2, -1, cols) for t in wmv], pieces, chip_id, f"adamw_{k}")
        res = [r.reshape(wmv[0].shape) for r in res]
        outs[k] = [jnp.swapaxes(r, 1, 2) for r in res] if k == 'w_ffn_in' else res

    packed = [jnp.stack([_pack_replicated(t, 0), _pack_replicated(t, 1)], axis=0) for t in (W, M, V)]
    rep_res = adamw_layers(*packed, [[(gathered_small[layer], d) for d in range(N_DEV)] for layer in range(2)], chip_id,
                           "adamw_replicated")
    for idx, res in enumerate(rep_res):
        for name, val in _unpack_replicated(res).items():
            outs.setdefault(name, [None] * 4)[idx] = val
    dw_sum = jnp.stack([sum_slabs(gathered_small[layer][:, _REP_ROWS:, :], f"sum_w_dw_{layer}") for layer in range(2)], axis=0)
    dw_mine = lax.dynamic_slice_in_dim(dw_sum[:, :CONV_W], my_dev * BLK, BLK, axis=2)
    res = adamw(w_dw.reshape(2 * CONV_W, BLK), m_w_dw.reshape(2 * CONV_W, BLK), v_w_dw.reshape(2 * CONV_W, BLK),
                [(dw_mine.reshape(2 * CONV_W, BLK), None)], "adamw_w_dw")
    outs['w_dw'] = [r.reshape(w_dw.shape) for r in res]

    result = [loss, grad_x]
    for idx in range(4):
        result += [outs[k][idx] for k in _WEIGHTS]
    return tuple(result)
```

```python
import functools
import math

import jax
import jax.numpy as jnp
from jax import lax
from jax.experimental import pallas as pl
from jax.experimental.pallas import tpu as pltpu

F32 = jnp.float32
BF = jnp.bfloat16

D = 1024
D_FF = 2816
PLE = 256
N_DEV = 8
HEADS = 8
BLK = 128
CHUNK = 64
CONV_W = 31
POOL_WINDOWS = (2, 4, 8, 16)
POOL_GD = 256
EPS = 1e-6

V7X_VMEM_BYTES = 64 * 2**20
VMEM_LIMIT = V7X_VMEM_BYTES * 7 // 8
HALO = 32
RC = 64
LC = 128
TM = 1024
TMB = 512
TMW = 256
TMP = 1024
TS = 2048

ADAM_LR, ADAM_B1, ADAM_B2, ADAM_EPS, ADAM_WD, ADAM_STEP = 0.001, 0.9, 0.999, 1e-08, 0.01, 10

MESH = pl.DeviceIdType.MESH
ANY = pl.BlockSpec(memory_space=pl.ANY)

_GELU_K0 = math.sqrt(2.0 / math.pi)
_GELU_K1 = 0.044715
_LOG2E = 1.4426950408889634


def _dot(a, b):
    return jnp.dot(a, b, preferred_element_type=F32)


def _dot_nt(a, b):
    return lax.dot_general(a, b, (((1,), (1,)), ((), ())), preferred_element_type=F32)


def _dot_tn(a, b):
    return lax.dot_general(a, b, (((0,), (0,)), ((), ())), preferred_element_type=F32)


def _sig(x):
    return 1.0 / (1.0 + jnp.exp2(x * (-_LOG2E)))


def _gelu(x):
    s = 1.0 / (1.0 + jnp.exp2(x * ((-2.0 * _GELU_K0 * _LOG2E) + (-2.0 * _GELU_K0 * _GELU_K1 * _LOG2E) * (x * x))))
    return x * s, s


def _gelu_grad(x, s):
    return s + x * s * (1.0 - s) * ((2.0 * _GELU_K0) + (6.0 * _GELU_K0 * _GELU_K1) * (x * x))


def _rstd(x):
    return lax.rsqrt(jnp.mean(x * x, axis=-1, keepdims=True) + EPS)


def _rms_bwd(x, gd, r):
    return r * gd - x * (r * r * r) * jnp.mean(gd * x, axis=-1, keepdims=True)


def _ln_fwd(x):
    mu = jnp.mean(x, axis=-1, keepdims=True)
    xc = x - mu
    rs = lax.rsqrt(jnp.mean(xc * xc, axis=-1, keepdims=True) + EPS)
    return xc * rs, rs


def _ln_bwd(dhat, hat, rs):
    return rs * (dhat - jnp.mean(dhat, axis=-1, keepdims=True) - hat * jnp.mean(dhat * hat, axis=-1, keepdims=True))


def _colsum(x):
    return jnp.sum(x, axis=0, keepdims=True)


def _accum(ref, first, val):
    @pl.when(first)
    def _():
        ref[...] = val

    @pl.when(jnp.logical_not(first))
    def _():
        ref[...] += val


def _sgu_mask(transposed):
    r = lax.broadcasted_iota(jnp.int32, (BLK, BLK), 0) // CHUNK
    c = lax.broadcasted_iota(jnp.int32, (BLK, BLK), 1) // CHUNK
    return (r <= c) if transposed else (c <= r)


def _inv_count(i, tm, w):
    t = lax.broadcasted_iota(jnp.int32, (tm, 1), 0) + i * tm
    return 1.0 / jnp.minimum(t + 1, w).astype(F32)


def _count(i, tm, w):
    t = lax.broadcasted_iota(jnp.int32, (tm, 1), 0) + i * tm
    return jnp.minimum(t + 1, w).astype(F32)


def _params(n_grid):
    return pltpu.CompilerParams(dimension_semantics=("arbitrary",) * n_grid, vmem_limit_bytes=VMEM_LIMIT)


def _sds(shape, dtype):
    return jax.ShapeDtypeStruct(shape, dtype)


def _cols(tm, width, cb):
    return pl.BlockSpec((tm, width), lambda i: (i, cb))


def _whole(shape):
    nd = len(shape)
    return pl.BlockSpec(shape, lambda i: (0,) * nd)


def _prev_halo(tm, width, cb):
    return pl.BlockSpec((HALO, width), lambda i: (jnp.maximum(i * (tm // HALO) - 1, 0), cb))


def _next_halo(tm, width, cb, s_len):
    last = s_len // HALO - 1
    return pl.BlockSpec((HALO, width), lambda i: (jnp.minimum((i + 1) * (tm // HALO), last), cb))


def _rowsharded(rows):
    return pl.BlockSpec((N_DEV, rows, D), lambda i: (0, 0, 0))


def _win_block(j):
    return pl.BlockSpec((None, D, D), lambda i: (j, 0, 0))


def _row_tile(rows, cap):
    t = min(rows, cap)
    while rows % t or t % 16:
        t -= 16
    return t


class Gather:
    def __init__(self, arrs):
        self.arrs = list(arrs)
        n = self.n = len(self.arrs)
        self.out_shape = [_sds((N_DEV,) + a.shape, a.dtype) for a in self.arrs]
        self.scratch = [pltpu.SemaphoreType.DMA((n, 7)), pltpu.SemaphoreType.DMA((n, 7)), pltpu.SemaphoreType.DMA((n,))]

    def _plan(self, ins, outs, sems):
        send, recv, local = sems
        x, y, c = lax.axis_index("x"), lax.axis_index("y"), lax.axis_index("c")
        me, sibling = (x, y, c), (x, y, 1 - c)
        chips = [(1 - x, y), (x, 1 - y), (1 - x, 1 - y)]

        def copy(a, k, block, to, src=None):
            dst = outs[a].at[4 * block[0] + 2 * block[1] + block[2]]
            return pltpu.make_async_remote_copy(
                src_ref=dst if src is None else src, dst_ref=dst, send_sem=send.at[a, k], recv_sem=recv.at[a, k],
                device_id=to, device_id_type=MESH)

        mine = [pltpu.make_async_copy(ins[a], outs[a].at[4 * x + 2 * y + c], local.at[a]) for a in range(self.n)]
        first = []
        for a in range(self.n):
            first.append(copy(a, 0, me, sibling, src=ins[a]))
            first += [copy(a, 1 + j, me, (*chip, c), src=ins[a]) for j, chip in enumerate(chips)]
        return me, sibling, chips, c, copy, mine, first

    def start(self, ins, outs, sems):
        *_, mine, first = self._plan(ins, outs, sems)
        for cp in mine + first:
            cp.start()

    def finish(self, ins, outs, sems):
        me, sibling, chips, c, copy, mine, first = self._plan(ins, outs, sems)
        passed = []
        for a in range(self.n):
            for j, chip in enumerate(chips):
                copy(a, 1 + j, (*chip, c), me).wait_recv()
                fwd = copy(a, 4 + j, (*chip, c), sibling)
                fwd.start()
                passed.append(fwd)
        for a in range(self.n):
            copy(a, 0, sibling, me).wait_recv()
            for j, chip in enumerate(chips):
                copy(a, 4 + j, (*chip, 1 - c), me).wait_recv()
        for cp in first + passed:
            cp.wait_send()
        for cp in mine:
            cp.wait()


class ToSibling:
    def __init__(self, parts):
        self.arrs = list(parts)
        n = self.n = len(self.arrs)
        self.out_shape = [_sds((4,) + p.shape[1:], p.dtype) for p in self.arrs]
        self.scratch = [pltpu.SemaphoreType.DMA((n,)), pltpu.SemaphoreType.DMA((n,))]

    def start(self, ins, outs, sems):
        send, recv = sems
        x, y, c = lax.axis_index("x"), lax.axis_index("y"), lax.axis_index("c")
        for a in range(self.n):
            for q in range(4):
                pltpu.make_async_remote_copy(
                    src_ref=ins[a].at[2 * q + 1 - c], dst_ref=outs[a].at[q], send_sem=send.at[a], recv_sem=recv.at[a],
                    device_id=(x, y, 1 - c), device_id_type=MESH).start()

    def finish(self, ins, outs, sems):
        send, recv = sems
        x, y, c = lax.axis_index("x"), lax.axis_index("y"), lax.axis_index("c")
        for a in range(self.n):
            pltpu.make_async_remote_copy(
                src_ref=outs[a], dst_ref=outs[a], send_sem=send.at[a], recv_sem=recv.at[a],
                device_id=(x, y, 1 - c), device_id_type=MESH).wait()


class ToChips:
    def __init__(self, cps):
        self.arrs = list(cps)
        n = self.n = len(self.arrs)
        self.out_shape = [_sds((3,) + p.shape[1:], p.dtype) for p in self.arrs]
        self.scratch = [pltpu.SemaphoreType.DMA((n,)), pltpu.SemaphoreType.DMA((n,))]

    def start(self, ins, outs, sems):
        send, recv = sems
        x, y, c = lax.axis_index("x"), lax.axis_index("y"), lax.axis_index("c")
        for a in range(self.n):
            for r, (px, py) in enumerate([(1 - x, y), (x, 1 - y), (1 - x, 1 - y)]):
                pltpu.make_async_remote_copy(
                    src_ref=ins[a].at[2 * px + py], dst_ref=outs[a].at[r], send_sem=send.at[a], recv_sem=recv.at[a],
                    device_id=(px, py, c), device_id_type=MESH).start()

    def finish(self, ins, outs, sems):
        send, recv = sems
        x, y, c = lax.axis_index("x"), lax.axis_index("y"), lax.axis_index("c")
        for a in range(self.n):
            pltpu.make_async_remote_copy(
                src_ref=outs[a], dst_ref=outs[a], send_sem=send.at[a], recv_sem=recv.at[a],
                device_id=(x, y, c), device_id_type=MESH).wait()


class Both:
    def __init__(self, a, b):
        self.a, self.b = a, b
        self.arrs, self.n = a.arrs + b.arrs, a.n + b.n
        self.out_shape, self.scratch = a.out_shape + b.out_shape, a.scratch + b.scratch

    def _each(self, ins, outs, sems):
        na, ns = self.a.n, len(self.a.scratch)
        return (self.a, (ins[:na], outs[:na], sems[:ns])), (self.b, (ins[na:], outs[na:], sems[ns:]))

    def start(self, ins, outs, sems):
        for comm, refs in self._each(ins, outs, sems):
            comm.start(*refs)

    def finish(self, ins, outs, sems):
        for comm, refs in self._each(ins, outs, sems):
            comm.finish(*refs)


def run_comm(comm, name):
    n = comm.n

    def body(*refs):
        ins, outs, sems = refs[:n], refs[n:2 * n], refs[2 * n:]
        comm.start(ins, outs, sems)
        comm.finish(ins, outs, sems)

    return pl.pallas_call(body, name=name, out_shape=comm.out_shape, in_specs=[ANY] * n, out_specs=[ANY] * n,
                          scratch_shapes=comm.scratch)(*comm.arrs)


def _pcall(body, args, *, name, grid, in_specs, out_specs, out_shape, scratch_shapes=(), comm=None):
    params = _params(len(grid))
    scratch_shapes = list(scratch_shapes)
    if comm is None:
        outs = pl.pallas_call(body, name=name, grid=grid, in_specs=in_specs, out_specs=out_specs, out_shape=out_shape,
                              scratch_shapes=scratch_shapes, compiler_params=params)(*args)
        return outs, None
    n_in, n_out, n_scr, nc = len(in_specs), len(out_specs), len(scratch_shapes), comm.n

    def hosted(*refs):
        ins, cins = refs[:n_in], refs[n_in:n_in + nc]
        o0 = n_in + nc
        outs, couts = refs[o0:o0 + n_out], refs[o0 + n_out:o0 + n_out + nc]
        s0 = o0 + n_out + nc
        scr, csems = refs[s0:s0 + n_scr], refs[s0 + n_scr:]
        first = pl.program_id(0) == 0
        last = pl.program_id(0) == grid[0] - 1
        for ax in range(1, len(grid)):
            first = jnp.logical_and(first, pl.program_id(ax) == 0)
            last = jnp.logical_and(last, pl.program_id(ax) == grid[ax] - 1)

        @pl.when(first)
        def _():
            comm.start(cins, couts, csems)
        body(*ins, *outs, *scr)

        @pl.when(last)
        def _():
            comm.finish(cins, couts, csems)

    res = pl.pallas_call(
        hosted, name=name, grid=grid, in_specs=list(in_specs) + [ANY] * nc, out_specs=list(out_specs) + [ANY] * nc,
        out_shape=list(out_shape) + comm.out_shape, scratch_shapes=scratch_shapes + comm.scratch,
        compiler_params=params)(*args, *comm.arrs)
    return res[:n_out], res[n_out:]


def add_core_side(parts, from_sibling, core, name):
    _, rows, cols = parts.shape
    tr = _row_tile(rows, 512)

    def body(c_ref, a_ref, b_ref, o_ref):
        o_ref[...] = (a_ref[...].astype(F32) + b_ref[...].astype(F32)).astype(o_ref.dtype)

    side = pl.BlockSpec((None, tr, cols), lambda q, i, c: (q, i, 0))
    return pl.pallas_call(
        body, name=name,
        grid_spec=pltpu.PrefetchScalarGridSpec(
            num_scalar_prefetch=1, grid=(4, rows // tr),
            in_specs=[pl.BlockSpec((None, tr, cols), lambda q, i, c: (2 * q + c[0], i, 0)), side], out_specs=side),
        out_shape=_sds(from_sibling.shape, BF), compiler_params=_params(2))(core, parts, from_sibling)


def adamw(w, m, v, pieces, name):
    rows, cols = w.shape
    tr = _row_tile(rows, 256) if rows % 16 == 0 else rows
    np_ = len(pieces)
    c1 = 1.0 / (1.0 - ADAM_B1 ** ADAM_STEP)
    c2 = 1.0 / (1.0 - ADAM_B2 ** ADAM_STEP)

    def body(*refs):
        w_ref, m_ref, v_ref = refs[:3]
        p_refs = refs[3:3 + np_]
        g_ref, d_ref, nm_ref, nv_ref = refs[3 + np_:]
        g = p_refs[0][...].astype(F32)
        for pr in p_refs[1:]:
            g = g + pr[...].astype(F32)
        nm = ADAM_B1 * m_ref[...] + (1.0 - ADAM_B1) * g
        nv = ADAM_B2 * v_ref[...] + (1.0 - ADAM_B2) * (g * g)
        g_ref[...] = g
        nm_ref[...] = nm
        nv_ref[...] = nv
        d_ref[...] = -ADAM_LR * ((nm * c1) / (jnp.sqrt(nv * c2) + ADAM_EPS) + ADAM_WD * w_ref[...])

    spec = pl.BlockSpec((tr, cols), lambda i: (i, 0))
    p_specs = []
    for arr, k in pieces:
        if k is None:
            p_specs.append(spec)
        else:
            p_specs.append(pl.BlockSpec((None, tr, cols), functools.partial(lambda i, kk: (kk, i, 0), kk=k)))
    out = _sds(w.shape, F32)
    return pl.pallas_call(body, name=name, grid=(rows // tr,), in_specs=[spec] * 3 + p_specs, out_specs=[spec] * 4,
                          out_shape=[out] * 4, compiler_params=_params(1))(w, m, v, *[a for a, _ in pieces])


OWN_CHIP = "own chip"


def adamw_layers(w, m, v, pieces, chip, name):
    _, rows, cols = w.shape
    tr = _row_tile(rows, 256)
    nt = rows // tr
    counts = [len(pieces[0]), len(pieces[1])]
    c1 = 1.0 / (1.0 - ADAM_B1 ** ADAM_STEP)
    c2 = 1.0 / (1.0 - ADAM_B2 ** ADAM_STEP)

    def body(chip_ref, *refs):
        w_ref, m_ref, v_ref = refs[:3]
        p_refs = refs[3:3 + sum(counts)]
        g_ref, d_ref, nm_ref, nv_ref = refs[3 + sum(counts):]
        sums = []
        for group in (p_refs[:counts[0]], p_refs[counts[0]:]):
            s = group[0][...].astype(F32)
            for pr in group[1:]:
                s = s + pr[...].astype(F32)
            sums.append(s)
        g = jnp.where(pl.program_id(0) == 0, sums[0], sums[1])
        nm = ADAM_B1 * m_ref[...] + (1.0 - ADAM_B1) * g
        nv = ADAM_B2 * v_ref[...] + (1.0 - ADAM_B2) * (g * g)
        g_ref[...] = g
        nm_ref[...] = nm
        nv_ref[...] = nv
        d_ref[...] = -ADAM_LR * ((nm * c1) / (jnp.sqrt(nv * c2) + ADAM_EPS) + ADAM_WD * w_ref[...])

    def rows_of(layer):
        parked = nt - 1 if layer == 0 else 0
        return lambda l, i: jnp.where(l == layer, i, parked)

    spec = pl.BlockSpec((None, tr, cols), lambda l, i, c: (l, i, 0))
    p_specs, p_args = [], []
    for layer in (0, 1):
        row_of = rows_of(layer)
        for arr, k in pieces[layer]:
            p_args.append(arr)
            if k is None:
                p_specs.append(pl.BlockSpec((tr, cols), functools.partial(lambda l, i, c, f: (f(l, i), 0), f=row_of)))
            elif k == OWN_CHIP:
                p_specs.append(pl.BlockSpec((None, tr, cols), functools.partial(lambda l, i, c, f: (c[0], f(l, i), 0), f=row_of)))
            else:
                p_specs.append(pl.BlockSpec((None, tr, cols), functools.partial(lambda l, i, c, f, kk: (kk, f(l, i), 0), f=row_of, kk=k)))
    out = _sds(w.shape, F32)
    return pl.pallas_call(
        body, name=name,
        grid_spec=pltpu.PrefetchScalarGridSpec(num_scalar_prefetch=1, grid=(2, nt), in_specs=[spec] * 3 + p_specs,
                                               out_specs=[spec] * 4),
        out_shape=[out] * 4, compiler_params=_params(2))(chip, w, m, v, *p_args)


def sum_slabs(g, name):
    n, rows, cols = g.shape

    def body(g_ref, o_ref):
        s = g_ref[0]
        for k in range(1, n):
            s = s + g_ref[k]
        o_ref[...] = s

    return pl.pallas_call(body, name=name, out_shape=_sds((rows, cols), F32))(g)


def norm_proj(h, g, w, name, transposed=False, comm=None):
    s_len = h.shape[0]
    nb, tn = (w.shape[0], w.shape[1]) if transposed else (w.shape[0], w.shape[2])
    tm = min(TMP, s_len)
    nt = s_len // tm
    matmul = _dot_nt if transposed else _dot

    def body(h_ref, g_ref, w_ref, o_ref, hn_ref, hn_s):
        rows = pl.ds(pl.multiple_of(pl.program_id(1) * tm, tm), tm)

        @pl.when(pl.program_id(0) == 0)
        def _():
            x = h_ref[...]
            hn = (x * _rstd(x) * g_ref[...]).astype(BF)
            hn_s[rows, :] = hn
            hn_ref[...] = hn
        o_ref[...] = matmul(hn_s[rows, :], w_ref[...]).astype(BF)

    def first_pass_rows(j, i):
        return (jnp.where(j == 0, i, nt - 1), 0)

    return _pcall(
        body, (h, g, w), name=name, grid=(nb, nt),
        in_specs=[pl.BlockSpec((tm, D), first_pass_rows), pl.BlockSpec((1, D), lambda j, i: (0, 0)),
                  pl.BlockSpec((None,) + w.shape[1:], lambda j, i: (j, 0, 0))],
        out_specs=[pl.BlockSpec((tm, tn), lambda j, i: (i, j)), pl.BlockSpec((tm, D), first_pass_rows)],
        out_shape=[_sds((s_len, nb * tn), BF), _sds((s_len, D), BF)],
        scratch_shapes=[pltpu.VMEM((s_len, D), BF)], comm=comm)


def _sgu_mix(ws_ref, vln_s, mix_s, bs_ref, tm, transposed):
    mask = _sgu_mask(transposed)
    for hd in range(HEADS):
        wm = jnp.where(mask, ws_ref[hd], 0.0).astype(BF)
        cs = slice(hd * BLK, (hd + 1) * BLK)
        for n in range(tm // BLK):
            rs = slice(n * BLK, (n + 1) * BLK)
            r = _dot(wm, vln_s[rs, cs])
            mix_s[rs, cs] = r if bs_ref is None else r + bs_ref[:, cs]


def sgu_fwd(proj, ws, bsfull, gv, bv, wo, name):
    s_len = proj.shape[0]
    tm = min(TM, s_len)

    def body(zu_ref, zv_ref, ws_ref, bs_ref, gv_ref, bv_ref, wo_ref, mix_ref, sgu_ref, br_ref, vln_s, mix_s):
        u, _ = _gelu(zu_ref[...].astype(F32))
        v, _ = _gelu(zv_ref[...].astype(F32))
        vhat, _ = _ln_fwd(v)
        vln_s[...] = (vhat * gv_ref[...] + bv_ref[...]).astype(BF)
        _sgu_mix(ws_ref, vln_s, mix_s, bs_ref, tm, False)
        mixed = mix_s[...].astype(BF)
        mix_ref[...] = mixed
        sgu = (u * mixed.astype(F32)).astype(BF)
        sgu_ref[...] = sgu
        br_ref[...] = _dot(sgu, wo_ref[...].reshape(D, D)).astype(BF)

    return pl.pallas_call(
        body, name=name, grid=(s_len // tm,),
        in_specs=[_cols(tm, D, 0), _cols(tm, D, 1), _whole((HEADS, BLK, BLK)), _whole((BLK, D)), _whole((1, D)), _whole((1, D)),
                  _rowsharded(BLK)],
        out_specs=[_cols(tm, D, 0)] * 3, out_shape=[_sds((s_len, D), BF)] * 3,
        scratch_shapes=[pltpu.VMEM((tm, D), BF), pltpu.VMEM((tm, D), F32)], compiler_params=_params(1),
    )(proj, proj, ws, bsfull, gv, bv, wo)


def _causal_conv(ext_s, out_s, wdw_ref, bias_ref, tm):
    def chunk(ci, carry):
        r0 = pl.multiple_of((ci // (D // LC)) * RC, RC)
        l0 = pl.multiple_of((ci % (D // LC)) * LC, LC)
        win = ext_s[pl.ds(r0, RC + HALO), pl.ds(l0, LC)]
        acc = jnp.broadcast_to(bias_ref[:, pl.ds(l0, LC)], (RC, LC))
        for r in range(8):
            wr = win if r == 0 else pltpu.roll(win, r, 0)
            for m in range(4):
                d = 8 * m + r
                if d < CONV_W:
                    k = CONV_W - 1 - d
                    acc = acc + wdw_ref[k:k + 1, pl.ds(l0, LC)] * wr[HALO - 8 * m:HALO - 8 * m + RC]
        out_s[pl.ds(r0, RC), pl.ds(l0, LC)] = acc
        return carry
    lax.fori_loop(0, (tm // RC) * (D // LC), chunk, 0)


def _glu_ext(a_ref, g_ref, ah_ref, gh_ref, ext_s, first):
    hh = ah_ref[...].astype(F32) * _sig(gh_ref[...].astype(F32))
    ext_s[0:HALO, :] = jnp.where(first, 0.0, hh)
    ext_s[HALO:, :] = a_ref[...].astype(F32) * _sig(g_ref[...].astype(F32))


def conv_fwd(proj, wdw, bdw, gln, bln, wo, name, comm=None):
    s_len = proj.shape[0]
    tm = min(TM, s_len)

    def body(a_ref, g_ref, ah_ref, gh_ref, wdw_ref, bdw_ref, gln_ref, bln_ref, wo_ref, cv_ref, cb_ref, br_ref, ext_s, conv_s):
        _glu_ext(a_ref, g_ref, ah_ref, gh_ref, ext_s, pl.program_id(0) == 0)
        _causal_conv(ext_s, conv_s, wdw_ref, bdw_ref, tm)
        cv = conv_s[...].astype(BF)
        cv_ref[...] = cv
        chat, _ = _ln_fwd(cv.astype(F32))
        yl = chat * gln_ref[...] + bln_ref[...]
        cb = (yl * _sig(yl)).astype(BF)
        cb_ref[...] = cb
        br_ref[...] = _dot(cb, wo_ref[...].reshape(D, D)).astype(BF)

    return _pcall(
        body, (proj, proj, proj, proj, wdw, bdw, gln, bln, wo), name=name, grid=(s_len // tm,),
        in_specs=[_cols(tm, D, 2), _cols(tm, D, 3), _prev_halo(tm, D, 2), _prev_halo(tm, D, 3), _whole((HALO, D)),
                  _whole((1, D)), _whole((1, D)), _whole((1, D)), _rowsharded(BLK)],
        out_specs=[_cols(tm, D, 0)] * 3, out_shape=[_sds((s_len, D), BF)] * 3,
        scratch_shapes=[pltpu.VMEM((tm + HALO, D), F32), pltpu.VMEM((tm, D), F32)], comm=comm)


def pool_fwd(proj, wpool, spool, wo, name):
    s_len = proj.shape[0]
    tm = min(TM, s_len)

    def body(z_ref, zh_ref, wp_ref, sp_ref, wo_ref, pooled_ref, pm_ref, br_ref, ext_s, mr_s):
        i = pl.program_id(0)
        ext_s[0:HALO, :] = jnp.where(i == 0, 0.0, zh_ref[...].astype(F32))
        ext_s[HALO:, :] = z_ref[...].astype(F32)
        for gi, w in enumerate(POOL_WINDOWS):
            cs = slice(gi * POOL_GD, (gi + 1) * POOL_GD)
            e = ext_s[:, cs]
            s = e
            sh = 1
            while sh < w:
                s = s + pltpu.roll(s, sh, 0)
                sh *= 2
            pooled = (s[HALO:] * _inv_count(i, tm, w) - e[HALO:]).astype(BF)
            pooled_ref[:, cs] = pooled
            mr_s[:, cs] = _dot(pooled, wp_ref[gi])
        pm = (mr_s[...] * sp_ref[...]).astype(BF)
        pm_ref[...] = pm
        br_ref[...] = _dot(pm, wo_ref[...].reshape(D, D)).astype(BF)

    return pl.pallas_call(
        body, name=name, grid=(s_len // tm,),
        in_specs=[_cols(tm, D, 4), _prev_halo(tm, D, 4), _whole((4, POOL_GD, POOL_GD)), _whole((1, D)), _rowsharded(BLK)],
        out_specs=[_cols(tm, D, 0)] * 3, out_shape=[_sds((s_len, D), BF)] * 3,
        scratch_shapes=[pltpu.VMEM((tm + HALO, D), F32), pltpu.VMEM((tm, D), F32)], compiler_params=_params(1),
    )(proj, proj, wpool, spool, wo)


def merge_out(proj, bra, brb, brc, h, wout, gpost, name, comm=None):
    s_len = h.shape[0]
    tm = min(TM, s_len)

    def body(z0, z1, z2, a_ref, b_ref, c_ref, h_ref, wo_ref, g_ref, mg_ref, mo_ref, h1_ref):
        merged = (_sig(z0[...].astype(F32)) * a_ref[...].astype(F32) + _sig(z1[...].astype(F32)) * b_ref[...].astype(F32)
                  + _sig(z2[...].astype(F32)) * c_ref[...].astype(F32)).astype(BF)
        mg_ref[...] = merged
        mo = _dot(merged, wo_ref[...].reshape(D, D))
        mo_ref[...] = mo.astype(BF)
        h1_ref[...] = h_ref[...] + mo * _rstd(mo) * g_ref[...]

    row = _cols(tm, D, 0)
    return _pcall(
        body, (proj, proj, proj, bra, brb, brc, h, wout, gpost), name=name, grid=(s_len // tm,),
        in_specs=[_cols(tm, D, 5), _cols(tm, D, 6), _cols(tm, D, 7), row, row, row, row, _rowsharded(BLK), _whole((1, D))],
        out_specs=[row] * 3, out_shape=[_sds((s_len, D), BF), _sds((s_len, D), BF), _sds((s_len, D), F32)], comm=comm)


def _p_spec(tm, layer):
    return pl.BlockSpec((None, None, tm, PLE), lambda i: (layer, 0, i, 0))


def ffn_out(ff, h1, p, wfo, gpost, wpg, wple, layer, name, comm=None):
    s_len = h1.shape[0]
    tm = min(TMB, s_len)

    def body(fg_ref, fu_ref, h1_ref, p_ref, wfo_ref, g_ref, wpg_ref, wple_ref, act_ref, f_ref, h2_ref, pg_ref, h3_ref):
        gt = fg_ref[...].astype(F32)
        act = (gt * _sig(gt) * fu_ref[...].astype(F32)).astype(BF)
        act_ref[...] = act
        f = _dot(act, wfo_ref[...].reshape(D_FF, D))
        f_ref[...] = f.astype(BF)
        h2 = h1_ref[...] + f * _rstd(f) * g_ref[...]
        h2_ref[...] = h2
        pg = _dot(h2.astype(BF), wpg_ref[...].reshape(D, D)).astype(BF)
        pg_ref[...] = pg
        pe = _dot(p_ref[...].astype(BF), wple_ref[...])
        h3_ref[...] = h2 + _sig(pg.astype(F32)) * pe

    row = _cols(tm, D, 0)
    return _pcall(
        body, (ff, ff, h1, p, wfo, gpost, wpg, wple), name=name, grid=(s_len // tm,),
        in_specs=[_cols(tm, D_FF, 0), _cols(tm, D_FF, 1), row, _p_spec(tm, layer), _rowsharded(D_FF // N_DEV),
                  _whole((1, D)), _rowsharded(BLK), _whole((PLE, D))],
        out_specs=[_cols(tm, D_FF, 0), row, row, row, row],
        out_shape=[_sds((s_len, D_FF), BF), _sds((s_len, D), BF), _sds((s_len, D), F32), _sds((s_len, D), BF), _sds((s_len, D), F32)],
        comm=comm)


def ple_ffn_bwd(dh3, pg, p, f, ff, wple, wpg, wfo, gpost, layer, name, target=None, comm=None):
    s_len = dh3.shape[0]
    tm = min(TMW, s_len)
    nt = s_len // tm
    with_loss = target is not None

    def body(*refs):
        if with_loss:
            t_ref, refs, loss_ref, loss_acc = refs[0], refs[1:-2], refs[-2], refs[-1]
        (dh3_ref, pg_ref, p_ref, f_ref, fg_ref, fu_ref, wple_ref, wpg_ref, wfo_ref, g_ref,
         dh2_ref, dpe_ref, dpg_ref, df_ref, dff_ref, dg_ref) = refs
        i = pl.program_id(0)
        dh3v = dh3_ref[...]
        if with_loss:
            err = dh3v - t_ref[...]
            dh3v = err * (1.0 / D)
            _accum(loss_acc, i == 0, _colsum(err * err))

            @pl.when(i == nt - 1)
            def _():
                loss_ref[...] = jnp.broadcast_to(jnp.sum(loss_acc[...], axis=1, keepdims=True) * (0.5 / D), (1, LC))
        s = _sig(pg_ref[...].astype(F32))
        pe = _dot(p_ref[...].astype(BF), wple_ref[...])
        dpe_ref[...] = (dh3v * s).astype(BF)
        dpg = (dh3v * pe * s * (1.0 - s)).astype(BF)
        dpg_ref[...] = dpg
        dh2 = dh3v + _dot_nt(dpg, wpg_ref[...].reshape(D, D))
        dh2_ref[...] = dh2
        fv = f_ref[...].astype(F32)
        r = _rstd(fv)
        _accum(dg_ref, i == 0, _colsum(dh2 * fv * r))
        df = _rms_bwd(fv, dh2 * g_ref[...], r).astype(BF)
        df_ref[...] = df
        dact = _dot_nt(df, wfo_ref[...].reshape(D_FF, D))
        gt = fg_ref[...].astype(F32)
        sg = _sig(gt)
        up = fu_ref[...].astype(F32)
        dff_ref[:, 0:D_FF] = (dact * up * sg * (1.0 + gt * (1.0 - sg))).astype(BF)
        dff_ref[:, D_FF:2 * D_FF] = (dact * gt * sg).astype(BF)

    row = _cols(tm, D, 0)
    args = (dh3, pg, p, f, ff, ff, wple, wpg, wfo, gpost)
    in_specs = [row, row, _p_spec(tm, layer), row, _cols(tm, D_FF, 0), _cols(tm, D_FF, 1), _whole((PLE, D)),
                _rowsharded(BLK), _rowsharded(D_FF // N_DEV), _whole((1, D))]
    out_specs = [row, row, row, row, _cols(tm, 2 * D_FF, 0), _whole((1, D))]
    out_shape = [_sds((s_len, D), F32), _sds((s_len, D), BF), _sds((s_len, D), BF), _sds((s_len, D), BF),
                 _sds((s_len, 2 * D_FF), BF), _sds((1, D), F32)]
    scratch = []
    if with_loss:
        args, in_specs = (target,) + args, [row] + in_specs
        out_specs, out_shape = out_specs + [_whole((1, LC))], out_shape + [_sds((1, LC), F32)]
        scratch = [pltpu.VMEM((1, D), F32)]
    return _pcall(body, args, name=name, grid=(nt,), in_specs=in_specs, out_specs=out_specs, out_shape=out_shape,
                  scratch_shapes=scratch, comm=comm)


def ffn_in_bwd(dff, wt, h1, dh2, gpre, name, comm=None):
    s_len = h1.shape[0]
    tm = min(TMW, s_len)
    nb, tn, _ = wt.shape

    def body(dff_ref, w_ref, h1_ref, dh2_ref, g_ref, dh1_ref, dg_ref):
        dhn = _dot(dff_ref[:, 0:tn], w_ref[0])
        for j in range(1, nb):
            dhn = dhn + _dot(dff_ref[:, j * tn:(j + 1) * tn], w_ref[j])
        x = h1_ref[...]
        r = _rstd(x)
        _accum(dg_ref, pl.program_id(0) == 0, _colsum(dhn * x * r))
        dh1_ref[...] = dh2_ref[...] + _rms_bwd(x, dhn * g_ref[...], r)

    row = _cols(tm, D, 0)
    return _pcall(
        body, (dff, wt, h1, dh2, gpre), name=name, grid=(s_len // tm,),
        in_specs=[_cols(tm, nb * tn, 0), pl.BlockSpec((nb, tn, D), lambda i: (0, 0, 0), pipeline_mode=pl.Buffered(1)), row, row,
                  _whole((1, D))],
        out_specs=[row, _whole((1, D))],
        out_shape=[_sds((s_len, D), F32), _sds((1, D), F32)], comm=comm)


def mix_post_bwd(dh1, mo, gpost, wout, proj, bra, brb, brc, win, name, comm=None):
    s_len = dh1.shape[0]
    tm = min(TMB, s_len)

    def body(dh1_ref, mo_ref, g_ref, wo_ref, z0, z1, z2, a_ref, b_ref, c_ref, w5, w6, w7,
             dmo_ref, da_ref, db_ref, dc_ref, dz_ref, dhn_ref, dg_ref):
        i = pl.program_id(0)
        dh1v = dh1_ref[...]
        mo_v = mo_ref[...].astype(F32)
        r = _rstd(mo_v)
        _accum(dg_ref, i == 0, _colsum(dh1v * mo_v * r))
        dmo = _rms_bwd(mo_v, dh1v * g_ref[...], r).astype(BF)
        dmo_ref[...] = dmo
        dmerged = _dot_nt(dmo, wo_ref[...].reshape(D, D))
        dhn = jnp.zeros((tm, D), F32)
        for k, (z, br, dbr, w) in enumerate(((z0, a_ref, da_ref, w5), (z1, b_ref, db_ref, w6), (z2, c_ref, dc_ref, w7))):
            s = _sig(z[...].astype(F32))
            dbr[...] = (dmerged * s).astype(BF)
            dz = (dmerged * br[...].astype(F32) * s * (1.0 - s)).astype(BF)
            dz_ref[:, k * D:(k + 1) * D] = dz
            dhn = dhn + _dot_nt(dz, w[...])
        dhn_ref[...] = dhn

    row = _cols(tm, D, 0)
    return _pcall(
        body, (dh1, mo, gpost, wout, proj, proj, proj, bra, brb, brc, win, win, win), name=name, grid=(s_len // tm,),
        in_specs=[row, row, _whole((1, D)), _rowsharded(BLK), _cols(tm, D, 5), _cols(tm, D, 6), _cols(tm, D, 7), row, row, row,
                  _win_block(5), _win_block(6), _win_block(7)],
        out_specs=[row, row, row, row, _cols(tm, 3 * D, 0), row, _whole((1, D))],
        out_shape=[_sds((s_len, D), BF)] * 4 + [_sds((s_len, 3 * D), BF), _sds((s_len, D), F32), _sds((1, D), F32)], comm=comm)


def sgu_bwd(dbr, wo, proj, mixed, wst, gv, bv, win, dhn_in, name, comm=None):
    s_len = dbr.shape[0]
    tm = min(TMB, s_len)
    nt = s_len // tm

    def body(dbr_ref, wo_ref, zu_ref, zv_ref, mix_ref, wst_ref, gv_ref, bv_ref, w0, w1, dhn_in_ref,
             dz_ref, dhn_ref, dws_ref, dbs_ref, dgv_ref, dbv_ref, vln_s, dmix_s, dvln_s, bs_acc):
        i = pl.program_id(0)
        first = i == 0
        dsgu = _dot_nt(dbr_ref[...], wo_ref[...].reshape(D, D))
        zu = zu_ref[...].astype(F32)
        zv = zv_ref[...].astype(F32)
        u, tu = _gelu(zu)
        v, tv = _gelu(zv)
        vhat, rs = _ln_fwd(v)
        vln_s[...] = (vhat * gv_ref[...] + bv_ref[...]).astype(BF)
        du = dsgu * mix_ref[...].astype(F32)
        dmix = dsgu * u
        dmix_s[...] = dmix.astype(BF)
        blocks = dmix[0:BLK]
        for n in range(1, tm // BLK):
            blocks = blocks + dmix[n * BLK:(n + 1) * BLK]
        _accum(bs_acc, first, blocks)
        for hd in range(HEADS):
            cs = slice(hd * BLK, (hd + 1) * BLK)
            g = _dot_nt(dmix_s[0:BLK, cs], vln_s[0:BLK, cs])
            for n in range(1, tm // BLK):
                g = g + _dot_nt(dmix_s[n * BLK:(n + 1) * BLK, cs], vln_s[n * BLK:(n + 1) * BLK, cs])

            @pl.when(first)
            def _():
                dws_ref[hd] = g

            @pl.when(jnp.logical_not(first))
            def _():
                dws_ref[hd] += g
        _sgu_mix(wst_ref, dmix_s, dvln_s, None, tm, True)
        dvln = dvln_s[...]
        _accum(dgv_ref, first, _colsum(dvln * vhat))
        _accum(dbv_ref, first, _colsum(dvln))
        dv = _ln_bwd(dvln * gv_ref[...], vhat, rs)
        dzu = (du * _gelu_grad(zu, tu)).astype(BF)
        dzv = (dv * _gelu_grad(zv, tv)).astype(BF)
        dz_ref[:, 0:D] = dzu
        dz_ref[:, D:2 * D] = dzv
        dhn_ref[...] = dhn_in_ref[...] + _dot_nt(dzu, w0[...]) + _dot_nt(dzv, w1[...])

        @pl.when(i == nt - 1)
        def _():
            mask = _sgu_mask(False)
            for hd in range(HEADS):
                dws_ref[hd] = jnp.where(mask, dws_ref[hd], 0.0)
                dbs_ref[:, hd:hd + 1] = jnp.sum(bs_acc[:, hd * BLK:(hd + 1) * BLK], axis=1, keepdims=True)

    row = _cols(tm, D, 0)
    vec = _whole((1, D))
    return _pcall(
        body, (dbr, wo, proj, proj, mixed, wst, gv, bv, win, win, dhn_in), name=name, grid=(nt,),
        in_specs=[row, _rowsharded(BLK), _cols(tm, D, 0), _cols(tm, D, 1), row, _whole((HEADS, BLK, BLK)),
                  vec, vec, _win_block(0), _win_block(1), row],
        out_specs=[_cols(tm, 2 * D, 0), row, _whole((HEADS, BLK, BLK)), _whole((BLK, HEADS)), vec, vec],
        out_shape=[_sds((s_len, 2 * D), BF), _sds((s_len, D), F32), _sds((HEADS, BLK, BLK), F32), _sds((BLK, HEADS), F32),
                   _sds((1, D), F32), _sds((1, D), F32)],
        scratch_shapes=[pltpu.VMEM((tm, D), BF), pltpu.VMEM((tm, D), BF), pltpu.VMEM((tm, D), F32), pltpu.VMEM((BLK, D), F32)],
        comm=comm)


def conv_bwd(dbr, wo, proj, conv, gln, bln, name, comm=None):
    s_len = dbr.shape[0]
    tm = min(TMB, s_len)
    nt = s_len // tm

    def body(dbr_ref, wo_ref, a_ref, g_ref, ah_ref, gh_ref, cv_ref, gln_ref, bln_ref,
             dc_ref, dw_ref, dbdw_ref, dgln_ref, dbln_ref, ext_s, dc_s, dw_acc):
        i = pl.program_id(0)
        first = i == 0
        dcb = _dot_nt(dbr_ref[...], wo_ref[...].reshape(D, D))
        _glu_ext(a_ref, g_ref, ah_ref, gh_ref, ext_s, first)
        chat, rs = _ln_fwd(cv_ref[...].astype(F32))
        yl = chat * gln_ref[...] + bln_ref[...]
        sy = _sig(yl)
        dyl = dcb * sy * (1.0 + yl * (1.0 - sy))
        _accum(dgln_ref, first, _colsum(dyl * chat))
        _accum(dbln_ref, first, _colsum(dyl))
        dc = _ln_bwd(dyl * gln_ref[...], chat, rs)
        _accum(dbdw_ref, first, _colsum(dc))
        dc_ref[...] = dc.astype(BF)
        dc_s[...] = dc

        @pl.when(first)
        def _():
            dw_acc[...] = jnp.zeros_like(dw_acc)

        def chunk(ci, carry):
            r0 = pl.multiple_of((ci // (D // LC)) * RC, RC)
            l0 = pl.multiple_of((ci % (D // LC)) * LC, LC)
            win = ext_s[pl.ds(r0, RC + HALO), pl.ds(l0, LC)]
            dcw = dc_s[pl.ds(r0, RC), pl.ds(l0, LC)]
            for r in range(8):
                wr = win if r == 0 else pltpu.roll(win, r, 0)
                for m in range(4):
                    d = 8 * m + r
                    if d < CONV_W:
                        k = CONV_W - 1 - d
                        prod = dcw * wr[HALO - 8 * m:HALO - 8 * m + RC]
                        dw_acc[k * 8:(k + 1) * 8, pl.ds(l0, LC)] += prod.reshape(RC // 8, 8, LC).sum(axis=0)
            return carry
        lax.fori_loop(0, (tm // RC) * (D // LC), chunk, 0)

        @pl.when(i == nt - 1)
        def _():
            dw_ref[...] = dw_acc[...].reshape(HALO, 8, D).sum(axis=1)

    row = _cols(tm, D, 0)
    vec = _whole((1, D))
    return _pcall(
        body, (dbr, wo, proj, proj, proj, proj, conv, gln, bln), name=name, grid=(nt,),
        in_specs=[row, _rowsharded(BLK), _cols(tm, D, 2), _cols(tm, D, 3), _prev_halo(tm, D, 2), _prev_halo(tm, D, 3),
                  row, vec, vec],
        out_specs=[row, _whole((HALO, D)), vec, vec, vec],
        out_shape=[_sds((s_len, D), BF), _sds((HALO, D), F32), _sds((1, D), F32), _sds((1, D), F32), _sds((1, D), F32)],
        scratch_shapes=[pltpu.VMEM((tm + HALO, D), F32), pltpu.VMEM((tm, D), F32), pltpu.VMEM((HALO * 8, D), F32)], comm=comm)


def pool_bwd(dbr, wo, pooled, wpool, spool, name):
    s_len = dbr.shape[0]
    tm = min(TMB, s_len)

    def body(dbr_ref, wo_ref, pl_ref, wp_ref, sp_ref, dmr_ref, q_ref, dsp_ref, mr_s):
        i = pl.program_id(0)
        dpm = _dot_nt(dbr_ref[...], wo_ref[...].reshape(D, D))
        for gi in range(4):
            cs = slice(gi * POOL_GD, (gi + 1) * POOL_GD)
            mr_s[:, cs] = _dot(pl_ref[:, cs], wp_ref[gi])
        _accum(dsp_ref, i == 0, _colsum(dpm * mr_s[...]))
        dmr = (dpm * sp_ref[...]).astype(BF)
        dmr_ref[...] = dmr
        for gi, w in enumerate(POOL_WINDOWS):
            cs = slice(gi * POOL_GD, (gi + 1) * POOL_GD)
            q_ref[:, cs] = (_dot_nt(dmr[:, cs], wp_ref[gi]) * _inv_count(i, tm, w)).astype(BF)

    row = _cols(tm, D, 0)
    return pl.pallas_call(
        body, name=name, grid=(s_len // tm,),
        in_specs=[row, _rowsharded(BLK), row, _whole((4, POOL_GD, POOL_GD)), _whole((1, D))],
        out_specs=[row, row, _whole((1, D))],
        out_shape=[_sds((s_len, D), BF), _sds((s_len, D), BF), _sds((1, D), F32)],
        scratch_shapes=[pltpu.VMEM((tm, D), F32)], compiler_params=_params(1))(dbr, wo, pooled, wpool, spool)


def seq_bwd(dc, q, proj, wdw, win, dhn_in, h, dh1, gpre, name, comm=None):
    s_len = dc.shape[0]
    tm = min(TMB, s_len)
    nt = s_len // tm

    def body(dc_ref, dch_ref, q_ref, qh_ref, a_ref, g_ref, wdw_ref, w2, w3, w4, dhn_in_ref, h_ref, dh1_ref, gpre_ref,
             dz_ref, dh_ref, dgpre_ref, ext_s, dhc_s, qext_s):
        i = pl.program_id(0)
        last = i == nt - 1
        ext_s[0:tm, :] = dc_ref[...].astype(F32)
        ext_s[tm:, :] = jnp.where(last, 0.0, dch_ref[...].astype(F32))

        def chunk(ci, carry):
            r0 = pl.multiple_of((ci // (D // LC)) * RC, RC)
            l0 = pl.multiple_of((ci % (D // LC)) * LC, LC)
            win_ = ext_s[pl.ds(r0, RC + HALO), pl.ds(l0, LC)]
            acc = jnp.zeros((RC, LC), F32)
            for r in range(8):
                wr = win_ if r == 0 else pltpu.roll(win_, RC + HALO - r, 0)
                for m in range(4):
                    d = 8 * m + r
                    if d < CONV_W:
                        k = CONV_W - 1 - d
                        acc = acc + wdw_ref[k:k + 1, pl.ds(l0, LC)] * wr[8 * m:8 * m + RC]
            dhc_s[pl.ds(r0, RC), pl.ds(l0, LC)] = acc
            return carry
        lax.fori_loop(0, (tm // RC) * (D // LC), chunk, 0)

        dhc = dhc_s[...]
        av = a_ref[...].astype(F32)
        sg = _sig(g_ref[...].astype(F32))
        da = (dhc * sg).astype(BF)
        dg = (dhc * av * sg * (1.0 - sg)).astype(BF)
        dz_ref[:, 0:D] = da
        dz_ref[:, D:2 * D] = dg

        qext_s[0:tm, :] = q_ref[...].astype(F32)
        qext_s[tm:, :] = jnp.where(last, 0.0, qh_ref[...].astype(F32))
        for gi, w in enumerate(POOL_WINDOWS):
            cs = slice(gi * POOL_GD, (gi + 1) * POOL_GD)
            e = qext_s[:, cs]
            s = e
            sh = 1
            while sh < w:
                s = s + pltpu.roll(s, tm + HALO - sh, 0)
                sh *= 2
            dz_ref[:, 2 * D + gi * POOL_GD:2 * D + (gi + 1) * POOL_GD] = (s[0:tm] - e[0:tm] * _count(i, tm, w)).astype(BF)
        dhn = dhn_in_ref[...] + _dot_nt(da, w2[...]) + _dot_nt(dg, w3[...]) + _dot_nt(dz_ref[:, 2 * D:3 * D], w4[...])
        x = h_ref[...]
        r = _rstd(x)
        _accum(dgpre_ref, i == 0, _colsum(dhn * x * r))
        dh_ref[...] = dh1_ref[...] + _rms_bwd(x, dhn * gpre_ref[...], r)

    row = _cols(tm, D, 0)
    return _pcall(
        body, (dc, dc, q, q, proj, proj, wdw, win, win, win, dhn_in, h, dh1, gpre), name=name, grid=(nt,),
        in_specs=[row, _next_halo(tm, D, 0, s_len), row, _next_halo(tm, D, 0, s_len), _cols(tm, D, 2), _cols(tm, D, 3),
                  _whole((HALO, D)), _win_block(2), _win_block(3), _win_block(4), row, row, row, _whole((1, D))],
        out_specs=[_cols(tm, 3 * D, 0), row, _whole((1, D))],
        out_shape=[_sds((s_len, 3 * D), BF), _sds((s_len, D), F32), _sds((1, D), F32)],
        scratch_shapes=[pltpu.VMEM((tm + HALO, D), F32), pltpu.VMEM((tm, D), F32), pltpu.VMEM((tm + HALO, D), F32)], comm=comm)


def wgrad(a, b, tk, tn, name, stacked=False, diag=False, a_spec=None, comm=None):
    s_len = b.shape[0]
    k_dim = a.shape[-1]
    n_dim = b.shape[1]
    ts = min(TS, s_len)
    nk = 1 if diag else k_dim // tk
    nn, ns = n_dim // tn, s_len // ts

    def body(a_ref, b_ref, o_ref, acc):
        s = pl.program_id(2)
        _accum(acc, s == 0, _dot_tn(a_ref[...].astype(BF), b_ref[...].astype(BF)))

        @pl.when(s == ns - 1)
        def _():
            o_ref[...] = acc[...].astype(BF).reshape(o_ref.shape)

    if a_spec is None:
        a_spec = pl.BlockSpec((ts, tk), (lambda k, n, s: (s, n)) if diag else (lambda k, n, s: (s, k)))
    if stacked or diag:
        out_shape = _sds((nn, tk if diag else k_dim, tn), BF)
        o_spec = pl.BlockSpec((1, tk, tn), lambda k, n, s: (n, k, 0))
    else:
        out_shape = _sds((k_dim, n_dim), BF)
        o_spec = pl.BlockSpec((tk, tn), lambda k, n, s: (k, n))
    (out,), got = _pcall(
        body, (a, b), name=name, grid=(nk, nn, ns),
        in_specs=[a_spec, pl.BlockSpec((ts, tn), lambda k, n, s: (s, n))], out_specs=[o_spec], out_shape=[out_shape],
        scratch_shapes=[pltpu.VMEM((tk, tn), F32)], comm=comm)
    return out if comm is None else (out, got)


_WEIGHTS = ['g_mix_pre', 'w_in', 'w_sgu_s', 'b_sgu_s', 'g_sgu_v', 'b_sgu_v', 'w_sgu_out', 'w_dw', 'b_dw', 'g_conv_ln', 'b_conv_ln',
            'w_conv_out', 'w_pool', 's_pool', 'w_pool_out', 'w_out', 'g_mix_post', 'g_ffn_pre', 'w_ffn_in', 'w_ffn_out', 'g_ffn_post',
            'w_ple', 'w_ple_gate']
_SHARDED = ['w_in', 'w_sgu_out', 'w_conv_out', 'w_pool', 'w_pool_out', 'w_out', 'w_ffn_in', 'w_ffn_out', 'w_ple', 'w_ple_gate']
_VECTORS = ['g_mix_pre', 'g_sgu_v', 'b_sgu_v', 'b_dw', 'g_conv_ln', 'b_conv_ln', 's_pool', 'g_mix_post', 'g_ffn_pre', 'g_ffn_post']
_SUBLANES = 8
_SGU_ROWS = HEADS * BLK * BLK // D
_REP_ROWS = _SUBLANES * (len(_VECTORS) + 2) + _SGU_ROWS


def _pack_replicated(t, layer):
    rows = [jnp.pad(t[k][layer].reshape(1, D), ((0, _SUBLANES - 1), (0, 0))) for k in _VECTORS + ['b_sgu_s']]
    return jnp.concatenate(rows + [t['w_sgu_s'][layer].reshape(_SGU_ROWS, D), jnp.zeros((_SUBLANES, D), F32)], axis=0)


def _unpack_replicated(packed):
    out = {}
    for i, k in enumerate(_VECTORS):
        out[k] = packed[:, _SUBLANES * i, :]
    o = _SUBLANES * len(_VECTORS)
    out['b_sgu_s'] = packed[:, o, :].reshape(2, HEADS, BLK)
    out['w_sgu_s'] = packed[:, o + _SUBLANES:o + _SUBLANES + _SGU_ROWS, :].reshape(2, HEADS, BLK, BLK)
    return out


def _pad_taps(w):
    return jnp.pad(w, ((0, HALO - CONV_W), (0, 0)))


def kernel(x, p, g_mix_pre, w_in, w_sgu_s, b_sgu_s, g_sgu_v, b_sgu_v, w_sgu_out, w_dw, b_dw, g_conv_ln, b_conv_ln, w_conv_out, w_pool, s_pool, w_pool_out, w_out, g_mix_post, g_ffn_pre, w_ffn_in, w_ffn_out, g_ffn_post, w_ple, w_ple_gate, loss_target, m_g_mix_pre, m_w_in, m_w_sgu_s, m_b_sgu_s, m_g_sgu_v, m_b_sgu_v, m_w_sgu_out, m_w_dw, m_b_dw, m_g_conv_ln, m_b_conv_ln, m_w_conv_out, m_w_pool, m_s_pool, m_w_pool_out, m_w_out, m_g_mix_post, m_g_ffn_pre, m_w_ffn_in, m_w_ffn_out, m_g_ffn_post, m_w_ple, m_w_ple_gate, v_g_mix_pre, v_w_in, v_w_sgu_s, v_b_sgu_s, v_g_sgu_v, v_b_sgu_v, v_w_sgu_out, v_w_dw, v_b_dw, v_g_conv_ln, v_b_conv_ln, v_w_conv_out, v_w_pool, v_s_pool, v_w_pool_out, v_w_out, v_g_mix_post, v_g_ffn_pre, v_w_ffn_in, v_w_ffn_out, v_g_ffn_post, v_w_ple, v_w_ple_gate):
    W = dict(g_mix_pre=g_mix_pre, w_in=w_in, w_sgu_s=w_sgu_s, b_sgu_s=b_sgu_s, g_sgu_v=g_sgu_v, b_sgu_v=b_sgu_v, w_sgu_out=w_sgu_out,
             w_dw=w_dw, b_dw=b_dw, g_conv_ln=g_conv_ln, b_conv_ln=b_conv_ln, w_conv_out=w_conv_out, w_pool=w_pool, s_pool=s_pool,
             w_pool_out=w_pool_out, w_out=w_out, g_mix_post=g_mix_post, g_ffn_pre=g_ffn_pre, w_ffn_in=w_ffn_in, w_ffn_out=w_ffn_out,
             g_ffn_post=g_ffn_post, w_ple=w_ple, w_ple_gate=w_ple_gate)
    M = dict(g_mix_pre=m_g_mix_pre, w_in=m_w_in, w_sgu_s=m_w_sgu_s, b_sgu_s=m_b_sgu_s, g_sgu_v=m_g_sgu_v, b_sgu_v=m_b_sgu_v,
             w_sgu_out=m_w_sgu_out, w_dw=m_w_dw, b_dw=m_b_dw, g_conv_ln=m_g_conv_ln, b_conv_ln=m_b_conv_ln, w_conv_out=m_w_conv_out,
             w_pool=m_w_pool, s_pool=m_s_pool, w_pool_out=m_w_pool_out, w_out=m_w_out, g_mix_post=m_g_mix_post, g_ffn_pre=m_g_ffn_pre,
             w_ffn_in=m_w_ffn_in, w_ffn_out=m_w_ffn_out, g_ffn_post=m_g_ffn_post, w_ple=m_w_ple, w_ple_gate=m_w_ple_gate)
    V = dict(g_mix_pre=v_g_mix_pre, w_in=v_w_in, w_sgu_s=v_w_sgu_s, b_sgu_s=v_b_sgu_s, g_sgu_v=v_g_sgu_v, b_sgu_v=v_b_sgu_v,
             w_sgu_out=v_w_sgu_out, w_dw=v_w_dw, b_dw=v_b_dw, g_conv_ln=v_g_conv_ln, b_conv_ln=v_b_conv_ln, w_conv_out=v_w_conv_out,
             w_pool=v_w_pool, s_pool=v_s_pool, w_pool_out=v_w_pool_out, w_out=v_w_out, g_mix_post=v_g_mix_post, g_ffn_pre=v_g_ffn_pre,
             w_ffn_in=v_w_ffn_in, w_ffn_out=v_w_ffn_out, g_ffn_post=v_g_ffn_post, w_ple=v_w_ple, w_ple_gate=v_w_ple_gate)

    my_c = lax.axis_index("c")
    core_id = my_c.astype(jnp.int32).reshape(1)
    my_chip = 2 * lax.axis_index("x") + lax.axis_index("y")
    chip_id = my_chip.astype(jnp.int32).reshape(1)
    my_dev = 2 * my_chip + my_c
    s_len = x.shape[1]
    h0 = x.reshape(s_len, D)
    target = loss_target.reshape(s_len, D)

    shard = [{k: W[k][l].astype(BF) for k in _SHARDED} for l in range(2)]
    for l in range(2):
        shard[l]['w_dw'] = w_dw[l]
        shard[l]['w_ffn_in'] = jnp.swapaxes(w_ffn_in[l], 0, 1).astype(BF)
    mixer_w = ['w_sgu_out', 'w_conv_out', 'w_pool', 'w_pool_out', 'w_out', 'w_dw']
    ffn_w = ['w_ffn_in', 'w_ffn_out', 'w_ple', 'w_ple_gate']
    hosted_gather = {
        'norm_proj_in': (0, mixer_w),
        'conv_fwd': (0, ffn_w),
        'merge_out': (1, mixer_w),
        'norm_proj_ffn': (1, ['w_in']),
        'ffn_out': (1, ffn_w),
    }
    G = [{'w_in': run_comm(Gather([shard[0]['w_in']]), "gather_w_in_0")[0]}, {}]

    def gather_in(layer, call):
        if layer != 0:
            return None, (lambda got: None)
        to_layer, keys = hosted_gather[call]
        return Gather([shard[to_layer][k] for k in keys]), (lambda got: G[to_layer].update(zip(keys, got)))

    def natural_mixer(g):
        wpool = jnp.transpose(g['w_pool'], (1, 0, 2, 3)).reshape(4, POOL_GD, POOL_GD)
        wdw = jnp.transpose(g['w_dw'].reshape(N_DEV, CONV_W, BLK), (1, 0, 2)).reshape(CONV_W, D)
        return dict(wpool=wpool, wdw=_pad_taps(wdw))

    def natural_ffn(g):
        wfit = g['w_ffn_in'].reshape(4, D_FF // 2, D)
        wple = jnp.transpose(g['w_ple'], (1, 0, 2)).reshape(PLE, D)
        return dict(wfit=wfit, wple=wple)

    def vec(name, layer):
        return W[name][layer].reshape(1, D)

    saved = []
    h = h0
    for l in range(2):
        g = G[l]
        comm, land = gather_in(l, 'norm_proj_in')
        (proj, hn), got = norm_proj(h, vec('g_mix_pre', l), g['w_in'], f"norm_proj_in_{l}", comm=comm)
        land(got)
        nat = natural_mixer(g)
        bsfull = jnp.repeat(b_sgu_s[l].T, BLK, axis=1)
        wst = jnp.swapaxes(w_sgu_s[l], 1, 2)
        mixed, sgu, bra = sgu_fwd(proj, w_sgu_s[l], bsfull, vec('g_sgu_v', l), vec('b_sgu_v', l), g['w_sgu_out'], f"sgu_fwd_{l}")
        comm, land = gather_in(l, 'conv_fwd')
        (conv, cb, brb), got = conv_fwd(proj, nat['wdw'], vec('b_dw', l), vec('g_conv_ln', l), vec('b_conv_ln', l), g['w_conv_out'],
                                        f"conv_fwd_{l}", comm=comm)
        land(got)
        nat.update(natural_ffn(g))
        pooled, pm, brc = pool_fwd(proj, nat['wpool'], vec('s_pool', l), g['w_pool_out'], f"pool_fwd_{l}")
        comm, land = gather_in(l, 'merge_out')
        (merged, mo, h1), got = merge_out(proj, bra, brb, brc, h, g['w_out'], vec('g_mix_post', l), f"merge_out_{l}", comm=comm)
        land(got)
        comm, land = gather_in(l, 'norm_proj_ffn')
        (ff, hn2), got = norm_proj(h1, vec('g_ffn_pre', l), nat['wfit'], f"norm_proj_ffn_{l}", transposed=True, comm=comm)
        land(got)
        comm, land = gather_in(l, 'ffn_out')
        (act, f, h2, pg, h3), got = ffn_out(ff, h1, p, g['w_ffn_out'], vec('g_ffn_post', l), g['w_ple_gate'], nat['wple'], l,
                                            f"ffn_out_{l}", comm=comm)
        land(got)
        saved.append(dict(h=h, nat=nat, wst=wst, proj=proj, hn=hn, mixed=mixed, sgu=sgu, bra=bra, conv=conv, cb=cb, brb=brb,
                          pooled=pooled, pm=pm, brc=brc, merged=merged, mo=mo, h1=h1, ff=ff, hn2=hn2, act=act, f=f, h2=h2, pg=pg))
        h = h3

    dh = h

    parts = {k: [None, None] for k in _SHARDED}
    small = {k: [None, None] for k in _VECTORS + ['b_sgu_s', 'w_sgu_s', 'w_dw']}
    chip_parts = [{}, {}]
    from_chips = [{}, {}]
    gathered_small = [None, None]

    def to_sibling(layer, keys):
        return ToSibling([parts[k][layer] for k in keys])

    def add_siblings(layer, keys, from_sibling):
        for k, rv in zip(keys, from_sibling):
            st = parts[k][layer]
            cols = st.shape[-1]
            chip_parts[layer][k] = add_core_side(st.reshape(N_DEV, -1, cols), rv.reshape(4, -1, cols), core_id,
                                                 f"rs_add_{k}_{layer}").reshape(rv.shape)

    def to_chips(layer, keys):
        return ToChips([chip_parts[layer][k] for k in keys])

    def small_pack(layer):
        return jnp.concatenate([_pack_replicated(small, layer), small['w_dw'][layer]], axis=0)

    ffn_group = ['w_ffn_in', 'w_ffn_out', 'w_ple_gate', 'w_ple']
    mix_group = ['w_out', 'w_sgu_out', 'w_conv_out', 'w_pool_out', 'w_pool']
    big = ['w_in', 'w_ffn_in']
    others = [k for k in _SHARDED if k not in big]
    hosted_rs = {
        'ple_ffn_bwd': (lambda: to_sibling(1, _SHARDED), lambda got: add_siblings(1, _SHARDED, got)),
        'mix_post_bwd': (lambda: Both(to_sibling(0, ffn_group), Gather([small_pack(1)])),
                         lambda got: (add_siblings(0, ffn_group, got[:-1]), gathered_small.__setitem__(1, got[-1]))),
        'ffn_in_bwd': (lambda: to_chips(1, others), lambda got: from_chips[1].update(zip(others, got))),
        'sgu_bwd': (lambda: to_chips(1, big), lambda got: from_chips[1].update(zip(big, got))),
        'conv_bwd': (lambda: to_chips(0, ffn_group), lambda got: from_chips[0].update(zip(ffn_group, got))),
        'wgrad_in_sgu': (lambda: to_sibling(0, mix_group), lambda got: add_siblings(0, mix_group, got)),
        'wgrad_in_seq': (lambda: to_chips(0, mix_group), lambda got: from_chips[0].update(zip(mix_group, got))),
        'wgrad_in_gate': (lambda: Gather([small_pack(0)]), lambda got: gathered_small.__setitem__(0, got[0])),
    }

    def exchange_in(layer, call):
        if layer != 0 or call not in hosted_rs:
            return None, (lambda got: None)
        make, land = hosted_rs[call]
        return make(), land

    for l in (1, 0):
        sv, g, nat = saved[l], G[l], saved[l]['nat']

        def wg(call, a, b, tk, tn, **kw):
            comm, land = exchange_in(l, call)
            if comm is None:
                return wgrad(a, b, tk, tn, f"{call}_{l}", **kw)
            out, got = wgrad(a, b, tk, tn, f"{call}_{l}", comm=comm, **kw)
            land(got)
            return out

        comm, land = exchange_in(l, 'ple_ffn_bwd')
        res, got = ple_ffn_bwd(
            dh, sv['pg'], p, sv['f'], sv['ff'], nat['wple'], g['w_ple_gate'], g['w_ffn_out'], vec('g_ffn_post', l), l,
            f"ple_ffn_bwd_{l}", target=target if l == 1 else None, comm=comm)
        land(got)
        dh2, dpe, dpg, df, dff, small['g_ffn_post'][l] = res[:6]
        if l == 1:
            loss = lax.psum(res[6][0, 0], ("x", "y", "c"))
        comm, land = exchange_in(l, 'ffn_in_bwd')
        (dh1, small['g_ffn_pre'][l]), got = ffn_in_bwd(dff, nat['wfit'], sv['h1'], dh2, vec('g_ffn_pre', l), f"ffn_in_bwd_{l}",
                                                       comm=comm)
        land(got)
        p_spec = pl.BlockSpec((None, None, min(TS, s_len), PLE), functools.partial(lambda k, n, s, ll: (ll, 0, s, 0), ll=l))
        parts['w_ple'][l] = jnp.transpose(wg('wgrad_ple', p, dpe, PLE, D, a_spec=p_spec).reshape(PLE, N_DEV, BLK), (1, 0, 2))
        parts['w_ple_gate'][l] = wg('wgrad_ple_gate', sv['h2'], dpg, D, D).reshape(N_DEV, BLK, D)
        parts['w_ffn_out'][l] = wg('wgrad_ffn_out', sv['act'], df, D_FF // 2, D).reshape(N_DEV, D_FF // N_DEV, D)
        parts['w_ffn_in'][l] = wg('wgrad_ffn_in', dff, sv['hn2'], D_FF // 2, D).reshape(N_DEV, D_FF // 4, D)

        comm, land = exchange_in(l, 'mix_post_bwd')
        (dmo, dbra, dbrb, dbrc, dzg, dhn_g, small['g_mix_post'][l]), got = mix_post_bwd(
            dh1, sv['mo'], vec('g_mix_post', l), g['w_out'], sv['proj'], sv['bra'], sv['brb'], sv['brc'], g['w_in'],
            f"mix_post_bwd_{l}", comm=comm)
        land(got)
        comm, land = exchange_in(l, 'sgu_bwd')
        (dzs, dhn_ag, dws, dbs, small['g_sgu_v'][l], small['b_sgu_v'][l]), got = sgu_bwd(
            dbra, g['w_sgu_out'], sv['proj'], sv['mixed'], sv['wst'], vec('g_sgu_v', l), vec('b_sgu_v', l), g['w_in'],
            dhn_g, f"sgu_bwd_{l}", comm=comm)
        land(got)
        small['w_sgu_s'][l] = dws
        small['b_sgu_s'][l] = dbs.T
        comm, land = exchange_in(l, 'conv_bwd')
        (dc, dwdw, small['b_dw'][l], small['g_conv_ln'][l], small['b_conv_ln'][l]), got = conv_bwd(
            dbrb, g['w_conv_out'], sv['proj'], sv['conv'], vec('g_conv_ln', l), vec('b_conv_ln', l), f"conv_bwd_{l}", comm=comm)
        land(got)
        small['w_dw'][l] = dwdw
        dmr, q, small['s_pool'][l] = pool_bwd(dbrc, g['w_pool_out'], sv['pooled'], nat['wpool'], vec('s_pool', l), f"pool_bwd_{l}")
        comm, land = exchange_in(l, 'seq_bwd')
        (dzc, dh, small['g_mix_pre'][l]), got = seq_bwd(dc, q, sv['proj'], nat['wdw'], g['w_in'], dhn_ag, sv['h'], dh1,
                                                        vec('g_mix_pre', l), f"seq_bwd_{l}", comm=comm)
        land(got)

        parts['w_out'][l] = wg('wgrad_out', sv['merged'], dmo, D, D).reshape(N_DEV, BLK, D)
        parts['w_sgu_out'][l] = wg('wgrad_sgu_out', sv['sgu'], dbra, D, D).reshape(N_DEV, BLK, D)
        parts['w_conv_out'][l] = wg('wgrad_conv_out', sv['cb'], dbrb, D, D).reshape(N_DEV, BLK, D)
        parts['w_pool_out'][l] = wg('wgrad_pool_out', sv['pm'], dbrc, D, D).reshape(N_DEV, BLK, D)
        g_pool = wg('wgrad_pool', sv['pooled'], dmr, POOL_GD, POOL_GD, diag=True)
        parts['w_pool'][l] = jnp.transpose(g_pool.reshape(4, N_DEV, POOL_GD // N_DEV, POOL_GD), (1, 0, 2, 3))
        parts['w_in'][l] = jnp.concatenate([
            wg('wgrad_in_sgu', sv['hn'], dzs, D, D, stacked=True),
            wg('wgrad_in_seq', sv['hn'], dzc, D, D, stacked=True),
            wg('wgrad_in_gate', sv['hn'], dzg, D, D, stacked=True)], axis=0)
    grad_x = dh.reshape(1, s_len, D)

    add_siblings(0, ['w_in'], run_comm(to_sibling(0, ['w_in']), "rs_to_sibling_w_in_0"))
    from_chips[0]['w_in'] = run_comm(to_chips(0, ['w_in']), "rs_to_chips_w_in_0")[0]

    outs = {}
    for k in _SHARDED:
        wmv = [jnp.swapaxes(t[k], 1, 2) if k == 'w_ffn_in' else t[k] for t in (W, M, V)]
        cols = wmv[0].shape[-1]
        pieces = []
        for layer in range(2):
            own4 = chip_parts[layer][k].reshape(4, -1, cols)
            rv3 = from_chips[layer][k].reshape(3, -1, cols)
            pieces.append([(own4, OWN_CHIP), (rv3, 0), (rv3, 1), (rv3, 2)])
        res = adamw_layers(*[t.reshape(2, -1, cols) for t in wmv], pieces, chip_id, f"adamw_{k}")
        res = [r.reshape(wmv[0].shape) for r in res]
        outs[k] = [jnp.swapaxes(r, 1, 2) for r in res] if k == 'w_ffn_in' else res

    packed = [jnp.stack([_pack_replicated(t, 0), _pack_replicated(t, 1)], axis=0) for t in (W, M, V)]
    rep_res = adamw_layers(*packed, [[(gathered_small[layer], d) for d in range(N_DEV)] for layer in range(2)], chip_id,
                           "adamw_replicated")
    for idx, res in enumerate(rep_res):
        for name, val in _unpack_replicated(res).items():
            outs.setdefault(name, [None] * 4)[idx] = val
    dw_sum = jnp.stack([sum_slabs(gathered_small[layer][:, _REP_ROWS:, :], f"sum_w_dw_{layer}") for layer in range(2)], axis=0)
    dw_mine = lax.dynamic_slice_in_dim(dw_sum[:, :CONV_W], my_dev * BLK, BLK, axis=2)
    res = adamw(w_dw.reshape(2 * CONV_W, BLK), m_w_dw.reshape(2 * CONV_W, BLK), v_w_dw.reshape(2 * CONV_W, BLK),
                [(dw_mine.reshape(2 * CONV_W, BLK), None)], "adamw_w_dw")
    outs['w_dw'] = [r.reshape(w_dw.shape) for r in res]

    result = [loss, grad_x]
    for idx in range(4):
        result += [outs[k][idx] for k in _WEIGHTS]
    return tuple(result)
```

```python
import functools
import math

import jax
import jax.numpy as jnp
from jax import lax
from jax.experimental import pallas as pl
from jax.experimental.pallas import tpu as pltpu

F32 = jnp.float32
BF = jnp.bfloat16

D = 1024
D_FF = 2816
PLE = 256
N_DEV = 8
HEADS = 8
BLK = 128
CHUNK = 64
CONV_W = 31
POOL_WINDOWS = (2, 4, 8, 16)
POOL_GD = 256
EPS = 1e-6

V7X_VMEM_BYTES = 64 * 2**20
VMEM_LIMIT = V7X_VMEM_BYTES * 7 // 8
HALO = 32
RC = 64
LC = 128
TM = 512
TMB = 512
TMW = 256
TMP = 1024
TS = 2048

ADAM_LR, ADAM_B1, ADAM_B2, ADAM_EPS, ADAM_WD, ADAM_STEP = 0.001, 0.9, 0.999, 1e-08, 0.01, 10

MESH = pl.DeviceIdType.MESH
ANY = pl.BlockSpec(memory_space=pl.ANY)

_GELU_K0 = math.sqrt(2.0 / math.pi)
_GELU_K1 = 0.044715
_LOG2E = 1.4426950408889634


def _dot(a, b):
    return jnp.dot(a, b, preferred_element_type=F32)


def _dot_nt(a, b):
    return lax.dot_general(a, b, (((1,), (1,)), ((), ())), preferred_element_type=F32)


def _dot_tn(a, b):
    return lax.dot_general(a, b, (((0,), (0,)), ((), ())), preferred_element_type=F32)


def _sig(x):
    return 1.0 / (1.0 + jnp.exp2(x * (-_LOG2E)))


def _gelu(x):
    s = 1.0 / (1.0 + jnp.exp2(x * ((-2.0 * _GELU_K0 * _LOG2E) + (-2.0 * _GELU_K0 * _GELU_K1 * _LOG2E) * (x * x))))
    return x * s, s


def _gelu_grad(x, s):
    return s + x * s * (1.0 - s) * ((2.0 * _GELU_K0) + (6.0 * _GELU_K0 * _GELU_K1) * (x * x))


def _rstd(x):
    return lax.rsqrt(jnp.mean(x * x, axis=-1, keepdims=True) + EPS)


def _rms_bwd(x, gd, r):
    return r * gd - x * (r * r * r) * jnp.mean(gd * x, axis=-1, keepdims=True)


def _ln_fwd(x):
    mu = jnp.mean(x, axis=-1, keepdims=True)
    xc = x - mu
    rs = lax.rsqrt(jnp.mean(xc * xc, axis=-1, keepdims=True) + EPS)
    return xc * rs, rs


def _ln_bwd(dhat, hat, rs):
    return rs * (dhat - jnp.mean(dhat, axis=-1, keepdims=True) - hat * jnp.mean(dhat * hat, axis=-1, keepdims=True))


def _colsum(x):
    return jnp.sum(x, axis=0, keepdims=True)


def _accum(ref, first, val):
    @pl.when(first)
    def _():
        ref[...] = val

    @pl.when(jnp.logical_not(first))
    def _():
        ref[...] += val


def _sgu_mask(transposed):
    r = lax.broadcasted_iota(jnp.int32, (BLK, BLK), 0) // CHUNK
    c = lax.broadcasted_iota(jnp.int32, (BLK, BLK), 1) // CHUNK
    return (r <= c) if transposed else (c <= r)


def _inv_count(i, tm, w):
    t = lax.broadcasted_iota(jnp.int32, (tm, 1), 0) + i * tm
    return 1.0 / jnp.minimum(t + 1, w).astype(F32)


def _count(i, tm, w):
    t = lax.broadcasted_iota(jnp.int32, (tm, 1), 0) + i * tm
    return jnp.minimum(t + 1, w).astype(F32)


def _params(n_grid):
    return pltpu.CompilerParams(dimension_semantics=("arbitrary",) * n_grid, vmem_limit_bytes=VMEM_LIMIT)


def _sds(shape, dtype):
    return jax.ShapeDtypeStruct(shape, dtype)


def _cols(tm, width, cb):
    return pl.BlockSpec((tm, width), lambda i: (i, cb))


def _whole(shape):
    nd = len(shape)
    return pl.BlockSpec(shape, lambda i: (0,) * nd)


def _prev_halo(tm, width, cb):
    return pl.BlockSpec((HALO, width), lambda i: (jnp.maximum(i * (tm // HALO) - 1, 0), cb))


def _next_halo(tm, width, cb, s_len):
    last = s_len // HALO - 1
    return pl.BlockSpec((HALO, width), lambda i: (jnp.minimum((i + 1) * (tm // HALO), last), cb))


def _rowsharded(rows):
    return pl.BlockSpec((N_DEV, rows, D), lambda i: (0, 0, 0))


def _win_block(j):
    return pl.BlockSpec((None, D, D), lambda i: (j, 0, 0))


def _row_tile(rows, cap):
    t = min(rows, cap)
    while rows % t or t % 16:
        t -= 16
    return t


class Gather:
    def __init__(self, arrs):
        self.arrs = list(arrs)
        n = self.n = len(self.arrs)
        self.out_shape = [_sds((N_DEV,) + a.shape, a.dtype) for a in self.arrs]
        self.scratch = [pltpu.SemaphoreType.DMA((n, 7)), pltpu.SemaphoreType.DMA((n, 7)), pltpu.SemaphoreType.DMA((n,))]

    def _plan(self, ins, outs, sems):
        send, recv, local = sems
        x, y, c = lax.axis_index("x"), lax.axis_index("y"), lax.axis_index("c")
        me, sibling = (x, y, c), (x, y, 1 - c)
        chips = [(1 - x, y), (x, 1 - y), (1 - x, 1 - y)]

        def copy(a, k, block, to, src=None):
            dst = outs[a].at[4 * block[0] + 2 * block[1] + block[2]]
            return pltpu.make_async_remote_copy(
                src_ref=dst if src is None else src, dst_ref=dst, send_sem=send.at[a, k], recv_sem=recv.at[a, k],
                device_id=to, device_id_type=MESH)

        mine = [pltpu.make_async_copy(ins[a], outs[a].at[4 * x + 2 * y + c], local.at[a]) for a in range(self.n)]
        first = []
        for a in range(self.n):
            first.append(copy(a, 0, me, sibling, src=ins[a]))
            first += [copy(a, 1 + j, me, (*chip, c), src=ins[a]) for j, chip in enumerate(chips)]
        return me, sibling, chips, c, copy, mine, first

    def start(self, ins, outs, sems):
        *_, mine, first = self._plan(ins, outs, sems)
        for cp in mine + first:
            cp.start()

    def advance(self, ins, outs, sems):
        me, sibling, chips, c, copy, _, _ = self._plan(ins, outs, sems)
        for a in range(self.n):
            for j, chip in enumerate(chips):
                copy(a, 1 + j, (*chip, c), me).wait_recv()
                copy(a, 4 + j, (*chip, c), sibling).start()

    def finish(self, ins, outs, sems):
        me, sibling, chips, c, copy, mine, first = self._plan(ins, outs, sems)
        passed = [copy(a, 4 + j, (*chip, c), sibling) for a in range(self.n) for j, chip in enumerate(chips)]
        for a in range(self.n):
            copy(a, 0, sibling, me).wait_recv()
            for j, chip in enumerate(chips):
                copy(a, 4 + j, (*chip, 1 - c), me).wait_recv()
        for cp in first + passed:
            cp.wait_send()
        for cp in mine:
            cp.wait()


class ToSibling:
    def __init__(self, parts):
        self.arrs = list(parts)
        n = self.n = len(self.arrs)
        self.out_shape = [_sds((4,) + p.shape[1:], p.dtype) for p in self.arrs]
        self.scratch = [pltpu.SemaphoreType.DMA((n,)), pltpu.SemaphoreType.DMA((n,))]

    def start(self, ins, outs, sems):
        send, recv = sems
        x, y, c = lax.axis_index("x"), lax.axis_index("y"), lax.axis_index("c")
        for a in range(self.n):
            for q in range(4):
                pltpu.make_async_remote_copy(
                    src_ref=ins[a].at[2 * q + 1 - c], dst_ref=outs[a].at[q], send_sem=send.at[a], recv_sem=recv.at[a],
                    device_id=(x, y, 1 - c), device_id_type=MESH).start()

    def advance(self, ins, outs, sems):
        pass

    def finish(self, ins, outs, sems):
        send, recv = sems
        x, y, c = lax.axis_index("x"), lax.axis_index("y"), lax.axis_index("c")
        for a in range(self.n):
            pltpu.make_async_remote_copy(
                src_ref=outs[a], dst_ref=outs[a], send_sem=send.at[a], recv_sem=recv.at[a],
                device_id=(x, y, 1 - c), device_id_type=MESH).wait()


class ToChips:
    def __init__(self, cps):
        self.arrs = list(cps)
        n = self.n = len(self.arrs)
        self.out_shape = [_sds((3,) + p.shape[1:], p.dtype) for p in self.arrs]
        self.scratch = [pltpu.SemaphoreType.DMA((n,)), pltpu.SemaphoreType.DMA((n,))]

    def start(self, ins, outs, sems):
        send, recv = sems
        x, y, c = lax.axis_index("x"), lax.axis_index("y"), lax.axis_index("c")
        for a in range(self.n):
            for r, (px, py) in enumerate([(1 - x, y), (x, 1 - y), (1 - x, 1 - y)]):
                pltpu.make_async_remote_copy(
                    src_ref=ins[a].at[2 * px + py], dst_ref=outs[a].at[r], send_sem=send.at[a], recv_sem=recv.at[a],
                    device_id=(px, py, c), device_id_type=MESH).start()

    def advance(self, ins, outs, sems):
        pass

    def finish(self, ins, outs, sems):
        send, recv = sems
        x, y, c = lax.axis_index("x"), lax.axis_index("y"), lax.axis_index("c")
        for a in range(self.n):
            pltpu.make_async_remote_copy(
                src_ref=outs[a], dst_ref=outs[a], send_sem=send.at[a], recv_sem=recv.at[a],
                device_id=(x, y, c), device_id_type=MESH).wait()


class Both:
    def __init__(self, a, b):
        self.a, self.b = a, b
        self.arrs, self.n = a.arrs + b.arrs, a.n + b.n
        self.out_shape, self.scratch = a.out_shape + b.out_shape, a.scratch + b.scratch

    def _each(self, ins, outs, sems):
        na, ns = self.a.n, len(self.a.scratch)
        return (self.a, (ins[:na], outs[:na], sems[:ns])), (self.b, (ins[na:], outs[na:], sems[ns:]))

    def start(self, ins, outs, sems):
        for comm, refs in self._each(ins, outs, sems):
            comm.start(*refs)

    def advance(self, ins, outs, sems):
        for comm, refs in self._each(ins, outs, sems):
            comm.advance(*refs)

    def finish(self, ins, outs, sems):
        for comm, refs in self._each(ins, outs, sems):
            comm.finish(*refs)


def run_comm(comm, name):
    n = comm.n

    def body(*refs):
        ins, outs, sems = refs[:n], refs[n:2 * n], refs[2 * n:]
        comm.start(ins, outs, sems)
        comm.advance(ins, outs, sems)
        comm.finish(ins, outs, sems)

    return pl.pallas_call(body, name=name, out_shape=comm.out_shape, in_specs=[ANY] * n, out_specs=[ANY] * n,
                          scratch_shapes=comm.scratch)(*comm.arrs)


def _pcall(body, args, *, name, grid, in_specs, out_specs, out_shape, scratch_shapes=(), comm=None):
    params = _params(len(grid))
    scratch_shapes = list(scratch_shapes)
    if comm is None:
        outs = pl.pallas_call(body, name=name, grid=grid, in_specs=in_specs, out_specs=out_specs, out_shape=out_shape,
                              scratch_shapes=scratch_shapes, compiler_params=params)(*args)
        return outs, None
    n_in, n_out, n_scr, nc = len(in_specs), len(out_specs), len(scratch_shapes), comm.n
    n_steps = math.prod(grid)
    advance_at = n_steps - 1 - max(1, n_steps // 8) if n_steps >= 4 else None

    def hosted(*refs):
        ins, cins = refs[:n_in], refs[n_in:n_in + nc]
        o0 = n_in + nc
        outs, couts = refs[o0:o0 + n_out], refs[o0 + n_out:o0 + n_out + nc]
        s0 = o0 + n_out + nc
        scr, csems = refs[s0:s0 + n_scr], refs[s0 + n_scr:]
        step = pl.program_id(0)
        for ax in range(1, len(grid)):
            step = step * grid[ax] + pl.program_id(ax)

        @pl.when(step == 0)
        def _():
            comm.start(cins, couts, csems)
        body(*ins, *outs, *scr)
        if advance_at is not None:
            @pl.when(step == advance_at)
            def _():
                comm.advance(cins, couts, csems)

        @pl.when(step == n_steps - 1)
        def _():
            if advance_at is None:
                comm.advance(cins, couts, csems)
            comm.finish(cins, couts, csems)

    res = pl.pallas_call(
        hosted, name=name, grid=grid, in_specs=list(in_specs) + [ANY] * nc, out_specs=list(out_specs) + [ANY] * nc,
        out_shape=list(out_shape) + comm.out_shape, scratch_shapes=scratch_shapes + comm.scratch,
        compiler_params=params)(*args, *comm.arrs)
    return res[:n_out], res[n_out:]


def add_core_side(parts, from_sibling, core, name):
    _, rows, cols = parts.shape
    tr = _row_tile(rows, 512)

    def body(c_ref, a_ref, b_ref, o_ref):
        o_ref[...] = (a_ref[...].astype(F32) + b_ref[...].astype(F32)).astype(o_ref.dtype)

    side = pl.BlockSpec((None, tr, cols), lambda q, i, c: (q, i, 0))
    return pl.pallas_call(
        body, name=name,
        grid_spec=pltpu.PrefetchScalarGridSpec(
            num_scalar_prefetch=1, grid=(4, rows // tr),
            in_specs=[pl.BlockSpec((None, tr, cols), lambda q, i, c: (2 * q + c[0], i, 0)), side], out_specs=side),
        out_shape=_sds(from_sibling.shape, BF), compiler_params=_params(2))(core, parts, from_sibling)


def adamw(w, m, v, pieces, name):
    rows, cols = w.shape
    tr = _row_tile(rows, 256) if rows % 16 == 0 else rows
    np_ = len(pieces)
    c1 = 1.0 / (1.0 - ADAM_B1 ** ADAM_STEP)
    c2 = 1.0 / (1.0 - ADAM_B2 ** ADAM_STEP)

    def body(*refs):
        w_ref, m_ref, v_ref = refs[:3]
        p_refs = refs[3:3 + np_]
        g_ref, d_ref, nm_ref, nv_ref = refs[3 + np_:]
        g = p_refs[0][...].astype(F32)
        for pr in p_refs[1:]:
            g = g + pr[...].astype(F32)
        nm = ADAM_B1 * m_ref[...] + (1.0 - ADAM_B1) * g
        nv = ADAM_B2 * v_ref[...] + (1.0 - ADAM_B2) * (g * g)
        g_ref[...] = g
        nm_ref[...] = nm
        nv_ref[...] = nv
        d_ref[...] = -ADAM_LR * ((nm * c1) / (jnp.sqrt(nv * c2) + ADAM_EPS) + ADAM_WD * w_ref[...])

    spec = pl.BlockSpec((tr, cols), lambda i: (i, 0))
    p_specs = []
    for arr, k in pieces:
        if k is None:
            p_specs.append(spec)
        else:
            p_specs.append(pl.BlockSpec((None, tr, cols), functools.partial(lambda i, kk: (kk, i, 0), kk=k)))
    out = _sds(w.shape, F32)
    return pl.pallas_call(body, name=name, grid=(rows // tr,), in_specs=[spec] * 3 + p_specs, out_specs=[spec] * 4,
                          out_shape=[out] * 4, compiler_params=_params(1))(w, m, v, *[a for a, _ in pieces])


OWN_CHIP = "own chip"


def adamw_layers(w, m, v, pieces, chip, name):
    _, rows, cols = w.shape
    tr = _row_tile(rows, 256)
    nt = rows // tr
    counts = [len(pieces[0]), len(pieces[1])]
    c1 = 1.0 / (1.0 - ADAM_B1 ** ADAM_STEP)
    c2 = 1.0 / (1.0 - ADAM_B2 ** ADAM_STEP)

    def body(chip_ref, *refs):
        w_ref, m_ref, v_ref = refs[:3]
        p_refs = refs[3:3 + sum(counts)]
        g_ref, d_ref, nm_ref, nv_ref = refs[3 + sum(counts):]
        sums = []
        for group in (p_refs[:counts[0]], p_refs[counts[0]:]):
            s = group[0][...].astype(F32)
            for pr in group[1:]:
                s = s + pr[...].astype(F32)
            sums.append(s)
        g = jnp.where(pl.program_id(0) == 0, sums[0], sums[1])
        nm = ADAM_B1 * m_ref[...] + (1.0 - ADAM_B1) * g
        nv = ADAM_B2 * v_ref[...] + (1.0 - ADAM_B2) * (g * g)
        g_ref[...] = g
        nm_ref[...] = nm
        nv_ref[...] = nv
        d_ref[...] = -ADAM_LR * ((nm * c1) / (jnp.sqrt(nv * c2) + ADAM_EPS) + ADAM_WD * w_ref[...])

    def rows_of(layer):
        parked = nt - 1 if layer == 0 else 0
        return lambda l, i: jnp.where(l == layer, i, parked)

    spec = pl.BlockSpec((None, tr, cols), lambda l, i, c: (l, i, 0))
    p_specs, p_args = [], []
    for layer in (0, 1):
        row_of = rows_of(layer)
        for arr, k in pieces[layer]:
            p_args.append(arr)
            if k is None:
                p_specs.append(pl.BlockSpec((tr, cols), functools.partial(lambda l, i, c, f: (f(l, i), 0), f=row_of)))
            elif k == OWN_CHIP:
                p_specs.append(pl.BlockSpec((None, tr, cols), functools.partial(lambda l, i, c, f: (c[0], f(l, i), 0), f=row_of)))
            else:
                p_specs.append(pl.BlockSpec((None, tr, cols), functools.partial(lambda l, i, c, f, kk: (kk, f(l, i), 0), f=row_of, kk=k)))
    out = _sds(w.shape, F32)
    return pl.pallas_call(
        body, name=name,
        grid_spec=pltpu.PrefetchScalarGridSpec(num_scalar_prefetch=1, grid=(2, nt), in_specs=[spec] * 3 + p_specs,
                                               out_specs=[spec] * 4),
        out_shape=[out] * 4, compiler_params=_params(2))(chip, w, m, v, *p_args)


def sum_slabs(g, name):
    n, rows, cols = g.shape

    def body(g_ref, o_ref):
        s = g_ref[0]
        for k in range(1, n):
            s = s + g_ref[k]
        o_ref[...] = s

    return pl.pallas_call(body, name=name, out_shape=_sds((rows, cols), F32))(g)


def norm_proj(h, g, w, name, transposed=False, comm=None):
    s_len = h.shape[0]
    nb, tn = (w.shape[0], w.shape[1]) if transposed else (w.shape[0], w.shape[2])
    tm = min(TMP, s_len)
    nt = s_len // tm
    matmul = _dot_nt if transposed else _dot

    def body(h_ref, g_ref, w_ref, o_ref, hn_ref, hn_s):
        rows = pl.ds(pl.multiple_of(pl.program_id(1) * tm, tm), tm)

        @pl.when(pl.program_id(0) == 0)
        def _():
            x = h_ref[...]
            hn = (x * _rstd(x) * g_ref[...]).astype(BF)
            hn_s[rows, :] = hn
            hn_ref[...] = hn
        o_ref[...] = matmul(hn_s[rows, :], w_ref[...]).astype(BF)

    def first_pass_rows(j, i):
        return (jnp.where(j == 0, i, nt - 1), 0)

    return _pcall(
        body, (h, g, w), name=name, grid=(nb, nt),
        in_specs=[pl.BlockSpec((tm, D), first_pass_rows), pl.BlockSpec((1, D), lambda j, i: (0, 0)),
                  pl.BlockSpec((None,) + w.shape[1:], lambda j, i: (j, 0, 0))],
        out_specs=[pl.BlockSpec((tm, tn), lambda j, i: (i, j)), pl.BlockSpec((tm, D), first_pass_rows)],
        out_shape=[_sds((s_len, nb * tn), BF), _sds((s_len, D), BF)],
        scratch_shapes=[pltpu.VMEM((s_len, D), BF)], comm=comm)


def _sgu_mix(ws_ref, vln_s, mix_s, bs_ref, tm, transposed):
    mask = _sgu_mask(transposed)
    for hd in range(HEADS):
        wm = jnp.where(mask, ws_ref[hd], 0.0).astype(BF)
        cs = slice(hd * BLK, (hd + 1) * BLK)
        for n in range(tm // BLK):
            rs = slice(n * BLK, (n + 1) * BLK)
            r = _dot(wm, vln_s[rs, cs])
            mix_s[rs, cs] = r if bs_ref is None else r + bs_ref[:, cs]


def sgu_fwd(proj, ws, bsfull, gv, bv, wo, name):
    s_len = proj.shape[0]
    tm = min(TM, s_len)

    def body(zu_ref, zv_ref, ws_ref, bs_ref, gv_ref, bv_ref, wo_ref, mix_ref, sgu_ref, br_ref, vln_s, mix_s):
        u, _ = _gelu(zu_ref[...].astype(F32))
        v, _ = _gelu(zv_ref[...].astype(F32))
        vhat, _ = _ln_fwd(v)
        vln_s[...] = (vhat * gv_ref[...] + bv_ref[...]).astype(BF)
        _sgu_mix(ws_ref, vln_s, mix_s, bs_ref, tm, False)
        mixed = mix_s[...].astype(BF)
        mix_ref[...] = mixed
        sgu = (u * mixed.astype(F32)).astype(BF)
        sgu_ref[...] = sgu
        br_ref[...] = _dot(sgu, wo_ref[...].reshape(D, D)).astype(BF)

    return pl.pallas_call(
        body, name=name, grid=(s_len // tm,),
        in_specs=[_cols(tm, D, 0), _cols(tm, D, 1), _whole((HEADS, BLK, BLK)), _whole((BLK, D)), _whole((1, D)), _whole((1, D)),
                  _rowsharded(BLK)],
        out_specs=[_cols(tm, D, 0)] * 3, out_shape=[_sds((s_len, D), BF)] * 3,
        scratch_shapes=[pltpu.VMEM((tm, D), BF), pltpu.VMEM((tm, D), F32)], compiler_params=_params(1),
    )(proj, proj, ws, bsfull, gv, bv, wo)


def _causal_conv(ext_s, out_s, wdw_ref, bias_ref, tm):
    def chunk(ci, carry):
        r0 = pl.multiple_of((ci // (D // LC)) * RC, RC)
        l0 = pl.multiple_of((ci % (D // LC)) * LC, LC)
        win = ext_s[pl.ds(r0, RC + HALO), pl.ds(l0, LC)]
        acc = jnp.broadcast_to(bias_ref[:, pl.ds(l0, LC)], (RC, LC))
        for r in range(8):
            wr = win if r == 0 else pltpu.roll(win, r, 0)
            for m in range(4):
                d = 8 * m + r
                if d < CONV_W:
                    k = CONV_W - 1 - d
                    acc = acc + wdw_ref[k:k + 1, pl.ds(l0, LC)] * wr[HALO - 8 * m:HALO - 8 * m + RC]
        out_s[pl.ds(r0, RC), pl.ds(l0, LC)] = acc
        return carry
    lax.fori_loop(0, (tm // RC) * (D // LC), chunk, 0)


def _glu_ext(a_ref, g_ref, ah_ref, gh_ref, ext_s, first):
    hh = ah_ref[...].astype(F32) * _sig(gh_ref[...].astype(F32))
    ext_s[0:HALO, :] = jnp.where(first, 0.0, hh)
    ext_s[HALO:, :] = a_ref[...].astype(F32) * _sig(g_ref[...].astype(F32))


def conv_fwd(proj, wdw, bdw, gln, bln, wo, name, comm=None):
    s_len = proj.shape[0]
    tm = min(TM, s_len)

    def body(a_ref, g_ref, ah_ref, gh_ref, wdw_ref, bdw_ref, gln_ref, bln_ref, wo_ref, cv_ref, cb_ref, br_ref, ext_s, conv_s):
        _glu_ext(a_ref, g_ref, ah_ref, gh_ref, ext_s, pl.program_id(0) == 0)
        _causal_conv(ext_s, conv_s, wdw_ref, bdw_ref, tm)
        cv = conv_s[...].astype(BF)
        cv_ref[...] = cv
        chat, _ = _ln_fwd(cv.astype(F32))
        yl = chat * gln_ref[...] + bln_ref[...]
        cb = (yl * _sig(yl)).astype(BF)
        cb_ref[...] = cb
        br_ref[...] = _dot(cb, wo_ref[...].reshape(D, D)).astype(BF)

    return _pcall(
        body, (proj, proj, proj, proj, wdw, bdw, gln, bln, wo), name=name, grid=(s_len // tm,),
        in_specs=[_cols(tm, D, 2), _cols(tm, D, 3), _prev_halo(tm, D, 2), _prev_halo(tm, D, 3), _whole((HALO, D)),
                  _whole((1, D)), _whole((1, D)), _whole((1, D)), _rowsharded(BLK)],
        out_specs=[_cols(tm, D, 0)] * 3, out_shape=[_sds((s_len, D), BF)] * 3,
        scratch_shapes=[pltpu.VMEM((tm + HALO, D), F32), pltpu.VMEM((tm, D), F32)], comm=comm)


def pool_fwd(proj, wpool, spool, wo, name):
    s_len = proj.shape[0]
    tm = min(TM, s_len)

    def body(z_ref, zh_ref, wp_ref, sp_ref, wo_ref, pooled_ref, pm_ref, br_ref, ext_s, mr_s):
        i = pl.program_id(0)
        ext_s[0:HALO, :] = jnp.where(i == 0, 0.0, zh_ref[...].astype(F32))
        ext_s[HALO:, :] = z_ref[...].astype(F32)
        for gi, w in enumerate(POOL_WINDOWS):
            cs = slice(gi * POOL_GD, (gi + 1) * POOL_GD)
            e = ext_s[:, cs]
            s = e
            sh = 1
            while sh < w:
                s = s + pltpu.roll(s, sh, 0)
                sh *= 2
            pooled = (s[HALO:] * _inv_count(i, tm, w) - e[HALO:]).astype(BF)
            pooled_ref[:, cs] = pooled
            mr_s[:, cs] = _dot(pooled, wp_ref[gi])
        pm = (mr_s[...] * sp_ref[...]).astype(BF)
        pm_ref[...] = pm
        br_ref[...] = _dot(pm, wo_ref[...].reshape(D, D)).astype(BF)

    return pl.pallas_call(
        body, name=name, grid=(s_len // tm,),
        in_specs=[_cols(tm, D, 4), _prev_halo(tm, D, 4), _whole((4, POOL_GD, POOL_GD)), _whole((1, D)), _rowsharded(BLK)],
        out_specs=[_cols(tm, D, 0)] * 3, out_shape=[_sds((s_len, D), BF)] * 3,
        scratch_shapes=[pltpu.VMEM((tm + HALO, D), F32), pltpu.VMEM((tm, D), F32)], compiler_params=_params(1),
    )(proj, proj, wpool, spool, wo)


def merge_out(proj, bra, brb, brc, h, wout, gpost, name, comm=None):
    s_len = h.shape[0]
    tm = min(TM, s_len)

    def body(z0, z1, z2, a_ref, b_ref, c_ref, h_ref, wo_ref, g_ref, mg_ref, mo_ref, h1_ref):
        merged = (_sig(z0[...].astype(F32)) * a_ref[...].astype(F32) + _sig(z1[...].astype(F32)) * b_ref[...].astype(F32)
                  + _sig(z2[...].astype(F32)) * c_ref[...].astype(F32)).astype(BF)
        mg_ref[...] = merged
        mo = _dot(merged, wo_ref[...].reshape(D, D))
        mo_ref[...] = mo.astype(BF)
        h1_ref[...] = h_ref[...] + mo * _rstd(mo) * g_ref[...]

    row = _cols(tm, D, 0)
    return _pcall(
        body, (proj, proj, proj, bra, brb, brc, h, wout, gpost), name=name, grid=(s_len // tm,),
        in_specs=[_cols(tm, D, 5), _cols(tm, D, 6), _cols(tm, D, 7), row, row, row, row, _rowsharded(BLK), _whole((1, D))],
        out_specs=[row] * 3, out_shape=[_sds((s_len, D), BF), _sds((s_len, D), BF), _sds((s_len, D), F32)], comm=comm)


def _p_spec(tm, layer):
    return pl.BlockSpec((None, None, tm, PLE), lambda i: (layer, 0, i, 0))


def ffn_out(ff, h1, p, wfo, gpost, wpg, wple, layer, name, comm=None):
    s_len = h1.shape[0]
    tm = min(TMB, s_len)

    def body(fg_ref, fu_ref, h1_ref, p_ref, wfo_ref, g_ref, wpg_ref, wple_ref, act_ref, f_ref, h2_ref, pg_ref, h3_ref):
        gt = fg_ref[...].astype(F32)
        act = (gt * _sig(gt) * fu_ref[...].astype(F32)).astype(BF)
        act_ref[...] = act
        f = _dot(act, wfo_ref[...].reshape(D_FF, D))
        f_ref[...] = f.astype(BF)
        h2 = h1_ref[...] + f * _rstd(f) * g_ref[...]
        h2_ref[...] = h2
        pg = _dot(h2.astype(BF), wpg_ref[...].reshape(D, D)).astype(BF)
        pg_ref[...] = pg
        pe = _dot(p_ref[...].astype(BF), wple_ref[...])
        h3_ref[...] = h2 + _sig(pg.astype(F32)) * pe

    row = _cols(tm, D, 0)
    return _pcall(
        body, (ff, ff, h1, p, wfo, gpost, wpg, wple), name=name, grid=(s_len // tm,),
        in_specs=[_cols(tm, D_FF, 0), _cols(tm, D_FF, 1), row, _p_spec(tm, layer), _rowsharded(D_FF // N_DEV),
                  _whole((1, D)), _rowsharded(BLK), _whole((PLE, D))],
        out_specs=[_cols(tm, D_FF, 0), row, row, row, row],
        out_shape=[_sds((s_len, D_FF), BF), _sds((s_len, D), BF), _sds((s_len, D), F32), _sds((s_len, D), BF), _sds((s_len, D), F32)],
        comm=comm)


def ple_ffn_bwd(dh3, pg, p, f, ff, wple, wpg, wfo, gpost, layer, name, target=None, comm=None):
    s_len = dh3.shape[0]
    tm = min(TMW, s_len)
    nt = s_len // tm
    with_loss = target is not None

    def body(*refs):
        if with_loss:
            t_ref, refs, loss_ref, loss_acc = refs[0], refs[1:-2], refs[-2], refs[-1]
        (dh3_ref, pg_ref, p_ref, f_ref, fg_ref, fu_ref, wple_ref, wpg_ref, wfo_ref, g_ref,
         dh2_ref, dpe_ref, dpg_ref, df_ref, dff_ref, dg_ref) = refs
        i = pl.program_id(0)
        dh3v = dh3_ref[...]
        if with_loss:
            err = dh3v - t_ref[...]
            dh3v = err * (1.0 / D)
            _accum(loss_acc, i == 0, _colsum(err * err))

            @pl.when(i == nt - 1)
            def _():
                loss_ref[...] = jnp.broadcast_to(jnp.sum(loss_acc[...], axis=1, keepdims=True) * (0.5 / D), (1, LC))
        s = _sig(pg_ref[...].astype(F32))
        pe = _dot(p_ref[...].astype(BF), wple_ref[...])
        dpe_ref[...] = (dh3v * s).astype(BF)
        dpg = (dh3v * pe * s * (1.0 - s)).astype(BF)
        dpg_ref[...] = dpg
        dh2 = dh3v + _dot_nt(dpg, wpg_ref[...].reshape(D, D))
        dh2_ref[...] = dh2
        fv = f_ref[...].astype(F32)
        r = _rstd(fv)
        _accum(dg_ref, i == 0, _colsum(dh2 * fv * r))
        df = _rms_bwd(fv, dh2 * g_ref[...], r).astype(BF)
        df_ref[...] = df
        dact = _dot_nt(df, wfo_ref[...].reshape(D_FF, D))
        gt = fg_ref[...].astype(F32)
        sg = _sig(gt)
        up = fu_ref[...].astype(F32)
        dff_ref[:, 0:D_FF] = (dact * up * sg * (1.0 + gt * (1.0 - sg))).astype(BF)
        dff_ref[:, D_FF:2 * D_FF] = (dact * gt * sg).astype(BF)

    row = _cols(tm, D, 0)
    args = (dh3, pg, p, f, ff, ff, wple, wpg, wfo, gpost)
    in_specs = [row, row, _p_spec(tm, layer), row, _cols(tm, D_FF, 0), _cols(tm, D_FF, 1), _whole((PLE, D)),
                _rowsharded(BLK), _rowsharded(D_FF // N_DEV), _whole((1, D))]
    out_specs = [row, row, row, row, _cols(tm, 2 * D_FF, 0), _whole((1, D))]
    out_shape = [_sds((s_len, D), F32), _sds((s_len, D), BF), _sds((s_len, D), BF), _sds((s_len, D), BF),
                 _sds((s_len, 2 * D_FF), BF), _sds((1, D), F32)]
    scratch = []
    if with_loss:
        args, in_specs = (target,) + args, [row] + in_specs
        out_specs, out_shape = out_specs + [_whole((1, LC))], out_shape + [_sds((1, LC), F32)]
        scratch = [pltpu.VMEM((1, D), F32)]
    return _pcall(body, args, name=name, grid=(nt,), in_specs=in_specs, out_specs=out_specs, out_shape=out_shape,
                  scratch_shapes=scratch, comm=comm)


def ffn_in_bwd(dff, wt, h1, dh2, gpre, name, comm=None):
    s_len = h1.shape[0]
    tm = min(TMW, s_len)
    nb, tn, _ = wt.shape

    def body(dff_ref, w_ref, h1_ref, dh2_ref, g_ref, dh1_ref, dg_ref):
        dhn = _dot(dff_ref[:, 0:tn], w_ref[0])
        for j in range(1, nb):
            dhn = dhn + _dot(dff_ref[:, j * tn:(j + 1) * tn], w_ref[j])
        x = h1_ref[...]
        r = _rstd(x)
        _accum(dg_ref, pl.program_id(0) == 0, _colsum(dhn * x * r))
        dh1_ref[...] = dh2_ref[...] + _rms_bwd(x, dhn * g_ref[...], r)

    row = _cols(tm, D, 0)
    return _pcall(
        body, (dff, wt, h1, dh2, gpre), name=name, grid=(s_len // tm,),
        in_specs=[_cols(tm, nb * tn, 0), pl.BlockSpec((nb, tn, D), lambda i: (0, 0, 0), pipeline_mode=pl.Buffered(1)), row, row,
                  _whole((1, D))],
        out_specs=[row, _whole((1, D))],
        out_shape=[_sds((s_len, D), F32), _sds((1, D), F32)], comm=comm)


def mix_post_bwd(dh1, mo, gpost, wout, proj, bra, brb, brc, win, name, comm=None):
    s_len = dh1.shape[0]
    tm = min(TMB, s_len)

    def body(dh1_ref, mo_ref, g_ref, wo_ref, z0, z1, z2, a_ref, b_ref, c_ref, w5, w6, w7,
             dmo_ref, da_ref, db_ref, dc_ref, dz_ref, dhn_ref, dg_ref):
        i = pl.program_id(0)
        dh1v = dh1_ref[...]
        mo_v = mo_ref[...].astype(F32)
        r = _rstd(mo_v)
        _accum(dg_ref, i == 0, _colsum(dh1v * mo_v * r))
        dmo = _rms_bwd(mo_v, dh1v * g_ref[...], r).astype(BF)
        dmo_ref[...] = dmo
        dmerged = _dot_nt(dmo, wo_ref[...].reshape(D, D))
        dhn = jnp.zeros((tm, D), F32)
        for k, (z, br, dbr, w) in enumerate(((z0, a_ref, da_ref, w5), (z1, b_ref, db_ref, w6), (z2, c_ref, dc_ref, w7))):
            s = _sig(z[...].astype(F32))
            dbr[...] = (dmerged * s).astype(BF)
            dz = (dmerged * br[...].astype(F32) * s * (1.0 - s)).astype(BF)
            dz_ref[:, k * D:(k + 1) * D] = dz
            dhn = dhn + _dot_nt(dz, w[...])
        dhn_ref[...] = dhn

    row = _cols(tm, D, 0)
    return _pcall(
        body, (dh1, mo, gpost, wout, proj, proj, proj, bra, brb, brc, win, win, win), name=name, grid=(s_len // tm,),
        in_specs=[row, row, _whole((1, D)), _rowsharded(BLK), _cols(tm, D, 5), _cols(tm, D, 6), _cols(tm, D, 7), row, row, row,
                  _win_block(5), _win_block(6), _win_block(7)],
        out_specs=[row, row, row, row, _cols(tm, 3 * D, 0), row, _whole((1, D))],
        out_shape=[_sds((s_len, D), BF)] * 4 + [_sds((s_len, 3 * D), BF), _sds((s_len, D), F32), _sds((1, D), F32)], comm=comm)


def sgu_bwd(dbr, wo, proj, mixed, wst, gv, bv, win, dhn_in, name, comm=None):
    s_len = dbr.shape[0]
    tm = min(TMB, s_len)
    nt = s_len // tm

    def body(dbr_ref, wo_ref, zu_ref, zv_ref, mix_ref, wst_ref, gv_ref, bv_ref, w0, w1, dhn_in_ref,
             dz_ref, dhn_ref, dws_ref, dbs_ref, dgv_ref, dbv_ref, vln_s, dmix_s, dvln_s, bs_acc):
        i = pl.program_id(0)
        first = i == 0
        dsgu = _dot_nt(dbr_ref[...], wo_ref[...].reshape(D, D))
        zu = zu_ref[...].astype(F32)
        zv = zv_ref[...].astype(F32)
        u, tu = _gelu(zu)
        v, tv = _gelu(zv)
        vhat, rs = _ln_fwd(v)
        vln_s[...] = (vhat * gv_ref[...] + bv_ref[...]).astype(BF)
        du = dsgu * mix_ref[...].astype(F32)
        dmix = dsgu * u
        dmix_s[...] = dmix.astype(BF)
        blocks = dmix[0:BLK]
        for n in range(1, tm // BLK):
            blocks = blocks + dmix[n * BLK:(n + 1) * BLK]
        _accum(bs_acc, first, blocks)
        for hd in range(HEADS):
            cs = slice(hd * BLK, (hd + 1) * BLK)
            g = _dot_nt(dmix_s[0:BLK, cs], vln_s[0:BLK, cs])
            for n in range(1, tm // BLK):
                g = g + _dot_nt(dmix_s[n * BLK:(n + 1) * BLK, cs], vln_s[n * BLK:(n + 1) * BLK, cs])

            @pl.when(first)
            def _():
                dws_ref[hd] = g

            @pl.when(jnp.logical_not(first))
            def _():
                dws_ref[hd] += g
        _sgu_mix(wst_ref, dmix_s, dvln_s, None, tm, True)
        dvln = dvln_s[...]
        _accum(dgv_ref, first, _colsum(dvln * vhat))
        _accum(dbv_ref, first, _colsum(dvln))
        dv = _ln_bwd(dvln * gv_ref[...], vhat, rs)
        dzu = (du * _gelu_grad(zu, tu)).astype(BF)
        dzv = (dv * _gelu_grad(zv, tv)).astype(BF)
        dz_ref[:, 0:D] = dzu
        dz_ref[:, D:2 * D] = dzv
        dhn_ref[...] = dhn_in_ref[...] + _dot_nt(dzu, w0[...]) + _dot_nt(dzv, w1[...])

        @pl.when(i == nt - 1)
        def _():
            mask = _sgu_mask(False)
            for hd in range(HEADS):
                dws_ref[hd] = jnp.where(mask, dws_ref[hd], 0.0)
                dbs_ref[:, hd:hd + 1] = jnp.sum(bs_acc[:, hd * BLK:(hd + 1) * BLK], axis=1, keepdims=True)

    row = _cols(tm, D, 0)
    vec = _whole((1, D))
    return _pcall(
        body, (dbr, wo, proj, proj, mixed, wst, gv, bv, win, win, dhn_in), name=name, grid=(nt,),
        in_specs=[row, _rowsharded(BLK), _cols(tm, D, 0), _cols(tm, D, 1), row, _whole((HEADS, BLK, BLK)),
                  vec, vec, _win_block(0), _win_block(1), row],
        out_specs=[_cols(tm, 2 * D, 0), row, _whole((HEADS, BLK, BLK)), _whole((BLK, HEADS)), vec, vec],
        out_shape=[_sds((s_len, 2 * D), BF), _sds((s_len, D), F32), _sds((HEADS, BLK, BLK), F32), _sds((BLK, HEADS), F32),
                   _sds((1, D), F32), _sds((1, D), F32)],
        scratch_shapes=[pltpu.VMEM((tm, D), BF), pltpu.VMEM((tm, D), BF), pltpu.VMEM((tm, D), F32), pltpu.VMEM((BLK, D), F32)],
        comm=comm)


def conv_bwd(dbr, wo, proj, conv, gln, bln, name, comm=None):
    s_len = dbr.shape[0]
    tm = min(TMB, s_len)
    nt = s_len // tm

    def body(dbr_ref, wo_ref, a_ref, g_ref, ah_ref, gh_ref, cv_ref, gln_ref, bln_ref,
             dc_ref, dw_ref, dbdw_ref, dgln_ref, dbln_ref, ext_s, dc_s, dw_acc):
        i = pl.program_id(0)
        first = i == 0
        dcb = _dot_nt(dbr_ref[...], wo_ref[...].reshape(D, D))
        _glu_ext(a_ref, g_ref, ah_ref, gh_ref, ext_s, first)
        chat, rs = _ln_fwd(cv_ref[...].astype(F32))
        yl = chat * gln_ref[...] + bln_ref[...]
        sy = _sig(yl)
        dyl = dcb * sy * (1.0 + yl * (1.0 - sy))
        _accum(dgln_ref, first, _colsum(dyl * chat))
        _accum(dbln_ref, first, _colsum(dyl))
        dc = _ln_bwd(dyl * gln_ref[...], chat, rs)
        _accum(dbdw_ref, first, _colsum(dc))
        dc_ref[...] = dc.astype(BF)
        dc_s[...] = dc

        @pl.when(first)
        def _():
            dw_acc[...] = jnp.zeros_like(dw_acc)

        def chunk(ci, carry):
            r0 = pl.multiple_of((ci // (D // LC)) * RC, RC)
            l0 = pl.multiple_of((ci % (D // LC)) * LC, LC)
            win = ext_s[pl.ds(r0, RC + HALO), pl.ds(l0, LC)]
            dcw = dc_s[pl.ds(r0, RC), pl.ds(l0, LC)]
            for r in range(8):
                wr = win if r == 0 else pltpu.roll(win, r, 0)
                for m in range(4):
                    d = 8 * m + r
                    if d < CONV_W:
                        k = CONV_W - 1 - d
                        prod = dcw * wr[HALO - 8 * m:HALO - 8 * m + RC]
                        dw_acc[k * 8:(k + 1) * 8, pl.ds(l0, LC)] += prod.reshape(RC // 8, 8, LC).sum(axis=0)
            return carry
        lax.fori_loop(0, (tm // RC) * (D // LC), chunk, 0)

        @pl.when(i == nt - 1)
        def _():
            dw_ref[...] = dw_acc[...].reshape(HALO, 8, D).sum(axis=1)

    row = _cols(tm, D, 0)
    vec = _whole((1, D))
    return _pcall(
        body, (dbr, wo, proj, proj, proj, proj, conv, gln, bln), name=name, grid=(nt,),
        in_specs=[row, _rowsharded(BLK), _cols(tm, D, 2), _cols(tm, D, 3), _prev_halo(tm, D, 2), _prev_halo(tm, D, 3),
                  row, vec, vec],
        out_specs=[row, _whole((HALO, D)), vec, vec, vec],
        out_shape=[_sds((s_len, D), BF), _sds((HALO, D), F32), _sds((1, D), F32), _sds((1, D), F32), _sds((1, D), F32)],
        scratch_shapes=[pltpu.VMEM((tm + HALO, D), F32), pltpu.VMEM((tm, D), F32), pltpu.VMEM((HALO * 8, D), F32)], comm=comm)


def pool_bwd(dbr, wo, pooled, wpool, spool, name):
    s_len = dbr.shape[0]
    tm = min(TMB, s_len)

    def body(dbr_ref, wo_ref, pl_ref, wp_ref, sp_ref, dmr_ref, q_ref, dsp_ref, mr_s):
        i = pl.program_id(0)
        dpm = _dot_nt(dbr_ref[...], wo_ref[...].reshape(D, D))
        for gi in range(4):
            cs = slice(gi * POOL_GD, (gi + 1) * POOL_GD)
            mr_s[:, cs] = _dot(pl_ref[:, cs], wp_ref[gi])
        _accum(dsp_ref, i == 0, _colsum(dpm * mr_s[...]))
        dmr = (dpm * sp_ref[...]).astype(BF)
        dmr_ref[...] = dmr
        for gi, w in enumerate(POOL_WINDOWS):
            cs = slice(gi * POOL_GD, (gi + 1) * POOL_GD)
            q_ref[:, cs] = (_dot_nt(dmr[:, cs], wp_ref[gi]) * _inv_count(i, tm, w)).astype(BF)

    row = _cols(tm, D, 0)
    return pl.pallas_call(
        body, name=name, grid=(s_len // tm,),
        in_specs=[row, _rowsharded(BLK), row, _whole((4, POOL_GD, POOL_GD)), _whole((1, D))],
        out_specs=[row, row, _whole((1, D))],
        out_shape=[_sds((s_len, D), BF), _sds((s_len, D), BF), _sds((1, D), F32)],
        scratch_shapes=[pltpu.VMEM((tm, D), F32)], compiler_params=_params(1))(dbr, wo, pooled, wpool, spool)


def seq_bwd(dc, q, proj, wdw, win, dhn_in, h, dh1, gpre, name, comm=None):
    s_len = dc.shape[0]
    tm = min(TMB, s_len)
    nt = s_len // tm

    def body(dc_ref, dch_ref, q_ref, qh_ref, a_ref, g_ref, wdw_ref, w2, w3, w4, dhn_in_ref, h_ref, dh1_ref, gpre_ref,
             dz_ref, dh_ref, dgpre_ref, ext_s, dhc_s, qext_s):
        i = pl.program_id(0)
        last = i == nt - 1
        ext_s[0:tm, :] = dc_ref[...].astype(F32)
        ext_s[tm:, :] = jnp.where(last, 0.0, dch_ref[...].astype(F32))

        def chunk(ci, carry):
            r0 = pl.multiple_of((ci // (D // LC)) * RC, RC)
            l0 = pl.multiple_of((ci % (D // LC)) * LC, LC)
            win_ = ext_s[pl.ds(r0, RC + HALO), pl.ds(l0, LC)]
            acc = jnp.zeros((RC, LC), F32)
            for r in range(8):
                wr = win_ if r == 0 else pltpu.roll(win_, RC + HALO - r, 0)
                for m in range(4):
                    d = 8 * m + r
                    if d < CONV_W:
                        k = CONV_W - 1 - d
                        acc = acc + wdw_ref[k:k + 1, pl.ds(l0, LC)] * wr[8 * m:8 * m + RC]
            dhc_s[pl.ds(r0, RC), pl.ds(l0, LC)] = acc
            return carry
        lax.fori_loop(0, (tm // RC) * (D // LC), chunk, 0)

        dhc = dhc_s[...]
        av = a_ref[...].astype(F32)
        sg = _sig(g_ref[...].astype(F32))
        da = (dhc * sg).astype(BF)
        dg = (dhc * av * sg * (1.0 - sg)).astype(BF)
        dz_ref[:, 0:D] = da
        dz_ref[:, D:2 * D] = dg

        qext_s[0:tm, :] = q_ref[...].astype(F32)
        qext_s[tm:, :] = jnp.where(last, 0.0, qh_ref[...].astype(F32))
        for gi, w in enumerate(POOL_WINDOWS):
            cs = slice(gi * POOL_GD, (gi + 1) * POOL_GD)
            e = qext_s[:, cs]
            s = e
            sh = 1
            while sh < w:
                s = s + pltpu.roll(s, tm + HALO - sh, 0)
                sh *= 2
            dz_ref[:, 2 * D + gi * POOL_GD:2 * D + (gi + 1) * POOL_GD] = (s[0:tm] - e[0:tm] * _count(i, tm, w)).astype(BF)
        dhn = dhn_in_ref[...] + _dot_nt(da, w2[...]) + _dot_nt(dg, w3[...]) + _dot_nt(dz_ref[:, 2 * D:3 * D], w4[...])
        x = h_ref[...]
        r = _rstd(x)
        _accum(dgpre_ref, i == 0, _colsum(dhn * x * r))
        dh_ref[...] = dh1_ref[...] + _rms_bwd(x, dhn * gpre_ref[...], r)

    row = _cols(tm, D, 0)
    return _pcall(
        body, (dc, dc, q, q, proj, proj, wdw, win, win, win, dhn_in, h, dh1, gpre), name=name, grid=(nt,),
        in_specs=[row, _next_halo(tm, D, 0, s_len), row, _next_halo(tm, D, 0, s_len), _cols(tm, D, 2), _cols(tm, D, 3),
                  _whole((HALO, D)), _win_block(2), _win_block(3), _win_block(4), row, row, row, _whole((1, D))],
        out_specs=[_cols(tm, 3 * D, 0), row, _whole((1, D))],
        out_shape=[_sds((s_len, 3 * D), BF), _sds((s_len, D), F32), _sds((1, D), F32)],
        scratch_shapes=[pltpu.VMEM((tm + HALO, D), F32), pltpu.VMEM((tm, D), F32), pltpu.VMEM((tm + HALO, D), F32)], comm=comm)


def wgrad(a, b, tk, tn, name, stacked=False, diag=False, a_spec=None, comm=None):
    s_len = b.shape[0]
    k_dim = a.shape[-1]
    n_dim = b.shape[1]
    ts = min(TS, s_len)
    nk = 1 if diag else k_dim // tk
    nn, ns = n_dim // tn, s_len // ts

    def body(a_ref, b_ref, o_ref, acc):
        s = pl.program_id(2)
        _accum(acc, s == 0, _dot_tn(a_ref[...].astype(BF), b_ref[...].astype(BF)))

        @pl.when(s == ns - 1)
        def _():
            o_ref[...] = acc[...].astype(BF).reshape(o_ref.shape)

    if a_spec is None:
        a_spec = pl.BlockSpec((ts, tk), (lambda k, n, s: (s, n)) if diag else (lambda k, n, s: (s, k)))
    if stacked or diag:
        out_shape = _sds((nn, tk if diag else k_dim, tn), BF)
        o_spec = pl.BlockSpec((1, tk, tn), lambda k, n, s: (n, k, 0))
    else:
        out_shape = _sds((k_dim, n_dim), BF)
        o_spec = pl.BlockSpec((tk, tn), lambda k, n, s: (k, n))
    (out,), got = _pcall(
        body, (a, b), name=name, grid=(nk, nn, ns),
        in_specs=[a_spec, pl.BlockSpec((ts, tn), lambda k, n, s: (s, n))], out_specs=[o_spec], out_shape=[out_shape],
        scratch_shapes=[pltpu.VMEM((tk, tn), F32)], comm=comm)
    return out if comm is None else (out, got)


_WEIGHTS = ['g_mix_pre', 'w_in', 'w_sgu_s', 'b_sgu_s', 'g_sgu_v', 'b_sgu_v', 'w_sgu_out', 'w_dw', 'b_dw', 'g_conv_ln', 'b_conv_ln',
            'w_conv_out', 'w_pool', 's_pool', 'w_pool_out', 'w_out', 'g_mix_post', 'g_ffn_pre', 'w_ffn_in', 'w_ffn_out', 'g_ffn_post',
            'w_ple', 'w_ple_gate']
_SHARDED = ['w_in', 'w_sgu_out', 'w_conv_out', 'w_pool', 'w_pool_out', 'w_out', 'w_ffn_in', 'w_ffn_out', 'w_ple', 'w_ple_gate']
_VECTORS = ['g_mix_pre', 'g_sgu_v', 'b_sgu_v', 'b_dw', 'g_conv_ln', 'b_conv_ln', 's_pool', 'g_mix_post', 'g_ffn_pre', 'g_ffn_post']
_SUBLANES = 8
_SGU_ROWS = HEADS * BLK * BLK // D
_REP_ROWS = _SUBLANES * (len(_VECTORS) + 2) + _SGU_ROWS


def _pack_replicated(t, layer):
    rows = [jnp.pad(t[k][layer].reshape(1, D), ((0, _SUBLANES - 1), (0, 0))) for k in _VECTORS + ['b_sgu_s']]
    return jnp.concatenate(rows + [t['w_sgu_s'][layer].reshape(_SGU_ROWS, D), jnp.zeros((_SUBLANES, D), F32)], axis=0)


def _unpack_replicated(packed):
    out = {}
    for i, k in enumerate(_VECTORS):
        out[k] = packed[:, _SUBLANES * i, :]
    o = _SUBLANES * len(_VECTORS)
    out['b_sgu_s'] = packed[:, o, :].reshape(2, HEADS, BLK)
    out['w_sgu_s'] = packed[:, o + _SUBLANES:o + _SUBLANES + _SGU_ROWS, :].reshape(2, HEADS, BLK, BLK)
    return out


def _pad_taps(w):
    return jnp.pad(w, ((0, HALO - CONV_W), (0, 0)))


def kernel(x, p, g_mix_pre, w_in, w_sgu_s, b_sgu_s, g_sgu_v, b_sgu_v, w_sgu_out, w_dw, b_dw, g_conv_ln, b_conv_ln, w_conv_out, w_pool, s_pool, w_pool_out, w_out, g_mix_post, g_ffn_pre, w_ffn_in, w_ffn_out, g_ffn_post, w_ple, w_ple_gate, loss_target, m_g_mix_pre, m_w_in, m_w_sgu_s, m_b_sgu_s, m_g_sgu_v, m_b_sgu_v, m_w_sgu_out, m_w_dw, m_b_dw, m_g_conv_ln, m_b_conv_ln, m_w_conv_out, m_w_pool, m_s_pool, m_w_pool_out, m_w_out, m_g_mix_post, m_g_ffn_pre, m_w_ffn_in, m_w_ffn_out, m_g_ffn_post, m_w_ple, m_w_ple_gate, v_g_mix_pre, v_w_in, v_w_sgu_s, v_b_sgu_s, v_g_sgu_v, v_b_sgu_v, v_w_sgu_out, v_w_dw, v_b_dw, v_g_conv_ln, v_b_conv_ln, v_w_conv_out, v_w_pool, v_s_pool, v_w_pool_out, v_w_out, v_g_mix_post, v_g_ffn_pre, v_w_ffn_in, v_w_ffn_out, v_g_ffn_post, v_w_ple, v_w_ple_gate):
    W = dict(g_mix_pre=g_mix_pre, w_in=w_in, w_sgu_s=w_sgu_s, b_sgu_s=b_sgu_s, g_sgu_v=g_sgu_v, b_sgu_v=b_sgu_v, w_sgu_out=w_sgu_out,
             w_dw=w_dw, b_dw=b_dw, g_conv_ln=g_conv_ln, b_conv_ln=b_conv_ln, w_conv_out=w_conv_out, w_pool=w_pool, s_pool=s_pool,
             w_pool_out=w_pool_out, w_out=w_out, g_mix_post=g_mix_post, g_ffn_pre=g_ffn_pre, w_ffn_in=w_ffn_in, w_ffn_out=w_ffn_out,
             g_ffn_post=g_ffn_post, w_ple=w_ple, w_ple_gate=w_ple_gate)
    M = dict(g_mix_pre=m_g_mix_pre, w_in=m_w_in, w_sgu_s=m_w_sgu_s, b_sgu_s=m_b_sgu_s, g_sgu_v=m_g_sgu_v, b_sgu_v=m_b_sgu_v,
             w_sgu_out=m_w_sgu_out, w_dw=m_w_dw, b_dw=m_b_dw, g_conv_ln=m_g_conv_ln, b_conv_ln=m_b_conv_ln, w_conv_out=m_w_conv_out,
             w_pool=m_w_pool, s_pool=m_s_pool, w_pool_out=m_w_pool_out, w_out=m_w_out, g_mix_post=m_g_mix_post, g_ffn_pre=m_g_ffn_pre,
             w_ffn_in=m_w_ffn_in, w_ffn_out=m_w_ffn_out, g_ffn_post=m_g_ffn_post, w_ple=m_w_ple, w_ple_gate=m_w_ple_gate)
    V = dict(g_mix_pre=v_g_mix_pre, w_in=v_w_in, w_sgu_s=v_w_sgu_s, b_sgu_s=v_b_sgu_s, g_sgu_v=v_g_sgu_v, b_sgu_v=v_b_sgu_v,
             w_sgu_out=v_w_sgu_out, w_dw=v_w_dw, b_dw=v_b_dw, g_conv_ln=v_g_conv_ln, b_conv_ln=v_b_conv_ln, w_conv_out=v_w_conv_out,
             w_pool=v_w_pool, s_pool=v_s_pool, w_pool_out=v_w_pool_out, w_out=v_w_out, g_mix_post=v_g_mix_post, g_ffn_pre=v_g_ffn_pre,
             w_ffn_in=v_w_ffn_in, w_ffn_out=v_w_ffn_out, g_ffn_post=v_g_ffn_post, w_ple=v_w_ple, w_ple_gate=v_w_ple_gate)

    my_c = lax.axis_index("c")
    core_id = my_c.astype(jnp.int32).reshape(1)
    my_chip = 2 * lax.axis_index("x") + lax.axis_index("y")
    chip_id = my_chip.astype(jnp.int32).reshape(1)
    my_dev = 2 * my_chip + my_c
    s_len = x.shape[1]
    h0 = x.reshape(s_len, D)
    target = loss_target.reshape(s_len, D)

    shard = [{k: W[k][l].astype(BF) for k in _SHARDED} for l in range(2)]
    for l in range(2):
        shard[l]['w_dw'] = w_dw[l]
        shard[l]['w_ffn_in'] = jnp.swapaxes(w_ffn_in[l], 0, 1).astype(BF)
    mixer_w = ['w_sgu_out', 'w_conv_out', 'w_pool', 'w_pool_out', 'w_out', 'w_dw']
    ffn_w = ['w_ffn_in', 'w_ffn_out', 'w_ple', 'w_ple_gate']
    hosted_gather = {
        'norm_proj_in': (0, mixer_w),
        'conv_fwd': (0, ffn_w),
        'merge_out': (1, mixer_w),
        'norm_proj_ffn': (1, ['w_in']),
        'ffn_out': (1, ffn_w),
    }
    G = [{'w_in': run_comm(Gather([shard[0]['w_in']]), "gather_w_in_0")[0]}, {}]

    def gather_in(layer, call):
        if layer != 0:
            return None, (lambda got: None)
        to_layer, keys = hosted_gather[call]
        return Gather([shard[to_layer][k] for k in keys]), (lambda got: G[to_layer].update(zip(keys, got)))

    def natural_mixer(g):
        wpool = jnp.transpose(g['w_pool'], (1, 0, 2, 3)).reshape(4, POOL_GD, POOL_GD)
        wdw = jnp.transpose(g['w_dw'].reshape(N_DEV, CONV_W, BLK), (1, 0, 2)).reshape(CONV_W, D)
        return dict(wpool=wpool, wdw=_pad_taps(wdw))

    def natural_ffn(g):
        wfit = g['w_ffn_in'].reshape(4, D_FF // 2, D)
        wple = jnp.transpose(g['w_ple'], (1, 0, 2)).reshape(PLE, D)
        return dict(wfit=wfit, wple=wple)

    def vec(name, layer):
        return W[name][layer].reshape(1, D)

    saved = []
    h = h0
    for l in range(2):
        g = G[l]
        comm, land = gather_in(l, 'norm_proj_in')
        (proj, hn), got = norm_proj(h, vec('g_mix_pre', l), g['w_in'], f"norm_proj_in_{l}", comm=comm)
        land(got)
        nat = natural_mixer(g)
        bsfull = jnp.repeat(b_sgu_s[l].T, BLK, axis=1)
        wst = jnp.swapaxes(w_sgu_s[l], 1, 2)
        mixed, sgu, bra = sgu_fwd(proj, w_sgu_s[l], bsfull, vec('g_sgu_v', l), vec('b_sgu_v', l), g['w_sgu_out'], f"sgu_fwd_{l}")
        comm, land = gather_in(l, 'conv_fwd')
        (conv, cb, brb), got = conv_fwd(proj, nat['wdw'], vec('b_dw', l), vec('g_conv_ln', l), vec('b_conv_ln', l), g['w_conv_out'],
                                        f"conv_fwd_{l}", comm=comm)
        land(got)
        nat.update(natural_ffn(g))
        pooled, pm, brc = pool_fwd(proj, nat['wpool'], vec('s_pool', l), g['w_pool_out'], f"pool_fwd_{l}")
        comm, land = gather_in(l, 'merge_out')
        (merged, mo, h1), got = merge_out(proj, bra, brb, brc, h, g['w_out'], vec('g_mix_post', l), f"merge_out_{l}", comm=comm)
        land(got)
        comm, land = gather_in(l, 'norm_proj_ffn')
        (ff, hn2), got = norm_proj(h1, vec('g_ffn_pre', l), nat['wfit'], f"norm_proj_ffn_{l}", transposed=True, comm=comm)
        land(got)
        comm, land = gather_in(l, 'ffn_out')
        (act, f, h2, pg, h3), got = ffn_out(ff, h1, p, g['w_ffn_out'], vec('g_ffn_post', l), g['w_ple_gate'], nat['wple'], l,
                                            f"ffn_out_{l}", comm=comm)
        land(got)
        saved.append(dict(h=h, nat=nat, wst=wst, proj=proj, hn=hn, mixed=mixed, sgu=sgu, bra=bra, conv=conv, cb=cb, brb=brb,
                          pooled=pooled, pm=pm, brc=brc, merged=merged, mo=mo, h1=h1, ff=ff, hn2=hn2, act=act, f=f, h2=h2, pg=pg))
        h = h3

    dh = h

    parts = {k: [None, None] for k in _SHARDED}
    small = {k: [None, None] for k in _VECTORS + ['b_sgu_s', 'w_sgu_s', 'w_dw']}
    chip_parts = [{}, {}]
    from_chips = [{}, {}]
    gathered_small = [None, None]

    def to_sibling(layer, keys):
        return ToSibling([parts[k][layer] for k in keys])

    def add_siblings(layer, keys, from_sibling):
        for k, rv in zip(keys, from_sibling):
            st = parts[k][layer]
            cols = st.shape[-1]
            chip_parts[layer][k] = add_core_side(st.reshape(N_DEV, -1, cols), rv.reshape(4, -1, cols), core_id,
                                                 f"rs_add_{k}_{layer}").reshape(rv.shape)

    def to_chips(layer, keys):
        return ToChips([chip_parts[layer][k] for k in keys])

    def small_pack(layer):
        return jnp.concatenate([_pack_replicated(small, layer), small['w_dw'][layer]], axis=0)

    ffn_group = ['w_ffn_in', 'w_ffn_out', 'w_ple_gate', 'w_ple']
    mix_group = ['w_out', 'w_sgu_out', 'w_conv_out', 'w_pool_out', 'w_pool']
    big = ['w_in', 'w_ffn_in']
    others = [k for k in _SHARDED if k not in big]
    hosted_rs = {
        'ple_ffn_bwd': (lambda: to_sibling(1, _SHARDED), lambda got: add_siblings(1, _SHARDED, got)),
        'mix_post_bwd': (lambda: Both(to_sibling(0, ffn_group), Gather([small_pack(1)])),
                         lambda got: (add_siblings(0, ffn_group, got[:-1]), gathered_small.__setitem__(1, got[-1]))),
        'ffn_in_bwd': (lambda: to_chips(1, others), lambda got: from_chips[1].update(zip(others, got))),
        'sgu_bwd': (lambda: to_chips(1, big), lambda got: from_chips[1].update(zip(big, got))),
        'conv_bwd': (lambda: to_chips(0, ffn_group), lambda got: from_chips[0].update(zip(ffn_group, got))),
        'wgrad_in_sgu': (lambda: to_sibling(0, mix_group), lambda got: add_siblings(0, mix_group, got)),
        'wgrad_in_seq': (lambda: to_chips(0, mix_group), lambda got: from_chips[0].update(zip(mix_group, got))),
        'wgrad_in_gate': (lambda: Gather([small_pack(0)]), lambda got: gathered_small.__setitem__(0, got[0])),
    }

    def exchange_in(layer, call):
        if layer != 0 or call not in hosted_rs:
            return None, (lambda got: None)
        make, land = hosted_rs[call]
        return make(), land

    for l in (1, 0):
        sv, g, nat = saved[l], G[l], saved[l]['nat']

        def wg(call, a, b, tk, tn, **kw):
            comm, land = exchange_in(l, call)
            if comm is None:
                return wgrad(a, b, tk, tn, f"{call}_{l}", **kw)
            out, got = wgrad(a, b, tk, tn, f"{call}_{l}", comm=comm, **kw)
            land(got)
            return out

        comm, land = exchange_in(l, 'ple_ffn_bwd')
        res, got = ple_ffn_bwd(
            dh, sv['pg'], p, sv['f'], sv['ff'], nat['wple'], g['w_ple_gate'], g['w_ffn_out'], vec('g_ffn_post', l), l,
            f"ple_ffn_bwd_{l}", target=target if l == 1 else None, comm=comm)
        land(got)
        dh2, dpe, dpg, df, dff, small['g_ffn_post'][l] = res[:6]
        if l == 1:
            loss = lax.psum(res[6][0, 0], ("x", "y", "c"))
        comm, land = exchange_in(l, 'ffn_in_bwd')
        (dh1, small['g_ffn_pre'][l]), got = ffn_in_bwd(dff, nat['wfit'], sv['h1'], dh2, vec('g_ffn_pre', l), f"ffn_in_bwd_{l}",
                                                       comm=comm)
        land(got)
        p_spec = pl.BlockSpec((None, None, min(TS, s_len), PLE), functools.partial(lambda k, n, s, ll: (ll, 0, s, 0), ll=l))
        parts['w_ple'][l] = jnp.transpose(wg('wgrad_ple', p, dpe, PLE, D, a_spec=p_spec).reshape(PLE, N_DEV, BLK), (1, 0, 2))
        parts['w_ple_gate'][l] = wg('wgrad_ple_gate', sv['h2'], dpg, D, D).reshape(N_DEV, BLK, D)
        parts['w_ffn_out'][l] = wg('wgrad_ffn_out', sv['act'], df, D_FF // 2, D).reshape(N_DEV, D_FF // N_DEV, D)
        parts['w_ffn_in'][l] = wg('wgrad_ffn_in', dff, sv['hn2'], D_FF // 2, D).reshape(N_DEV, D_FF // 4, D)

        comm, land = exchange_in(l, 'mix_post_bwd')
        (dmo, dbra, dbrb, dbrc, dzg, dhn_g, small['g_mix_post'][l]), got = mix_post_bwd(
            dh1, sv['mo'], vec('g_mix_post', l), g['w_out'], sv['proj'], sv['bra'], sv['brb'], sv['brc'], g['w_in'],
            f"mix_post_bwd_{l}", comm=comm)
        land(got)
        comm, land = exchange_in(l, 'sgu_bwd')
        (dzs, dhn_ag, dws, dbs, small['g_sgu_v'][l], small['b_sgu_v'][l]), got = sgu_bwd(
            dbra, g['w_sgu_out'], sv['proj'], sv['mixed'], sv['wst'], vec('g_sgu_v', l), vec('b_sgu_v', l), g['w_in'],
            dhn_g, f"sgu_bwd_{l}", comm=comm)
        land(got)
        small['w_sgu_s'][l] = dws
        small['b_sgu_s'][l] = dbs.T
        comm, land = exchange_in(l, 'conv_bwd')
        (dc, dwdw, small['b_dw'][l], small['g_conv_ln'][l], small['b_conv_ln'][l]), got = conv_bwd(
            dbrb, g['w_conv_out'], sv['proj'], sv['conv'], vec('g_conv_ln', l), vec('b_conv_ln', l), f"conv_bwd_{l}", comm=comm)
        land(got)
        small['w_dw'][l] = dwdw
        dmr, q, small['s_pool'][l] = pool_bwd(dbrc, g['w_pool_out'], sv['pooled'], nat['wpool'], vec('s_pool', l), f"pool_bwd_{l}")
        comm, land = exchange_in(l, 'seq_bwd')
        (dzc, dh, small['g_mix_pre'][l]), got = seq_bwd(dc, q, sv['proj'], nat['wdw'], g['w_in'], dhn_ag, sv['h'], dh1,
                                                        vec('g_mix_pre', l), f"seq_bwd_{l}", comm=comm)
        land(got)

        parts['w_out'][l] = wg('wgrad_out', sv['merged'], dmo, D, D).reshape(N_DEV, BLK, D)
        parts['w_sgu_out'][l] = wg('wgrad_sgu_out', sv['sgu'], dbra, D, D).reshape(N_DEV, BLK, D)
        parts['w_conv_out'][l] = wg('wgrad_conv_out', sv['cb'], dbrb, D, D).reshape(N_DEV, BLK, D)
        parts['w_pool_out'][l] = wg('wgrad_pool_out', sv['pm'], dbrc, D, D).reshape(N_DEV, BLK, D)
        g_pool = wg('wgrad_pool', sv['pooled'], dmr, POOL_GD, POOL_GD, diag=True)
        parts['w_pool'][l] = jnp.transpose(g_pool.reshape(4, N_DEV, POOL_GD // N_DEV, POOL_GD), (1, 0, 2, 3))
        parts['w_in'][l] = jnp.concatenate([
            wg('wgrad_in_sgu', sv['hn'], dzs, D, D, stacked=True),
            wg('wgrad_in_seq', sv['hn'], dzc, D, D, stacked=True),
            wg('wgrad_in_gate', sv['hn'], dzg, D, D, stacked=True)], axis=0)
    grad_x = dh.reshape(1, s_len, D)

    add_siblings(0, ['w_in'], run_comm(to_sibling(0, ['w_in']), "rs_to_sibling_w_in_0"))
    from_chips[0]['w_in'] = run_comm(to_chips(0, ['w_in']), "rs_to_chips_w_in_0")[0]

    outs = {}
    for k in _SHARDED:
        wmv = [jnp.swapaxes(t[k], 1, 2) if k == 'w_ffn_in' else t[k] for t in (W, M, V)]
        cols = wmv[0].shape[-1]
        pieces = []
        for layer in range(2):
            own4 = chip_parts[layer][k].reshape(4, -1, cols)
            rv3 = from_chips[layer][k].reshape(3, -1, cols)
            pieces.append([(own4, OWN_CHIP), (rv3, 0), (rv3, 1), (rv3, 2)])
        res = adamw_layers(*[t.reshape(2, -1, cols) for t in wmv], pieces, chip_id, f"adamw_{k}")
        res = [r.reshape(wmv[0].shape) for r in res]
        outs[k] = [jnp.swapaxes(r, 1, 2) for r in res] if k == 'w_ffn_in' else res

    packed = [jnp.stack([_pack_replicated(t, 0), _pack_replicated(t, 1)], axis=0) for t in (W, M, V)]
    rep_res = adamw_layers(*packed, [[(gathered_small[layer], d) for d in range(N_DEV)] for layer in range(2)], chip_id,
                           "adamw_replicated")
    for idx, res in enumerate(rep_res):
        for name, val in _unpack_replicated(res).items():
            outs.setdefault(name, [None] * 4)[idx] = val
    dw_sum = jnp.stack([sum_slabs(gathered_small[layer][:, _REP_ROWS:, :], f"sum_w_dw_{layer}") for layer in range(2)], axis=0)
    dw_mine = lax.dynamic_slice_in_dim(dw_sum[:, :CONV_W], my_dev * BLK, BLK, axis=2)
    res = adamw(w_dw.reshape(2 * CONV_W, BLK), m_w_dw.reshape(2 * CONV_W, BLK), v_w_dw.reshape(2 * CONV_W, BLK),
                [(dw_mine.reshape(2 * CONV_W, BLK), None)], "adamw_w_dw")
    outs['w_dw'] = [r.reshape(w_dw.shape) for r in res]

    result = [loss, grad_x]
    for idx in range(4):
        result += [outs[k][idx] for k in _WEIGHTS]
    return tuple(result)
```
